```python
import math
import jax, jax.numpy as jnp
from jax import lax
import numpy as np

D_MODEL = 2048
BATCH = 2
SEQ = 4096
DEPTH = 1

ATT_HEADS = 8
ATT_QK_DIM = 64
ATT_V_DIM = 128
ATT_WIDTH = ATT_HEADS * ATT_V_DIM
Q_BLOCK = 128
SG_GROUPS = 8
SG_DIM = 128
SG_WIDTH = SG_GROUPS * SG_DIM
SG_CHUNK = 128
MIX_WIDTH = ATT_WIDTH + SG_WIDTH
Q_OFF = 0
K_OFF = Q_OFF + ATT_HEADS * 2 * ATT_QK_DIM
V_OFF = K_OFF + ATT_HEADS * 2 * ATT_QK_DIM
U_OFF = V_OFF + ATT_WIDTH
G_OFF = U_OFF + SG_WIDTH
IN_WIDTH = G_OFF + SG_WIDTH
REL_BUCKETS = 32
REL_MAX_DIST = 128
N_GROUPS = 4
EXPERTS_PER_GROUP = 8
N_EXPERTS = N_GROUPS * EXPERTS_PER_GROUP
EXPERT_TOPK = 2
D_EXPERT = 256
DN_ALPHA = (2.0 * DEPTH) ** 0.25
DN_BETA = (8.0 * DEPTH) ** -0.25
LN_EPS = 1e-5

kernel_name = "hybrid_diffattn_gmlp_hmoe_deepnorm_encoder"


def _lambda_init(layer):
    return 0.8 - 0.6 * math.exp(-0.3 * layer)


def _layernorm(x, g, b):
    xf = x.astype(jnp.float32)
    mu = jnp.mean(xf, axis=-1, keepdims=True)
    var = jnp.mean(jnp.square(xf - mu), axis=-1, keepdims=True)
    return ((xf - mu) * lax.rsqrt(var + LN_EPS)).astype(x.dtype) * g + b


def _rel_bucket(rel):
    half = REL_BUCKETS // 2
    max_exact = half // 2
    ret = jnp.where(rel > 0, half, 0)
    n = jnp.abs(rel)
    nf = jnp.maximum(n, 1).astype(jnp.float32)
    large = max_exact + (jnp.log(nf / max_exact) / math.log(REL_MAX_DIST / max_exact)
                         * (half - max_exact)).astype(jnp.int32)
    large = jnp.minimum(large, half - 1)
    return ret + jnp.where(n < max_exact, n, large)


def _diff_attention(q, k, v, rel_bias, lam, subln_g, lam_init):
    B, S = q.shape[0], q.shape[1]
    nb = S // Q_BLOCK
    qb = (q * (ATT_QK_DIM ** -0.5)).reshape(B, nb, Q_BLOCK, ATT_HEADS, 2, ATT_QK_DIM)
    qb = qb.transpose(1, 0, 2, 3, 4, 5)
    kpos = jnp.arange(S, dtype=jnp.int32)

    def block(args):
        qblk, start = args
        qpos = start + jnp.arange(Q_BLOCK, dtype=jnp.int32)
        bias = rel_bias[_rel_bucket(kpos[None, :] - qpos[:, None])]
        bias = bias.transpose(2, 0, 1).astype(jnp.float32)
        logits = jnp.einsum('bqhmd,bkhmd->bmhqk', qblk, k).astype(jnp.float32) + bias[None, None]
        probs = jax.nn.softmax(logits, axis=-1)
        attn = probs[:, 0] - lam * probs[:, 1]
        return jnp.einsum('bhqk,bkhd->bqhd', attn.astype(v.dtype), v)

    starts = jnp.arange(nb, dtype=jnp.int32) * Q_BLOCK
    o = lax.map(block, (qb, starts))
    o = o.transpose(1, 0, 2, 3, 4).reshape(B, S, ATT_HEADS, ATT_V_DIM)
    of = o.astype(jnp.float32)
    of = of * lax.rsqrt(jnp.mean(jnp.square(of), axis=-1, keepdims=True) + LN_EPS)
    o = of.astype(v.dtype) * subln_g * (1.0 - lam_init)
    return o.reshape(B, S, ATT_WIDTH)


def _spatial_gating(u, vg, ln_g, ln_b, w_s, b_s):
    B, S = u.shape[0], u.shape[1]
    nc = S // SG_CHUNK
    v = vg.reshape(B, nc, SG_CHUNK, SG_GROUPS, SG_DIM)
    vn = _layernorm(v, ln_g, ln_b)
    mixed = jnp.einsum('gpq,bcqge->bcpge', w_s, vn) + b_s.T[:, :, None]
    return (u.reshape(B, nc, SG_CHUNK, SG_GROUPS, SG_DIM) * mixed).reshape(B, S, SG_WIDTH)


def _hier_moe(x, w_rg, b_rg, w_re, b_re, w_gate, w_up, w_down):
    B, S, D = x.shape
    t = x.reshape(B * S, D)
    g_logits = jnp.einsum('td,dg->tg', t, w_rg).astype(jnp.float32) + b_rg
    g_prob = jax.nn.softmax(g_logits, axis=-1)
    g_idx = jnp.argmax(g_logits, axis=-1)
    g_gate = jnp.take_along_axis(g_prob, g_idx[:, None], axis=-1)
    e_all = jnp.einsum('td,gde->tge', t, w_re).astype(jnp.float32) + b_re
    e_logits = jnp.take_along_axis(e_all, g_idx[:, None, None], axis=1)[:, 0]
    top_v, top_i = lax.top_k(e_logits, EXPERT_TOPK)
    top_w = jax.nn.softmax(top_v, axis=-1) * g_gate
    flat_idx = g_idx[:, None] * EXPERTS_PER_GROUP + top_i
    combine = jnp.sum(jax.nn.one_hot(flat_idx, N_EXPERTS, dtype=jnp.float32) * top_w[..., None], axis=1)
    h = jax.nn.silu(jnp.einsum('td,edf->tef', t, w_gate)) * jnp.einsum('td,edf->tef', t, w_up)
    h = h * combine[:, :, None].astype(t.dtype)
    return jnp.einsum('tef,efd->td', h, w_down).reshape(B, S, D)


def setup_inputs(seed: int = 0) -> dict:
    key = jax.random.key(seed)
    ks = jax.random.split(key, 24)
    f32 = jnp.float32
    D = D_MODEL
    nrm = lambda k, s, sc: jax.random.normal(k, s, f32) * sc
    col_scale = jnp.ones((IN_WIDTH,), f32).at[V_OFF:V_OFF + ATT_WIDTH].set(DN_BETA)
    return {
        "x": jax.random.normal(ks[0], (BATCH, SEQ, D), f32),
        "w_in": nrm(ks[1], (DEPTH, D, IN_WIDTH), D ** -0.5) * col_scale,
        "w_out": nrm(ks[2], (DEPTH, MIX_WIDTH, D), MIX_WIDTH ** -0.5 * DN_BETA),
        "ln1_g": 1.0 + nrm(ks[3], (DEPTH, D), 0.01),
        "ln1_b": nrm(ks[4], (DEPTH, D), 0.01),
        "ln2_g": 1.0 + nrm(ks[5], (DEPTH, D), 0.01),
        "ln2_b": nrm(ks[6], (DEPTH, D), 0.01),
        "rel_bias": nrm(ks[7], (REL_BUCKETS, ATT_HEADS), 0.5),
        "lam_q1": nrm(ks[8], (DEPTH, ATT_QK_DIM), 0.1),
        "lam_k1": nrm(ks[9], (DEPTH, ATT_QK_DIM), 0.1),
        "lam_q2": nrm(ks[10], (DEPTH, ATT_QK_DIM), 0.1),
        "lam_k2": nrm(ks[11], (DEPTH, ATT_QK_DIM), 0.1),
        "subln_g": 1.0 + nrm(ks[12], (DEPTH, ATT_V_DIM), 0.01),
        "sg_ln_g": 1.0 + nrm(ks[13], (DEPTH, SG_GROUPS, SG_DIM), 0.01),
        "sg_ln_b": nrm(ks[14], (DEPTH, SG_GROUPS, SG_DIM), 0.01),
        "sg_w": nrm(ks[15], (DEPTH, SG_GROUPS, SG_CHUNK, SG_CHUNK), SG_CHUNK ** -0.5),
        "sg_b": 1.0 + nrm(ks[16], (DEPTH, SG_GROUPS, SG_CHUNK), 0.1),
        "w_router_group": nrm(ks[17], (DEPTH, D, N_GROUPS), D ** -0.5),
        "b_router_group": nrm(ks[18], (DEPTH, N_GROUPS), 0.01),
        "w_router_expert": nrm(ks[19], (DEPTH, N_GROUPS, D, EXPERTS_PER_GROUP), D ** -0.5),
        "b_router_expert": nrm(ks[20], (DEPTH, N_GROUPS, EXPERTS_PER_GROUP), 0.01),
        "w_exp_gate": nrm(ks[21], (DEPTH, N_EXPERTS, D, D_EXPERT), D ** -0.5),
        "w_exp_up": nrm(ks[22], (DEPTH, N_EXPERTS, D, D_EXPERT), D ** -0.5),
        "w_exp_down": nrm(ks[23], (DEPTH, N_EXPERTS, D_EXPERT, D), D_EXPERT ** -0.5 * DN_BETA),
    }


def reference(x, w_in, w_out, ln1_g, ln1_b, ln2_g, ln2_b, rel_bias, lam_q1, lam_k1, lam_q2, lam_k2,
              subln_g, sg_ln_g, sg_ln_b, sg_w, sg_b, w_router_group, b_router_group,
              w_router_expert, b_router_expert, w_exp_gate, w_exp_up, w_exp_down):
    B, S = x.shape[0], x.shape[1]
    h = x
    for l in range(DEPTH):
        lam_init = _lambda_init(l)
        proj = jnp.einsum('bsd,de->bse', h, w_in[l])
        q = proj[..., Q_OFF:K_OFF].reshape(B, S, ATT_HEADS, 2, ATT_QK_DIM)
        k = proj[..., K_OFF:V_OFF].reshape(B, S, ATT_HEADS, 2, ATT_QK_DIM)
        v = proj[..., V_OFF:U_OFF].reshape(B, S, ATT_HEADS, ATT_V_DIM)
        uv = jax.nn.gelu(proj[..., U_OFF:IN_WIDTH])
        u, vg = uv[..., :SG_WIDTH], uv[..., SG_WIDTH:]
        lam = (jnp.exp(jnp.sum(lam_q1[l].astype(jnp.float32) * lam_k1[l].astype(jnp.float32)))
               - jnp.exp(jnp.sum(lam_q2[l].astype(jnp.float32) * lam_k2[l].astype(jnp.float32)))
               + lam_init)
        a = _diff_attention(q, k, v, rel_bias, lam, subln_g[l], lam_init)
        s = _spatial_gating(u, vg, sg_ln_g[l], sg_ln_b[l], sg_w[l], sg_b[l])
        mix = jnp.einsum('bse,ed->bsd', jnp.concatenate([a, s], axis=-1), w_out[l])
        h = _layernorm(DN_ALPHA * h + mix, ln1_g[l], ln1_b[l])
        ffn = _hier_moe(h, w_router_group[l], b_router_group[l], w_router_expert[l], b_router_expert[l],
                        w_exp_gate[l], w_exp_up[l], w_exp_down[l])
        h = _layernorm(DN_ALPHA * h + ffn, ln2_g[l], ln2_b[l])
    return h
```

```python
import functools
import math

import jax
import jax.numpy as jnp
from jax import lax
from jax.experimental import pallas as pl
from jax.experimental.pallas import tpu as pltpu

F32 = jnp.float32
BF16 = jnp.bfloat16
U32 = jnp.uint32
I32 = jnp.int32

ATT_HEADS = 8
ATT_QK_DIM = 64
ATT_V_DIM = 128
ATT_WIDTH = ATT_HEADS * ATT_V_DIM
SG_GROUPS = 8
SG_DIM = 128
SG_WIDTH = SG_GROUPS * SG_DIM
SG_CHUNK = 128
REL_BUCKETS = 32
REL_MAX_DIST = 128
N_GROUPS = 4
EXPERTS_PER_GROUP = 8
N_EXPERTS = N_GROUPS * EXPERTS_PER_GROUP
D_EXPERT = 256
DEPTH = 1
DN_ALPHA = (2.0 * DEPTH) ** 0.25
LN_EPS = 1e-5
LOG2E = math.log2(math.e)

LANES = 128
SUBLANES = 8
VMEM_LIMIT_BYTES = 56 * 1024 * 1024

TM_IN = 1024
TN_IN = 512
QB = 256
KC = 256
TM_MIX = 256
TMX = 256
TM_OUT = 256
ROUTE_ROWS = 8
NR_EXPERT0 = 8
NR_PAD = NR_EXPERT0 + 32


def _cparams(sem):
    return pltpu.CompilerParams(dimension_semantics=sem, vmem_limit_bytes=VMEM_LIMIT_BYTES)


def _inproj_kernel(x_ref, w_ref, o_ref, xb_ref, *, n_q, n_lin, q_scale):
    j = pl.program_id(1)

    @pl.when(j == 0)
    def _():
        xb_ref[...] = x_ref[...].astype(BF16)

    acc = jnp.dot(xb_ref[...], w_ref[...].astype(BF16), preferred_element_type=F32)

    @pl.when(j < n_q)
    def _():
        o_ref[...] = (acc * q_scale).astype(BF16)

    @pl.when(jnp.logical_and(j >= n_q, j < n_lin))
    def _():
        o_ref[...] = acc.astype(BF16)

    @pl.when(j >= n_lin)
    def _():
        o_ref[...] = jax.nn.gelu(acc).astype(BF16)


def _inproj(x2, w_in):
    t, d = x2.shape
    n = w_in.shape[1]
    n_q = (ATT_HEADS * 2 * ATT_QK_DIM) // TN_IN
    n_lin = (2 * ATT_HEADS * 2 * ATT_QK_DIM + ATT_WIDTH) // TN_IN
    kern = functools.partial(_inproj_kernel, n_q=n_q, n_lin=n_lin,
                             q_scale=(ATT_QK_DIM ** -0.5) * LOG2E)
    return pl.pallas_call(
        kern,
        grid=(t // TM_IN, n // TN_IN),
        in_specs=[pl.BlockSpec((TM_IN, d), lambda i, j: (i, 0)),
                  pl.BlockSpec((d, TN_IN), lambda i, j: (0, j))],
        out_specs=pl.BlockSpec((TM_IN, TN_IN), lambda i, j: (i, j)),
        out_shape=jax.ShapeDtypeStruct((t, n), BF16),
        scratch_shapes=[pltpu.VMEM((TM_IN, d), BF16)],
        compiler_params=_cparams(("arbitrary", "arbitrary")),
        name="inproj",
    )(x2, w_in)


def _attn_kernel(lam_ref, q_ref, k_ref, v_ref, bias_ref, g_ref, o_ref,
                 vaug_ref, s_ref, p_ref, m_ref, *, n_kc, out_scale):
    qi = pl.program_id(2)

    @pl.when(qi == 0)
    def _():
        vaug_ref[:, :ATT_V_DIM] = v_ref[0]
        vaug_ref[:, ATT_V_DIM:] = jnp.ones((vaug_ref.shape[0], ATT_V_DIM), BF16)

    q = q_ref[0]
    lane = lax.broadcasted_iota(I32, q.shape, 1)
    zero = jnp.zeros_like(q)
    qq = jnp.concatenate([jnp.where(lane < ATT_QK_DIM, q, zero),
                          jnp.where(lane >= ATT_QK_DIM, q, zero)], axis=0)

    m_ref[...] = jnp.full(m_ref.shape, -jnp.inf, F32)

    def pass1(kc, carry):
        ks = pl.multiple_of(kc * KC, KC)
        kblk = k_ref[0, pl.ds(ks, KC), :]
        s = lax.dot_general(qq, kblk, (((1,), (1,)), ((), ())),
                            preferred_element_type=F32)
        b = bias_ref[0, jnp.clip(kc - qi, -2, 2) + 2]
        s = s + jnp.concatenate([b, b], axis=0)
        s_ref[kc] = s
        m_ref[...] = jnp.maximum(m_ref[...], jnp.maximum(s[:, :LANES], s[:, LANES:]))
        return carry

    lax.fori_loop(0, n_kc, pass1, 0)
    m = jnp.max(m_ref[...], axis=1, keepdims=True)

    def pass2(kc, carry):
        p_ref[kc] = jnp.exp2(s_ref[kc] - m).astype(BF16)
        return carry

    lax.fori_loop(0, n_kc, pass2, 0)

    acc = jnp.zeros((2 * QB, 2 * ATT_V_DIM), F32)
    for kc in range(n_kc):
        acc = acc + jnp.dot(p_ref[kc], vaug_ref[kc * KC:(kc + 1) * KC, :],
                            preferred_element_type=F32)
    lam = lam_ref[0]
    o0 = acc[:QB, :ATT_V_DIM] / acc[:QB, ATT_V_DIM:ATT_V_DIM + 1]
    o1 = acc[QB:, :ATT_V_DIM] / acc[QB:, ATT_V_DIM:ATT_V_DIM + 1]
    of = o0 - lam * o1
    of = of * lax.rsqrt(jnp.mean(of * of, axis=-1, keepdims=True) + LN_EPS)
    o_ref[0] = (of * (g_ref[...] * out_scale)).astype(o_ref.dtype)


def _attention(proj3, bias5, lam, subln_g, lam_init):
    b, s, _ = proj3.shape
    n_kc = s // KC
    h_off_k = ATT_HEADS
    h_off_v = 2 * ATT_HEADS
    kern = functools.partial(_attn_kernel, n_kc=n_kc, out_scale=1.0 - lam_init)
    return pl.pallas_call(
        kern,
        grid=(b, ATT_HEADS, s // QB),
        in_specs=[
            pl.BlockSpec(memory_space=pltpu.SMEM),
            pl.BlockSpec((1, QB, LANES), lambda bi, h, qi: (bi, qi, h)),
            pl.BlockSpec((1, s, LANES), lambda bi, h, qi: (bi, 0, h_off_k + h)),
            pl.BlockSpec((1, s, LANES), lambda bi, h, qi: (bi, 0, h_off_v + h)),
            pl.BlockSpec((1, 5, QB, KC), lambda bi, h, qi: (h, 0, 0, 0)),
            pl.BlockSpec((1, ATT_V_DIM), lambda bi, h, qi: (0, 0)),
        ],
        out_specs=pl.BlockSpec((1, QB, ATT_V_DIM), lambda bi, h, qi: (bi, qi, h)),
        out_shape=jax.ShapeDtypeStruct((b, s, ATT_WIDTH), BF16),
        scratch_shapes=[
            pltpu.VMEM((s, 2 * ATT_V_DIM), BF16),
            pltpu.VMEM((n_kc, 2 * QB, KC), F32),
            pltpu.VMEM((n_kc, 2 * QB, KC), BF16),
            pltpu.VMEM((2 * QB, LANES), F32),
        ],
        compiler_params=_cparams(("arbitrary", "arbitrary", "arbitrary")),
        name="attn",
    )(lam, proj3, proj3, proj3, bias5, subln_g)


def _rel_bucket(rel):
    half = REL_BUCKETS // 2
    max_exact = half // 2
    ret = jnp.where(rel > 0, half, 0)
    n = jnp.abs(rel)
    nf = jnp.maximum(n, 1).astype(F32)
    large = max_exact + (jnp.log(nf / max_exact) / math.log(REL_MAX_DIST / max_exact)
                         * (half - max_exact)).astype(I32)
    large = jnp.minimum(large, half - 1)
    return ret + jnp.where(n < max_exact, n, large)


def _bias_tiles(rel_bias, s):
    assert REL_MAX_DIST <= KC and QB == KC
    r = jnp.arange(QB, dtype=I32)[:, None]
    c = jnp.arange(KC, dtype=I32)[None, :]
    d = jnp.arange(-2, 3, dtype=I32)[:, None, None]
    rel = jnp.clip(d * KC + c - r, -(s - 1), s - 1)
    tiles = rel_bias[_rel_bucket(rel)]
    return jnp.transpose(tiles, (3, 0, 1, 2)).astype(F32) * LOG2E


def _layernorm(v, g, b):
    mu = jnp.mean(v, axis=-1, keepdims=True)
    var = jnp.mean(jnp.square(v - mu), axis=-1, keepdims=True)
    return (v - mu) * lax.rsqrt(var + LN_EPS) * g + b


def _pack_bf16_pair(v):
    n = v.shape[1] // 2
    bits = lax.bitcast_convert_type(v.astype(BF16).astype(F32), U32)
    return (bits[:, :n] >> 16) | (bits[:, n:] & jnp.uint32(0xFFFF0000))


def _unpack_bf16_pair(w):
    lo = lax.bitcast_convert_type(w << 16, F32)
    hi = lax.bitcast_convert_type(w & jnp.uint32(0xFFFF0000), F32)
    return lo, hi


def _first_argmax(vals, nrows):
    mx = jnp.max(vals, axis=0, keepdims=True)
    row = lax.broadcasted_iota(I32, vals.shape, 0)
    idx = jnp.min(jnp.where(vals == mx, row, nrows), axis=0, keepdims=True)
    return mx, idx


def _mix_kernel(a_ref, u_ref, vg_ref, x_ref, wout_ref, sgw_ref, sgb_ref, sgg_ref, sgbeta_ref,
                g1_ref, b1_ref, wr_ref, br_ref,
                h1_ref, hp_ref, ri_ref, rw_ref, cnt_ref, run_ref):
    i = pl.program_id(0)

    @pl.when(i == 0)
    def _():
        run_ref[...] = jnp.zeros_like(run_ref)

    n_chunks = TM_MIX // SG_CHUNK
    rows = []
    for c in range(n_chunks):
        cols = []
        for g in range(SG_GROUPS):
            rs = slice(c * SG_CHUNK, (c + 1) * SG_CHUNK)
            cs = slice(g * SG_DIM, (g + 1) * SG_DIM)
            vt = vg_ref[rs, cs].astype(F32)
            vn = _layernorm(vt, sgg_ref[g:g + 1, :], sgbeta_ref[g:g + 1, :])
            mixed = jnp.dot(sgw_ref[g], vn.astype(BF16), preferred_element_type=F32) + sgb_ref[g]
            cols.append((u_ref[rs, cs].astype(F32) * mixed).astype(BF16))
        rows.append(jnp.concatenate(cols, axis=1))
    sgate = jnp.concatenate(rows, axis=0)

    mix = jnp.dot(a_ref[...], wout_ref[:ATT_WIDTH, :], preferred_element_type=F32)
    mix = mix + jnp.dot(sgate, wout_ref[ATT_WIDTH:, :], preferred_element_type=F32)
    h1 = _layernorm(DN_ALPHA * x_ref[...] + mix, g1_ref[...], b1_ref[...])
    h1_ref[...] = h1
    hp_ref[...] = _pack_bf16_pair(h1)

    lt = lax.dot_general(wr_ref[...], h1, (((1,), (1,)), ((), ())),
                         preferred_element_type=F32, precision=lax.Precision.HIGHEST)
    lt = lt + br_ref[...]
    g_logits = lt[0:N_GROUPS]
    gmax, g_idx = _first_argmax(g_logits, N_GROUPS)
    g_gate = 1.0 / jnp.sum(jnp.exp(g_logits - gmax), axis=0, keepdims=True)
    e_logits = jnp.zeros((EXPERTS_PER_GROUP, TM_MIX), F32)
    for g in range(N_GROUPS):
        lo = NR_EXPERT0 + g * EXPERTS_PER_GROUP
        e_logits = jnp.where(g_idx == g, lt[lo:lo + EXPERTS_PER_GROUP], e_logits)
    v1, i1 = _first_argmax(e_logits, EXPERTS_PER_GROUP)
    erow = lax.broadcasted_iota(I32, e_logits.shape, 0)
    v2, i2 = _first_argmax(jnp.where(erow == i1, -jnp.inf, e_logits), EXPERTS_PER_GROUP)
    t = jnp.exp(v2 - v1)
    w1 = g_gate / (1.0 + t)
    w2 = g_gate * t / (1.0 + t)
    e1 = g_idx * EXPERTS_PER_GROUP + i1
    e2 = g_idx * EXPERTS_PER_GROUP + i2

    xrow = lax.broadcasted_iota(I32, (N_EXPERTS, TM_MIX), 0)
    oh1 = xrow == e1
    oh2 = xrow == e2
    oh = jnp.where(jnp.logical_or(oh1, oh2), 1.0, 0.0)
    tr = lax.broadcasted_iota(I32, (TM_MIX, TM_MIX), 0)
    tc = lax.broadcasted_iota(I32, (TM_MIX, TM_MIX), 1)
    before = jnp.where(tr < tc, 1.0, 0.0).astype(BF16)
    cum = jnp.dot(oh.astype(BF16), before, preferred_element_type=F32)
    base = cum + run_ref[:, 0:1]
    rank1 = jnp.sum(jnp.where(oh1, base, 0.0), axis=0, keepdims=True).astype(I32)
    rank2 = jnp.sum(jnp.where(oh2, base, 0.0), axis=0, keepdims=True).astype(I32)
    run_ref[...] = run_ref[...] + jnp.sum(oh, axis=1, keepdims=True)
    cnt_ref[...] = run_ref[...].astype(I32)

    zi = jnp.zeros((ROUTE_ROWS - 4, TM_MIX), I32)
    ri_ref[...] = jnp.concatenate([e1, e2, rank1, rank2, zi], axis=0)
    zw = jnp.zeros((ROUTE_ROWS - 2, TM_MIX), F32)
    rw_ref[...] = jnp.concatenate([w1, w2, zw], axis=0)


def _mix(a2, proj2, x2, wout_b, sgw_b, sgb, sgg, sgbeta, g1, b1, wr, br):
    t, d = x2.shape
    u_blk = (proj2.shape[1] - 2 * SG_WIDTH) // SG_WIDTH
    const = lambda *shape: pl.BlockSpec(shape, lambda i: (0,) * len(shape))
    return pl.pallas_call(
        _mix_kernel,
        grid=(t // TM_MIX,),
        in_specs=[
            pl.BlockSpec((TM_MIX, ATT_WIDTH), lambda i: (i, 0)),
            pl.BlockSpec((TM_MIX, SG_WIDTH), lambda i: (i, u_blk)),
            pl.BlockSpec((TM_MIX, SG_WIDTH), lambda i: (i, u_blk + 1)),
            pl.BlockSpec((TM_MIX, d), lambda i: (i, 0)),
            const(d, d),
            const(SG_GROUPS, SG_CHUNK, SG_CHUNK),
            const(SG_GROUPS, SG_CHUNK, 1),
            const(SG_GROUPS, SG_DIM),
            const(SG_GROUPS, SG_DIM),
            const(1, d),
            const(1, d),
            const(NR_PAD, d),
            const(NR_PAD, 1),
        ],
        out_specs=[
            pl.BlockSpec((TM_MIX, d), lambda i: (i, 0)),
            pl.BlockSpec((TM_MIX, d // 2), lambda i: (i, 0)),
            pl.BlockSpec((ROUTE_ROWS, TM_MIX), lambda i: (0, i)),
            pl.BlockSpec((ROUTE_ROWS, TM_MIX), lambda i: (0, i)),
            const(N_EXPERTS, LANES),
        ],
        out_shape=[
            jax.ShapeDtypeStruct((t, d), F32),
            jax.ShapeDtypeStruct((t, d // 2), U32),
            jax.ShapeDtypeStruct((ROUTE_ROWS, t), I32),
            jax.ShapeDtypeStruct((ROUTE_ROWS, t), F32),
            jax.ShapeDtypeStruct((N_EXPERTS, LANES), I32),
        ],
        scratch_shapes=[pltpu.VMEM((N_EXPERTS, LANES), F32)],
        compiler_params=_cparams(("arbitrary",)),
        name="mix",
    )(a2, proj2, proj2, x2, wout_b, sgw_b, sgb, sgg, sgbeta, g1, b1, wr, br)


DISPATCH_CHUNK = 256


def _dispatch_kernel(pos_ref, last_ref, hp_hbm, xs_hbm, zero_ref, zsem, sem):
    i = pl.program_id(0)

    def row_copy(tok, dst):
        return pltpu.make_async_copy(hp_hbm.at[pl.ds(tok, 1)], xs_hbm.at[pl.ds(dst, 1)], sem)

    def zero_copy(e):
        start = pl.multiple_of(last_ref[e] * TMX, TMX)
        return pltpu.make_async_copy(zero_ref, xs_hbm.at[pl.ds(start, TMX)], zsem)

    @pl.when(i == 0)
    def _():
        zero_ref[...] = jnp.zeros_like(zero_ref)

        def zstart(e, c):
            zero_copy(e).start()
            return c

        def zwait(e, c):
            zero_copy(e).wait()
            return c

        lax.fori_loop(0, N_EXPERTS, zstart, 0)
        lax.fori_loop(0, N_EXPERTS, zwait, 0)

    base = i * DISPATCH_CHUNK

    def start(j, c):
        row_copy(base + j, pos_ref[0, j]).start()
        row_copy(base + j, pos_ref[1, j]).start()
        return c

    def wait(j, c):
        row_copy(base + j, pos_ref[0, j]).wait()
        row_copy(base + j, pos_ref[1, j]).wait()
        return c

    lax.fori_loop(0, DISPATCH_CHUNK, start, 0)
    lax.fori_loop(0, DISPATCH_CHUNK, wait, 0)


def _dispatch(pos3, last_tile, hp, n_rows):
    t, w = hp.shape
    return pl.pallas_call(
        _dispatch_kernel,
        grid=(t // DISPATCH_CHUNK,),
        in_specs=[
            pl.BlockSpec((None, 2, DISPATCH_CHUNK), lambda i: (i, 0, 0), memory_space=pltpu.SMEM),
            pl.BlockSpec(memory_space=pltpu.SMEM),
            pl.BlockSpec(memory_space=pl.ANY),
        ],
        out_specs=pl.BlockSpec(memory_space=pl.ANY),
        out_shape=jax.ShapeDtypeStruct((n_rows, w), U32),
        scratch_shapes=[pltpu.VMEM((TMX, w), U32),
                        pltpu.SemaphoreType.DMA(()),
                        pltpu.SemaphoreType.DMA(())],
        compiler_params=_cparams(("arbitrary",)),
        name="dispatch",
    )(pos3, last_tile, hp)


def _experts_kernel(te_ref, nu_ref, x_ref, wg_ref, wu_ref, wd_ref, y_ref, wgu_s, wd_s):
    i = pl.program_id(0)
    d = wg_ref.shape[1]

    @pl.when(i < nu_ref[0])
    def _():
        first = jnp.logical_or(i == 0, te_ref[i] != te_ref[jnp.maximum(i - 1, 0)])

        @pl.when(first)
        def _():
            wgu_s[:, :D_EXPERT] = wg_ref[0].astype(BF16)
            wgu_s[:, D_EXPERT:] = wu_ref[0].astype(BF16)
            wd_s[...] = wd_ref[0].astype(BF16)

        lo, hi = _unpack_bf16_pair(x_ref[...])
        gu = jnp.dot(lo.astype(BF16), wgu_s[:d // 2, :], preferred_element_type=F32)
        gu = gu + jnp.dot(hi.astype(BF16), wgu_s[d // 2:, :], preferred_element_type=F32)
        gate = gu[:, :D_EXPERT]
        hmid = (gate * jax.nn.sigmoid(gate) * gu[:, D_EXPERT:]).astype(BF16)
        y = jnp.dot(hmid, wd_s[...], preferred_element_type=F32)
        y_ref[...] = _pack_bf16_pair(y)


def _experts(tile_expert, n_used, xs, w_gate, w_up, w_down):
    n_rows, w = xs.shape
    d = w_gate.shape[1]
    n_tiles = n_rows // TMX

    def row_map(i, te, nu):
        return (jnp.minimum(i, nu[0] - 1), 0)

    def w_map(i, te, nu):
        return (te[i], 0, 0)

    grid_spec = pltpu.PrefetchScalarGridSpec(
        num_scalar_prefetch=2,
        grid=(n_tiles,),
        in_specs=[
            pl.BlockSpec((TMX, w), row_map),
            pl.BlockSpec((1, d, D_EXPERT), w_map),
            pl.BlockSpec((1, d, D_EXPERT), w_map),
            pl.BlockSpec((1, D_EXPERT, d), w_map),
        ],
        out_specs=pl.BlockSpec((TMX, w), row_map),
        scratch_shapes=[pltpu.VMEM((d, 2 * D_EXPERT), BF16),
                        pltpu.VMEM((D_EXPERT, d), BF16)],
    )
    return pl.pallas_call(
        _experts_kernel,
        grid_spec=grid_spec,
        out_shape=jax.ShapeDtypeStruct((n_rows, w), U32),
        compiler_params=_cparams(("arbitrary",)),
        name="experts",
    )(tile_expert, n_used, xs, w_gate, w_up, w_down)


def _combine_kernel(pos_ref, posn_ref, y_hbm, h1_ref, wt_ref, g2_ref, b2_ref, o_ref, ybuf, sem):
    i = pl.program_id(0)
    n = pl.num_programs(0)
    slot = i % 2

    def row_copy(src, sl, which, j):
        return pltpu.make_async_copy(y_hbm.at[pl.ds(src, 1)],
                                     ybuf.at[sl, which, pl.ds(j, 1)], sem.at[sl])

    def issue(p_ref, sl):
        def body(j, c):
            row_copy(p_ref[0, j], sl, 0, j).start()
            row_copy(p_ref[1, j], sl, 1, j).start()
            return c
        lax.fori_loop(0, TM_OUT, body, 0)

    @pl.when(i == 0)
    def _():
        issue(pos_ref, 0)

    @pl.when(i + 1 < n)
    def _():
        issue(posn_ref, 1 - slot)

    def wait(j, c):
        row_copy(pos_ref[0, j], slot, 0, j).wait()
        row_copy(pos_ref[1, j], slot, 1, j).wait()
        return c

    lax.fori_loop(0, TM_OUT, wait, 0)

    lo1, hi1 = _unpack_bf16_pair(ybuf[slot, 0])
    lo2, hi2 = _unpack_bf16_pair(ybuf[slot, 1])
    w1 = wt_ref[:, 0:1]
    w2 = wt_ref[:, 1:2]
    ffn = jnp.concatenate([w1 * lo1 + w2 * lo2, w1 * hi1 + w2 * hi2], axis=1)
    o_ref[...] = _layernorm(DN_ALPHA * h1_ref[...] + ffn, g2_ref[...], b2_ref[...])


def _combine(pos3, ys, h1, wtok, g2, b2):
    t, d = h1.shape
    w = ys.shape[1]
    n = t // TM_OUT
    return pl.pallas_call(
        _combine_kernel,
        grid=(n,),
        in_specs=[
            pl.BlockSpec((None, 2, TM_OUT), lambda i: (i, 0, 0), memory_space=pltpu.SMEM),
            pl.BlockSpec((None, 2, TM_OUT), lambda i: (jnp.minimum(i + 1, n - 1), 0, 0),
                         memory_space=pltpu.SMEM),
            pl.BlockSpec(memory_space=pl.ANY),
            pl.BlockSpec((TM_OUT, d), lambda i: (i, 0)),
            pl.BlockSpec((TM_OUT, ROUTE_ROWS), lambda i: (i, 0)),
            pl.BlockSpec((1, d), lambda i: (0, 0)),
            pl.BlockSpec((1, d), lambda i: (0, 0)),
        ],
        out_specs=pl.BlockSpec((TM_OUT, d), lambda i: (i, 0)),
        out_shape=jax.ShapeDtypeStruct((t, d), F32),
        scratch_shapes=[pltpu.VMEM((2, 2, TM_OUT, w), U32),
                        pltpu.SemaphoreType.DMA((2,))],
        compiler_params=_cparams(("arbitrary",)),
        name="combine",
    )(pos3, pos3, ys, h1, wtok, g2, b2)


def _lambda_init(layer):
    return 0.8 - 0.6 * math.exp(-0.3 * layer)


def kernel(x, w_in, w_out, ln1_g, ln1_b, ln2_g, ln2_b, rel_bias, lam_q1, lam_k1, lam_q2, lam_k2,
           subln_g, sg_ln_g, sg_ln_b, sg_w, sg_b, w_router_group, b_router_group,
           w_router_expert, b_router_expert, w_exp_gate, w_exp_up, w_exp_down):
    b, s, d = x.shape
    t = b * s
    assert w_in.shape[0] == DEPTH == 1
    l = 0
    lam_init = _lambda_init(l)
    x2 = x.reshape(t, d)

    proj = _inproj(x2, w_in[l])

    lam = (jnp.exp(jnp.sum(lam_q1[l].astype(F32) * lam_k1[l].astype(F32)))
           - jnp.exp(jnp.sum(lam_q2[l].astype(F32) * lam_k2[l].astype(F32))) + lam_init)
    a = _attention(proj.reshape(b, s, -1), _bias_tiles(rel_bias, s), lam.reshape(1),
                   subln_g[l].reshape(1, -1), lam_init)

    wr = jnp.concatenate([w_router_group[l].T,
                          jnp.zeros((NR_EXPERT0 - N_GROUPS, d), F32),
                          jnp.transpose(w_router_expert[l], (0, 2, 1)).reshape(N_EXPERTS, d)],
                         axis=0)
    br = jnp.concatenate([b_router_group[l], jnp.zeros((NR_EXPERT0 - N_GROUPS,), F32),
                          b_router_expert[l].reshape(-1)]).reshape(NR_PAD, 1)
    h1, hp, ri, rw, cnt = _mix(
        a.reshape(t, -1), proj, x2, w_out[l].astype(BF16), sg_w[l].astype(BF16),
        sg_b[l].reshape(SG_GROUPS, SG_CHUNK, 1), sg_ln_g[l], sg_ln_b[l],
        ln1_g[l].reshape(1, d), ln1_b[l].reshape(1, d), wr, br)

    counts = cnt[:, 0]
    tiles_e = (counts + TMX - 1) // TMX
    tile_end = jnp.cumsum(tiles_e)
    tile_start = tile_end - tiles_e
    n_used = tile_end[-1:]
    max_tiles = (2 * t) // TMX + N_EXPERTS
    tile_ids = jnp.arange(max_tiles, dtype=I32)
    tile_expert = jnp.minimum(jnp.searchsorted(tile_end, tile_ids, side="right"),
                              N_EXPERTS - 1).astype(I32)
    tile_expert = jnp.where(tile_ids < n_used[0], tile_expert,
                            tile_expert[jnp.maximum(n_used[0] - 1, 0)])
    last_tile = jnp.maximum(tile_end - 1, 0).astype(I32)
    row_start = (tile_start * TMX).astype(I32)
    pos = row_start[ri[0:2]] + ri[2:4]
    pos3 = pos.reshape(2, t // TM_OUT, TM_OUT).transpose(1, 0, 2)

    xs = _dispatch(pos3, last_tile, hp, max_tiles * TMX)
    ys = _experts(tile_expert, n_used.astype(I32), xs, w_exp_gate[l], w_exp_up[l], w_exp_down[l])
    out = _combine(pos3, ys, h1, rw.T, ln2_g[l].reshape(1, d), ln2_b[l].reshape(1, d))
    return out.reshape(b, s, d)
```

```python
import functools
import math

import jax
import jax.numpy as jnp
from jax import lax
from jax.experimental import pallas as pl
from jax.experimental.pallas import tpu as pltpu

F32 = jnp.float32
BF16 = jnp.bfloat16
U32 = jnp.uint32
I32 = jnp.int32

ATT_HEADS = 8
ATT_QK_DIM = 64
ATT_V_DIM = 128
ATT_WIDTH = ATT_HEADS * ATT_V_DIM
SG_GROUPS = 8
SG_DIM = 128
SG_WIDTH = SG_GROUPS * SG_DIM
SG_CHUNK = 128
REL_BUCKETS = 32
REL_MAX_DIST = 128
N_GROUPS = 4
EXPERTS_PER_GROUP = 8
N_EXPERTS = N_GROUPS * EXPERTS_PER_GROUP
D_EXPERT = 256
DEPTH = 1
DN_ALPHA = (2.0 * DEPTH) ** 0.25
LN_EPS = 1e-5
LOG2E = math.log2(math.e)

LANES = 128
SUBLANES = 8
VMEM_LIMIT_BYTES = 56 * 1024 * 1024

TM_IN = 1024
TN_IN = 512
QB = 256
KC = 256
TM_MIX = 256
TMX = 256
TM_OUT = 256
ROUTE_ROWS = 8
NR_EXPERT0 = 8
NR_PAD = NR_EXPERT0 + 32


def _cparams(sem):
    return pltpu.CompilerParams(dimension_semantics=sem, vmem_limit_bytes=VMEM_LIMIT_BYTES)


def _inproj_kernel(x_ref, w_ref, o_ref, xb_ref, *, n_q, n_lin, q_scale):
    j = pl.program_id(1)

    @pl.when(j == 0)
    def _():
        xb_ref[...] = x_ref[...].astype(BF16)

    acc = jnp.dot(xb_ref[...], w_ref[...].astype(BF16), preferred_element_type=F32)

    @pl.when(j < n_q)
    def _():
        o_ref[...] = (acc * q_scale).astype(BF16)

    @pl.when(jnp.logical_and(j >= n_q, j < n_lin))
    def _():
        o_ref[...] = acc.astype(BF16)

    @pl.when(j >= n_lin)
    def _():
        o_ref[...] = jax.nn.gelu(acc).astype(BF16)


def _inproj(x2, w_in):
    t, d = x2.shape
    n = w_in.shape[1]
    n_q = (ATT_HEADS * 2 * ATT_QK_DIM) // TN_IN
    n_lin = (2 * ATT_HEADS * 2 * ATT_QK_DIM + ATT_WIDTH) // TN_IN
    kern = functools.partial(_inproj_kernel, n_q=n_q, n_lin=n_lin,
                             q_scale=(ATT_QK_DIM ** -0.5) * LOG2E)
    return pl.pallas_call(
        kern,
        grid=(t // TM_IN, n // TN_IN),
        in_specs=[pl.BlockSpec((TM_IN, d), lambda i, j: (i, 0)),
                  pl.BlockSpec((d, TN_IN), lambda i, j: (0, j))],
        out_specs=pl.BlockSpec((TM_IN, TN_IN), lambda i, j: (i, j)),
        out_shape=jax.ShapeDtypeStruct((t, n), BF16),
        scratch_shapes=[pltpu.VMEM((TM_IN, d), BF16)],
        compiler_params=_cparams(("arbitrary", "arbitrary")),
        name="inproj",
    )(x2, w_in)


def _attn_kernel(lam_ref, q_ref, k_ref, v_ref, bias_ref, g_ref, o_ref, vaug_ref,
                 *, n_kc, out_scale):
    qi = pl.program_id(2)

    @pl.when(qi == 0)
    def _():
        vaug_ref[:, :ATT_V_DIM] = v_ref[0]
        vaug_ref[:, ATT_V_DIM:] = jnp.ones((vaug_ref.shape[0], ATT_V_DIM), BF16)

    q = q_ref[0]
    lane = lax.broadcasted_iota(I32, q.shape, 1)
    zero = jnp.zeros_like(q)
    qq = jnp.concatenate([jnp.where(lane < ATT_QK_DIM, q, zero),
                          jnp.where(lane >= ATT_QK_DIM, q, zero)], axis=0)

    m = jnp.full((2 * QB, 1), -jnp.inf, F32)
    acc = jnp.zeros((2 * QB, 2 * ATT_V_DIM), F32)
    for kc in range(n_kc):
        kblk = k_ref[0, kc * KC:(kc + 1) * KC, :]
        s = lax.dot_general(qq, kblk, (((1,), (1,)), ((), ())),
                            preferred_element_type=F32)
        b = bias_ref[0, jnp.clip(kc - qi, -2, 2) + 2]
        s = s + jnp.concatenate([b, b], axis=0)
        m_new = jnp.maximum(m, jnp.max(s, axis=1, keepdims=True))
        p = jnp.exp2(s - m_new).astype(BF16)
        acc = acc * jnp.exp2(m - m_new) + jnp.dot(
            p, vaug_ref[kc * KC:(kc + 1) * KC, :], preferred_element_type=F32)
        m = m_new
    lam = lam_ref[0]
    o0 = acc[:QB, :ATT_V_DIM] / acc[:QB, ATT_V_DIM:ATT_V_DIM + 1]
    o1 = acc[QB:, :ATT_V_DIM] / acc[QB:, ATT_V_DIM:ATT_V_DIM + 1]
    of = o0 - lam * o1
    of = of * lax.rsqrt(jnp.mean(of * of, axis=-1, keepdims=True) + LN_EPS)
    o_ref[0] = (of * (g_ref[...] * out_scale)).astype(o_ref.dtype)


def _attention(proj3, bias5, lam, subln_g, lam_init):
    b, s, _ = proj3.shape
    n_kc = s // KC
    h_off_k = ATT_HEADS
    h_off_v = 2 * ATT_HEADS
    kern = functools.partial(_attn_kernel, n_kc=n_kc, out_scale=1.0 - lam_init)
    return pl.pallas_call(
        kern,
        grid=(b, ATT_HEADS, s // QB),
        in_specs=[
            pl.BlockSpec(memory_space=pltpu.SMEM),
            pl.BlockSpec((1, QB, LANES), lambda bi, h, qi: (bi, qi, h)),
            pl.BlockSpec((1, s, LANES), lambda bi, h, qi: (bi, 0, h_off_k + h)),
            pl.BlockSpec((1, s, LANES), lambda bi, h, qi: (bi, 0, h_off_v + h)),
            pl.BlockSpec((1, 5, QB, KC), lambda bi, h, qi: (h, 0, 0, 0)),
            pl.BlockSpec((1, ATT_V_DIM), lambda bi, h, qi: (0, 0)),
        ],
        out_specs=pl.BlockSpec((1, QB, ATT_V_DIM), lambda bi, h, qi: (bi, qi, h)),
        out_shape=jax.ShapeDtypeStruct((b, s, ATT_WIDTH), BF16),
        scratch_shapes=[pltpu.VMEM((s, 2 * ATT_V_DIM), BF16)],
        compiler_params=_cparams(("arbitrary", "arbitrary", "arbitrary")),
        name="attn",
    )(lam, proj3, proj3, proj3, bias5, subln_g)


def _rel_bucket(rel):
    half = REL_BUCKETS // 2
    max_exact = half // 2
    ret = jnp.where(rel > 0, half, 0)
    n = jnp.abs(rel)
    nf = jnp.maximum(n, 1).astype(F32)
    large = max_exact + (jnp.log(nf / max_exact) / math.log(REL_MAX_DIST / max_exact)
                         * (half - max_exact)).astype(I32)
    large = jnp.minimum(large, half - 1)
    return ret + jnp.where(n < max_exact, n, large)


def _bias_tiles(rel_bias, s):
    assert REL_MAX_DIST <= KC and QB == KC
    period = 2 * KC
    j = jnp.arange(period, dtype=I32)
    off = jnp.where(j < KC, j, j - period)
    d = jnp.arange(-2, 3, dtype=I32)[:, None]
    rel = jnp.clip(d * KC + off[None, :], -(s - 1), s - 1)
    g = jnp.transpose(rel_bias[_rel_bucket(rel)], (2, 0, 1)).astype(F32) * LOG2E
    flat = jnp.tile(g, (1, 1, QB))[..., :QB * (period - 1)]
    return flat.reshape(ATT_HEADS, 5, QB, period - 1)[..., :KC]


def _layernorm(v, g, b):
    mu = jnp.mean(v, axis=-1, keepdims=True)
    var = jnp.mean(jnp.square(v - mu), axis=-1, keepdims=True)
    return (v - mu) * lax.rsqrt(var + LN_EPS) * g + b


def _pack_bf16_pair(v):
    n = v.shape[1] // 2
    bits = lax.bitcast_convert_type(v.astype(BF16).astype(F32), U32)
    return (bits[:, :n] >> 16) | (bits[:, n:] & jnp.uint32(0xFFFF0000))


def _unpack_bf16_pair(w):
    lo = lax.bitcast_convert_type(w << 16, F32)
    hi = lax.bitcast_convert_type(w & jnp.uint32(0xFFFF0000), F32)
    return lo, hi


def _first_argmax(vals, nrows):
    mx = jnp.max(vals, axis=0, keepdims=True)
    row = lax.broadcasted_iota(I32, vals.shape, 0)
    idx = jnp.min(jnp.where(vals == mx, row, nrows), axis=0, keepdims=True)
    return mx, idx


def _mix_kernel(a_ref, u_ref, vg_ref, x_ref, wout_ref, sgw_ref, sgb_ref, sgg_ref, sgbeta_ref,
                g1_ref, b1_ref, wr_ref, br_ref,
                h1_ref, hp_ref, ri_ref, rw_ref, cnt_ref, run_ref):
    i = pl.program_id(0)

    @pl.when(i == 0)
    def _():
        run_ref[...] = jnp.zeros_like(run_ref)

    n_chunks = TM_MIX // SG_CHUNK
    rows = []
    for c in range(n_chunks):
        cols = []
        for g in range(SG_GROUPS):
            rs = slice(c * SG_CHUNK, (c + 1) * SG_CHUNK)
            cs = slice(g * SG_DIM, (g + 1) * SG_DIM)
            vt = vg_ref[rs, cs].astype(F32)
            vn = _layernorm(vt, sgg_ref[g:g + 1, :], sgbeta_ref[g:g + 1, :])
            mixed = jnp.dot(sgw_ref[g], vn.astype(BF16), preferred_element_type=F32) + sgb_ref[g]
            cols.append((u_ref[rs, cs].astype(F32) * mixed).astype(BF16))
        rows.append(jnp.concatenate(cols, axis=1))
    sgate = jnp.concatenate(rows, axis=0)

    mix = jnp.dot(a_ref[...], wout_ref[:ATT_WIDTH, :], preferred_element_type=F32)
    mix = mix + jnp.dot(sgate, wout_ref[ATT_WIDTH:, :], preferred_element_type=F32)
    h1 = _layernorm(DN_ALPHA * x_ref[...] + mix, g1_ref[...], b1_ref[...])
    h1_ref[...] = h1
    hp_ref[...] = _pack_bf16_pair(h1)

    lt = lax.dot_general(wr_ref[...], h1, (((1,), (1,)), ((), ())),
                         preferred_element_type=F32, precision=lax.Precision.HIGHEST)
    lt = lt + br_ref[...]
    g_logits = lt[0:N_GROUPS]
    gmax, g_idx = _first_argmax(g_logits, N_GROUPS)
    g_gate = 1.0 / jnp.sum(jnp.exp(g_logits - gmax), axis=0, keepdims=True)
    e_logits = jnp.zeros((EXPERTS_PER_GROUP, TM_MIX), F32)
    for g in range(N_GROUPS):
        lo = NR_EXPERT0 + g * EXPERTS_PER_GROUP
        e_logits = jnp.where(g_idx == g, lt[lo:lo + EXPERTS_PER_GROUP], e_logits)
    v1, i1 = _first_argmax(e_logits, EXPERTS_PER_GROUP)
    erow = lax.broadcasted_iota(I32, e_logits.shape, 0)
    v2, i2 = _first_argmax(jnp.where(erow == i1, -jnp.inf, e_logits), EXPERTS_PER_GROUP)
    t = jnp.exp(v2 - v1)
    w1 = g_gate / (1.0 + t)
    w2 = g_gate * t / (1.0 + t)
    e1 = g_idx * EXPERTS_PER_GROUP + i1
    e2 = g_idx * EXPERTS_PER_GROUP + i2

    xrow = lax.broadcasted_iota(I32, (N_EXPERTS, TM_MIX), 0)
    oh1 = xrow == e1
    oh2 = xrow == e2
    oh = jnp.where(jnp.logical_or(oh1, oh2), 1.0, 0.0)
    tr = lax.broadcasted_iota(I32, (TM_MIX, TM_MIX), 0)
    tc = lax.broadcasted_iota(I32, (TM_MIX, TM_MIX), 1)
    before = jnp.where(tr < tc, 1.0, 0.0).astype(BF16)
    cum = jnp.dot(oh.astype(BF16), before, preferred_element_type=F32)
    base = cum + run_ref[:, 0:1]
    rank1 = jnp.sum(jnp.where(oh1, base, 0.0), axis=0, keepdims=True).astype(I32)
    rank2 = jnp.sum(jnp.where(oh2, base, 0.0), axis=0, keepdims=True).astype(I32)
    run_ref[...] = run_ref[...] + jnp.sum(oh, axis=1, keepdims=True)
    cnt_ref[...] = run_ref[...].astype(I32)

    zi = jnp.zeros((ROUTE_ROWS - 4, TM_MIX), I32)
    ri_ref[...] = jnp.concatenate([e1, e2, rank1, rank2, zi], axis=0)
    zw = jnp.zeros((ROUTE_ROWS - 2, TM_MIX), F32)
    rw_ref[...] = jnp.concatenate([w1, w2, zw], axis=0)


def _mix(a2, proj2, x2, wout_b, sgw_b, sgb, sgg, sgbeta, g1, b1, wr, br):
    t, d = x2.shape
    u_blk = (proj2.shape[1] - 2 * SG_WIDTH) // SG_WIDTH
    const = lambda *shape: pl.BlockSpec(shape, lambda i: (0,) * len(shape))
    return pl.pallas_call(
        _mix_kernel,
        grid=(t // TM_MIX,),
        in_specs=[
            pl.BlockSpec((TM_MIX, ATT_WIDTH), lambda i: (i, 0)),
            pl.BlockSpec((TM_MIX, SG_WIDTH), lambda i: (i, u_blk)),
            pl.BlockSpec((TM_MIX, SG_WIDTH), lambda i: (i, u_blk + 1)),
            pl.BlockSpec((TM_MIX, d), lambda i: (i, 0)),
            const(d, d),
            const(SG_GROUPS, SG_CHUNK, SG_CHUNK),
            const(SG_GROUPS, SG_CHUNK, 1),
            const(SG_GROUPS, SG_DIM),
            const(SG_GROUPS, SG_DIM),
            const(1, d),
            const(1, d),
            const(NR_PAD, d),
            const(NR_PAD, 1),
        ],
        out_specs=[
            pl.BlockSpec((TM_MIX, d), lambda i: (i, 0)),
            pl.BlockSpec((TM_MIX, d // 2), lambda i: (i, 0)),
            pl.BlockSpec((ROUTE_ROWS, TM_MIX), lambda i: (0, i)),
            pl.BlockSpec((ROUTE_ROWS, TM_MIX), lambda i: (0, i)),
            const(N_EXPERTS, LANES),
        ],
        out_shape=[
            jax.ShapeDtypeStruct((t, d), F32),
            jax.ShapeDtypeStruct((t, d // 2), U32),
            jax.ShapeDtypeStruct((ROUTE_ROWS, t), I32),
            jax.ShapeDtypeStruct((ROUTE_ROWS, t), F32),
            jax.ShapeDtypeStruct((N_EXPERTS, LANES), I32),
        ],
        scratch_shapes=[pltpu.VMEM((N_EXPERTS, LANES), F32)],
        compiler_params=_cparams(("arbitrary",)),
        name="mix",
    )(a2, proj2, proj2, x2, wout_b, sgw_b, sgb, sgg, sgbeta, g1, b1, wr, br)


DISPATCH_CHUNK = 256


def _dispatch_kernel(pos_ref, last_ref, hp_ref, xs_hbm, zero_ref, zsem, sem):
    i = pl.program_id(0)

    def row_copy(j, dst):
        return pltpu.make_async_copy(hp_ref.at[pl.ds(j, 1)], xs_hbm.at[pl.ds(dst, 1)], sem)

    def zero_copy(e):
        start = pl.multiple_of(last_ref[e] * TMX, TMX)
        return pltpu.make_async_copy(zero_ref, xs_hbm.at[pl.ds(start, TMX)], zsem)

    @pl.when(i == 0)
    def _():
        zero_ref[...] = jnp.zeros_like(zero_ref)

        def zstart(e, c):
            zero_copy(e).start()
            return c

        def zwait(e, c):
            zero_copy(e).wait()
            return c

        lax.fori_loop(0, N_EXPERTS, zstart, 0)
        lax.fori_loop(0, N_EXPERTS, zwait, 0)

    def start(j, c):
        row_copy(j, pos_ref[0, j]).start()
        row_copy(j, pos_ref[1, j]).start()
        return c

    def wait(j, c):
        row_copy(j, pos_ref[0, j]).wait()
        row_copy(j, pos_ref[1, j]).wait()
        return c

    lax.fori_loop(0, DISPATCH_CHUNK, start, 0, unroll=8)
    lax.fori_loop(0, DISPATCH_CHUNK, wait, 0, unroll=8)


def _dispatch(pos3, last_tile, hp, n_rows):
    t, w = hp.shape
    return pl.pallas_call(
        _dispatch_kernel,
        grid=(t // DISPATCH_CHUNK,),
        in_specs=[
            pl.BlockSpec((None, 2, DISPATCH_CHUNK), lambda i: (i, 0, 0), memory_space=pltpu.SMEM),
            pl.BlockSpec(memory_space=pltpu.SMEM),
            pl.BlockSpec((DISPATCH_CHUNK, w), lambda i: (i, 0)),
        ],
        out_specs=pl.BlockSpec(memory_space=pl.ANY),
        out_shape=jax.ShapeDtypeStruct((n_rows, w), U32),
        scratch_shapes=[pltpu.VMEM((TMX, w), U32),
                        pltpu.SemaphoreType.DMA(()),
                        pltpu.SemaphoreType.DMA(())],
        compiler_params=_cparams(("arbitrary",)),
        name="dispatch",
    )(pos3, last_tile, hp)


def _experts_kernel(te_ref, nu_ref, x_ref, wg_ref, wu_ref, wd_ref, y_ref, wgu_s, wd_s):
    i = pl.program_id(0)
    d = wg_ref.shape[1]

    @pl.when(i < nu_ref[0])
    def _():
        first = jnp.logical_or(i == 0, te_ref[i] != te_ref[jnp.maximum(i - 1, 0)])

        @pl.when(first)
        def _():
            wgu_s[:, :D_EXPERT] = wg_ref[0].astype(BF16)
            wgu_s[:, D_EXPERT:] = wu_ref[0].astype(BF16)
            wd_s[...] = wd_ref[0].astype(BF16)

        lo, hi = _unpack_bf16_pair(x_ref[...])
        gu = jnp.dot(lo.astype(BF16), wgu_s[:d // 2, :], preferred_element_type=F32)
        gu = gu + jnp.dot(hi.astype(BF16), wgu_s[d // 2:, :], preferred_element_type=F32)
        gate = gu[:, :D_EXPERT]
        hmid = (gate * jax.nn.sigmoid(gate) * gu[:, D_EXPERT:]).astype(BF16)
        y = jnp.dot(hmid, wd_s[...], preferred_element_type=F32)
        y_ref[...] = _pack_bf16_pair(y)


def _experts(tile_expert, n_used, xs, w_gate, w_up, w_down):
    n_rows, w = xs.shape
    d = w_gate.shape[1]
    n_tiles = n_rows // TMX

    def row_map(i, te, nu):
        return (jnp.minimum(i, nu[0] - 1), 0)

    def w_map(i, te, nu):
        return (te[i], 0, 0)

    grid_spec = pltpu.PrefetchScalarGridSpec(
        num_scalar_prefetch=2,
        grid=(n_tiles,),
        in_specs=[
            pl.BlockSpec((TMX, w), row_map),
            pl.BlockSpec((1, d, D_EXPERT), w_map),
            pl.BlockSpec((1, d, D_EXPERT), w_map),
            pl.BlockSpec((1, D_EXPERT, d), w_map),
        ],
        out_specs=pl.BlockSpec((TMX, w), row_map),
        scratch_shapes=[pltpu.VMEM((d, 2 * D_EXPERT), BF16),
                        pltpu.VMEM((D_EXPERT, d), BF16)],
    )
    return pl.pallas_call(
        _experts_kernel,
        grid_spec=grid_spec,
        out_shape=jax.ShapeDtypeStruct((n_rows, w), U32),
        compiler_params=_cparams(("arbitrary",)),
        name="experts",
    )(tile_expert, n_used, xs, w_gate, w_up, w_down)


def _combine_kernel(pos_ref, posn_ref, y_hbm, h1_ref, wt_ref, g2_ref, b2_ref, o_ref, ybuf, sem):
    i = pl.program_id(0)
    n = pl.num_programs(0)
    slot = i % 2

    def row_copy(src, sl, which, j):
        return pltpu.make_async_copy(y_hbm.at[pl.ds(src, 1)],
                                     ybuf.at[sl, which, pl.ds(j, 1)], sem.at[sl])

    def issue(p_ref, sl):
        def body(j, c):
            row_copy(p_ref[0, j], sl, 0, j).start()
            row_copy(p_ref[1, j], sl, 1, j).start()
            return c
        lax.fori_loop(0, TM_OUT, body, 0, unroll=8)

    @pl.when(i == 0)
    def _():
        issue(pos_ref, 0)

    @pl.when(i + 1 < n)
    def _():
        issue(posn_ref, 1 - slot)

    def wait(j, c):
        row_copy(pos_ref[0, j], slot, 0, j).wait()
        row_copy(pos_ref[1, j], slot, 1, j).wait()
        return c

    lax.fori_loop(0, TM_OUT, wait, 0, unroll=8)

    lo1, hi1 = _unpack_bf16_pair(ybuf[slot, 0])
    lo2, hi2 = _unpack_bf16_pair(ybuf[slot, 1])
    w1 = wt_ref[:, 0:1]
    w2 = wt_ref[:, 1:2]
    ffn = jnp.concatenate([w1 * lo1 + w2 * lo2, w1 * hi1 + w2 * hi2], axis=1)
    o_ref[...] = _layernorm(DN_ALPHA * h1_ref[...] + ffn, g2_ref[...], b2_ref[...])


def _combine(pos3, ys, h1, wtok, g2, b2):
    t, d = h1.shape
    w = ys.shape[1]
    n = t // TM_OUT
    return pl.pallas_call(
        _combine_kernel,
        grid=(n,),
        in_specs=[
            pl.BlockSpec((None, 2, TM_OUT), lambda i: (i, 0, 0), memory_space=pltpu.SMEM),
            pl.BlockSpec((None, 2, TM_OUT), lambda i: (jnp.minimum(i + 1, n - 1), 0, 0),
                         memory_space=pltpu.SMEM),
            pl.BlockSpec(memory_space=pl.ANY),
            pl.BlockSpec((TM_OUT, d), lambda i: (i, 0)),
            pl.BlockSpec((TM_OUT, ROUTE_ROWS), lambda i: (i, 0)),
            pl.BlockSpec((1, d), lambda i: (0, 0)),
            pl.BlockSpec((1, d), lambda i: (0, 0)),
        ],
        out_specs=pl.BlockSpec((TM_OUT, d), lambda i: (i, 0)),
        out_shape=jax.ShapeDtypeStruct((t, d), F32),
        scratch_shapes=[pltpu.VMEM((2, 2, TM_OUT, w), U32),
                        pltpu.SemaphoreType.DMA((2,))],
        compiler_params=_cparams(("arbitrary",)),
        name="combine",
    )(pos3, pos3, ys, h1, wtok, g2, b2)


def _lambda_init(layer):
    return 0.8 - 0.6 * math.exp(-0.3 * layer)


def kernel(x, w_in, w_out, ln1_g, ln1_b, ln2_g, ln2_b, rel_bias, lam_q1, lam_k1, lam_q2, lam_k2,
           subln_g, sg_ln_g, sg_ln_b, sg_w, sg_b, w_router_group, b_router_group,
           w_router_expert, b_router_expert, w_exp_gate, w_exp_up, w_exp_down):
    b, s, d = x.shape
    t = b * s
    assert w_in.shape[0] == DEPTH == 1
    l = 0
    lam_init = _lambda_init(l)
    x2 = x.reshape(t, d)

    proj = _inproj(x2, w_in[l])

    lam = (jnp.exp(jnp.sum(lam_q1[l].astype(F32) * lam_k1[l].astype(F32)))
           - jnp.exp(jnp.sum(lam_q2[l].astype(F32) * lam_k2[l].astype(F32))) + lam_init)
    a = _attention(proj.reshape(b, s, -1), _bias_tiles(rel_bias, s), lam.reshape(1),
                   subln_g[l].reshape(1, -1), lam_init)

    wr = jnp.concatenate([w_router_group[l].T,
                          jnp.zeros((NR_EXPERT0 - N_GROUPS, d), F32),
                          jnp.transpose(w_router_expert[l], (0, 2, 1)).reshape(N_EXPERTS, d)],
                         axis=0)
    br = jnp.concatenate([b_router_group[l], jnp.zeros((NR_EXPERT0 - N_GROUPS,), F32),
                          b_router_expert[l].reshape(-1)]).reshape(NR_PAD, 1)
    h1, hp, ri, rw, cnt = _mix(
        a.reshape(t, -1), proj, x2, w_out[l].astype(BF16), sg_w[l].astype(BF16),
        sg_b[l].reshape(SG_GROUPS, SG_CHUNK, 1), sg_ln_g[l], sg_ln_b[l],
        ln1_g[l].reshape(1, d), ln1_b[l].reshape(1, d), wr, br)

    counts = cnt[:, 0]
    tiles_e = (counts + TMX - 1) // TMX
    tile_end = jnp.cumsum(tiles_e)
    tile_start = tile_end - tiles_e
    n_used = tile_end[-1:]
    max_tiles = (2 * t) // TMX + N_EXPERTS
    tile_ids = jnp.arange(max_tiles, dtype=I32)
    tile_expert = jnp.minimum(jnp.searchsorted(tile_end, tile_ids, side="right"),
                              N_EXPERTS - 1).astype(I32)
    tile_expert = jnp.where(tile_ids < n_used[0], tile_expert,
                            tile_expert[jnp.maximum(n_used[0] - 1, 0)])
    last_tile = jnp.maximum(tile_end - 1, 0).astype(I32)
    row_start = (tile_start * TMX).astype(I32)
    eids = jnp.arange(N_EXPERTS, dtype=I32)[:, None, None]
    pos = jnp.sum(jnp.where(ri[0:2][None] == eids, row_start[:, None, None], 0), axis=0) + ri[2:4]
    pos3 = pos.reshape(2, t // TM_OUT, TM_OUT).transpose(1, 0, 2)

    xs = _dispatch(pos3, last_tile, hp, max_tiles * TMX)
    ys = _experts(tile_expert, n_used.astype(I32), xs, w_exp_gate[l], w_exp_up[l], w_exp_down[l])
    out = _combine(pos3, ys, h1, rw.T, ln2_g[l].reshape(1, d), ln2_b[l].reshape(1, d))
    return out.reshape(b, s, d)
```

```python
import functools
import math

import jax
import jax.numpy as jnp
from jax import lax
from jax.experimental import pallas as pl
from jax.experimental.pallas import tpu as pltpu

F32 = jnp.float32
BF16 = jnp.bfloat16
U32 = jnp.uint32
I32 = jnp.int32

ATT_HEADS = 8
ATT_QK_DIM = 64
ATT_V_DIM = 128
ATT_WIDTH = ATT_HEADS * ATT_V_DIM
SG_GROUPS = 8
SG_DIM = 128
SG_WIDTH = SG_GROUPS * SG_DIM
SG_CHUNK = 128
REL_BUCKETS = 32
REL_MAX_DIST = 128
N_GROUPS = 4
EXPERTS_PER_GROUP = 8
N_EXPERTS = N_GROUPS * EXPERTS_PER_GROUP
D_EXPERT = 256
DEPTH = 1
DN_ALPHA = (2.0 * DEPTH) ** 0.25
LN_EPS = 1e-5
LOG2E = math.log2(math.e)

LANES = 128
SUBLANES = 8
VMEM_LIMIT_BYTES = 56 * 1024 * 1024

TM_IN = 1024
TN_IN = 1024
QB = 256
KC = 256
TM_MIX = 512
TMX = 256
TM_OUT = 256
ROUTE_ROWS = 8
NR_EXPERT0 = 8
NR_PAD = NR_EXPERT0 + 32


def _cparams(sem):
    return pltpu.CompilerParams(dimension_semantics=sem, vmem_limit_bytes=VMEM_LIMIT_BYTES)


def _inproj_kernel(x_ref, w_ref, o_ref, xb_ref, *, n_q, n_lin, q_scale):
    j = pl.program_id(1)

    @pl.when(j == 0)
    def _():
        xb_ref[...] = x_ref[...].astype(BF16)

    acc = jnp.dot(xb_ref[...], w_ref[...].astype(BF16), preferred_element_type=F32)

    @pl.when(j < n_q)
    def _():
        o_ref[...] = (acc * q_scale).astype(BF16)

    @pl.when(jnp.logical_and(j >= n_q, j < n_lin))
    def _():
        o_ref[...] = acc.astype(BF16)

    @pl.when(j >= n_lin)
    def _():
        o_ref[...] = jax.nn.gelu(acc).astype(BF16)


def _inproj(x2, w_in):
    t, d = x2.shape
    n = w_in.shape[1]
    n_q = (ATT_HEADS * 2 * ATT_QK_DIM) // TN_IN
    n_lin = (2 * ATT_HEADS * 2 * ATT_QK_DIM + ATT_WIDTH) // TN_IN
    kern = functools.partial(_inproj_kernel, n_q=n_q, n_lin=n_lin,
                             q_scale=(ATT_QK_DIM ** -0.5) * LOG2E)
    return pl.pallas_call(
        kern,
        grid=(t // TM_IN, n // TN_IN),
        in_specs=[pl.BlockSpec((TM_IN, d), lambda i, j: (i, 0)),
                  pl.BlockSpec((d, TN_IN), lambda i, j: (0, j))],
        out_specs=pl.BlockSpec((TM_IN, TN_IN), lambda i, j: (i, j)),
        out_shape=jax.ShapeDtypeStruct((t, n), BF16),
        scratch_shapes=[pltpu.VMEM((TM_IN, d), BF16)],
        compiler_params=_cparams(("arbitrary", "arbitrary")),
        name="inproj",
    )(x2, w_in)


def _attn_kernel(lam_ref, q_ref, k_ref, v_ref, bias_ref, g_ref, o_ref, vaug_ref,
                 *, n_kc, out_scale):
    qi = pl.program_id(2)

    @pl.when(qi == 0)
    def _():
        vaug_ref[:, :ATT_V_DIM] = v_ref[0]
        vaug_ref[:, ATT_V_DIM:] = jnp.ones((vaug_ref.shape[0], ATT_V_DIM), BF16)

    q = q_ref[0]
    lane = lax.broadcasted_iota(I32, q.shape, 1)
    zero = jnp.zeros_like(q)
    qq = jnp.concatenate([jnp.where(lane < ATT_QK_DIM, q, zero),
                          jnp.where(lane >= ATT_QK_DIM, q, zero)], axis=0)

    m = jnp.full((2 * QB, 1), -jnp.inf, F32)
    acc = jnp.zeros((2 * QB, 2 * ATT_V_DIM), F32)
    for kc in range(n_kc):
        kblk = k_ref[0, kc * KC:(kc + 1) * KC, :]
        s = lax.dot_general(qq, kblk, (((1,), (1,)), ((), ())),
                            preferred_element_type=F32)
        b = bias_ref[0, jnp.clip(kc - qi, -2, 2) + 2]
        s = s + jnp.concatenate([b, b], axis=0)
        m_new = jnp.maximum(m, jnp.max(s, axis=1, keepdims=True))
        p = jnp.exp2(s - m_new).astype(BF16)
        acc = acc * jnp.exp2(m - m_new) + jnp.dot(
            p, vaug_ref[kc * KC:(kc + 1) * KC, :], preferred_element_type=F32)
        m = m_new
    lam = lam_ref[0]
    o0 = acc[:QB, :ATT_V_DIM] / acc[:QB, ATT_V_DIM:ATT_V_DIM + 1]
    o1 = acc[QB:, :ATT_V_DIM] / acc[QB:, ATT_V_DIM:ATT_V_DIM + 1]
    of = o0 - lam * o1
    of = of * lax.rsqrt(jnp.mean(of * of, axis=-1, keepdims=True) + LN_EPS)
    o_ref[0] = (of * (g_ref[...] * out_scale)).astype(o_ref.dtype)


def _attention(proj3, bias5, lam, subln_g, lam_init):
    b, s, _ = proj3.shape
    n_kc = s // KC
    h_off_k = ATT_HEADS
    h_off_v = 2 * ATT_HEADS
    kern = functools.partial(_attn_kernel, n_kc=n_kc, out_scale=1.0 - lam_init)
    return pl.pallas_call(
        kern,
        grid=(b, ATT_HEADS, s // QB),
        in_specs=[
            pl.BlockSpec(memory_space=pltpu.SMEM),
            pl.BlockSpec((1, QB, LANES), lambda bi, h, qi: (bi, qi, h)),
            pl.BlockSpec((1, s, LANES), lambda bi, h, qi: (bi, 0, h_off_k + h)),
            pl.BlockSpec((1, s, LANES), lambda bi, h, qi: (bi, 0, h_off_v + h)),
            pl.BlockSpec((1, 5, QB, KC), lambda bi, h, qi: (h, 0, 0, 0)),
            pl.BlockSpec((1, ATT_V_DIM), lambda bi, h, qi: (0, 0)),
        ],
        out_specs=pl.BlockSpec((1, QB, ATT_V_DIM), lambda bi, h, qi: (bi, qi, h)),
        out_shape=jax.ShapeDtypeStruct((b, s, ATT_WIDTH), BF16),
        scratch_shapes=[pltpu.VMEM((s, 2 * ATT_V_DIM), BF16)],
        compiler_params=_cparams(("arbitrary", "arbitrary", "arbitrary")),
        name="attn",
    )(lam, proj3, proj3, proj3, bias5, subln_g)


def _rel_bucket(rel):
    half = REL_BUCKETS // 2
    max_exact = half // 2
    ret = jnp.where(rel > 0, half, 0)
    n = jnp.abs(rel)
    nf = jnp.maximum(n, 1).astype(F32)
    large = max_exact + (jnp.log(nf / max_exact) / math.log(REL_MAX_DIST / max_exact)
                         * (half - max_exact)).astype(I32)
    large = jnp.minimum(large, half - 1)
    return ret + jnp.where(n < max_exact, n, large)


def _bias_tiles(rel_bias, s):
    assert REL_MAX_DIST <= KC and QB == KC
    period = 2 * KC
    j = jnp.arange(period, dtype=I32)
    off = jnp.where(j < KC, j, j - period)
    d = jnp.arange(-2, 3, dtype=I32)[:, None]
    rel = jnp.clip(d * KC + off[None, :], -(s - 1), s - 1)
    g = jnp.transpose(rel_bias[_rel_bucket(rel)], (2, 0, 1)).astype(F32) * LOG2E
    flat = jnp.tile(g, (1, 1, QB))[..., :QB * (period - 1)]
    return flat.reshape(ATT_HEADS, 5, QB, period - 1)[..., :KC]


def _layernorm(v, g, b):
    mu = jnp.mean(v, axis=-1, keepdims=True)
    var = jnp.mean(jnp.square(v - mu), axis=-1, keepdims=True)
    return (v - mu) * lax.rsqrt(var + LN_EPS) * g + b


def _pack_bf16_pair(v):
    n = v.shape[1] // 2
    bits = lax.bitcast_convert_type(v.astype(BF16).astype(F32), U32)
    return (bits[:, :n] >> 16) | (bits[:, n:] & jnp.uint32(0xFFFF0000))


def _unpack_bf16_pair(w):
    lo = lax.bitcast_convert_type(w << 16, F32)
    hi = lax.bitcast_convert_type(w & jnp.uint32(0xFFFF0000), F32)
    return lo, hi


def _first_argmax(vals, nrows):
    mx = jnp.max(vals, axis=0, keepdims=True)
    row = lax.broadcasted_iota(I32, vals.shape, 0)
    idx = jnp.min(jnp.where(vals == mx, row, nrows), axis=0, keepdims=True)
    return mx, idx


def _mix_kernel(a_ref, u_ref, vg_ref, x_ref, wout_ref, sgw_ref, sgb_ref, sgg_ref, sgbeta_ref,
                g1_ref, b1_ref, wr_ref, br_ref,
                h1_ref, hp_ref, ri_ref, rw_ref, cnt_ref, run_ref):
    i = pl.program_id(0)

    @pl.when(i == 0)
    def _():
        run_ref[...] = jnp.zeros_like(run_ref)

    n_chunks = TM_MIX // SG_CHUNK
    rows = []
    for c in range(n_chunks):
        cols = []
        for g in range(SG_GROUPS):
            rs = slice(c * SG_CHUNK, (c + 1) * SG_CHUNK)
            cs = slice(g * SG_DIM, (g + 1) * SG_DIM)
            vt = vg_ref[rs, cs].astype(F32)
            vn = _layernorm(vt, sgg_ref[g:g + 1, :], sgbeta_ref[g:g + 1, :])
            mixed = jnp.dot(sgw_ref[g], vn.astype(BF16), preferred_element_type=F32) + sgb_ref[g]
            cols.append((u_ref[rs, cs].astype(F32) * mixed).astype(BF16))
        rows.append(jnp.concatenate(cols, axis=1))
    sgate = jnp.concatenate(rows, axis=0)

    mix = jnp.dot(a_ref[...], wout_ref[:ATT_WIDTH, :], preferred_element_type=F32)
    mix = mix + jnp.dot(sgate, wout_ref[ATT_WIDTH:, :], preferred_element_type=F32)
    h1 = _layernorm(DN_ALPHA * x_ref[...] + mix, g1_ref[...], b1_ref[...])
    h1_ref[...] = h1
    hp_ref[...] = _pack_bf16_pair(h1)

    lt = lax.dot_general(wr_ref[...], h1, (((1,), (1,)), ((), ())),
                         preferred_element_type=F32, precision=lax.Precision.HIGHEST)
    lt = lt + br_ref[...]
    g_logits = lt[0:N_GROUPS]
    gmax, g_idx = _first_argmax(g_logits, N_GROUPS)
    g_gate = 1.0 / jnp.sum(jnp.exp(g_logits - gmax), axis=0, keepdims=True)
    e_logits = jnp.zeros((EXPERTS_PER_GROUP, TM_MIX), F32)
    for g in range(N_GROUPS):
        lo = NR_EXPERT0 + g * EXPERTS_PER_GROUP
        e_logits = jnp.where(g_idx == g, lt[lo:lo + EXPERTS_PER_GROUP], e_logits)
    v1, i1 = _first_argmax(e_logits, EXPERTS_PER_GROUP)
    erow = lax.broadcasted_iota(I32, e_logits.shape, 0)
    v2, i2 = _first_argmax(jnp.where(erow == i1, -jnp.inf, e_logits), EXPERTS_PER_GROUP)
    t = jnp.exp(v2 - v1)
    w1 = g_gate / (1.0 + t)
    w2 = g_gate * t / (1.0 + t)
    e1 = g_idx * EXPERTS_PER_GROUP + i1
    e2 = g_idx * EXPERTS_PER_GROUP + i2

    xrow = lax.broadcasted_iota(I32, (N_EXPERTS, TM_MIX), 0)
    oh1 = xrow == e1
    oh2 = xrow == e2
    oh = jnp.where(jnp.logical_or(oh1, oh2), 1.0, 0.0)
    tr = lax.broadcasted_iota(I32, (TM_MIX, TM_MIX), 0)
    tc = lax.broadcasted_iota(I32, (TM_MIX, TM_MIX), 1)
    before = jnp.where(tr < tc, 1.0, 0.0).astype(BF16)
    cum = jnp.dot(oh.astype(BF16), before, preferred_element_type=F32)
    base = cum + run_ref[:, 0:1]
    rank1 = jnp.sum(jnp.where(oh1, base, 0.0), axis=0, keepdims=True).astype(I32)
    rank2 = jnp.sum(jnp.where(oh2, base, 0.0), axis=0, keepdims=True).astype(I32)
    run_ref[...] = run_ref[...] + jnp.sum(oh, axis=1, keepdims=True)
    cnt_ref[...] = run_ref[...].astype(I32)

    zi = jnp.zeros((ROUTE_ROWS - 4, TM_MIX), I32)
    ri_ref[...] = jnp.concatenate([e1, e2, rank1, rank2, zi], axis=0)
    zw = jnp.zeros((ROUTE_ROWS - 2, TM_MIX), F32)
    rw_ref[...] = jnp.concatenate([w1, w2, zw], axis=0)


def _mix(a2, proj2, x2, wout_b, sgw_b, sgb, sgg, sgbeta, g1, b1, wr, br):
    t, d = x2.shape
    u_blk = (proj2.shape[1] - 2 * SG_WIDTH) // SG_WIDTH
    const = lambda *shape: pl.BlockSpec(shape, lambda i: (0,) * len(shape))
    return pl.pallas_call(
        _mix_kernel,
        grid=(t // TM_MIX,),
        in_specs=[
            pl.BlockSpec((TM_MIX, ATT_WIDTH), lambda i: (i, 0)),
            pl.BlockSpec((TM_MIX, SG_WIDTH), lambda i: (i, u_blk)),
            pl.BlockSpec((TM_MIX, SG_WIDTH), lambda i: (i, u_blk + 1)),
            pl.BlockSpec((TM_MIX, d), lambda i: (i, 0)),
            pl.BlockSpec((d, d), lambda i: (0, 0), pipeline_mode=pl.Buffered(1)),
            const(SG_GROUPS, SG_CHUNK, SG_CHUNK),
            const(SG_GROUPS, SG_CHUNK, 1),
            const(SG_GROUPS, SG_DIM),
            const(SG_GROUPS, SG_DIM),
            const(1, d),
            const(1, d),
            const(NR_PAD, d),
            const(NR_PAD, 1),
        ],
        out_specs=[
            pl.BlockSpec((TM_MIX, d), lambda i: (i, 0)),
            pl.BlockSpec((TM_MIX, d // 2), lambda i: (i, 0)),
            pl.BlockSpec((ROUTE_ROWS, TM_MIX), lambda i: (0, i)),
            pl.BlockSpec((ROUTE_ROWS, TM_MIX), lambda i: (0, i)),
            const(N_EXPERTS, LANES),
        ],
        out_shape=[
            jax.ShapeDtypeStruct((t, d), F32),
            jax.ShapeDtypeStruct((t, d // 2), U32),
            jax.ShapeDtypeStruct((ROUTE_ROWS, t), I32),
            jax.ShapeDtypeStruct((ROUTE_ROWS, t), F32),
            jax.ShapeDtypeStruct((N_EXPERTS, LANES), I32),
        ],
        scratch_shapes=[pltpu.VMEM((N_EXPERTS, LANES), F32)],
        compiler_params=_cparams(("arbitrary",)),
        name="mix",
    )(a2, proj2, proj2, x2, wout_b, sgw_b, sgb, sgg, sgbeta, g1, b1, wr, br)


DISPATCH_CHUNK = 256


def _dispatch_kernel(pos_ref, last_ref, hp_ref, xs_hbm, zero_ref, zsem, sem):
    i = pl.program_id(0)

    def row_copy(j, dst):
        return pltpu.make_async_copy(hp_ref.at[pl.ds(j, 1)], xs_hbm.at[pl.ds(dst, 1)], sem)

    def zero_copy(e):
        start = pl.multiple_of(last_ref[e] * TMX, TMX)
        return pltpu.make_async_copy(zero_ref, xs_hbm.at[pl.ds(start, TMX)], zsem)

    @pl.when(i == 0)
    def _():
        zero_ref[...] = jnp.zeros_like(zero_ref)

        def zstart(e, c):
            zero_copy(e).start()
            return c

        def zwait(e, c):
            zero_copy(e).wait()
            return c

        lax.fori_loop(0, N_EXPERTS, zstart, 0)
        lax.fori_loop(0, N_EXPERTS, zwait, 0)

    def start(j, c):
        row_copy(j, pos_ref[0, j]).start()
        row_copy(j, pos_ref[1, j]).start()
        return c

    def wait(j, c):
        row_copy(j, pos_ref[0, j]).wait()
        row_copy(j, pos_ref[1, j]).wait()
        return c

    lax.fori_loop(0, DISPATCH_CHUNK, start, 0, unroll=8)
    lax.fori_loop(0, DISPATCH_CHUNK, wait, 0, unroll=8)


def _dispatch(pos3, last_tile, hp, n_rows):
    t, w = hp.shape
    return pl.pallas_call(
        _dispatch_kernel,
        grid=(t // DISPATCH_CHUNK,),
        in_specs=[
            pl.BlockSpec((None, 2, DISPATCH_CHUNK), lambda i: (i, 0, 0), memory_space=pltpu.SMEM),
            pl.BlockSpec(memory_space=pltpu.SMEM),
            pl.BlockSpec((DISPATCH_CHUNK, w), lambda i: (i, 0)),
        ],
        out_specs=pl.BlockSpec(memory_space=pl.ANY),
        out_shape=jax.ShapeDtypeStruct((n_rows, w), U32),
        scratch_shapes=[pltpu.VMEM((TMX, w), U32),
                        pltpu.SemaphoreType.DMA(()),
                        pltpu.SemaphoreType.DMA(())],
        compiler_params=_cparams(("arbitrary",)),
        name="dispatch",
    )(pos3, last_tile, hp)


def _experts_kernel(te_ref, nu_ref, x_ref, wg_ref, wu_ref, wd_ref, y_ref, wgu_s, wd_s):
    i = pl.program_id(0)
    d = wg_ref.shape[1]

    @pl.when(i < nu_ref[0])
    def _():
        first = jnp.logical_or(i == 0, te_ref[i] != te_ref[jnp.maximum(i - 1, 0)])

        @pl.when(first)
        def _():
            wgu_s[:, :D_EXPERT] = wg_ref[0].astype(BF16)
            wgu_s[:, D_EXPERT:] = wu_ref[0].astype(BF16)
            wd_s[...] = wd_ref[0].astype(BF16)

        lo, hi = _unpack_bf16_pair(x_ref[...])
        gu = jnp.dot(lo.astype(BF16), wgu_s[:d // 2, :], preferred_element_type=F32)
        gu = gu + jnp.dot(hi.astype(BF16), wgu_s[d // 2:, :], preferred_element_type=F32)
        gate = gu[:, :D_EXPERT]
        hmid = (gate * jax.nn.sigmoid(gate) * gu[:, D_EXPERT:]).astype(BF16)
        y = jnp.dot(hmid, wd_s[...], preferred_element_type=F32)
        y_ref[...] = _pack_bf16_pair(y)


def _experts(tile_expert, n_used, xs, w_gate, w_up, w_down):
    n_rows, w = xs.shape
    d = w_gate.shape[1]
    n_tiles = n_rows // TMX

    def row_map(i, te, nu):
        return (jnp.minimum(i, jnp.maximum(nu[0] - 1, 0)), 0)

    def w_map(i, te, nu):
        return (te[i], 0, 0)

    grid_spec = pltpu.PrefetchScalarGridSpec(
        num_scalar_prefetch=2,
        grid=(n_tiles,),
        in_specs=[
            pl.BlockSpec((TMX, w), row_map),
            pl.BlockSpec((1, d, D_EXPERT), w_map),
            pl.BlockSpec((1, d, D_EXPERT), w_map),
            pl.BlockSpec((1, D_EXPERT, d), w_map),
        ],
        out_specs=pl.BlockSpec((TMX, w), row_map),
        scratch_shapes=[pltpu.VMEM((d, 2 * D_EXPERT), BF16),
                        pltpu.VMEM((D_EXPERT, d), BF16)],
    )
    return pl.pallas_call(
        _experts_kernel,
        grid_spec=grid_spec,
        out_shape=jax.ShapeDtypeStruct((n_rows, w), U32),
        compiler_params=_cparams(("arbitrary",)),
        name="experts",
    )(tile_expert, n_used, xs, w_gate, w_up, w_down)


def _combine_kernel(pos_ref, posn_ref, y_hbm, h1_ref, wt_ref, g2_ref, b2_ref, o_ref, ybuf, sem):
    i = pl.program_id(0)
    n = pl.num_programs(0)
    slot = i % 2

    def row_copy(src, sl, which, j):
        return pltpu.make_async_copy(y_hbm.at[pl.ds(src, 1)],
                                     ybuf.at[sl, which, pl.ds(j, 1)], sem.at[sl])

    def issue(p_ref, sl):
        def body(j, c):
            row_copy(p_ref[0, j], sl, 0, j).start()
            row_copy(p_ref[1, j], sl, 1, j).start()
            return c
        lax.fori_loop(0, TM_OUT, body, 0, unroll=8)

    @pl.when(i == 0)
    def _():
        issue(pos_ref, 0)

    @pl.when(i + 1 < n)
    def _():
        issue(posn_ref, 1 - slot)

    def wait(j, c):
        row_copy(pos_ref[0, j], slot, 0, j).wait()
        row_copy(pos_ref[1, j], slot, 1, j).wait()
        return c

    lax.fori_loop(0, TM_OUT, wait, 0, unroll=8)

    lo1, hi1 = _unpack_bf16_pair(ybuf[slot, 0])
    lo2, hi2 = _unpack_bf16_pair(ybuf[slot, 1])
    w1 = wt_ref[:, 0:1]
    w2 = wt_ref[:, 1:2]
    ffn = jnp.concatenate([w1 * lo1 + w2 * lo2, w1 * hi1 + w2 * hi2], axis=1)
    o_ref[...] = _layernorm(DN_ALPHA * h1_ref[...] + ffn, g2_ref[...], b2_ref[...])


def _combine(pos3, ys, h1, wtok, g2, b2):
    t, d = h1.shape
    w = ys.shape[1]
    n = t // TM_OUT
    return pl.pallas_call(
        _combine_kernel,
        grid=(n,),
        in_specs=[
            pl.BlockSpec((None, 2, TM_OUT), lambda i: (i, 0, 0), memory_space=pltpu.SMEM),
            pl.BlockSpec((None, 2, TM_OUT), lambda i: (jnp.minimum(i + 1, n - 1), 0, 0),
                         memory_space=pltpu.SMEM),
            pl.BlockSpec(memory_space=pl.ANY),
            pl.BlockSpec((TM_OUT, d), lambda i: (i, 0)),
            pl.BlockSpec((TM_OUT, ROUTE_ROWS), lambda i: (i, 0)),
            pl.BlockSpec((1, d), lambda i: (0, 0)),
            pl.BlockSpec((1, d), lambda i: (0, 0)),
        ],
        out_specs=pl.BlockSpec((TM_OUT, d), lambda i: (i, 0)),
        out_shape=jax.ShapeDtypeStruct((t, d), F32),
        scratch_shapes=[pltpu.VMEM((2, 2, TM_OUT, w), U32),
                        pltpu.SemaphoreType.DMA((2,))],
        compiler_params=_cparams(("arbitrary",)),
        name="combine",
    )(pos3, pos3, ys, h1, wtok, g2, b2)


def _lambda_init(layer):
    return 0.8 - 0.6 * math.exp(-0.3 * layer)


def kernel(x, w_in, w_out, ln1_g, ln1_b, ln2_g, ln2_b, rel_bias, lam_q1, lam_k1, lam_q2, lam_k2,
           subln_g, sg_ln_g, sg_ln_b, sg_w, sg_b, w_router_group, b_router_group,
           w_router_expert, b_router_expert, w_exp_gate, w_exp_up, w_exp_down):
    b, s, d = x.shape
    t = b * s
    assert w_in.shape[0] == DEPTH == 1
    l = 0
    lam_init = _lambda_init(l)
    x2 = x.reshape(t, d)

    proj = _inproj(x2, w_in[l])

    lam = (jnp.exp(jnp.sum(lam_q1[l].astype(F32) * lam_k1[l].astype(F32)))
           - jnp.exp(jnp.sum(lam_q2[l].astype(F32) * lam_k2[l].astype(F32))) + lam_init)
    a = _attention(proj.reshape(b, s, -1), _bias_tiles(rel_bias, s), lam.reshape(1),
                   subln_g[l].reshape(1, -1), lam_init)

    wr = jnp.concatenate([w_router_group[l].T,
                          jnp.zeros((NR_EXPERT0 - N_GROUPS, d), F32),
                          jnp.transpose(w_router_expert[l], (0, 2, 1)).reshape(N_EXPERTS, d)],
                         axis=0)
    br = jnp.concatenate([b_router_group[l], jnp.zeros((NR_EXPERT0 - N_GROUPS,), F32),
                          b_router_expert[l].reshape(-1)]).reshape(NR_PAD, 1)
    h1, hp, ri, rw, cnt = _mix(
        a.reshape(t, -1), proj, x2, w_out[l].astype(BF16), sg_w[l].astype(BF16),
        sg_b[l].reshape(SG_GROUPS, SG_CHUNK, 1), sg_ln_g[l], sg_ln_b[l],
        ln1_g[l].reshape(1, d), ln1_b[l].reshape(1, d), wr, br)

    counts = cnt[:, 0]
    tiles_e = (counts + TMX - 1) // TMX
    tile_end = jnp.cumsum(tiles_e)
    tile_start = tile_end - tiles_e
    n_used = tile_end[-1:]
    max_tiles = (2 * t) // TMX + N_EXPERTS
    tile_ids = jnp.arange(max_tiles, dtype=I32)
    tile_expert = jnp.minimum(jnp.searchsorted(tile_end, tile_ids, side="right"),
                              N_EXPERTS - 1).astype(I32)
    tile_expert = jnp.where(tile_ids < n_used[0], tile_expert,
                            tile_expert[jnp.maximum(n_used[0] - 1, 0)])
    last_tile = jnp.maximum(tile_end - 1, 0).astype(I32)
    row_start = (tile_start * TMX).astype(I32)
    eids = jnp.arange(N_EXPERTS, dtype=I32)[:, None, None]
    pos = jnp.sum(jnp.where(ri[0:2][None] == eids, row_start[:, None, None], 0), axis=0) + ri[2:4]
    pos3 = pos.reshape(2, t // TM_OUT, TM_OUT).transpose(1, 0, 2)

    xs = _dispatch(pos3, last_tile, hp, max_tiles * TMX)
    ys = _experts(tile_expert, n_used.astype(I32), xs, w_exp_gate[l], w_exp_up[l], w_exp_down[l])
    out = _combine(pos3, ys, h1, rw.T, ln2_g[l].reshape(1, d), ln2_b[l].reshape(1, d))
    return out.reshape(b, s, d)
```

```python
import functools
import math

import jax
import jax.numpy as jnp
from jax import lax
from jax.experimental import pallas as pl
from jax.experimental.pallas import tpu as pltpu

F32 = jnp.float32
BF16 = jnp.bfloat16
U32 = jnp.uint32
I32 = jnp.int32

ATT_HEADS = 8
ATT_QK_DIM = 64
ATT_V_DIM = 128
ATT_WIDTH = ATT_HEADS * ATT_V_DIM
SG_GROUPS = 8
SG_DIM = 128
SG_WIDTH = SG_GROUPS * SG_DIM
SG_CHUNK = 128
REL_BUCKETS = 32
REL_MAX_DIST = 128
N_GROUPS = 4
EXPERTS_PER_GROUP = 8
N_EXPERTS = N_GROUPS * EXPERTS_PER_GROUP
D_EXPERT = 256
DEPTH = 1
DN_ALPHA = (2.0 * DEPTH) ** 0.25
LN_EPS = 1e-5
LOG2E = math.log2(math.e)

LANES = 128
SUBLANES = 8
VMEM_LIMIT_BYTES = 56 * 1024 * 1024

TM_IN = 1024
TN_IN = 1024
QB = 256
KC = 256
VT_ROWS = ATT_V_DIM + 16
TM_MIX = 512
TMX = 256
TM_OUT = 256
ROUTE_ROWS = 8
NR_EXPERT0 = 8
NR_PAD = NR_EXPERT0 + 32


def _cparams(sem, flags=None):
    return pltpu.CompilerParams(dimension_semantics=sem, vmem_limit_bytes=VMEM_LIMIT_BYTES,
                                flags=flags)


def _inproj_kernel(x_ref, w_ref, o_ref, xb_ref, *, n_q, n_lin, q_scale):
    j = pl.program_id(1)

    @pl.when(j == 0)
    def _():
        xb_ref[...] = x_ref[...].astype(BF16)

    acc = jnp.dot(xb_ref[...], w_ref[...].astype(BF16), preferred_element_type=F32)

    @pl.when(j < n_q)
    def _():
        o_ref[...] = (acc * q_scale).astype(BF16)

    @pl.when(jnp.logical_and(j >= n_q, j < n_lin))
    def _():
        o_ref[...] = acc.astype(BF16)

    @pl.when(j >= n_lin)
    def _():
        o_ref[...] = jax.nn.gelu(acc).astype(BF16)


def _inproj(x2, w_in):
    t, d = x2.shape
    n = w_in.shape[1]
    n_q = (ATT_HEADS * 2 * ATT_QK_DIM) // TN_IN
    n_lin = (2 * ATT_HEADS * 2 * ATT_QK_DIM + ATT_WIDTH) // TN_IN
    kern = functools.partial(_inproj_kernel, n_q=n_q, n_lin=n_lin,
                             q_scale=(ATT_QK_DIM ** -0.5) * LOG2E)
    return pl.pallas_call(
        kern,
        grid=(t // TM_IN, n // TN_IN),
        in_specs=[pl.BlockSpec((TM_IN, d), lambda i, j: (i, 0)),
                  pl.BlockSpec((d, TN_IN), lambda i, j: (0, j))],
        out_specs=pl.BlockSpec((TM_IN, TN_IN), lambda i, j: (i, j)),
        out_shape=jax.ShapeDtypeStruct((t, n), BF16),
        scratch_shapes=[pltpu.VMEM((TM_IN, d), BF16)],
        compiler_params=_cparams(("arbitrary", "arbitrary")),
        name="inproj",
    )(x2, w_in)


def _attn_kernel(lam_ref, q_ref, k_ref, v_ref, bias_ref, g_ref, o_ref, vt_ref,
                 *, n_kc, out_scale):
    qi = pl.program_id(2)

    @pl.when(qi == 0)
    def _():
        row = lax.broadcasted_iota(I32, (VT_ROWS - ATT_V_DIM, KC), 0)
        ones_rows = jnp.where(row == 0, 1.0, 0.0).astype(BF16)
        for c in range(n_kc):
            blk = v_ref[0, c * KC:(c + 1) * KC, :].astype(F32)
            vt_ref[c, :ATT_V_DIM, :] = blk.T.astype(BF16)
            vt_ref[c, ATT_V_DIM:, :] = ones_rows

    q = q_ref[0]
    lane = lax.broadcasted_iota(I32, q.shape, 1)
    zero = jnp.zeros_like(q)
    qq = jnp.concatenate([jnp.where(lane < ATT_QK_DIM, q, zero),
                          jnp.where(lane >= ATT_QK_DIM, q, zero)], axis=0)

    m = jnp.full((1, 2 * QB), -jnp.inf, F32)
    acc = jnp.zeros((VT_ROWS, 2 * QB), F32)
    band = (0, 1, n_kc - 1)
    c_left = bias_ref[0, 0, 0:1, 0:1]
    c_right = bias_ref[0, 4, 0:1, 0:1]

    def chunk_index(j):
        kc = qi + j
        return jnp.where(kc >= n_kc, kc - n_kc, kc)

    def scores(j):
        kc = chunk_index(j)
        kblk = k_ref[0, pl.ds(pl.multiple_of(kc * KC, KC), KC), :]
        s = lax.dot_general(kblk, qq, (((1,), (1,)), ((), ())),
                            preferred_element_type=F32)
        if j in band:
            b = bias_ref[0, jnp.clip(kc - qi, -2, 2) + 2]
            s = s + jnp.concatenate([b, b], axis=1)
            return s, jnp.max(s, axis=0, keepdims=True), None
        c = jnp.where(qi + j >= n_kc, c_left, c_right)
        return s, jnp.max(s, axis=0, keepdims=True) + c, c

    def weights(sc, m):
        s, cmax, c = sc
        m_new = jnp.maximum(m, cmax)
        shift = m_new if c is None else m_new - c
        return jnp.exp2(s - shift).astype(BF16), jnp.exp2(m - m_new), m_new

    ahead = [scores(0), scores(1)]
    p, alpha, m = weights(ahead[0], m)
    for j in range(n_kc):
        if j + 2 < n_kc:
            ahead.append(scores(j + 2))
        if j + 1 < n_kc:
            p_next, alpha_next, m = weights(ahead[j + 1], m)
        acc = acc * alpha + jnp.dot(vt_ref[chunk_index(j)], p, preferred_element_type=F32)
        if j + 1 < n_kc:
            p, alpha = p_next, alpha_next
    lam = lam_ref[0]
    o0 = acc[:ATT_V_DIM, :QB] / acc[ATT_V_DIM:ATT_V_DIM + 1, :QB]
    o1 = acc[:ATT_V_DIM, QB:] / acc[ATT_V_DIM:ATT_V_DIM + 1, QB:]
    of = o0 - lam * o1
    of = of * lax.rsqrt(jnp.mean(of * of, axis=0, keepdims=True) + LN_EPS)
    o_ref[0] = (of * (g_ref[...] * out_scale)).T.astype(o_ref.dtype)


def _attention(proj3, bias5, lam, subln_g, lam_init):
    b, s, _ = proj3.shape
    n_kc = s // KC
    h_off_k = ATT_HEADS
    h_off_v = 2 * ATT_HEADS
    kern = functools.partial(_attn_kernel, n_kc=n_kc, out_scale=1.0 - lam_init)
    return pl.pallas_call(
        kern,
        grid=(b, ATT_HEADS, s // QB),
        in_specs=[
            pl.BlockSpec(memory_space=pltpu.SMEM),
            pl.BlockSpec((1, QB, LANES), lambda bi, h, qi: (bi, qi, h)),
            pl.BlockSpec((1, s, LANES), lambda bi, h, qi: (bi, 0, h_off_k + h)),
            pl.BlockSpec((1, s, LANES), lambda bi, h, qi: (bi, 0, h_off_v + h)),
            pl.BlockSpec((1, 5, KC, QB), lambda bi, h, qi: (h, 0, 0, 0)),
            pl.BlockSpec((ATT_V_DIM, 1), lambda bi, h, qi: (0, 0)),
        ],
        out_specs=pl.BlockSpec((1, QB, ATT_V_DIM), lambda bi, h, qi: (bi, qi, h)),
        out_shape=jax.ShapeDtypeStruct((b, s, ATT_WIDTH), BF16),
        scratch_shapes=[pltpu.VMEM((n_kc, VT_ROWS, KC), BF16)],
        compiler_params=_cparams(("arbitrary", "arbitrary", "arbitrary"),
                                 ),
        name="attn",
    )(lam, proj3, proj3, proj3, bias5, subln_g)


def _rel_bucket(rel):
    half = REL_BUCKETS // 2
    max_exact = half // 2
    ret = jnp.where(rel > 0, half, 0)
    n = jnp.abs(rel)
    nf = jnp.maximum(n, 1).astype(F32)
    large = max_exact + (jnp.log(nf / max_exact) / math.log(REL_MAX_DIST / max_exact)
                         * (half - max_exact)).astype(I32)
    large = jnp.minimum(large, half - 1)
    return ret + jnp.where(n < max_exact, n, large)


def _bias_tiles(rel_bias, s):
    assert REL_MAX_DIST <= KC and QB == KC
    period = 2 * KC
    j = jnp.arange(period, dtype=I32)
    off = jnp.where(j < KC, j, j - period)
    d = jnp.arange(-2, 3, dtype=I32)[:, None]
    rel = jnp.clip(d * KC + off[None, :], -(s - 1), s - 1)
    g = jnp.transpose(rel_bias[_rel_bucket(rel)], (2, 0, 1)).astype(F32) * LOG2E
    hseq = jnp.roll(g[..., ::-1], 1, axis=-1)
    flat = jnp.tile(hseq, (1, 1, KC))[..., :KC * (period - 1)]
    return flat.reshape(ATT_HEADS, 5, KC, period - 1)[..., :QB]


def _layernorm(v, g, b):
    mu = jnp.mean(v, axis=-1, keepdims=True)
    var = jnp.mean(jnp.square(v - mu), axis=-1, keepdims=True)
    return (v - mu) * lax.rsqrt(var + LN_EPS) * g + b


def _pack_bf16_pair(v):
    n = v.shape[1] // 2
    bits = lax.bitcast_convert_type(v.astype(BF16).astype(F32), U32)
    return (bits[:, :n] >> 16) | (bits[:, n:] & jnp.uint32(0xFFFF0000))


def _unpack_bf16_pair(w):
    lo = lax.bitcast_convert_type(w << 16, F32)
    hi = lax.bitcast_convert_type(w & jnp.uint32(0xFFFF0000), F32)
    return lo, hi


def _first_argmax(vals, nrows):
    mx = jnp.max(vals, axis=0, keepdims=True)
    row = lax.broadcasted_iota(I32, vals.shape, 0)
    idx = jnp.min(jnp.where(vals == mx, row, nrows), axis=0, keepdims=True)
    return mx, idx


def _mix_kernel(a_ref, u_ref, vg_ref, x_ref, wout_ref, sgw_ref, sgb_ref, sgg_ref, sgbeta_ref,
                g1_ref, b1_ref, wr_ref, br_ref,
                h1_ref, hp_ref, ri_ref, rw_ref, cnt_ref, run_ref):
    i = pl.program_id(0)

    @pl.when(i == 0)
    def _():
        run_ref[...] = jnp.zeros_like(run_ref)

    n_chunks = TM_MIX // SG_CHUNK
    rows = []
    for c in range(n_chunks):
        cols = []
        for g in range(SG_GROUPS):
            rs = slice(c * SG_CHUNK, (c + 1) * SG_CHUNK)
            cs = slice(g * SG_DIM, (g + 1) * SG_DIM)
            vt = vg_ref[rs, cs].astype(F32)
            vn = _layernorm(vt, sgg_ref[g:g + 1, :], sgbeta_ref[g:g + 1, :])
            mixed = jnp.dot(sgw_ref[g], vn.astype(BF16), preferred_element_type=F32) + sgb_ref[g]
            cols.append((u_ref[rs, cs].astype(F32) * mixed).astype(BF16))
        rows.append(jnp.concatenate(cols, axis=1))
    sgate = jnp.concatenate(rows, axis=0)

    mix = jnp.dot(a_ref[...], wout_ref[:ATT_WIDTH, :], preferred_element_type=F32)
    mix = mix + jnp.dot(sgate, wout_ref[ATT_WIDTH:, :], preferred_element_type=F32)
    h1 = _layernorm(DN_ALPHA * x_ref[...] + mix, g1_ref[...], b1_ref[...])
    h1_ref[...] = h1
    hp_ref[...] = _pack_bf16_pair(h1)

    lt = lax.dot_general(wr_ref[...], h1, (((1,), (1,)), ((), ())),
                         preferred_element_type=F32, precision=lax.Precision.HIGHEST)
    lt = lt + br_ref[...]
    g_logits = lt[0:N_GROUPS]
    gmax, g_idx = _first_argmax(g_logits, N_GROUPS)
    g_gate = 1.0 / jnp.sum(jnp.exp(g_logits - gmax), axis=0, keepdims=True)
    e_logits = jnp.zeros((EXPERTS_PER_GROUP, TM_MIX), F32)
    for g in range(N_GROUPS):
        lo = NR_EXPERT0 + g * EXPERTS_PER_GROUP
        e_logits = jnp.where(g_idx == g, lt[lo:lo + EXPERTS_PER_GROUP], e_logits)
    v1, i1 = _first_argmax(e_logits, EXPERTS_PER_GROUP)
    erow = lax.broadcasted_iota(I32, e_logits.shape, 0)
    v2, i2 = _first_argmax(jnp.where(erow == i1, -jnp.inf, e_logits), EXPERTS_PER_GROUP)
    t = jnp.exp(v2 - v1)
    w1 = g_gate / (1.0 + t)
    w2 = g_gate * t / (1.0 + t)
    e1 = g_idx * EXPERTS_PER_GROUP + i1
    e2 = g_idx * EXPERTS_PER_GROUP + i2

    xrow = lax.broadcasted_iota(I32, (N_EXPERTS, TM_MIX), 0)
    oh1 = xrow == e1
    oh2 = xrow == e2
    oh = jnp.where(jnp.logical_or(oh1, oh2), 1.0, 0.0)
    tr = lax.broadcasted_iota(I32, (TM_MIX, TM_MIX), 0)
    tc = lax.broadcasted_iota(I32, (TM_MIX, TM_MIX), 1)
    before = jnp.where(tr < tc, 1.0, 0.0).astype(BF16)
    cum = jnp.dot(oh.astype(BF16), before, preferred_element_type=F32)
    base = cum + run_ref[:, 0:1]
    rank1 = jnp.sum(jnp.where(oh1, base, 0.0), axis=0, keepdims=True).astype(I32)
    rank2 = jnp.sum(jnp.where(oh2, base, 0.0), axis=0, keepdims=True).astype(I32)
    run_ref[...] = run_ref[...] + jnp.sum(oh, axis=1, keepdims=True)
    cnt_ref[...] = run_ref[...].astype(I32)

    zi = jnp.zeros((ROUTE_ROWS - 4, TM_MIX), I32)
    ri_ref[...] = jnp.concatenate([e1, e2, rank1, rank2, zi], axis=0)
    zw = jnp.zeros((ROUTE_ROWS - 2, TM_MIX), F32)
    rw_ref[...] = jnp.concatenate([w1, w2, zw], axis=0)


def _mix(a2, proj2, x2, wout_b, sgw_b, sgb, sgg, sgbeta, g1, b1, wr, br):
    t, d = x2.shape
    u_blk = (proj2.shape[1] - 2 * SG_WIDTH) // SG_WIDTH
    const = lambda *shape: pl.BlockSpec(shape, lambda i: (0,) * len(shape))
    return pl.pallas_call(
        _mix_kernel,
        grid=(t // TM_MIX,),
        in_specs=[
            pl.BlockSpec((TM_MIX, ATT_WIDTH), lambda i: (i, 0)),
            pl.BlockSpec((TM_MIX, SG_WIDTH), lambda i: (i, u_blk)),
            pl.BlockSpec((TM_MIX, SG_WIDTH), lambda i: (i, u_blk + 1)),
            pl.BlockSpec((TM_MIX, d), lambda i: (i, 0)),
            pl.BlockSpec((d, d), lambda i: (0, 0), pipeline_mode=pl.Buffered(1)),
            const(SG_GROUPS, SG_CHUNK, SG_CHUNK),
            const(SG_GROUPS, SG_CHUNK, 1),
            const(SG_GROUPS, SG_DIM),
            const(SG_GROUPS, SG_DIM),
            const(1, d),
            const(1, d),
            const(NR_PAD, d),
            const(NR_PAD, 1),
        ],
        out_specs=[
            pl.BlockSpec((TM_MIX, d), lambda i: (i, 0)),
            pl.BlockSpec((TM_MIX, d // 2), lambda i: (i, 0)),
            pl.BlockSpec((ROUTE_ROWS, TM_MIX), lambda i: (0, i)),
            pl.BlockSpec((ROUTE_ROWS, TM_MIX), lambda i: (0, i)),
            const(N_EXPERTS, LANES),
        ],
        out_shape=[
            jax.ShapeDtypeStruct((t, d), F32),
            jax.ShapeDtypeStruct((t, d // 2), U32),
            jax.ShapeDtypeStruct((ROUTE_ROWS, t), I32),
            jax.ShapeDtypeStruct((ROUTE_ROWS, t), F32),
            jax.ShapeDtypeStruct((N_EXPERTS, LANES), I32),
        ],
        scratch_shapes=[pltpu.VMEM((N_EXPERTS, LANES), F32)],
        compiler_params=_cparams(("arbitrary",)),
        name="mix",
    )(a2, proj2, proj2, x2, wout_b, sgw_b, sgb, sgg, sgbeta, g1, b1, wr, br)


DISPATCH_CHUNK = 256


def _dispatch_kernel(pos_ref, last_ref, hp_ref, xs_hbm, zero_ref, zsem, sem):
    i = pl.program_id(0)

    def row_copy(j, dst):
        return pltpu.make_async_copy(hp_ref.at[pl.ds(j, 1)], xs_hbm.at[pl.ds(dst, 1)], sem)

    def zero_copy(e):
        start = pl.multiple_of(last_ref[e] * TMX, TMX)
        return pltpu.make_async_copy(zero_ref, xs_hbm.at[pl.ds(start, TMX)], zsem)

    @pl.when(i == 0)
    def _():
        zero_ref[...] = jnp.zeros_like(zero_ref)

        def zstart(e, c):
            zero_copy(e).start()
            return c

        def zwait(e, c):
            zero_copy(e).wait()
            return c

        lax.fori_loop(0, N_EXPERTS, zstart, 0)
        lax.fori_loop(0, N_EXPERTS, zwait, 0)

    def start(j, c):
        row_copy(j, pos_ref[0, j]).start()
        row_copy(j, pos_ref[1, j]).start()
        return c

    def wait(j, c):
        row_copy(j, pos_ref[0, j]).wait()
        row_copy(j, pos_ref[1, j]).wait()
        return c

    lax.fori_loop(0, DISPATCH_CHUNK, start, 0, unroll=8)
    lax.fori_loop(0, DISPATCH_CHUNK, wait, 0, unroll=8)


def _dispatch(pos3, last_tile, hp, n_rows):
    t, w = hp.shape
    return pl.pallas_call(
        _dispatch_kernel,
        grid=(t // DISPATCH_CHUNK,),
        in_specs=[
            pl.BlockSpec((None, 2, DISPATCH_CHUNK), lambda i: (i, 0, 0), memory_space=pltpu.SMEM),
            pl.BlockSpec(memory_space=pltpu.SMEM),
            pl.BlockSpec((DISPATCH_CHUNK, w), lambda i: (i, 0)),
        ],
        out_specs=pl.BlockSpec(memory_space=pl.ANY),
        out_shape=jax.ShapeDtypeStruct((n_rows, w), U32),
        scratch_shapes=[pltpu.VMEM((TMX, w), U32),
                        pltpu.SemaphoreType.DMA(()),
                        pltpu.SemaphoreType.DMA(())],
        compiler_params=_cparams(("arbitrary",)),
        name="dispatch",
    )(pos3, last_tile, hp)


def _experts_kernel(te_ref, nu_ref, x_ref, wg_ref, wu_ref, wd_ref, y_ref, wgu_s, wd_s):
    i = pl.program_id(0)
    d = wg_ref.shape[1]

    @pl.when(i < nu_ref[0])
    def _():
        first = jnp.logical_or(i == 0, te_ref[i] != te_ref[jnp.maximum(i - 1, 0)])

        @pl.when(first)
        def _():
            wgu_s[:, :D_EXPERT] = wg_ref[0].astype(BF16)
            wgu_s[:, D_EXPERT:] = wu_ref[0].astype(BF16)
            wd_s[...] = wd_ref[0].astype(BF16)

        lo, hi = _unpack_bf16_pair(x_ref[...])
        gu = jnp.dot(lo.astype(BF16), wgu_s[:d // 2, :], preferred_element_type=F32)
        gu = gu + jnp.dot(hi.astype(BF16), wgu_s[d // 2:, :], preferred_element_type=F32)
        gate = gu[:, :D_EXPERT]
        hmid = (gate * jax.nn.sigmoid(gate) * gu[:, D_EXPERT:]).astype(BF16)
        y = jnp.dot(hmid, wd_s[...], preferred_element_type=F32)
        y_ref[...] = _pack_bf16_pair(y)


def _experts(tile_expert, n_used, xs, w_gate, w_up, w_down):
    n_rows, w = xs.shape
    d = w_gate.shape[1]
    n_tiles = n_rows // TMX

    def row_map(i, te, nu):
        return (jnp.minimum(i, jnp.maximum(nu[0] - 1, 0)), 0)

    def w_map(i, te, nu):
        return (te[i], 0, 0)

    grid_spec = pltpu.PrefetchScalarGridSpec(
        num_scalar_prefetch=2,
        grid=(n_tiles,),
        in_specs=[
            pl.BlockSpec((TMX, w), row_map),
            pl.BlockSpec((1, d, D_EXPERT), w_map),
            pl.BlockSpec((1, d, D_EXPERT), w_map),
            pl.BlockSpec((1, D_EXPERT, d), w_map),
        ],
        out_specs=pl.BlockSpec((TMX, w), row_map),
        scratch_shapes=[pltpu.VMEM((d, 2 * D_EXPERT), BF16),
                        pltpu.VMEM((D_EXPERT, d), BF16)],
    )
    return pl.pallas_call(
        _experts_kernel,
        grid_spec=grid_spec,
        out_shape=jax.ShapeDtypeStruct((n_rows, w), U32),
        compiler_params=_cparams(("arbitrary",)),
        name="experts",
    )(tile_expert, n_used, xs, w_gate, w_up, w_down)


def _combine_kernel(pos_ref, posn_ref, y_hbm, h1_ref, wt_ref, g2_ref, b2_ref, o_ref, ybuf, sem):
    i = pl.program_id(0)
    n = pl.num_programs(0)
    slot = i % 2

    def row_copy(src, sl, which, j):
        return pltpu.make_async_copy(y_hbm.at[pl.ds(src, 1)],
                                     ybuf.at[sl, which, pl.ds(j, 1)], sem.at[sl])

    def issue(p_ref, sl):
        def body(j, c):
            row_copy(p_ref[0, j], sl, 0, j).start()
            row_copy(p_ref[1, j], sl, 1, j).start()
            return c
        lax.fori_loop(0, TM_OUT, body, 0, unroll=8)

    @pl.when(i == 0)
    def _():
        issue(pos_ref, 0)

    @pl.when(i + 1 < n)
    def _():
        issue(posn_ref, 1 - slot)

    def wait(j, c):
        row_copy(pos_ref[0, j], slot, 0, j).wait()
        row_copy(pos_ref[1, j], slot, 1, j).wait()
        return c

    lax.fori_loop(0, TM_OUT, wait, 0, unroll=8)

    lo1, hi1 = _unpack_bf16_pair(ybuf[slot, 0])
    lo2, hi2 = _unpack_bf16_pair(ybuf[slot, 1])
    w1 = wt_ref[:, 0:1]
    w2 = wt_ref[:, 1:2]
    ffn = jnp.concatenate([w1 * lo1 + w2 * lo2, w1 * hi1 + w2 * hi2], axis=1)
    o_ref[...] = _layernorm(DN_ALPHA * h1_ref[...] + ffn, g2_ref[...], b2_ref[...])


def _combine(pos3, ys, h1, wtok, g2, b2):
    t, d = h1.shape
    w = ys.shape[1]
    n = t // TM_OUT
    return pl.pallas_call(
        _combine_kernel,
        grid=(n,),
        in_specs=[
            pl.BlockSpec((None, 2, TM_OUT), lambda i: (i, 0, 0), memory_space=pltpu.SMEM),
            pl.BlockSpec((None, 2, TM_OUT), lambda i: (jnp.minimum(i + 1, n - 1), 0, 0),
                         memory_space=pltpu.SMEM),
            pl.BlockSpec(memory_space=pl.ANY),
            pl.BlockSpec((TM_OUT, d), lambda i: (i, 0)),
            pl.BlockSpec((TM_OUT, ROUTE_ROWS), lambda i: (i, 0)),
            pl.BlockSpec((1, d), lambda i: (0, 0)),
            pl.BlockSpec((1, d), lambda i: (0, 0)),
        ],
        out_specs=pl.BlockSpec((TM_OUT, d), lambda i: (i, 0)),
        out_shape=jax.ShapeDtypeStruct((t, d), F32),
        scratch_shapes=[pltpu.VMEM((2, 2, TM_OUT, w), U32),
                        pltpu.SemaphoreType.DMA((2,))],
        compiler_params=_cparams(("arbitrary",)),
        name="combine",
    )(pos3, pos3, ys, h1, wtok, g2, b2)


def _lambda_init(layer):
    return 0.8 - 0.6 * math.exp(-0.3 * layer)


def kernel(x, w_in, w_out, ln1_g, ln1_b, ln2_g, ln2_b, rel_bias, lam_q1, lam_k1, lam_q2, lam_k2,
           subln_g, sg_ln_g, sg_ln_b, sg_w, sg_b, w_router_group, b_router_group,
           w_router_expert, b_router_expert, w_exp_gate, w_exp_up, w_exp_down):
    b, s, d = x.shape
    t = b * s
    assert w_in.shape[0] == DEPTH == 1
    l = 0
    lam_init = _lambda_init(l)
    x2 = x.reshape(t, d)

    proj = _inproj(x2, w_in[l])

    lam = (jnp.exp(jnp.sum(lam_q1[l].astype(F32) * lam_k1[l].astype(F32)))
           - jnp.exp(jnp.sum(lam_q2[l].astype(F32) * lam_k2[l].astype(F32))) + lam_init)
    a = _attention(proj.reshape(b, s, -1), _bias_tiles(rel_bias, s), lam.reshape(1),
                   subln_g[l].reshape(-1, 1), lam_init)

    wr = jnp.concatenate([w_router_group[l].T,
                          jnp.zeros((NR_EXPERT0 - N_GROUPS, d), F32),
                          jnp.transpose(w_router_expert[l], (0, 2, 1)).reshape(N_EXPERTS, d)],
                         axis=0)
    br = jnp.concatenate([b_router_group[l], jnp.zeros((NR_EXPERT0 - N_GROUPS,), F32),
                          b_router_expert[l].reshape(-1)]).reshape(NR_PAD, 1)
    h1, hp, ri, rw, cnt = _mix(
        a.reshape(t, -1), proj, x2, w_out[l].astype(BF16), sg_w[l].astype(BF16),
        sg_b[l].reshape(SG_GROUPS, SG_CHUNK, 1), sg_ln_g[l], sg_ln_b[l],
        ln1_g[l].reshape(1, d), ln1_b[l].reshape(1, d), wr, br)

    counts = cnt[:, 0]
    tiles_e = (counts + TMX - 1) // TMX
    tile_end = jnp.cumsum(tiles_e)
    tile_start = tile_end - tiles_e
    n_used = tile_end[-1:]
    max_tiles = (2 * t) // TMX + N_EXPERTS
    tile_ids = jnp.arange(max_tiles, dtype=I32)
    tile_expert = jnp.minimum(jnp.searchsorted(tile_end, tile_ids, side="right"),
                              N_EXPERTS - 1).astype(I32)
    tile_expert = jnp.where(tile_ids < n_used[0], tile_expert,
                            tile_expert[jnp.maximum(n_used[0] - 1, 0)])
    last_tile = jnp.maximum(tile_end - 1, 0).astype(I32)
    row_start = (tile_start * TMX).astype(I32)
    eids = jnp.arange(N_EXPERTS, dtype=I32)[:, None, None]
    pos = jnp.sum(jnp.where(ri[0:2][None] == eids, row_start[:, None, None], 0), axis=0) + ri[2:4]
    pos3 = pos.reshape(2, t // TM_OUT, TM_OUT).transpose(1, 0, 2)

    xs = _dispatch(pos3, last_tile, hp, max_tiles * TMX)
    ys = _experts(tile_expert, n_used.astype(I32), xs, w_exp_gate[l], w_exp_up[l], w_exp_down[l])
    out = _combine(pos3, ys, h1, rw.T, ln2_g[l].reshape(1, d), ln2_b[l].reshape(1, d))
    return out.reshape(b, s, d)
```

```python
import functools
import math

import jax
import jax.numpy as jnp
from jax import lax
from jax.experimental import pallas as pl
from jax.experimental.pallas import tpu as pltpu

F32 = jnp.float32
BF16 = jnp.bfloat16
U32 = jnp.uint32
I32 = jnp.int32

ATT_HEADS = 8
ATT_QK_DIM = 64
ATT_V_DIM = 128
ATT_WIDTH = ATT_HEADS * ATT_V_DIM
SG_GROUPS = 8
SG_DIM = 128
SG_WIDTH = SG_GROUPS * SG_DIM
SG_CHUNK = 128
REL_BUCKETS = 32
REL_MAX_DIST = 128
N_GROUPS = 4
EXPERTS_PER_GROUP = 8
N_EXPERTS = N_GROUPS * EXPERTS_PER_GROUP
D_EXPERT = 256
DEPTH = 1
DN_ALPHA = (2.0 * DEPTH) ** 0.25
LN_EPS = 1e-5
LOG2E = math.log2(math.e)

LANES = 128
SUBLANES = 8
VMEM_LIMIT_BYTES = 56 * 1024 * 1024

TM_IN = 1024
TN_IN = 1024
QB = 256
KC = 256
N_BIAS_TILES = 5
TM_MIX = 512
TMX = 256
TM_OUT = 256
ROUTE_ROWS = 8
NR_EXPERT0 = 8
NR_PAD = NR_EXPERT0 + 32


def _cparams(sem, flags=None):
    return pltpu.CompilerParams(dimension_semantics=sem, vmem_limit_bytes=VMEM_LIMIT_BYTES,
                                flags=flags)


def _inproj_kernel(x_ref, w_ref, o_ref, xb_ref, *, n_q, n_lin, q_scale):
    j = pl.program_id(1)

    @pl.when(j == 0)
    def _():
        xb_ref[...] = x_ref[...].astype(BF16)

    acc = jnp.dot(xb_ref[...], w_ref[...].astype(BF16), preferred_element_type=F32)

    @pl.when(j < n_q)
    def _():
        o_ref[...] = (acc * q_scale).astype(BF16)

    @pl.when(jnp.logical_and(j >= n_q, j < n_lin))
    def _():
        o_ref[...] = acc.astype(BF16)

    @pl.when(j >= n_lin)
    def _():
        o_ref[...] = jax.nn.gelu(acc).astype(BF16)


def _inproj(x2, w_in):
    t, d = x2.shape
    n = w_in.shape[1]
    n_q = (ATT_HEADS * 2 * ATT_QK_DIM) // TN_IN
    n_lin = (2 * ATT_HEADS * 2 * ATT_QK_DIM + ATT_WIDTH) // TN_IN
    kern = functools.partial(_inproj_kernel, n_q=n_q, n_lin=n_lin,
                             q_scale=(ATT_QK_DIM ** -0.5) * LOG2E)
    return pl.pallas_call(
        kern,
        grid=(t // TM_IN, n // TN_IN),
        in_specs=[pl.BlockSpec((TM_IN, d), lambda i, j: (i, 0)),
                  pl.BlockSpec((d, TN_IN), lambda i, j: (0, j))],
        out_specs=pl.BlockSpec((TM_IN, TN_IN), lambda i, j: (i, j)),
        out_shape=jax.ShapeDtypeStruct((t, n), BF16),
        scratch_shapes=[pltpu.VMEM((TM_IN, d), BF16)],
        compiler_params=_cparams(("arbitrary", "arbitrary")),
        name="inproj",
    )(x2, w_in)


def _attn_kernel(lam_ref, q_ref, k_ref, v_ref, bseq_ref, g_ref, o_ref, vaug_ref, bias_ref,
                 *, n_kc, out_scale):
    qi = pl.program_id(2)

    @pl.when(qi == 0)
    def _():
        ones = jnp.ones((KC, ATT_V_DIM), BF16)
        for c in range(n_kc):
            vaug_ref[c, :, :ATT_V_DIM] = v_ref[0, c * KC:(c + 1) * KC, :]
            vaug_ref[c, :, ATT_V_DIM:] = ones
        for t in range(N_BIAS_TILES):
            seq = jnp.broadcast_to(bseq_ref[0, t:t + 1, :], (QB, 2 * KC))
            bias_ref[t] = pltpu.roll(seq, 0, 1, stride=1, stride_axis=0)[:, :KC]

    q = q_ref[0]
    lane = lax.broadcasted_iota(I32, q.shape, 1)
    zero = jnp.zeros_like(q)
    qq = jnp.concatenate([jnp.where(lane < ATT_QK_DIM, q, zero),
                          jnp.where(lane >= ATT_QK_DIM, q, zero)], axis=0)

    m = jnp.full((2 * QB, 1), -jnp.inf, F32)
    acc = jnp.zeros((2 * QB, 2 * ATT_V_DIM), F32)
    for kc in range(n_kc):
        kblk = k_ref[0, kc * KC:(kc + 1) * KC, :]
        s = lax.dot_general(qq, kblk, (((1,), (1,)), ((), ())),
                            preferred_element_type=F32)
        b = bias_ref[jnp.clip(kc - qi, -2, 2) + 2]
        s = s + jnp.concatenate([b, b], axis=0)
        m_new = jnp.maximum(m, jnp.max(s, axis=1, keepdims=True))
        p = jnp.exp2(s - m_new).astype(BF16)
        acc = acc * jnp.exp2(m - m_new) + jnp.dot(p, vaug_ref[kc],
                                                  preferred_element_type=F32)
        m = m_new
    lam = lam_ref[0]
    o0 = acc[:QB, :ATT_V_DIM] / acc[:QB, ATT_V_DIM:ATT_V_DIM + 1]
    o1 = acc[QB:, :ATT_V_DIM] / acc[QB:, ATT_V_DIM:ATT_V_DIM + 1]
    of = o0 - lam * o1
    of = of * lax.rsqrt(jnp.mean(of * of, axis=-1, keepdims=True) + LN_EPS)
    o_ref[0] = (of * (g_ref[...] * out_scale)).astype(o_ref.dtype)


def _attention(proj3, bias_seq, lam, subln_g, lam_init):
    b, s, _ = proj3.shape
    n_kc = s // KC
    h_off_k = ATT_HEADS
    h_off_v = 2 * ATT_HEADS
    kern = functools.partial(_attn_kernel, n_kc=n_kc, out_scale=1.0 - lam_init)
    return pl.pallas_call(
        kern,
        grid=(b, ATT_HEADS, s // QB),
        in_specs=[
            pl.BlockSpec(memory_space=pltpu.SMEM),
            pl.BlockSpec((1, QB, LANES), lambda bi, h, qi: (bi, qi, h)),
            pl.BlockSpec((1, s, LANES), lambda bi, h, qi: (bi, 0, h_off_k + h)),
            pl.BlockSpec((1, s, LANES), lambda bi, h, qi: (bi, 0, h_off_v + h)),
            pl.BlockSpec((1, N_BIAS_TILES, 2 * KC), lambda bi, h, qi: (h, 0, 0)),
            pl.BlockSpec((1, ATT_V_DIM), lambda bi, h, qi: (0, 0)),
        ],
        out_specs=pl.BlockSpec((1, QB, ATT_V_DIM), lambda bi, h, qi: (bi, qi, h)),
        out_shape=jax.ShapeDtypeStruct((b, s, ATT_WIDTH), BF16),
        scratch_shapes=[pltpu.VMEM((n_kc, KC, 2 * ATT_V_DIM), BF16),
                        pltpu.VMEM((N_BIAS_TILES, QB, KC), F32)],
        compiler_params=_cparams(("arbitrary", "arbitrary", "arbitrary")),
        name="attn",
    )(lam, proj3, proj3, proj3, bias_seq, subln_g)


def _rel_bucket(rel):
    half = REL_BUCKETS // 2
    max_exact = half // 2
    ret = jnp.where(rel > 0, half, 0)
    n = jnp.abs(rel)
    nf = jnp.maximum(n, 1).astype(F32)
    large = max_exact + (jnp.log(nf / max_exact) / math.log(REL_MAX_DIST / max_exact)
                         * (half - max_exact)).astype(I32)
    large = jnp.minimum(large, half - 1)
    return ret + jnp.where(n < max_exact, n, large)


def _bias_sequences(rel_bias, s):
    assert REL_MAX_DIST <= KC and QB == KC
    period = 2 * KC
    j = jnp.arange(period, dtype=I32)
    off = jnp.where(j < KC, j, j - period)
    d = jnp.arange(-2, 3, dtype=I32)[:, None]
    rel = jnp.clip(d * KC + off[None, :], -(s - 1), s - 1)
    return jnp.transpose(rel_bias[_rel_bucket(rel)], (2, 0, 1)).astype(F32) * LOG2E


def _layernorm(v, g, b):
    mu = jnp.mean(v, axis=-1, keepdims=True)
    var = jnp.mean(jnp.square(v - mu), axis=-1, keepdims=True)
    return (v - mu) * lax.rsqrt(var + LN_EPS) * g + b


def _pack_bf16_pair(v):
    n = v.shape[1] // 2
    bits = lax.bitcast_convert_type(v.astype(BF16).astype(F32), U32)
    return (bits[:, :n] >> 16) | (bits[:, n:] & jnp.uint32(0xFFFF0000))


def _unpack_bf16_pair(w):
    lo = lax.bitcast_convert_type(w << 16, F32)
    hi = lax.bitcast_convert_type(w & jnp.uint32(0xFFFF0000), F32)
    return lo, hi


def _first_argmax(vals, nrows):
    mx = jnp.max(vals, axis=0, keepdims=True)
    row = lax.broadcasted_iota(I32, vals.shape, 0)
    idx = jnp.min(jnp.where(vals == mx, row, nrows), axis=0, keepdims=True)
    return mx, idx


def _mix_kernel(a_ref, u_ref, vg_ref, x_ref, wout_ref, sgw_ref, sgb_ref, sgg_ref, sgbeta_ref,
                g1_ref, b1_ref, wr_ref, br_ref,
                h1_ref, hp_ref, ri_ref, rw_ref, cnt_ref, run_ref):
    i = pl.program_id(0)

    @pl.when(i == 0)
    def _():
        run_ref[...] = jnp.zeros_like(run_ref)

    n_chunks = TM_MIX // SG_CHUNK
    rows = []
    for c in range(n_chunks):
        cols = []
        for g in range(SG_GROUPS):
            rs = slice(c * SG_CHUNK, (c + 1) * SG_CHUNK)
            cs = slice(g * SG_DIM, (g + 1) * SG_DIM)
            vt = vg_ref[rs, cs].astype(F32)
            vn = _layernorm(vt, sgg_ref[g:g + 1, :], sgbeta_ref[g:g + 1, :])
            mixed = jnp.dot(sgw_ref[g], vn.astype(BF16), preferred_element_type=F32) + sgb_ref[g]
            cols.append((u_ref[rs, cs].astype(F32) * mixed).astype(BF16))
        rows.append(jnp.concatenate(cols, axis=1))
    sgate = jnp.concatenate(rows, axis=0)

    mix = jnp.dot(a_ref[...], wout_ref[:ATT_WIDTH, :], preferred_element_type=F32)
    mix = mix + jnp.dot(sgate, wout_ref[ATT_WIDTH:, :], preferred_element_type=F32)
    h1 = _layernorm(DN_ALPHA * x_ref[...] + mix, g1_ref[...], b1_ref[...])
    h1_ref[...] = h1
    hp_ref[...] = _pack_bf16_pair(h1)

    lt = lax.dot_general(wr_ref[...], h1, (((1,), (1,)), ((), ())),
                         preferred_element_type=F32, precision=lax.Precision.HIGHEST)
    lt = lt + br_ref[...]
    g_logits = lt[0:N_GROUPS]
    gmax, g_idx = _first_argmax(g_logits, N_GROUPS)
    g_gate = 1.0 / jnp.sum(jnp.exp(g_logits - gmax), axis=0, keepdims=True)
    e_logits = jnp.zeros((EXPERTS_PER_GROUP, TM_MIX), F32)
    for g in range(N_GROUPS):
        lo = NR_EXPERT0 + g * EXPERTS_PER_GROUP
        e_logits = jnp.where(g_idx == g, lt[lo:lo + EXPERTS_PER_GROUP], e_logits)
    v1, i1 = _first_argmax(e_logits, EXPERTS_PER_GROUP)
    erow = lax.broadcasted_iota(I32, e_logits.shape, 0)
    v2, i2 = _first_argmax(jnp.where(erow == i1, -jnp.inf, e_logits), EXPERTS_PER_GROUP)
    t = jnp.exp(v2 - v1)
    w1 = g_gate / (1.0 + t)
    w2 = g_gate * t / (1.0 + t)
    e1 = g_idx * EXPERTS_PER_GROUP + i1
    e2 = g_idx * EXPERTS_PER_GROUP + i2

    xrow = lax.broadcasted_iota(I32, (N_EXPERTS, TM_MIX), 0)
    oh1 = xrow == e1
    oh2 = xrow == e2
    oh = jnp.where(jnp.logical_or(oh1, oh2), 1.0, 0.0)
    tr = lax.broadcasted_iota(I32, (TM_MIX, TM_MIX), 0)
    tc = lax.broadcasted_iota(I32, (TM_MIX, TM_MIX), 1)
    before = jnp.where(tr < tc, 1.0, 0.0).astype(BF16)
    cum = jnp.dot(oh.astype(BF16), before, preferred_element_type=F32)
    base = cum + run_ref[:, 0:1]
    rank1 = jnp.sum(jnp.where(oh1, base, 0.0), axis=0, keepdims=True).astype(I32)
    rank2 = jnp.sum(jnp.where(oh2, base, 0.0), axis=0, keepdims=True).astype(I32)
    run_ref[...] = run_ref[...] + jnp.sum(oh, axis=1, keepdims=True)
    cnt_ref[...] = run_ref[...].astype(I32)

    zi = jnp.zeros((ROUTE_ROWS - 4, TM_MIX), I32)
    ri_ref[...] = jnp.concatenate([e1, e2, rank1, rank2, zi], axis=0)
    zw = jnp.zeros((ROUTE_ROWS - 2, TM_MIX), F32)
    rw_ref[...] = jnp.concatenate([w1, w2, zw], axis=0)


def _mix(a2, proj2, x2, wout_b, sgw_b, sgb, sgg, sgbeta, g1, b1, wr, br):
    t, d = x2.shape
    u_blk = (proj2.shape[1] - 2 * SG_WIDTH) // SG_WIDTH
    const = lambda *shape: pl.BlockSpec(shape, lambda i: (0,) * len(shape))
    return pl.pallas_call(
        _mix_kernel,
        grid=(t // TM_MIX,),
        in_specs=[
            pl.BlockSpec((TM_MIX, ATT_WIDTH), lambda i: (i, 0)),
            pl.BlockSpec((TM_MIX, SG_WIDTH), lambda i: (i, u_blk)),
            pl.BlockSpec((TM_MIX, SG_WIDTH), lambda i: (i, u_blk + 1)),
            pl.BlockSpec((TM_MIX, d), lambda i: (i, 0)),
            pl.BlockSpec((d, d), lambda i: (0, 0), pipeline_mode=pl.Buffered(1)),
            const(SG_GROUPS, SG_CHUNK, SG_CHUNK),
            const(SG_GROUPS, SG_CHUNK, 1),
            const(SG_GROUPS, SG_DIM),
            const(SG_GROUPS, SG_DIM),
            const(1, d),
            const(1, d),
            const(NR_PAD, d),
            const(NR_PAD, 1),
        ],
        out_specs=[
            pl.BlockSpec((TM_MIX, d), lambda i: (i, 0)),
            pl.BlockSpec((TM_MIX, d // 2), lambda i: (i, 0)),
            pl.BlockSpec((ROUTE_ROWS, TM_MIX), lambda i: (0, i)),
            pl.BlockSpec((ROUTE_ROWS, TM_MIX), lambda i: (0, i)),
            const(N_EXPERTS, LANES),
        ],
        out_shape=[
            jax.ShapeDtypeStruct((t, d), F32),
            jax.ShapeDtypeStruct((t, d // 2), U32),
            jax.ShapeDtypeStruct((ROUTE_ROWS, t), I32),
            jax.ShapeDtypeStruct((ROUTE_ROWS, t), F32),
            jax.ShapeDtypeStruct((N_EXPERTS, LANES), I32),
        ],
        scratch_shapes=[pltpu.VMEM((N_EXPERTS, LANES), F32)],
        compiler_params=_cparams(("arbitrary",)),
        name="mix",
    )(a2, proj2, proj2, x2, wout_b, sgw_b, sgb, sgg, sgbeta, g1, b1, wr, br)


DISPATCH_CHUNK = 256


def _dispatch_kernel(pos_ref, last_ref, hp_ref, xs_hbm, zero_ref, zsem, sem):
    i = pl.program_id(0)

    def row_copy(j, dst):
        return pltpu.make_async_copy(hp_ref.at[pl.ds(j, 1)], xs_hbm.at[pl.ds(dst, 1)], sem)

    def zero_copy(e):
        start = pl.multiple_of(last_ref[e] * TMX, TMX)
        return pltpu.make_async_copy(zero_ref, xs_hbm.at[pl.ds(start, TMX)], zsem)

    @pl.when(i == 0)
    def _():
        zero_ref[...] = jnp.zeros_like(zero_ref)

        def zstart(e, c):
            zero_copy(e).start()
            return c

        def zwait(e, c):
            zero_copy(e).wait()
            return c

        lax.fori_loop(0, N_EXPERTS, zstart, 0)
        lax.fori_loop(0, N_EXPERTS, zwait, 0)

    def start(j, c):
        row_copy(j, pos_ref[0, j]).start()
        row_copy(j, pos_ref[1, j]).start()
        return c

    def wait(j, c):
        row_copy(j, pos_ref[0, j]).wait()
        row_copy(j, pos_ref[1, j]).wait()
        return c

    lax.fori_loop(0, DISPATCH_CHUNK, start, 0, unroll=8)
    lax.fori_loop(0, DISPATCH_CHUNK, wait, 0, unroll=8)


def _dispatch(pos3, last_tile, hp, n_rows):
    t, w = hp.shape
    return pl.pallas_call(
        _dispatch_kernel,
        grid=(t // DISPATCH_CHUNK,),
        in_specs=[
            pl.BlockSpec((None, 2, DISPATCH_CHUNK), lambda i: (i, 0, 0), memory_space=pltpu.SMEM),
            pl.BlockSpec(memory_space=pltpu.SMEM),
            pl.BlockSpec((DISPATCH_CHUNK, w), lambda i: (i, 0)),
        ],
        out_specs=pl.BlockSpec(memory_space=pl.ANY),
        out_shape=jax.ShapeDtypeStruct((n_rows, w), U32),
        scratch_shapes=[pltpu.VMEM((TMX, w), U32),
                        pltpu.SemaphoreType.DMA(()),
                        pltpu.SemaphoreType.DMA(())],
        compiler_params=_cparams(("arbitrary",)),
        name="dispatch",
    )(pos3, last_tile, hp)


def _experts_kernel(te_ref, nu_ref, x_ref, wg_ref, wu_ref, wd_ref, y_ref, wgu_s, wd_s):
    i = pl.program_id(0)
    d = wg_ref.shape[1]

    @pl.when(i < nu_ref[0])
    def _():
        first = jnp.logical_or(i == 0, te_ref[i] != te_ref[jnp.maximum(i - 1, 0)])

        @pl.when(first)
        def _():
            wgu_s[:, :D_EXPERT] = wg_ref[0].astype(BF16)
            wgu_s[:, D_EXPERT:] = wu_ref[0].astype(BF16)
            wd_s[...] = wd_ref[0].astype(BF16)

        lo, hi = _unpack_bf16_pair(x_ref[...])
        gu = jnp.dot(lo.astype(BF16), wgu_s[:d // 2, :], preferred_element_type=F32)
        gu = gu + jnp.dot(hi.astype(BF16), wgu_s[d // 2:, :], preferred_element_type=F32)
        gate = gu[:, :D_EXPERT]
        hmid = (gate * jax.nn.sigmoid(gate) * gu[:, D_EXPERT:]).astype(BF16)
        y = jnp.dot(hmid, wd_s[...], preferred_element_type=F32)
        y_ref[...] = _pack_bf16_pair(y)


def _experts(tile_expert, n_used, xs, w_gate, w_up, w_down):
    n_rows, w = xs.shape
    d = w_gate.shape[1]
    n_tiles = n_rows // TMX

    def row_map(i, te, nu):
        return (jnp.minimum(i, jnp.maximum(nu[0] - 1, 0)), 0)

    def w_map(i, te, nu):
        return (te[i], 0, 0)

    grid_spec = pltpu.PrefetchScalarGridSpec(
        num_scalar_prefetch=2,
        grid=(n_tiles,),
        in_specs=[
            pl.BlockSpec((TMX, w), row_map),
            pl.BlockSpec((1, d, D_EXPERT), w_map),
            pl.BlockSpec((1, d, D_EXPERT), w_map),
            pl.BlockSpec((1, D_EXPERT, d), w_map),
        ],
        out_specs=pl.BlockSpec((TMX, w), row_map),
        scratch_shapes=[pltpu.VMEM((d, 2 * D_EXPERT), BF16),
                        pltpu.VMEM((D_EXPERT, d), BF16)],
    )
    return pl.pallas_call(
        _experts_kernel,
        grid_spec=grid_spec,
        out_shape=jax.ShapeDtypeStruct((n_rows, w), U32),
        compiler_params=_cparams(("arbitrary",)),
        name="experts",
    )(tile_expert, n_used, xs, w_gate, w_up, w_down)


def _combine_kernel(pos_ref, posn_ref, y_hbm, h1_ref, wt_ref, g2_ref, b2_ref, o_ref, ybuf, sem):
    i = pl.program_id(0)
    n = pl.num_programs(0)
    slot = i % 2

    def row_copy(src, sl, which, j):
        return pltpu.make_async_copy(y_hbm.at[pl.ds(src, 1)],
                                     ybuf.at[sl, which, pl.ds(j, 1)], sem.at[sl])

    def issue(p_ref, sl):
        def body(j, c):
            row_copy(p_ref[0, j], sl, 0, j).start()
            row_copy(p_ref[1, j], sl, 1, j).start()
            return c
        lax.fori_loop(0, TM_OUT, body, 0, unroll=8)

    @pl.when(i == 0)
    def _():
        issue(pos_ref, 0)

    @pl.when(i + 1 < n)
    def _():
        issue(posn_ref, 1 - slot)

    def wait(j, c):
        row_copy(pos_ref[0, j], slot, 0, j).wait()
        row_copy(pos_ref[1, j], slot, 1, j).wait()
        return c

    lax.fori_loop(0, TM_OUT, wait, 0, unroll=8)

    lo1, hi1 = _unpack_bf16_pair(ybuf[slot, 0])
    lo2, hi2 = _unpack_bf16_pair(ybuf[slot, 1])
    w1 = wt_ref[:, 0:1]
    w2 = wt_ref[:, 1:2]
    ffn = jnp.concatenate([w1 * lo1 + w2 * lo2, w1 * hi1 + w2 * hi2], axis=1)
    o_ref[...] = _layernorm(DN_ALPHA * h1_ref[...] + ffn, g2_ref[...], b2_ref[...])


def _combine(pos3, ys, h1, wtok, g2, b2):
    t, d = h1.shape
    w = ys.shape[1]
    n = t // TM_OUT
    return pl.pallas_call(
        _combine_kernel,
        grid=(n,),
        in_specs=[
            pl.BlockSpec((None, 2, TM_OUT), lambda i: (i, 0, 0), memory_space=pltpu.SMEM),
            pl.BlockSpec((None, 2, TM_OUT), lambda i: (jnp.minimum(i + 1, n - 1), 0, 0),
                         memory_space=pltpu.SMEM),
            pl.BlockSpec(memory_space=pl.ANY),
            pl.BlockSpec((TM_OUT, d), lambda i: (i, 0)),
            pl.BlockSpec((TM_OUT, ROUTE_ROWS), lambda i: (i, 0)),
            pl.BlockSpec((1, d), lambda i: (0, 0)),
            pl.BlockSpec((1, d), lambda i: (0, 0)),
        ],
        out_specs=pl.BlockSpec((TM_OUT, d), lambda i: (i, 0)),
        out_shape=jax.ShapeDtypeStruct((t, d), F32),
        scratch_shapes=[pltpu.VMEM((2, 2, TM_OUT, w), U32),
                        pltpu.SemaphoreType.DMA((2,))],
        compiler_params=_cparams(("arbitrary",)),
        name="combine",
    )(pos3, pos3, ys, h1, wtok, g2, b2)


def _lambda_init(layer):
    return 0.8 - 0.6 * math.exp(-0.3 * layer)


def kernel(x, w_in, w_out, ln1_g, ln1_b, ln2_g, ln2_b, rel_bias, lam_q1, lam_k1, lam_q2, lam_k2,
           subln_g, sg_ln_g, sg_ln_b, sg_w, sg_b, w_router_group, b_router_group,
           w_router_expert, b_router_expert, w_exp_gate, w_exp_up, w_exp_down):
    b, s, d = x.shape
    t = b * s
    assert w_in.shape[0] == DEPTH == 1
    l = 0
    lam_init = _lambda_init(l)
    x2 = x.reshape(t, d)

    proj = _inproj(x2, w_in[l])

    lam = (jnp.exp(jnp.sum(lam_q1[l].astype(F32) * lam_k1[l].astype(F32)))
           - jnp.exp(jnp.sum(lam_q2[l].astype(F32) * lam_k2[l].astype(F32))) + lam_init)
    a = _attention(proj.reshape(b, s, -1), _bias_sequences(rel_bias, s), lam.reshape(1),
                   subln_g[l].reshape(1, -1), lam_init)

    wr = jnp.concatenate([w_router_group[l].T,
                          jnp.zeros((NR_EXPERT0 - N_GROUPS, d), F32),
                          jnp.transpose(w_router_expert[l], (0, 2, 1)).reshape(N_EXPERTS, d)],
                         axis=0)
    br = jnp.concatenate([b_router_group[l], jnp.zeros((NR_EXPERT0 - N_GROUPS,), F32),
                          b_router_expert[l].reshape(-1)]).reshape(NR_PAD, 1)
    h1, hp, ri, rw, cnt = _mix(
        a.reshape(t, -1), proj, x2, w_out[l].astype(BF16), sg_w[l].astype(BF16),
        sg_b[l].reshape(SG_GROUPS, SG_CHUNK, 1), sg_ln_g[l], sg_ln_b[l],
        ln1_g[l].reshape(1, d), ln1_b[l].reshape(1, d), wr, br)

    counts = cnt[:, 0]
    tiles_e = (counts + TMX - 1) // TMX
    tile_end = jnp.cumsum(tiles_e)
    tile_start = tile_end - tiles_e
    n_used = tile_end[-1:]
    max_tiles = (2 * t) // TMX + N_EXPERTS
    tile_ids = jnp.arange(max_tiles, dtype=I32)
    tile_expert = jnp.minimum(jnp.searchsorted(tile_end, tile_ids, side="right"),
                              N_EXPERTS - 1).astype(I32)
    tile_expert = jnp.where(tile_ids < n_used[0], tile_expert,
                            tile_expert[jnp.maximum(n_used[0] - 1, 0)])
    last_tile = jnp.maximum(tile_end - 1, 0).astype(I32)
    row_start = (tile_start * TMX).astype(I32)
    eids = jnp.arange(N_EXPERTS, dtype=I32)[:, None, None]
    pos = jnp.sum(jnp.where(ri[0:2][None] == eids, row_start[:, None, None], 0), axis=0) + ri[2:4]
    pos3 = pos.reshape(2, t // TM_OUT, TM_OUT).transpose(1, 0, 2)

    xs = _dispatch(pos3, last_tile, hp, max_tiles * TMX)
    ys = _experts(tile_expert, n_used.astype(I32), xs, w_exp_gate[l], w_exp_up[l], w_exp_down[l])
    out = _combine(pos3, ys, h1, rw.T, ln2_g[l].reshape(1, d), ln2_b[l].reshape(1, d))
    return out.reshape(b, s, d)
```

```python
import functools
import math

import jax
import jax.numpy as jnp
from jax import lax
from jax.experimental import pallas as pl
from jax.experimental.pallas import tpu as pltpu

F32 = jnp.float32
BF16 = jnp.bfloat16
U32 = jnp.uint32
I32 = jnp.int32

ATT_HEADS = 8
ATT_QK_DIM = 64
ATT_V_DIM = 128
ATT_WIDTH = ATT_HEADS * ATT_V_DIM
SG_GROUPS = 8
SG_DIM = 128
SG_WIDTH = SG_GROUPS * SG_DIM
SG_CHUNK = 128
REL_BUCKETS = 32
REL_MAX_DIST = 128
N_GROUPS = 4
EXPERTS_PER_GROUP = 8
N_EXPERTS = N_GROUPS * EXPERTS_PER_GROUP
D_EXPERT = 256
DEPTH = 1
DN_ALPHA = (2.0 * DEPTH) ** 0.25
LN_EPS = 1e-5
LOG2E = math.log2(math.e)

LANES = 128
SUBLANES = 8
VMEM_LIMIT_BYTES = 56 * 1024 * 1024

TM_IN = 1024
TN_IN = 1024
QB = 256
KC = 256
N_BIAS_TILES = 5
KC_PER_STEP = 1
TM_MIX = 512
TMX = 256
TM_OUT = 256
ROUTE_ROWS = 8
NR_EXPERT0 = 8
NR_PAD = NR_EXPERT0 + 32


def _cparams(sem, flags=None):
    return pltpu.CompilerParams(dimension_semantics=sem, vmem_limit_bytes=VMEM_LIMIT_BYTES,
                                flags=flags)


def _inproj_kernel(x_ref, w_ref, o_ref, xb_ref, *, n_q, n_lin, q_scale):
    j = pl.program_id(1)

    @pl.when(j == 0)
    def _():
        xb_ref[...] = x_ref[...].astype(BF16)

    acc = jnp.dot(xb_ref[...], w_ref[...].astype(BF16), preferred_element_type=F32)
    scale = jnp.where(j < n_q, q_scale, 1.0)
    o_ref[...] = jnp.where(j >= n_lin, jax.nn.gelu(acc), acc * scale).astype(BF16)


def _inproj(x2, w_in):
    t, d = x2.shape
    n = w_in.shape[1]
    n_q = (ATT_HEADS * 2 * ATT_QK_DIM) // TN_IN
    n_lin = (2 * ATT_HEADS * 2 * ATT_QK_DIM + ATT_WIDTH) // TN_IN
    kern = functools.partial(_inproj_kernel, n_q=n_q, n_lin=n_lin,
                             q_scale=(ATT_QK_DIM ** -0.5) * LOG2E)
    return pl.pallas_call(
        kern,
        grid=(t // TM_IN, n // TN_IN),
        in_specs=[pl.BlockSpec((TM_IN, d), lambda i, j: (i, 0)),
                  pl.BlockSpec((d, TN_IN), lambda i, j: (0, j))],
        out_specs=pl.BlockSpec((TM_IN, TN_IN), lambda i, j: (i, j)),
        out_shape=jax.ShapeDtypeStruct((t, n), BF16),
        scratch_shapes=[pltpu.VMEM((TM_IN, d), BF16)],
        compiler_params=_cparams(("arbitrary", "arbitrary")),
        name="inproj",
    )(x2, w_in)


def _attn_kernel(lam_ref, q_ref, k_ref, v_ref, bseq_ref, g_ref, o_ref, vaug_ref, bias_ref,
                 *, n_kc, out_scale):
    qi = pl.program_id(2)

    @pl.when(qi == 0)
    def _():
        vaug_ref[:, :ATT_V_DIM] = v_ref[0]
        vaug_ref[:, ATT_V_DIM:] = jnp.ones((vaug_ref.shape[0], ATT_V_DIM), BF16)
        for t in range(N_BIAS_TILES):
            seq = jnp.broadcast_to(bseq_ref[0, t:t + 1, :], (QB, 2 * KC))
            bias_ref[t] = pltpu.roll(seq, 0, 1, stride=1, stride_axis=0)[:, :KC]

    q = q_ref[0]
    lane = lax.broadcasted_iota(I32, q.shape, 1)
    zero = jnp.zeros_like(q)
    qq = jnp.concatenate([jnp.where(lane < ATT_QK_DIM, q, zero),
                          jnp.where(lane >= ATT_QK_DIM, q, zero)], axis=0)

    m = jnp.full((2 * QB, 1), -jnp.inf, F32)
    acc = jnp.zeros((2 * QB, 2 * ATT_V_DIM), F32)
    for step in range(n_kc // KC_PER_STEP):
        parts = []
        for kc in range(step * KC_PER_STEP, (step + 1) * KC_PER_STEP):
            kblk = k_ref[0, kc * KC:(kc + 1) * KC, :]
            sk = lax.dot_general(qq, kblk, (((1,), (1,)), ((), ())),
                                 preferred_element_type=F32)
            b = bias_ref[jnp.clip(kc - qi, -2, 2) + 2]
            parts.append(sk + jnp.concatenate([b, b], axis=0))
        s = jnp.concatenate(parts, axis=1)
        m_new = jnp.maximum(m, jnp.max(s, axis=1, keepdims=True))
        p = jnp.exp2(s - m_new).astype(BF16)
        vblk = vaug_ref[step * KC_PER_STEP * KC:(step + 1) * KC_PER_STEP * KC, :]
        acc = acc * jnp.exp2(m - m_new) + jnp.dot(p, vblk, preferred_element_type=F32)
        m = m_new
    lam = lam_ref[0]
    o0 = acc[:QB, :ATT_V_DIM] / acc[:QB, ATT_V_DIM:ATT_V_DIM + 1]
    o1 = acc[QB:, :ATT_V_DIM] / acc[QB:, ATT_V_DIM:ATT_V_DIM + 1]
    of = o0 - lam * o1
    of = of * lax.rsqrt(jnp.mean(of * of, axis=-1, keepdims=True) + LN_EPS)
    o_ref[0] = (of * (g_ref[...] * out_scale)).astype(o_ref.dtype)


def _attention(proj3, bias_seq, lam, subln_g, lam_init):
    b, s, _ = proj3.shape
    n_kc = s // KC
    h_off_k = ATT_HEADS
    h_off_v = 2 * ATT_HEADS
    kern = functools.partial(_attn_kernel, n_kc=n_kc, out_scale=1.0 - lam_init)
    return pl.pallas_call(
        kern,
        grid=(b, ATT_HEADS, s // QB),
        in_specs=[
            pl.BlockSpec(memory_space=pltpu.SMEM),
            pl.BlockSpec((1, QB, LANES), lambda bi, h, qi: (bi, qi, h)),
            pl.BlockSpec((1, s, LANES), lambda bi, h, qi: (bi, 0, h_off_k + h)),
            pl.BlockSpec((1, s, LANES), lambda bi, h, qi: (bi, 0, h_off_v + h)),
            pl.BlockSpec((1, N_BIAS_TILES, 2 * KC), lambda bi, h, qi: (h, 0, 0)),
            pl.BlockSpec((1, ATT_V_DIM), lambda bi, h, qi: (0, 0)),
        ],
        out_specs=pl.BlockSpec((1, QB, ATT_V_DIM), lambda bi, h, qi: (bi, qi, h)),
        out_shape=jax.ShapeDtypeStruct((b, s, ATT_WIDTH), BF16),
        scratch_shapes=[pltpu.VMEM((s, 2 * ATT_V_DIM), BF16),
                        pltpu.VMEM((N_BIAS_TILES, QB, KC), F32)],
        compiler_params=_cparams(("arbitrary", "arbitrary", "arbitrary")),
        name="attn",
    )(lam, proj3, proj3, proj3, bias_seq, subln_g)


def _rel_bucket(rel):
    half = REL_BUCKETS // 2
    max_exact = half // 2
    ret = jnp.where(rel > 0, half, 0)
    n = jnp.abs(rel)
    nf = jnp.maximum(n, 1).astype(F32)
    large = max_exact + (jnp.log(nf / max_exact) / math.log(REL_MAX_DIST / max_exact)
                         * (half - max_exact)).astype(I32)
    large = jnp.minimum(large, half - 1)
    return ret + jnp.where(n < max_exact, n, large)


def _bias_sequences(rel_bias, s):
    assert REL_MAX_DIST <= KC and QB == KC
    period = 2 * KC
    j = jnp.arange(period, dtype=I32)
    off = jnp.where(j < KC, j, j - period)
    d = jnp.arange(-2, 3, dtype=I32)[:, None]
    rel = jnp.clip(d * KC + off[None, :], -(s - 1), s - 1)
    return jnp.transpose(rel_bias[_rel_bucket(rel)], (2, 0, 1)).astype(F32) * LOG2E


def _layernorm(v, g, b):
    mu = jnp.mean(v, axis=-1, keepdims=True)
    var = jnp.mean(jnp.square(v - mu), axis=-1, keepdims=True)
    return (v - mu) * lax.rsqrt(var + LN_EPS) * g + b


def _pack_bf16_pair(v):
    n = v.shape[1] // 2
    bits = lax.bitcast_convert_type(v.astype(BF16).astype(F32), U32)
    return (bits[:, :n] >> 16) | (bits[:, n:] & jnp.uint32(0xFFFF0000))


def _unpack_bf16_pair(w):
    lo = lax.bitcast_convert_type(w << 16, F32)
    hi = lax.bitcast_convert_type(w & jnp.uint32(0xFFFF0000), F32)
    return lo, hi


def _first_argmax(vals, nrows):
    mx = jnp.max(vals, axis=0, keepdims=True)
    row = lax.broadcasted_iota(I32, vals.shape, 0)
    idx = jnp.min(jnp.where(vals == mx, row, nrows), axis=0, keepdims=True)
    return mx, idx


def _mix_kernel(a_ref, u_ref, vg_ref, x_ref, wout_ref, sgw_ref, sgb_ref, sgg_ref, sgbeta_ref,
                g1_ref, b1_ref, wr_ref, br_ref,
                h1_ref, hp_ref, ri_ref, rw_ref, cnt_ref, run_ref):
    i = pl.program_id(0)

    @pl.when(i == 0)
    def _():
        run_ref[...] = jnp.zeros_like(run_ref)

    n_chunks = TM_MIX // SG_CHUNK
    rows = []
    for c in range(n_chunks):
        cols = []
        for g in range(SG_GROUPS):
            rs = slice(c * SG_CHUNK, (c + 1) * SG_CHUNK)
            cs = slice(g * SG_DIM, (g + 1) * SG_DIM)
            vt = vg_ref[rs, cs].astype(F32)
            vn = _layernorm(vt, sgg_ref[g:g + 1, :], sgbeta_ref[g:g + 1, :])
            mixed = jnp.dot(sgw_ref[g], vn.astype(BF16), preferred_element_type=F32) + sgb_ref[g]
            cols.append((u_ref[rs, cs].astype(F32) * mixed).astype(BF16))
        rows.append(jnp.concatenate(cols, axis=1))
    sgate = jnp.concatenate(rows, axis=0)

    mix = jnp.dot(a_ref[...], wout_ref[:ATT_WIDTH, :], preferred_element_type=F32)
    mix = mix + jnp.dot(sgate, wout_ref[ATT_WIDTH:, :], preferred_element_type=F32)
    h1 = _layernorm(DN_ALPHA * x_ref[...] + mix, g1_ref[...], b1_ref[...])
    h1_ref[...] = h1
    hp_ref[...] = _pack_bf16_pair(h1)

    lt = lax.dot_general(wr_ref[...], h1, (((1,), (1,)), ((), ())),
                         preferred_element_type=F32, precision=lax.Precision.HIGHEST)
    lt = lt + br_ref[...]
    g_logits = lt[0:N_GROUPS]
    gmax, g_idx = _first_argmax(g_logits, N_GROUPS)
    g_gate = 1.0 / jnp.sum(jnp.exp(g_logits - gmax), axis=0, keepdims=True)
    e_logits = jnp.zeros((EXPERTS_PER_GROUP, TM_MIX), F32)
    for g in range(N_GROUPS):
        lo = NR_EXPERT0 + g * EXPERTS_PER_GROUP
        e_logits = jnp.where(g_idx == g, lt[lo:lo + EXPERTS_PER_GROUP], e_logits)
    v1, i1 = _first_argmax(e_logits, EXPERTS_PER_GROUP)
    erow = lax.broadcasted_iota(I32, e_logits.shape, 0)
    v2, i2 = _first_argmax(jnp.where(erow == i1, -jnp.inf, e_logits), EXPERTS_PER_GROUP)
    t = jnp.exp(v2 - v1)
    w1 = g_gate / (1.0 + t)
    w2 = g_gate * t / (1.0 + t)
    e1 = g_idx * EXPERTS_PER_GROUP + i1
    e2 = g_idx * EXPERTS_PER_GROUP + i2

    xrow = lax.broadcasted_iota(I32, (N_EXPERTS, TM_MIX), 0)
    oh1 = xrow == e1
    oh2 = xrow == e2
    oh = jnp.where(jnp.logical_or(oh1, oh2), 1.0, 0.0)
    tr = lax.broadcasted_iota(I32, (TM_MIX, TM_MIX), 0)
    tc = lax.broadcasted_iota(I32, (TM_MIX, TM_MIX), 1)
    before = jnp.where(tr < tc, 1.0, 0.0).astype(BF16)
    cum = jnp.dot(oh.astype(BF16), before, preferred_element_type=F32)
    base = cum + run_ref[:, 0:1]
    rank1 = jnp.sum(jnp.where(oh1, base, 0.0), axis=0, keepdims=True).astype(I32)
    rank2 = jnp.sum(jnp.where(oh2, base, 0.0), axis=0, keepdims=True).astype(I32)
    run_ref[...] = run_ref[...] + jnp.sum(oh, axis=1, keepdims=True)
    cnt_ref[...] = run_ref[...].astype(I32)

    zi = jnp.zeros((ROUTE_ROWS - 4, TM_MIX), I32)
    ri_ref[...] = jnp.concatenate([e1, e2, rank1, rank2, zi], axis=0)
    zw = jnp.zeros((ROUTE_ROWS - 2, TM_MIX), F32)
    rw_ref[...] = jnp.concatenate([w1, w2, zw], axis=0)


def _mix(a2, proj2, x2, wout_b, sgw_b, sgb, sgg, sgbeta, g1, b1, wr, br):
    t, d = x2.shape
    u_blk = (proj2.shape[1] - 2 * SG_WIDTH) // SG_WIDTH
    const = lambda *shape: pl.BlockSpec(shape, lambda i: (0,) * len(shape))
    return pl.pallas_call(
        _mix_kernel,
        grid=(t // TM_MIX,),
        in_specs=[
            pl.BlockSpec((TM_MIX, ATT_WIDTH), lambda i: (i, 0)),
            pl.BlockSpec((TM_MIX, SG_WIDTH), lambda i: (i, u_blk)),
            pl.BlockSpec((TM_MIX, SG_WIDTH), lambda i: (i, u_blk + 1)),
            pl.BlockSpec((TM_MIX, d), lambda i: (i, 0)),
            pl.BlockSpec((d, d), lambda i: (0, 0), pipeline_mode=pl.Buffered(1)),
            const(SG_GROUPS, SG_CHUNK, SG_CHUNK),
            const(SG_GROUPS, SG_CHUNK, 1),
            const(SG_GROUPS, SG_DIM),
            const(SG_GROUPS, SG_DIM),
            const(1, d),
            const(1, d),
            const(NR_PAD, d),
            const(NR_PAD, 1),
        ],
        out_specs=[
            pl.BlockSpec((TM_MIX, d), lambda i: (i, 0)),
            pl.BlockSpec((TM_MIX, d // 2), lambda i: (i, 0)),
            pl.BlockSpec((ROUTE_ROWS, TM_MIX), lambda i: (0, i)),
            pl.BlockSpec((ROUTE_ROWS, TM_MIX), lambda i: (0, i)),
            const(N_EXPERTS, LANES),
        ],
        out_shape=[
            jax.ShapeDtypeStruct((t, d), F32),
            jax.ShapeDtypeStruct((t, d // 2), U32),
            jax.ShapeDtypeStruct((ROUTE_ROWS, t), I32),
            jax.ShapeDtypeStruct((ROUTE_ROWS, t), F32),
            jax.ShapeDtypeStruct((N_EXPERTS, LANES), I32),
        ],
        scratch_shapes=[pltpu.VMEM((N_EXPERTS, LANES), F32)],
        compiler_params=_cparams(("arbitrary",)),
        name="mix",
    )(a2, proj2, proj2, x2, wout_b, sgw_b, sgb, sgg, sgbeta, g1, b1, wr, br)


DISPATCH_CHUNK = 256


def _dispatch_kernel(pos_ref, last_ref, hp_ref, xs_hbm, zero_ref, zsem, sem):
    i = pl.program_id(0)

    def row_copy(j, dst):
        return pltpu.make_async_copy(hp_ref.at[pl.ds(j, 1)], xs_hbm.at[pl.ds(dst, 1)], sem)

    def zero_copy(e):
        start = pl.multiple_of(last_ref[e] * TMX, TMX)
        return pltpu.make_async_copy(zero_ref, xs_hbm.at[pl.ds(start, TMX)], zsem)

    @pl.when(i == 0)
    def _():
        zero_ref[...] = jnp.zeros_like(zero_ref)

        def zstart(e, c):
            zero_copy(e).start()
            return c

        def zwait(e, c):
            zero_copy(e).wait()
            return c

        lax.fori_loop(0, N_EXPERTS, zstart, 0)
        lax.fori_loop(0, N_EXPERTS, zwait, 0)

    def start(j, c):
        row_copy(j, pos_ref[0, j]).start()
        row_copy(j, pos_ref[1, j]).start()
        return c

    def wait(j, c):
        row_copy(j, pos_ref[0, j]).wait()
        row_copy(j, pos_ref[1, j]).wait()
        return c

    lax.fori_loop(0, DISPATCH_CHUNK, start, 0, unroll=8)
    lax.fori_loop(0, DISPATCH_CHUNK, wait, 0, unroll=8)


def _dispatch(pos3, last_tile, hp, n_rows):
    t, w = hp.shape
    return pl.pallas_call(
        _dispatch_kernel,
        grid=(t // DISPATCH_CHUNK,),
        in_specs=[
            pl.BlockSpec((None, 2, DISPATCH_CHUNK), lambda i: (i, 0, 0), memory_space=pltpu.SMEM),
            pl.BlockSpec(memory_space=pltpu.SMEM),
            pl.BlockSpec((DISPATCH_CHUNK, w), lambda i: (i, 0)),
        ],
        out_specs=pl.BlockSpec(memory_space=pl.ANY),
        out_shape=jax.ShapeDtypeStruct((n_rows, w), U32),
        scratch_shapes=[pltpu.VMEM((TMX, w), U32),
                        pltpu.SemaphoreType.DMA(()),
                        pltpu.SemaphoreType.DMA(())],
        compiler_params=_cparams(("arbitrary",)),
        name="dispatch",
    )(pos3, last_tile, hp)


def _experts_kernel(te_ref, nu_ref, x_ref, wg_ref, wu_ref, wd_ref, y_ref, wgu_s, wd_s):
    i = pl.program_id(0)
    d = wg_ref.shape[1]

    @pl.when(i < nu_ref[0])
    def _():
        first = jnp.logical_or(i == 0, te_ref[i] != te_ref[jnp.maximum(i - 1, 0)])

        @pl.when(first)
        def _():
            wgu_s[:, :D_EXPERT] = wg_ref[0].astype(BF16)
            wgu_s[:, D_EXPERT:] = wu_ref[0].astype(BF16)
            wd_s[...] = wd_ref[0].astype(BF16)

        lo, hi = _unpack_bf16_pair(x_ref[...])
        gu = jnp.dot(lo.astype(BF16), wgu_s[:d // 2, :], preferred_element_type=F32)
        gu = gu + jnp.dot(hi.astype(BF16), wgu_s[d // 2:, :], preferred_element_type=F32)
        gate = gu[:, :D_EXPERT]
        hmid = (gate * jax.nn.sigmoid(gate) * gu[:, D_EXPERT:]).astype(BF16)
        y = jnp.dot(hmid, wd_s[...], preferred_element_type=F32)
        y_ref[...] = _pack_bf16_pair(y)


def _experts(tile_expert, n_used, xs, w_gate, w_up, w_down):
    n_rows, w = xs.shape
    d = w_gate.shape[1]
    n_tiles = n_rows // TMX

    def row_map(i, te, nu):
        return (jnp.minimum(i, jnp.maximum(nu[0] - 1, 0)), 0)

    def w_map(i, te, nu):
        return (te[i], 0, 0)

    grid_spec = pltpu.PrefetchScalarGridSpec(
        num_scalar_prefetch=2,
        grid=(n_tiles,),
        in_specs=[
            pl.BlockSpec((TMX, w), row_map),
            pl.BlockSpec((1, d, D_EXPERT), w_map),
            pl.BlockSpec((1, d, D_EXPERT), w_map),
            pl.BlockSpec((1, D_EXPERT, d), w_map),
        ],
        out_specs=pl.BlockSpec((TMX, w), row_map),
        scratch_shapes=[pltpu.VMEM((d, 2 * D_EXPERT), BF16),
                        pltpu.VMEM((D_EXPERT, d), BF16)],
    )
    return pl.pallas_call(
        _experts_kernel,
        grid_spec=grid_spec,
        out_shape=jax.ShapeDtypeStruct((n_rows, w), U32),
        compiler_params=_cparams(("arbitrary",)),
        name="experts",
    )(tile_expert, n_used, xs, w_gate, w_up, w_down)


def _combine_kernel(pos_ref, posn_ref, y_hbm, h1_ref, wt_ref, g2_ref, b2_ref, o_ref, ybuf, sem):
    i = pl.program_id(0)
    n = pl.num_programs(0)
    slot = i % 2

    def row_copy(src, sl, which, j):
        return pltpu.make_async_copy(y_hbm.at[pl.ds(src, 1)],
                                     ybuf.at[sl, which, pl.ds(j, 1)], sem.at[sl])

    def issue(p_ref, sl):
        def body(j, c):
            row_copy(p_ref[0, j], sl, 0, j).start()
            row_copy(p_ref[1, j], sl, 1, j).start()
            return c
        lax.fori_loop(0, TM_OUT, body, 0, unroll=8)

    @pl.when(i == 0)
    def _():
        issue(pos_ref, 0)

    @pl.when(i + 1 < n)
    def _():
        issue(posn_ref, 1 - slot)

    def wait(j, c):
        row_copy(pos_ref[0, j], slot, 0, j).wait()
        row_copy(pos_ref[1, j], slot, 1, j).wait()
        return c

    lax.fori_loop(0, TM_OUT, wait, 0, unroll=8)

    lo1, hi1 = _unpack_bf16_pair(ybuf[slot, 0])
    lo2, hi2 = _unpack_bf16_pair(ybuf[slot, 1])
    w1 = wt_ref[:, 0:1]
    w2 = wt_ref[:, 1:2]
    ffn = jnp.concatenate([w1 * lo1 + w2 * lo2, w1 * hi1 + w2 * hi2], axis=1)
    o_ref[...] = _layernorm(DN_ALPHA * h1_ref[...] + ffn, g2_ref[...], b2_ref[...])


def _combine(pos3, ys, h1, wtok, g2, b2):
    t, d = h1.shape
    w = ys.shape[1]
    n = t // TM_OUT
    return pl.pallas_call(
        _combine_kernel,
        grid=(n,),
        in_specs=[
            pl.BlockSpec((None, 2, TM_OUT), lambda i: (i, 0, 0), memory_space=pltpu.SMEM),
            pl.BlockSpec((None, 2, TM_OUT), lambda i: (jnp.minimum(i + 1, n - 1), 0, 0),
                         memory_space=pltpu.SMEM),
            pl.BlockSpec(memory_space=pl.ANY),
            pl.BlockSpec((TM_OUT, d), lambda i: (i, 0)),
            pl.BlockSpec((TM_OUT, ROUTE_ROWS), lambda i: (i, 0)),
            pl.BlockSpec((1, d), lambda i: (0, 0)),
            pl.BlockSpec((1, d), lambda i: (0, 0)),
        ],
        out_specs=pl.BlockSpec((TM_OUT, d), lambda i: (i, 0)),
        out_shape=jax.ShapeDtypeStruct((t, d), F32),
        scratch_shapes=[pltpu.VMEM((2, 2, TM_OUT, w), U32),
                        pltpu.SemaphoreType.DMA((2,))],
        compiler_params=_cparams(("arbitrary",)),
        name="combine",
    )(pos3, pos3, ys, h1, wtok, g2, b2)


def _lambda_init(layer):
    return 0.8 - 0.6 * math.exp(-0.3 * layer)


def kernel(x, w_in, w_out, ln1_g, ln1_b, ln2_g, ln2_b, rel_bias, lam_q1, lam_k1, lam_q2, lam_k2,
           subln_g, sg_ln_g, sg_ln_b, sg_w, sg_b, w_router_group, b_router_group,
           w_router_expert, b_router_expert, w_exp_gate, w_exp_up, w_exp_down):
    b, s, d = x.shape
    t = b * s
    assert w_in.shape[0] == DEPTH == 1
    l = 0
    lam_init = _lambda_init(l)
    x2 = x.reshape(t, d)

    proj = _inproj(x2, w_in[l])

    lam = (jnp.exp(jnp.sum(lam_q1[l].astype(F32) * lam_k1[l].astype(F32)))
           - jnp.exp(jnp.sum(lam_q2[l].astype(F32) * lam_k2[l].astype(F32))) + lam_init)
    a = _attention(proj.reshape(b, s, -1), _bias_sequences(rel_bias, s), lam.reshape(1),
                   subln_g[l].reshape(1, -1), lam_init)

    wr = jnp.concatenate([w_router_group[l].T,
                          jnp.zeros((NR_EXPERT0 - N_GROUPS, d), F32),
                          jnp.transpose(w_router_expert[l], (0, 2, 1)).reshape(N_EXPERTS, d)],
                         axis=0)
    br = jnp.concatenate([b_router_group[l], jnp.zeros((NR_EXPERT0 - N_GROUPS,), F32),
                          b_router_expert[l].reshape(-1)]).reshape(NR_PAD, 1)
    h1, hp, ri, rw, cnt = _mix(
        a.reshape(t, -1), proj, x2, w_out[l].astype(BF16), sg_w[l].astype(BF16),
        sg_b[l].reshape(SG_GROUPS, SG_CHUNK, 1), sg_ln_g[l], sg_ln_b[l],
        ln1_g[l].reshape(1, d), ln1_b[l].reshape(1, d), wr, br)

    counts = cnt[:, 0]
    tiles_e = (counts + TMX - 1) // TMX
    tile_end = jnp.cumsum(tiles_e)
    tile_start = tile_end - tiles_e
    n_used = tile_end[-1:]
    max_tiles = (2 * t) // TMX + N_EXPERTS
    tile_ids = jnp.arange(max_tiles, dtype=I32)
    tile_expert = jnp.minimum(jnp.searchsorted(tile_end, tile_ids, side="right"),
                              N_EXPERTS - 1).astype(I32)
    tile_expert = jnp.where(tile_ids < n_used[0], tile_expert,
                            tile_expert[jnp.maximum(n_used[0] - 1, 0)])
    last_tile = jnp.maximum(tile_end - 1, 0).astype(I32)
    row_start = (tile_start * TMX).astype(I32)
    eids = jnp.arange(N_EXPERTS, dtype=I32)[:, None, None]
    pos = jnp.sum(jnp.where(ri[0:2][None] == eids, row_start[:, None, None], 0), axis=0) + ri[2:4]
    pos3 = pos.reshape(2, t // TM_OUT, TM_OUT).transpose(1, 0, 2)

    xs = _dispatch(pos3, last_tile, hp, max_tiles * TMX)
    ys = _experts(tile_expert, n_used.astype(I32), xs, w_exp_gate[l], w_exp_up[l], w_exp_down[l])
    out = _combine(pos3, ys, h1, rw.T, ln2_g[l].reshape(1, d), ln2_b[l].reshape(1, d))
    return out.reshape(b, s, d)
```

```python
import functools
import math

import jax
import jax.numpy as jnp
from jax import lax
from jax.experimental import pallas as pl
from jax.experimental.pallas import tpu as pltpu

F32 = jnp.float32
BF16 = jnp.bfloat16
U32 = jnp.uint32
I32 = jnp.int32

ATT_HEADS = 8
ATT_QK_DIM = 64
ATT_V_DIM = 128
ATT_WIDTH = ATT_HEADS * ATT_V_DIM
SG_GROUPS = 8
SG_DIM = 128
SG_WIDTH = SG_GROUPS * SG_DIM
SG_CHUNK = 128
REL_BUCKETS = 32
REL_MAX_DIST = 128
N_GROUPS = 4
EXPERTS_PER_GROUP = 8
N_EXPERTS = N_GROUPS * EXPERTS_PER_GROUP
D_EXPERT = 256
DEPTH = 1
DN_ALPHA = (2.0 * DEPTH) ** 0.25
LN_EPS = 1e-5
LOG2E = math.log2(math.e)

LANES = 128
SUBLANES = 8
VMEM_LIMIT_BYTES = 56 * 1024 * 1024

TM_IN = 1024
TN_IN = 1024
QB = 256
KC = 256
N_BIAS_TILES = 5
KC_PER_STEP = 1
TM_MIX = 512
TMX = 256
TM_OUT = 256
ROUTE_ROWS = 8
NR_EXPERT0 = 8
NR_PAD = NR_EXPERT0 + 32


def _cparams(sem, flags=None):
    return pltpu.CompilerParams(dimension_semantics=sem, vmem_limit_bytes=VMEM_LIMIT_BYTES,
                                flags=flags)


def _inproj_kernel(x_ref, w_ref, o_ref, xb_ref, *, n_q, n_lin, q_scale):
    j = pl.program_id(1)

    @pl.when(j == 0)
    def _():
        xb_ref[...] = x_ref[...].astype(BF16)

    acc = jnp.dot(xb_ref[...], w_ref[...].astype(BF16), preferred_element_type=F32)

    @pl.when(j < n_q)
    def _():
        o_ref[...] = (acc * q_scale).astype(BF16)

    @pl.when(jnp.logical_and(j >= n_q, j < n_lin))
    def _():
        o_ref[...] = acc.astype(BF16)

    @pl.when(j >= n_lin)
    def _():
        o_ref[...] = jax.nn.gelu(acc).astype(BF16)


def _inproj(x2, w_in):
    t, d = x2.shape
    n = w_in.shape[1]
    n_q = (ATT_HEADS * 2 * ATT_QK_DIM) // TN_IN
    n_lin = (2 * ATT_HEADS * 2 * ATT_QK_DIM + ATT_WIDTH) // TN_IN
    kern = functools.partial(_inproj_kernel, n_q=n_q, n_lin=n_lin,
                             q_scale=(ATT_QK_DIM ** -0.5) * LOG2E)
    return pl.pallas_call(
        kern,
        grid=(t // TM_IN, n // TN_IN),
        in_specs=[pl.BlockSpec((TM_IN, d), lambda i, j: (i, 0)),
                  pl.BlockSpec((d, TN_IN), lambda i, j: (0, j))],
        out_specs=pl.BlockSpec((TM_IN, TN_IN), lambda i, j: (i, j)),
        out_shape=jax.ShapeDtypeStruct((t, n), BF16),
        scratch_shapes=[pltpu.VMEM((TM_IN, d), BF16)],
        compiler_params=_cparams(("arbitrary", "arbitrary")),
        name="inproj",
    )(x2, w_in)


def _attn_kernel(lam_ref, q_ref, k_ref, v_ref, bseq_ref, g_ref, o_ref, vaug_ref, bias_ref,
                 *, n_kc, out_scale):
    qi = pl.program_id(2)

    @pl.when(qi == 0)
    def _():
        vaug_ref[:, :ATT_V_DIM] = v_ref[0]
        vaug_ref[:, ATT_V_DIM:] = jnp.ones((vaug_ref.shape[0], ATT_V_DIM), BF16)
        for t in range(N_BIAS_TILES):
            seq = jnp.broadcast_to(bseq_ref[0, t:t + 1, :], (QB, 2 * KC))
            bias_ref[t] = pltpu.roll(seq, 0, 1, stride=1, stride_axis=0)[:, :KC]

    q = q_ref[0]
    lane = lax.broadcasted_iota(I32, q.shape, 1)
    zero = jnp.zeros_like(q)
    qq = jnp.concatenate([jnp.where(lane < ATT_QK_DIM, q, zero),
                          jnp.where(lane >= ATT_QK_DIM, q, zero)], axis=0)

    m = jnp.full((2 * QB, 1), -jnp.inf, F32)
    acc = jnp.zeros((2 * QB, 2 * ATT_V_DIM), F32)
    for step in range(n_kc // KC_PER_STEP):
        parts = []
        for kc in range(step * KC_PER_STEP, (step + 1) * KC_PER_STEP):
            kblk = k_ref[0, kc * KC:(kc + 1) * KC, :]
            sk = lax.dot_general(qq, kblk, (((1,), (1,)), ((), ())),
                                 preferred_element_type=F32)
            b = bias_ref[jnp.clip(kc - qi, -2, 2) + 2]
            parts.append(sk + jnp.concatenate([b, b], axis=0))
        s = jnp.concatenate(parts, axis=1)
        m_new = jnp.maximum(m, jnp.max(s, axis=1, keepdims=True))
        p = jnp.exp2(s - m_new).astype(BF16)
        vblk = vaug_ref[step * KC_PER_STEP * KC:(step + 1) * KC_PER_STEP * KC, :]
        acc = acc * jnp.exp2(m - m_new) + jnp.dot(p, vblk, preferred_element_type=F32)
        m = m_new
    lam = lam_ref[0]
    o0 = acc[:QB, :ATT_V_DIM] / acc[:QB, ATT_V_DIM:ATT_V_DIM + 1]
    o1 = acc[QB:, :ATT_V_DIM] / acc[QB:, ATT_V_DIM:ATT_V_DIM + 1]
    of = o0 - lam * o1
    of = of * lax.rsqrt(jnp.mean(of * of, axis=-1, keepdims=True) + LN_EPS)
    o_ref[0] = (of * (g_ref[...] * out_scale)).astype(o_ref.dtype)


VT_ROWS = ATT_V_DIM + 16


def _attn_kernel_t(lam_ref, q_ref, k_ref, v_ref, bseq_ref, g_ref, o_ref, vt_ref, bias_ref,
                   *, n_kc, out_scale):
    qi = pl.program_id(2)

    @pl.when(qi == 0)
    def _():
        row = lax.broadcasted_iota(I32, (VT_ROWS - ATT_V_DIM, KC), 0)
        ones_rows = jnp.where(row == 0, 1.0, 0.0).astype(BF16)
        for c in range(n_kc):
            blk = v_ref[0, c * KC:(c + 1) * KC, :].astype(F32)
            vt_ref[c, :ATT_V_DIM, :] = blk.T.astype(BF16)
            vt_ref[c, ATT_V_DIM:, :] = ones_rows
        for t in range(N_BIAS_TILES):
            seq = jnp.broadcast_to(bseq_ref[0, t:t + 1, :], (KC, 2 * QB))
            bias_ref[t] = pltpu.roll(seq, 0, 1, stride=1, stride_axis=0)[:, :QB]

    q = q_ref[0]
    lane = lax.broadcasted_iota(I32, q.shape, 1)
    zero = jnp.zeros_like(q)
    qq = jnp.concatenate([jnp.where(lane < ATT_QK_DIM, q, zero),
                          jnp.where(lane >= ATT_QK_DIM, q, zero)], axis=0)

    def scores(kc):
        kblk = k_ref[0, kc * KC:(kc + 1) * KC, :]
        s = lax.dot_general(kblk, qq, (((1,), (1,)), ((), ())),
                            preferred_element_type=F32)
        b = bias_ref[jnp.clip(kc - qi, -2, 2) + 2]
        s = s + jnp.concatenate([b, b], axis=1)
        return s, jnp.max(s, axis=0, keepdims=True)

    def weights(sc, m):
        s, cmax = sc
        m_new = jnp.maximum(m, cmax)
        return jnp.exp2(s - m_new).astype(BF16), jnp.exp2(m - m_new), m_new

    m = jnp.full((1, 2 * QB), -jnp.inf, F32)
    acc = jnp.zeros((VT_ROWS, 2 * QB), F32)
    ahead = [scores(0), scores(1)]
    p, alpha, m = weights(ahead[0], m)
    for kc in range(n_kc):
        if kc + 2 < n_kc:
            ahead.append(scores(kc + 2))
        if kc + 1 < n_kc:
            p_next, alpha_next, m = weights(ahead[kc + 1], m)
        acc = acc * alpha + jnp.dot(vt_ref[kc], p, preferred_element_type=F32)
        if kc + 1 < n_kc:
            p, alpha = p_next, alpha_next
    lam = lam_ref[0]
    o0 = acc[:ATT_V_DIM, :QB] / acc[ATT_V_DIM:ATT_V_DIM + 1, :QB]
    o1 = acc[:ATT_V_DIM, QB:] / acc[ATT_V_DIM:ATT_V_DIM + 1, QB:]
    of = o0 - lam * o1
    of = of * lax.rsqrt(jnp.mean(of * of, axis=0, keepdims=True) + LN_EPS)
    o_ref[0] = (of * (g_ref[...] * out_scale)).T.astype(o_ref.dtype)


def _attention(proj3, bias_seq, lam, subln_g, lam_init):
    b, s, _ = proj3.shape
    n_kc = s // KC
    h_off_k = ATT_HEADS
    h_off_v = 2 * ATT_HEADS
    kern = functools.partial(_attn_kernel_t, n_kc=n_kc, out_scale=1.0 - lam_init)
    bias_seq = jnp.roll(bias_seq[..., ::-1], 1, axis=-1)
    subln_g = subln_g.reshape(-1, 1)
    return pl.pallas_call(
        kern,
        grid=(b, ATT_HEADS, s // QB),
        in_specs=[
            pl.BlockSpec(memory_space=pltpu.SMEM),
            pl.BlockSpec((1, QB, LANES), lambda bi, h, qi: (bi, qi, h)),
            pl.BlockSpec((1, s, LANES), lambda bi, h, qi: (bi, 0, h_off_k + h)),
            pl.BlockSpec((1, s, LANES), lambda bi, h, qi: (bi, 0, h_off_v + h)),
            pl.BlockSpec((1, N_BIAS_TILES, 2 * KC), lambda bi, h, qi: (h, 0, 0)),
            pl.BlockSpec((ATT_V_DIM, 1), lambda bi, h, qi: (0, 0)),
        ],
        out_specs=pl.BlockSpec((1, QB, ATT_V_DIM), lambda bi, h, qi: (bi, qi, h)),
        out_shape=jax.ShapeDtypeStruct((b, s, ATT_WIDTH), BF16),
        scratch_shapes=[pltpu.VMEM((n_kc, VT_ROWS, KC), BF16),
                        pltpu.VMEM((N_BIAS_TILES, KC, QB), F32)],
        compiler_params=_cparams(("arbitrary", "arbitrary", "arbitrary")),
        name="attn",
    )(lam, proj3, proj3, proj3, bias_seq, subln_g)


def _rel_bucket(rel):
    half = REL_BUCKETS // 2
    max_exact = half // 2
    ret = jnp.where(rel > 0, half, 0)
    n = jnp.abs(rel)
    nf = jnp.maximum(n, 1).astype(F32)
    large = max_exact + (jnp.log(nf / max_exact) / math.log(REL_MAX_DIST / max_exact)
                         * (half - max_exact)).astype(I32)
    large = jnp.minimum(large, half - 1)
    return ret + jnp.where(n < max_exact, n, large)


def _bias_sequences(rel_bias, s):
    assert REL_MAX_DIST <= KC and QB == KC
    period = 2 * KC
    j = jnp.arange(period, dtype=I32)
    off = jnp.where(j < KC, j, j - period)
    d = jnp.arange(-2, 3, dtype=I32)[:, None]
    rel = jnp.clip(d * KC + off[None, :], -(s - 1), s - 1)
    return jnp.transpose(rel_bias[_rel_bucket(rel)], (2, 0, 1)).astype(F32) * LOG2E


def _layernorm(v, g, b):
    mu = jnp.mean(v, axis=-1, keepdims=True)
    var = jnp.mean(jnp.square(v - mu), axis=-1, keepdims=True)
    return (v - mu) * lax.rsqrt(var + LN_EPS) * g + b


def _pack_bf16_pair(v):
    n = v.shape[1] // 2
    bits = lax.bitcast_convert_type(v.astype(BF16).astype(F32), U32)
    return (bits[:, :n] >> 16) | (bits[:, n:] & jnp.uint32(0xFFFF0000))


def _unpack_bf16_pair(w):
    lo = lax.bitcast_convert_type(w << 16, F32)
    hi = lax.bitcast_convert_type(w & jnp.uint32(0xFFFF0000), F32)
    return lo, hi


def _first_argmax(vals, nrows):
    mx = jnp.max(vals, axis=0, keepdims=True)
    row = lax.broadcasted_iota(I32, vals.shape, 0)
    idx = jnp.min(jnp.where(vals == mx, row, nrows), axis=0, keepdims=True)
    return mx, idx


def _mix_kernel(a_ref, u_ref, vg_ref, x_ref, wout_ref, sgw_ref, sgb_ref, sgg_ref, sgbeta_ref,
                g1_ref, b1_ref, wr_ref, br_ref,
                h1_ref, hp_ref, ri_ref, rw_ref, cnt_ref, run_ref):
    i = pl.program_id(0)

    @pl.when(i == 0)
    def _():
        run_ref[...] = jnp.zeros_like(run_ref)

    n_chunks = TM_MIX // SG_CHUNK
    rows = []
    for c in range(n_chunks):
        cols = []
        for g in range(SG_GROUPS):
            rs = slice(c * SG_CHUNK, (c + 1) * SG_CHUNK)
            cs = slice(g * SG_DIM, (g + 1) * SG_DIM)
            vt = vg_ref[rs, cs].astype(F32)
            vn = _layernorm(vt, sgg_ref[g:g + 1, :], sgbeta_ref[g:g + 1, :])
            mixed = jnp.dot(sgw_ref[g], vn.astype(BF16), preferred_element_type=F32) + sgb_ref[g]
            cols.append((u_ref[rs, cs].astype(F32) * mixed).astype(BF16))
        rows.append(jnp.concatenate(cols, axis=1))
    sgate = jnp.concatenate(rows, axis=0)

    mix = jnp.dot(a_ref[...], wout_ref[:ATT_WIDTH, :], preferred_element_type=F32)
    mix = mix + jnp.dot(sgate, wout_ref[ATT_WIDTH:, :], preferred_element_type=F32)
    h1 = _layernorm(DN_ALPHA * x_ref[...] + mix, g1_ref[...], b1_ref[...])
    h1_ref[...] = h1
    hp_ref[...] = _pack_bf16_pair(h1)

    lt = lax.dot_general(wr_ref[...], h1, (((1,), (1,)), ((), ())),
                         preferred_element_type=F32, precision=lax.Precision.HIGHEST)
    lt = lt + br_ref[...]
    g_logits = lt[0:N_GROUPS]
    gmax, g_idx = _first_argmax(g_logits, N_GROUPS)
    g_gate = 1.0 / jnp.sum(jnp.exp(g_logits - gmax), axis=0, keepdims=True)
    e_logits = jnp.zeros((EXPERTS_PER_GROUP, TM_MIX), F32)
    for g in range(N_GROUPS):
        lo = NR_EXPERT0 + g * EXPERTS_PER_GROUP
        e_logits = jnp.where(g_idx == g, lt[lo:lo + EXPERTS_PER_GROUP], e_logits)
    v1, i1 = _first_argmax(e_logits, EXPERTS_PER_GROUP)
    erow = lax.broadcasted_iota(I32, e_logits.shape, 0)
    v2, i2 = _first_argmax(jnp.where(erow == i1, -jnp.inf, e_logits), EXPERTS_PER_GROUP)
    t = jnp.exp(v2 - v1)
    w1 = g_gate / (1.0 + t)
    w2 = g_gate * t / (1.0 + t)
    e1 = g_idx * EXPERTS_PER_GROUP + i1
    e2 = g_idx * EXPERTS_PER_GROUP + i2

    xrow = lax.broadcasted_iota(I32, (N_EXPERTS, TM_MIX), 0)
    oh1 = xrow == e1
    oh2 = xrow == e2
    oh = jnp.where(jnp.logical_or(oh1, oh2), 1.0, 0.0)
    tr = lax.broadcasted_iota(I32, (TM_MIX, TM_MIX), 0)
    tc = lax.broadcasted_iota(I32, (TM_MIX, TM_MIX), 1)
    before = jnp.where(tr < tc, 1.0, 0.0).astype(BF16)
    cum = jnp.dot(oh.astype(BF16), before, preferred_element_type=F32)
    base = cum + run_ref[:, 0:1]
    rank1 = jnp.sum(jnp.where(oh1, base, 0.0), axis=0, keepdims=True).astype(I32)
    rank2 = jnp.sum(jnp.where(oh2, base, 0.0), axis=0, keepdims=True).astype(I32)
    run_ref[...] = run_ref[...] + jnp.sum(oh, axis=1, keepdims=True)
    cnt_ref[...] = run_ref[...].astype(I32)

    zi = jnp.zeros((ROUTE_ROWS - 4, TM_MIX), I32)
    ri_ref[...] = jnp.concatenate([e1, e2, rank1, rank2, zi], axis=0)
    zw = jnp.zeros((ROUTE_ROWS - 2, TM_MIX), F32)
    rw_ref[...] = jnp.concatenate([w1, w2, zw], axis=0)


def _mix(a2, proj2, x2, wout_b, sgw_b, sgb, sgg, sgbeta, g1, b1, wr, br):
    t, d = x2.shape
    u_blk = (proj2.shape[1] - 2 * SG_WIDTH) // SG_WIDTH
    const = lambda *shape: pl.BlockSpec(shape, lambda i: (0,) * len(shape))
    return pl.pallas_call(
        _mix_kernel,
        grid=(t // TM_MIX,),
        in_specs=[
            pl.BlockSpec((TM_MIX, ATT_WIDTH), lambda i: (i, 0)),
            pl.BlockSpec((TM_MIX, SG_WIDTH), lambda i: (i, u_blk)),
            pl.BlockSpec((TM_MIX, SG_WIDTH), lambda i: (i, u_blk + 1)),
            pl.BlockSpec((TM_MIX, d), lambda i: (i, 0)),
            pl.BlockSpec((d, d), lambda i: (0, 0), pipeline_mode=pl.Buffered(1)),
            const(SG_GROUPS, SG_CHUNK, SG_CHUNK),
            const(SG_GROUPS, SG_CHUNK, 1),
            const(SG_GROUPS, SG_DIM),
            const(SG_GROUPS, SG_DIM),
            const(1, d),
            const(1, d),
            const(NR_PAD, d),
            const(NR_PAD, 1),
        ],
        out_specs=[
            pl.BlockSpec((TM_MIX, d), lambda i: (i, 0)),
            pl.BlockSpec((TM_MIX, d // 2), lambda i: (i, 0)),
            pl.BlockSpec((ROUTE_ROWS, TM_MIX), lambda i: (0, i)),
            pl.BlockSpec((ROUTE_ROWS, TM_MIX), lambda i: (0, i)),
            const(N_EXPERTS, LANES),
        ],
        out_shape=[
            jax.ShapeDtypeStruct((t, d), F32),
            jax.ShapeDtypeStruct((t, d // 2), U32),
            jax.ShapeDtypeStruct((ROUTE_ROWS, t), I32),
            jax.ShapeDtypeStruct((ROUTE_ROWS, t), F32),
            jax.ShapeDtypeStruct((N_EXPERTS, LANES), I32),
        ],
        scratch_shapes=[pltpu.VMEM((N_EXPERTS, LANES), F32)],
        compiler_params=_cparams(("arbitrary",)),
        name="mix",
    )(a2, proj2, proj2, x2, wout_b, sgw_b, sgb, sgg, sgbeta, g1, b1, wr, br)


DISPATCH_CHUNK = 256


def _dispatch_kernel(pos_ref, last_ref, hp_ref, xs_hbm, zero_ref, zsem, sem):
    i = pl.program_id(0)

    def row_copy(j, dst):
        return pltpu.make_async_copy(hp_ref.at[pl.ds(j, 1)], xs_hbm.at[pl.ds(dst, 1)], sem)

    def zero_copy(e):
        start = pl.multiple_of(last_ref[e] * TMX, TMX)
        return pltpu.make_async_copy(zero_ref, xs_hbm.at[pl.ds(start, TMX)], zsem)

    @pl.when(i == 0)
    def _():
        zero_ref[...] = jnp.zeros_like(zero_ref)

        def zstart(e, c):
            zero_copy(e).start()
            return c

        def zwait(e, c):
            zero_copy(e).wait()
            return c

        lax.fori_loop(0, N_EXPERTS, zstart, 0)
        lax.fori_loop(0, N_EXPERTS, zwait, 0)

    def start(j, c):
        row_copy(j, pos_ref[0, j]).start()
        row_copy(j, pos_ref[1, j]).start()
        return c

    def wait(j, c):
        row_copy(j, pos_ref[0, j]).wait()
        row_copy(j, pos_ref[1, j]).wait()
        return c

    lax.fori_loop(0, DISPATCH_CHUNK, start, 0, unroll=8)
    lax.fori_loop(0, DISPATCH_CHUNK, wait, 0, unroll=8)


def _dispatch(pos3, last_tile, hp, n_rows):
    t, w = hp.shape
    return pl.pallas_call(
        _dispatch_kernel,
        grid=(t // DISPATCH_CHUNK,),
        in_specs=[
            pl.BlockSpec((None, 2, DISPATCH_CHUNK), lambda i: (i, 0, 0), memory_space=pltpu.SMEM),
            pl.BlockSpec(memory_space=pltpu.SMEM),
            pl.BlockSpec((DISPATCH_CHUNK, w), lambda i: (i, 0)),
        ],
        out_specs=pl.BlockSpec(memory_space=pl.ANY),
        out_shape=jax.ShapeDtypeStruct((n_rows, w), U32),
        scratch_shapes=[pltpu.VMEM((TMX, w), U32),
                        pltpu.SemaphoreType.DMA(()),
                        pltpu.SemaphoreType.DMA(())],
        compiler_params=_cparams(("arbitrary",)),
        name="dispatch",
    )(pos3, last_tile, hp)


def _experts_kernel(te_ref, nu_ref, x_ref, wg_ref, wu_ref, wd_ref, y_ref, wgu_s, wd_s):
    i = pl.program_id(0)
    d = wg_ref.shape[1]

    @pl.when(i < nu_ref[0])
    def _():
        first = jnp.logical_or(i == 0, te_ref[i] != te_ref[jnp.maximum(i - 1, 0)])

        @pl.when(first)
        def _():
            wgu_s[:, :D_EXPERT] = wg_ref[0].astype(BF16)
            wgu_s[:, D_EXPERT:] = wu_ref[0].astype(BF16)
            wd_s[...] = wd_ref[0].astype(BF16)

        lo, hi = _unpack_bf16_pair(x_ref[...])
        gu = jnp.dot(lo.astype(BF16), wgu_s[:d // 2, :], preferred_element_type=F32)
        gu = gu + jnp.dot(hi.astype(BF16), wgu_s[d // 2:, :], preferred_element_type=F32)
        gate = gu[:, :D_EXPERT]
        hmid = (gate * jax.nn.sigmoid(gate) * gu[:, D_EXPERT:]).astype(BF16)
        y = jnp.dot(hmid, wd_s[...], preferred_element_type=F32)
        y_ref[...] = _pack_bf16_pair(y)


def _experts(tile_expert, n_used, xs, w_gate, w_up, w_down):
    n_rows, w = xs.shape
    d = w_gate.shape[1]
    n_tiles = n_rows // TMX

    def row_map(i, te, nu):
        return (jnp.minimum(i, jnp.maximum(nu[0] - 1, 0)), 0)

    def w_map(i, te, nu):
        return (te[i], 0, 0)

    grid_spec = pltpu.PrefetchScalarGridSpec(
        num_scalar_prefetch=2,
        grid=(n_tiles,),
        in_specs=[
            pl.BlockSpec((TMX, w), row_map),
            pl.BlockSpec((1, d, D_EXPERT), w_map),
            pl.BlockSpec((1, d, D_EXPERT), w_map),
            pl.BlockSpec((1, D_EXPERT, d), w_map),
        ],
        out_specs=pl.BlockSpec((TMX, w), row_map),
        scratch_shapes=[pltpu.VMEM((d, 2 * D_EXPERT), BF16),
                        pltpu.VMEM((D_EXPERT, d), BF16)],
    )
    return pl.pallas_call(
        _experts_kernel,
        grid_spec=grid_spec,
        out_shape=jax.ShapeDtypeStruct((n_rows, w), U32),
        compiler_params=_cparams(("arbitrary",)),
        name="experts",
    )(tile_expert, n_used, xs, w_gate, w_up, w_down)


def _combine_kernel(pos_ref, posn_ref, y_hbm, h1_ref, wt_ref, g2_ref, b2_ref, o_ref, ybuf, sem):
    i = pl.program_id(0)
    n = pl.num_programs(0)
    slot = i % 2

    def row_copy(src, sl, which, j):
        return pltpu.make_async_copy(y_hbm.at[pl.ds(src, 1)],
                                     ybuf.at[sl, which, pl.ds(j, 1)], sem.at[sl])

    def issue(p_ref, sl):
        def body(j, c):
            row_copy(p_ref[0, j], sl, 0, j).start()
            row_copy(p_ref[1, j], sl, 1, j).start()
            return c
        lax.fori_loop(0, TM_OUT, body, 0, unroll=8)

    @pl.when(i == 0)
    def _():
        issue(pos_ref, 0)

    @pl.when(i + 1 < n)
    def _():
        issue(posn_ref, 1 - slot)

    def wait(j, c):
        row_copy(pos_ref[0, j], slot, 0, j).wait()
        row_copy(pos_ref[1, j], slot, 1, j).wait()
        return c

    lax.fori_loop(0, TM_OUT, wait, 0, unroll=8)

    lo1, hi1 = _unpack_bf16_pair(ybuf[slot, 0])
    lo2, hi2 = _unpack_bf16_pair(ybuf[slot, 1])
    w1 = wt_ref[:, 0:1]
    w2 = wt_ref[:, 1:2]
    ffn = jnp.concatenate([w1 * lo1 + w2 * lo2, w1 * hi1 + w2 * hi2], axis=1)
    o_ref[...] = _layernorm(DN_ALPHA * h1_ref[...] + ffn, g2_ref[...], b2_ref[...])


def _combine(pos3, ys, h1, wtok, g2, b2):
    t, d = h1.shape
    w = ys.shape[1]
    n = t // TM_OUT
    return pl.pallas_call(
        _combine_kernel,
        grid=(n,),
        in_specs=[
            pl.BlockSpec((None, 2, TM_OUT), lambda i: (i, 0, 0), memory_space=pltpu.SMEM),
            pl.BlockSpec((None, 2, TM_OUT), lambda i: (jnp.minimum(i + 1, n - 1), 0, 0),
                         memory_space=pltpu.SMEM),
            pl.BlockSpec(memory_space=pl.ANY),
            pl.BlockSpec((TM_OUT, d), lambda i: (i, 0)),
            pl.BlockSpec((TM_OUT, ROUTE_ROWS), lambda i: (i, 0)),
            pl.BlockSpec((1, d), lambda i: (0, 0)),
            pl.BlockSpec((1, d), lambda i: (0, 0)),
        ],
        out_specs=pl.BlockSpec((TM_OUT, d), lambda i: (i, 0)),
        out_shape=jax.ShapeDtypeStruct((t, d), F32),
        scratch_shapes=[pltpu.VMEM((2, 2, TM_OUT, w), U32),
                        pltpu.SemaphoreType.DMA((2,))],
        compiler_params=_cparams(("arbitrary",)),
        name="combine",
    )(pos3, pos3, ys, h1, wtok, g2, b2)


def _lambda_init(layer):
    return 0.8 - 0.6 * math.exp(-0.3 * layer)


def kernel(x, w_in, w_out, ln1_g, ln1_b, ln2_g, ln2_b, rel_bias, lam_q1, lam_k1, lam_q2, lam_k2,
           subln_g, sg_ln_g, sg_ln_b, sg_w, sg_b, w_router_group, b_router_group,
           w_router_expert, b_router_expert, w_exp_gate, w_exp_up, w_exp_down):
    b, s, d = x.shape
    t = b * s
    assert w_in.shape[0] == DEPTH == 1
    l = 0
    lam_init = _lambda_init(l)
    x2 = x.reshape(t, d)

    proj = _inproj(x2, w_in[l])

    lam = (jnp.exp(jnp.sum(lam_q1[l].astype(F32) * lam_k1[l].astype(F32)))
           - jnp.exp(jnp.sum(lam_q2[l].astype(F32) * lam_k2[l].astype(F32))) + lam_init)
    a = _attention(proj.reshape(b, s, -1), _bias_sequences(rel_bias, s), lam.reshape(1),
                   subln_g[l].reshape(1, -1), lam_init)

    wr = jnp.concatenate([w_router_group[l].T,
                          jnp.zeros((NR_EXPERT0 - N_GROUPS, d), F32),
                          jnp.transpose(w_router_expert[l], (0, 2, 1)).reshape(N_EXPERTS, d)],
                         axis=0)
    br = jnp.concatenate([b_router_group[l], jnp.zeros((NR_EXPERT0 - N_GROUPS,), F32),
                          b_router_expert[l].reshape(-1)]).reshape(NR_PAD, 1)
    h1, hp, ri, rw, cnt = _mix(
        a.reshape(t, -1), proj, x2, w_out[l].astype(BF16), sg_w[l].astype(BF16),
        sg_b[l].reshape(SG_GROUPS, SG_CHUNK, 1), sg_ln_g[l], sg_ln_b[l],
        ln1_g[l].reshape(1, d), ln1_b[l].reshape(1, d), wr, br)

    counts = cnt[:, 0]
    tiles_e = (counts + TMX - 1) // TMX
    tile_end = jnp.cumsum(tiles_e)
    tile_start = tile_end - tiles_e
    n_used = tile_end[-1:]
    max_tiles = (2 * t) // TMX + N_EXPERTS
    tile_ids = jnp.arange(max_tiles, dtype=I32)
    tile_expert = jnp.minimum(jnp.searchsorted(tile_end, tile_ids, side="right"),
                              N_EXPERTS - 1).astype(I32)
    tile_expert = jnp.where(tile_ids < n_used[0], tile_expert,
                            tile_expert[jnp.maximum(n_used[0] - 1, 0)])
    last_tile = jnp.maximum(tile_end - 1, 0).astype(I32)
    row_start = (tile_start * TMX).astype(I32)
    eids = jnp.arange(N_EXPERTS, dtype=I32)[:, None, None]
    pos = jnp.sum(jnp.where(ri[0:2][None] == eids, row_start[:, None, None], 0), axis=0) + ri[2:4]
    pos3 = pos.reshape(2, t // TM_OUT, TM_OUT).transpose(1, 0, 2)

    xs = _dispatch(pos3, last_tile, hp, max_tiles * TMX)
    ys = _experts(tile_expert, n_used.astype(I32), xs, w_exp_gate[l], w_exp_up[l], w_exp_down[l])
    out = _combine(pos3, ys, h1, rw.T, ln2_g[l].reshape(1, d), ln2_b[l].reshape(1, d))
    return out.reshape(b, s, d)
```

```python
import functools
import math

import jax
import jax.numpy as jnp
from jax import lax
from jax.experimental import pallas as pl
from jax.experimental.pallas import tpu as pltpu

F32 = jnp.float32
BF16 = jnp.bfloat16
U32 = jnp.uint32
I32 = jnp.int32

ATT_HEADS = 8
ATT_QK_DIM = 64
ATT_V_DIM = 128
ATT_WIDTH = ATT_HEADS * ATT_V_DIM
SG_GROUPS = 8
SG_DIM = 128
SG_WIDTH = SG_GROUPS * SG_DIM
SG_CHUNK = 128
REL_BUCKETS = 32
REL_MAX_DIST = 128
N_GROUPS = 4
EXPERTS_PER_GROUP = 8
N_EXPERTS = N_GROUPS * EXPERTS_PER_GROUP
D_EXPERT = 256
DEPTH = 1
DN_ALPHA = (2.0 * DEPTH) ** 0.25
LN_EPS = 1e-5
LOG2E = math.log2(math.e)

LANES = 128
SUBLANES = 8
VMEM_LIMIT_BYTES = 56 * 1024 * 1024

TM_IN = 1024
TN_IN = 1024
QB = 256
KC = 256
N_BIAS_TILES = 5
TM_MIX = 512
TMX = 256
TM_OUT = 256
ROUTE_ROWS = 8
NR_EXPERT0 = 8
NR_PAD = NR_EXPERT0 + 32


def _cparams(sem, flags=None):
    return pltpu.CompilerParams(dimension_semantics=sem, vmem_limit_bytes=VMEM_LIMIT_BYTES,
                                flags=flags)


def _inproj_kernel(x_ref, w_ref, o_ref, xb_ref, *, n_q, n_lin, q_scale):
    j = pl.program_id(1)

    @pl.when(j == 0)
    def _():
        xb_ref[...] = x_ref[...].astype(BF16)

    acc = jnp.dot(xb_ref[...], w_ref[...].astype(BF16), preferred_element_type=F32)

    @pl.when(j < n_q)
    def _():
        o_ref[...] = (acc * q_scale).astype(BF16)

    @pl.when(jnp.logical_and(j >= n_q, j < n_lin))
    def _():
        o_ref[...] = acc.astype(BF16)

    @pl.when(j >= n_lin)
    def _():
        o_ref[...] = jax.nn.gelu(acc).astype(BF16)


def _inproj(x2, w_in):
    t, d = x2.shape
    n = w_in.shape[1]
    n_q = (ATT_HEADS * 2 * ATT_QK_DIM) // TN_IN
    n_lin = (2 * ATT_HEADS * 2 * ATT_QK_DIM + ATT_WIDTH) // TN_IN
    kern = functools.partial(_inproj_kernel, n_q=n_q, n_lin=n_lin,
                             q_scale=(ATT_QK_DIM ** -0.5) * LOG2E)
    return pl.pallas_call(
        kern,
        grid=(t // TM_IN, n // TN_IN),
        in_specs=[pl.BlockSpec((TM_IN, d), lambda i, j: (i, 0)),
                  pl.BlockSpec((d, TN_IN), lambda i, j: (0, j))],
        out_specs=pl.BlockSpec((TM_IN, TN_IN), lambda i, j: (i, j)),
        out_shape=jax.ShapeDtypeStruct((t, n), BF16),
        scratch_shapes=[pltpu.VMEM((TM_IN, d), BF16)],
        compiler_params=_cparams(("arbitrary", "arbitrary")),
        name="inproj",
    )(x2, w_in)


def _attn_kernel(lam_ref, q_ref, k_ref, v_ref, bseq_ref, g_ref, o_ref, vaug_ref, bias_ref,
                 *, n_kc, out_scale):
    qi = pl.program_id(2)

    @pl.when(qi == 0)
    def _():
        vaug_ref[:, :ATT_V_DIM] = v_ref[0]
        vaug_ref[:, ATT_V_DIM:] = jnp.ones((vaug_ref.shape[0], ATT_V_DIM), BF16)
        for t in range(N_BIAS_TILES):
            seq = jnp.broadcast_to(bseq_ref[0, t:t + 1, :], (QB, 2 * KC))
            bias_ref[t] = pltpu.roll(seq, 0, 1, stride=1, stride_axis=0)[:, :KC]

    q = q_ref[0]
    lane = lax.broadcasted_iota(I32, q.shape, 1)
    zero = jnp.zeros_like(q)
    qq = jnp.concatenate([jnp.where(lane < ATT_QK_DIM, q, zero),
                          jnp.where(lane >= ATT_QK_DIM, q, zero)], axis=0)

    m = jnp.full((2 * QB, 1), -jnp.inf, F32)
    acc = jnp.zeros((2 * QB, 2 * ATT_V_DIM), F32)
    for kc in range(n_kc):
        kblk = k_ref[0, kc * KC:(kc + 1) * KC, :]
        s = lax.dot_general(qq, kblk, (((1,), (1,)), ((), ())),
                            preferred_element_type=F32)
        b = bias_ref[jnp.clip(kc - qi, -2, 2) + 2]
        s = s + jnp.concatenate([b, b], axis=0)
        m_new = jnp.maximum(m, jnp.max(s, axis=1, keepdims=True))
        p = jnp.exp2(s - m_new).astype(BF16)
        acc = acc * jnp.exp2(m - m_new) + jnp.dot(
            p, vaug_ref[kc * KC:(kc + 1) * KC, :], preferred_element_type=F32)
        m = m_new
    lam = lam_ref[0]
    o0 = acc[:QB, :ATT_V_DIM] / acc[:QB, ATT_V_DIM:ATT_V_DIM + 1]
    o1 = acc[QB:, :ATT_V_DIM] / acc[QB:, ATT_V_DIM:ATT_V_DIM + 1]
    of = o0 - lam * o1
    of = of * lax.rsqrt(jnp.mean(of * of, axis=-1, keepdims=True) + LN_EPS)
    o_ref[0] = (of * (g_ref[...] * out_scale)).astype(o_ref.dtype)


def _attention(proj3, bias_seq, lam, subln_g, lam_init):
    b, s, _ = proj3.shape
    n_kc = s // KC
    h_off_k = ATT_HEADS
    h_off_v = 2 * ATT_HEADS
    kern = functools.partial(_attn_kernel, n_kc=n_kc, out_scale=1.0 - lam_init)
    return pl.pallas_call(
        kern,
        grid=(b, ATT_HEADS, s // QB),
        in_specs=[
            pl.BlockSpec(memory_space=pltpu.SMEM),
            pl.BlockSpec((1, QB, LANES), lambda bi, h, qi: (bi, qi, h)),
            pl.BlockSpec((1, s, LANES), lambda bi, h, qi: (bi, 0, h_off_k + h)),
            pl.BlockSpec((1, s, LANES), lambda bi, h, qi: (bi, 0, h_off_v + h)),
            pl.BlockSpec((1, N_BIAS_TILES, 2 * KC), lambda bi, h, qi: (h, 0, 0)),
            pl.BlockSpec((1, ATT_V_DIM), lambda bi, h, qi: (0, 0)),
        ],
        out_specs=pl.BlockSpec((1, QB, ATT_V_DIM), lambda bi, h, qi: (bi, qi, h)),
        out_shape=jax.ShapeDtypeStruct((b, s, ATT_WIDTH), BF16),
        scratch_shapes=[pltpu.VMEM((s, 2 * ATT_V_DIM), BF16),
                        pltpu.VMEM((N_BIAS_TILES, QB, KC), F32)],
        compiler_params=_cparams(("arbitrary", "arbitrary", "arbitrary")),
        name="attn",
    )(lam, proj3, proj3, proj3, bias_seq, subln_g)


def _rel_bucket(rel):
    half = REL_BUCKETS // 2
    max_exact = half // 2
    ret = jnp.where(rel > 0, half, 0)
    n = jnp.abs(rel)
    nf = jnp.maximum(n, 1).astype(F32)
    large = max_exact + (jnp.log(nf / max_exact) / math.log(REL_MAX_DIST / max_exact)
                         * (half - max_exact)).astype(I32)
    large = jnp.minimum(large, half - 1)
    return ret + jnp.where(n < max_exact, n, large)


def _bias_sequences(rel_bias, s):
    assert REL_MAX_DIST <= KC and QB == KC
    period = 2 * KC
    j = jnp.arange(period, dtype=I32)
    off = jnp.where(j < KC, j, j - period)
    d = jnp.arange(-2, 3, dtype=I32)[:, None]
    rel = jnp.clip(d * KC + off[None, :], -(s - 1), s - 1)
    return jnp.transpose(rel_bias[_rel_bucket(rel)], (2, 0, 1)).astype(F32) * LOG2E


def _layernorm(v, g, b):
    mu = jnp.mean(v, axis=-1, keepdims=True)
    var = jnp.mean(jnp.square(v - mu), axis=-1, keepdims=True)
    return (v - mu) * lax.rsqrt(var + LN_EPS) * g + b


def _pack_bf16_pair(v):
    n = v.shape[1] // 2
    bits = lax.bitcast_convert_type(v.astype(BF16).astype(F32), U32)
    return (bits[:, :n] >> 16) | (bits[:, n:] & jnp.uint32(0xFFFF0000))


def _unpack_bf16_pair(w):
    lo = lax.bitcast_convert_type(w << 16, F32)
    hi = lax.bitcast_convert_type(w & jnp.uint32(0xFFFF0000), F32)
    return lo, hi


def _first_argmax(vals, nrows):
    mx = jnp.max(vals, axis=0, keepdims=True)
    row = lax.broadcasted_iota(I32, vals.shape, 0)
    idx = jnp.min(jnp.where(vals == mx, row, nrows), axis=0, keepdims=True)
    return mx, idx


def _mix_kernel(a_ref, u_ref, vg_ref, x_ref, wout_ref, sgw_ref, sgb_ref, sgg_ref, sgbeta_ref,
                g1_ref, b1_ref, wr_ref, br_ref,
                h1_ref, hp_ref, ri_ref, rw_ref, cnt_ref, run_ref):
    i = pl.program_id(0)

    @pl.when(i == 0)
    def _():
        run_ref[...] = jnp.zeros_like(run_ref)

    n_chunks = TM_MIX // SG_CHUNK
    rows = []
    for c in range(n_chunks):
        cols = []
        for g in range(SG_GROUPS):
            rs = slice(c * SG_CHUNK, (c + 1) * SG_CHUNK)
            cs = slice(g * SG_DIM, (g + 1) * SG_DIM)
            vt = vg_ref[rs, cs].astype(F32)
            vn = _layernorm(vt, sgg_ref[g:g + 1, :], sgbeta_ref[g:g + 1, :])
            mixed = jnp.dot(sgw_ref[g], vn.astype(BF16), preferred_element_type=F32) + sgb_ref[g]
            cols.append((u_ref[rs, cs].astype(F32) * mixed).astype(BF16))
        rows.append(jnp.concatenate(cols, axis=1))
    sgate = jnp.concatenate(rows, axis=0)

    mix = jnp.dot(a_ref[...], wout_ref[:ATT_WIDTH, :], preferred_element_type=F32)
    mix = mix + jnp.dot(sgate, wout_ref[ATT_WIDTH:, :], preferred_element_type=F32)
    h1 = _layernorm(DN_ALPHA * x_ref[...] + mix, g1_ref[...], b1_ref[...])
    h1_ref[...] = h1
    hp_ref[...] = _pack_bf16_pair(h1)

    lt = lax.dot_general(wr_ref[...], h1, (((1,), (1,)), ((), ())),
                         preferred_element_type=F32, precision=lax.Precision.HIGHEST)
    lt = lt + br_ref[...]
    g_logits = lt[0:N_GROUPS]
    gmax, g_idx = _first_argmax(g_logits, N_GROUPS)
    g_gate = 1.0 / jnp.sum(jnp.exp(g_logits - gmax), axis=0, keepdims=True)
    e_logits = jnp.zeros((EXPERTS_PER_GROUP, TM_MIX), F32)
    for g in range(N_GROUPS):
        lo = NR_EXPERT0 + g * EXPERTS_PER_GROUP
        e_logits = jnp.where(g_idx == g, lt[lo:lo + EXPERTS_PER_GROUP], e_logits)
    v1, i1 = _first_argmax(e_logits, EXPERTS_PER_GROUP)
    erow = lax.broadcasted_iota(I32, e_logits.shape, 0)
    v2, i2 = _first_argmax(jnp.where(erow == i1, -jnp.inf, e_logits), EXPERTS_PER_GROUP)
    t = jnp.exp(v2 - v1)
    w1 = g_gate / (1.0 + t)
    w2 = g_gate * t / (1.0 + t)
    e1 = g_idx * EXPERTS_PER_GROUP + i1
    e2 = g_idx * EXPERTS_PER_GROUP + i2

    xrow = lax.broadcasted_iota(I32, (N_EXPERTS, TM_MIX), 0)
    oh1 = xrow == e1
    oh2 = xrow == e2
    oh = jnp.where(jnp.logical_or(oh1, oh2), 1.0, 0.0)
    tr = lax.broadcasted_iota(I32, (TM_MIX, TM_MIX), 0)
    tc = lax.broadcasted_iota(I32, (TM_MIX, TM_MIX), 1)
    before = jnp.where(tr < tc, 1.0, 0.0).astype(BF16)
    cum = jnp.dot(oh.astype(BF16), before, preferred_element_type=F32)
    base = cum + run_ref[:, 0:1]
    rank1 = jnp.sum(jnp.where(oh1, base, 0.0), axis=0, keepdims=True).astype(I32)
    rank2 = jnp.sum(jnp.where(oh2, base, 0.0), axis=0, keepdims=True).astype(I32)
    run_ref[...] = run_ref[...] + jnp.sum(oh, axis=1, keepdims=True)
    cnt_ref[...] = run_ref[...].astype(I32)

    zi = jnp.zeros((ROUTE_ROWS - 4, TM_MIX), I32)
    ri_ref[...] = jnp.concatenate([e1, e2, rank1, rank2, zi], axis=0)
    zw = jnp.zeros((ROUTE_ROWS - 2, TM_MIX), F32)
    rw_ref[...] = jnp.concatenate([w1, w2, zw], axis=0)


def _mix(a2, proj2, x2, wout_b, sgw_b, sgb, sgg, sgbeta, g1, b1, wr, br):
    t, d = x2.shape
    u_blk = (proj2.shape[1] - 2 * SG_WIDTH) // SG_WIDTH
    const = lambda *shape: pl.BlockSpec(shape, lambda i: (0,) * len(shape))
    return pl.pallas_call(
        _mix_kernel,
        grid=(t // TM_MIX,),
        in_specs=[
            pl.BlockSpec((TM_MIX, ATT_WIDTH), lambda i: (i, 0)),
            pl.BlockSpec((TM_MIX, SG_WIDTH), lambda i: (i, u_blk)),
            pl.BlockSpec((TM_MIX, SG_WIDTH), lambda i: (i, u_blk + 1)),
            pl.BlockSpec((TM_MIX, d), lambda i: (i, 0)),
            pl.BlockSpec((d, d), lambda i: (0, 0), pipeline_mode=pl.Buffered(1)),
            const(SG_GROUPS, SG_CHUNK, SG_CHUNK),
            const(SG_GROUPS, SG_CHUNK, 1),
            const(SG_GROUPS, SG_DIM),
            const(SG_GROUPS, SG_DIM),
            const(1, d),
            const(1, d),
            const(NR_PAD, d),
            const(NR_PAD, 1),
        ],
        out_specs=[
            pl.BlockSpec((TM_MIX, d), lambda i: (i, 0)),
            pl.BlockSpec((TM_MIX, d // 2), lambda i: (i, 0)),
            pl.BlockSpec((ROUTE_ROWS, TM_MIX), lambda i: (0, i)),
            pl.BlockSpec((ROUTE_ROWS, TM_MIX), lambda i: (0, i)),
            const(N_EXPERTS, LANES),
        ],
        out_shape=[
            jax.ShapeDtypeStruct((t, d), F32),
            jax.ShapeDtypeStruct((t, d // 2), U32),
            jax.ShapeDtypeStruct((ROUTE_ROWS, t), I32),
            jax.ShapeDtypeStruct((ROUTE_ROWS, t), F32),
            jax.ShapeDtypeStruct((N_EXPERTS, LANES), I32),
        ],
        scratch_shapes=[pltpu.VMEM((N_EXPERTS, LANES), F32)],
        compiler_params=_cparams(("arbitrary",)),
        name="mix",
    )(a2, proj2, proj2, x2, wout_b, sgw_b, sgb, sgg, sgbeta, g1, b1, wr, br)


DISPATCH_CHUNK = 256


def _dispatch_kernel(pos_ref, last_ref, hp_ref, xs_hbm, zero_ref, zsem, sem):
    i = pl.program_id(0)

    def row_copy(j, dst):
        return pltpu.make_async_copy(hp_ref.at[pl.ds(j, 1)], xs_hbm.at[pl.ds(dst, 1)], sem)

    def zero_copy(e):
        start = pl.multiple_of(last_ref[e] * TMX, TMX)
        return pltpu.make_async_copy(zero_ref, xs_hbm.at[pl.ds(start, TMX)], zsem)

    @pl.when(i == 0)
    def _():
        zero_ref[...] = jnp.zeros_like(zero_ref)

        def zstart(e, c):
            zero_copy(e).start()
            return c

        def zwait(e, c):
            zero_copy(e).wait()
            return c

        lax.fori_loop(0, N_EXPERTS, zstart, 0)
        lax.fori_loop(0, N_EXPERTS, zwait, 0)

    def start(j, c):
        row_copy(j, pos_ref[0, j]).start()
        row_copy(j, pos_ref[1, j]).start()
        return c

    def wait(j, c):
        row_copy(j, pos_ref[0, j]).wait()
        row_copy(j, pos_ref[1, j]).wait()
        return c

    lax.fori_loop(0, DISPATCH_CHUNK, start, 0, unroll=8)
    lax.fori_loop(0, DISPATCH_CHUNK, wait, 0, unroll=8)


def _dispatch(pos3, last_tile, hp, n_rows):
    t, w = hp.shape
    return pl.pallas_call(
        _dispatch_kernel,
        grid=(t // DISPATCH_CHUNK,),
        in_specs=[
            pl.BlockSpec((None, 2, DISPATCH_CHUNK), lambda i: (i, 0, 0), memory_space=pltpu.SMEM),
            pl.BlockSpec(memory_space=pltpu.SMEM),
            pl.BlockSpec((DISPATCH_CHUNK, w), lambda i: (i, 0)),
        ],
        out_specs=pl.BlockSpec(memory_space=pl.ANY),
        out_shape=jax.ShapeDtypeStruct((n_rows, w), U32),
        scratch_shapes=[pltpu.VMEM((TMX, w), U32),
                        pltpu.SemaphoreType.DMA(()),
                        pltpu.SemaphoreType.DMA(())],
        compiler_params=_cparams(("arbitrary",)),
        name="dispatch",
    )(pos3, last_tile, hp)


def _experts_kernel(ts_ref, nt_ref, nu_ref, x_hbm, wg_ref, wu_ref, wd_ref, y_hbm,
                    wgu_s, wd_s, xbuf, ybuf, xsem, ysem):
    e = pl.program_id(0)
    d = wg_ref.shape[1]
    n_used = nu_ref[0]

    def x_copy(g, slot):
        rows = pl.ds(pl.multiple_of(g * TMX, TMX), TMX)
        return pltpu.make_async_copy(x_hbm.at[rows], xbuf.at[slot], xsem.at[slot])

    def y_copy(g, slot):
        rows = pl.ds(pl.multiple_of(g * TMX, TMX), TMX)
        return pltpu.make_async_copy(ybuf.at[slot], y_hbm.at[rows], ysem.at[slot])

    @pl.when(jnp.logical_and(e == 0, n_used > 0))
    def _():
        x_copy(0, 0).start()

    @pl.when(nt_ref[e] > 0)
    def _():
        wgu_s[:, :D_EXPERT] = wg_ref[0].astype(BF16)
        wgu_s[:, D_EXPERT:] = wu_ref[0].astype(BF16)
        wd_s[...] = wd_ref[0].astype(BF16)

    def tile(t, carry):
        g = ts_ref[e] + t
        slot = g % 2
        x_copy(g, slot).wait()

        @pl.when(g + 1 < n_used)
        def _():
            x_copy(g + 1, 1 - slot).start()

        @pl.when(g >= 2)
        def _():
            y_copy(g - 2, slot).wait()

        lo, hi = _unpack_bf16_pair(xbuf[slot])
        gu = jnp.dot(lo.astype(BF16), wgu_s[:d // 2, :], preferred_element_type=F32)
        gu = gu + jnp.dot(hi.astype(BF16), wgu_s[d // 2:, :], preferred_element_type=F32)
        gate = gu[:, :D_EXPERT]
        hmid = (gate * jax.nn.sigmoid(gate) * gu[:, D_EXPERT:]).astype(BF16)
        y = jnp.dot(hmid, wd_s[...], preferred_element_type=F32)
        ybuf[slot] = _pack_bf16_pair(y)
        y_copy(g, slot).start()
        return carry

    lax.fori_loop(0, nt_ref[e], tile, 0)

    @pl.when(e == pl.num_programs(0) - 1)
    def _():
        @pl.when(n_used >= 2)
        def _():
            y_copy(n_used - 2, (n_used - 2) % 2).wait()

        @pl.when(n_used >= 1)
        def _():
            y_copy(n_used - 1, (n_used - 1) % 2).wait()


def _experts(tile_start, n_tiles_e, n_used, xs, w_gate, w_up, w_down):
    n_rows, w = xs.shape
    d = w_gate.shape[1]

    def w_map(e, ts, nt, nu):
        return (e, 0, 0)

    grid_spec = pltpu.PrefetchScalarGridSpec(
        num_scalar_prefetch=3,
        grid=(N_EXPERTS,),
        in_specs=[
            pl.BlockSpec(memory_space=pl.ANY),
            pl.BlockSpec((1, d, D_EXPERT), w_map),
            pl.BlockSpec((1, d, D_EXPERT), w_map),
            pl.BlockSpec((1, D_EXPERT, d), w_map),
        ],
        out_specs=pl.BlockSpec(memory_space=pl.ANY),
        scratch_shapes=[pltpu.VMEM((d, 2 * D_EXPERT), BF16),
                        pltpu.VMEM((D_EXPERT, d), BF16),
                        pltpu.VMEM((2, TMX, w), U32),
                        pltpu.VMEM((2, TMX, w), U32),
                        pltpu.SemaphoreType.DMA((2,)),
                        pltpu.SemaphoreType.DMA((2,))],
    )
    return pl.pallas_call(
        _experts_kernel,
        grid_spec=grid_spec,
        out_shape=jax.ShapeDtypeStruct((n_rows, w), U32),
        compiler_params=_cparams(("arbitrary",)),
        name="experts",
    )(tile_start, n_tiles_e, n_used, xs, w_gate, w_up, w_down)


def _combine_kernel(pos_ref, posn_ref, y_hbm, h1_ref, wt_ref, g2_ref, b2_ref, o_ref, ybuf, sem):
    i = pl.program_id(0)
    n = pl.num_programs(0)
    slot = i % 2

    def row_copy(src, sl, which, j):
        return pltpu.make_async_copy(y_hbm.at[pl.ds(src, 1)],
                                     ybuf.at[sl, which, pl.ds(j, 1)], sem.at[sl])

    def issue(p_ref, sl):
        def body(j, c):
            row_copy(p_ref[0, j], sl, 0, j).start()
            row_copy(p_ref[1, j], sl, 1, j).start()
            return c
        lax.fori_loop(0, TM_OUT, body, 0, unroll=8)

    @pl.when(i == 0)
    def _():
        issue(pos_ref, 0)

    @pl.when(i + 1 < n)
    def _():
        issue(posn_ref, 1 - slot)

    def wait(j, c):
        row_copy(pos_ref[0, j], slot, 0, j).wait()
        row_copy(pos_ref[1, j], slot, 1, j).wait()
        return c

    lax.fori_loop(0, TM_OUT, wait, 0, unroll=8)

    lo1, hi1 = _unpack_bf16_pair(ybuf[slot, 0])
    lo2, hi2 = _unpack_bf16_pair(ybuf[slot, 1])
    w1 = wt_ref[:, 0:1]
    w2 = wt_ref[:, 1:2]
    ffn = jnp.concatenate([w1 * lo1 + w2 * lo2, w1 * hi1 + w2 * hi2], axis=1)
    o_ref[...] = _layernorm(DN_ALPHA * h1_ref[...] + ffn, g2_ref[...], b2_ref[...])


def _combine(pos3, ys, h1, wtok, g2, b2):
    t, d = h1.shape
    w = ys.shape[1]
    n = t // TM_OUT
    return pl.pallas_call(
        _combine_kernel,
        grid=(n,),
        in_specs=[
            pl.BlockSpec((None, 2, TM_OUT), lambda i: (i, 0, 0), memory_space=pltpu.SMEM),
            pl.BlockSpec((None, 2, TM_OUT), lambda i: (jnp.minimum(i + 1, n - 1), 0, 0),
                         memory_space=pltpu.SMEM),
            pl.BlockSpec(memory_space=pl.ANY),
            pl.BlockSpec((TM_OUT, d), lambda i: (i, 0)),
            pl.BlockSpec((TM_OUT, ROUTE_ROWS), lambda i: (i, 0)),
            pl.BlockSpec((1, d), lambda i: (0, 0)),
            pl.BlockSpec((1, d), lambda i: (0, 0)),
        ],
        out_specs=pl.BlockSpec((TM_OUT, d), lambda i: (i, 0)),
        out_shape=jax.ShapeDtypeStruct((t, d), F32),
        scratch_shapes=[pltpu.VMEM((2, 2, TM_OUT, w), U32),
                        pltpu.SemaphoreType.DMA((2,))],
        compiler_params=_cparams(("arbitrary",)),
        name="combine",
    )(pos3, pos3, ys, h1, wtok, g2, b2)


def _lambda_init(layer):
    return 0.8 - 0.6 * math.exp(-0.3 * layer)


def kernel(x, w_in, w_out, ln1_g, ln1_b, ln2_g, ln2_b, rel_bias, lam_q1, lam_k1, lam_q2, lam_k2,
           subln_g, sg_ln_g, sg_ln_b, sg_w, sg_b, w_router_group, b_router_group,
           w_router_expert, b_router_expert, w_exp_gate, w_exp_up, w_exp_down):
    b, s, d = x.shape
    t = b * s
    assert w_in.shape[0] == DEPTH == 1
    l = 0
    lam_init = _lambda_init(l)
    x2 = x.reshape(t, d)

    proj = _inproj(x2, w_in[l])

    lam = (jnp.exp(jnp.sum(lam_q1[l].astype(F32) * lam_k1[l].astype(F32)))
           - jnp.exp(jnp.sum(lam_q2[l].astype(F32) * lam_k2[l].astype(F32))) + lam_init)
    a = _attention(proj.reshape(b, s, -1), _bias_sequences(rel_bias, s), lam.reshape(1),
                   subln_g[l].reshape(1, -1), lam_init)

    wr = jnp.concatenate([w_router_group[l].T,
                          jnp.zeros((NR_EXPERT0 - N_GROUPS, d), F32),
                          jnp.transpose(w_router_expert[l], (0, 2, 1)).reshape(N_EXPERTS, d)],
                         axis=0)
    br = jnp.concatenate([b_router_group[l], jnp.zeros((NR_EXPERT0 - N_GROUPS,), F32),
                          b_router_expert[l].reshape(-1)]).reshape(NR_PAD, 1)
    h1, hp, ri, rw, cnt = _mix(
        a.reshape(t, -1), proj, x2, w_out[l].astype(BF16), sg_w[l].astype(BF16),
        sg_b[l].reshape(SG_GROUPS, SG_CHUNK, 1), sg_ln_g[l], sg_ln_b[l],
        ln1_g[l].reshape(1, d), ln1_b[l].reshape(1, d), wr, br)

    counts = cnt[:, 0]
    tiles_e = (counts + TMX - 1) // TMX
    tile_end = jnp.cumsum(tiles_e)
    tile_start = tile_end - tiles_e
    n_used = tile_end[-1:]
    max_tiles = (2 * t) // TMX + N_EXPERTS
    last_tile = jnp.maximum(tile_end - 1, 0).astype(I32)
    row_start = (tile_start * TMX).astype(I32)
    eids = jnp.arange(N_EXPERTS, dtype=I32)[:, None, None]
    pos = jnp.sum(jnp.where(ri[0:2][None] == eids, row_start[:, None, None], 0), axis=0) + ri[2:4]
    pos3 = pos.reshape(2, t // TM_OUT, TM_OUT).transpose(1, 0, 2)

    xs = _dispatch(pos3, last_tile, hp, max_tiles * TMX)
    ys = _experts(tile_start.astype(I32), tiles_e.astype(I32), n_used.astype(I32), xs,
                  w_exp_gate[l], w_exp_up[l], w_exp_down[l])
    out = _combine(pos3, ys, h1, rw.T, ln2_g[l].reshape(1, d), ln2_b[l].reshape(1, d))
    return out.reshape(b, s, d)
```

```python
import functools
import math

import jax
import jax.numpy as jnp
from jax import lax
from jax.experimental import pallas as pl
from jax.experimental.pallas import tpu as pltpu

F32 = jnp.float32
BF16 = jnp.bfloat16
U32 = jnp.uint32
I32 = jnp.int32

ATT_HEADS = 8
ATT_QK_DIM = 64
ATT_V_DIM = 128
ATT_WIDTH = ATT_HEADS * ATT_V_DIM
SG_GROUPS = 8
SG_DIM = 128
SG_WIDTH = SG_GROUPS * SG_DIM
SG_CHUNK = 128
REL_BUCKETS = 32
REL_MAX_DIST = 128
N_GROUPS = 4
EXPERTS_PER_GROUP = 8
N_EXPERTS = N_GROUPS * EXPERTS_PER_GROUP
D_EXPERT = 256
DEPTH = 1
DN_ALPHA = (2.0 * DEPTH) ** 0.25
LN_EPS = 1e-5
LOG2E = math.log2(math.e)

LANES = 128
SUBLANES = 8
VMEM_LIMIT_BYTES = 56 * 1024 * 1024

TM_IN = 1024
TN_IN = 1024
QB = 256
KC = 256
N_BIAS_TILES = 5
TM_MIX = 512
TMX = 256
TM_OUT = 256
ROUTE_ROWS = 8
NR_EXPERT0 = 8
NR_PAD = NR_EXPERT0 + 32


def _cparams(sem, flags=None):
    return pltpu.CompilerParams(dimension_semantics=sem, vmem_limit_bytes=VMEM_LIMIT_BYTES,
                                flags=flags)


def _inproj_kernel(x_ref, w_ref, o_ref, xb_ref, *, n_q, n_lin, q_scale):
    j = pl.program_id(1)

    @pl.when(j == 0)
    def _():
        xb_ref[...] = x_ref[...].astype(BF16)

    acc = jnp.dot(xb_ref[...], w_ref[...].astype(BF16), preferred_element_type=F32)

    @pl.when(j < n_q)
    def _():
        o_ref[...] = (acc * q_scale).astype(BF16)

    @pl.when(jnp.logical_and(j >= n_q, j < n_lin))
    def _():
        o_ref[...] = acc.astype(BF16)

    @pl.when(j >= n_lin)
    def _():
        o_ref[...] = jax.nn.gelu(acc).astype(BF16)


def _inproj(x2, w_in):
    t, d = x2.shape
    n = w_in.shape[1]
    n_q = (ATT_HEADS * 2 * ATT_QK_DIM) // TN_IN
    n_lin = (2 * ATT_HEADS * 2 * ATT_QK_DIM + ATT_WIDTH) // TN_IN
    kern = functools.partial(_inproj_kernel, n_q=n_q, n_lin=n_lin,
                             q_scale=(ATT_QK_DIM ** -0.5) * LOG2E)
    return pl.pallas_call(
        kern,
        grid=(t // TM_IN, n // TN_IN),
        in_specs=[pl.BlockSpec((TM_IN, d), lambda i, j: (i, 0)),
                  pl.BlockSpec((d, TN_IN), lambda i, j: (0, j))],
        out_specs=pl.BlockSpec((TM_IN, TN_IN), lambda i, j: (i, j)),
        out_shape=jax.ShapeDtypeStruct((t, n), BF16),
        scratch_shapes=[pltpu.VMEM((TM_IN, d), BF16)],
        compiler_params=_cparams(("arbitrary", "arbitrary")),
        name="inproj",
    )(x2, w_in)


def _attn_kernel(lam_ref, q_ref, k_ref, v_ref, bseq_ref, g_ref, o_ref, vaug_ref, bias_ref,
                 *, n_kc, out_scale):
    qi = pl.program_id(2)

    @pl.when(qi == 0)
    def _():
        vaug_ref[:, :ATT_V_DIM] = v_ref[0]
        vaug_ref[:, ATT_V_DIM:] = jnp.ones((vaug_ref.shape[0], ATT_V_DIM), BF16)
        for t in range(N_BIAS_TILES):
            seq = jnp.broadcast_to(bseq_ref[0, t:t + 1, :], (QB, 2 * KC))
            bias_ref[t] = pltpu.roll(seq, 0, 1, stride=1, stride_axis=0)[:, :KC]

    q = q_ref[0]
    lane = lax.broadcasted_iota(I32, q.shape, 1)
    zero = jnp.zeros_like(q)
    qq = jnp.concatenate([jnp.where(lane < ATT_QK_DIM, q, zero),
                          jnp.where(lane >= ATT_QK_DIM, q, zero)], axis=0)

    m = jnp.full((2 * QB, 1), -jnp.inf, F32)
    acc = jnp.zeros((2 * QB, 2 * ATT_V_DIM), F32)
    for kc in range(n_kc):
        kblk = k_ref[0, kc * KC:(kc + 1) * KC, :]
        s = lax.dot_general(qq, kblk, (((1,), (1,)), ((), ())),
                            preferred_element_type=F32)
        b = bias_ref[jnp.clip(kc - qi, -2, 2) + 2]
        s = s + jnp.concatenate([b, b], axis=0)
        m_new = jnp.maximum(m, jnp.max(s, axis=1, keepdims=True))
        p = jnp.exp2(s - m_new).astype(BF16)
        acc = acc * jnp.exp2(m - m_new) + jnp.dot(
            p, vaug_ref[kc * KC:(kc + 1) * KC, :], preferred_element_type=F32)
        m = m_new
    lam = lam_ref[0]
    o0 = acc[:QB, :ATT_V_DIM] / acc[:QB, ATT_V_DIM:ATT_V_DIM + 1]
    o1 = acc[QB:, :ATT_V_DIM] / acc[QB:, ATT_V_DIM:ATT_V_DIM + 1]
    of = o0 - lam * o1
    of = of * lax.rsqrt(jnp.mean(of * of, axis=-1, keepdims=True) + LN_EPS)
    o_ref[0] = (of * (g_ref[...] * out_scale)).astype(o_ref.dtype)


def _attention(proj3, bias_seq, lam, subln_g, lam_init):
    b, s, _ = proj3.shape
    n_kc = s // KC
    h_off_k = ATT_HEADS
    h_off_v = 2 * ATT_HEADS
    kern = functools.partial(_attn_kernel, n_kc=n_kc, out_scale=1.0 - lam_init)
    return pl.pallas_call(
        kern,
        grid=(b, ATT_HEADS, s // QB),
        in_specs=[
            pl.BlockSpec(memory_space=pltpu.SMEM),
            pl.BlockSpec((1, QB, LANES), lambda bi, h, qi: (bi, qi, h)),
            pl.BlockSpec((1, s, LANES), lambda bi, h, qi: (bi, 0, h_off_k + h)),
            pl.BlockSpec((1, s, LANES), lambda bi, h, qi: (bi, 0, h_off_v + h)),
            pl.BlockSpec((1, N_BIAS_TILES, 2 * KC), lambda bi, h, qi: (h, 0, 0)),
            pl.BlockSpec((1, ATT_V_DIM), lambda bi, h, qi: (0, 0)),
        ],
        out_specs=pl.BlockSpec((1, QB, ATT_V_DIM), lambda bi, h, qi: (bi, qi, h)),
        out_shape=jax.ShapeDtypeStruct((b, s, ATT_WIDTH), BF16),
        scratch_shapes=[pltpu.VMEM((s, 2 * ATT_V_DIM), BF16),
                        pltpu.VMEM((N_BIAS_TILES, QB, KC), F32)],
        compiler_params=_cparams(("arbitrary", "arbitrary", "arbitrary")),
        name="attn",
    )(lam, proj3, proj3, proj3, bias_seq, subln_g)


def _rel_bucket(rel):
    half = REL_BUCKETS // 2
    max_exact = half // 2
    ret = jnp.where(rel > 0, half, 0)
    n = jnp.abs(rel)
    nf = jnp.maximum(n, 1).astype(F32)
    large = max_exact + (jnp.log(nf / max_exact) / math.log(REL_MAX_DIST / max_exact)
                         * (half - max_exact)).astype(I32)
    large = jnp.minimum(large, half - 1)
    return ret + jnp.where(n < max_exact, n, large)


def _bias_sequences(rel_bias, s):
    assert REL_MAX_DIST <= KC and QB == KC
    period = 2 * KC
    j = jnp.arange(period, dtype=I32)
    off = jnp.where(j < KC, j, j - period)
    d = jnp.arange(-2, 3, dtype=I32)[:, None]
    rel = jnp.clip(d * KC + off[None, :], -(s - 1), s - 1)
    return jnp.transpose(rel_bias[_rel_bucket(rel)], (2, 0, 1)).astype(F32) * LOG2E


def _layernorm(v, g, b):
    mu = jnp.mean(v, axis=-1, keepdims=True)
    var = jnp.mean(jnp.square(v - mu), axis=-1, keepdims=True)
    return (v - mu) * lax.rsqrt(var + LN_EPS) * g + b


def _pack_bf16_pair(v):
    n = v.shape[1] // 2
    bits = lax.bitcast_convert_type(v.astype(BF16).astype(F32), U32)
    return (bits[:, :n] >> 16) | (bits[:, n:] & jnp.uint32(0xFFFF0000))


def _unpack_bf16_pair(w):
    lo = lax.bitcast_convert_type(w << 16, F32)
    hi = lax.bitcast_convert_type(w & jnp.uint32(0xFFFF0000), F32)
    return lo, hi


def _first_argmax(vals, nrows):
    mx = jnp.max(vals, axis=0, keepdims=True)
    row = lax.broadcasted_iota(I32, vals.shape, 0)
    idx = jnp.min(jnp.where(vals == mx, row, nrows), axis=0, keepdims=True)
    return mx, idx


def _mix_kernel(a_ref, u_ref, vg_ref, x_ref, wout_ref, sgw_ref, sgb_ref, sgg_ref, sgbeta_ref,
                g1_ref, b1_ref, wr_ref, br_ref,
                h1_ref, hp_ref, ri_ref, rw_ref, cnt_ref, run_ref):
    i = pl.program_id(0)

    @pl.when(i == 0)
    def _():
        run_ref[...] = jnp.zeros_like(run_ref)

    n_chunks = TM_MIX // SG_CHUNK
    rows = []
    for c in range(n_chunks):
        cols = []
        for g in range(SG_GROUPS):
            rs = slice(c * SG_CHUNK, (c + 1) * SG_CHUNK)
            cs = slice(g * SG_DIM, (g + 1) * SG_DIM)
            vt = vg_ref[rs, cs].astype(F32)
            vn = _layernorm(vt, sgg_ref[g:g + 1, :], sgbeta_ref[g:g + 1, :])
            mixed = jnp.dot(sgw_ref[g], vn.astype(BF16), preferred_element_type=F32) + sgb_ref[g]
            cols.append((u_ref[rs, cs].astype(F32) * mixed).astype(BF16))
        rows.append(jnp.concatenate(cols, axis=1))
    sgate = jnp.concatenate(rows, axis=0)

    mix = jnp.dot(a_ref[...], wout_ref[:ATT_WIDTH, :], preferred_element_type=F32)
    mix = mix + jnp.dot(sgate, wout_ref[ATT_WIDTH:, :], preferred_element_type=F32)
    h1 = _layernorm(DN_ALPHA * x_ref[...] + mix, g1_ref[...], b1_ref[...])
    h1_ref[...] = h1
    hp_ref[...] = _pack_bf16_pair(h1)

    lt = lax.dot_general(wr_ref[...], h1, (((1,), (1,)), ((), ())),
                         preferred_element_type=F32, precision=lax.Precision.HIGHEST)
    lt = lt + br_ref[...]
    g_logits = lt[0:N_GROUPS]
    gmax, g_idx = _first_argmax(g_logits, N_GROUPS)
    g_gate = 1.0 / jnp.sum(jnp.exp(g_logits - gmax), axis=0, keepdims=True)
    e_logits = jnp.zeros((EXPERTS_PER_GROUP, TM_MIX), F32)
    for g in range(N_GROUPS):
        lo = NR_EXPERT0 + g * EXPERTS_PER_GROUP
        e_logits = jnp.where(g_idx == g, lt[lo:lo + EXPERTS_PER_GROUP], e_logits)
    v1, i1 = _first_argmax(e_logits, EXPERTS_PER_GROUP)
    erow = lax.broadcasted_iota(I32, e_logits.shape, 0)
    v2, i2 = _first_argmax(jnp.where(erow == i1, -jnp.inf, e_logits), EXPERTS_PER_GROUP)
    t = jnp.exp(v2 - v1)
    w1 = g_gate / (1.0 + t)
    w2 = g_gate * t / (1.0 + t)
    e1 = g_idx * EXPERTS_PER_GROUP + i1
    e2 = g_idx * EXPERTS_PER_GROUP + i2

    xrow = lax.broadcasted_iota(I32, (N_EXPERTS, TM_MIX), 0)
    oh1 = xrow == e1
    oh2 = xrow == e2
    oh = jnp.where(jnp.logical_or(oh1, oh2), 1.0, 0.0)
    tr = lax.broadcasted_iota(I32, (TM_MIX, TM_MIX), 0)
    tc = lax.broadcasted_iota(I32, (TM_MIX, TM_MIX), 1)
    before = jnp.where(tr < tc, 1.0, 0.0).astype(BF16)
    cum = jnp.dot(oh.astype(BF16), before, preferred_element_type=F32)
    base = cum + run_ref[:, 0:1]
    rank1 = jnp.sum(jnp.where(oh1, base, 0.0), axis=0, keepdims=True).astype(I32)
    rank2 = jnp.sum(jnp.where(oh2, base, 0.0), axis=0, keepdims=True).astype(I32)
    run_ref[...] = run_ref[...] + jnp.sum(oh, axis=1, keepdims=True)
    cnt_ref[...] = run_ref[...].astype(I32)

    zi = jnp.zeros((ROUTE_ROWS - 4, TM_MIX), I32)
    ri_ref[...] = jnp.concatenate([e1, e2, rank1, rank2, zi], axis=0)
    zw = jnp.zeros((ROUTE_ROWS - 2, TM_MIX), F32)
    rw_ref[...] = jnp.concatenate([w1, w2, zw], axis=0)


def _mix(a2, proj2, x2, wout_b, sgw_b, sgb, sgg, sgbeta, g1, b1, wr, br):
    t, d = x2.shape
    u_blk = (proj2.shape[1] - 2 * SG_WIDTH) // SG_WIDTH
    const = lambda *shape: pl.BlockSpec(shape, lambda i: (0,) * len(shape))
    return pl.pallas_call(
        _mix_kernel,
        grid=(t // TM_MIX,),
        in_specs=[
            pl.BlockSpec((TM_MIX, ATT_WIDTH), lambda i: (i, 0)),
            pl.BlockSpec((TM_MIX, SG_WIDTH), lambda i: (i, u_blk)),
            pl.BlockSpec((TM_MIX, SG_WIDTH), lambda i: (i, u_blk + 1)),
            pl.BlockSpec((TM_MIX, d), lambda i: (i, 0)),
            pl.BlockSpec((d, d), lambda i: (0, 0), pipeline_mode=pl.Buffered(1)),
            const(SG_GROUPS, SG_CHUNK, SG_CHUNK),
            const(SG_GROUPS, SG_CHUNK, 1),
            const(SG_GROUPS, SG_DIM),
            const(SG_GROUPS, SG_DIM),
            const(1, d),
            const(1, d),
            const(NR_PAD, d),
            const(NR_PAD, 1),
        ],
        out_specs=[
            pl.BlockSpec((TM_MIX, d), lambda i: (i, 0)),
            pl.BlockSpec((TM_MIX, d // 2), lambda i: (i, 0)),
            pl.BlockSpec((ROUTE_ROWS, TM_MIX), lambda i: (0, i)),
            pl.BlockSpec((ROUTE_ROWS, TM_MIX), lambda i: (0, i)),
            const(N_EXPERTS, LANES),
        ],
        out_shape=[
            jax.ShapeDtypeStruct((t, d), F32),
            jax.ShapeDtypeStruct((t, d // 2), U32),
            jax.ShapeDtypeStruct((ROUTE_ROWS, t), I32),
            jax.ShapeDtypeStruct((ROUTE_ROWS, t), F32),
            jax.ShapeDtypeStruct((N_EXPERTS, LANES), I32),
        ],
        scratch_shapes=[pltpu.VMEM((N_EXPERTS, LANES), F32)],
        compiler_params=_cparams(("arbitrary",)),
        name="mix",
    )(a2, proj2, proj2, x2, wout_b, sgw_b, sgb, sgg, sgbeta, g1, b1, wr, br)


DISPATCH_CHUNK = 256


def _dispatch_kernel(pos_ref, last_ref, hp_ref, xs_hbm, zero_ref, zsem, sem):
    i = pl.program_id(0)

    def row_copy(j, dst):
        return pltpu.make_async_copy(hp_ref.at[pl.ds(j, 1)], xs_hbm.at[pl.ds(dst, 1)], sem)

    def zero_copy(e):
        start = pl.multiple_of(last_ref[e] * TMX, TMX)
        return pltpu.make_async_copy(zero_ref, xs_hbm.at[pl.ds(start, TMX)], zsem)

    @pl.when(i == 0)
    def _():
        zero_ref[...] = jnp.zeros_like(zero_ref)

        def zstart(e, c):
            zero_copy(e).start()
            return c

        def zwait(e, c):
            zero_copy(e).wait()
            return c

        lax.fori_loop(0, N_EXPERTS, zstart, 0)
        lax.fori_loop(0, N_EXPERTS, zwait, 0)

    def start(j, c):
        row_copy(j, pos_ref[0, j]).start()
        row_copy(j, pos_ref[1, j]).start()
        return c

    def wait(j, c):
        row_copy(j, pos_ref[0, j]).wait()
        row_copy(j, pos_ref[1, j]).wait()
        return c

    lax.fori_loop(0, DISPATCH_CHUNK, start, 0, unroll=8)
    lax.fori_loop(0, DISPATCH_CHUNK, wait, 0, unroll=8)


def _dispatch(pos3, last_tile, hp, n_rows):
    t, w = hp.shape
    return pl.pallas_call(
        _dispatch_kernel,
        grid=(t // DISPATCH_CHUNK,),
        in_specs=[
            pl.BlockSpec((None, 2, DISPATCH_CHUNK), lambda i: (i, 0, 0), memory_space=pltpu.SMEM),
            pl.BlockSpec(memory_space=pltpu.SMEM),
            pl.BlockSpec((DISPATCH_CHUNK, w), lambda i: (i, 0)),
        ],
        out_specs=pl.BlockSpec(memory_space=pl.ANY),
        out_shape=jax.ShapeDtypeStruct((n_rows, w), U32),
        scratch_shapes=[pltpu.VMEM((TMX, w), U32),
                        pltpu.SemaphoreType.DMA(()),
                        pltpu.SemaphoreType.DMA(())],
        compiler_params=_cparams(("arbitrary",)),
        name="dispatch",
    )(pos3, last_tile, hp)


def _experts_kernel(ts_ref, nt_ref, nu_ref, x_hbm, wg_ref, wu_ref, wd_ref, y_hbm,
                    wgu_s, wd_s, xbuf, ybuf, xsem, ysem):
    e = pl.program_id(0)
    d = wg_ref.shape[1]
    n_used = nu_ref[0]

    def x_copy(g, slot):
        rows = pl.ds(pl.multiple_of(g * TMX, TMX), TMX)
        return pltpu.make_async_copy(x_hbm.at[rows], xbuf.at[slot], xsem.at[slot])

    def y_copy(g, slot):
        rows = pl.ds(pl.multiple_of(g * TMX, TMX), TMX)
        return pltpu.make_async_copy(ybuf.at[slot], y_hbm.at[rows], ysem.at[slot])

    @pl.when(jnp.logical_and(e == 0, n_used > 0))
    def _():
        x_copy(0, 0).start(priority=1)

    @pl.when(nt_ref[e] > 0)
    def _():
        wgu_s[:, :D_EXPERT] = wg_ref[0].astype(BF16)
        wgu_s[:, D_EXPERT:] = wu_ref[0].astype(BF16)
        wd_s[...] = wd_ref[0].astype(BF16)

    def tile(t, carry):
        g = ts_ref[e] + t
        slot = g % 2
        x_copy(g, slot).wait()

        @pl.when(g + 1 < n_used)
        def _():
            x_copy(g + 1, 1 - slot).start(priority=1)

        @pl.when(g >= 2)
        def _():
            y_copy(g - 2, slot).wait()

        lo, hi = _unpack_bf16_pair(xbuf[slot])
        gu = jnp.dot(lo.astype(BF16), wgu_s[:d // 2, :], preferred_element_type=F32)
        gu = gu + jnp.dot(hi.astype(BF16), wgu_s[d // 2:, :], preferred_element_type=F32)
        gate = gu[:, :D_EXPERT]
        hmid = (gate * jax.nn.sigmoid(gate) * gu[:, D_EXPERT:]).astype(BF16)
        y = jnp.dot(hmid, wd_s[...], preferred_element_type=F32)
        ybuf[slot] = _pack_bf16_pair(y)
        y_copy(g, slot).start(priority=1)
        return carry

    lax.fori_loop(0, nt_ref[e], tile, 0)

    @pl.when(e == pl.num_programs(0) - 1)
    def _():
        @pl.when(n_used >= 2)
        def _():
            y_copy(n_used - 2, (n_used - 2) % 2).wait()

        @pl.when(n_used >= 1)
        def _():
            y_copy(n_used - 1, (n_used - 1) % 2).wait()


def _experts(tile_start, n_tiles_e, n_used, xs, w_gate, w_up, w_down):
    n_rows, w = xs.shape
    d = w_gate.shape[1]

    def w_map(e, ts, nt, nu):
        return (e, 0, 0)

    grid_spec = pltpu.PrefetchScalarGridSpec(
        num_scalar_prefetch=3,
        grid=(N_EXPERTS,),
        in_specs=[
            pl.BlockSpec(memory_space=pl.ANY),
            pl.BlockSpec((1, d, D_EXPERT), w_map),
            pl.BlockSpec((1, d, D_EXPERT), w_map),
            pl.BlockSpec((1, D_EXPERT, d), w_map),
        ],
        out_specs=pl.BlockSpec(memory_space=pl.ANY),
        scratch_shapes=[pltpu.VMEM((d, 2 * D_EXPERT), BF16),
                        pltpu.VMEM((D_EXPERT, d), BF16),
                        pltpu.VMEM((2, TMX, w), U32),
                        pltpu.VMEM((2, TMX, w), U32),
                        pltpu.SemaphoreType.DMA((2,)),
                        pltpu.SemaphoreType.DMA((2,))],
    )
    return pl.pallas_call(
        _experts_kernel,
        grid_spec=grid_spec,
        out_shape=jax.ShapeDtypeStruct((n_rows, w), U32),
        compiler_params=_cparams(("arbitrary",)),
        name="experts",
    )(tile_start, n_tiles_e, n_used, xs, w_gate, w_up, w_down)


def _combine_kernel(pos_ref, posn_ref, y_hbm, h1_ref, wt_ref, g2_ref, b2_ref, o_ref, ybuf, sem):
    i = pl.program_id(0)
    n = pl.num_programs(0)
    slot = i % 2

    def row_copy(src, sl, which, j):
        return pltpu.make_async_copy(y_hbm.at[pl.ds(src, 1)],
                                     ybuf.at[sl, which, pl.ds(j, 1)], sem.at[sl])

    def issue(p_ref, sl):
        def body(j, c):
            row_copy(p_ref[0, j], sl, 0, j).start()
            row_copy(p_ref[1, j], sl, 1, j).start()
            return c
        lax.fori_loop(0, TM_OUT, body, 0, unroll=8)

    @pl.when(i == 0)
    def _():
        issue(pos_ref, 0)

    @pl.when(i + 1 < n)
    def _():
        issue(posn_ref, 1 - slot)

    def wait(j, c):
        row_copy(pos_ref[0, j], slot, 0, j).wait()
        row_copy(pos_ref[1, j], slot, 1, j).wait()
        return c

    lax.fori_loop(0, TM_OUT, wait, 0, unroll=8)

    lo1, hi1 = _unpack_bf16_pair(ybuf[slot, 0])
    lo2, hi2 = _unpack_bf16_pair(ybuf[slot, 1])
    w1 = wt_ref[:, 0:1]
    w2 = wt_ref[:, 1:2]
    ffn = jnp.concatenate([w1 * lo1 + w2 * lo2, w1 * hi1 + w2 * hi2], axis=1)
    o_ref[...] = _layernorm(DN_ALPHA * h1_ref[...] + ffn, g2_ref[...], b2_ref[...])


def _combine(pos3, ys, h1, wtok, g2, b2):
    t, d = h1.shape
    w = ys.shape[1]
    n = t // TM_OUT
    return pl.pallas_call(
        _combine_kernel,
        grid=(n,),
        in_specs=[
            pl.BlockSpec((None, 2, TM_OUT), lambda i: (i, 0, 0), memory_space=pltpu.SMEM),
            pl.BlockSpec((None, 2, TM_OUT), lambda i: (jnp.minimum(i + 1, n - 1), 0, 0),
                         memory_space=pltpu.SMEM),
            pl.BlockSpec(memory_space=pl.ANY),
            pl.BlockSpec((TM_OUT, d), lambda i: (i, 0)),
            pl.BlockSpec((TM_OUT, ROUTE_ROWS), lambda i: (i, 0)),
            pl.BlockSpec((1, d), lambda i: (0, 0)),
            pl.BlockSpec((1, d), lambda i: (0, 0)),
        ],
        out_specs=pl.BlockSpec((TM_OUT, d), lambda i: (i, 0)),
        out_shape=jax.ShapeDtypeStruct((t, d), F32),
        scratch_shapes=[pltpu.VMEM((2, 2, TM_OUT, w), U32),
                        pltpu.SemaphoreType.DMA((2,))],
        compiler_params=_cparams(("arbitrary",)),
        name="combine",
    )(pos3, pos3, ys, h1, wtok, g2, b2)


def _lambda_init(layer):
    return 0.8 - 0.6 * math.exp(-0.3 * layer)


def kernel(x, w_in, w_out, ln1_g, ln1_b, ln2_g, ln2_b, rel_bias, lam_q1, lam_k1, lam_q2, lam_k2,
           subln_g, sg_ln_g, sg_ln_b, sg_w, sg_b, w_router_group, b_router_group,
           w_router_expert, b_router_expert, w_exp_gate, w_exp_up, w_exp_down):
    b, s, d = x.shape
    t = b * s
    assert w_in.shape[0] == DEPTH == 1
    l = 0
    lam_init = _lambda_init(l)
    x2 = x.reshape(t, d)

    proj = _inproj(x2, w_in[l])

    lam = (jnp.exp(jnp.sum(lam_q1[l].astype(F32) * lam_k1[l].astype(F32)))
           - jnp.exp(jnp.sum(lam_q2[l].astype(F32) * lam_k2[l].astype(F32))) + lam_init)
    a = _attention(proj.reshape(b, s, -1), _bias_sequences(rel_bias, s), lam.reshape(1),
                   subln_g[l].reshape(1, -1), lam_init)

    wr = jnp.concatenate([w_router_group[l].T,
                          jnp.zeros((NR_EXPERT0 - N_GROUPS, d), F32),
                          jnp.transpose(w_router_expert[l], (0, 2, 1)).reshape(N_EXPERTS, d)],
                         axis=0)
    br = jnp.concatenate([b_router_group[l], jnp.zeros((NR_EXPERT0 - N_GROUPS,), F32),
                          b_router_expert[l].reshape(-1)]).reshape(NR_PAD, 1)
    h1, hp, ri, rw, cnt = _mix(
        a.reshape(t, -1), proj, x2, w_out[l].astype(BF16), sg_w[l].astype(BF16),
        sg_b[l].reshape(SG_GROUPS, SG_CHUNK, 1), sg_ln_g[l], sg_ln_b[l],
        ln1_g[l].reshape(1, d), ln1_b[l].reshape(1, d), wr, br)

    counts = cnt[:, 0]
    tiles_e = (counts + TMX - 1) // TMX
    tile_end = jnp.cumsum(tiles_e)
    tile_start = tile_end - tiles_e
    n_used = tile_end[-1:]
    max_tiles = (2 * t) // TMX + N_EXPERTS
    last_tile = jnp.maximum(tile_end - 1, 0).astype(I32)
    row_start = (tile_start * TMX).astype(I32)
    eids = jnp.arange(N_EXPERTS, dtype=I32)[:, None, None]
    pos = jnp.sum(jnp.where(ri[0:2][None] == eids, row_start[:, None, None], 0), axis=0) + ri[2:4]
    pos3 = pos.reshape(2, t // TM_OUT, TM_OUT).transpose(1, 0, 2)

    xs = _dispatch(pos3, last_tile, hp, max_tiles * TMX)
    ys = _experts(tile_start.astype(I32), tiles_e.astype(I32), n_used.astype(I32), xs,
                  w_exp_gate[l], w_exp_up[l], w_exp_down[l])
    out = _combine(pos3, ys, h1, rw.T, ln2_g[l].reshape(1, d), ln2_b[l].reshape(1, d))
    return out.reshape(b, s, d)
```

```python
import functools
import math

import jax
import jax.numpy as jnp
from jax import lax
from jax.experimental import pallas as pl
from jax.experimental.pallas import tpu as pltpu

F32 = jnp.float32
BF16 = jnp.bfloat16
U32 = jnp.uint32
I32 = jnp.int32

ATT_HEADS = 8
ATT_QK_DIM = 64
ATT_V_DIM = 128
ATT_WIDTH = ATT_HEADS * ATT_V_DIM
SG_GROUPS = 8
SG_DIM = 128
SG_WIDTH = SG_GROUPS * SG_DIM
SG_CHUNK = 128
REL_BUCKETS = 32
REL_MAX_DIST = 128
N_GROUPS = 4
EXPERTS_PER_GROUP = 8
N_EXPERTS = N_GROUPS * EXPERTS_PER_GROUP
D_EXPERT = 256
DEPTH = 1
DN_ALPHA = (2.0 * DEPTH) ** 0.25
LN_EPS = 1e-5
LOG2E = math.log2(math.e)

LANES = 128
SUBLANES = 8
VMEM_LIMIT_BYTES = 56 * 1024 * 1024

TM_IN = 1024
TN_IN = 1024
QB = 256
KC = 256
N_BIAS_TILES = 5
TM_MIX = 512
TMX = 256
TM_OUT = 256
ROUTE_ROWS = 8
NR_EXPERT0 = 8
NR_PAD = NR_EXPERT0 + 32


def _cparams(sem, flags=None):
    return pltpu.CompilerParams(dimension_semantics=sem, vmem_limit_bytes=VMEM_LIMIT_BYTES,
                                flags=flags)


def _inproj_kernel(x_ref, w_ref, o_ref, xb_ref, *, n_q, n_lin, q_scale):
    j = pl.program_id(1)

    @pl.when(j == 0)
    def _():
        xb_ref[...] = x_ref[...].astype(BF16)

    acc = jnp.dot(xb_ref[...], w_ref[...].astype(BF16), preferred_element_type=F32)

    @pl.when(j < n_q)
    def _():
        o_ref[...] = (acc * q_scale).astype(BF16)

    @pl.when(jnp.logical_and(j >= n_q, j < n_lin))
    def _():
        o_ref[...] = acc.astype(BF16)

    @pl.when(j >= n_lin)
    def _():
        o_ref[...] = jax.nn.gelu(acc).astype(BF16)


def _inproj(x2, w_in):
    t, d = x2.shape
    n = w_in.shape[1]
    n_q = (ATT_HEADS * 2 * ATT_QK_DIM) // TN_IN
    n_lin = (2 * ATT_HEADS * 2 * ATT_QK_DIM + ATT_WIDTH) // TN_IN
    kern = functools.partial(_inproj_kernel, n_q=n_q, n_lin=n_lin,
                             q_scale=(ATT_QK_DIM ** -0.5) * LOG2E)
    return pl.pallas_call(
        kern,
        grid=(t // TM_IN, n // TN_IN),
        in_specs=[pl.BlockSpec((TM_IN, d), lambda i, j: (i, 0)),
                  pl.BlockSpec((d, TN_IN), lambda i, j: (0, j))],
        out_specs=pl.BlockSpec((TM_IN, TN_IN), lambda i, j: (i, j)),
        out_shape=jax.ShapeDtypeStruct((t, n), BF16),
        scratch_shapes=[pltpu.VMEM((TM_IN, d), BF16)],
        compiler_params=_cparams(("arbitrary", "arbitrary")),
        name="inproj",
    )(x2, w_in)


def _attn_kernel(lam_ref, q_ref, k_ref, v_ref, bseq_ref, g_ref, o_ref, vaug_ref, bias_ref,
                 *, n_kc, out_scale):
    qi = pl.program_id(2)

    @pl.when(qi == 0)
    def _():
        vaug_ref[:, :ATT_V_DIM] = v_ref[0]
        vaug_ref[:, ATT_V_DIM:] = jnp.ones((vaug_ref.shape[0], ATT_V_DIM), BF16)
        for t in range(N_BIAS_TILES):
            seq = jnp.broadcast_to(bseq_ref[0, t:t + 1, :], (QB, 2 * KC))
            bias_ref[t] = pltpu.roll(seq, 0, 1, stride=1, stride_axis=0)[:, :KC]

    q = q_ref[0]
    lane = lax.broadcasted_iota(I32, q.shape, 1)
    zero = jnp.zeros_like(q)
    qq = jnp.concatenate([jnp.where(lane < ATT_QK_DIM, q, zero),
                          jnp.where(lane >= ATT_QK_DIM, q, zero)], axis=0)

    m = jnp.full((2 * QB, 1), -jnp.inf, F32)
    acc = jnp.zeros((2 * QB, 2 * ATT_V_DIM), F32)
    for kc in range(n_kc):
        kblk = k_ref[0, kc * KC:(kc + 1) * KC, :]
        s = lax.dot_general(qq, kblk, (((1,), (1,)), ((), ())),
                            preferred_element_type=F32)
        b = bias_ref[jnp.clip(kc - qi, -2, 2) + 2]
        s = s + jnp.concatenate([b, b], axis=0)
        m_new = jnp.maximum(m, jnp.max(s, axis=1, keepdims=True))
        p = jnp.exp2(s - m_new).astype(BF16)
        acc = acc * jnp.exp2(m - m_new) + jnp.dot(
            p, vaug_ref[kc * KC:(kc + 1) * KC, :], preferred_element_type=F32)
        m = m_new
    lam = lam_ref[0]
    o0 = acc[:QB, :ATT_V_DIM] / acc[:QB, ATT_V_DIM:ATT_V_DIM + 1]
    o1 = acc[QB:, :ATT_V_DIM] / acc[QB:, ATT_V_DIM:ATT_V_DIM + 1]
    of = o0 - lam * o1
    of = of * lax.rsqrt(jnp.mean(of * of, axis=-1, keepdims=True) + LN_EPS)
    o_ref[0] = (of * (g_ref[...] * out_scale)).astype(o_ref.dtype)


def _attention(proj3, bias_seq, lam, subln_g, lam_init):
    b, s, _ = proj3.shape
    n_kc = s // KC
    h_off_k = ATT_HEADS
    h_off_v = 2 * ATT_HEADS
    kern = functools.partial(_attn_kernel, n_kc=n_kc, out_scale=1.0 - lam_init)
    return pl.pallas_call(
        kern,
        grid=(b, ATT_HEADS, s // QB),
        in_specs=[
            pl.BlockSpec(memory_space=pltpu.SMEM),
            pl.BlockSpec((1, QB, LANES), lambda bi, h, qi: (bi, qi, h)),
            pl.BlockSpec((1, s, LANES), lambda bi, h, qi: (bi, 0, h_off_k + h)),
            pl.BlockSpec((1, s, LANES), lambda bi, h, qi: (bi, 0, h_off_v + h)),
            pl.BlockSpec((1, N_BIAS_TILES, 2 * KC), lambda bi, h, qi: (h, 0, 0)),
            pl.BlockSpec((1, ATT_V_DIM), lambda bi, h, qi: (0, 0)),
        ],
        out_specs=pl.BlockSpec((1, QB, ATT_V_DIM), lambda bi, h, qi: (bi, qi, h)),
        out_shape=jax.ShapeDtypeStruct((b, s, ATT_WIDTH), BF16),
        scratch_shapes=[pltpu.VMEM((s, 2 * ATT_V_DIM), BF16),
                        pltpu.VMEM((N_BIAS_TILES, QB, KC), F32)],
        compiler_params=_cparams(("arbitrary", "arbitrary", "arbitrary")),
        name="attn",
    )(lam, proj3, proj3, proj3, bias_seq, subln_g)


def _rel_bucket(rel):
    half = REL_BUCKETS // 2
    max_exact = half // 2
    ret = jnp.where(rel > 0, half, 0)
    n = jnp.abs(rel)
    nf = jnp.maximum(n, 1).astype(F32)
    large = max_exact + (jnp.log(nf / max_exact) / math.log(REL_MAX_DIST / max_exact)
                         * (half - max_exact)).astype(I32)
    large = jnp.minimum(large, half - 1)
    return ret + jnp.where(n < max_exact, n, large)


def _bias_sequences(rel_bias, s):
    assert REL_MAX_DIST <= KC and QB == KC
    period = 2 * KC
    j = jnp.arange(period, dtype=I32)
    off = jnp.where(j < KC, j, j - period)
    d = jnp.arange(-2, 3, dtype=I32)[:, None]
    rel = jnp.clip(d * KC + off[None, :], -(s - 1), s - 1)
    return jnp.transpose(rel_bias[_rel_bucket(rel)], (2, 0, 1)).astype(F32) * LOG2E


def _layernorm(v, g, b):
    mu = jnp.mean(v, axis=-1, keepdims=True)
    var = jnp.mean(jnp.square(v - mu), axis=-1, keepdims=True)
    return (v - mu) * lax.rsqrt(var + LN_EPS) * g + b


def _pack_bf16_pair(v):
    n = v.shape[1] // 2
    bits = lax.bitcast_convert_type(v.astype(BF16).astype(F32), U32)
    return (bits[:, :n] >> 16) | (bits[:, n:] & jnp.uint32(0xFFFF0000))


def _unpack_bf16_pair(w):
    lo = lax.bitcast_convert_type(w << 16, F32)
    hi = lax.bitcast_convert_type(w & jnp.uint32(0xFFFF0000), F32)
    return lo, hi


def _first_argmax(vals, nrows):
    mx = jnp.max(vals, axis=0, keepdims=True)
    row = lax.broadcasted_iota(I32, vals.shape, 0)
    idx = jnp.min(jnp.where(vals == mx, row, nrows), axis=0, keepdims=True)
    return mx, idx


def _mix_kernel(a_ref, u_ref, vg_ref, x_ref, wout_ref, sgw_ref, sgb_ref, sgg_ref, sgbeta_ref,
                g1_ref, b1_ref, wr_ref, br_ref,
                h1_ref, hp_ref, ri_ref, rw_ref, cnt_ref, run_ref):
    i = pl.program_id(0)

    @pl.when(i == 0)
    def _():
        run_ref[...] = jnp.zeros_like(run_ref)

    n_chunks = TM_MIX // SG_CHUNK
    rows = []
    for c in range(n_chunks):
        cols = []
        for g in range(SG_GROUPS):
            rs = slice(c * SG_CHUNK, (c + 1) * SG_CHUNK)
            cs = slice(g * SG_DIM, (g + 1) * SG_DIM)
            vt = vg_ref[rs, cs].astype(F32)
            vn = _layernorm(vt, sgg_ref[g:g + 1, :], sgbeta_ref[g:g + 1, :])
            mixed = jnp.dot(sgw_ref[g], vn.astype(BF16), preferred_element_type=F32) + sgb_ref[g]
            cols.append((u_ref[rs, cs].astype(F32) * mixed).astype(BF16))
        rows.append(jnp.concatenate(cols, axis=1))
    sgate = jnp.concatenate(rows, axis=0)

    mix = jnp.dot(a_ref[...], wout_ref[:ATT_WIDTH, :], preferred_element_type=F32)
    mix = mix + jnp.dot(sgate, wout_ref[ATT_WIDTH:, :], preferred_element_type=F32)
    h1 = _layernorm(DN_ALPHA * x_ref[...] + mix, g1_ref[...], b1_ref[...])
    h1_ref[...] = h1
    hp_ref[...] = _pack_bf16_pair(h1)

    lt = lax.dot_general(wr_ref[...], h1, (((1,), (1,)), ((), ())),
                         preferred_element_type=F32, precision=lax.Precision.HIGHEST)
    lt = lt + br_ref[...]
    g_logits = lt[0:N_GROUPS]
    gmax, g_idx = _first_argmax(g_logits, N_GROUPS)
    g_gate = 1.0 / jnp.sum(jnp.exp(g_logits - gmax), axis=0, keepdims=True)
    e_logits = jnp.zeros((EXPERTS_PER_GROUP, TM_MIX), F32)
    for g in range(N_GROUPS):
        lo = NR_EXPERT0 + g * EXPERTS_PER_GROUP
        e_logits = jnp.where(g_idx == g, lt[lo:lo + EXPERTS_PER_GROUP], e_logits)
    v1, i1 = _first_argmax(e_logits, EXPERTS_PER_GROUP)
    erow = lax.broadcasted_iota(I32, e_logits.shape, 0)
    v2, i2 = _first_argmax(jnp.where(erow == i1, -jnp.inf, e_logits), EXPERTS_PER_GROUP)
    t = jnp.exp(v2 - v1)
    w1 = g_gate / (1.0 + t)
    w2 = g_gate * t / (1.0 + t)
    e1 = g_idx * EXPERTS_PER_GROUP + i1
    e2 = g_idx * EXPERTS_PER_GROUP + i2

    xrow = lax.broadcasted_iota(I32, (N_EXPERTS, TM_MIX), 0)
    oh1 = xrow == e1
    oh2 = xrow == e2
    oh = jnp.where(jnp.logical_or(oh1, oh2), 1.0, 0.0)
    tr = lax.broadcasted_iota(I32, (TM_MIX, TM_MIX), 0)
    tc = lax.broadcasted_iota(I32, (TM_MIX, TM_MIX), 1)
    before = jnp.where(tr < tc, 1.0, 0.0).astype(BF16)
    cum = jnp.dot(oh.astype(BF16), before, preferred_element_type=F32)
    base = cum + run_ref[:, 0:1]
    rank1 = jnp.sum(jnp.where(oh1, base, 0.0), axis=0, keepdims=True).astype(I32)
    rank2 = jnp.sum(jnp.where(oh2, base, 0.0), axis=0, keepdims=True).astype(I32)
    run_ref[...] = run_ref[...] + jnp.sum(oh, axis=1, keepdims=True)
    cnt_ref[...] = run_ref[...].astype(I32)

    zi = jnp.zeros((ROUTE_ROWS - 4, TM_MIX), I32)
    ri_ref[...] = jnp.concatenate([e1, e2, rank1, rank2, zi], axis=0)
    zw = jnp.zeros((ROUTE_ROWS - 2, TM_MIX), F32)
    rw_ref[...] = jnp.concatenate([w1, w2, zw], axis=0)


def _mix(a2, proj2, x2, wout_b, sgw_b, sgb, sgg, sgbeta, g1, b1, wr, br):
    t, d = x2.shape
    u_blk = (proj2.shape[1] - 2 * SG_WIDTH) // SG_WIDTH
    const = lambda *shape: pl.BlockSpec(shape, lambda i: (0,) * len(shape))
    return pl.pallas_call(
        _mix_kernel,
        grid=(t // TM_MIX,),
        in_specs=[
            pl.BlockSpec((TM_MIX, ATT_WIDTH), lambda i: (i, 0)),
            pl.BlockSpec((TM_MIX, SG_WIDTH), lambda i: (i, u_blk)),
            pl.BlockSpec((TM_MIX, SG_WIDTH), lambda i: (i, u_blk + 1)),
            pl.BlockSpec((TM_MIX, d), lambda i: (i, 0)),
            pl.BlockSpec((d, d), lambda i: (0, 0), pipeline_mode=pl.Buffered(1)),
            const(SG_GROUPS, SG_CHUNK, SG_CHUNK),
            const(SG_GROUPS, SG_CHUNK, 1),
            const(SG_GROUPS, SG_DIM),
            const(SG_GROUPS, SG_DIM),
            const(1, d),
            const(1, d),
            const(NR_PAD, d),
            const(NR_PAD, 1),
        ],
        out_specs=[
            pl.BlockSpec((TM_MIX, d), lambda i: (i, 0)),
            pl.BlockSpec((TM_MIX, d // 2), lambda i: (i, 0)),
            pl.BlockSpec((ROUTE_ROWS, TM_MIX), lambda i: (0, i)),
            pl.BlockSpec((ROUTE_ROWS, TM_MIX), lambda i: (0, i)),
            const(N_EXPERTS, LANES),
        ],
        out_shape=[
            jax.ShapeDtypeStruct((t, d), F32),
            jax.ShapeDtypeStruct((t, d // 2), U32),
            jax.ShapeDtypeStruct((ROUTE_ROWS, t), I32),
            jax.ShapeDtypeStruct((ROUTE_ROWS, t), F32),
            jax.ShapeDtypeStruct((N_EXPERTS, LANES), I32),
        ],
        scratch_shapes=[pltpu.VMEM((N_EXPERTS, LANES), F32)],
        compiler_params=_cparams(("arbitrary",)),
        name="mix",
    )(a2, proj2, proj2, x2, wout_b, sgw_b, sgb, sgg, sgbeta, g1, b1, wr, br)


DISPATCH_CHUNK = 256


def _dispatch_kernel(pos_ref, last_ref, hp_ref, xs_hbm, zero_ref, sbuf, zsem, sem):
    i = pl.program_id(0)
    slot = i % 2

    def row_copy(sl, j, dst):
        return pltpu.make_async_copy(sbuf.at[sl, pl.ds(j, 1)], xs_hbm.at[pl.ds(dst, 1)],
                                     sem.at[sl])

    def zero_copy(e):
        start = pl.multiple_of(last_ref[e] * TMX, TMX)
        return pltpu.make_async_copy(zero_ref, xs_hbm.at[pl.ds(start, TMX)], zsem)

    @pl.when(i == 0)
    def _():
        zero_ref[...] = jnp.zeros_like(zero_ref)

        def zstart(e, c):
            zero_copy(e).start()
            return c

        def zwait(e, c):
            zero_copy(e).wait()
            return c

        lax.fori_loop(0, N_EXPERTS, zstart, 0)
        lax.fori_loop(0, N_EXPERTS, zwait, 0)

    sbuf[slot] = hp_ref[...]

    def start(j, c):
        row_copy(slot, j, pos_ref[0, j]).start(priority=0)
        row_copy(slot, j, pos_ref[1, j]).start(priority=1)
        return c

    def drain(sl):
        def wait(j, c):
            row_copy(sl, 0, 0).wait()
            row_copy(sl, 0, 0).wait()
            return c
        lax.fori_loop(0, DISPATCH_CHUNK, wait, 0, unroll=8)

    lax.fori_loop(0, DISPATCH_CHUNK, start, 0, unroll=8)

    @pl.when(i > 0)
    def _():
        drain(1 - slot)

    @pl.when(i == pl.num_programs(0) - 1)
    def _():
        drain(slot)


def _dispatch(pos3, last_tile, hp, n_rows):
    t, w = hp.shape
    return pl.pallas_call(
        _dispatch_kernel,
        grid=(t // DISPATCH_CHUNK,),
        in_specs=[
            pl.BlockSpec((None, 2, DISPATCH_CHUNK), lambda i: (i, 0, 0), memory_space=pltpu.SMEM),
            pl.BlockSpec(memory_space=pltpu.SMEM),
            pl.BlockSpec((DISPATCH_CHUNK, w), lambda i: (i, 0)),
        ],
        out_specs=pl.BlockSpec(memory_space=pl.ANY),
        out_shape=jax.ShapeDtypeStruct((n_rows, w), U32),
        scratch_shapes=[pltpu.VMEM((TMX, w), U32),
                        pltpu.VMEM((2, DISPATCH_CHUNK, w), U32),
                        pltpu.SemaphoreType.DMA(()),
                        pltpu.SemaphoreType.DMA((2,))],
        compiler_params=_cparams(("arbitrary",)),
        name="dispatch",
    )(pos3, last_tile, hp)


def _experts_kernel(ts_ref, nt_ref, nu_ref, x_hbm, wg_ref, wu_ref, wd_ref, y_hbm,
                    wgu_s, wd_s, xbuf, ybuf, xsem, ysem):
    e = pl.program_id(0)
    d = wg_ref.shape[1]
    n_used = nu_ref[0]

    def x_copy(g, slot):
        rows = pl.ds(pl.multiple_of(g * TMX, TMX), TMX)
        return pltpu.make_async_copy(x_hbm.at[rows], xbuf.at[slot], xsem.at[slot])

    def y_copy(g, slot):
        rows = pl.ds(pl.multiple_of(g * TMX, TMX), TMX)
        return pltpu.make_async_copy(ybuf.at[slot], y_hbm.at[rows], ysem.at[slot])

    @pl.when(jnp.logical_and(e == 0, n_used > 0))
    def _():
        x_copy(0, 0).start(priority=1)

    @pl.when(nt_ref[e] > 0)
    def _():
        wgu_s[:, :D_EXPERT] = wg_ref[0].astype(BF16)
        wgu_s[:, D_EXPERT:] = wu_ref[0].astype(BF16)
        wd_s[...] = wd_ref[0].astype(BF16)

    def tile(t, carry):
        g = ts_ref[e] + t
        slot = g % 2
        x_copy(g, slot).wait()

        @pl.when(g + 1 < n_used)
        def _():
            x_copy(g + 1, 1 - slot).start(priority=1)

        @pl.when(g >= 2)
        def _():
            y_copy(g - 2, slot).wait()

        lo, hi = _unpack_bf16_pair(xbuf[slot])
        gu = jnp.dot(lo.astype(BF16), wgu_s[:d // 2, :], preferred_element_type=F32)
        gu = gu + jnp.dot(hi.astype(BF16), wgu_s[d // 2:, :], preferred_element_type=F32)
        gate = gu[:, :D_EXPERT]
        hmid = (gate * jax.nn.sigmoid(gate) * gu[:, D_EXPERT:]).astype(BF16)
        y = jnp.dot(hmid, wd_s[...], preferred_element_type=F32)
        ybuf[slot] = _pack_bf16_pair(y)
        y_copy(g, slot).start(priority=1)
        return carry

    lax.fori_loop(0, nt_ref[e], tile, 0)

    @pl.when(e == pl.num_programs(0) - 1)
    def _():
        @pl.when(n_used >= 2)
        def _():
            y_copy(n_used - 2, (n_used - 2) % 2).wait()

        @pl.when(n_used >= 1)
        def _():
            y_copy(n_used - 1, (n_used - 1) % 2).wait()


def _experts(tile_start, n_tiles_e, n_used, xs, w_gate, w_up, w_down):
    n_rows, w = xs.shape
    d = w_gate.shape[1]

    def w_map(e, ts, nt, nu):
        return (e, 0, 0)

    grid_spec = pltpu.PrefetchScalarGridSpec(
        num_scalar_prefetch=3,
        grid=(N_EXPERTS,),
        in_specs=[
            pl.BlockSpec(memory_space=pl.ANY),
            pl.BlockSpec((1, d, D_EXPERT), w_map),
            pl.BlockSpec((1, d, D_EXPERT), w_map),
            pl.BlockSpec((1, D_EXPERT, d), w_map),
        ],
        out_specs=pl.BlockSpec(memory_space=pl.ANY),
        scratch_shapes=[pltpu.VMEM((d, 2 * D_EXPERT), BF16),
                        pltpu.VMEM((D_EXPERT, d), BF16),
                        pltpu.VMEM((2, TMX, w), U32),
                        pltpu.VMEM((2, TMX, w), U32),
                        pltpu.SemaphoreType.DMA((2,)),
                        pltpu.SemaphoreType.DMA((2,))],
    )
    return pl.pallas_call(
        _experts_kernel,
        grid_spec=grid_spec,
        out_shape=jax.ShapeDtypeStruct((n_rows, w), U32),
        compiler_params=_cparams(("arbitrary",)),
        name="experts",
    )(tile_start, n_tiles_e, n_used, xs, w_gate, w_up, w_down)


def _combine_kernel(pos_ref, posn_ref, y_hbm, h1_ref, wt_ref, g2_ref, b2_ref, o_ref, ybuf, sem):
    i = pl.program_id(0)
    n = pl.num_programs(0)
    slot = i % 2

    def row_copy(src, sl, which, j):
        return pltpu.make_async_copy(y_hbm.at[pl.ds(src, 1)],
                                     ybuf.at[sl, which, pl.ds(j, 1)], sem.at[sl])

    def issue(p_ref, sl):
        def body(j, c):
            row_copy(p_ref[0, j], sl, 0, j).start(priority=0)
            row_copy(p_ref[1, j], sl, 1, j).start(priority=1)
            return c
        lax.fori_loop(0, TM_OUT, body, 0, unroll=8)

    @pl.when(i == 0)
    def _():
        issue(pos_ref, 0)

    @pl.when(i + 1 < n)
    def _():
        issue(posn_ref, 1 - slot)

    def wait(j, c):
        row_copy(pos_ref[0, j], slot, 0, j).wait()
        row_copy(pos_ref[1, j], slot, 1, j).wait()
        return c

    lax.fori_loop(0, TM_OUT, wait, 0, unroll=8)

    lo1, hi1 = _unpack_bf16_pair(ybuf[slot, 0])
    lo2, hi2 = _unpack_bf16_pair(ybuf[slot, 1])
    w1 = wt_ref[:, 0:1]
    w2 = wt_ref[:, 1:2]
    ffn = jnp.concatenate([w1 * lo1 + w2 * lo2, w1 * hi1 + w2 * hi2], axis=1)
    o_ref[...] = _layernorm(DN_ALPHA * h1_ref[...] + ffn, g2_ref[...], b2_ref[...])


def _combine(pos3, ys, h1, wtok, g2, b2):
    t, d = h1.shape
    w = ys.shape[1]
    n = t // TM_OUT
    return pl.pallas_call(
        _combine_kernel,
        grid=(n,),
        in_specs=[
            pl.BlockSpec((None, 2, TM_OUT), lambda i: (i, 0, 0), memory_space=pltpu.SMEM),
            pl.BlockSpec((None, 2, TM_OUT), lambda i: (jnp.minimum(i + 1, n - 1), 0, 0),
                         memory_space=pltpu.SMEM),
            pl.BlockSpec(memory_space=pl.ANY),
            pl.BlockSpec((TM_OUT, d), lambda i: (i, 0)),
            pl.BlockSpec((TM_OUT, ROUTE_ROWS), lambda i: (i, 0)),
            pl.BlockSpec((1, d), lambda i: (0, 0)),
            pl.BlockSpec((1, d), lambda i: (0, 0)),
        ],
        out_specs=pl.BlockSpec((TM_OUT, d), lambda i: (i, 0)),
        out_shape=jax.ShapeDtypeStruct((t, d), F32),
        scratch_shapes=[pltpu.VMEM((2, 2, TM_OUT, w), U32),
                        pltpu.SemaphoreType.DMA((2,))],
        compiler_params=_cparams(("arbitrary",)),
        name="combine",
    )(pos3, pos3, ys, h1, wtok, g2, b2)


def _lambda_init(layer):
    return 0.8 - 0.6 * math.exp(-0.3 * layer)


def kernel(x, w_in, w_out, ln1_g, ln1_b, ln2_g, ln2_b, rel_bias, lam_q1, lam_k1, lam_q2, lam_k2,
           subln_g, sg_ln_g, sg_ln_b, sg_w, sg_b, w_router_group, b_router_group,
           w_router_expert, b_router_expert, w_exp_gate, w_exp_up, w_exp_down):
    b, s, d = x.shape
    t = b * s
    assert w_in.shape[0] == DEPTH == 1
    l = 0
    lam_init = _lambda_init(l)
    x2 = x.reshape(t, d)

    proj = _inproj(x2, w_in[l])

    lam = (jnp.exp(jnp.sum(lam_q1[l].astype(F32) * lam_k1[l].astype(F32)))
           - jnp.exp(jnp.sum(lam_q2[l].astype(F32) * lam_k2[l].astype(F32))) + lam_init)
    a = _attention(proj.reshape(b, s, -1), _bias_sequences(rel_bias, s), lam.reshape(1),
                   subln_g[l].reshape(1, -1), lam_init)

    wr = jnp.concatenate([w_router_group[l].T,
                          jnp.zeros((NR_EXPERT0 - N_GROUPS, d), F32),
                          jnp.transpose(w_router_expert[l], (0, 2, 1)).reshape(N_EXPERTS, d)],
                         axis=0)
    br = jnp.concatenate([b_router_group[l], jnp.zeros((NR_EXPERT0 - N_GROUPS,), F32),
                          b_router_expert[l].reshape(-1)]).reshape(NR_PAD, 1)
    h1, hp, ri, rw, cnt = _mix(
        a.reshape(t, -1), proj, x2, w_out[l].astype(BF16), sg_w[l].astype(BF16),
        sg_b[l].reshape(SG_GROUPS, SG_CHUNK, 1), sg_ln_g[l], sg_ln_b[l],
        ln1_g[l].reshape(1, d), ln1_b[l].reshape(1, d), wr, br)

    counts = cnt[:, 0]
    tiles_e = (counts + TMX - 1) // TMX
    tile_end = jnp.cumsum(tiles_e)
    tile_start = tile_end - tiles_e
    n_used = tile_end[-1:]
    max_tiles = (2 * t) // TMX + N_EXPERTS
    last_tile = jnp.maximum(tile_end - 1, 0).astype(I32)
    row_start = (tile_start * TMX).astype(I32)
    eids = jnp.arange(N_EXPERTS, dtype=I32)[:, None, None]
    pos = jnp.sum(jnp.where(ri[0:2][None] == eids, row_start[:, None, None], 0), axis=0) + ri[2:4]
    pos3 = pos.reshape(2, t // TM_OUT, TM_OUT).transpose(1, 0, 2)

    xs = _dispatch(pos3, last_tile, hp, max_tiles * TMX)
    ys = _experts(tile_start.astype(I32), tiles_e.astype(I32), n_used.astype(I32), xs,
                  w_exp_gate[l], w_exp_up[l], w_exp_down[l])
    out = _combine(pos3, ys, h1, rw.T, ln2_g[l].reshape(1, d), ln2_b[l].reshape(1, d))
    return out.reshape(b, s, d)
```

```python
import functools
import math

import jax
import jax.numpy as jnp
from jax import lax
from jax.experimental import pallas as pl
from jax.experimental.pallas import tpu as pltpu

F32 = jnp.float32
BF16 = jnp.bfloat16
U32 = jnp.uint32
I32 = jnp.int32

ATT_HEADS = 8
ATT_QK_DIM = 64
ATT_V_DIM = 128
ATT_WIDTH = ATT_HEADS * ATT_V_DIM
SG_GROUPS = 8
SG_DIM = 128
SG_WIDTH = SG_GROUPS * SG_DIM
SG_CHUNK = 128
REL_BUCKETS = 32
REL_MAX_DIST = 128
N_GROUPS = 4
EXPERTS_PER_GROUP = 8
N_EXPERTS = N_GROUPS * EXPERTS_PER_GROUP
D_EXPERT = 256
DEPTH = 1
DN_ALPHA = (2.0 * DEPTH) ** 0.25
LN_EPS = 1e-5
LOG2E = math.log2(math.e)

LANES = 128
SUBLANES = 8
VMEM_LIMIT_BYTES = 56 * 1024 * 1024

TM_IN = 1024
TN_IN = 1024
QB = 256
KC = 256
N_BIAS_TILES = 5
TM_MIX = 512
TMX = 256
TM_OUT = 256
ROUTE_ROWS = 8
NR_EXPERT0 = 8
NR_PAD = NR_EXPERT0 + 32


def _cparams(sem, flags=None):
    return pltpu.CompilerParams(dimension_semantics=sem, vmem_limit_bytes=VMEM_LIMIT_BYTES,
                                flags=flags)


def _inproj_kernel(x_ref, w_ref, o_ref, xb_ref, *, n_q, n_lin, q_scale):
    j = pl.program_id(1)

    @pl.when(j == 0)
    def _():
        xb_ref[...] = x_ref[...].astype(BF16)

    acc = jnp.dot(xb_ref[...], w_ref[...].astype(BF16), preferred_element_type=F32)

    @pl.when(j < n_q)
    def _():
        o_ref[...] = (acc * q_scale).astype(BF16)

    @pl.when(jnp.logical_and(j >= n_q, j < n_lin))
    def _():
        o_ref[...] = acc.astype(BF16)

    @pl.when(j >= n_lin)
    def _():
        o_ref[...] = jax.nn.gelu(acc).astype(BF16)


def _inproj(x2, w_in):
    t, d = x2.shape
    n = w_in.shape[1]
    n_q = (ATT_HEADS * 2 * ATT_QK_DIM) // TN_IN
    n_lin = (2 * ATT_HEADS * 2 * ATT_QK_DIM + ATT_WIDTH) // TN_IN
    kern = functools.partial(_inproj_kernel, n_q=n_q, n_lin=n_lin,
                             q_scale=(ATT_QK_DIM ** -0.5) * LOG2E)
    return pl.pallas_call(
        kern,
        grid=(t // TM_IN, n // TN_IN),
        in_specs=[pl.BlockSpec((TM_IN, d), lambda i, j: (i, 0)),
                  pl.BlockSpec((d, TN_IN), lambda i, j: (0, j))],
        out_specs=pl.BlockSpec((TM_IN, TN_IN), lambda i, j: (i, j)),
        out_shape=jax.ShapeDtypeStruct((t, n), BF16),
        scratch_shapes=[pltpu.VMEM((TM_IN, d), BF16)],
        compiler_params=_cparams(("arbitrary", "arbitrary")),
        name="inproj",
    )(x2, w_in)


def _attn_kernel(lam_ref, q_ref, k_ref, v_ref, bseq_ref, g_ref, *rest, n_kc, out_scale):
    n_side = (len(rest) - 3) // 2
    side_in, o_ref, side_out = rest[:n_side], rest[n_side], rest[n_side + 1:2 * n_side + 1]
    vaug_ref, bias_ref = rest[2 * n_side + 1:]
    qi = pl.program_id(2)
    for src, dst in zip(side_in, side_out):
        dst[...] = src[...].astype(dst.dtype)

    @pl.when(qi == 0)
    def _():
        vaug_ref[:, :ATT_V_DIM] = v_ref[0]
        vaug_ref[:, ATT_V_DIM:] = jnp.ones((vaug_ref.shape[0], ATT_V_DIM), BF16)
        for t in range(N_BIAS_TILES):
            seq = jnp.broadcast_to(bseq_ref[0, t:t + 1, :], (QB, 2 * KC))
            bias_ref[t] = pltpu.roll(seq, 0, 1, stride=1, stride_axis=0)[:, :KC]

    q = q_ref[0]
    lane = lax.broadcasted_iota(I32, q.shape, 1)
    zero = jnp.zeros_like(q)
    qq = jnp.concatenate([jnp.where(lane < ATT_QK_DIM, q, zero),
                          jnp.where(lane >= ATT_QK_DIM, q, zero)], axis=0)

    m = jnp.full((2 * QB, 1), -jnp.inf, F32)
    acc = jnp.zeros((2 * QB, 2 * ATT_V_DIM), F32)
    for kc in range(n_kc):
        kblk = k_ref[0, kc * KC:(kc + 1) * KC, :]
        s = lax.dot_general(qq, kblk, (((1,), (1,)), ((), ())),
                            preferred_element_type=F32)
        b = bias_ref[jnp.clip(kc - qi, -2, 2) + 2]
        s = s + jnp.concatenate([b, b], axis=0)
        m_new = jnp.maximum(m, jnp.max(s, axis=1, keepdims=True))
        p = jnp.exp2(s - m_new).astype(BF16)
        acc = acc * jnp.exp2(m - m_new) + jnp.dot(
            p, vaug_ref[kc * KC:(kc + 1) * KC, :], preferred_element_type=F32)
        m = m_new
    lam = lam_ref[0]
    o0 = acc[:QB, :ATT_V_DIM] / acc[:QB, ATT_V_DIM:ATT_V_DIM + 1]
    o1 = acc[QB:, :ATT_V_DIM] / acc[QB:, ATT_V_DIM:ATT_V_DIM + 1]
    of = o0 - lam * o1
    of = of * lax.rsqrt(jnp.mean(of * of, axis=-1, keepdims=True) + LN_EPS)
    o_ref[0] = (of * (g_ref[...] * out_scale)).astype(o_ref.dtype)


def _attention(proj3, bias_seq, lam, subln_g, lam_init, side):
    b, s, _ = proj3.shape
    n_kc = s // KC
    n_q = s // QB
    n_steps = b * ATT_HEADS * n_q
    h_off_k = ATT_HEADS
    h_off_v = 2 * ATT_HEADS
    kern = functools.partial(_attn_kernel, n_kc=n_kc, out_scale=1.0 - lam_init)

    def side_map(bi, h, qi):
        return ((bi * ATT_HEADS + h) * n_q + qi, 0)

    side_specs = []
    for arr in side:
        rows, cols = arr.shape
        assert rows % (n_steps * 2 * SUBLANES) == 0
        side_specs.append(pl.BlockSpec((rows // n_steps, cols), side_map))
    return pl.pallas_call(
        kern,
        grid=(b, ATT_HEADS, n_q),
        in_specs=[
            pl.BlockSpec(memory_space=pltpu.SMEM),
            pl.BlockSpec((1, QB, LANES), lambda bi, h, qi: (bi, qi, h)),
            pl.BlockSpec((1, s, LANES), lambda bi, h, qi: (bi, 0, h_off_k + h)),
            pl.BlockSpec((1, s, LANES), lambda bi, h, qi: (bi, 0, h_off_v + h)),
            pl.BlockSpec((1, N_BIAS_TILES, 2 * KC), lambda bi, h, qi: (h, 0, 0)),
            pl.BlockSpec((1, ATT_V_DIM), lambda bi, h, qi: (0, 0)),
        ] + side_specs,
        out_specs=[pl.BlockSpec((1, QB, ATT_V_DIM), lambda bi, h, qi: (bi, qi, h))] + side_specs,
        out_shape=[jax.ShapeDtypeStruct((b, s, ATT_WIDTH), BF16)]
        + [jax.ShapeDtypeStruct(arr.shape, BF16) for arr in side],
        scratch_shapes=[pltpu.VMEM((s, 2 * ATT_V_DIM), BF16),
                        pltpu.VMEM((N_BIAS_TILES, QB, KC), F32)],
        compiler_params=_cparams(("arbitrary", "arbitrary", "arbitrary")),
        name="attn",
    )(lam, proj3, proj3, proj3, bias_seq, subln_g, *side)


def _rel_bucket(rel):
    half = REL_BUCKETS // 2
    max_exact = half // 2
    ret = jnp.where(rel > 0, half, 0)
    n = jnp.abs(rel)
    nf = jnp.maximum(n, 1).astype(F32)
    large = max_exact + (jnp.log(nf / max_exact) / math.log(REL_MAX_DIST / max_exact)
                         * (half - max_exact)).astype(I32)
    large = jnp.minimum(large, half - 1)
    return ret + jnp.where(n < max_exact, n, large)


def _bias_sequences(rel_bias, s):
    assert REL_MAX_DIST <= KC and QB == KC
    period = 2 * KC
    j = jnp.arange(period, dtype=I32)
    off = jnp.where(j < KC, j, j - period)
    d = jnp.arange(-2, 3, dtype=I32)[:, None]
    rel = jnp.clip(d * KC + off[None, :], -(s - 1), s - 1)
    return jnp.transpose(rel_bias[_rel_bucket(rel)], (2, 0, 1)).astype(F32) * LOG2E


def _layernorm(v, g, b):
    mu = jnp.mean(v, axis=-1, keepdims=True)
    var = jnp.mean(jnp.square(v - mu), axis=-1, keepdims=True)
    return (v - mu) * lax.rsqrt(var + LN_EPS) * g + b


def _pack_bf16_pair(v):
    n = v.shape[1] // 2
    bits = lax.bitcast_convert_type(v.astype(BF16).astype(F32), U32)
    return (bits[:, :n] >> 16) | (bits[:, n:] & jnp.uint32(0xFFFF0000))


def _unpack_bf16_pair(w):
    lo = lax.bitcast_convert_type(w << 16, F32)
    hi = lax.bitcast_convert_type(w & jnp.uint32(0xFFFF0000), F32)
    return lo, hi


def _first_argmax(vals, nrows):
    mx = jnp.max(vals, axis=0, keepdims=True)
    row = lax.broadcasted_iota(I32, vals.shape, 0)
    idx = jnp.min(jnp.where(vals == mx, row, nrows), axis=0, keepdims=True)
    return mx, idx


def _mix_kernel(a_ref, u_ref, vg_ref, x_ref, wout_ref, sgw_ref, sgb_ref, sgg_ref, sgbeta_ref,
                g1_ref, b1_ref, wr_ref, br_ref,
                h1_ref, hp_ref, ri_ref, rw_ref, cnt_ref, run_ref):
    i = pl.program_id(0)

    @pl.when(i == 0)
    def _():
        run_ref[...] = jnp.zeros_like(run_ref)

    n_chunks = TM_MIX // SG_CHUNK
    rows = []
    for c in range(n_chunks):
        cols = []
        for g in range(SG_GROUPS):
            rs = slice(c * SG_CHUNK, (c + 1) * SG_CHUNK)
            cs = slice(g * SG_DIM, (g + 1) * SG_DIM)
            vt = vg_ref[rs, cs].astype(F32)
            vn = _layernorm(vt, sgg_ref[g:g + 1, :], sgbeta_ref[g:g + 1, :])
            mixed = jnp.dot(sgw_ref[g], vn.astype(BF16), preferred_element_type=F32) + sgb_ref[g]
            cols.append((u_ref[rs, cs].astype(F32) * mixed).astype(BF16))
        rows.append(jnp.concatenate(cols, axis=1))
    sgate = jnp.concatenate(rows, axis=0)

    mix = jnp.dot(a_ref[...], wout_ref[:ATT_WIDTH, :], preferred_element_type=F32)
    mix = mix + jnp.dot(sgate, wout_ref[ATT_WIDTH:, :], preferred_element_type=F32)
    h1 = _layernorm(DN_ALPHA * x_ref[...] + mix, g1_ref[...], b1_ref[...])
    h1_ref[...] = h1
    hp_ref[...] = _pack_bf16_pair(h1)

    lt = lax.dot_general(wr_ref[...], h1, (((1,), (1,)), ((), ())),
                         preferred_element_type=F32, precision=lax.Precision.HIGHEST)
    lt = lt + br_ref[...]
    g_logits = lt[0:N_GROUPS]
    gmax, g_idx = _first_argmax(g_logits, N_GROUPS)
    g_gate = 1.0 / jnp.sum(jnp.exp(g_logits - gmax), axis=0, keepdims=True)
    e_logits = jnp.zeros((EXPERTS_PER_GROUP, TM_MIX), F32)
    for g in range(N_GROUPS):
        lo = NR_EXPERT0 + g * EXPERTS_PER_GROUP
        e_logits = jnp.where(g_idx == g, lt[lo:lo + EXPERTS_PER_GROUP], e_logits)
    v1, i1 = _first_argmax(e_logits, EXPERTS_PER_GROUP)
    erow = lax.broadcasted_iota(I32, e_logits.shape, 0)
    v2, i2 = _first_argmax(jnp.where(erow == i1, -jnp.inf, e_logits), EXPERTS_PER_GROUP)
    t = jnp.exp(v2 - v1)
    w1 = g_gate / (1.0 + t)
    w2 = g_gate * t / (1.0 + t)
    e1 = g_idx * EXPERTS_PER_GROUP + i1
    e2 = g_idx * EXPERTS_PER_GROUP + i2

    xrow = lax.broadcasted_iota(I32, (N_EXPERTS, TM_MIX), 0)
    oh1 = xrow == e1
    oh2 = xrow == e2
    oh = jnp.where(jnp.logical_or(oh1, oh2), 1.0, 0.0)
    tr = lax.broadcasted_iota(I32, (TM_MIX, TM_MIX), 0)
    tc = lax.broadcasted_iota(I32, (TM_MIX, TM_MIX), 1)
    before = jnp.where(tr < tc, 1.0, 0.0).astype(BF16)
    cum = jnp.dot(oh.astype(BF16), before, preferred_element_type=F32)
    base = cum + run_ref[:, 0:1]
    rank1 = jnp.sum(jnp.where(oh1, base, 0.0), axis=0, keepdims=True).astype(I32)
    rank2 = jnp.sum(jnp.where(oh2, base, 0.0), axis=0, keepdims=True).astype(I32)
    run_ref[...] = run_ref[...] + jnp.sum(oh, axis=1, keepdims=True)
    cnt_ref[...] = run_ref[...].astype(I32)

    zi = jnp.zeros((ROUTE_ROWS - 4, TM_MIX), I32)
    ri_ref[...] = jnp.concatenate([e1, e2, rank1, rank2, zi], axis=0)
    zw = jnp.zeros((ROUTE_ROWS - 2, TM_MIX), F32)
    rw_ref[...] = jnp.concatenate([w1, w2, zw], axis=0)


def _mix(a2, proj2, x2, wout_b, sgw_b, sgb, sgg, sgbeta, g1, b1, wr, br):
    t, d = x2.shape
    u_blk = (proj2.shape[1] - 2 * SG_WIDTH) // SG_WIDTH
    const = lambda *shape: pl.BlockSpec(shape, lambda i: (0,) * len(shape))
    return pl.pallas_call(
        _mix_kernel,
        grid=(t // TM_MIX,),
        in_specs=[
            pl.BlockSpec((TM_MIX, ATT_WIDTH), lambda i: (i, 0)),
            pl.BlockSpec((TM_MIX, SG_WIDTH), lambda i: (i, u_blk)),
            pl.BlockSpec((TM_MIX, SG_WIDTH), lambda i: (i, u_blk + 1)),
            pl.BlockSpec((TM_MIX, d), lambda i: (i, 0)),
            pl.BlockSpec((d, d), lambda i: (0, 0), pipeline_mode=pl.Buffered(1)),
            const(SG_GROUPS, SG_CHUNK, SG_CHUNK),
            const(SG_GROUPS, SG_CHUNK, 1),
            const(SG_GROUPS, SG_DIM),
            const(SG_GROUPS, SG_DIM),
            const(1, d),
            const(1, d),
            const(NR_PAD, d),
            const(NR_PAD, 1),
        ],
        out_specs=[
            pl.BlockSpec((TM_MIX, d), lambda i: (i, 0)),
            pl.BlockSpec((TM_MIX, d // 2), lambda i: (i, 0)),
            pl.BlockSpec((ROUTE_ROWS, TM_MIX), lambda i: (0, i)),
            pl.BlockSpec((ROUTE_ROWS, TM_MIX), lambda i: (0, i)),
            const(N_EXPERTS, LANES),
        ],
        out_shape=[
            jax.ShapeDtypeStruct((t, d), F32),
            jax.ShapeDtypeStruct((t, d // 2), U32),
            jax.ShapeDtypeStruct((ROUTE_ROWS, t), I32),
            jax.ShapeDtypeStruct((ROUTE_ROWS, t), F32),
            jax.ShapeDtypeStruct((N_EXPERTS, LANES), I32),
        ],
        scratch_shapes=[pltpu.VMEM((N_EXPERTS, LANES), F32)],
        compiler_params=_cparams(("arbitrary",)),
        name="mix",
    )(a2, proj2, proj2, x2, wout_b, sgw_b, sgb, sgg, sgbeta, g1, b1, wr, br)


DISPATCH_CHUNK = 256


def _dispatch_kernel(pos_ref, last_ref, hp_ref, xs_hbm, zero_ref, sbuf, zsem, sem):
    i = pl.program_id(0)
    slot = i % 2

    def row_copy(sl, j, dst):
        return pltpu.make_async_copy(sbuf.at[sl, pl.ds(j, 1)], xs_hbm.at[pl.ds(dst, 1)],
                                     sem.at[sl])

    def zero_copy(e):
        start = pl.multiple_of(last_ref[e] * TMX, TMX)
        return pltpu.make_async_copy(zero_ref, xs_hbm.at[pl.ds(start, TMX)], zsem)

    @pl.when(i == 0)
    def _():
        zero_ref[...] = jnp.zeros_like(zero_ref)

        def zstart(e, c):
            zero_copy(e).start()
            return c

        def zwait(e, c):
            zero_copy(e).wait()
            return c

        lax.fori_loop(0, N_EXPERTS, zstart, 0)
        lax.fori_loop(0, N_EXPERTS, zwait, 0)

    sbuf[slot] = hp_ref[...]

    def start(j, c):
        row_copy(slot, j, pos_ref[0, j]).start(priority=0)
        row_copy(slot, j, pos_ref[1, j]).start(priority=1)
        return c

    def drain(sl):
        def wait(j, c):
            row_copy(sl, 0, 0).wait()
            row_copy(sl, 0, 0).wait()
            return c
        lax.fori_loop(0, DISPATCH_CHUNK, wait, 0, unroll=8)

    lax.fori_loop(0, DISPATCH_CHUNK, start, 0, unroll=8)

    @pl.when(i > 0)
    def _():
        drain(1 - slot)

    @pl.when(i == pl.num_programs(0) - 1)
    def _():
        drain(slot)


def _dispatch(pos3, last_tile, hp, n_rows):
    t, w = hp.shape
    return pl.pallas_call(
        _dispatch_kernel,
        grid=(t // DISPATCH_CHUNK,),
        in_specs=[
            pl.BlockSpec((None, 2, DISPATCH_CHUNK), lambda i: (i, 0, 0), memory_space=pltpu.SMEM),
            pl.BlockSpec(memory_space=pltpu.SMEM),
            pl.BlockSpec((DISPATCH_CHUNK, w), lambda i: (i, 0)),
        ],
        out_specs=pl.BlockSpec(memory_space=pl.ANY),
        out_shape=jax.ShapeDtypeStruct((n_rows, w), U32),
        scratch_shapes=[pltpu.VMEM((TMX, w), U32),
                        pltpu.VMEM((2, DISPATCH_CHUNK, w), U32),
                        pltpu.SemaphoreType.DMA(()),
                        pltpu.SemaphoreType.DMA((2,))],
        compiler_params=_cparams(("arbitrary",)),
        name="dispatch",
    )(pos3, last_tile, hp)


def _experts_kernel(ts_ref, nt_ref, nu_ref, x_hbm, wg_ref, wu_ref, wd_ref, y_hbm,
                    xbuf, ybuf, xsem, ysem):
    e = pl.program_id(0)
    d = wg_ref.shape[1]
    n_used = nu_ref[0]

    def x_copy(g, slot):
        rows = pl.ds(pl.multiple_of(g * TMX, TMX), TMX)
        return pltpu.make_async_copy(x_hbm.at[rows], xbuf.at[slot], xsem.at[slot])

    def y_copy(g, slot):
        rows = pl.ds(pl.multiple_of(g * TMX, TMX), TMX)
        return pltpu.make_async_copy(ybuf.at[slot], y_hbm.at[rows], ysem.at[slot])

    @pl.when(jnp.logical_and(e == 0, n_used > 0))
    def _():
        x_copy(0, 0).start(priority=1)

    def tile(t, carry):
        g = ts_ref[e] + t
        slot = g % 2
        x_copy(g, slot).wait()

        @pl.when(g + 1 < n_used)
        def _():
            x_copy(g + 1, 1 - slot).start(priority=1)

        @pl.when(g >= 2)
        def _():
            y_copy(g - 2, slot).wait()

        lo, hi = _unpack_bf16_pair(xbuf[slot])
        lo, hi = lo.astype(BF16), hi.astype(BF16)
        gate = jnp.dot(lo, wg_ref[0, :d // 2, :], preferred_element_type=F32)
        gate = gate + jnp.dot(hi, wg_ref[0, d // 2:, :], preferred_element_type=F32)
        up = jnp.dot(lo, wu_ref[0, :d // 2, :], preferred_element_type=F32)
        up = up + jnp.dot(hi, wu_ref[0, d // 2:, :], preferred_element_type=F32)
        hmid = (gate * jax.nn.sigmoid(gate) * up).astype(BF16)
        y = jnp.dot(hmid, wd_ref[0], preferred_element_type=F32)
        ybuf[slot] = _pack_bf16_pair(y)
        y_copy(g, slot).start(priority=1)
        return carry

    lax.fori_loop(0, nt_ref[e], tile, 0)

    @pl.when(e == pl.num_programs(0) - 1)
    def _():
        @pl.when(n_used >= 2)
        def _():
            y_copy(n_used - 2, (n_used - 2) % 2).wait()

        @pl.when(n_used >= 1)
        def _():
            y_copy(n_used - 1, (n_used - 1) % 2).wait()


def _experts(tile_start, n_tiles_e, n_used, xs, w_gate, w_up, w_down):
    n_rows, w = xs.shape
    d = w_gate.shape[1]

    def w_map(e, ts, nt, nu):
        return (e, 0, 0)

    grid_spec = pltpu.PrefetchScalarGridSpec(
        num_scalar_prefetch=3,
        grid=(N_EXPERTS,),
        in_specs=[
            pl.BlockSpec(memory_space=pl.ANY),
            pl.BlockSpec((1, d, D_EXPERT), w_map),
            pl.BlockSpec((1, d, D_EXPERT), w_map),
            pl.BlockSpec((1, D_EXPERT, d), w_map),
        ],
        out_specs=pl.BlockSpec(memory_space=pl.ANY),
        scratch_shapes=[pltpu.VMEM((2, TMX, w), U32),
                        pltpu.VMEM((2, TMX, w), U32),
                        pltpu.SemaphoreType.DMA((2,)),
                        pltpu.SemaphoreType.DMA((2,))],
    )
    return pl.pallas_call(
        _experts_kernel,
        grid_spec=grid_spec,
        out_shape=jax.ShapeDtypeStruct((n_rows, w), U32),
        compiler_params=_cparams(("arbitrary",)),
        name="experts",
    )(tile_start, n_tiles_e, n_used, xs, w_gate, w_up, w_down)


def _combine_kernel(pos_ref, posn_ref, y_hbm, h1_ref, wt_ref, g2_ref, b2_ref, o_ref, ybuf, sem):
    i = pl.program_id(0)
    n = pl.num_programs(0)
    slot = i % 2

    def row_copy(src, sl, which, j):
        return pltpu.make_async_copy(y_hbm.at[pl.ds(src, 1)],
                                     ybuf.at[sl, which, pl.ds(j, 1)], sem.at[sl])

    def issue(p_ref, sl):
        def body(j, c):
            row_copy(p_ref[0, j], sl, 0, j).start(priority=0)
            row_copy(p_ref[1, j], sl, 1, j).start(priority=1)
            return c
        lax.fori_loop(0, TM_OUT, body, 0, unroll=8)

    @pl.when(i == 0)
    def _():
        issue(pos_ref, 0)

    @pl.when(i + 1 < n)
    def _():
        issue(posn_ref, 1 - slot)

    def wait(j, c):
        row_copy(pos_ref[0, j], slot, 0, j).wait()
        row_copy(pos_ref[1, j], slot, 1, j).wait()
        return c

    lax.fori_loop(0, TM_OUT, wait, 0, unroll=8)

    lo1, hi1 = _unpack_bf16_pair(ybuf[slot, 0])
    lo2, hi2 = _unpack_bf16_pair(ybuf[slot, 1])
    w1 = wt_ref[:, 0:1]
    w2 = wt_ref[:, 1:2]
    ffn = jnp.concatenate([w1 * lo1 + w2 * lo2, w1 * hi1 + w2 * hi2], axis=1)
    o_ref[...] = _layernorm(DN_ALPHA * h1_ref[...] + ffn, g2_ref[...], b2_ref[...])


def _combine(pos3, ys, h1, wtok, g2, b2):
    t, d = h1.shape
    w = ys.shape[1]
    n = t // TM_OUT
    return pl.pallas_call(
        _combine_kernel,
        grid=(n,),
        in_specs=[
            pl.BlockSpec((None, 2, TM_OUT), lambda i: (i, 0, 0), memory_space=pltpu.SMEM),
            pl.BlockSpec((None, 2, TM_OUT), lambda i: (jnp.minimum(i + 1, n - 1), 0, 0),
                         memory_space=pltpu.SMEM),
            pl.BlockSpec(memory_space=pl.ANY),
            pl.BlockSpec((TM_OUT, d), lambda i: (i, 0)),
            pl.BlockSpec((TM_OUT, ROUTE_ROWS), lambda i: (i, 0)),
            pl.BlockSpec((1, d), lambda i: (0, 0)),
            pl.BlockSpec((1, d), lambda i: (0, 0)),
        ],
        out_specs=pl.BlockSpec((TM_OUT, d), lambda i: (i, 0)),
        out_shape=jax.ShapeDtypeStruct((t, d), F32),
        scratch_shapes=[pltpu.VMEM((2, 2, TM_OUT, w), U32),
                        pltpu.SemaphoreType.DMA((2,))],
        compiler_params=_cparams(("arbitrary",)),
        name="combine",
    )(pos3, pos3, ys, h1, wtok, g2, b2)


def _lambda_init(layer):
    return 0.8 - 0.6 * math.exp(-0.3 * layer)


def kernel(x, w_in, w_out, ln1_g, ln1_b, ln2_g, ln2_b, rel_bias, lam_q1, lam_k1, lam_q2, lam_k2,
           subln_g, sg_ln_g, sg_ln_b, sg_w, sg_b, w_router_group, b_router_group,
           w_router_expert, b_router_expert, w_exp_gate, w_exp_up, w_exp_down):
    b, s, d = x.shape
    t = b * s
    assert w_in.shape[0] == DEPTH == 1
    l = 0
    lam_init = _lambda_init(l)
    x2 = x.reshape(t, d)

    proj = _inproj(x2, w_in[l])

    lam = (jnp.exp(jnp.sum(lam_q1[l].astype(F32) * lam_k1[l].astype(F32)))
           - jnp.exp(jnp.sum(lam_q2[l].astype(F32) * lam_k2[l].astype(F32))) + lam_init)
    side = [w_exp_gate[l].reshape(-1, D_EXPERT), w_exp_up[l].reshape(-1, D_EXPERT),
            w_exp_down[l].reshape(-1, d)]
    a, wg_b, wu_b, wd_b = _attention(
        proj.reshape(b, s, -1), _bias_sequences(rel_bias, s), lam.reshape(1),
        subln_g[l].reshape(1, -1), lam_init, side)

    wr = jnp.concatenate([w_router_group[l].T,
                          jnp.zeros((NR_EXPERT0 - N_GROUPS, d), F32),
                          jnp.transpose(w_router_expert[l], (0, 2, 1)).reshape(N_EXPERTS, d)],
                         axis=0)
    br = jnp.concatenate([b_router_group[l], jnp.zeros((NR_EXPERT0 - N_GROUPS,), F32),
                          b_router_expert[l].reshape(-1)]).reshape(NR_PAD, 1)
    h1, hp, ri, rw, cnt = _mix(
        a.reshape(t, -1), proj, x2, w_out[l].astype(BF16), sg_w[l].astype(BF16),
        sg_b[l].reshape(SG_GROUPS, SG_CHUNK, 1), sg_ln_g[l], sg_ln_b[l],
        ln1_g[l].reshape(1, d), ln1_b[l].reshape(1, d), wr, br)

    counts = cnt[:, 0]
    tiles_e = (counts + TMX - 1) // TMX
    tile_end = jnp.cumsum(tiles_e)
    tile_start = tile_end - tiles_e
    n_used = tile_end[-1:]
    max_tiles = (2 * t) // TMX + N_EXPERTS
    last_tile = jnp.maximum(tile_end - 1, 0).astype(I32)
    row_start = (tile_start * TMX).astype(I32)
    eids = jnp.arange(N_EXPERTS, dtype=I32)[:, None, None]
    pos = jnp.sum(jnp.where(ri[0:2][None] == eids, row_start[:, None, None], 0), axis=0) + ri[2:4]
    pos3 = pos.reshape(2, t // TM_OUT, TM_OUT).transpose(1, 0, 2)

    xs = _dispatch(pos3, last_tile, hp, max_tiles * TMX)
    ys = _experts(tile_start.astype(I32), tiles_e.astype(I32), n_used.astype(I32), xs,
                  wg_b.reshape(N_EXPERTS, d, D_EXPERT), wu_b.reshape(N_EXPERTS, d, D_EXPERT),
                  wd_b.reshape(N_EXPERTS, D_EXPERT, d))
    out = _combine(pos3, ys, h1, rw.T, ln2_g[l].reshape(1, d), ln2_b[l].reshape(1, d))
    return out.reshape(b, s, d)
```

```python
import functools
import math

import jax
import jax.numpy as jnp
from jax import lax
from jax.experimental import pallas as pl
from jax.experimental.pallas import tpu as pltpu

F32 = jnp.float32
BF16 = jnp.bfloat16
U32 = jnp.uint32
I32 = jnp.int32

ATT_HEADS = 8
ATT_QK_DIM = 64
ATT_V_DIM = 128
ATT_WIDTH = ATT_HEADS * ATT_V_DIM
SG_GROUPS = 8
SG_DIM = 128
SG_WIDTH = SG_GROUPS * SG_DIM
SG_CHUNK = 128
REL_BUCKETS = 32
REL_MAX_DIST = 128
N_GROUPS = 4
EXPERTS_PER_GROUP = 8
N_EXPERTS = N_GROUPS * EXPERTS_PER_GROUP
D_EXPERT = 256
DEPTH = 1
DN_ALPHA = (2.0 * DEPTH) ** 0.25
LN_EPS = 1e-5
LOG2E = math.log2(math.e)

LANES = 128
SUBLANES = 8
VMEM_LIMIT_BYTES = 56 * 1024 * 1024

TM_IN = 1024
TN_IN = 1024
QB = 256
KC = 256
N_BIAS_TILES = 5
TM_MIX = 512
TMX = 256
TM_OUT = 256
ROUTE_ROWS = 8
NR_EXPERT0 = 8
NR_PAD = NR_EXPERT0 + 32


def _cparams(sem, flags=None):
    return pltpu.CompilerParams(dimension_semantics=sem, vmem_limit_bytes=VMEM_LIMIT_BYTES,
                                flags=flags)


def _inproj_kernel(x_ref, w_ref, o_ref, xb_ref, *, n_q, n_lin, q_scale):
    j = pl.program_id(1)

    @pl.when(j == 0)
    def _():
        xb_ref[...] = x_ref[...].astype(BF16)

    acc = jnp.dot(xb_ref[...], w_ref[...].astype(BF16), preferred_element_type=F32)

    @pl.when(j < n_q)
    def _():
        o_ref[...] = (acc * q_scale).astype(BF16)

    @pl.when(jnp.logical_and(j >= n_q, j < n_lin))
    def _():
        o_ref[...] = acc.astype(BF16)

    @pl.when(j >= n_lin)
    def _():
        o_ref[...] = jax.nn.gelu(acc).astype(BF16)


def _inproj(x2, w_in):
    t, d = x2.shape
    n = w_in.shape[1]
    n_q = (ATT_HEADS * 2 * ATT_QK_DIM) // TN_IN
    n_lin = (2 * ATT_HEADS * 2 * ATT_QK_DIM + ATT_WIDTH) // TN_IN
    kern = functools.partial(_inproj_kernel, n_q=n_q, n_lin=n_lin,
                             q_scale=(ATT_QK_DIM ** -0.5) * LOG2E)
    return pl.pallas_call(
        kern,
        grid=(t // TM_IN, n // TN_IN),
        in_specs=[pl.BlockSpec((TM_IN, d), lambda i, j: (i, 0)),
                  pl.BlockSpec((d, TN_IN), lambda i, j: (0, j))],
        out_specs=pl.BlockSpec((TM_IN, TN_IN), lambda i, j: (i, j)),
        out_shape=jax.ShapeDtypeStruct((t, n), BF16),
        scratch_shapes=[pltpu.VMEM((TM_IN, d), BF16)],
        compiler_params=_cparams(("arbitrary", "arbitrary")),
        name="inproj",
    )(x2, w_in)


def _attn_kernel(lam_ref, q_ref, k_ref, v_ref, bseq_ref, g_ref, *rest, n_kc, out_scale):
    n_side = (len(rest) - 3) // 2
    side_in, o_ref, side_out = rest[:n_side], rest[n_side], rest[n_side + 1:2 * n_side + 1]
    vaug_ref, bias_ref = rest[2 * n_side + 1:]
    qi = pl.program_id(2)

    @pl.when(qi == 0)
    def _():
        vaug_ref[:, :ATT_V_DIM] = v_ref[0]
        vaug_ref[:, ATT_V_DIM:] = jnp.ones((vaug_ref.shape[0], ATT_V_DIM), BF16)
        for t in range(N_BIAS_TILES):
            seq = jnp.broadcast_to(bseq_ref[0, t:t + 1, :], (QB, 2 * KC))
            bias_ref[t] = pltpu.roll(seq, 0, 1, stride=1, stride_axis=0)[:, :KC]

    q = q_ref[0]
    lane = lax.broadcasted_iota(I32, q.shape, 1)
    zero = jnp.zeros_like(q)
    qq = jnp.concatenate([jnp.where(lane < ATT_QK_DIM, q, zero),
                          jnp.where(lane >= ATT_QK_DIM, q, zero)], axis=0)

    for src, dst in zip(side_in, side_out):
        dst[...] = src[...].astype(dst.dtype)

    m = jnp.full((2 * QB, 1), -jnp.inf, F32)
    acc = jnp.zeros((2 * QB, 2 * ATT_V_DIM), F32)
    for kc in range(n_kc):
        kblk = k_ref[0, kc * KC:(kc + 1) * KC, :]
        s = lax.dot_general(qq, kblk, (((1,), (1,)), ((), ())),
                            preferred_element_type=F32)
        b = bias_ref[jnp.clip(kc - qi, -2, 2) + 2]
        s = s + jnp.concatenate([b, b], axis=0)
        m_new = jnp.maximum(m, jnp.max(s, axis=1, keepdims=True))
        p = jnp.exp2(s - m_new).astype(BF16)
        acc = acc * jnp.exp2(m - m_new) + jnp.dot(
            p, vaug_ref[kc * KC:(kc + 1) * KC, :], preferred_element_type=F32)
        m = m_new
    lam = lam_ref[0]
    o0 = acc[:QB, :ATT_V_DIM] / acc[:QB, ATT_V_DIM:ATT_V_DIM + 1]
    o1 = acc[QB:, :ATT_V_DIM] / acc[QB:, ATT_V_DIM:ATT_V_DIM + 1]
    of = o0 - lam * o1
    of = of * lax.rsqrt(jnp.mean(of * of, axis=-1, keepdims=True) + LN_EPS)
    o_ref[0] = (of * (g_ref[...] * out_scale)).astype(o_ref.dtype)


def _attention(proj3, bias_seq, lam, subln_g, lam_init, side):
    b, s, _ = proj3.shape
    n_kc = s // KC
    n_q = s // QB
    n_steps = b * ATT_HEADS * n_q
    h_off_k = ATT_HEADS
    h_off_v = 2 * ATT_HEADS
    kern = functools.partial(_attn_kernel, n_kc=n_kc, out_scale=1.0 - lam_init)

    def side_map(bi, h, qi):
        return ((bi * ATT_HEADS + h) * n_q + qi, 0)

    side_specs = []
    for arr in side:
        rows, cols = arr.shape
        assert rows % (n_steps * 2 * SUBLANES) == 0
        side_specs.append(pl.BlockSpec((rows // n_steps, cols), side_map))
    return pl.pallas_call(
        kern,
        grid=(b, ATT_HEADS, n_q),
        in_specs=[
            pl.BlockSpec(memory_space=pltpu.SMEM),
            pl.BlockSpec((1, QB, LANES), lambda bi, h, qi: (bi, qi, h)),
            pl.BlockSpec((1, s, LANES), lambda bi, h, qi: (bi, 0, h_off_k + h)),
            pl.BlockSpec((1, s, LANES), lambda bi, h, qi: (bi, 0, h_off_v + h)),
            pl.BlockSpec((1, N_BIAS_TILES, 2 * KC), lambda bi, h, qi: (h, 0, 0)),
            pl.BlockSpec((1, ATT_V_DIM), lambda bi, h, qi: (0, 0)),
        ] + side_specs,
        out_specs=[pl.BlockSpec((1, QB, ATT_V_DIM), lambda bi, h, qi: (bi, qi, h))] + side_specs,
        out_shape=[jax.ShapeDtypeStruct((b, s, ATT_WIDTH), BF16)]
        + [jax.ShapeDtypeStruct(arr.shape, BF16) for arr in side],
        scratch_shapes=[pltpu.VMEM((s, 2 * ATT_V_DIM), BF16),
                        pltpu.VMEM((N_BIAS_TILES, QB, KC), F32)],
        compiler_params=_cparams(("arbitrary", "arbitrary", "arbitrary")),
        name="attn",
    )(lam, proj3, proj3, proj3, bias_seq, subln_g, *side)


def _rel_bucket(rel):
    half = REL_BUCKETS // 2
    max_exact = half // 2
    ret = jnp.where(rel > 0, half, 0)
    n = jnp.abs(rel)
    nf = jnp.maximum(n, 1).astype(F32)
    large = max_exact + (jnp.log(nf / max_exact) / math.log(REL_MAX_DIST / max_exact)
                         * (half - max_exact)).astype(I32)
    large = jnp.minimum(large, half - 1)
    return ret + jnp.where(n < max_exact, n, large)


def _bias_sequences(rel_bias, s):
    assert REL_MAX_DIST <= KC and QB == KC
    period = 2 * KC
    j = jnp.arange(period, dtype=I32)
    off = jnp.where(j < KC, j, j - period)
    d = jnp.arange(-2, 3, dtype=I32)[:, None]
    rel = jnp.clip(d * KC + off[None, :], -(s - 1), s - 1)
    return jnp.transpose(rel_bias[_rel_bucket(rel)], (2, 0, 1)).astype(F32) * LOG2E


def _layernorm(v, g, b):
    mu = jnp.mean(v, axis=-1, keepdims=True)
    var = jnp.mean(jnp.square(v - mu), axis=-1, keepdims=True)
    return (v - mu) * lax.rsqrt(var + LN_EPS) * g + b


def _pack_bf16_pair(v):
    n = v.shape[1] // 2
    bits = lax.bitcast_convert_type(v.astype(BF16).astype(F32), U32)
    return (bits[:, :n] >> 16) | (bits[:, n:] & jnp.uint32(0xFFFF0000))


def _unpack_bf16_pair(w):
    lo = lax.bitcast_convert_type(w << 16, F32)
    hi = lax.bitcast_convert_type(w & jnp.uint32(0xFFFF0000), F32)
    return lo, hi


def _first_argmax(vals, nrows):
    mx = jnp.max(vals, axis=0, keepdims=True)
    row = lax.broadcasted_iota(I32, vals.shape, 0)
    idx = jnp.min(jnp.where(vals == mx, row, nrows), axis=0, keepdims=True)
    return mx, idx


def _mix_kernel(a_ref, u_ref, vg_ref, x_ref, wout_ref, sgw_ref, sgb_ref, sgg_ref, sgbeta_ref,
                g1_ref, b1_ref, wr_ref, br_ref,
                h1_ref, hp_ref, ri_ref, rw_ref, cnt_ref, run_ref):
    i = pl.program_id(0)

    @pl.when(i == 0)
    def _():
        run_ref[...] = jnp.zeros_like(run_ref)

    n_chunks = TM_MIX // SG_CHUNK
    rows = []
    for c in range(n_chunks):
        cols = []
        for g in range(SG_GROUPS):
            rs = slice(c * SG_CHUNK, (c + 1) * SG_CHUNK)
            cs = slice(g * SG_DIM, (g + 1) * SG_DIM)
            vt = vg_ref[rs, cs].astype(F32)
            vn = _layernorm(vt, sgg_ref[g:g + 1, :], sgbeta_ref[g:g + 1, :])
            mixed = jnp.dot(sgw_ref[g], vn.astype(BF16), preferred_element_type=F32) + sgb_ref[g]
            cols.append((u_ref[rs, cs].astype(F32) * mixed).astype(BF16))
        rows.append(jnp.concatenate(cols, axis=1))
    sgate = jnp.concatenate(rows, axis=0)

    mix = jnp.dot(a_ref[...], wout_ref[:ATT_WIDTH, :], preferred_element_type=F32)
    mix = mix + jnp.dot(sgate, wout_ref[ATT_WIDTH:, :], preferred_element_type=F32)
    h1 = _layernorm(DN_ALPHA * x_ref[...] + mix, g1_ref[...], b1_ref[...])
    h1_ref[...] = h1
    hp_ref[...] = _pack_bf16_pair(h1)

    lt = lax.dot_general(wr_ref[...], h1, (((1,), (1,)), ((), ())),
                         preferred_element_type=F32, precision=lax.Precision.HIGHEST)
    lt = lt + br_ref[...]
    g_logits = lt[0:N_GROUPS]
    gmax, g_idx = _first_argmax(g_logits, N_GROUPS)
    g_gate = 1.0 / jnp.sum(jnp.exp(g_logits - gmax), axis=0, keepdims=True)
    e_logits = jnp.zeros((EXPERTS_PER_GROUP, TM_MIX), F32)
    for g in range(N_GROUPS):
        lo = NR_EXPERT0 + g * EXPERTS_PER_GROUP
        e_logits = jnp.where(g_idx == g, lt[lo:lo + EXPERTS_PER_GROUP], e_logits)
    v1, i1 = _first_argmax(e_logits, EXPERTS_PER_GROUP)
    erow = lax.broadcasted_iota(I32, e_logits.shape, 0)
    v2, i2 = _first_argmax(jnp.where(erow == i1, -jnp.inf, e_logits), EXPERTS_PER_GROUP)
    t = jnp.exp(v2 - v1)
    w1 = g_gate / (1.0 + t)
    w2 = g_gate * t / (1.0 + t)
    e1 = g_idx * EXPERTS_PER_GROUP + i1
    e2 = g_idx * EXPERTS_PER_GROUP + i2

    xrow = lax.broadcasted_iota(I32, (N_EXPERTS, TM_MIX), 0)
    oh1 = xrow == e1
    oh2 = xrow == e2
    oh = jnp.where(jnp.logical_or(oh1, oh2), 1.0, 0.0)
    tr = lax.broadcasted_iota(I32, (TM_MIX, TM_MIX), 0)
    tc = lax.broadcasted_iota(I32, (TM_MIX, TM_MIX), 1)
    before = jnp.where(tr < tc, 1.0, 0.0).astype(BF16)
    cum = jnp.dot(oh.astype(BF16), before, preferred_element_type=F32)
    base = cum + run_ref[:, 0:1]
    rank1 = jnp.sum(jnp.where(oh1, base, 0.0), axis=0, keepdims=True).astype(I32)
    rank2 = jnp.sum(jnp.where(oh2, base, 0.0), axis=0, keepdims=True).astype(I32)
    run_ref[...] = run_ref[...] + jnp.sum(oh, axis=1, keepdims=True)
    cnt_ref[...] = run_ref[...].astype(I32)

    zi = jnp.zeros((ROUTE_ROWS - 4, TM_MIX), I32)
    ri_ref[...] = jnp.concatenate([e1, e2, rank1, rank2, zi], axis=0)
    zw = jnp.zeros((ROUTE_ROWS - 2, TM_MIX), F32)
    rw_ref[...] = jnp.concatenate([w1, w2, zw], axis=0)


def _mix(a2, proj2, x2, wout_b, sgw_b, sgb, sgg, sgbeta, g1, b1, wr, br):
    t, d = x2.shape
    u_blk = (proj2.shape[1] - 2 * SG_WIDTH) // SG_WIDTH
    const = lambda *shape: pl.BlockSpec(shape, lambda i: (0,) * len(shape))
    return pl.pallas_call(
        _mix_kernel,
        grid=(t // TM_MIX,),
        in_specs=[
            pl.BlockSpec((TM_MIX, ATT_WIDTH), lambda i: (i, 0)),
            pl.BlockSpec((TM_MIX, SG_WIDTH), lambda i: (i, u_blk)),
            pl.BlockSpec((TM_MIX, SG_WIDTH), lambda i: (i, u_blk + 1)),
            pl.BlockSpec((TM_MIX, d), lambda i: (i, 0)),
            pl.BlockSpec((d, d), lambda i: (0, 0), pipeline_mode=pl.Buffered(1)),
            const(SG_GROUPS, SG_CHUNK, SG_CHUNK),
            const(SG_GROUPS, SG_CHUNK, 1),
            const(SG_GROUPS, SG_DIM),
            const(SG_GROUPS, SG_DIM),
            const(1, d),
            const(1, d),
            const(NR_PAD, d),
            const(NR_PAD, 1),
        ],
        out_specs=[
            pl.BlockSpec((TM_MIX, d), lambda i: (i, 0)),
            pl.BlockSpec((TM_MIX, d // 2), lambda i: (i, 0)),
            pl.BlockSpec((ROUTE_ROWS, TM_MIX), lambda i: (0, i)),
            pl.BlockSpec((ROUTE_ROWS, TM_MIX), lambda i: (0, i)),
            const(N_EXPERTS, LANES),
        ],
        out_shape=[
            jax.ShapeDtypeStruct((t, d), F32),
            jax.ShapeDtypeStruct((t, d // 2), U32),
            jax.ShapeDtypeStruct((ROUTE_ROWS, t), I32),
            jax.ShapeDtypeStruct((ROUTE_ROWS, t), F32),
            jax.ShapeDtypeStruct((N_EXPERTS, LANES), I32),
        ],
        scratch_shapes=[pltpu.VMEM((N_EXPERTS, LANES), F32)],
        compiler_params=_cparams(("arbitrary",)),
        name="mix",
    )(a2, proj2, proj2, x2, wout_b, sgw_b, sgb, sgg, sgbeta, g1, b1, wr, br)


DISPATCH_CHUNK = 256


def _dispatch_kernel(pos_ref, last_ref, hp_ref, xs_hbm, zero_ref, sbuf, zsem, sem):
    i = pl.program_id(0)
    slot = i % 2

    def row_copy(sl, j, dst):
        return pltpu.make_async_copy(sbuf.at[sl, pl.ds(j, 1)], xs_hbm.at[pl.ds(dst, 1)],
                                     sem.at[sl])

    def zero_copy(e):
        start = pl.multiple_of(last_ref[e] * TMX, TMX)
        return pltpu.make_async_copy(zero_ref, xs_hbm.at[pl.ds(start, TMX)], zsem)

    @pl.when(i == 0)
    def _():
        zero_ref[...] = jnp.zeros_like(zero_ref)

        def zstart(e, c):
            zero_copy(e).start()
            return c

        def zwait(e, c):
            zero_copy(e).wait()
            return c

        lax.fori_loop(0, N_EXPERTS, zstart, 0)
        lax.fori_loop(0, N_EXPERTS, zwait, 0)

    sbuf[slot] = hp_ref[...]

    def start(j, c):
        row_copy(slot, j, pos_ref[0, j]).start(priority=0)
        row_copy(slot, j, pos_ref[1, j]).start(priority=1)
        return c

    def drain(sl):
        def wait(j, c):
            row_copy(sl, 0, 0).wait()
            row_copy(sl, 0, 0).wait()
            return c
        lax.fori_loop(0, DISPATCH_CHUNK, wait, 0, unroll=8)

    lax.fori_loop(0, DISPATCH_CHUNK, start, 0, unroll=8)

    @pl.when(i > 0)
    def _():
        drain(1 - slot)

    @pl.when(i == pl.num_programs(0) - 1)
    def _():
        drain(slot)


def _dispatch(pos3, last_tile, hp, n_rows):
    t, w = hp.shape
    return pl.pallas_call(
        _dispatch_kernel,
        grid=(t // DISPATCH_CHUNK,),
        in_specs=[
            pl.BlockSpec((None, 2, DISPATCH_CHUNK), lambda i: (i, 0, 0), memory_space=pltpu.SMEM),
            pl.BlockSpec(memory_space=pltpu.SMEM),
            pl.BlockSpec((DISPATCH_CHUNK, w), lambda i: (i, 0)),
        ],
        out_specs=pl.BlockSpec(memory_space=pl.ANY),
        out_shape=jax.ShapeDtypeStruct((n_rows, w), U32),
        scratch_shapes=[pltpu.VMEM((TMX, w), U32),
                        pltpu.VMEM((2, DISPATCH_CHUNK, w), U32),
                        pltpu.SemaphoreType.DMA(()),
                        pltpu.SemaphoreType.DMA((2,))],
        compiler_params=_cparams(("arbitrary",)),
        name="dispatch",
    )(pos3, last_tile, hp)


def _experts_kernel(ts_ref, nt_ref, nu_ref, x_hbm, wg_ref, wu_ref, wd_ref, y_hbm,
                    xbuf, ybuf, xsem, ysem):
    e = pl.program_id(0)
    d = wg_ref.shape[1]
    n_used = nu_ref[0]

    def x_copy(g, slot):
        rows = pl.ds(pl.multiple_of(g * TMX, TMX), TMX)
        return pltpu.make_async_copy(x_hbm.at[rows], xbuf.at[slot], xsem.at[slot])

    def y_copy(g, slot):
        rows = pl.ds(pl.multiple_of(g * TMX, TMX), TMX)
        return pltpu.make_async_copy(ybuf.at[slot], y_hbm.at[rows], ysem.at[slot])

    @pl.when(jnp.logical_and(e == 0, n_used > 0))
    def _():
        x_copy(0, 0).start(priority=1)

    def tile(t, carry):
        g = ts_ref[e] + t
        slot = g % 2
        x_copy(g, slot).wait()

        @pl.when(g + 1 < n_used)
        def _():
            x_copy(g + 1, 1 - slot).start(priority=1)

        @pl.when(g >= 2)
        def _():
            y_copy(g - 2, slot).wait()

        lo, hi = _unpack_bf16_pair(xbuf[slot])
        lo, hi = lo.astype(BF16), hi.astype(BF16)
        gate = jnp.dot(lo, wg_ref[0, :d // 2, :], preferred_element_type=F32)
        gate = gate + jnp.dot(hi, wg_ref[0, d // 2:, :], preferred_element_type=F32)
        up = jnp.dot(lo, wu_ref[0, :d // 2, :], preferred_element_type=F32)
        up = up + jnp.dot(hi, wu_ref[0, d // 2:, :], preferred_element_type=F32)
        hmid = (gate * jax.nn.sigmoid(gate) * up).astype(BF16)
        y = jnp.dot(hmid, wd_ref[0], preferred_element_type=F32)
        ybuf[slot] = _pack_bf16_pair(y)
        y_copy(g, slot).start(priority=1)
        return carry

    lax.fori_loop(0, nt_ref[e], tile, 0)

    @pl.when(e == pl.num_programs(0) - 1)
    def _():
        @pl.when(n_used >= 2)
        def _():
            y_copy(n_used - 2, (n_used - 2) % 2).wait()

        @pl.when(n_used >= 1)
        def _():
            y_copy(n_used - 1, (n_used - 1) % 2).wait()


def _experts(tile_start, n_tiles_e, n_used, xs, w_gate, w_up, w_down):
    n_rows, w = xs.shape
    d = w_gate.shape[1]

    def w_map(e, ts, nt, nu):
        return (e, 0, 0)

    grid_spec = pltpu.PrefetchScalarGridSpec(
        num_scalar_prefetch=3,
        grid=(N_EXPERTS,),
        in_specs=[
            pl.BlockSpec(memory_space=pl.ANY),
            pl.BlockSpec((1, d, D_EXPERT), w_map),
            pl.BlockSpec((1, d, D_EXPERT), w_map),
            pl.BlockSpec((1, D_EXPERT, d), w_map),
        ],
        out_specs=pl.BlockSpec(memory_space=pl.ANY),
        scratch_shapes=[pltpu.VMEM((2, TMX, w), U32),
                        pltpu.VMEM((2, TMX, w), U32),
                        pltpu.SemaphoreType.DMA((2,)),
                        pltpu.SemaphoreType.DMA((2,))],
    )
    return pl.pallas_call(
        _experts_kernel,
        grid_spec=grid_spec,
        out_shape=jax.ShapeDtypeStruct((n_rows, w), U32),
        compiler_params=_cparams(("arbitrary",)),
        name="experts",
    )(tile_start, n_tiles_e, n_used, xs, w_gate, w_up, w_down)


def _combine_kernel(pos_ref, posn_ref, y_hbm, h1_ref, wt_ref, g2_ref, b2_ref, o_ref, ybuf, sem):
    i = pl.program_id(0)
    n = pl.num_programs(0)
    slot = i % 2

    def row_copy(src, sl, which, j):
        return pltpu.make_async_copy(y_hbm.at[pl.ds(src, 1)],
                                     ybuf.at[sl, which, pl.ds(j, 1)], sem.at[sl])

    def issue(p_ref, sl):
        def body(j, c):
            row_copy(p_ref[0, j], sl, 0, j).start(priority=0)
            row_copy(p_ref[1, j], sl, 1, j).start(priority=1)
            return c
        lax.fori_loop(0, TM_OUT, body, 0, unroll=8)

    @pl.when(i == 0)
    def _():
        issue(pos_ref, 0)

    @pl.when(i + 1 < n)
    def _():
        issue(posn_ref, 1 - slot)

    def wait(j, c):
        row_copy(pos_ref[0, j], slot, 0, j).wait()
        row_copy(pos_ref[1, j], slot, 1, j).wait()
        return c

    lax.fori_loop(0, TM_OUT, wait, 0, unroll=8)

    lo1, hi1 = _unpack_bf16_pair(ybuf[slot, 0])
    lo2, hi2 = _unpack_bf16_pair(ybuf[slot, 1])
    w1 = wt_ref[:, 0:1]
    w2 = wt_ref[:, 1:2]
    ffn = jnp.concatenate([w1 * lo1 + w2 * lo2, w1 * hi1 + w2 * hi2], axis=1)
    o_ref[...] = _layernorm(DN_ALPHA * h1_ref[...] + ffn, g2_ref[...], b2_ref[...])


def _combine(pos3, ys, h1, wtok, g2, b2):
    t, d = h1.shape
    w = ys.shape[1]
    n = t // TM_OUT
    return pl.pallas_call(
        _combine_kernel,
        grid=(n,),
        in_specs=[
            pl.BlockSpec((None, 2, TM_OUT), lambda i: (i, 0, 0), memory_space=pltpu.SMEM),
            pl.BlockSpec((None, 2, TM_OUT), lambda i: (jnp.minimum(i + 1, n - 1), 0, 0),
                         memory_space=pltpu.SMEM),
            pl.BlockSpec(memory_space=pl.ANY),
            pl.BlockSpec((TM_OUT, d), lambda i: (i, 0)),
            pl.BlockSpec((TM_OUT, ROUTE_ROWS), lambda i: (i, 0)),
            pl.BlockSpec((1, d), lambda i: (0, 0)),
            pl.BlockSpec((1, d), lambda i: (0, 0)),
        ],
        out_specs=pl.BlockSpec((TM_OUT, d), lambda i: (i, 0)),
        out_shape=jax.ShapeDtypeStruct((t, d), F32),
        scratch_shapes=[pltpu.VMEM((2, 2, TM_OUT, w), U32),
                        pltpu.SemaphoreType.DMA((2,))],
        compiler_params=_cparams(("arbitrary",)),
        name="combine",
    )(pos3, pos3, ys, h1, wtok, g2, b2)


def _lambda_init(layer):
    return 0.8 - 0.6 * math.exp(-0.3 * layer)


def kernel(x, w_in, w_out, ln1_g, ln1_b, ln2_g, ln2_b, rel_bias, lam_q1, lam_k1, lam_q2, lam_k2,
           subln_g, sg_ln_g, sg_ln_b, sg_w, sg_b, w_router_group, b_router_group,
           w_router_expert, b_router_expert, w_exp_gate, w_exp_up, w_exp_down):
    b, s, d = x.shape
    t = b * s
    assert w_in.shape[0] == DEPTH == 1
    l = 0
    lam_init = _lambda_init(l)
    x2 = x.reshape(t, d)

    proj = _inproj(x2, w_in[l])

    lam = (jnp.exp(jnp.sum(lam_q1[l].astype(F32) * lam_k1[l].astype(F32)))
           - jnp.exp(jnp.sum(lam_q2[l].astype(F32) * lam_k2[l].astype(F32))) + lam_init)
    side = [w_exp_gate[l].reshape(-1, D_EXPERT), w_exp_up[l].reshape(-1, D_EXPERT),
            w_exp_down[l].reshape(-1, d)]
    a, wg_b, wu_b, wd_b = _attention(
        proj.reshape(b, s, -1), _bias_sequences(rel_bias, s), lam.reshape(1),
        subln_g[l].reshape(1, -1), lam_init, side)

    wr = jnp.concatenate([w_router_group[l].T,
                          jnp.zeros((NR_EXPERT0 - N_GROUPS, d), F32),
                          jnp.transpose(w_router_expert[l], (0, 2, 1)).reshape(N_EXPERTS, d)],
                         axis=0)
    br = jnp.concatenate([b_router_group[l], jnp.zeros((NR_EXPERT0 - N_GROUPS,), F32),
                          b_router_expert[l].reshape(-1)]).reshape(NR_PAD, 1)
    h1, hp, ri, rw, cnt = _mix(
        a.reshape(t, -1), proj, x2, w_out[l].astype(BF16), sg_w[l].astype(BF16),
        sg_b[l].reshape(SG_GROUPS, SG_CHUNK, 1), sg_ln_g[l], sg_ln_b[l],
        ln1_g[l].reshape(1, d), ln1_b[l].reshape(1, d), wr, br)

    counts = cnt[:, 0]
    tiles_e = (counts + TMX - 1) // TMX
    tile_end = jnp.cumsum(tiles_e)
    tile_start = tile_end - tiles_e
    n_used = tile_end[-1:]
    max_tiles = (2 * t) // TMX + N_EXPERTS
    last_tile = jnp.maximum(tile_end - 1, 0).astype(I32)
    row_start = (tile_start * TMX).astype(I32)
    eids = jnp.arange(N_EXPERTS, dtype=I32)[:, None, None]
    pos = jnp.sum(jnp.where(ri[0:2][None] == eids, row_start[:, None, None], 0), axis=0) + ri[2:4]
    pos3 = pos.reshape(2, t // TM_OUT, TM_OUT).transpose(1, 0, 2)

    xs = _dispatch(pos3, last_tile, hp, max_tiles * TMX)
    ys = _experts(tile_start.astype(I32), tiles_e.astype(I32), n_used.astype(I32), xs,
                  wg_b.reshape(N_EXPERTS, d, D_EXPERT), wu_b.reshape(N_EXPERTS, d, D_EXPERT),
                  wd_b.reshape(N_EXPERTS, D_EXPERT, d))
    out = _combine(pos3, ys, h1, rw.T, ln2_g[l].reshape(1, d), ln2_b[l].reshape(1, d))
    return out.reshape(b, s, d)
```

```python
import functools
import math

import jax
import jax.numpy as jnp
from jax import lax
from jax.experimental import pallas as pl
from jax.experimental.pallas import tpu as pltpu

F32 = jnp.float32
BF16 = jnp.bfloat16
U32 = jnp.uint32
I32 = jnp.int32

ATT_HEADS = 8
ATT_QK_DIM = 64
ATT_V_DIM = 128
ATT_WIDTH = ATT_HEADS * ATT_V_DIM
SG_GROUPS = 8
SG_DIM = 128
SG_WIDTH = SG_GROUPS * SG_DIM
SG_CHUNK = 128
REL_BUCKETS = 32
REL_MAX_DIST = 128
N_GROUPS = 4
EXPERTS_PER_GROUP = 8
N_EXPERTS = N_GROUPS * EXPERTS_PER_GROUP
D_EXPERT = 256
DEPTH = 1
DN_ALPHA = (2.0 * DEPTH) ** 0.25
LN_EPS = 1e-5
LOG2E = math.log2(math.e)

LANES = 128
SUBLANES = 8
VMEM_LIMIT_BYTES = 56 * 1024 * 1024

TM_IN = 1024
TN_IN = 1024
QB = 256
KC = 256
N_BIAS_TILES = 5
TM_MIX = 512
TMX = 256
TM_OUT = 256
ROUTE_ROWS = 8
NR_EXPERT0 = 8
NR_PAD = NR_EXPERT0 + 32


def _cparams(sem, flags=None):
    return pltpu.CompilerParams(dimension_semantics=sem, vmem_limit_bytes=VMEM_LIMIT_BYTES,
                                flags=flags)


def _inproj_kernel(x_ref, w_ref, o_ref, xb_ref, *, n_q, n_lin, q_scale):
    j = pl.program_id(1)

    @pl.when(j == 0)
    def _():
        xb_ref[...] = x_ref[...].astype(BF16)

    acc = jnp.dot(xb_ref[...], w_ref[...].astype(BF16), preferred_element_type=F32)

    @pl.when(j < n_q)
    def _():
        o_ref[...] = (acc * q_scale).astype(BF16)

    @pl.when(jnp.logical_and(j >= n_q, j < n_lin))
    def _():
        o_ref[...] = acc.astype(BF16)

    @pl.when(j >= n_lin)
    def _():
        o_ref[...] = jax.nn.gelu(acc).astype(BF16)


def _inproj(x2, w_in):
    t, d = x2.shape
    n = w_in.shape[1]
    n_q = (ATT_HEADS * 2 * ATT_QK_DIM) // TN_IN
    n_lin = (2 * ATT_HEADS * 2 * ATT_QK_DIM + ATT_WIDTH) // TN_IN
    kern = functools.partial(_inproj_kernel, n_q=n_q, n_lin=n_lin,
                             q_scale=(ATT_QK_DIM ** -0.5) * LOG2E)
    return pl.pallas_call(
        kern,
        grid=(t // TM_IN, n // TN_IN),
        in_specs=[pl.BlockSpec((TM_IN, d), lambda i, j: (i, 0)),
                  pl.BlockSpec((d, TN_IN), lambda i, j: (0, j))],
        out_specs=pl.BlockSpec((TM_IN, TN_IN), lambda i, j: (i, j)),
        out_shape=jax.ShapeDtypeStruct((t, n), BF16),
        scratch_shapes=[pltpu.VMEM((TM_IN, d), BF16)],
        compiler_params=_cparams(("arbitrary", "arbitrary")),
        name="inproj",
    )(x2, w_in)


def _attn_kernel(lam_ref, q_ref, k_ref, v_ref, bseq_ref, g_ref, *rest, n_kc, out_scale):
    n_side = (len(rest) - 3) // 2
    side_in, o_ref, side_out = rest[:n_side], rest[n_side], rest[n_side + 1:2 * n_side + 1]
    vaug_ref, bias_ref = rest[2 * n_side + 1:]
    qi = pl.program_id(2)

    @pl.when(qi == 0)
    def _():
        vaug_ref[:, :ATT_V_DIM] = v_ref[0]
        vaug_ref[:, ATT_V_DIM:] = jnp.ones((vaug_ref.shape[0], ATT_V_DIM), BF16)
        for t in range(N_BIAS_TILES):
            seq = jnp.broadcast_to(bseq_ref[0, t:t + 1, :], (QB, 2 * KC))
            bias_ref[t] = pltpu.roll(seq, 0, 1, stride=1, stride_axis=0)[:, :KC]

    q = q_ref[0]
    lane = lax.broadcasted_iota(I32, q.shape, 1)
    zero = jnp.zeros_like(q)
    qq = jnp.concatenate([jnp.where(lane < ATT_QK_DIM, q, zero),
                          jnp.where(lane >= ATT_QK_DIM, q, zero)], axis=0)

    for src, dst in zip(side_in, side_out):
        dst[...] = src[...].astype(dst.dtype)

    m = jnp.full((2 * QB, 1), -jnp.inf, F32)
    acc = jnp.zeros((2 * QB, 2 * ATT_V_DIM), F32)
    for kc in range(n_kc):
        kblk = k_ref[0, kc * KC:(kc + 1) * KC, :]
        s = lax.dot_general(qq, kblk, (((1,), (1,)), ((), ())),
                            preferred_element_type=F32)
        b = bias_ref[jnp.clip(kc - qi, -2, 2) + 2]
        s = s + jnp.concatenate([b, b], axis=0)
        m_new = jnp.maximum(m, jnp.max(s, axis=1, keepdims=True))
        p = jnp.exp2(s - m_new).astype(BF16)
        acc = acc * jnp.exp2(m - m_new) + jnp.dot(
            p, vaug_ref[kc * KC:(kc + 1) * KC, :], preferred_element_type=F32)
        m = m_new
    lam = lam_ref[0]
    o0 = acc[:QB, :ATT_V_DIM] / acc[:QB, ATT_V_DIM:ATT_V_DIM + 1]
    o1 = acc[QB:, :ATT_V_DIM] / acc[QB:, ATT_V_DIM:ATT_V_DIM + 1]
    of = o0 - lam * o1
    of = of * lax.rsqrt(jnp.mean(of * of, axis=-1, keepdims=True) + LN_EPS)
    o_ref[0] = (of * (g_ref[...] * out_scale)).astype(o_ref.dtype)


def _attention(proj3, bias_seq, lam, subln_g, lam_init, side):
    b, s, _ = proj3.shape
    n_kc = s // KC
    n_q = s // QB
    n_steps = b * ATT_HEADS * n_q
    h_off_k = ATT_HEADS
    h_off_v = 2 * ATT_HEADS
    kern = functools.partial(_attn_kernel, n_kc=n_kc, out_scale=1.0 - lam_init)

    def side_map(bi, h, qi):
        return ((bi * ATT_HEADS + h) * n_q + qi, 0)

    side_specs = []
    for arr in side:
        rows, cols = arr.shape
        assert rows % (n_steps * 2 * SUBLANES) == 0
        side_specs.append(pl.BlockSpec((rows // n_steps, cols), side_map))
    return pl.pallas_call(
        kern,
        grid=(b, ATT_HEADS, n_q),
        in_specs=[
            pl.BlockSpec(memory_space=pltpu.SMEM),
            pl.BlockSpec((1, QB, LANES), lambda bi, h, qi: (bi, qi, h)),
            pl.BlockSpec((1, s, LANES), lambda bi, h, qi: (bi, 0, h_off_k + h)),
            pl.BlockSpec((1, s, LANES), lambda bi, h, qi: (bi, 0, h_off_v + h)),
            pl.BlockSpec((1, N_BIAS_TILES, 2 * KC), lambda bi, h, qi: (h, 0, 0)),
            pl.BlockSpec((1, ATT_V_DIM), lambda bi, h, qi: (0, 0)),
        ] + side_specs,
        out_specs=[pl.BlockSpec((1, QB, ATT_V_DIM), lambda bi, h, qi: (bi, qi, h))] + side_specs,
        out_shape=[jax.ShapeDtypeStruct((b, s, ATT_WIDTH), BF16)]
        + [jax.ShapeDtypeStruct(arr.shape, BF16) for arr in side],
        scratch_shapes=[pltpu.VMEM((s, 2 * ATT_V_DIM), BF16),
                        pltpu.VMEM((N_BIAS_TILES, QB, KC), F32)],
        compiler_params=_cparams(("arbitrary", "arbitrary", "arbitrary")),
        name="attn",
    )(lam, proj3, proj3, proj3, bias_seq, subln_g, *side)


def _rel_bucket(rel):
    half = REL_BUCKETS // 2
    max_exact = half // 2
    ret = jnp.where(rel > 0, half, 0)
    n = jnp.abs(rel)
    nf = jnp.maximum(n, 1).astype(F32)
    large = max_exact + (jnp.log(nf / max_exact) / math.log(REL_MAX_DIST / max_exact)
                         * (half - max_exact)).astype(I32)
    large = jnp.minimum(large, half - 1)
    return ret + jnp.where(n < max_exact, n, large)


def _bias_sequences(rel_bias, s):
    assert REL_MAX_DIST <= KC and QB == KC
    period = 2 * KC
    j = jnp.arange(period, dtype=I32)
    off = jnp.where(j < KC, j, j - period)
    d = jnp.arange(-2, 3, dtype=I32)[:, None]
    rel = jnp.clip(d * KC + off[None, :], -(s - 1), s - 1)
    return jnp.transpose(rel_bias[_rel_bucket(rel)], (2, 0, 1)).astype(F32) * LOG2E


def _layernorm(v, g, b):
    mu = jnp.mean(v, axis=-1, keepdims=True)
    var = jnp.mean(jnp.square(v - mu), axis=-1, keepdims=True)
    return (v - mu) * lax.rsqrt(var + LN_EPS) * g + b


def _pack_bf16_pair(v):
    n = v.shape[1] // 2
    bits = lax.bitcast_convert_type(v.astype(BF16).astype(F32), U32)
    return (bits[:, :n] >> 16) | (bits[:, n:] & jnp.uint32(0xFFFF0000))


def _unpack_bf16_pair(w):
    lo = lax.bitcast_convert_type(w << 16, F32)
    hi = lax.bitcast_convert_type(w & jnp.uint32(0xFFFF0000), F32)
    return lo, hi


def _first_argmax(vals, nrows):
    mx = jnp.max(vals, axis=0, keepdims=True)
    row = lax.broadcasted_iota(I32, vals.shape, 0)
    idx = jnp.min(jnp.where(vals == mx, row, nrows), axis=0, keepdims=True)
    return mx, idx


def _mix_kernel(a_ref, u_ref, vg_ref, x_ref, wout_ref, sgw_ref, sgb_ref, sgg_ref, sgbeta_ref,
                g1_ref, b1_ref, wr_ref, br_ref,
                h1_ref, hp_ref, ri_ref, rw_ref, cnt_ref, run_ref):
    i = pl.program_id(0)

    @pl.when(i == 0)
    def _():
        run_ref[...] = jnp.zeros_like(run_ref)

    n_chunks = TM_MIX // SG_CHUNK
    rows = []
    for c in range(n_chunks):
        cols = []
        for g in range(SG_GROUPS):
            rs = slice(c * SG_CHUNK, (c + 1) * SG_CHUNK)
            cs = slice(g * SG_DIM, (g + 1) * SG_DIM)
            vt = vg_ref[rs, cs].astype(F32)
            vn = _layernorm(vt, sgg_ref[g:g + 1, :], sgbeta_ref[g:g + 1, :])
            mixed = jnp.dot(sgw_ref[g], vn.astype(BF16), preferred_element_type=F32) + sgb_ref[g]
            cols.append((u_ref[rs, cs].astype(F32) * mixed).astype(BF16))
        rows.append(jnp.concatenate(cols, axis=1))
    sgate = jnp.concatenate(rows, axis=0)

    mix = jnp.dot(a_ref[...], wout_ref[:ATT_WIDTH, :], preferred_element_type=F32)
    mix = mix + jnp.dot(sgate, wout_ref[ATT_WIDTH:, :], preferred_element_type=F32)
    h1 = _layernorm(DN_ALPHA * x_ref[...] + mix, g1_ref[...], b1_ref[...])
    h1_ref[...] = h1
    hp_ref[...] = _pack_bf16_pair(h1)

    lt = lax.dot_general(wr_ref[...], h1, (((1,), (1,)), ((), ())),
                         preferred_element_type=F32, precision=lax.Precision.HIGHEST)
    lt = lt + br_ref[...]
    g_logits = lt[0:N_GROUPS]
    gmax, g_idx = _first_argmax(g_logits, N_GROUPS)
    g_gate = 1.0 / jnp.sum(jnp.exp(g_logits - gmax), axis=0, keepdims=True)
    e_logits = jnp.zeros((EXPERTS_PER_GROUP, TM_MIX), F32)
    for g in range(N_GROUPS):
        lo = NR_EXPERT0 + g * EXPERTS_PER_GROUP
        e_logits = jnp.where(g_idx == g, lt[lo:lo + EXPERTS_PER_GROUP], e_logits)
    v1, i1 = _first_argmax(e_logits, EXPERTS_PER_GROUP)
    erow = lax.broadcasted_iota(I32, e_logits.shape, 0)
    v2, i2 = _first_argmax(jnp.where(erow == i1, -jnp.inf, e_logits), EXPERTS_PER_GROUP)
    t = jnp.exp(v2 - v1)
    w1 = g_gate / (1.0 + t)
    w2 = g_gate * t / (1.0 + t)
    e1 = g_idx * EXPERTS_PER_GROUP + i1
    e2 = g_idx * EXPERTS_PER_GROUP + i2

    xrow = lax.broadcasted_iota(I32, (N_EXPERTS, TM_MIX), 0)
    oh1 = xrow == e1
    oh2 = xrow == e2
    oh = jnp.where(jnp.logical_or(oh1, oh2), 1.0, 0.0)
    tr = lax.broadcasted_iota(I32, (TM_MIX, TM_MIX), 0)
    tc = lax.broadcasted_iota(I32, (TM_MIX, TM_MIX), 1)
    before = jnp.where(tr < tc, 1.0, 0.0).astype(BF16)
    cum = jnp.dot(oh.astype(BF16), before, preferred_element_type=F32)
    base = cum + run_ref[:, 0:1]
    rank1 = jnp.sum(jnp.where(oh1, base, 0.0), axis=0, keepdims=True).astype(I32)
    rank2 = jnp.sum(jnp.where(oh2, base, 0.0), axis=0, keepdims=True).astype(I32)
    run_ref[...] = run_ref[...] + jnp.sum(oh, axis=1, keepdims=True)
    cnt_ref[...] = run_ref[...].astype(I32)

    zi = jnp.zeros((ROUTE_ROWS - 4, TM_MIX), I32)
    ri_ref[...] = jnp.concatenate([e1, e2, rank1, rank2, zi], axis=0)
    zw = jnp.zeros((ROUTE_ROWS - 2, TM_MIX), F32)
    rw_ref[...] = jnp.concatenate([w1, w2, zw], axis=0)


def _mix(a2, proj2, x2, wout_b, sgw_b, sgb, sgg, sgbeta, g1, b1, wr, br):
    t, d = x2.shape
    u_blk = (proj2.shape[1] - 2 * SG_WIDTH) // SG_WIDTH
    const = lambda *shape: pl.BlockSpec(shape, lambda i: (0,) * len(shape))
    return pl.pallas_call(
        _mix_kernel,
        grid=(t // TM_MIX,),
        in_specs=[
            pl.BlockSpec((TM_MIX, ATT_WIDTH), lambda i: (i, 0)),
            pl.BlockSpec((TM_MIX, SG_WIDTH), lambda i: (i, u_blk)),
            pl.BlockSpec((TM_MIX, SG_WIDTH), lambda i: (i, u_blk + 1)),
            pl.BlockSpec((TM_MIX, d), lambda i: (i, 0)),
            pl.BlockSpec((d, d), lambda i: (0, 0), pipeline_mode=pl.Buffered(1)),
            const(SG_GROUPS, SG_CHUNK, SG_CHUNK),
            const(SG_GROUPS, SG_CHUNK, 1),
            const(SG_GROUPS, SG_DIM),
            const(SG_GROUPS, SG_DIM),
            const(1, d),
            const(1, d),
            const(NR_PAD, d),
            const(NR_PAD, 1),
        ],
        out_specs=[
            pl.BlockSpec((TM_MIX, d), lambda i: (i, 0)),
            pl.BlockSpec((TM_MIX, d // 2), lambda i: (i, 0)),
            pl.BlockSpec((ROUTE_ROWS, TM_MIX), lambda i: (0, i)),
            pl.BlockSpec((ROUTE_ROWS, TM_MIX), lambda i: (0, i)),
            const(N_EXPERTS, LANES),
        ],
        out_shape=[
            jax.ShapeDtypeStruct((t, d), F32),
            jax.ShapeDtypeStruct((t, d // 2), U32),
            jax.ShapeDtypeStruct((ROUTE_ROWS, t), I32),
            jax.ShapeDtypeStruct((ROUTE_ROWS, t), F32),
            jax.ShapeDtypeStruct((N_EXPERTS, LANES), I32),
        ],
        scratch_shapes=[pltpu.VMEM((N_EXPERTS, LANES), F32)],
        compiler_params=_cparams(("arbitrary",)),
        name="mix",
    )(a2, proj2, proj2, x2, wout_b, sgw_b, sgb, sgg, sgbeta, g1, b1, wr, br)


DISPATCH_CHUNK = 256


def _dispatch_kernel(pos_ref, last_ref, hp_ref, xs_hbm, zero_ref, sbuf, zsem, sem):
    i = pl.program_id(0)
    slot = i % 2

    def row_copy(sl, j, dst):
        return pltpu.make_async_copy(sbuf.at[sl, pl.ds(j, 1)], xs_hbm.at[pl.ds(dst, 1)],
                                     sem.at[sl])

    def zero_copy(e):
        start = pl.multiple_of(last_ref[e] * TMX, TMX)
        return pltpu.make_async_copy(zero_ref, xs_hbm.at[pl.ds(start, TMX)], zsem)

    @pl.when(i == 0)
    def _():
        zero_ref[...] = jnp.zeros_like(zero_ref)

        def zstart(e, c):
            zero_copy(e).start()
            return c

        def zwait(e, c):
            zero_copy(e).wait()
            return c

        lax.fori_loop(0, N_EXPERTS, zstart, 0)
        lax.fori_loop(0, N_EXPERTS, zwait, 0)

    sbuf[slot] = hp_ref[...]

    def start(j, c):
        row_copy(slot, j, pos_ref[0, j]).start(priority=0)
        row_copy(slot, j, pos_ref[1, j]).start(priority=1)
        return c

    def drain(sl):
        def wait(j, c):
            row_copy(sl, 0, 0).wait()
            row_copy(sl, 0, 0).wait()
            return c
        lax.fori_loop(0, DISPATCH_CHUNK, wait, 0, unroll=8)

    lax.fori_loop(0, DISPATCH_CHUNK, start, 0, unroll=8)

    @pl.when(i > 0)
    def _():
        drain(1 - slot)

    @pl.when(i == pl.num_programs(0) - 1)
    def _():
        drain(slot)


def _dispatch(pos3, last_tile, hp, n_rows):
    t, w = hp.shape
    return pl.pallas_call(
        _dispatch_kernel,
        grid=(t // DISPATCH_CHUNK,),
        in_specs=[
            pl.BlockSpec((None, 2, DISPATCH_CHUNK), lambda i: (i, 0, 0), memory_space=pltpu.SMEM),
            pl.BlockSpec(memory_space=pltpu.SMEM),
            pl.BlockSpec((DISPATCH_CHUNK, w), lambda i: (i, 0)),
        ],
        out_specs=pl.BlockSpec(memory_space=pl.ANY),
        out_shape=jax.ShapeDtypeStruct((n_rows, w), U32),
        scratch_shapes=[pltpu.VMEM((TMX, w), U32),
                        pltpu.VMEM((2, DISPATCH_CHUNK, w), U32),
                        pltpu.SemaphoreType.DMA(()),
                        pltpu.SemaphoreType.DMA((2,))],
        compiler_params=_cparams(("arbitrary",)),
        name="dispatch",
    )(pos3, last_tile, hp)


def _experts_kernel(ts_ref, nt_ref, nu_ref, x_hbm, wg_ref, wu_ref, wd_ref, y_hbm,
                    xbuf, ybuf, xsem, ysem):
    e = pl.program_id(0)
    d = wg_ref.shape[1]
    n_used = nu_ref[0]

    def x_copy(g, slot):
        rows = pl.ds(pl.multiple_of(g * TMX, TMX), TMX)
        return pltpu.make_async_copy(x_hbm.at[rows], xbuf.at[slot], xsem.at[slot])

    def y_copy(g, slot):
        rows = pl.ds(pl.multiple_of(g * TMX, TMX), TMX)
        return pltpu.make_async_copy(ybuf.at[slot], y_hbm.at[rows], ysem.at[slot])

    @pl.when(jnp.logical_and(e == 0, n_used > 0))
    def _():
        x_copy(0, 0).start(priority=1)

    def tile(t, carry):
        g = ts_ref[e] + t
        slot = g % 2
        x_copy(g, slot).wait()

        @pl.when(g + 1 < n_used)
        def _():
            x_copy(g + 1, 1 - slot).start(priority=1)

        @pl.when(g >= 2)
        def _():
            y_copy(g - 2, slot).wait()

        lo, hi = _unpack_bf16_pair(xbuf[slot])
        lo, hi = lo.astype(BF16), hi.astype(BF16)
        gate = jnp.dot(lo, wg_ref[0, :d // 2, :], preferred_element_type=F32)
        gate = gate + jnp.dot(hi, wg_ref[0, d // 2:, :], preferred_element_type=F32)
        up = jnp.dot(lo, wu_ref[0, :d // 2, :], preferred_element_type=F32)
        up = up + jnp.dot(hi, wu_ref[0, d // 2:, :], preferred_element_type=F32)
        hmid = (gate * jax.nn.sigmoid(gate) * up).astype(BF16)
        y = jnp.dot(hmid, wd_ref[0], preferred_element_type=F32)
        ybuf[slot] = _pack_bf16_pair(y)
        y_copy(g, slot).start(priority=1)
        return carry

    lax.fori_loop(0, nt_ref[e], tile, 0)

    @pl.when(e == pl.num_programs(0) - 1)
    def _():
        @pl.when(n_used >= 2)
        def _():
            y_copy(n_used - 2, (n_used - 2) % 2).wait()

        @pl.when(n_used >= 1)
        def _():
            y_copy(n_used - 1, (n_used - 1) % 2).wait()


def _experts(tile_start, n_tiles_e, n_used, xs, w_gate, w_up, w_down):
    n_rows, w = xs.shape
    d = w_gate.shape[1]

    def w_map(e, ts, nt, nu):
        return (e, 0, 0)

    grid_spec = pltpu.PrefetchScalarGridSpec(
        num_scalar_prefetch=3,
        grid=(N_EXPERTS,),
        in_specs=[
            pl.BlockSpec(memory_space=pl.ANY),
            pl.BlockSpec((1, d, D_EXPERT), w_map),
            pl.BlockSpec((1, d, D_EXPERT), w_map),
            pl.BlockSpec((1, D_EXPERT, d), w_map),
        ],
        out_specs=pl.BlockSpec(memory_space=pl.ANY),
        scratch_shapes=[pltpu.VMEM((2, TMX, w), U32),
                        pltpu.VMEM((2, TMX, w), U32),
                        pltpu.SemaphoreType.DMA((2,)),
                        pltpu.SemaphoreType.DMA((2,))],
    )
    return pl.pallas_call(
        _experts_kernel,
        grid_spec=grid_spec,
        out_shape=jax.ShapeDtypeStruct((n_rows, w), U32),
        compiler_params=_cparams(("arbitrary",)),
        name="experts",
    )(tile_start, n_tiles_e, n_used, xs, w_gate, w_up, w_down)


def _combine_kernel(pos_ref, posn_ref, y_hbm, h1_ref, wt_ref, g2_ref, b2_ref, o_ref, ybuf, sem):
    i = pl.program_id(0)
    n = pl.num_programs(0)
    slot = i % 2

    def row_copy(src, sl, which, j):
        return pltpu.make_async_copy(y_hbm.at[pl.ds(src, 1)],
                                     ybuf.at[sl, which, pl.ds(j, 1)], sem.at[sl])

    def issue(p_ref, sl):
        def body(j, c):
            row_copy(p_ref[0, j], sl, 0, j).start(priority=0)
            row_copy(p_ref[1, j], sl, 1, j).start(priority=1)
            return c
        lax.fori_loop(0, TM_OUT, body, 0, unroll=8)

    def drain(sl):
        def wait(j, c):
            row_copy(0, sl, 0, 0).wait()
            row_copy(0, sl, 1, 0).wait()
            return c
        lax.fori_loop(0, TM_OUT, wait, 0, unroll=8)

    @pl.when(i == 0)
    def _():
        issue(pos_ref, 0)

    drain(slot)

    for j in range(TM_OUT):
        row_copy(posn_ref[0, j], 1 - slot, 0, j).start(priority=0)
        row_copy(posn_ref[1, j], 1 - slot, 1, j).start(priority=1)

    lo1, hi1 = _unpack_bf16_pair(ybuf[slot, 0])
    lo2, hi2 = _unpack_bf16_pair(ybuf[slot, 1])
    w1 = wt_ref[:, 0:1]
    w2 = wt_ref[:, 1:2]
    ffn = jnp.concatenate([w1 * lo1 + w2 * lo2, w1 * hi1 + w2 * hi2], axis=1)
    o_ref[...] = _layernorm(DN_ALPHA * h1_ref[...] + ffn, g2_ref[...], b2_ref[...])

    @pl.when(i == n - 1)
    def _():
        drain(1 - slot)


def _combine(pos3, ys, h1, wtok, g2, b2):
    t, d = h1.shape
    w = ys.shape[1]
    n = t // TM_OUT
    return pl.pallas_call(
        _combine_kernel,
        grid=(n,),
        in_specs=[
            pl.BlockSpec((None, 2, TM_OUT), lambda i: (i, 0, 0), memory_space=pltpu.SMEM),
            pl.BlockSpec((None, 2, TM_OUT), lambda i: (jnp.minimum(i + 1, n - 1), 0, 0),
                         memory_space=pltpu.SMEM),
            pl.BlockSpec(memory_space=pl.ANY),
            pl.BlockSpec((TM_OUT, d), lambda i: (i, 0)),
            pl.BlockSpec((TM_OUT, ROUTE_ROWS), lambda i: (i, 0)),
            pl.BlockSpec((1, d), lambda i: (0, 0)),
            pl.BlockSpec((1, d), lambda i: (0, 0)),
        ],
        out_specs=pl.BlockSpec((TM_OUT, d), lambda i: (i, 0)),
        out_shape=jax.ShapeDtypeStruct((t, d), F32),
        scratch_shapes=[pltpu.VMEM((2, 2, TM_OUT, w), U32),
                        pltpu.SemaphoreType.DMA((2,))],
        compiler_params=_cparams(("arbitrary",)),
        name="combine",
    )(pos3, pos3, ys, h1, wtok, g2, b2)


def _lambda_init(layer):
    return 0.8 - 0.6 * math.exp(-0.3 * layer)


def kernel(x, w_in, w_out, ln1_g, ln1_b, ln2_g, ln2_b, rel_bias, lam_q1, lam_k1, lam_q2, lam_k2,
           subln_g, sg_ln_g, sg_ln_b, sg_w, sg_b, w_router_group, b_router_group,
           w_router_expert, b_router_expert, w_exp_gate, w_exp_up, w_exp_down):
    b, s, d = x.shape
    t = b * s
    assert w_in.shape[0] == DEPTH == 1
    l = 0
    lam_init = _lambda_init(l)
    x2 = x.reshape(t, d)

    proj = _inproj(x2, w_in[l])

    lam = (jnp.exp(jnp.sum(lam_q1[l].astype(F32) * lam_k1[l].astype(F32)))
           - jnp.exp(jnp.sum(lam_q2[l].astype(F32) * lam_k2[l].astype(F32))) + lam_init)
    side = [w_exp_gate[l].reshape(-1, D_EXPERT), w_exp_up[l].reshape(-1, D_EXPERT),
            w_exp_down[l].reshape(-1, d)]
    a, wg_b, wu_b, wd_b = _attention(
        proj.reshape(b, s, -1), _bias_sequences(rel_bias, s), lam.reshape(1),
        subln_g[l].reshape(1, -1), lam_init, side)

    wr = jnp.concatenate([w_router_group[l].T,
                          jnp.zeros((NR_EXPERT0 - N_GROUPS, d), F32),
                          jnp.transpose(w_router_expert[l], (0, 2, 1)).reshape(N_EXPERTS, d)],
                         axis=0)
    br = jnp.concatenate([b_router_group[l], jnp.zeros((NR_EXPERT0 - N_GROUPS,), F32),
                          b_router_expert[l].reshape(-1)]).reshape(NR_PAD, 1)
    h1, hp, ri, rw, cnt = _mix(
        a.reshape(t, -1), proj, x2, w_out[l].astype(BF16), sg_w[l].astype(BF16),
        sg_b[l].reshape(SG_GROUPS, SG_CHUNK, 1), sg_ln_g[l], sg_ln_b[l],
        ln1_g[l].reshape(1, d), ln1_b[l].reshape(1, d), wr, br)

    counts = cnt[:, 0]
    tiles_e = (counts + TMX - 1) // TMX
    tile_end = jnp.cumsum(tiles_e)
    tile_start = tile_end - tiles_e
    n_used = tile_end[-1:]
    max_tiles = (2 * t) // TMX + N_EXPERTS
    last_tile = jnp.maximum(tile_end - 1, 0).astype(I32)
    row_start = (tile_start * TMX).astype(I32)
    eids = jnp.arange(N_EXPERTS, dtype=I32)[:, None, None]
    pos = jnp.sum(jnp.where(ri[0:2][None] == eids, row_start[:, None, None], 0), axis=0) + ri[2:4]
    pos3 = pos.reshape(2, t // TM_OUT, TM_OUT).transpose(1, 0, 2)

    xs = _dispatch(pos3, last_tile, hp, max_tiles * TMX)
    ys = _experts(tile_start.astype(I32), tiles_e.astype(I32), n_used.astype(I32), xs,
                  wg_b.reshape(N_EXPERTS, d, D_EXPERT), wu_b.reshape(N_EXPERTS, d, D_EXPERT),
                  wd_b.reshape(N_EXPERTS, D_EXPERT, d))
    out = _combine(pos3, ys, h1, rw.T, ln2_g[l].reshape(1, d), ln2_b[l].reshape(1, d))
    return out.reshape(b, s, d)
```

```python
import functools
import math

import jax
import jax.numpy as jnp
from jax import lax
from jax.experimental import pallas as pl
from jax.experimental.pallas import tpu as pltpu

F32 = jnp.float32
BF16 = jnp.bfloat16
U32 = jnp.uint32
I32 = jnp.int32

ATT_HEADS = 8
ATT_QK_DIM = 64
ATT_V_DIM = 128
ATT_WIDTH = ATT_HEADS * ATT_V_DIM
SG_GROUPS = 8
SG_DIM = 128
SG_WIDTH = SG_GROUPS * SG_DIM
SG_CHUNK = 128
REL_BUCKETS = 32
REL_MAX_DIST = 128
N_GROUPS = 4
EXPERTS_PER_GROUP = 8
N_EXPERTS = N_GROUPS * EXPERTS_PER_GROUP
D_EXPERT = 256
DEPTH = 1
DN_ALPHA = (2.0 * DEPTH) ** 0.25
LN_EPS = 1e-5
LOG2E = math.log2(math.e)

LANES = 128
SUBLANES = 8
VMEM_LIMIT_BYTES = 56 * 1024 * 1024

TM_IN = 1024
TN_IN = 1024
QB = 256
KC = 256
N_BIAS_TILES = 5
TM_MIX = 512
TMX = 256
TM_OUT = 256
ROUTE_ROWS = 8
XS_CAP = 8192 + TMX
NR_EXPERT0 = 8
NR_PAD = NR_EXPERT0 + 32


def _cparams(sem, flags=None):
    return pltpu.CompilerParams(dimension_semantics=sem, vmem_limit_bytes=VMEM_LIMIT_BYTES,
                                flags=flags)


def _inproj_kernel(x_ref, w_ref, o_ref, xb_ref, *, n_q, n_lin, q_scale):
    j = pl.program_id(1)

    @pl.when(j == 0)
    def _():
        xb_ref[...] = x_ref[...].astype(BF16)

    acc = jnp.dot(xb_ref[...], w_ref[...].astype(BF16), preferred_element_type=F32)

    @pl.when(j < n_q)
    def _():
        o_ref[...] = (acc * q_scale).astype(BF16)

    @pl.when(jnp.logical_and(j >= n_q, j < n_lin))
    def _():
        o_ref[...] = acc.astype(BF16)

    @pl.when(j >= n_lin)
    def _():
        o_ref[...] = jax.nn.gelu(acc).astype(BF16)


def _inproj(x2, w_in):
    t, d = x2.shape
    n = w_in.shape[1]
    n_q = (ATT_HEADS * 2 * ATT_QK_DIM) // TN_IN
    n_lin = (2 * ATT_HEADS * 2 * ATT_QK_DIM + ATT_WIDTH) // TN_IN
    kern = functools.partial(_inproj_kernel, n_q=n_q, n_lin=n_lin,
                             q_scale=(ATT_QK_DIM ** -0.5) * LOG2E)
    return pl.pallas_call(
        kern,
        grid=(t // TM_IN, n // TN_IN),
        in_specs=[pl.BlockSpec((TM_IN, d), lambda i, j: (i, 0)),
                  pl.BlockSpec((d, TN_IN), lambda i, j: (0, j))],
        out_specs=pl.BlockSpec((TM_IN, TN_IN), lambda i, j: (i, j)),
        out_shape=jax.ShapeDtypeStruct((t, n), BF16),
        scratch_shapes=[pltpu.VMEM((TM_IN, d), BF16)],
        compiler_params=_cparams(("arbitrary", "arbitrary")),
        name="inproj",
    )(x2, w_in)


def _attn_kernel(lam_ref, q_ref, k_ref, v_ref, bseq_ref, g_ref, *rest, n_kc, out_scale):
    n_side = (len(rest) - 3) // 2
    side_in, o_ref, side_out = rest[:n_side], rest[n_side], rest[n_side + 1:2 * n_side + 1]
    vaug_ref, bias_ref = rest[2 * n_side + 1:]
    qi = pl.program_id(2)

    @pl.when(qi == 0)
    def _():
        vaug_ref[:, :ATT_V_DIM] = v_ref[0]
        vaug_ref[:, ATT_V_DIM:] = jnp.ones((vaug_ref.shape[0], ATT_V_DIM), BF16)
        for t in range(N_BIAS_TILES):
            seq = jnp.broadcast_to(bseq_ref[0, t:t + 1, :], (QB, 2 * KC))
            bias_ref[t] = pltpu.roll(seq, 0, 1, stride=1, stride_axis=0)[:, :KC]

    q = q_ref[0]
    lane = lax.broadcasted_iota(I32, q.shape, 1)
    zero = jnp.zeros_like(q)
    qq = jnp.concatenate([jnp.where(lane < ATT_QK_DIM, q, zero),
                          jnp.where(lane >= ATT_QK_DIM, q, zero)], axis=0)

    for src, dst in zip(side_in, side_out):
        dst[...] = src[...].astype(dst.dtype)

    m = jnp.full((2 * QB, 1), -jnp.inf, F32)
    acc = jnp.zeros((2 * QB, 2 * ATT_V_DIM), F32)
    for kc in range(n_kc):
        kblk = k_ref[0, kc * KC:(kc + 1) * KC, :]
        s = lax.dot_general(qq, kblk, (((1,), (1,)), ((), ())),
                            preferred_element_type=F32)
        b = bias_ref[jnp.clip(kc - qi, -2, 2) + 2]
        s = s + jnp.concatenate([b, b], axis=0)
        m_new = jnp.maximum(m, jnp.max(s, axis=1, keepdims=True))
        p = jnp.exp2(s - m_new).astype(BF16)
        acc = acc * jnp.exp2(m - m_new) + jnp.dot(
            p, vaug_ref[kc * KC:(kc + 1) * KC, :], preferred_element_type=F32)
        m = m_new
    lam = lam_ref[0]
    o0 = acc[:QB, :ATT_V_DIM] / acc[:QB, ATT_V_DIM:ATT_V_DIM + 1]
    o1 = acc[QB:, :ATT_V_DIM] / acc[QB:, ATT_V_DIM:ATT_V_DIM + 1]
    of = o0 - lam * o1
    of = of * lax.rsqrt(jnp.mean(of * of, axis=-1, keepdims=True) + LN_EPS)
    o_ref[0] = (of * (g_ref[...] * out_scale)).astype(o_ref.dtype)


def _attention(proj3, bias_seq, lam, subln_g, lam_init, side):
    b, s, _ = proj3.shape
    n_kc = s // KC
    n_q = s // QB
    n_steps = b * ATT_HEADS * n_q
    h_off_k = ATT_HEADS
    h_off_v = 2 * ATT_HEADS
    kern = functools.partial(_attn_kernel, n_kc=n_kc, out_scale=1.0 - lam_init)

    def side_map(bi, h, qi):
        return ((bi * ATT_HEADS + h) * n_q + qi, 0)

    side_specs = []
    for arr in side:
        rows, cols = arr.shape
        assert rows % (n_steps * 2 * SUBLANES) == 0
        side_specs.append(pl.BlockSpec((rows // n_steps, cols), side_map))
    return pl.pallas_call(
        kern,
        grid=(b, ATT_HEADS, n_q),
        in_specs=[
            pl.BlockSpec(memory_space=pltpu.SMEM),
            pl.BlockSpec((1, QB, LANES), lambda bi, h, qi: (bi, qi, h)),
            pl.BlockSpec((1, s, LANES), lambda bi, h, qi: (bi, 0, h_off_k + h)),
            pl.BlockSpec((1, s, LANES), lambda bi, h, qi: (bi, 0, h_off_v + h)),
            pl.BlockSpec((1, N_BIAS_TILES, 2 * KC), lambda bi, h, qi: (h, 0, 0)),
            pl.BlockSpec((1, ATT_V_DIM), lambda bi, h, qi: (0, 0)),
        ] + side_specs,
        out_specs=[pl.BlockSpec((1, QB, ATT_V_DIM), lambda bi, h, qi: (bi, qi, h))] + side_specs,
        out_shape=[jax.ShapeDtypeStruct((b, s, ATT_WIDTH), BF16)]
        + [jax.ShapeDtypeStruct(arr.shape, BF16) for arr in side],
        scratch_shapes=[pltpu.VMEM((s, 2 * ATT_V_DIM), BF16),
                        pltpu.VMEM((N_BIAS_TILES, QB, KC), F32)],
        compiler_params=_cparams(("arbitrary", "arbitrary", "arbitrary")),
        name="attn",
    )(lam, proj3, proj3, proj3, bias_seq, subln_g, *side)


def _rel_bucket(rel):
    half = REL_BUCKETS // 2
    max_exact = half // 2
    ret = jnp.where(rel > 0, half, 0)
    n = jnp.abs(rel)
    nf = jnp.maximum(n, 1).astype(F32)
    large = max_exact + (jnp.log(nf / max_exact) / math.log(REL_MAX_DIST / max_exact)
                         * (half - max_exact)).astype(I32)
    large = jnp.minimum(large, half - 1)
    return ret + jnp.where(n < max_exact, n, large)


def _bias_sequences(rel_bias, s):
    assert REL_MAX_DIST <= KC and QB == KC
    period = 2 * KC
    j = jnp.arange(period, dtype=I32)
    off = jnp.where(j < KC, j, j - period)
    d = jnp.arange(-2, 3, dtype=I32)[:, None]
    rel = jnp.clip(d * KC + off[None, :], -(s - 1), s - 1)
    return jnp.transpose(rel_bias[_rel_bucket(rel)], (2, 0, 1)).astype(F32) * LOG2E


def _layernorm(v, g, b):
    mu = jnp.mean(v, axis=-1, keepdims=True)
    var = jnp.mean(jnp.square(v - mu), axis=-1, keepdims=True)
    return (v - mu) * lax.rsqrt(var + LN_EPS) * g + b


def _pack_bf16_pair(v):
    n = v.shape[1] // 2
    bits = lax.bitcast_convert_type(v.astype(BF16).astype(F32), U32)
    return (bits[:, :n] >> 16) | (bits[:, n:] & jnp.uint32(0xFFFF0000))


def _unpack_bf16_pair(w):
    lo = lax.bitcast_convert_type(w << 16, F32)
    hi = lax.bitcast_convert_type(w & jnp.uint32(0xFFFF0000), F32)
    return lo, hi


def _first_argmax(vals, nrows):
    mx = jnp.max(vals, axis=0, keepdims=True)
    row = lax.broadcasted_iota(I32, vals.shape, 0)
    idx = jnp.min(jnp.where(vals == mx, row, nrows), axis=0, keepdims=True)
    return mx, idx


def _mix_kernel(a_ref, u_ref, vg_ref, x_ref, wout_ref, sgw_ref, sgb_ref, sgg_ref, sgbeta_ref,
                g1_ref, b1_ref, wr_ref, br_ref,
                h1_ref, xs_hbm, ri_ref, rw_ref, cnt_ref,
                run_ref, hp_s, posv_ref, psm, ssem, rsem):
    i = pl.program_id(0)
    n = pl.num_programs(0)
    slot = i % 2
    prev = 1 - slot

    def pos_copy(sl):
        return pltpu.make_async_copy(posv_ref, psm.at[sl], ssem.at[sl])

    def row_copy(sl, which, j):
        return pltpu.make_async_copy(hp_s.at[sl, pl.ds(j, 1)],
                                     xs_hbm.at[pl.ds(psm[sl, which, j], 1)], rsem.at[sl])

    def drain(sl):
        def wait(j, c):
            row_copy(sl, 0, 0).wait()
            row_copy(sl, 1, 0).wait()
            return c
        lax.fori_loop(0, TM_MIX, wait, 0, unroll=8)

    @pl.when(i == 0)
    def _():
        run_ref[...] = jnp.zeros_like(run_ref)
        hp_s[1] = jnp.zeros(hp_s.shape[1:], hp_s.dtype)
        spare = N_EXPERTS * XS_CAP + lax.broadcasted_iota(I32, posv_ref.shape, 1)
        spare = spare + TM_MIX * lax.broadcasted_iota(I32, posv_ref.shape, 0)
        posv_ref[...] = jnp.where(spare < xs_hbm.shape[0], spare, 0)
        pos_copy(1).start()

    @pl.when(i > 0)
    def _():
        drain(slot)

    pos_copy(prev).wait()

    for j in range(TM_MIX):
        row_copy(prev, 0, j).start(priority=0)
        row_copy(prev, 1, j).start(priority=1)

    n_chunks = TM_MIX // SG_CHUNK
    rows = []
    for c in range(n_chunks):
        cols = []
        for g in range(SG_GROUPS):
            rs = slice(c * SG_CHUNK, (c + 1) * SG_CHUNK)
            cs = slice(g * SG_DIM, (g + 1) * SG_DIM)
            vt = vg_ref[rs, cs].astype(F32)
            vn = _layernorm(vt, sgg_ref[g:g + 1, :], sgbeta_ref[g:g + 1, :])
            mixed = jnp.dot(sgw_ref[g], vn.astype(BF16), preferred_element_type=F32) + sgb_ref[g]
            cols.append((u_ref[rs, cs].astype(F32) * mixed).astype(BF16))
        rows.append(jnp.concatenate(cols, axis=1))
    sgate = jnp.concatenate(rows, axis=0)

    mix = jnp.dot(a_ref[...], wout_ref[:ATT_WIDTH, :], preferred_element_type=F32)
    mix = mix + jnp.dot(sgate, wout_ref[ATT_WIDTH:, :], preferred_element_type=F32)
    h1 = _layernorm(DN_ALPHA * x_ref[...] + mix, g1_ref[...], b1_ref[...])
    h1_ref[...] = h1
    hp_s[slot] = _pack_bf16_pair(h1)

    lt = lax.dot_general(wr_ref[...], h1, (((1,), (1,)), ((), ())),
                         preferred_element_type=F32, precision=lax.Precision.HIGHEST)
    lt = lt + br_ref[...]
    g_logits = lt[0:N_GROUPS]
    gmax, g_idx = _first_argmax(g_logits, N_GROUPS)
    g_gate = 1.0 / jnp.sum(jnp.exp(g_logits - gmax), axis=0, keepdims=True)
    e_logits = jnp.zeros((EXPERTS_PER_GROUP, TM_MIX), F32)
    for g in range(N_GROUPS):
        lo = NR_EXPERT0 + g * EXPERTS_PER_GROUP
        e_logits = jnp.where(g_idx == g, lt[lo:lo + EXPERTS_PER_GROUP], e_logits)
    v1, i1 = _first_argmax(e_logits, EXPERTS_PER_GROUP)
    erow = lax.broadcasted_iota(I32, e_logits.shape, 0)
    v2, i2 = _first_argmax(jnp.where(erow == i1, -jnp.inf, e_logits), EXPERTS_PER_GROUP)
    t = jnp.exp(v2 - v1)
    w1 = g_gate / (1.0 + t)
    w2 = g_gate * t / (1.0 + t)
    e1 = g_idx * EXPERTS_PER_GROUP + i1
    e2 = g_idx * EXPERTS_PER_GROUP + i2

    xrow = lax.broadcasted_iota(I32, (N_EXPERTS, TM_MIX), 0)
    oh1 = xrow == e1
    oh2 = xrow == e2
    oh = jnp.where(jnp.logical_or(oh1, oh2), 1.0, 0.0)
    tr = lax.broadcasted_iota(I32, (TM_MIX, TM_MIX), 0)
    tc = lax.broadcasted_iota(I32, (TM_MIX, TM_MIX), 1)
    before = jnp.where(tr < tc, 1.0, 0.0).astype(BF16)
    cum = jnp.dot(oh.astype(BF16), before, preferred_element_type=F32)
    base = cum + run_ref[:, 0:1]
    rank1 = jnp.sum(jnp.where(oh1, base, 0.0), axis=0, keepdims=True).astype(I32)
    rank2 = jnp.sum(jnp.where(oh2, base, 0.0), axis=0, keepdims=True).astype(I32)
    run_ref[...] = run_ref[...] + jnp.sum(oh, axis=1, keepdims=True)
    cnt_ref[...] = run_ref[...].astype(I32)

    zi = jnp.zeros((ROUTE_ROWS - 4, TM_MIX), I32)
    ri_ref[...] = jnp.concatenate([e1, e2, rank1, rank2, zi], axis=0)
    zw = jnp.zeros((ROUTE_ROWS - 2, TM_MIX), F32)
    rw_ref[...] = jnp.concatenate([w1, w2, zw], axis=0)

    posv_ref[...] = jnp.concatenate([e1 * XS_CAP + rank1, e2 * XS_CAP + rank2, zi, zi[:2]], axis=0)
    pos_copy(slot).start()

    @pl.when(i == n - 1)
    def _():
        pos_copy(slot).wait()

        def last(j, c):
            row_copy(slot, 0, j).start(priority=0)
            row_copy(slot, 1, j).start(priority=1)
            return c
        lax.fori_loop(0, TM_MIX, last, 0, unroll=8)
        drain(prev)
        drain(slot)


def _mix(a2, proj2, x2, wout_b, sgw_b, sgb, sgg, sgbeta, g1, b1, wr, br):
    t, d = x2.shape
    assert XS_CAP >= t + TMX
    u_blk = (proj2.shape[1] - 2 * SG_WIDTH) // SG_WIDTH
    const = lambda *shape: pl.BlockSpec(shape, lambda i: (0,) * len(shape))
    return pl.pallas_call(
        _mix_kernel,
        grid=(t // TM_MIX,),
        in_specs=[
            pl.BlockSpec((TM_MIX, ATT_WIDTH), lambda i: (i, 0)),
            pl.BlockSpec((TM_MIX, SG_WIDTH), lambda i: (i, u_blk)),
            pl.BlockSpec((TM_MIX, SG_WIDTH), lambda i: (i, u_blk + 1)),
            pl.BlockSpec((TM_MIX, d), lambda i: (i, 0)),
            pl.BlockSpec((d, d), lambda i: (0, 0), pipeline_mode=pl.Buffered(1)),
            const(SG_GROUPS, SG_CHUNK, SG_CHUNK),
            const(SG_GROUPS, SG_CHUNK, 1),
            const(SG_GROUPS, SG_DIM),
            const(SG_GROUPS, SG_DIM),
            const(1, d),
            const(1, d),
            const(NR_PAD, d),
            const(NR_PAD, 1),
        ],
        out_specs=[
            pl.BlockSpec((TM_MIX, d), lambda i: (i, 0)),
            pl.BlockSpec(memory_space=pl.ANY),
            pl.BlockSpec((ROUTE_ROWS, TM_MIX), lambda i: (0, i)),
            pl.BlockSpec((ROUTE_ROWS, TM_MIX), lambda i: (0, i)),
            const(N_EXPERTS, LANES),
        ],
        out_shape=[
            jax.ShapeDtypeStruct((t, d), F32),
            jax.ShapeDtypeStruct((N_EXPERTS * XS_CAP + 2 * TM_MIX, d // 2), U32),
            jax.ShapeDtypeStruct((ROUTE_ROWS, t), I32),
            jax.ShapeDtypeStruct((ROUTE_ROWS, t), F32),
            jax.ShapeDtypeStruct((N_EXPERTS, LANES), I32),
        ],
        scratch_shapes=[pltpu.VMEM((N_EXPERTS, LANES), F32),
                        pltpu.VMEM((2, TM_MIX, d // 2), U32),
                        pltpu.VMEM((ROUTE_ROWS, TM_MIX), I32),
                        pltpu.SMEM((2, ROUTE_ROWS, TM_MIX), I32),
                        pltpu.SemaphoreType.DMA((2,)),
                        pltpu.SemaphoreType.DMA((2,))],
        compiler_params=_cparams(("arbitrary",)),
        name="mix",
    )(a2, proj2, proj2, x2, wout_b, sgw_b, sgb, sgg, sgbeta, g1, b1, wr, br)


def _experts_kernel(ts_ref, nt_ref, nu_ref, trow_ref, nval_ref, x_hbm, wg_ref, wu_ref, wd_ref,
                    y_hbm, xbuf, ybuf, xsem, ysem):
    e = pl.program_id(0)
    d = wg_ref.shape[1]
    n_used = nu_ref[0]

    def x_copy(g, slot):
        rows = pl.ds(pl.multiple_of(trow_ref[g], TMX), TMX)
        return pltpu.make_async_copy(x_hbm.at[rows], xbuf.at[slot], xsem.at[slot])

    def y_copy(g, slot):
        rows = pl.ds(pl.multiple_of(g * TMX, TMX), TMX)
        return pltpu.make_async_copy(ybuf.at[slot], y_hbm.at[rows], ysem.at[slot])

    @pl.when(jnp.logical_and(e == 0, n_used > 0))
    def _():
        x_copy(0, 0).start(priority=1)

    def tile(t, carry):
        g = ts_ref[e] + t
        slot = g % 2
        x_copy(g, slot).wait()

        @pl.when(g + 1 < n_used)
        def _():
            x_copy(g + 1, 1 - slot).start(priority=1)

        @pl.when(g >= 2)
        def _():
            y_copy(g - 2, slot).wait()

        row = lax.broadcasted_iota(I32, xbuf.shape[1:], 0)
        xw = jnp.where(row < nval_ref[g], xbuf[slot], jnp.uint32(0))
        lo, hi = _unpack_bf16_pair(xw)
        lo, hi = lo.astype(BF16), hi.astype(BF16)
        gate = jnp.dot(lo, wg_ref[0, :d // 2, :], preferred_element_type=F32)
        gate = gate + jnp.dot(hi, wg_ref[0, d // 2:, :], preferred_element_type=F32)
        up = jnp.dot(lo, wu_ref[0, :d // 2, :], preferred_element_type=F32)
        up = up + jnp.dot(hi, wu_ref[0, d // 2:, :], preferred_element_type=F32)
        hmid = (gate * jax.nn.sigmoid(gate) * up).astype(BF16)
        y = jnp.dot(hmid, wd_ref[0], preferred_element_type=F32)
        ybuf[slot] = _pack_bf16_pair(y)
        y_copy(g, slot).start(priority=1)
        return carry

    lax.fori_loop(0, nt_ref[e], tile, 0)

    @pl.when(e == pl.num_programs(0) - 1)
    def _():
        @pl.when(n_used >= 2)
        def _():
            y_copy(n_used - 2, (n_used - 2) % 2).wait()

        @pl.when(n_used >= 1)
        def _():
            y_copy(n_used - 1, (n_used - 1) % 2).wait()


def _experts(tile_start, n_tiles_e, n_used, tile_row, tile_valid, xs, w_gate, w_up, w_down):
    w = xs.shape[1]
    d = w_gate.shape[1]
    n_rows = tile_row.shape[0] * TMX

    def w_map(e, ts, nt, nu, trow, nval):
        return (e, 0, 0)

    grid_spec = pltpu.PrefetchScalarGridSpec(
        num_scalar_prefetch=5,
        grid=(N_EXPERTS,),
        in_specs=[
            pl.BlockSpec(memory_space=pl.ANY),
            pl.BlockSpec((1, d, D_EXPERT), w_map),
            pl.BlockSpec((1, d, D_EXPERT), w_map),
            pl.BlockSpec((1, D_EXPERT, d), w_map),
        ],
        out_specs=pl.BlockSpec(memory_space=pl.ANY),
        scratch_shapes=[pltpu.VMEM((2, TMX, w), U32),
                        pltpu.VMEM((2, TMX, w), U32),
                        pltpu.SemaphoreType.DMA((2,)),
                        pltpu.SemaphoreType.DMA((2,))],
    )
    return pl.pallas_call(
        _experts_kernel,
        grid_spec=grid_spec,
        out_shape=jax.ShapeDtypeStruct((n_rows, w), U32),
        compiler_params=_cparams(("arbitrary",)),
        name="experts",
    )(tile_start, n_tiles_e, n_used, tile_row, tile_valid, xs, w_gate, w_up, w_down)


def _combine_kernel(pos_ref, posn_ref, y_hbm, h1_ref, wt_ref, g2_ref, b2_ref, o_ref, ybuf, sem):
    i = pl.program_id(0)
    n = pl.num_programs(0)
    slot = i % 2

    def row_copy(src, sl, which, j):
        return pltpu.make_async_copy(y_hbm.at[pl.ds(src, 1)],
                                     ybuf.at[sl, which, pl.ds(j, 1)], sem.at[sl])

    def issue(p_ref, sl):
        def body(j, c):
            row_copy(p_ref[0, j], sl, 0, j).start(priority=0)
            row_copy(p_ref[1, j], sl, 1, j).start(priority=1)
            return c
        lax.fori_loop(0, TM_OUT, body, 0, unroll=8)

    def drain(sl):
        def wait(j, c):
            row_copy(0, sl, 0, 0).wait()
            row_copy(0, sl, 1, 0).wait()
            return c
        lax.fori_loop(0, TM_OUT, wait, 0, unroll=8)

    @pl.when(i == 0)
    def _():
        issue(pos_ref, 0)

    drain(slot)

    for j in range(TM_OUT):
        row_copy(posn_ref[0, j], 1 - slot, 0, j).start(priority=0)
        row_copy(posn_ref[1, j], 1 - slot, 1, j).start(priority=1)

    lo1, hi1 = _unpack_bf16_pair(ybuf[slot, 0])
    lo2, hi2 = _unpack_bf16_pair(ybuf[slot, 1])
    w1 = wt_ref[:, 0:1]
    w2 = wt_ref[:, 1:2]
    ffn = jnp.concatenate([w1 * lo1 + w2 * lo2, w1 * hi1 + w2 * hi2], axis=1)
    o_ref[...] = _layernorm(DN_ALPHA * h1_ref[...] + ffn, g2_ref[...], b2_ref[...])

    @pl.when(i == n - 1)
    def _():
        drain(1 - slot)


def _combine(pos3, ys, h1, wtok, g2, b2):
    t, d = h1.shape
    w = ys.shape[1]
    n = t // TM_OUT
    return pl.pallas_call(
        _combine_kernel,
        grid=(n,),
        in_specs=[
            pl.BlockSpec((None, 2, TM_OUT), lambda i: (i, 0, 0), memory_space=pltpu.SMEM),
            pl.BlockSpec((None, 2, TM_OUT), lambda i: (jnp.minimum(i + 1, n - 1), 0, 0),
                         memory_space=pltpu.SMEM),
            pl.BlockSpec(memory_space=pl.ANY),
            pl.BlockSpec((TM_OUT, d), lambda i: (i, 0)),
            pl.BlockSpec((TM_OUT, ROUTE_ROWS), lambda i: (i, 0)),
            pl.BlockSpec((1, d), lambda i: (0, 0)),
            pl.BlockSpec((1, d), lambda i: (0, 0)),
        ],
        out_specs=pl.BlockSpec((TM_OUT, d), lambda i: (i, 0)),
        out_shape=jax.ShapeDtypeStruct((t, d), F32),
        scratch_shapes=[pltpu.VMEM((2, 2, TM_OUT, w), U32),
                        pltpu.SemaphoreType.DMA((2,))],
        compiler_params=_cparams(("arbitrary",)),
        name="combine",
    )(pos3, pos3, ys, h1, wtok, g2, b2)


def _lambda_init(layer):
    return 0.8 - 0.6 * math.exp(-0.3 * layer)


def kernel(x, w_in, w_out, ln1_g, ln1_b, ln2_g, ln2_b, rel_bias, lam_q1, lam_k1, lam_q2, lam_k2,
           subln_g, sg_ln_g, sg_ln_b, sg_w, sg_b, w_router_group, b_router_group,
           w_router_expert, b_router_expert, w_exp_gate, w_exp_up, w_exp_down):
    b, s, d = x.shape
    t = b * s
    assert w_in.shape[0] == DEPTH == 1
    l = 0
    lam_init = _lambda_init(l)
    x2 = x.reshape(t, d)

    proj = _inproj(x2, w_in[l])

    lam = (jnp.exp(jnp.sum(lam_q1[l].astype(F32) * lam_k1[l].astype(F32)))
           - jnp.exp(jnp.sum(lam_q2[l].astype(F32) * lam_k2[l].astype(F32))) + lam_init)
    side = [w_exp_gate[l].reshape(-1, D_EXPERT), w_exp_up[l].reshape(-1, D_EXPERT),
            w_exp_down[l].reshape(-1, d)]
    a, wg_b, wu_b, wd_b = _attention(
        proj.reshape(b, s, -1), _bias_sequences(rel_bias, s), lam.reshape(1),
        subln_g[l].reshape(1, -1), lam_init, side)

    wr = jnp.concatenate([w_router_group[l].T,
                          jnp.zeros((NR_EXPERT0 - N_GROUPS, d), F32),
                          jnp.transpose(w_router_expert[l], (0, 2, 1)).reshape(N_EXPERTS, d)],
                         axis=0)
    br = jnp.concatenate([b_router_group[l], jnp.zeros((NR_EXPERT0 - N_GROUPS,), F32),
                          b_router_expert[l].reshape(-1)]).reshape(NR_PAD, 1)
    h1, xs, ri, rw, cnt = _mix(
        a.reshape(t, -1), proj, x2, w_out[l].astype(BF16), sg_w[l].astype(BF16),
        sg_b[l].reshape(SG_GROUPS, SG_CHUNK, 1), sg_ln_g[l], sg_ln_b[l],
        ln1_g[l].reshape(1, d), ln1_b[l].reshape(1, d), wr, br)

    counts = cnt[:, 0]
    tiles_e = (counts + TMX - 1) // TMX
    tile_end = jnp.cumsum(tiles_e)
    tile_start = tile_end - tiles_e
    n_used = tile_end[-1:]
    max_tiles = (2 * t) // TMX + N_EXPERTS
    row_start = (tile_start * TMX).astype(I32)
    eids = jnp.arange(N_EXPERTS, dtype=I32)[:, None, None]
    pos = jnp.sum(jnp.where(ri[0:2][None] == eids, row_start[:, None, None], 0), axis=0) + ri[2:4]
    pos3 = pos.reshape(2, t // TM_OUT, TM_OUT).transpose(1, 0, 2)
    tile_ids = jnp.arange(max_tiles, dtype=I32)
    tile_e = jnp.minimum(jnp.searchsorted(tile_end, tile_ids, side="right"), N_EXPERTS - 1)
    local = tile_ids - tile_start[tile_e]
    in_use = tile_ids < n_used[0]
    tile_row = jnp.where(in_use, tile_e * XS_CAP + local * TMX, 0).astype(I32)
    tile_valid = jnp.where(in_use, jnp.clip(counts[tile_e] - local * TMX, 0, TMX), 0).astype(I32)

    ys = _experts(tile_start.astype(I32), tiles_e.astype(I32), n_used.astype(I32),
                  tile_row, tile_valid, xs,
                  wg_b.reshape(N_EXPERTS, d, D_EXPERT), wu_b.reshape(N_EXPERTS, d, D_EXPERT),
                  wd_b.reshape(N_EXPERTS, D_EXPERT, d))
    out = _combine(pos3, ys, h1, rw.T, ln2_g[l].reshape(1, d), ln2_b[l].reshape(1, d))
    return out.reshape(b, s, d)
```

```python
import functools
import math

import jax
import jax.numpy as jnp
from jax import lax
from jax.experimental import pallas as pl
from jax.experimental.pallas import tpu as pltpu

F32 = jnp.float32
BF16 = jnp.bfloat16
U32 = jnp.uint32
I32 = jnp.int32

ATT_HEADS = 8
ATT_QK_DIM = 64
ATT_V_DIM = 128
ATT_WIDTH = ATT_HEADS * ATT_V_DIM
SG_GROUPS = 8
SG_DIM = 128
SG_WIDTH = SG_GROUPS * SG_DIM
SG_CHUNK = 128
REL_BUCKETS = 32
REL_MAX_DIST = 128
N_GROUPS = 4
EXPERTS_PER_GROUP = 8
N_EXPERTS = N_GROUPS * EXPERTS_PER_GROUP
D_EXPERT = 256
DEPTH = 1
DN_ALPHA = (2.0 * DEPTH) ** 0.25
LN_EPS = 1e-5
LOG2E = math.log2(math.e)

LANES = 128
SUBLANES = 8
VMEM_LIMIT_BYTES = 56 * 1024 * 1024

TM_IN = 1024
TN_IN = 1024
QB = 256
KC = 256
N_BIAS_TILES = 5
TM_MIX = 512
TMX = 256
TM_OUT = 256
ROUTE_ROWS = 8
XS_CAP = 8192 + TMX
NR_EXPERT0 = 8
NR_PAD = NR_EXPERT0 + 32


def _cparams(sem, flags=None):
    return pltpu.CompilerParams(dimension_semantics=sem, vmem_limit_bytes=VMEM_LIMIT_BYTES,
                                flags=flags)


def _inproj_kernel(x_ref, w_ref, o_ref, xb_ref, *, n_q, n_lin, q_scale):
    j = pl.program_id(1)

    @pl.when(j == 0)
    def _():
        xb_ref[...] = x_ref[...].astype(BF16)

    acc = jnp.dot(xb_ref[...], w_ref[...].astype(BF16), preferred_element_type=F32)

    @pl.when(j < n_q)
    def _():
        o_ref[...] = (acc * q_scale).astype(BF16)

    @pl.when(jnp.logical_and(j >= n_q, j < n_lin))
    def _():
        o_ref[...] = acc.astype(BF16)

    @pl.when(j >= n_lin)
    def _():
        o_ref[...] = jax.nn.gelu(acc).astype(BF16)


def _inproj(x2, w_in):
    t, d = x2.shape
    n = w_in.shape[1]
    n_q = (ATT_HEADS * 2 * ATT_QK_DIM) // TN_IN
    n_lin = (2 * ATT_HEADS * 2 * ATT_QK_DIM + ATT_WIDTH) // TN_IN
    kern = functools.partial(_inproj_kernel, n_q=n_q, n_lin=n_lin,
                             q_scale=(ATT_QK_DIM ** -0.5) * LOG2E)
    return pl.pallas_call(
        kern,
        grid=(t // TM_IN, n // TN_IN),
        in_specs=[pl.BlockSpec((TM_IN, d), lambda i, j: (i, 0)),
                  pl.BlockSpec((d, TN_IN), lambda i, j: (0, j))],
        out_specs=pl.BlockSpec((TM_IN, TN_IN), lambda i, j: (i, j)),
        out_shape=jax.ShapeDtypeStruct((t, n), BF16),
        scratch_shapes=[pltpu.VMEM((TM_IN, d), BF16)],
        compiler_params=_cparams(("arbitrary", "arbitrary")),
        name="inproj",
    )(x2, w_in)


def _attn_kernel(lam_ref, q_ref, k_ref, v_ref, bseq_ref, g_ref, *rest, n_kc, out_scale):
    n_side = (len(rest) - 3) // 2
    side_in, o_ref, side_out = rest[:n_side], rest[n_side], rest[n_side + 1:2 * n_side + 1]
    vaug_ref, bias_ref = rest[2 * n_side + 1:]
    qi = pl.program_id(2)

    @pl.when(qi == 0)
    def _():
        vaug_ref[:, :ATT_V_DIM] = v_ref[0]
        vaug_ref[:, ATT_V_DIM:] = jnp.ones((vaug_ref.shape[0], ATT_V_DIM), BF16)
        for t in range(N_BIAS_TILES):
            seq = jnp.broadcast_to(bseq_ref[0, t:t + 1, :], (QB, 2 * KC))
            bias_ref[t] = pltpu.roll(seq, 0, 1, stride=1, stride_axis=0)[:, :KC]

    q = q_ref[0]
    lane = lax.broadcasted_iota(I32, q.shape, 1)
    zero = jnp.zeros_like(q)
    qq = jnp.concatenate([jnp.where(lane < ATT_QK_DIM, q, zero),
                          jnp.where(lane >= ATT_QK_DIM, q, zero)], axis=0)

    for src, dst in zip(side_in, side_out):
        dst[...] = src[...].astype(dst.dtype)

    m = jnp.full((2 * QB, 1), -jnp.inf, F32)
    acc = jnp.zeros((2 * QB, 2 * ATT_V_DIM), F32)
    for kc in range(n_kc):
        kblk = k_ref[0, kc * KC:(kc + 1) * KC, :]
        s = lax.dot_general(qq, kblk, (((1,), (1,)), ((), ())),
                            preferred_element_type=F32)
        b = bias_ref[jnp.clip(kc - qi, -2, 2) + 2]
        s = s + jnp.concatenate([b, b], axis=0)
        m_new = jnp.maximum(m, jnp.max(s, axis=1, keepdims=True))
        p = jnp.exp2(s - m_new).astype(BF16)
        acc = acc * jnp.exp2(m - m_new) + jnp.dot(
            p, vaug_ref[kc * KC:(kc + 1) * KC, :], preferred_element_type=F32)
        m = m_new
    lam = lam_ref[0]
    o0 = acc[:QB, :ATT_V_DIM] / acc[:QB, ATT_V_DIM:ATT_V_DIM + 1]
    o1 = acc[QB:, :ATT_V_DIM] / acc[QB:, ATT_V_DIM:ATT_V_DIM + 1]
    of = o0 - lam * o1
    of = of * lax.rsqrt(jnp.mean(of * of, axis=-1, keepdims=True) + LN_EPS)
    o_ref[0] = (of * (g_ref[...] * out_scale)).astype(o_ref.dtype)


def _attention(proj3, bias_seq, lam, subln_g, lam_init, side):
    b, s, _ = proj3.shape
    n_kc = s // KC
    n_q = s // QB
    n_steps = b * ATT_HEADS * n_q
    h_off_k = ATT_HEADS
    h_off_v = 2 * ATT_HEADS
    kern = functools.partial(_attn_kernel, n_kc=n_kc, out_scale=1.0 - lam_init)

    def side_map(bi, h, qi):
        return ((bi * ATT_HEADS + h) * n_q + qi, 0)

    side_specs = []
    for arr in side:
        rows, cols = arr.shape
        assert rows % (n_steps * 2 * SUBLANES) == 0
        side_specs.append(pl.BlockSpec((rows // n_steps, cols), side_map))
    return pl.pallas_call(
        kern,
        grid=(b, ATT_HEADS, n_q),
        in_specs=[
            pl.BlockSpec(memory_space=pltpu.SMEM),
            pl.BlockSpec((1, QB, LANES), lambda bi, h, qi: (bi, qi, h)),
            pl.BlockSpec((1, s, LANES), lambda bi, h, qi: (bi, 0, h_off_k + h)),
            pl.BlockSpec((1, s, LANES), lambda bi, h, qi: (bi, 0, h_off_v + h)),
            pl.BlockSpec((1, N_BIAS_TILES, 2 * KC), lambda bi, h, qi: (h, 0, 0)),
            pl.BlockSpec((1, ATT_V_DIM), lambda bi, h, qi: (0, 0)),
        ] + side_specs,
        out_specs=[pl.BlockSpec((1, QB, ATT_V_DIM), lambda bi, h, qi: (bi, qi, h))] + side_specs,
        out_shape=[jax.ShapeDtypeStruct((b, s, ATT_WIDTH), BF16)]
        + [jax.ShapeDtypeStruct(arr.shape, BF16) for arr in side],
        scratch_shapes=[pltpu.VMEM((s, 2 * ATT_V_DIM), BF16),
                        pltpu.VMEM((N_BIAS_TILES, QB, KC), F32)],
        compiler_params=_cparams(("arbitrary", "arbitrary", "arbitrary")),
        name="attn",
    )(lam, proj3, proj3, proj3, bias_seq, subln_g, *side)


def _rel_bucket(rel):
    half = REL_BUCKETS // 2
    max_exact = half // 2
    ret = jnp.where(rel > 0, half, 0)
    n = jnp.abs(rel)
    nf = jnp.maximum(n, 1).astype(F32)
    large = max_exact + (jnp.log(nf / max_exact) / math.log(REL_MAX_DIST / max_exact)
                         * (half - max_exact)).astype(I32)
    large = jnp.minimum(large, half - 1)
    return ret + jnp.where(n < max_exact, n, large)


def _bias_sequences(rel_bias, s):
    assert REL_MAX_DIST <= KC and QB == KC
    period = 2 * KC
    j = jnp.arange(period, dtype=I32)
    off = jnp.where(j < KC, j, j - period)
    d = jnp.arange(-2, 3, dtype=I32)[:, None]
    rel = jnp.clip(d * KC + off[None, :], -(s - 1), s - 1)
    return jnp.transpose(rel_bias[_rel_bucket(rel)], (2, 0, 1)).astype(F32) * LOG2E


def _layernorm(v, g, b):
    mu = jnp.mean(v, axis=-1, keepdims=True)
    var = jnp.mean(jnp.square(v - mu), axis=-1, keepdims=True)
    return (v - mu) * lax.rsqrt(var + LN_EPS) * g + b


def _pack_bf16_pair(v):
    n = v.shape[1] // 2
    bits = lax.bitcast_convert_type(v.astype(BF16).astype(F32), U32)
    return (bits[:, :n] >> 16) | (bits[:, n:] & jnp.uint32(0xFFFF0000))


def _unpack_bf16_pair(w):
    lo = lax.bitcast_convert_type(w << 16, F32)
    hi = lax.bitcast_convert_type(w & jnp.uint32(0xFFFF0000), F32)
    return lo, hi


def _first_argmax(vals, nrows):
    mx = jnp.max(vals, axis=0, keepdims=True)
    row = lax.broadcasted_iota(I32, vals.shape, 0)
    idx = jnp.min(jnp.where(vals == mx, row, nrows), axis=0, keepdims=True)
    return mx, idx


def _mix_kernel(a_ref, u_ref, vg_ref, x_ref, wout_ref, sgw_ref, sgb_ref, sgg_ref, sgbeta_ref,
                g1_ref, b1_ref, wr_ref, br_ref,
                h1_ref, xs_hbm, ri_ref, rw_ref, cnt_ref,
                run_ref, hp_s, posv_ref, psm, ssem, rsem):
    i = pl.program_id(0)
    n = pl.num_programs(0)
    slot = i % 2
    prev = 1 - slot

    def pos_copy(sl):
        return pltpu.make_async_copy(posv_ref, psm.at[sl], ssem.at[sl])

    def row_copy(sl, which, j):
        return pltpu.make_async_copy(hp_s.at[sl, pl.ds(j, 1)],
                                     xs_hbm.at[pl.ds(psm[sl, which, j], 1)], rsem.at[sl])

    def drain(sl):
        def wait(j, c):
            row_copy(sl, 0, 0).wait()
            row_copy(sl, 1, 0).wait()
            return c
        lax.fori_loop(0, TM_MIX, wait, 0, unroll=8)

    @pl.when(i == 0)
    def _():
        run_ref[...] = jnp.zeros_like(run_ref)
        hp_s[1] = jnp.zeros(hp_s.shape[1:], hp_s.dtype)
        spare = N_EXPERTS * XS_CAP + lax.broadcasted_iota(I32, posv_ref.shape, 1)
        spare = spare + TM_MIX * lax.broadcasted_iota(I32, posv_ref.shape, 0)
        posv_ref[...] = jnp.where(spare < xs_hbm.shape[0], spare, 0)
        pos_copy(1).start()

    @pl.when(i > 0)
    def _():
        drain(slot)

    pos_copy(prev).wait()

    for j in range(TM_MIX):
        row_copy(prev, 0, j).start(priority=0)
        row_copy(prev, 1, j).start(priority=1)

    n_chunks = TM_MIX // SG_CHUNK
    rows = []
    for c in range(n_chunks):
        cols = []
        for g in range(SG_GROUPS):
            rs = slice(c * SG_CHUNK, (c + 1) * SG_CHUNK)
            cs = slice(g * SG_DIM, (g + 1) * SG_DIM)
            vt = vg_ref[rs, cs].astype(F32)
            vn = _layernorm(vt, sgg_ref[g:g + 1, :], sgbeta_ref[g:g + 1, :])
            mixed = jnp.dot(sgw_ref[g], vn.astype(BF16), preferred_element_type=F32) + sgb_ref[g]
            cols.append((u_ref[rs, cs].astype(F32) * mixed).astype(BF16))
        rows.append(jnp.concatenate(cols, axis=1))
    sgate = jnp.concatenate(rows, axis=0)

    mix = jnp.dot(a_ref[...], wout_ref[:ATT_WIDTH, :], preferred_element_type=F32)
    mix = mix + jnp.dot(sgate, wout_ref[ATT_WIDTH:, :], preferred_element_type=F32)
    h1 = _layernorm(DN_ALPHA * x_ref[...] + mix, g1_ref[...], b1_ref[...])
    h1_ref[...] = h1
    hp_s[slot] = _pack_bf16_pair(h1)

    nt = (((1,), (1,)), ((), ()))
    h1_hi = h1.astype(BF16)
    h1_lo = (h1 - h1_hi.astype(F32)).astype(BF16)
    both = lax.dot_general(wr_ref[...], h1_hi, nt, preferred_element_type=F32)
    lt = both[:NR_PAD] + both[NR_PAD:] + lax.dot_general(
        wr_ref[:NR_PAD, :], h1_lo, nt, preferred_element_type=F32)
    lt = lt + br_ref[...]
    g_logits = lt[0:N_GROUPS]
    gmax, g_idx = _first_argmax(g_logits, N_GROUPS)
    g_gate = 1.0 / jnp.sum(jnp.exp(g_logits - gmax), axis=0, keepdims=True)
    e_logits = jnp.zeros((EXPERTS_PER_GROUP, TM_MIX), F32)
    for g in range(N_GROUPS):
        lo = NR_EXPERT0 + g * EXPERTS_PER_GROUP
        e_logits = jnp.where(g_idx == g, lt[lo:lo + EXPERTS_PER_GROUP], e_logits)
    v1, i1 = _first_argmax(e_logits, EXPERTS_PER_GROUP)
    erow = lax.broadcasted_iota(I32, e_logits.shape, 0)
    v2, i2 = _first_argmax(jnp.where(erow == i1, -jnp.inf, e_logits), EXPERTS_PER_GROUP)
    t = jnp.exp(v2 - v1)
    w1 = g_gate / (1.0 + t)
    w2 = g_gate * t / (1.0 + t)
    e1 = g_idx * EXPERTS_PER_GROUP + i1
    e2 = g_idx * EXPERTS_PER_GROUP + i2

    xrow = lax.broadcasted_iota(I32, (N_EXPERTS, TM_MIX), 0)
    oh1 = xrow == e1
    oh2 = xrow == e2
    oh = jnp.where(jnp.logical_or(oh1, oh2), 1.0, 0.0)
    tr = lax.broadcasted_iota(I32, (TM_MIX, TM_MIX), 0)
    tc = lax.broadcasted_iota(I32, (TM_MIX, TM_MIX), 1)
    before = jnp.where(tr < tc, 1.0, 0.0).astype(BF16)
    cum = jnp.dot(oh.astype(BF16), before, preferred_element_type=F32)
    base = cum + run_ref[:, 0:1]
    rank1 = jnp.sum(jnp.where(oh1, base, 0.0), axis=0, keepdims=True).astype(I32)
    rank2 = jnp.sum(jnp.where(oh2, base, 0.0), axis=0, keepdims=True).astype(I32)
    run_ref[...] = run_ref[...] + jnp.sum(oh, axis=1, keepdims=True)
    cnt_ref[...] = run_ref[...].astype(I32)

    zi = jnp.zeros((ROUTE_ROWS - 4, TM_MIX), I32)
    ri_ref[...] = jnp.concatenate([e1, e2, rank1, rank2, zi], axis=0)
    zw = jnp.zeros((ROUTE_ROWS - 2, TM_MIX), F32)
    rw_ref[...] = jnp.concatenate([w1, w2, zw], axis=0)

    posv_ref[...] = jnp.concatenate([e1 * XS_CAP + rank1, e2 * XS_CAP + rank2, zi, zi[:2]], axis=0)
    pos_copy(slot).start()

    @pl.when(i == n - 1)
    def _():
        pos_copy(slot).wait()

        def last(j, c):
            row_copy(slot, 0, j).start(priority=0)
            row_copy(slot, 1, j).start(priority=1)
            return c
        lax.fori_loop(0, TM_MIX, last, 0, unroll=8)
        drain(prev)
        drain(slot)


def _mix(a2, proj2, x2, wout_b, sgw_b, sgb, sgg, sgbeta, g1, b1, wr, br):
    t, d = x2.shape
    assert XS_CAP >= t + TMX
    u_blk = (proj2.shape[1] - 2 * SG_WIDTH) // SG_WIDTH
    const = lambda *shape: pl.BlockSpec(shape, lambda i: (0,) * len(shape))
    return pl.pallas_call(
        _mix_kernel,
        grid=(t // TM_MIX,),
        in_specs=[
            pl.BlockSpec((TM_MIX, ATT_WIDTH), lambda i: (i, 0)),
            pl.BlockSpec((TM_MIX, SG_WIDTH), lambda i: (i, u_blk)),
            pl.BlockSpec((TM_MIX, SG_WIDTH), lambda i: (i, u_blk + 1)),
            pl.BlockSpec((TM_MIX, d), lambda i: (i, 0)),
            pl.BlockSpec((d, d), lambda i: (0, 0), pipeline_mode=pl.Buffered(1)),
            const(SG_GROUPS, SG_CHUNK, SG_CHUNK),
            const(SG_GROUPS, SG_CHUNK, 1),
            const(SG_GROUPS, SG_DIM),
            const(SG_GROUPS, SG_DIM),
            const(1, d),
            const(1, d),
            const(2 * NR_PAD, d),
            const(NR_PAD, 1),
        ],
        out_specs=[
            pl.BlockSpec((TM_MIX, d), lambda i: (i, 0)),
            pl.BlockSpec(memory_space=pl.ANY),
            pl.BlockSpec((ROUTE_ROWS, TM_MIX), lambda i: (0, i)),
            pl.BlockSpec((ROUTE_ROWS, TM_MIX), lambda i: (0, i)),
            const(N_EXPERTS, LANES),
        ],
        out_shape=[
            jax.ShapeDtypeStruct((t, d), F32),
            jax.ShapeDtypeStruct((N_EXPERTS * XS_CAP + 2 * TM_MIX, d // 2), U32),
            jax.ShapeDtypeStruct((ROUTE_ROWS, t), I32),
            jax.ShapeDtypeStruct((ROUTE_ROWS, t), F32),
            jax.ShapeDtypeStruct((N_EXPERTS, LANES), I32),
        ],
        scratch_shapes=[pltpu.VMEM((N_EXPERTS, LANES), F32),
                        pltpu.VMEM((2, TM_MIX, d // 2), U32),
                        pltpu.VMEM((ROUTE_ROWS, TM_MIX), I32),
                        pltpu.SMEM((2, ROUTE_ROWS, TM_MIX), I32),
                        pltpu.SemaphoreType.DMA((2,)),
                        pltpu.SemaphoreType.DMA((2,))],
        compiler_params=_cparams(("arbitrary",)),
        name="mix",
    )(a2, proj2, proj2, x2, wout_b, sgw_b, sgb, sgg, sgbeta, g1, b1, wr, br)


def _experts_kernel(ts_ref, nt_ref, nu_ref, trow_ref, nval_ref, x_hbm, wg_ref, wu_ref, wd_ref,
                    y_hbm, xbuf, ybuf, xsem, ysem):
    e = pl.program_id(0)
    d = wg_ref.shape[1]
    n_used = nu_ref[0]

    def x_copy(g, slot):
        rows = pl.ds(pl.multiple_of(trow_ref[g], TMX), TMX)
        return pltpu.make_async_copy(x_hbm.at[rows], xbuf.at[slot], xsem.at[slot])

    def y_copy(g, slot):
        rows = pl.ds(pl.multiple_of(g * TMX, TMX), TMX)
        return pltpu.make_async_copy(ybuf.at[slot], y_hbm.at[rows], ysem.at[slot])

    @pl.when(jnp.logical_and(e == 0, n_used > 0))
    def _():
        x_copy(0, 0).start(priority=1)

    def tile(t, carry):
        g = ts_ref[e] + t
        slot = g % 2
        x_copy(g, slot).wait()

        @pl.when(g + 1 < n_used)
        def _():
            x_copy(g + 1, 1 - slot).start(priority=1)

        @pl.when(g >= 2)
        def _():
            y_copy(g - 2, slot).wait()

        row = lax.broadcasted_iota(I32, xbuf.shape[1:], 0)
        xw = jnp.where(row < nval_ref[g], xbuf[slot], jnp.uint32(0))
        lo, hi = _unpack_bf16_pair(xw)
        lo, hi = lo.astype(BF16), hi.astype(BF16)
        gate = jnp.dot(lo, wg_ref[0, :d // 2, :], preferred_element_type=F32)
        gate = gate + jnp.dot(hi, wg_ref[0, d // 2:, :], preferred_element_type=F32)
        up = jnp.dot(lo, wu_ref[0, :d // 2, :], preferred_element_type=F32)
        up = up + jnp.dot(hi, wu_ref[0, d // 2:, :], preferred_element_type=F32)
        hmid = (gate * jax.nn.sigmoid(gate) * up).astype(BF16)
        y = jnp.dot(hmid, wd_ref[0], preferred_element_type=F32)
        ybuf[slot] = _pack_bf16_pair(y)
        y_copy(g, slot).start(priority=1)
        return carry

    lax.fori_loop(0, nt_ref[e], tile, 0)

    @pl.when(e == pl.num_programs(0) - 1)
    def _():
        @pl.when(n_used >= 2)
        def _():
            y_copy(n_used - 2, (n_used - 2) % 2).wait()

        @pl.when(n_used >= 1)
        def _():
            y_copy(n_used - 1, (n_used - 1) % 2).wait()


def _experts(tile_start, n_tiles_e, n_used, tile_row, tile_valid, xs, w_gate, w_up, w_down):
    w = xs.shape[1]
    d = w_gate.shape[1]
    n_rows = tile_row.shape[0] * TMX

    def w_map(e, ts, nt, nu, trow, nval):
        return (e, 0, 0)

    grid_spec = pltpu.PrefetchScalarGridSpec(
        num_scalar_prefetch=5,
        grid=(N_EXPERTS,),
        in_specs=[
            pl.BlockSpec(memory_space=pl.ANY),
            pl.BlockSpec((1, d, D_EXPERT), w_map),
            pl.BlockSpec((1, d, D_EXPERT), w_map),
            pl.BlockSpec((1, D_EXPERT, d), w_map),
        ],
        out_specs=pl.BlockSpec(memory_space=pl.ANY),
        scratch_shapes=[pltpu.VMEM((2, TMX, w), U32),
                        pltpu.VMEM((2, TMX, w), U32),
                        pltpu.SemaphoreType.DMA((2,)),
                        pltpu.SemaphoreType.DMA((2,))],
    )
    return pl.pallas_call(
        _experts_kernel,
        grid_spec=grid_spec,
        out_shape=jax.ShapeDtypeStruct((n_rows, w), U32),
        compiler_params=_cparams(("arbitrary",)),
        name="experts",
    )(tile_start, n_tiles_e, n_used, tile_row, tile_valid, xs, w_gate, w_up, w_down)


def _combine_kernel(pos_ref, posn_ref, y_hbm, h1_ref, wt_ref, g2_ref, b2_ref, o_ref, ybuf, sem):
    i = pl.program_id(0)
    n = pl.num_programs(0)
    slot = i % 2

    def row_copy(src, sl, which, j):
        return pltpu.make_async_copy(y_hbm.at[pl.ds(src, 1)],
                                     ybuf.at[sl, which, pl.ds(j, 1)], sem.at[sl])

    def issue(p_ref, sl):
        def body(j, c):
            row_copy(p_ref[0, j], sl, 0, j).start(priority=0)
            row_copy(p_ref[1, j], sl, 1, j).start(priority=1)
            return c
        lax.fori_loop(0, TM_OUT, body, 0, unroll=8)

    def drain(sl):
        def wait(j, c):
            row_copy(0, sl, 0, 0).wait()
            row_copy(0, sl, 1, 0).wait()
            return c
        lax.fori_loop(0, TM_OUT, wait, 0, unroll=8)

    @pl.when(i == 0)
    def _():
        issue(pos_ref, 0)

    drain(slot)

    for j in range(TM_OUT):
        row_copy(posn_ref[0, j], 1 - slot, 0, j).start(priority=0)
        row_copy(posn_ref[1, j], 1 - slot, 1, j).start(priority=1)

    lo1, hi1 = _unpack_bf16_pair(ybuf[slot, 0])
    lo2, hi2 = _unpack_bf16_pair(ybuf[slot, 1])
    w1 = wt_ref[:, 0:1]
    w2 = wt_ref[:, 1:2]
    ffn = jnp.concatenate([w1 * lo1 + w2 * lo2, w1 * hi1 + w2 * hi2], axis=1)
    o_ref[...] = _layernorm(DN_ALPHA * h1_ref[...] + ffn, g2_ref[...], b2_ref[...])

    @pl.when(i == n - 1)
    def _():
        drain(1 - slot)


def _combine(pos3, ys, h1, wtok, g2, b2):
    t, d = h1.shape
    w = ys.shape[1]
    n = t // TM_OUT
    return pl.pallas_call(
        _combine_kernel,
        grid=(n,),
        in_specs=[
            pl.BlockSpec((None, 2, TM_OUT), lambda i: (i, 0, 0), memory_space=pltpu.SMEM),
            pl.BlockSpec((None, 2, TM_OUT), lambda i: (jnp.minimum(i + 1, n - 1), 0, 0),
                         memory_space=pltpu.SMEM),
            pl.BlockSpec(memory_space=pl.ANY),
            pl.BlockSpec((TM_OUT, d), lambda i: (i, 0)),
            pl.BlockSpec((TM_OUT, ROUTE_ROWS), lambda i: (i, 0)),
            pl.BlockSpec((1, d), lambda i: (0, 0)),
            pl.BlockSpec((1, d), lambda i: (0, 0)),
        ],
        out_specs=pl.BlockSpec((TM_OUT, d), lambda i: (i, 0)),
        out_shape=jax.ShapeDtypeStruct((t, d), F32),
        scratch_shapes=[pltpu.VMEM((2, 2, TM_OUT, w), U32),
                        pltpu.SemaphoreType.DMA((2,))],
        compiler_params=_cparams(("arbitrary",)),
        name="combine",
    )(pos3, pos3, ys, h1, wtok, g2, b2)


def _lambda_init(layer):
    return 0.8 - 0.6 * math.exp(-0.3 * layer)


def kernel(x, w_in, w_out, ln1_g, ln1_b, ln2_g, ln2_b, rel_bias, lam_q1, lam_k1, lam_q2, lam_k2,
           subln_g, sg_ln_g, sg_ln_b, sg_w, sg_b, w_router_group, b_router_group,
           w_router_expert, b_router_expert, w_exp_gate, w_exp_up, w_exp_down):
    b, s, d = x.shape
    t = b * s
    assert w_in.shape[0] == DEPTH == 1
    l = 0
    lam_init = _lambda_init(l)
    x2 = x.reshape(t, d)

    proj = _inproj(x2, w_in[l])

    lam = (jnp.exp(jnp.sum(lam_q1[l].astype(F32) * lam_k1[l].astype(F32)))
           - jnp.exp(jnp.sum(lam_q2[l].astype(F32) * lam_k2[l].astype(F32))) + lam_init)
    side = [w_exp_gate[l].reshape(-1, D_EXPERT), w_exp_up[l].reshape(-1, D_EXPERT),
            w_exp_down[l].reshape(-1, d)]
    a, wg_b, wu_b, wd_b = _attention(
        proj.reshape(b, s, -1), _bias_sequences(rel_bias, s), lam.reshape(1),
        subln_g[l].reshape(1, -1), lam_init, side)

    wr = jnp.concatenate([w_router_group[l].T,
                          jnp.zeros((NR_EXPERT0 - N_GROUPS, d), F32),
                          jnp.transpose(w_router_expert[l], (0, 2, 1)).reshape(N_EXPERTS, d)],
                         axis=0)
    br = jnp.concatenate([b_router_group[l], jnp.zeros((NR_EXPERT0 - N_GROUPS,), F32),
                          b_router_expert[l].reshape(-1)]).reshape(NR_PAD, 1)
    wr_hi = wr.astype(BF16)
    wr = jnp.concatenate([wr_hi, (wr - wr_hi.astype(F32)).astype(BF16)], axis=0)
    h1, xs, ri, rw, cnt = _mix(
        a.reshape(t, -1), proj, x2, w_out[l].astype(BF16), sg_w[l].astype(BF16),
        sg_b[l].reshape(SG_GROUPS, SG_CHUNK, 1), sg_ln_g[l], sg_ln_b[l],
        ln1_g[l].reshape(1, d), ln1_b[l].reshape(1, d), wr, br)

    counts = cnt[:, 0]
    tiles_e = (counts + TMX - 1) // TMX
    tile_end = jnp.cumsum(tiles_e)
    tile_start = tile_end - tiles_e
    n_used = tile_end[-1:]
    max_tiles = (2 * t) // TMX + N_EXPERTS
    row_start = (tile_start * TMX).astype(I32)
    eids = jnp.arange(N_EXPERTS, dtype=I32)[:, None, None]
    pos = jnp.sum(jnp.where(ri[0:2][None] == eids, row_start[:, None, None], 0), axis=0) + ri[2:4]
    pos3 = pos.reshape(2, t // TM_OUT, TM_OUT).transpose(1, 0, 2)
    tile_ids = jnp.arange(max_tiles, dtype=I32)
    tile_e = jnp.minimum(jnp.searchsorted(tile_end, tile_ids, side="right"), N_EXPERTS - 1)
    local = tile_ids - tile_start[tile_e]
    in_use = tile_ids < n_used[0]
    tile_row = jnp.where(in_use, tile_e * XS_CAP + local * TMX, 0).astype(I32)
    tile_valid = jnp.where(in_use, jnp.clip(counts[tile_e] - local * TMX, 0, TMX), 0).astype(I32)

    ys = _experts(tile_start.astype(I32), tiles_e.astype(I32), n_used.astype(I32),
                  tile_row, tile_valid, xs,
                  wg_b.reshape(N_EXPERTS, d, D_EXPERT), wu_b.reshape(N_EXPERTS, d, D_EXPERT),
                  wd_b.reshape(N_EXPERTS, D_EXPERT, d))
    out = _combine(pos3, ys, h1, rw.T, ln2_g[l].reshape(1, d), ln2_b[l].reshape(1, d))
    return out.reshape(b, s, d)
```

```python
import functools
import math

import jax
import jax.numpy as jnp
from jax import lax
from jax.experimental import pallas as pl
from jax.experimental.pallas import tpu as pltpu

F32 = jnp.float32
BF16 = jnp.bfloat16
U32 = jnp.uint32
I32 = jnp.int32

ATT_HEADS = 8
ATT_QK_DIM = 64
ATT_V_DIM = 128
ATT_WIDTH = ATT_HEADS * ATT_V_DIM
SG_GROUPS = 8
SG_DIM = 128
SG_WIDTH = SG_GROUPS * SG_DIM
SG_CHUNK = 128
REL_BUCKETS = 32
REL_MAX_DIST = 128
N_GROUPS = 4
EXPERTS_PER_GROUP = 8
N_EXPERTS = N_GROUPS * EXPERTS_PER_GROUP
D_EXPERT = 256
DEPTH = 1
DN_ALPHA = (2.0 * DEPTH) ** 0.25
LN_EPS = 1e-5
LOG2E = math.log2(math.e)

LANES = 128
SUBLANES = 8
VMEM_LIMIT_BYTES = 56 * 1024 * 1024

TM_IN = 1024
TN_IN = 1024
QB = 256
KC = 256
N_BIAS_TILES = 5
TM_MIX = 512
TMX = 256
TM_OUT = 256
ROUTE_ROWS = 8
XS_CAP = 8192 + TMX
NR_EXPERT0 = 8
NR_PAD = NR_EXPERT0 + 32


def _cparams(sem, flags=None):
    return pltpu.CompilerParams(dimension_semantics=sem, vmem_limit_bytes=VMEM_LIMIT_BYTES,
                                flags=flags)


def _inproj_kernel(x_ref, w_ref, o_ref, xb_ref, *, n_q, n_lin, q_scale):
    j = pl.program_id(1)

    @pl.when(j == 0)
    def _():
        xb_ref[...] = x_ref[...].astype(BF16)

    acc = jnp.dot(xb_ref[...], w_ref[...].astype(BF16), preferred_element_type=F32)

    @pl.when(j < n_q)
    def _():
        o_ref[...] = (acc * q_scale).astype(BF16)

    @pl.when(jnp.logical_and(j >= n_q, j < n_lin))
    def _():
        o_ref[...] = acc.astype(BF16)

    @pl.when(j >= n_lin)
    def _():
        o_ref[...] = jax.nn.gelu(acc).astype(BF16)


def _inproj(x2, w_in):
    t, d = x2.shape
    n = w_in.shape[1]
    n_q = (ATT_HEADS * 2 * ATT_QK_DIM) // TN_IN
    n_lin = (2 * ATT_HEADS * 2 * ATT_QK_DIM + ATT_WIDTH) // TN_IN
    kern = functools.partial(_inproj_kernel, n_q=n_q, n_lin=n_lin,
                             q_scale=(ATT_QK_DIM ** -0.5) * LOG2E)
    return pl.pallas_call(
        kern,
        grid=(t // TM_IN, n // TN_IN),
        in_specs=[pl.BlockSpec((TM_IN, d), lambda i, j: (i, 0)),
                  pl.BlockSpec((d, TN_IN), lambda i, j: (0, j))],
        out_specs=pl.BlockSpec((TM_IN, TN_IN), lambda i, j: (i, j)),
        out_shape=jax.ShapeDtypeStruct((t, n), BF16),
        scratch_shapes=[pltpu.VMEM((TM_IN, d), BF16)],
        compiler_params=_cparams(("arbitrary", "arbitrary")),
        name="inproj",
    )(x2, w_in)


def _attn_kernel(lam_ref, q_ref, k_ref, v_ref, bseq_ref, g_ref, *rest, n_kc, out_scale):
    n_side = (len(rest) - 3) // 2
    side_in, o_ref, side_out = rest[:n_side], rest[n_side], rest[n_side + 1:2 * n_side + 1]
    vaug_ref, bias_ref = rest[2 * n_side + 1:]
    qi = pl.program_id(2)

    @pl.when(qi == 0)
    def _():
        vaug_ref[:, :ATT_V_DIM] = v_ref[0]
        vaug_ref[:, ATT_V_DIM:] = jnp.ones((vaug_ref.shape[0], ATT_V_DIM), BF16)
        for t in range(N_BIAS_TILES):
            seq = jnp.broadcast_to(bseq_ref[0, t:t + 1, :], (QB, 2 * KC))
            bias_ref[t] = pltpu.roll(seq, 0, 1, stride=1, stride_axis=0)[:, :KC]

    q = q_ref[0]
    lane = lax.broadcasted_iota(I32, q.shape, 1)
    zero = jnp.zeros_like(q)
    qq = jnp.concatenate([jnp.where(lane < ATT_QK_DIM, q, zero),
                          jnp.where(lane >= ATT_QK_DIM, q, zero)], axis=0)

    for src, dst in zip(side_in, side_out):
        dst[...] = src[...].astype(dst.dtype)

    m = jnp.full((2 * QB, 1), -jnp.inf, F32)
    acc = jnp.zeros((2 * QB, 2 * ATT_V_DIM), F32)
    for kc in range(n_kc):
        kblk = k_ref[0, kc * KC:(kc + 1) * KC, :]
        s = lax.dot_general(qq, kblk, (((1,), (1,)), ((), ())),
                            preferred_element_type=F32)
        b = bias_ref[jnp.clip(kc - qi, -2, 2) + 2]
        s = s + jnp.concatenate([b, b], axis=0)
        m_new = jnp.maximum(m, jnp.max(s, axis=1, keepdims=True))
        p = jnp.exp2(s - m_new).astype(BF16)
        acc = acc * jnp.exp2(m - m_new) + jnp.dot(
            p, vaug_ref[kc * KC:(kc + 1) * KC, :], preferred_element_type=F32)
        m = m_new
    lam = lam_ref[0]
    o0 = acc[:QB, :ATT_V_DIM] / acc[:QB, ATT_V_DIM:ATT_V_DIM + 1]
    o1 = acc[QB:, :ATT_V_DIM] / acc[QB:, ATT_V_DIM:ATT_V_DIM + 1]
    of = o0 - lam * o1
    of = of * lax.rsqrt(jnp.mean(of * of, axis=-1, keepdims=True) + LN_EPS)
    o_ref[0] = (of * (g_ref[...] * out_scale)).astype(o_ref.dtype)


def _attention(proj3, bias_seq, lam, subln_g, lam_init, side):
    b, s, _ = proj3.shape
    n_kc = s // KC
    n_q = s // QB
    n_steps = b * ATT_HEADS * n_q
    h_off_k = ATT_HEADS
    h_off_v = 2 * ATT_HEADS
    kern = functools.partial(_attn_kernel, n_kc=n_kc, out_scale=1.0 - lam_init)

    def side_map(bi, h, qi):
        return ((bi * ATT_HEADS + h) * n_q + qi, 0)

    side_specs = []
    for arr in side:
        rows, cols = arr.shape
        assert rows % (n_steps * 2 * SUBLANES) == 0
        side_specs.append(pl.BlockSpec((rows // n_steps, cols), side_map))
    return pl.pallas_call(
        kern,
        grid=(b, ATT_HEADS, n_q),
        in_specs=[
            pl.BlockSpec(memory_space=pltpu.SMEM),
            pl.BlockSpec((1, QB, LANES), lambda bi, h, qi: (bi, qi, h)),
            pl.BlockSpec((1, s, LANES), lambda bi, h, qi: (bi, 0, h_off_k + h)),
            pl.BlockSpec((1, s, LANES), lambda bi, h, qi: (bi, 0, h_off_v + h)),
            pl.BlockSpec((1, N_BIAS_TILES, 2 * KC), lambda bi, h, qi: (h, 0, 0)),
            pl.BlockSpec((1, ATT_V_DIM), lambda bi, h, qi: (0, 0)),
        ] + side_specs,
        out_specs=[pl.BlockSpec((1, QB, ATT_V_DIM), lambda bi, h, qi: (bi, qi, h))] + side_specs,
        out_shape=[jax.ShapeDtypeStruct((b, s, ATT_WIDTH), BF16)]
        + [jax.ShapeDtypeStruct(arr.shape, BF16) for arr in side],
        scratch_shapes=[pltpu.VMEM((s, 2 * ATT_V_DIM), BF16),
                        pltpu.VMEM((N_BIAS_TILES, QB, KC), F32)],
        compiler_params=_cparams(("arbitrary", "arbitrary", "arbitrary")),
        name="attn",
    )(lam, proj3, proj3, proj3, bias_seq, subln_g, *side)


def _rel_bucket(rel):
    half = REL_BUCKETS // 2
    max_exact = half // 2
    ret = jnp.where(rel > 0, half, 0)
    n = jnp.abs(rel)
    nf = jnp.maximum(n, 1).astype(F32)
    large = max_exact + (jnp.log(nf / max_exact) / math.log(REL_MAX_DIST / max_exact)
                         * (half - max_exact)).astype(I32)
    large = jnp.minimum(large, half - 1)
    return ret + jnp.where(n < max_exact, n, large)


def _bias_sequences(rel_bias, s):
    assert REL_MAX_DIST <= KC and QB == KC
    period = 2 * KC
    j = jnp.arange(period, dtype=I32)
    off = jnp.where(j < KC, j, j - period)
    d = jnp.arange(-2, 3, dtype=I32)[:, None]
    rel = jnp.clip(d * KC + off[None, :], -(s - 1), s - 1)
    return jnp.transpose(rel_bias[_rel_bucket(rel)], (2, 0, 1)).astype(F32) * LOG2E


def _layernorm(v, g, b):
    mu = jnp.mean(v, axis=-1, keepdims=True)
    var = jnp.mean(jnp.square(v - mu), axis=-1, keepdims=True)
    return (v - mu) * lax.rsqrt(var + LN_EPS) * g + b


def _pack_bf16_pair(v):
    n = v.shape[1] // 2
    bits = lax.bitcast_convert_type(v.astype(BF16).astype(F32), U32)
    return (bits[:, :n] >> 16) | (bits[:, n:] & jnp.uint32(0xFFFF0000))


def _unpack_bf16_pair(w):
    lo = lax.bitcast_convert_type(w << 16, F32)
    hi = lax.bitcast_convert_type(w & jnp.uint32(0xFFFF0000), F32)
    return lo, hi


def _first_argmax(vals, nrows):
    mx = jnp.max(vals, axis=0, keepdims=True)
    row = lax.broadcasted_iota(I32, vals.shape, 0)
    idx = jnp.min(jnp.where(vals == mx, row, nrows), axis=0, keepdims=True)
    return mx, idx


def _mix_kernel(a_ref, u_ref, vg_ref, x_ref, wout_ref, sgw_ref, sgb_ref, sgg_ref, sgbeta_ref,
                g1_ref, b1_ref, wr_ref, br_ref,
                h1_ref, xs_hbm, ri_ref, rw_ref, cnt_ref,
                run_ref, hp_s, posv_ref, psm, ssem, rsem):
    i = pl.program_id(0)
    n = pl.num_programs(0)
    slot = i % 2
    prev = 1 - slot

    def pos_copy(sl):
        return pltpu.make_async_copy(posv_ref, psm.at[sl], ssem.at[sl])

    def row_copy(sl, which, j):
        return pltpu.make_async_copy(hp_s.at[sl, pl.ds(j, 1)],
                                     xs_hbm.at[pl.ds(psm[sl, which, j], 1)], rsem.at[sl])

    def drain(sl):
        def wait(j, c):
            row_copy(sl, 0, 0).wait()
            row_copy(sl, 1, 0).wait()
            return c
        lax.fori_loop(0, TM_MIX, wait, 0, unroll=8)

    @pl.when(i == 0)
    def _():
        run_ref[...] = jnp.zeros_like(run_ref)
        hp_s[1] = jnp.zeros(hp_s.shape[1:], hp_s.dtype)
        spare = N_EXPERTS * XS_CAP + lax.broadcasted_iota(I32, posv_ref.shape, 1)
        spare = spare + TM_MIX * lax.broadcasted_iota(I32, posv_ref.shape, 0)
        posv_ref[...] = jnp.where(spare < xs_hbm.shape[0], spare, 0)
        pos_copy(1).start()

    @pl.when(i > 0)
    def _():
        drain(slot)

    pos_copy(prev).wait()

    for j in range(TM_MIX):
        row_copy(prev, 0, j).start(priority=0)
        row_copy(prev, 1, j).start(priority=1)

    h1_ref[...] = jnp.dot(a_ref[...], wout_ref[:ATT_WIDTH, :], preferred_element_type=F32)

    n_chunks = TM_MIX // SG_CHUNK
    rows = []
    for c in range(n_chunks):
        cols = []
        for g in range(SG_GROUPS):
            rs = slice(c * SG_CHUNK, (c + 1) * SG_CHUNK)
            cs = slice(g * SG_DIM, (g + 1) * SG_DIM)
            vt = vg_ref[rs, cs].astype(F32)
            vn = _layernorm(vt, sgg_ref[g:g + 1, :], sgbeta_ref[g:g + 1, :])
            mixed = jnp.dot(sgw_ref[g], vn.astype(BF16), preferred_element_type=F32) + sgb_ref[g]
            cols.append((u_ref[rs, cs].astype(F32) * mixed).astype(BF16))
        rows.append(jnp.concatenate(cols, axis=1))
    sgate = jnp.concatenate(rows, axis=0)

    mix = h1_ref[...] + jnp.dot(sgate, wout_ref[ATT_WIDTH:, :], preferred_element_type=F32)
    h1 = _layernorm(DN_ALPHA * x_ref[...] + mix, g1_ref[...], b1_ref[...])
    h1_ref[...] = h1
    hp_s[slot] = _pack_bf16_pair(h1)

    nt = (((1,), (1,)), ((), ()))
    h1_hi = h1.astype(BF16)
    h1_lo = (h1 - h1_hi.astype(F32)).astype(BF16)
    both = lax.dot_general(wr_ref[...], h1_hi, nt, preferred_element_type=F32)
    lt = both[:NR_PAD] + both[NR_PAD:] + lax.dot_general(
        wr_ref[:NR_PAD, :], h1_lo, nt, preferred_element_type=F32)
    lt = lt + br_ref[...]
    g_logits = lt[0:N_GROUPS]
    gmax, g_idx = _first_argmax(g_logits, N_GROUPS)
    g_gate = 1.0 / jnp.sum(jnp.exp(g_logits - gmax), axis=0, keepdims=True)
    e_logits = jnp.zeros((EXPERTS_PER_GROUP, TM_MIX), F32)
    for g in range(N_GROUPS):
        lo = NR_EXPERT0 + g * EXPERTS_PER_GROUP
        e_logits = jnp.where(g_idx == g, lt[lo:lo + EXPERTS_PER_GROUP], e_logits)
    v1, i1 = _first_argmax(e_logits, EXPERTS_PER_GROUP)
    erow = lax.broadcasted_iota(I32, e_logits.shape, 0)
    v2, i2 = _first_argmax(jnp.where(erow == i1, -jnp.inf, e_logits), EXPERTS_PER_GROUP)
    t = jnp.exp(v2 - v1)
    w1 = g_gate / (1.0 + t)
    w2 = g_gate * t / (1.0 + t)
    e1 = g_idx * EXPERTS_PER_GROUP + i1
    e2 = g_idx * EXPERTS_PER_GROUP + i2

    xrow = lax.broadcasted_iota(I32, (N_EXPERTS, TM_MIX), 0)
    oh1 = xrow == e1
    oh2 = xrow == e2
    oh = jnp.where(jnp.logical_or(oh1, oh2), 1.0, 0.0)
    tr = lax.broadcasted_iota(I32, (TM_MIX, TM_MIX), 0)
    tc = lax.broadcasted_iota(I32, (TM_MIX, TM_MIX), 1)
    before = jnp.where(tr < tc, 1.0, 0.0).astype(BF16)
    cum = jnp.dot(oh.astype(BF16), before, preferred_element_type=F32)
    base = cum + run_ref[:, 0:1]
    rank1 = jnp.sum(jnp.where(oh1, base, 0.0), axis=0, keepdims=True).astype(I32)
    rank2 = jnp.sum(jnp.where(oh2, base, 0.0), axis=0, keepdims=True).astype(I32)
    run_ref[...] = run_ref[...] + jnp.sum(oh, axis=1, keepdims=True)
    cnt_ref[...] = run_ref[...].astype(I32)

    zi = jnp.zeros((ROUTE_ROWS - 4, TM_MIX), I32)
    ri_ref[...] = jnp.concatenate([e1, e2, rank1, rank2, zi], axis=0)
    zw = jnp.zeros((ROUTE_ROWS - 2, TM_MIX), F32)
    rw_ref[...] = jnp.concatenate([w1, w2, zw], axis=0)

    posv_ref[...] = jnp.concatenate([e1 * XS_CAP + rank1, e2 * XS_CAP + rank2, zi, zi[:2]], axis=0)
    pos_copy(slot).start()

    @pl.when(i == n - 1)
    def _():
        pos_copy(slot).wait()

        def last(j, c):
            row_copy(slot, 0, j).start(priority=0)
            row_copy(slot, 1, j).start(priority=1)
            return c
        lax.fori_loop(0, TM_MIX, last, 0, unroll=8)
        drain(prev)
        drain(slot)


def _mix(a2, proj2, x2, wout_b, sgw_b, sgb, sgg, sgbeta, g1, b1, wr, br):
    t, d = x2.shape
    assert XS_CAP >= t + TMX
    u_blk = (proj2.shape[1] - 2 * SG_WIDTH) // SG_WIDTH
    const = lambda *shape: pl.BlockSpec(shape, lambda i: (0,) * len(shape))
    return pl.pallas_call(
        _mix_kernel,
        grid=(t // TM_MIX,),
        in_specs=[
            pl.BlockSpec((TM_MIX, ATT_WIDTH), lambda i: (i, 0)),
            pl.BlockSpec((TM_MIX, SG_WIDTH), lambda i: (i, u_blk)),
            pl.BlockSpec((TM_MIX, SG_WIDTH), lambda i: (i, u_blk + 1)),
            pl.BlockSpec((TM_MIX, d), lambda i: (i, 0)),
            pl.BlockSpec((d, d), lambda i: (0, 0), pipeline_mode=pl.Buffered(1)),
            const(SG_GROUPS, SG_CHUNK, SG_CHUNK),
            const(SG_GROUPS, SG_CHUNK, 1),
            const(SG_GROUPS, SG_DIM),
            const(SG_GROUPS, SG_DIM),
            const(1, d),
            const(1, d),
            const(2 * NR_PAD, d),
            const(NR_PAD, 1),
        ],
        out_specs=[
            pl.BlockSpec((TM_MIX, d), lambda i: (i, 0)),
            pl.BlockSpec(memory_space=pl.ANY),
            pl.BlockSpec((ROUTE_ROWS, TM_MIX), lambda i: (0, i)),
            pl.BlockSpec((ROUTE_ROWS, TM_MIX), lambda i: (0, i)),
            const(N_EXPERTS, LANES),
        ],
        out_shape=[
            jax.ShapeDtypeStruct((t, d), F32),
            jax.ShapeDtypeStruct((N_EXPERTS * XS_CAP + 2 * TM_MIX, d // 2), U32),
            jax.ShapeDtypeStruct((ROUTE_ROWS, t), I32),
            jax.ShapeDtypeStruct((ROUTE_ROWS, t), F32),
            jax.ShapeDtypeStruct((N_EXPERTS, LANES), I32),
        ],
        scratch_shapes=[pltpu.VMEM((N_EXPERTS, LANES), F32),
                        pltpu.VMEM((2, TM_MIX, d // 2), U32),
                        pltpu.VMEM((ROUTE_ROWS, TM_MIX), I32),
                        pltpu.SMEM((2, ROUTE_ROWS, TM_MIX), I32),
                        pltpu.SemaphoreType.DMA((2,)),
                        pltpu.SemaphoreType.DMA((2,))],
        compiler_params=_cparams(("arbitrary",)),
        name="mix",
    )(a2, proj2, proj2, x2, wout_b, sgw_b, sgb, sgg, sgbeta, g1, b1, wr, br)


def _experts_kernel(ts_ref, nt_ref, nu_ref, trow_ref, nval_ref, x_hbm, wg_ref, wu_ref, wd_ref,
                    y_hbm, xbuf, ybuf, xsem, ysem):
    e = pl.program_id(0)
    d = wg_ref.shape[1]
    n_used = nu_ref[0]

    def x_copy(g, slot):
        rows = pl.ds(pl.multiple_of(trow_ref[g], TMX), TMX)
        return pltpu.make_async_copy(x_hbm.at[rows], xbuf.at[slot], xsem.at[slot])

    def y_copy(g, slot):
        rows = pl.ds(pl.multiple_of(g * TMX, TMX), TMX)
        return pltpu.make_async_copy(ybuf.at[slot], y_hbm.at[rows], ysem.at[slot])

    @pl.when(jnp.logical_and(e == 0, n_used > 0))
    def _():
        x_copy(0, 0).start(priority=1)

    def tile(t, carry):
        g = ts_ref[e] + t
        slot = g % 2
        x_copy(g, slot).wait()

        @pl.when(g + 1 < n_used)
        def _():
            x_copy(g + 1, 1 - slot).start(priority=1)

        @pl.when(g >= 2)
        def _():
            y_copy(g - 2, slot).wait()

        row = lax.broadcasted_iota(I32, xbuf.shape[1:], 0)
        xw = jnp.where(row < nval_ref[g], xbuf[slot], jnp.uint32(0))
        lo, hi = _unpack_bf16_pair(xw)
        lo, hi = lo.astype(BF16), hi.astype(BF16)
        gate = jnp.dot(lo, wg_ref[0, :d // 2, :], preferred_element_type=F32)
        gate = gate + jnp.dot(hi, wg_ref[0, d // 2:, :], preferred_element_type=F32)
        up = jnp.dot(lo, wu_ref[0, :d // 2, :], preferred_element_type=F32)
        up = up + jnp.dot(hi, wu_ref[0, d // 2:, :], preferred_element_type=F32)
        hmid = (gate * jax.nn.sigmoid(gate) * up).astype(BF16)
        y = jnp.dot(hmid, wd_ref[0], preferred_element_type=F32)
        ybuf[slot] = _pack_bf16_pair(y)
        y_copy(g, slot).start(priority=1)
        return carry

    lax.fori_loop(0, nt_ref[e], tile, 0)

    @pl.when(e == pl.num_programs(0) - 1)
    def _():
        @pl.when(n_used >= 2)
        def _():
            y_copy(n_used - 2, (n_used - 2) % 2).wait()

        @pl.when(n_used >= 1)
        def _():
            y_copy(n_used - 1, (n_used - 1) % 2).wait()


def _experts(tile_start, n_tiles_e, n_used, tile_row, tile_valid, xs, w_gate, w_up, w_down):
    w = xs.shape[1]
    d = w_gate.shape[1]
    n_rows = tile_row.shape[0] * TMX

    def w_map(e, ts, nt, nu, trow, nval):
        return (e, 0, 0)

    grid_spec = pltpu.PrefetchScalarGridSpec(
        num_scalar_prefetch=5,
        grid=(N_EXPERTS,),
        in_specs=[
            pl.BlockSpec(memory_space=pl.ANY),
            pl.BlockSpec((1, d, D_EXPERT), w_map),
            pl.BlockSpec((1, d, D_EXPERT), w_map),
            pl.BlockSpec((1, D_EXPERT, d), w_map),
        ],
        out_specs=pl.BlockSpec(memory_space=pl.ANY),
        scratch_shapes=[pltpu.VMEM((2, TMX, w), U32),
                        pltpu.VMEM((2, TMX, w), U32),
                        pltpu.SemaphoreType.DMA((2,)),
                        pltpu.SemaphoreType.DMA((2,))],
    )
    return pl.pallas_call(
        _experts_kernel,
        grid_spec=grid_spec,
        out_shape=jax.ShapeDtypeStruct((n_rows, w), U32),
        compiler_params=_cparams(("arbitrary",)),
        name="experts",
    )(tile_start, n_tiles_e, n_used, tile_row, tile_valid, xs, w_gate, w_up, w_down)


def _combine_kernel(pos_ref, posn_ref, y_hbm, h1_ref, wt_ref, g2_ref, b2_ref, o_ref, ybuf, sem):
    i = pl.program_id(0)
    n = pl.num_programs(0)
    slot = i % 2

    def row_copy(src, sl, which, j):
        return pltpu.make_async_copy(y_hbm.at[pl.ds(src, 1)],
                                     ybuf.at[sl, which, pl.ds(j, 1)], sem.at[sl])

    def issue(p_ref, sl):
        def body(j, c):
            row_copy(p_ref[0, j], sl, 0, j).start(priority=0)
            row_copy(p_ref[1, j], sl, 1, j).start(priority=1)
            return c
        lax.fori_loop(0, TM_OUT, body, 0, unroll=8)

    def drain(sl):
        def wait(j, c):
            row_copy(0, sl, 0, 0).wait()
            row_copy(0, sl, 1, 0).wait()
            return c
        lax.fori_loop(0, TM_OUT, wait, 0, unroll=8)

    @pl.when(i == 0)
    def _():
        issue(pos_ref, 0)

    drain(slot)

    for j in range(TM_OUT):
        row_copy(posn_ref[0, j], 1 - slot, 0, j).start(priority=0)
        row_copy(posn_ref[1, j], 1 - slot, 1, j).start(priority=1)

    lo1, hi1 = _unpack_bf16_pair(ybuf[slot, 0])
    lo2, hi2 = _unpack_bf16_pair(ybuf[slot, 1])
    w1 = wt_ref[:, 0:1]
    w2 = wt_ref[:, 1:2]
    ffn = jnp.concatenate([w1 * lo1 + w2 * lo2, w1 * hi1 + w2 * hi2], axis=1)
    o_ref[...] = _layernorm(DN_ALPHA * h1_ref[...] + ffn, g2_ref[...], b2_ref[...])

    @pl.when(i == n - 1)
    def _():
        drain(1 - slot)


def _combine(pos3, ys, h1, wtok, g2, b2):
    t, d = h1.shape
    w = ys.shape[1]
    n = t // TM_OUT
    return pl.pallas_call(
        _combine_kernel,
        grid=(n,),
        in_specs=[
            pl.BlockSpec((None, 2, TM_OUT), lambda i: (i, 0, 0), memory_space=pltpu.SMEM),
            pl.BlockSpec((None, 2, TM_OUT), lambda i: (jnp.minimum(i + 1, n - 1), 0, 0),
                         memory_space=pltpu.SMEM),
            pl.BlockSpec(memory_space=pl.ANY),
            pl.BlockSpec((TM_OUT, d), lambda i: (i, 0)),
            pl.BlockSpec((TM_OUT, ROUTE_ROWS), lambda i: (i, 0)),
            pl.BlockSpec((1, d), lambda i: (0, 0)),
            pl.BlockSpec((1, d), lambda i: (0, 0)),
        ],
        out_specs=pl.BlockSpec((TM_OUT, d), lambda i: (i, 0)),
        out_shape=jax.ShapeDtypeStruct((t, d), F32),
        scratch_shapes=[pltpu.VMEM((2, 2, TM_OUT, w), U32),
                        pltpu.SemaphoreType.DMA((2,))],
        compiler_params=_cparams(("arbitrary",)),
        name="combine",
    )(pos3, pos3, ys, h1, wtok, g2, b2)


def _lambda_init(layer):
    return 0.8 - 0.6 * math.exp(-0.3 * layer)


def kernel(x, w_in, w_out, ln1_g, ln1_b, ln2_g, ln2_b, rel_bias, lam_q1, lam_k1, lam_q2, lam_k2,
           subln_g, sg_ln_g, sg_ln_b, sg_w, sg_b, w_router_group, b_router_group,
           w_router_expert, b_router_expert, w_exp_gate, w_exp_up, w_exp_down):
    b, s, d = x.shape
    t = b * s
    assert w_in.shape[0] == DEPTH == 1
    l = 0
    lam_init = _lambda_init(l)
    x2 = x.reshape(t, d)

    proj = _inproj(x2, w_in[l])

    lam = (jnp.exp(jnp.sum(lam_q1[l].astype(F32) * lam_k1[l].astype(F32)))
           - jnp.exp(jnp.sum(lam_q2[l].astype(F32) * lam_k2[l].astype(F32))) + lam_init)
    side = [w_exp_gate[l].reshape(-1, D_EXPERT), w_exp_up[l].reshape(-1, D_EXPERT),
            w_exp_down[l].reshape(-1, d)]
    a, wg_b, wu_b, wd_b = _attention(
        proj.reshape(b, s, -1), _bias_sequences(rel_bias, s), lam.reshape(1),
        subln_g[l].reshape(1, -1), lam_init, side)

    wr = jnp.concatenate([w_router_group[l].T,
                          jnp.zeros((NR_EXPERT0 - N_GROUPS, d), F32),
                          jnp.transpose(w_router_expert[l], (0, 2, 1)).reshape(N_EXPERTS, d)],
                         axis=0)
    br = jnp.concatenate([b_router_group[l], jnp.zeros((NR_EXPERT0 - N_GROUPS,), F32),
                          b_router_expert[l].reshape(-1)]).reshape(NR_PAD, 1)
    wr_hi = wr.astype(BF16)
    wr = jnp.concatenate([wr_hi, (wr - wr_hi.astype(F32)).astype(BF16)], axis=0)
    h1, xs, ri, rw, cnt = _mix(
        a.reshape(t, -1), proj, x2, w_out[l].astype(BF16), sg_w[l].astype(BF16),
        sg_b[l].reshape(SG_GROUPS, SG_CHUNK, 1), sg_ln_g[l], sg_ln_b[l],
        ln1_g[l].reshape(1, d), ln1_b[l].reshape(1, d), wr, br)

    counts = cnt[:, 0]
    tiles_e = (counts + TMX - 1) // TMX
    tile_end = jnp.cumsum(tiles_e)
    tile_start = tile_end - tiles_e
    n_used = tile_end[-1:]
    max_tiles = (2 * t) // TMX + N_EXPERTS
    row_start = (tile_start * TMX).astype(I32)
    eids = jnp.arange(N_EXPERTS, dtype=I32)[:, None, None]
    pos = jnp.sum(jnp.where(ri[0:2][None] == eids, row_start[:, None, None], 0), axis=0) + ri[2:4]
    pos3 = pos.reshape(2, t // TM_OUT, TM_OUT).transpose(1, 0, 2)
    tile_ids = jnp.arange(max_tiles, dtype=I32)
    tile_e = jnp.minimum(jnp.searchsorted(tile_end, tile_ids, side="right"), N_EXPERTS - 1)
    local = tile_ids - tile_start[tile_e]
    in_use = tile_ids < n_used[0]
    tile_row = jnp.where(in_use, tile_e * XS_CAP + local * TMX, 0).astype(I32)
    tile_valid = jnp.where(in_use, jnp.clip(counts[tile_e] - local * TMX, 0, TMX), 0).astype(I32)

    ys = _experts(tile_start.astype(I32), tiles_e.astype(I32), n_used.astype(I32),
                  tile_row, tile_valid, xs,
                  wg_b.reshape(N_EXPERTS, d, D_EXPERT), wu_b.reshape(N_EXPERTS, d, D_EXPERT),
                  wd_b.reshape(N_EXPERTS, D_EXPERT, d))
    out = _combine(pos3, ys, h1, rw.T, ln2_g[l].reshape(1, d), ln2_b[l].reshape(1, d))
    return out.reshape(b, s, d)
```

```python
import functools
import math

import jax
import jax.numpy as jnp
from jax import lax
from jax.experimental import pallas as pl
from jax.experimental.pallas import tpu as pltpu

F32 = jnp.float32
BF16 = jnp.bfloat16
U32 = jnp.uint32
I32 = jnp.int32

ATT_HEADS = 8
ATT_QK_DIM = 64
ATT_V_DIM = 128
ATT_WIDTH = ATT_HEADS * ATT_V_DIM
SG_GROUPS = 8
SG_DIM = 128
SG_WIDTH = SG_GROUPS * SG_DIM
SG_CHUNK = 128
REL_BUCKETS = 32
REL_MAX_DIST = 128
N_GROUPS = 4
EXPERTS_PER_GROUP = 8
N_EXPERTS = N_GROUPS * EXPERTS_PER_GROUP
D_EXPERT = 256
DEPTH = 1
DN_ALPHA = (2.0 * DEPTH) ** 0.25
LN_EPS = 1e-5
LOG2E = math.log2(math.e)

LANES = 128
SUBLANES = 8
VMEM_LIMIT_BYTES = 56 * 1024 * 1024

TM_IN = 1024
TN_IN = 1024
QB = 256
KC = 256
N_BIAS_TILES = 5
TM_MIX = 512
TMX = 512
TM_OUT = 256
ROUTE_ROWS = 8
XS_CAP = 8192 + TMX
NR_EXPERT0 = 8
NR_PAD = NR_EXPERT0 + 32


def _cparams(sem, flags=None):
    return pltpu.CompilerParams(dimension_semantics=sem, vmem_limit_bytes=VMEM_LIMIT_BYTES,
                                flags=flags)


def _inproj_kernel(x_ref, w_ref, o_ref, xb_ref, *, n_q, n_lin, q_scale):
    j = pl.program_id(1)

    @pl.when(j == 0)
    def _():
        xb_ref[...] = x_ref[...].astype(BF16)

    acc = jnp.dot(xb_ref[...], w_ref[...].astype(BF16), preferred_element_type=F32)

    @pl.when(j < n_q)
    def _():
        o_ref[...] = (acc * q_scale).astype(BF16)

    @pl.when(jnp.logical_and(j >= n_q, j < n_lin))
    def _():
        o_ref[...] = acc.astype(BF16)

    @pl.when(j >= n_lin)
    def _():
        o_ref[...] = jax.nn.gelu(acc).astype(BF16)


def _inproj(x2, w_in):
    t, d = x2.shape
    n = w_in.shape[1]
    n_q = (ATT_HEADS * 2 * ATT_QK_DIM) // TN_IN
    n_lin = (2 * ATT_HEADS * 2 * ATT_QK_DIM + ATT_WIDTH) // TN_IN
    kern = functools.partial(_inproj_kernel, n_q=n_q, n_lin=n_lin,
                             q_scale=(ATT_QK_DIM ** -0.5) * LOG2E)
    return pl.pallas_call(
        kern,
        grid=(t // TM_IN, n // TN_IN),
        in_specs=[pl.BlockSpec((TM_IN, d), lambda i, j: (i, 0)),
                  pl.BlockSpec((d, TN_IN), lambda i, j: (0, j))],
        out_specs=pl.BlockSpec((TM_IN, TN_IN), lambda i, j: (i, j)),
        out_shape=jax.ShapeDtypeStruct((t, n), BF16),
        scratch_shapes=[pltpu.VMEM((TM_IN, d), BF16)],
        compiler_params=_cparams(("arbitrary", "arbitrary")),
        name="inproj",
    )(x2, w_in)


def _attn_kernel(lam_ref, q_ref, k_ref, v_ref, bseq_ref, g_ref, *rest, n_kc, out_scale):
    n_side = (len(rest) - 3) // 2
    side_in, o_ref, side_out = rest[:n_side], rest[n_side], rest[n_side + 1:2 * n_side + 1]
    vaug_ref, bias_ref = rest[2 * n_side + 1:]
    qi = pl.program_id(2)

    @pl.when(qi == 0)
    def _():
        vaug_ref[:, :ATT_V_DIM] = v_ref[0]
        vaug_ref[:, ATT_V_DIM:] = jnp.ones((vaug_ref.shape[0], ATT_V_DIM), BF16)
        for t in range(N_BIAS_TILES):
            seq = jnp.broadcast_to(bseq_ref[0, t:t + 1, :], (QB, 2 * KC))
            bias_ref[t] = pltpu.roll(seq, 0, 1, stride=1, stride_axis=0)[:, :KC]

    q = q_ref[0]
    lane = lax.broadcasted_iota(I32, q.shape, 1)
    zero = jnp.zeros_like(q)
    qq = jnp.concatenate([jnp.where(lane < ATT_QK_DIM, q, zero),
                          jnp.where(lane >= ATT_QK_DIM, q, zero)], axis=0)

    for src, dst in zip(side_in, side_out):
        dst[...] = src[...].astype(dst.dtype)

    m = jnp.full((2 * QB, 1), -jnp.inf, F32)
    acc = jnp.zeros((2 * QB, 2 * ATT_V_DIM), F32)
    for kc in range(n_kc):
        kblk = k_ref[0, kc * KC:(kc + 1) * KC, :]
        s = lax.dot_general(qq, kblk, (((1,), (1,)), ((), ())),
                            preferred_element_type=F32)
        b = bias_ref[jnp.clip(kc - qi, -2, 2) + 2]
        s = s + jnp.concatenate([b, b], axis=0)
        m_new = jnp.maximum(m, jnp.max(s, axis=1, keepdims=True))
        p = jnp.exp2(s - m_new).astype(BF16)
        acc = acc * jnp.exp2(m - m_new) + jnp.dot(
            p, vaug_ref[kc * KC:(kc + 1) * KC, :], preferred_element_type=F32)
        m = m_new
    lam = lam_ref[0]
    o0 = acc[:QB, :ATT_V_DIM] / acc[:QB, ATT_V_DIM:ATT_V_DIM + 1]
    o1 = acc[QB:, :ATT_V_DIM] / acc[QB:, ATT_V_DIM:ATT_V_DIM + 1]
    of = o0 - lam * o1
    of = of * lax.rsqrt(jnp.mean(of * of, axis=-1, keepdims=True) + LN_EPS)
    o_ref[0] = (of * (g_ref[...] * out_scale)).astype(o_ref.dtype)


def _attention(proj3, bias_seq, lam, subln_g, lam_init, side):
    b, s, _ = proj3.shape
    n_kc = s // KC
    n_q = s // QB
    n_steps = b * ATT_HEADS * n_q
    h_off_k = ATT_HEADS
    h_off_v = 2 * ATT_HEADS
    kern = functools.partial(_attn_kernel, n_kc=n_kc, out_scale=1.0 - lam_init)

    def side_map(bi, h, qi):
        return ((bi * ATT_HEADS + h) * n_q + qi, 0)

    side_specs = []
    for arr in side:
        rows, cols = arr.shape
        assert rows % (n_steps * 2 * SUBLANES) == 0
        side_specs.append(pl.BlockSpec((rows // n_steps, cols), side_map))
    return pl.pallas_call(
        kern,
        grid=(b, ATT_HEADS, n_q),
        in_specs=[
            pl.BlockSpec(memory_space=pltpu.SMEM),
            pl.BlockSpec((1, QB, LANES), lambda bi, h, qi: (bi, qi, h)),
            pl.BlockSpec((1, s, LANES), lambda bi, h, qi: (bi, 0, h_off_k + h)),
            pl.BlockSpec((1, s, LANES), lambda bi, h, qi: (bi, 0, h_off_v + h)),
            pl.BlockSpec((1, N_BIAS_TILES, 2 * KC), lambda bi, h, qi: (h, 0, 0)),
            pl.BlockSpec((1, ATT_V_DIM), lambda bi, h, qi: (0, 0)),
        ] + side_specs,
        out_specs=[pl.BlockSpec((1, QB, ATT_V_DIM), lambda bi, h, qi: (bi, qi, h))] + side_specs,
        out_shape=[jax.ShapeDtypeStruct((b, s, ATT_WIDTH), BF16)]
        + [jax.ShapeDtypeStruct(arr.shape, BF16) for arr in side],
        scratch_shapes=[pltpu.VMEM((s, 2 * ATT_V_DIM), BF16),
                        pltpu.VMEM((N_BIAS_TILES, QB, KC), F32)],
        compiler_params=_cparams(("arbitrary", "arbitrary", "arbitrary")),
        name="attn",
    )(lam, proj3, proj3, proj3, bias_seq, subln_g, *side)


def _rel_bucket(rel):
    half = REL_BUCKETS // 2
    max_exact = half // 2
    ret = jnp.where(rel > 0, half, 0)
    n = jnp.abs(rel)
    nf = jnp.maximum(n, 1).astype(F32)
    large = max_exact + (jnp.log(nf / max_exact) / math.log(REL_MAX_DIST / max_exact)
                         * (half - max_exact)).astype(I32)
    large = jnp.minimum(large, half - 1)
    return ret + jnp.where(n < max_exact, n, large)


def _bias_sequences(rel_bias, s):
    assert REL_MAX_DIST <= KC and QB == KC
    period = 2 * KC
    j = jnp.arange(period, dtype=I32)
    off = jnp.where(j < KC, j, j - period)
    d = jnp.arange(-2, 3, dtype=I32)[:, None]
    rel = jnp.clip(d * KC + off[None, :], -(s - 1), s - 1)
    return jnp.transpose(rel_bias[_rel_bucket(rel)], (2, 0, 1)).astype(F32) * LOG2E


def _layernorm(v, g, b):
    mu = jnp.mean(v, axis=-1, keepdims=True)
    var = jnp.mean(jnp.square(v - mu), axis=-1, keepdims=True)
    return (v - mu) * lax.rsqrt(var + LN_EPS) * g + b


def _pack_bf16_pair(v):
    n = v.shape[1] // 2
    bits = lax.bitcast_convert_type(v.astype(BF16).astype(F32), U32)
    return (bits[:, :n] >> 16) | (bits[:, n:] & jnp.uint32(0xFFFF0000))


def _unpack_bf16_pair(w):
    lo = lax.bitcast_convert_type(w << 16, F32)
    hi = lax.bitcast_convert_type(w & jnp.uint32(0xFFFF0000), F32)
    return lo, hi


def _first_argmax(vals, nrows):
    mx = jnp.max(vals, axis=0, keepdims=True)
    row = lax.broadcasted_iota(I32, vals.shape, 0)
    idx = jnp.min(jnp.where(vals == mx, row, nrows), axis=0, keepdims=True)
    return mx, idx


def _mix_kernel(a_ref, u_ref, vg_ref, x_ref, wout_ref, sgw_ref, sgb_ref, sgg_ref, sgbeta_ref,
                g1_ref, b1_ref, wr_ref, br_ref,
                h1_ref, xs_hbm, ri_ref, rw_ref, cnt_ref,
                run_ref, hp_s, posv_ref, psm, ssem, rsem):
    i = pl.program_id(0)
    n = pl.num_programs(0)
    slot = i % 2
    prev = 1 - slot

    def pos_copy(sl):
        return pltpu.make_async_copy(posv_ref, psm.at[sl], ssem.at[sl])

    def row_copy(sl, which, j):
        return pltpu.make_async_copy(hp_s.at[sl, pl.ds(j, 1)],
                                     xs_hbm.at[pl.ds(psm[sl, which, j], 1)], rsem.at[sl])

    def drain(sl):
        def wait(j, c):
            row_copy(sl, 0, 0).wait()
            row_copy(sl, 1, 0).wait()
            return c
        lax.fori_loop(0, TM_MIX, wait, 0, unroll=8)

    @pl.when(i == 0)
    def _():
        run_ref[...] = jnp.zeros_like(run_ref)
        hp_s[1] = jnp.zeros(hp_s.shape[1:], hp_s.dtype)
        spare = N_EXPERTS * XS_CAP + lax.broadcasted_iota(I32, posv_ref.shape, 1)
        spare = spare + TM_MIX * lax.broadcasted_iota(I32, posv_ref.shape, 0)
        posv_ref[...] = jnp.where(spare < xs_hbm.shape[0], spare, 0)
        pos_copy(1).start()

    @pl.when(i > 0)
    def _():
        drain(slot)

    pos_copy(prev).wait()

    for j in range(TM_MIX):
        row_copy(prev, 0, j).start(priority=0)
        row_copy(prev, 1, j).start(priority=1)

    h1_ref[...] = jnp.dot(a_ref[...], wout_ref[:ATT_WIDTH, :], preferred_element_type=F32)

    n_chunks = TM_MIX // SG_CHUNK
    rows = []
    for c in range(n_chunks):
        cols = []
        for g in range(SG_GROUPS):
            rs = slice(c * SG_CHUNK, (c + 1) * SG_CHUNK)
            cs = slice(g * SG_DIM, (g + 1) * SG_DIM)
            vt = vg_ref[rs, cs].astype(F32)
            vn = _layernorm(vt, sgg_ref[g:g + 1, :], sgbeta_ref[g:g + 1, :])
            mixed = jnp.dot(sgw_ref[g], vn.astype(BF16), preferred_element_type=F32) + sgb_ref[g]
            cols.append((u_ref[rs, cs].astype(F32) * mixed).astype(BF16))
        rows.append(jnp.concatenate(cols, axis=1))
    sgate = jnp.concatenate(rows, axis=0)

    mix = h1_ref[...] + jnp.dot(sgate, wout_ref[ATT_WIDTH:, :], preferred_element_type=F32)
    h1 = _layernorm(DN_ALPHA * x_ref[...] + mix, g1_ref[...], b1_ref[...])
    h1_ref[...] = h1
    hp_s[slot] = _pack_bf16_pair(h1)

    nt = (((1,), (1,)), ((), ()))
    h1_hi = h1.astype(BF16)
    h1_lo = (h1 - h1_hi.astype(F32)).astype(BF16)
    both = lax.dot_general(wr_ref[...], h1_hi, nt, preferred_element_type=F32)
    lt = both[:NR_PAD] + both[NR_PAD:] + lax.dot_general(
        wr_ref[:NR_PAD, :], h1_lo, nt, preferred_element_type=F32)
    lt = lt + br_ref[...]
    g_logits = lt[0:N_GROUPS]
    gmax, g_idx = _first_argmax(g_logits, N_GROUPS)
    g_gate = 1.0 / jnp.sum(jnp.exp(g_logits - gmax), axis=0, keepdims=True)
    e_logits = jnp.zeros((EXPERTS_PER_GROUP, TM_MIX), F32)
    for g in range(N_GROUPS):
        lo = NR_EXPERT0 + g * EXPERTS_PER_GROUP
        e_logits = jnp.where(g_idx == g, lt[lo:lo + EXPERTS_PER_GROUP], e_logits)
    v1, i1 = _first_argmax(e_logits, EXPERTS_PER_GROUP)
    erow = lax.broadcasted_iota(I32, e_logits.shape, 0)
    v2, i2 = _first_argmax(jnp.where(erow == i1, -jnp.inf, e_logits), EXPERTS_PER_GROUP)
    t = jnp.exp(v2 - v1)
    w1 = g_gate / (1.0 + t)
    w2 = g_gate * t / (1.0 + t)
    e1 = g_idx * EXPERTS_PER_GROUP + i1
    e2 = g_idx * EXPERTS_PER_GROUP + i2

    xrow = lax.broadcasted_iota(I32, (N_EXPERTS, TM_MIX), 0)
    oh1 = xrow == e1
    oh2 = xrow == e2
    oh = jnp.where(jnp.logical_or(oh1, oh2), 1.0, 0.0)
    tr = lax.broadcasted_iota(I32, (TM_MIX, TM_MIX), 0)
    tc = lax.broadcasted_iota(I32, (TM_MIX, TM_MIX), 1)
    before = jnp.where(tr < tc, 1.0, 0.0).astype(BF16)
    cum = jnp.dot(oh.astype(BF16), before, preferred_element_type=F32)
    base = cum + run_ref[:, 0:1]
    rank1 = jnp.sum(jnp.where(oh1, base, 0.0), axis=0, keepdims=True).astype(I32)
    rank2 = jnp.sum(jnp.where(oh2, base, 0.0), axis=0, keepdims=True).astype(I32)
    run_ref[...] = run_ref[...] + jnp.sum(oh, axis=1, keepdims=True)
    cnt_ref[...] = run_ref[...].astype(I32)

    zi = jnp.zeros((ROUTE_ROWS - 4, TM_MIX), I32)
    ri_ref[...] = jnp.concatenate([e1, e2, rank1, rank2, zi], axis=0)
    zw = jnp.zeros((ROUTE_ROWS - 2, TM_MIX), F32)
    rw_ref[...] = jnp.concatenate([w1, w2, zw], axis=0)

    posv_ref[...] = jnp.concatenate([e1 * XS_CAP + rank1, e2 * XS_CAP + rank2, zi, zi[:2]], axis=0)
    pos_copy(slot).start()

    @pl.when(i == n - 1)
    def _():
        pos_copy(slot).wait()

        def last(j, c):
            row_copy(slot, 0, j).start(priority=0)
            row_copy(slot, 1, j).start(priority=1)
            return c
        lax.fori_loop(0, TM_MIX, last, 0, unroll=8)
        drain(prev)
        drain(slot)


def _mix(a2, proj2, x2, wout_b, sgw_b, sgb, sgg, sgbeta, g1, b1, wr, br):
    t, d = x2.shape
    assert XS_CAP >= t + TMX
    u_blk = (proj2.shape[1] - 2 * SG_WIDTH) // SG_WIDTH
    const = lambda *shape: pl.BlockSpec(shape, lambda i: (0,) * len(shape))
    return pl.pallas_call(
        _mix_kernel,
        grid=(t // TM_MIX,),
        in_specs=[
            pl.BlockSpec((TM_MIX, ATT_WIDTH), lambda i: (i, 0)),
            pl.BlockSpec((TM_MIX, SG_WIDTH), lambda i: (i, u_blk)),
            pl.BlockSpec((TM_MIX, SG_WIDTH), lambda i: (i, u_blk + 1)),
            pl.BlockSpec((TM_MIX, d), lambda i: (i, 0)),
            pl.BlockSpec((d, d), lambda i: (0, 0), pipeline_mode=pl.Buffered(1)),
            const(SG_GROUPS, SG_CHUNK, SG_CHUNK),
            const(SG_GROUPS, SG_CHUNK, 1),
            const(SG_GROUPS, SG_DIM),
            const(SG_GROUPS, SG_DIM),
            const(1, d),
            const(1, d),
            const(2 * NR_PAD, d),
            const(NR_PAD, 1),
        ],
        out_specs=[
            pl.BlockSpec((TM_MIX, d), lambda i: (i, 0)),
            pl.BlockSpec(memory_space=pl.ANY),
            pl.BlockSpec((ROUTE_ROWS, TM_MIX), lambda i: (0, i)),
            pl.BlockSpec((ROUTE_ROWS, TM_MIX), lambda i: (0, i)),
            const(N_EXPERTS, LANES),
        ],
        out_shape=[
            jax.ShapeDtypeStruct((t, d), F32),
            jax.ShapeDtypeStruct((N_EXPERTS * XS_CAP + 2 * TM_MIX, d // 2), U32),
            jax.ShapeDtypeStruct((ROUTE_ROWS, t), I32),
            jax.ShapeDtypeStruct((ROUTE_ROWS, t), F32),
            jax.ShapeDtypeStruct((N_EXPERTS, LANES), I32),
        ],
        scratch_shapes=[pltpu.VMEM((N_EXPERTS, LANES), F32),
                        pltpu.VMEM((2, TM_MIX, d // 2), U32),
                        pltpu.VMEM((ROUTE_ROWS, TM_MIX), I32),
                        pltpu.SMEM((2, ROUTE_ROWS, TM_MIX), I32),
                        pltpu.SemaphoreType.DMA((2,)),
                        pltpu.SemaphoreType.DMA((2,))],
        compiler_params=_cparams(("arbitrary",)),
        name="mix",
    )(a2, proj2, proj2, x2, wout_b, sgw_b, sgb, sgg, sgbeta, g1, b1, wr, br)


def _experts_kernel(ts_ref, nt_ref, nu_ref, trow_ref, nval_ref, x_hbm, wg_ref, wu_ref, wd_ref,
                    y_hbm, xbuf, ybuf, xsem, ysem):
    e = pl.program_id(0)
    d = wg_ref.shape[1]
    n_used = nu_ref[0]

    def x_copy(g, slot):
        rows = pl.ds(pl.multiple_of(trow_ref[g], TMX), TMX)
        return pltpu.make_async_copy(x_hbm.at[rows], xbuf.at[slot], xsem.at[slot])

    def y_copy(g, slot):
        rows = pl.ds(pl.multiple_of(g * TMX, TMX), TMX)
        return pltpu.make_async_copy(ybuf.at[slot], y_hbm.at[rows], ysem.at[slot])

    @pl.when(jnp.logical_and(e == 0, n_used > 0))
    def _():
        x_copy(0, 0).start(priority=1)

    def tile(t, carry):
        g = ts_ref[e] + t
        slot = g % 2
        x_copy(g, slot).wait()

        @pl.when(g + 1 < n_used)
        def _():
            x_copy(g + 1, 1 - slot).start(priority=1)

        @pl.when(g >= 2)
        def _():
            y_copy(g - 2, slot).wait()

        row = lax.broadcasted_iota(I32, xbuf.shape[1:], 0)
        xw = jnp.where(row < nval_ref[g], xbuf[slot], jnp.uint32(0))
        lo, hi = _unpack_bf16_pair(xw)
        lo, hi = lo.astype(BF16), hi.astype(BF16)
        gate = jnp.dot(lo, wg_ref[0, :d // 2, :], preferred_element_type=F32)
        gate = gate + jnp.dot(hi, wg_ref[0, d // 2:, :], preferred_element_type=F32)
        up = jnp.dot(lo, wu_ref[0, :d // 2, :], preferred_element_type=F32)
        up = up + jnp.dot(hi, wu_ref[0, d // 2:, :], preferred_element_type=F32)
        hmid = (gate * jax.nn.sigmoid(gate) * up).astype(BF16)
        y = jnp.dot(hmid, wd_ref[0], preferred_element_type=F32)
        ybuf[slot] = _pack_bf16_pair(y)
        y_copy(g, slot).start(priority=1)
        return carry

    lax.fori_loop(0, nt_ref[e], tile, 0)

    @pl.when(e == pl.num_programs(0) - 1)
    def _():
        @pl.when(n_used >= 2)
        def _():
            y_copy(n_used - 2, (n_used - 2) % 2).wait()

        @pl.when(n_used >= 1)
        def _():
            y_copy(n_used - 1, (n_used - 1) % 2).wait()


def _experts(tile_start, n_tiles_e, n_used, tile_row, tile_valid, xs, w_gate, w_up, w_down):
    w = xs.shape[1]
    d = w_gate.shape[1]
    n_rows = tile_row.shape[0] * TMX

    def w_map(e, ts, nt, nu, trow, nval):
        return (e, 0, 0)

    grid_spec = pltpu.PrefetchScalarGridSpec(
        num_scalar_prefetch=5,
        grid=(N_EXPERTS,),
        in_specs=[
            pl.BlockSpec(memory_space=pl.ANY),
            pl.BlockSpec((1, d, D_EXPERT), w_map),
            pl.BlockSpec((1, d, D_EXPERT), w_map),
            pl.BlockSpec((1, D_EXPERT, d), w_map),
        ],
        out_specs=pl.BlockSpec(memory_space=pl.ANY),
        scratch_shapes=[pltpu.VMEM((2, TMX, w), U32),
                        pltpu.VMEM((2, TMX, w), U32),
                        pltpu.SemaphoreType.DMA((2,)),
                        pltpu.SemaphoreType.DMA((2,))],
    )
    return pl.pallas_call(
        _experts_kernel,
        grid_spec=grid_spec,
        out_shape=jax.ShapeDtypeStruct((n_rows, w), U32),
        compiler_params=_cparams(("arbitrary",)),
        name="experts",
    )(tile_start, n_tiles_e, n_used, tile_row, tile_valid, xs, w_gate, w_up, w_down)


def _combine_kernel(pos_ref, posn_ref, y_hbm, h1_ref, wt_ref, g2_ref, b2_ref, o_ref, ybuf, sem):
    i = pl.program_id(0)
    n = pl.num_programs(0)
    slot = i % 2

    def row_copy(src, sl, which, j):
        return pltpu.make_async_copy(y_hbm.at[pl.ds(src, 1)],
                                     ybuf.at[sl, which, pl.ds(j, 1)], sem.at[sl])

    def issue(p_ref, sl):
        def body(j, c):
            row_copy(p_ref[0, j], sl, 0, j).start(priority=0)
            row_copy(p_ref[1, j], sl, 1, j).start(priority=1)
            return c
        lax.fori_loop(0, TM_OUT, body, 0, unroll=8)

    def drain(sl):
        def wait(j, c):
            row_copy(0, sl, 0, 0).wait()
            row_copy(0, sl, 1, 0).wait()
            return c
        lax.fori_loop(0, TM_OUT, wait, 0, unroll=8)

    @pl.when(i == 0)
    def _():
        issue(pos_ref, 0)

    drain(slot)

    for j in range(TM_OUT):
        row_copy(posn_ref[0, j], 1 - slot, 0, j).start(priority=0)
        row_copy(posn_ref[1, j], 1 - slot, 1, j).start(priority=1)

    lo1, hi1 = _unpack_bf16_pair(ybuf[slot, 0])
    lo2, hi2 = _unpack_bf16_pair(ybuf[slot, 1])
    w1 = wt_ref[:, 0:1]
    w2 = wt_ref[:, 1:2]
    ffn = jnp.concatenate([w1 * lo1 + w2 * lo2, w1 * hi1 + w2 * hi2], axis=1)
    o_ref[...] = _layernorm(DN_ALPHA * h1_ref[...] + ffn, g2_ref[...], b2_ref[...])

    @pl.when(i == n - 1)
    def _():
        drain(1 - slot)


def _combine(pos3, ys, h1, wtok, g2, b2):
    t, d = h1.shape
    w = ys.shape[1]
    n = t // TM_OUT
    return pl.pallas_call(
        _combine_kernel,
        grid=(n,),
        in_specs=[
            pl.BlockSpec((None, 2, TM_OUT), lambda i: (i, 0, 0), memory_space=pltpu.SMEM),
            pl.BlockSpec((None, 2, TM_OUT), lambda i: (jnp.minimum(i + 1, n - 1), 0, 0),
                         memory_space=pltpu.SMEM),
            pl.BlockSpec(memory_space=pl.ANY),
            pl.BlockSpec((TM_OUT, d), lambda i: (i, 0)),
            pl.BlockSpec((TM_OUT, ROUTE_ROWS), lambda i: (i, 0)),
            pl.BlockSpec((1, d), lambda i: (0, 0)),
            pl.BlockSpec((1, d), lambda i: (0, 0)),
        ],
        out_specs=pl.BlockSpec((TM_OUT, d), lambda i: (i, 0)),
        out_shape=jax.ShapeDtypeStruct((t, d), F32),
        scratch_shapes=[pltpu.VMEM((2, 2, TM_OUT, w), U32),
                        pltpu.SemaphoreType.DMA((2,))],
        compiler_params=_cparams(("arbitrary",)),
        name="combine",
    )(pos3, pos3, ys, h1, wtok, g2, b2)


def _lambda_init(layer):
    return 0.8 - 0.6 * math.exp(-0.3 * layer)


def kernel(x, w_in, w_out, ln1_g, ln1_b, ln2_g, ln2_b, rel_bias, lam_q1, lam_k1, lam_q2, lam_k2,
           subln_g, sg_ln_g, sg_ln_b, sg_w, sg_b, w_router_group, b_router_group,
           w_router_expert, b_router_expert, w_exp_gate, w_exp_up, w_exp_down):
    b, s, d = x.shape
    t = b * s
    assert w_in.shape[0] == DEPTH == 1
    l = 0
    lam_init = _lambda_init(l)
    x2 = x.reshape(t, d)

    proj = _inproj(x2, w_in[l])

    lam = (jnp.exp(jnp.sum(lam_q1[l].astype(F32) * lam_k1[l].astype(F32)))
           - jnp.exp(jnp.sum(lam_q2[l].astype(F32) * lam_k2[l].astype(F32))) + lam_init)
    side = [w_exp_gate[l].reshape(-1, D_EXPERT), w_exp_up[l].reshape(-1, D_EXPERT),
            w_exp_down[l].reshape(-1, d)]
    a, wg_b, wu_b, wd_b = _attention(
        proj.reshape(b, s, -1), _bias_sequences(rel_bias, s), lam.reshape(1),
        subln_g[l].reshape(1, -1), lam_init, side)

    wr = jnp.concatenate([w_router_group[l].T,
                          jnp.zeros((NR_EXPERT0 - N_GROUPS, d), F32),
                          jnp.transpose(w_router_expert[l], (0, 2, 1)).reshape(N_EXPERTS, d)],
                         axis=0)
    br = jnp.concatenate([b_router_group[l], jnp.zeros((NR_EXPERT0 - N_GROUPS,), F32),
                          b_router_expert[l].reshape(-1)]).reshape(NR_PAD, 1)
    wr_hi = wr.astype(BF16)
    wr = jnp.concatenate([wr_hi, (wr - wr_hi.astype(F32)).astype(BF16)], axis=0)
    h1, xs, ri, rw, cnt = _mix(
        a.reshape(t, -1), proj, x2, w_out[l].astype(BF16), sg_w[l].astype(BF16),
        sg_b[l].reshape(SG_GROUPS, SG_CHUNK, 1), sg_ln_g[l], sg_ln_b[l],
        ln1_g[l].reshape(1, d), ln1_b[l].reshape(1, d), wr, br)

    counts = cnt[:, 0]
    tiles_e = (counts + TMX - 1) // TMX
    tile_end = jnp.cumsum(tiles_e)
    tile_start = tile_end - tiles_e
    n_used = tile_end[-1:]
    max_tiles = (2 * t) // TMX + N_EXPERTS
    row_start = (tile_start * TMX).astype(I32)
    eids = jnp.arange(N_EXPERTS, dtype=I32)[:, None, None]
    pos = jnp.sum(jnp.where(ri[0:2][None] == eids, row_start[:, None, None], 0), axis=0) + ri[2:4]
    pos3 = pos.reshape(2, t // TM_OUT, TM_OUT).transpose(1, 0, 2)
    tile_ids = jnp.arange(max_tiles, dtype=I32)
    tile_e = jnp.minimum(jnp.searchsorted(tile_end, tile_ids, side="right"), N_EXPERTS - 1)
    local = tile_ids - tile_start[tile_e]
    in_use = tile_ids < n_used[0]
    tile_row = jnp.where(in_use, tile_e * XS_CAP + local * TMX, 0).astype(I32)
    tile_valid = jnp.where(in_use, jnp.clip(counts[tile_e] - local * TMX, 0, TMX), 0).astype(I32)

    ys = _experts(tile_start.astype(I32), tiles_e.astype(I32), n_used.astype(I32),
                  tile_row, tile_valid, xs,
                  wg_b.reshape(N_EXPERTS, d, D_EXPERT), wu_b.reshape(N_EXPERTS, d, D_EXPERT),
                  wd_b.reshape(N_EXPERTS, D_EXPERT, d))
    out = _combine(pos3, ys, h1, rw.T, ln2_g[l].reshape(1, d), ln2_b[l].reshape(1, d))
    return out.reshape(b, s, d)
```

```python
import functools
import math

import jax
import jax.numpy as jnp
from jax import lax
from jax.experimental import pallas as pl
from jax.experimental.pallas import tpu as pltpu

F32 = jnp.float32
BF16 = jnp.bfloat16
U32 = jnp.uint32
I32 = jnp.int32

ATT_HEADS = 8
ATT_QK_DIM = 64
ATT_V_DIM = 128
ATT_WIDTH = ATT_HEADS * ATT_V_DIM
SG_GROUPS = 8
SG_DIM = 128
SG_WIDTH = SG_GROUPS * SG_DIM
SG_CHUNK = 128
REL_BUCKETS = 32
REL_MAX_DIST = 128
N_GROUPS = 4
EXPERTS_PER_GROUP = 8
N_EXPERTS = N_GROUPS * EXPERTS_PER_GROUP
D_EXPERT = 256
DEPTH = 1
DN_ALPHA = (2.0 * DEPTH) ** 0.25
LN_EPS = 1e-5
LOG2E = math.log2(math.e)

LANES = 128
SUBLANES = 8
VMEM_LIMIT_BYTES = 56 * 1024 * 1024

TM_IN = 1024
TN_IN = 1024
QB = 256
KC = 256
N_BIAS_TILES = 5
TM_MIX = 512
TMX = 512
TM_OUT = 256
ROUTE_ROWS = 8
XS_CAP = 8192 + TMX
NR_EXPERT0 = 8
NR_PAD = NR_EXPERT0 + 32


def _cparams(sem, flags=None):
    return pltpu.CompilerParams(dimension_semantics=sem, vmem_limit_bytes=VMEM_LIMIT_BYTES,
                                flags=flags)


def _inproj_kernel(x_ref, w_ref, o_ref, xb_ref, *, n_q, n_lin, q_scale):
    j = pl.program_id(1)

    @pl.when(j == 0)
    def _():
        xb_ref[...] = x_ref[...].astype(BF16)

    acc = jnp.dot(xb_ref[...], w_ref[...].astype(BF16), preferred_element_type=F32)

    @pl.when(j < n_q)
    def _():
        o_ref[...] = (acc * q_scale).astype(BF16)

    @pl.when(jnp.logical_and(j >= n_q, j < n_lin))
    def _():
        o_ref[...] = acc.astype(BF16)

    @pl.when(j >= n_lin)
    def _():
        o_ref[...] = jax.nn.gelu(acc).astype(BF16)


def _inproj(x2, w_in):
    t, d = x2.shape
    n = w_in.shape[1]
    n_q = (ATT_HEADS * 2 * ATT_QK_DIM) // TN_IN
    n_lin = (2 * ATT_HEADS * 2 * ATT_QK_DIM + ATT_WIDTH) // TN_IN
    kern = functools.partial(_inproj_kernel, n_q=n_q, n_lin=n_lin,
                             q_scale=(ATT_QK_DIM ** -0.5) * LOG2E)
    return pl.pallas_call(
        kern,
        grid=(t // TM_IN, n // TN_IN),
        in_specs=[pl.BlockSpec((TM_IN, d), lambda i, j: (i, 0)),
                  pl.BlockSpec((d, TN_IN), lambda i, j: (0, j))],
        out_specs=pl.BlockSpec((TM_IN, TN_IN), lambda i, j: (i, j)),
        out_shape=jax.ShapeDtypeStruct((t, n), BF16),
        scratch_shapes=[pltpu.VMEM((TM_IN, d), BF16)],
        compiler_params=_cparams(("arbitrary", "arbitrary")),
        name="inproj",
    )(x2, w_in)


def _attn_kernel(lam_ref, q_ref, k_ref, v_ref, bseq_ref, g_ref, *rest, n_kc, out_scale):
    n_side = (len(rest) - 3) // 2
    side_in, o_ref, side_out = rest[:n_side], rest[n_side], rest[n_side + 1:2 * n_side + 1]
    vaug_ref, bias_ref = rest[2 * n_side + 1:]
    qi = pl.program_id(2)

    @pl.when(qi == 0)
    def _():
        vaug_ref[:, :ATT_V_DIM] = v_ref[0]
        vaug_ref[:, ATT_V_DIM:] = jnp.ones((vaug_ref.shape[0], ATT_V_DIM), BF16)
        for t in range(N_BIAS_TILES):
            seq = jnp.broadcast_to(bseq_ref[0, t:t + 1, :], (QB, 2 * KC))
            bias_ref[t] = pltpu.roll(seq, 0, 1, stride=1, stride_axis=0)[:, :KC]

    q = q_ref[0]
    lane = lax.broadcasted_iota(I32, q.shape, 1)
    zero = jnp.zeros_like(q)
    qq = jnp.concatenate([jnp.where(lane < ATT_QK_DIM, q, zero),
                          jnp.where(lane >= ATT_QK_DIM, q, zero)], axis=0)

    for src, dst in zip(side_in, side_out):
        dst[...] = src[...].astype(dst.dtype)

    m = jnp.full((2 * QB, 1), -jnp.inf, F32)
    acc = jnp.zeros((2 * QB, 2 * ATT_V_DIM), F32)
    for kc in range(n_kc):
        kblk = k_ref[0, kc * KC:(kc + 1) * KC, :]
        s = lax.dot_general(qq, kblk, (((1,), (1,)), ((), ())),
                            preferred_element_type=F32)
        b = bias_ref[jnp.clip(kc - qi, -2, 2) + 2]
        s = s + jnp.concatenate([b, b], axis=0)
        m_new = jnp.maximum(m, jnp.max(s, axis=1, keepdims=True))
        p = jnp.exp2(s - m_new).astype(BF16)
        acc = acc * jnp.exp2(m - m_new) + jnp.dot(
            p, vaug_ref[kc * KC:(kc + 1) * KC, :], preferred_element_type=F32)
        m = m_new
    lam = lam_ref[0]
    o0 = acc[:QB, :ATT_V_DIM] / acc[:QB, ATT_V_DIM:ATT_V_DIM + 1]
    o1 = acc[QB:, :ATT_V_DIM] / acc[QB:, ATT_V_DIM:ATT_V_DIM + 1]
    of = o0 - lam * o1
    of = of * lax.rsqrt(jnp.mean(of * of, axis=-1, keepdims=True) + LN_EPS)
    o_ref[0] = (of * (g_ref[...] * out_scale)).astype(o_ref.dtype)


def _attention(proj3, bias_seq, lam, subln_g, lam_init, side):
    b, s, _ = proj3.shape
    n_kc = s // KC
    n_q = s // QB
    n_steps = b * ATT_HEADS * n_q
    h_off_k = ATT_HEADS
    h_off_v = 2 * ATT_HEADS
    kern = functools.partial(_attn_kernel, n_kc=n_kc, out_scale=1.0 - lam_init)

    def side_map(bi, h, qi):
        return ((bi * ATT_HEADS + h) * n_q + qi, 0)

    side_specs = []
    for arr in side:
        rows, cols = arr.shape
        assert rows % (n_steps * 2 * SUBLANES) == 0
        side_specs.append(pl.BlockSpec((rows // n_steps, cols), side_map))
    return pl.pallas_call(
        kern,
        grid=(b, ATT_HEADS, n_q),
        in_specs=[
            pl.BlockSpec(memory_space=pltpu.SMEM),
            pl.BlockSpec((1, QB, LANES), lambda bi, h, qi: (bi, qi, h)),
            pl.BlockSpec((1, s, LANES), lambda bi, h, qi: (bi, 0, h_off_k + h)),
            pl.BlockSpec((1, s, LANES), lambda bi, h, qi: (bi, 0, h_off_v + h)),
            pl.BlockSpec((1, N_BIAS_TILES, 2 * KC), lambda bi, h, qi: (h, 0, 0)),
            pl.BlockSpec((1, ATT_V_DIM), lambda bi, h, qi: (0, 0)),
        ] + side_specs,
        out_specs=[pl.BlockSpec((1, QB, ATT_V_DIM), lambda bi, h, qi: (bi, qi, h))] + side_specs,
        out_shape=[jax.ShapeDtypeStruct((b, s, ATT_WIDTH), BF16)]
        + [jax.ShapeDtypeStruct(arr.shape, BF16) for arr in side],
        scratch_shapes=[pltpu.VMEM((s, 2 * ATT_V_DIM), BF16),
                        pltpu.VMEM((N_BIAS_TILES, QB, KC), F32)],
        compiler_params=_cparams(("arbitrary", "arbitrary", "arbitrary")),
        name="attn",
    )(lam, proj3, proj3, proj3, bias_seq, subln_g, *side)


def _rel_bucket(rel):
    half = REL_BUCKETS // 2
    max_exact = half // 2
    ret = jnp.where(rel > 0, half, 0)
    n = jnp.abs(rel)
    nf = jnp.maximum(n, 1).astype(F32)
    large = max_exact + (jnp.log(nf / max_exact) / math.log(REL_MAX_DIST / max_exact)
                         * (half - max_exact)).astype(I32)
    large = jnp.minimum(large, half - 1)
    return ret + jnp.where(n < max_exact, n, large)


def _bias_sequences(rel_bias, s):
    assert REL_MAX_DIST <= KC and QB == KC
    period = 2 * KC
    j = jnp.arange(period, dtype=I32)
    off = jnp.where(j < KC, j, j - period)
    d = jnp.arange(-2, 3, dtype=I32)[:, None]
    rel = jnp.clip(d * KC + off[None, :], -(s - 1), s - 1)
    return jnp.transpose(rel_bias[_rel_bucket(rel)], (2, 0, 1)).astype(F32) * LOG2E


def _layernorm(v, g, b):
    mu = jnp.mean(v, axis=-1, keepdims=True)
    var = jnp.mean(jnp.square(v - mu), axis=-1, keepdims=True)
    return (v - mu) * lax.rsqrt(var + LN_EPS) * g + b


def _pack_bf16_pair(v):
    n = v.shape[1] // 2
    bits = lax.bitcast_convert_type(v.astype(BF16).astype(F32), U32)
    return (bits[:, :n] >> 16) | (bits[:, n:] & jnp.uint32(0xFFFF0000))


def _unpack_bf16_pair(w):
    lo = lax.bitcast_convert_type(w << 16, F32)
    hi = lax.bitcast_convert_type(w & jnp.uint32(0xFFFF0000), F32)
    return lo, hi


def _first_argmax(vals, nrows):
    mx = jnp.max(vals, axis=0, keepdims=True)
    row = lax.broadcasted_iota(I32, vals.shape, 0)
    idx = jnp.min(jnp.where(vals == mx, row, nrows), axis=0, keepdims=True)
    return mx, idx


def _mix_kernel(a_ref, u_ref, vg_ref, x_ref, wout_ref, sgw_ref, sgb_ref, sgg_ref, sgbeta_ref,
                g1_ref, b1_ref, wr_ref, br_ref,
                h1_ref, xs_hbm, ri_ref, rw_ref, cnt_ref,
                run_ref, hp_s, posv_ref, psm, ssem, rsem):
    i = pl.program_id(0)
    n = pl.num_programs(0)
    slot = i % 2
    prev = 1 - slot

    def pos_copy(sl):
        return pltpu.make_async_copy(posv_ref, psm.at[sl], ssem.at[sl])

    def row_copy(sl, which, j):
        return pltpu.make_async_copy(hp_s.at[sl, pl.ds(j, 1)],
                                     xs_hbm.at[pl.ds(psm[sl, which, j], 1)], rsem.at[sl])

    def drain(sl):
        def wait(j, c):
            row_copy(sl, 0, 0).wait()
            row_copy(sl, 1, 0).wait()
            return c
        lax.fori_loop(0, TM_MIX, wait, 0, unroll=8)

    @pl.when(i == 0)
    def _():
        run_ref[...] = jnp.zeros_like(run_ref)
        hp_s[1] = jnp.zeros(hp_s.shape[1:], hp_s.dtype)
        spare = N_EXPERTS * XS_CAP + lax.broadcasted_iota(I32, posv_ref.shape, 1)
        spare = spare + TM_MIX * lax.broadcasted_iota(I32, posv_ref.shape, 0)
        posv_ref[...] = jnp.where(spare < xs_hbm.shape[0], spare, 0)
        pos_copy(1).start()

    @pl.when(i > 0)
    def _():
        drain(slot)

    pos_copy(prev).wait()

    for j in range(TM_MIX):
        row_copy(prev, 0, j).start(priority=0)
        row_copy(prev, 1, j).start(priority=1)

    h1_ref[...] = jnp.dot(a_ref[...], wout_ref[:ATT_WIDTH, :], preferred_element_type=F32)

    n_chunks = TM_MIX // SG_CHUNK
    rows = []
    for c in range(n_chunks):
        cols = []
        for g in range(SG_GROUPS):
            rs = slice(c * SG_CHUNK, (c + 1) * SG_CHUNK)
            cs = slice(g * SG_DIM, (g + 1) * SG_DIM)
            vt = vg_ref[rs, cs].astype(F32)
            vn = _layernorm(vt, sgg_ref[g:g + 1, :], sgbeta_ref[g:g + 1, :])
            mixed = jnp.dot(sgw_ref[g], vn.astype(BF16), preferred_element_type=F32) + sgb_ref[g]
            cols.append((u_ref[rs, cs].astype(F32) * mixed).astype(BF16))
        rows.append(jnp.concatenate(cols, axis=1))
    sgate = jnp.concatenate(rows, axis=0)

    mix = h1_ref[...] + jnp.dot(sgate, wout_ref[ATT_WIDTH:, :], preferred_element_type=F32)
    h1 = _layernorm(DN_ALPHA * x_ref[...] + mix, g1_ref[...], b1_ref[...])
    h1_ref[...] = h1
    hp_s[slot] = _pack_bf16_pair(h1)

    nt = (((1,), (1,)), ((), ()))
    h1_hi = h1.astype(BF16)
    h1_lo = (h1 - h1_hi.astype(F32)).astype(BF16)
    both = lax.dot_general(wr_ref[...], h1_hi, nt, preferred_element_type=F32)
    lt = both[:NR_PAD] + both[NR_PAD:] + lax.dot_general(
        wr_ref[:NR_PAD, :], h1_lo, nt, preferred_element_type=F32)
    lt = lt + br_ref[...]
    g_logits = lt[0:N_GROUPS]
    gmax, g_idx = _first_argmax(g_logits, N_GROUPS)
    g_gate = 1.0 / jnp.sum(jnp.exp(g_logits - gmax), axis=0, keepdims=True)
    e_logits = jnp.zeros((EXPERTS_PER_GROUP, TM_MIX), F32)
    for g in range(N_GROUPS):
        lo = NR_EXPERT0 + g * EXPERTS_PER_GROUP
        e_logits = jnp.where(g_idx == g, lt[lo:lo + EXPERTS_PER_GROUP], e_logits)
    v1, i1 = _first_argmax(e_logits, EXPERTS_PER_GROUP)
    erow = lax.broadcasted_iota(I32, e_logits.shape, 0)
    v2, i2 = _first_argmax(jnp.where(erow == i1, -jnp.inf, e_logits), EXPERTS_PER_GROUP)
    t = jnp.exp(v2 - v1)
    w1 = g_gate / (1.0 + t)
    w2 = g_gate * t / (1.0 + t)
    e1 = g_idx * EXPERTS_PER_GROUP + i1
    e2 = g_idx * EXPERTS_PER_GROUP + i2

    xrow = lax.broadcasted_iota(I32, (N_EXPERTS, TM_MIX), 0)
    oh1 = xrow == e1
    oh2 = xrow == e2
    oh = jnp.where(jnp.logical_or(oh1, oh2), 1.0, 0.0)
    tr = lax.broadcasted_iota(I32, (TM_MIX, TM_MIX), 0)
    tc = lax.broadcasted_iota(I32, (TM_MIX, TM_MIX), 1)
    before = jnp.where(tr < tc, 1.0, 0.0).astype(BF16)
    cum = jnp.dot(oh.astype(BF16), before, preferred_element_type=F32)
    base = cum + run_ref[:, 0:1]
    rank1 = jnp.sum(jnp.where(oh1, base, 0.0), axis=0, keepdims=True).astype(I32)
    rank2 = jnp.sum(jnp.where(oh2, base, 0.0), axis=0, keepdims=True).astype(I32)
    run_ref[...] = run_ref[...] + jnp.sum(oh, axis=1, keepdims=True)
    cnt_ref[...] = run_ref[...].astype(I32)

    zi = jnp.zeros((ROUTE_ROWS - 4, TM_MIX), I32)
    ri_ref[...] = jnp.concatenate([e1, e2, rank1, rank2, zi], axis=0)
    zw = jnp.zeros((ROUTE_ROWS - 2, TM_MIX), F32)
    rw_ref[...] = jnp.concatenate([w1, w2, zw], axis=0)

    posv_ref[...] = jnp.concatenate([e1 * XS_CAP + rank1, e2 * XS_CAP + rank2, zi, zi[:2]], axis=0)
    pos_copy(slot).start()

    @pl.when(i == n - 1)
    def _():
        pos_copy(slot).wait()

        def last(j, c):
            row_copy(slot, 0, j).start(priority=0)
            row_copy(slot, 1, j).start(priority=1)
            return c
        lax.fori_loop(0, TM_MIX, last, 0, unroll=8)
        drain(prev)
        drain(slot)


def _mix(a2, proj2, x2, wout_b, sgw_b, sgb, sgg, sgbeta, g1, b1, wr, br):
    t, d = x2.shape
    assert XS_CAP >= t + TMX
    u_blk = (proj2.shape[1] - 2 * SG_WIDTH) // SG_WIDTH
    const = lambda *shape: pl.BlockSpec(shape, lambda i: (0,) * len(shape))
    return pl.pallas_call(
        _mix_kernel,
        grid=(t // TM_MIX,),
        in_specs=[
            pl.BlockSpec((TM_MIX, ATT_WIDTH), lambda i: (i, 0)),
            pl.BlockSpec((TM_MIX, SG_WIDTH), lambda i: (i, u_blk)),
            pl.BlockSpec((TM_MIX, SG_WIDTH), lambda i: (i, u_blk + 1)),
            pl.BlockSpec((TM_MIX, d), lambda i: (i, 0)),
            pl.BlockSpec((d, d), lambda i: (0, 0), pipeline_mode=pl.Buffered(1)),
            const(SG_GROUPS, SG_CHUNK, SG_CHUNK),
            const(SG_GROUPS, SG_CHUNK, 1),
            const(SG_GROUPS, SG_DIM),
            const(SG_GROUPS, SG_DIM),
            const(1, d),
            const(1, d),
            const(2 * NR_PAD, d),
            const(NR_PAD, 1),
        ],
        out_specs=[
            pl.BlockSpec((TM_MIX, d), lambda i: (i, 0)),
            pl.BlockSpec(memory_space=pl.ANY),
            pl.BlockSpec((ROUTE_ROWS, TM_MIX), lambda i: (0, i)),
            pl.BlockSpec((ROUTE_ROWS, TM_MIX), lambda i: (0, i)),
            const(N_EXPERTS, LANES),
        ],
        out_shape=[
            jax.ShapeDtypeStruct((t, d), F32),
            jax.ShapeDtypeStruct((N_EXPERTS * XS_CAP + 2 * TM_MIX, d // 2), U32),
            jax.ShapeDtypeStruct((ROUTE_ROWS, t), I32),
            jax.ShapeDtypeStruct((ROUTE_ROWS, t), F32),
            jax.ShapeDtypeStruct((N_EXPERTS, LANES), I32),
        ],
        scratch_shapes=[pltpu.VMEM((N_EXPERTS, LANES), F32),
                        pltpu.VMEM((2, TM_MIX, d // 2), U32),
                        pltpu.VMEM((ROUTE_ROWS, TM_MIX), I32),
                        pltpu.SMEM((2, ROUTE_ROWS, TM_MIX), I32),
                        pltpu.SemaphoreType.DMA((2,)),
                        pltpu.SemaphoreType.DMA((2,))],
        compiler_params=_cparams(("arbitrary",)),
        name="mix",
    )(a2, proj2, proj2, x2, wout_b, sgw_b, sgb, sgg, sgbeta, g1, b1, wr, br)


def _experts_kernel(ts_ref, nt_ref, nu_ref, trow_ref, nval_ref, x_hbm, wg_ref, wu_ref, wd_ref,
                    y_hbm, xbuf, ybuf, xsem, ysem):
    e = pl.program_id(0)
    d = wg_ref.shape[1]
    n_used = nu_ref[0]

    def x_copy(g, slot):
        rows = pl.ds(pl.multiple_of(trow_ref[g], TMX), TMX)
        return pltpu.make_async_copy(x_hbm.at[rows], xbuf.at[slot], xsem.at[slot])

    def y_copy(g, slot):
        rows = pl.ds(pl.multiple_of(g * TMX, TMX), TMX)
        return pltpu.make_async_copy(ybuf.at[slot], y_hbm.at[rows], ysem.at[slot])

    @pl.when(jnp.logical_and(e == 0, n_used > 0))
    def _():
        x_copy(0, 0).start(priority=1)

    def tile(t, carry):
        g = ts_ref[e] + t
        slot = g % 2
        x_copy(g, slot).wait()

        @pl.when(g + 1 < n_used)
        def _():
            x_copy(g + 1, 1 - slot).start(priority=1)

        @pl.when(g >= 2)
        def _():
            y_copy(g - 2, slot).wait()

        row = lax.broadcasted_iota(I32, xbuf.shape[1:], 0)
        xw = jnp.where(row < nval_ref[g], xbuf[slot], jnp.uint32(0))
        lo, hi = _unpack_bf16_pair(xw)
        lo, hi = lo.astype(BF16), hi.astype(BF16)
        gate = jnp.dot(lo, wg_ref[0, :d // 2, :], preferred_element_type=F32)
        gate = gate + jnp.dot(hi, wg_ref[0, d // 2:, :], preferred_element_type=F32)
        up = jnp.dot(lo, wu_ref[0, :d // 2, :], preferred_element_type=F32)
        up = up + jnp.dot(hi, wu_ref[0, d // 2:, :], preferred_element_type=F32)
        hmid = (gate * jax.nn.sigmoid(gate) * up).astype(BF16)
        y = jnp.dot(hmid, wd_ref[0], preferred_element_type=F32)
        ybuf[slot] = _pack_bf16_pair(y)
        y_copy(g, slot).start(priority=1)
        return carry

    lax.fori_loop(0, nt_ref[e], tile, 0)

    @pl.when(e == pl.num_programs(0) - 1)
    def _():
        @pl.when(n_used >= 2)
        def _():
            y_copy(n_used - 2, (n_used - 2) % 2).wait()

        @pl.when(n_used >= 1)
        def _():
            y_copy(n_used - 1, (n_used - 1) % 2).wait()


def _experts(tile_start, n_tiles_e, n_used, tile_row, tile_valid, xs, w_gate, w_up, w_down):
    w = xs.shape[1]
    d = w_gate.shape[1]
    n_rows = tile_row.shape[0] * TMX

    def w_map(e, ts, nt, nu, trow, nval):
        return (e, 0, 0)

    grid_spec = pltpu.PrefetchScalarGridSpec(
        num_scalar_prefetch=5,
        grid=(N_EXPERTS,),
        in_specs=[
            pl.BlockSpec(memory_space=pl.ANY),
            pl.BlockSpec((1, d, D_EXPERT), w_map),
            pl.BlockSpec((1, d, D_EXPERT), w_map),
            pl.BlockSpec((1, D_EXPERT, d), w_map),
        ],
        out_specs=pl.BlockSpec(memory_space=pl.ANY),
        scratch_shapes=[pltpu.VMEM((2, TMX, w), U32),
                        pltpu.VMEM((2, TMX, w), U32),
                        pltpu.SemaphoreType.DMA((2,)),
                        pltpu.SemaphoreType.DMA((2,))],
    )
    return pl.pallas_call(
        _experts_kernel,
        grid_spec=grid_spec,
        out_shape=jax.ShapeDtypeStruct((n_rows, w), U32),
        compiler_params=_cparams(("arbitrary",)),
        name="experts",
    )(tile_start, n_tiles_e, n_used, tile_row, tile_valid, xs, w_gate, w_up, w_down)


def _combine_kernel(pos_ref, posn_ref, y_hbm, h1_ref, wt_ref, g2_ref, b2_ref, o_ref, ybuf, sem):
    i = pl.program_id(0)
    n = pl.num_programs(0)

    def row_copy(src, buf, which, j):
        return pltpu.make_async_copy(y_hbm.at[pl.ds(src, 1)],
                                     ybuf.at[buf, which, pl.ds(j, 1)], sem.at[buf])

    def drain(buf):
        def wait(j, c):
            row_copy(0, buf, 0, 0).wait()
            row_copy(0, buf, 1, 0).wait()
            return c
        lax.fori_loop(0, TM_OUT, wait, 0, unroll=8)

    def issue(p_ref, half, buf):
        for j in range(TM_OUT):
            row_copy(p_ref[half, 0, j], buf, 0, j).start(priority=0)
            row_copy(p_ref[half, 1, j], buf, 1, j).start(priority=1)

    def finish(half, buf):
        rows = slice(half * TM_OUT, (half + 1) * TM_OUT)
        lo1, hi1 = _unpack_bf16_pair(ybuf[buf, 0])
        lo2, hi2 = _unpack_bf16_pair(ybuf[buf, 1])
        w1 = wt_ref[rows, 0:1]
        w2 = wt_ref[rows, 1:2]
        ffn = jnp.concatenate([w1 * lo1 + w2 * lo2, w1 * hi1 + w2 * hi2], axis=1)
        o_ref[rows, :] = _layernorm(DN_ALPHA * h1_ref[rows, :] + ffn, g2_ref[...], b2_ref[...])

    @pl.when(i == 0)
    def _():
        def first(j, c):
            row_copy(pos_ref[0, 0, j], 0, 0, j).start(priority=0)
            row_copy(pos_ref[0, 1, j], 0, 1, j).start(priority=1)
            return c
        lax.fori_loop(0, TM_OUT, first, 0, unroll=8)

    drain(0)
    issue(pos_ref, 1, 1)
    finish(0, 0)
    drain(1)
    issue(posn_ref, 0, 0)
    finish(1, 1)

    @pl.when(i == n - 1)
    def _():
        drain(0)


def _combine(pos4, ys, h1, wtok, g2, b2):
    t, d = h1.shape
    w = ys.shape[1]
    n = t // (2 * TM_OUT)
    return pl.pallas_call(
        _combine_kernel,
        grid=(n,),
        in_specs=[
            pl.BlockSpec((None, 2, 2, TM_OUT), lambda i: (i, 0, 0, 0), memory_space=pltpu.SMEM),
            pl.BlockSpec((None, 2, 2, TM_OUT), lambda i: (jnp.minimum(i + 1, n - 1), 0, 0, 0),
                         memory_space=pltpu.SMEM),
            pl.BlockSpec(memory_space=pl.ANY),
            pl.BlockSpec((2 * TM_OUT, d), lambda i: (i, 0)),
            pl.BlockSpec((2 * TM_OUT, ROUTE_ROWS), lambda i: (i, 0)),
            pl.BlockSpec((1, d), lambda i: (0, 0)),
            pl.BlockSpec((1, d), lambda i: (0, 0)),
        ],
        out_specs=pl.BlockSpec((2 * TM_OUT, d), lambda i: (i, 0)),
        out_shape=jax.ShapeDtypeStruct((t, d), F32),
        scratch_shapes=[pltpu.VMEM((2, 2, TM_OUT, w), U32),
                        pltpu.SemaphoreType.DMA((2,))],
        compiler_params=_cparams(("arbitrary",)),
        name="combine",
    )(pos4, pos4, ys, h1, wtok, g2, b2)


def _lambda_init(layer):
    return 0.8 - 0.6 * math.exp(-0.3 * layer)


def kernel(x, w_in, w_out, ln1_g, ln1_b, ln2_g, ln2_b, rel_bias, lam_q1, lam_k1, lam_q2, lam_k2,
           subln_g, sg_ln_g, sg_ln_b, sg_w, sg_b, w_router_group, b_router_group,
           w_router_expert, b_router_expert, w_exp_gate, w_exp_up, w_exp_down):
    b, s, d = x.shape
    t = b * s
    assert w_in.shape[0] == DEPTH == 1
    l = 0
    lam_init = _lambda_init(l)
    x2 = x.reshape(t, d)

    proj = _inproj(x2, w_in[l])

    lam = (jnp.exp(jnp.sum(lam_q1[l].astype(F32) * lam_k1[l].astype(F32)))
           - jnp.exp(jnp.sum(lam_q2[l].astype(F32) * lam_k2[l].astype(F32))) + lam_init)
    side = [w_exp_gate[l].reshape(-1, D_EXPERT), w_exp_up[l].reshape(-1, D_EXPERT),
            w_exp_down[l].reshape(-1, d)]
    a, wg_b, wu_b, wd_b = _attention(
        proj.reshape(b, s, -1), _bias_sequences(rel_bias, s), lam.reshape(1),
        subln_g[l].reshape(1, -1), lam_init, side)

    wr = jnp.concatenate([w_router_group[l].T,
                          jnp.zeros((NR_EXPERT0 - N_GROUPS, d), F32),
                          jnp.transpose(w_router_expert[l], (0, 2, 1)).reshape(N_EXPERTS, d)],
                         axis=0)
    br = jnp.concatenate([b_router_group[l], jnp.zeros((NR_EXPERT0 - N_GROUPS,), F32),
                          b_router_expert[l].reshape(-1)]).reshape(NR_PAD, 1)
    wr_hi = wr.astype(BF16)
    wr = jnp.concatenate([wr_hi, (wr - wr_hi.astype(F32)).astype(BF16)], axis=0)
    h1, xs, ri, rw, cnt = _mix(
        a.reshape(t, -1), proj, x2, w_out[l].astype(BF16), sg_w[l].astype(BF16),
        sg_b[l].reshape(SG_GROUPS, SG_CHUNK, 1), sg_ln_g[l], sg_ln_b[l],
        ln1_g[l].reshape(1, d), ln1_b[l].reshape(1, d), wr, br)

    counts = cnt[:, 0]
    tiles_e = (counts + TMX - 1) // TMX
    tile_end = jnp.cumsum(tiles_e)
    tile_start = tile_end - tiles_e
    n_used = tile_end[-1:]
    max_tiles = (2 * t) // TMX + N_EXPERTS
    row_start = (tile_start * TMX).astype(I32)
    eids = jnp.arange(N_EXPERTS, dtype=I32)[:, None, None]
    pos = jnp.sum(jnp.where(ri[0:2][None] == eids, row_start[:, None, None], 0), axis=0) + ri[2:4]
    pos4 = pos.reshape(2, t // (2 * TM_OUT), 2, TM_OUT).transpose(1, 2, 0, 3)
    tile_ids = jnp.arange(max_tiles, dtype=I32)
    tile_e = jnp.minimum(jnp.searchsorted(tile_end, tile_ids, side="right"), N_EXPERTS - 1)
    local = tile_ids - tile_start[tile_e]
    in_use = tile_ids < n_used[0]
    tile_row = jnp.where(in_use, tile_e * XS_CAP + local * TMX, 0).astype(I32)
    tile_valid = jnp.where(in_use, jnp.clip(counts[tile_e] - local * TMX, 0, TMX), 0).astype(I32)

    ys = _experts(tile_start.astype(I32), tiles_e.astype(I32), n_used.astype(I32),
                  tile_row, tile_valid, xs,
                  wg_b.reshape(N_EXPERTS, d, D_EXPERT), wu_b.reshape(N_EXPERTS, d, D_EXPERT),
                  wd_b.reshape(N_EXPERTS, D_EXPERT, d))
    out = _combine(pos4, ys, h1, rw.T, ln2_g[l].reshape(1, d), ln2_b[l].reshape(1, d))
    return out.reshape(b, s, d)
```

```python
import functools
import math

import jax
import jax.numpy as jnp
from jax import lax
from jax.experimental import pallas as pl
from jax.experimental.pallas import tpu as pltpu

F32 = jnp.float32
BF16 = jnp.bfloat16
U32 = jnp.uint32
I32 = jnp.int32

ATT_HEADS = 8
ATT_QK_DIM = 64
ATT_V_DIM = 128
ATT_WIDTH = ATT_HEADS * ATT_V_DIM
SG_GROUPS = 8
SG_DIM = 128
SG_WIDTH = SG_GROUPS * SG_DIM
SG_CHUNK = 128
REL_BUCKETS = 32
REL_MAX_DIST = 128
N_GROUPS = 4
EXPERTS_PER_GROUP = 8
N_EXPERTS = N_GROUPS * EXPERTS_PER_GROUP
D_EXPERT = 256
DEPTH = 1
DN_ALPHA = (2.0 * DEPTH) ** 0.25
LN_EPS = 1e-5
LOG2E = math.log2(math.e)

LANES = 128
SUBLANES = 8
VMEM_LIMIT_BYTES = 56 * 1024 * 1024

TM_IN = 1024
TN_IN = 1024
QB = 256
KC = 256
N_BIAS_TILES = 5
TM_MIX = 512
TMX = 256
TM_OUT = 256
N_XSLOTS = 4
ROUTE_ROWS = 8
XS_CAP = 8192 + TMX
NR_EXPERT0 = 8
NR_PAD = NR_EXPERT0 + 32


def _cparams(sem, flags=None):
    return pltpu.CompilerParams(dimension_semantics=sem, vmem_limit_bytes=VMEM_LIMIT_BYTES,
                                flags=flags)


def _inproj_kernel(x_ref, w_ref, o_ref, xb_ref, *, n_q, n_lin, q_scale):
    j = pl.program_id(1)

    @pl.when(j == 0)
    def _():
        xb_ref[...] = x_ref[...].astype(BF16)

    acc = jnp.dot(xb_ref[...], w_ref[...].astype(BF16), preferred_element_type=F32)

    @pl.when(j < n_q)
    def _():
        o_ref[...] = (acc * q_scale).astype(BF16)

    @pl.when(jnp.logical_and(j >= n_q, j < n_lin))
    def _():
        o_ref[...] = acc.astype(BF16)

    @pl.when(j >= n_lin)
    def _():
        o_ref[...] = jax.nn.gelu(acc).astype(BF16)


def _inproj(x2, w_in):
    t, d = x2.shape
    n = w_in.shape[1]
    n_q = (ATT_HEADS * 2 * ATT_QK_DIM) // TN_IN
    n_lin = (2 * ATT_HEADS * 2 * ATT_QK_DIM + ATT_WIDTH) // TN_IN
    kern = functools.partial(_inproj_kernel, n_q=n_q, n_lin=n_lin,
                             q_scale=(ATT_QK_DIM ** -0.5) * LOG2E)
    return pl.pallas_call(
        kern,
        grid=(t // TM_IN, n // TN_IN),
        in_specs=[pl.BlockSpec((TM_IN, d), lambda i, j: (i, 0)),
                  pl.BlockSpec((d, TN_IN), lambda i, j: (0, j))],
        out_specs=pl.BlockSpec((TM_IN, TN_IN), lambda i, j: (i, j)),
        out_shape=jax.ShapeDtypeStruct((t, n), BF16),
        scratch_shapes=[pltpu.VMEM((TM_IN, d), BF16)],
        compiler_params=_cparams(("arbitrary", "arbitrary")),
        name="inproj",
    )(x2, w_in)


def _attn_kernel(lam_ref, q_ref, k_ref, v_ref, bseq_ref, g_ref, *rest, n_kc, out_scale):
    n_side = (len(rest) - 3) // 2
    side_in, o_ref, side_out = rest[:n_side], rest[n_side], rest[n_side + 1:2 * n_side + 1]
    vaug_ref, bias_ref = rest[2 * n_side + 1:]
    qi = pl.program_id(2)

    @pl.when(qi == 0)
    def _():
        vaug_ref[:, :ATT_V_DIM] = v_ref[0]
        vaug_ref[:, ATT_V_DIM:] = jnp.ones((vaug_ref.shape[0], ATT_V_DIM), BF16)
        for t in range(N_BIAS_TILES):
            seq = jnp.broadcast_to(bseq_ref[0, t:t + 1, :], (QB, 2 * KC))
            bias_ref[t] = pltpu.roll(seq, 0, 1, stride=1, stride_axis=0)[:, :KC]

    q = q_ref[0]
    lane = lax.broadcasted_iota(I32, q.shape, 1)
    zero = jnp.zeros_like(q)
    qq = jnp.concatenate([jnp.where(lane < ATT_QK_DIM, q, zero),
                          jnp.where(lane >= ATT_QK_DIM, q, zero)], axis=0)

    for src, dst in zip(side_in, side_out):
        dst[...] = src[...].astype(dst.dtype)

    m = jnp.full((2 * QB, 1), -jnp.inf, F32)
    acc = jnp.zeros((2 * QB, 2 * ATT_V_DIM), F32)
    for kc in range(n_kc):
        kblk = k_ref[0, kc * KC:(kc + 1) * KC, :]
        s = lax.dot_general(qq, kblk, (((1,), (1,)), ((), ())),
                            preferred_element_type=F32)
        b = bias_ref[jnp.clip(kc - qi, -2, 2) + 2]
        s = s + jnp.concatenate([b, b], axis=0)
        m_new = jnp.maximum(m, jnp.max(s, axis=1, keepdims=True))
        p = jnp.exp2(s - m_new).astype(BF16)
        acc = acc * jnp.exp2(m - m_new) + jnp.dot(
            p, vaug_ref[kc * KC:(kc + 1) * KC, :], preferred_element_type=F32)
        m = m_new
    lam = lam_ref[0]
    o0 = acc[:QB, :ATT_V_DIM] / acc[:QB, ATT_V_DIM:ATT_V_DIM + 1]
    o1 = acc[QB:, :ATT_V_DIM] / acc[QB:, ATT_V_DIM:ATT_V_DIM + 1]
    of = o0 - lam * o1
    of = of * lax.rsqrt(jnp.mean(of * of, axis=-1, keepdims=True) + LN_EPS)
    o_ref[0] = (of * (g_ref[...] * out_scale)).astype(o_ref.dtype)


def _attention(proj3, bias_seq, lam, subln_g, lam_init, side):
    b, s, _ = proj3.shape
    n_kc = s // KC
    n_q = s // QB
    n_steps = b * ATT_HEADS * n_q
    h_off_k = ATT_HEADS
    h_off_v = 2 * ATT_HEADS
    kern = functools.partial(_attn_kernel, n_kc=n_kc, out_scale=1.0 - lam_init)

    def side_map(bi, h, qi):
        return ((bi * ATT_HEADS + h) * n_q + qi, 0)

    side_specs = []
    for arr in side:
        rows, cols = arr.shape
        assert rows % (n_steps * 2 * SUBLANES) == 0
        side_specs.append(pl.BlockSpec((rows // n_steps, cols), side_map))
    return pl.pallas_call(
        kern,
        grid=(b, ATT_HEADS, n_q),
        in_specs=[
            pl.BlockSpec(memory_space=pltpu.SMEM),
            pl.BlockSpec((1, QB, LANES), lambda bi, h, qi: (bi, qi, h)),
            pl.BlockSpec((1, s, LANES), lambda bi, h, qi: (bi, 0, h_off_k + h)),
            pl.BlockSpec((1, s, LANES), lambda bi, h, qi: (bi, 0, h_off_v + h)),
            pl.BlockSpec((1, N_BIAS_TILES, 2 * KC), lambda bi, h, qi: (h, 0, 0)),
            pl.BlockSpec((1, ATT_V_DIM), lambda bi, h, qi: (0, 0)),
        ] + side_specs,
        out_specs=[pl.BlockSpec((1, QB, ATT_V_DIM), lambda bi, h, qi: (bi, qi, h))] + side_specs,
        out_shape=[jax.ShapeDtypeStruct((b, s, ATT_WIDTH), BF16)]
        + [jax.ShapeDtypeStruct(arr.shape, BF16) for arr in side],
        scratch_shapes=[pltpu.VMEM((s, 2 * ATT_V_DIM), BF16),
                        pltpu.VMEM((N_BIAS_TILES, QB, KC), F32)],
        compiler_params=_cparams(("arbitrary", "arbitrary", "arbitrary")),
        name="attn",
    )(lam, proj3, proj3, proj3, bias_seq, subln_g, *side)


def _rel_bucket(rel):
    half = REL_BUCKETS // 2
    max_exact = half // 2
    ret = jnp.where(rel > 0, half, 0)
    n = jnp.abs(rel)
    nf = jnp.maximum(n, 1).astype(F32)
    large = max_exact + (jnp.log(nf / max_exact) / math.log(REL_MAX_DIST / max_exact)
                         * (half - max_exact)).astype(I32)
    large = jnp.minimum(large, half - 1)
    return ret + jnp.where(n < max_exact, n, large)


def _bias_sequences(rel_bias, s):
    assert REL_MAX_DIST <= KC and QB == KC
    period = 2 * KC
    j = jnp.arange(period, dtype=I32)
    off = jnp.where(j < KC, j, j - period)
    d = jnp.arange(-2, 3, dtype=I32)[:, None]
    rel = jnp.clip(d * KC + off[None, :], -(s - 1), s - 1)
    return jnp.transpose(rel_bias[_rel_bucket(rel)], (2, 0, 1)).astype(F32) * LOG2E


def _layernorm(v, g, b):
    mu = jnp.mean(v, axis=-1, keepdims=True)
    var = jnp.mean(jnp.square(v - mu), axis=-1, keepdims=True)
    return (v - mu) * lax.rsqrt(var + LN_EPS) * g + b


def _pack_bf16_pair(v):
    n = v.shape[1] // 2
    bits = lax.bitcast_convert_type(v.astype(BF16).astype(F32), U32)
    return (bits[:, :n] >> 16) | (bits[:, n:] & jnp.uint32(0xFFFF0000))


def _unpack_bf16_pair(w):
    lo = lax.bitcast_convert_type(w << 16, F32)
    hi = lax.bitcast_convert_type(w & jnp.uint32(0xFFFF0000), F32)
    return lo, hi


def _first_argmax(vals, nrows):
    mx = jnp.max(vals, axis=0, keepdims=True)
    row = lax.broadcasted_iota(I32, vals.shape, 0)
    idx = jnp.min(jnp.where(vals == mx, row, nrows), axis=0, keepdims=True)
    return mx, idx


def _mix_kernel(a_ref, u_ref, vg_ref, x_ref, wout_ref, sgw_ref, sgb_ref, sgg_ref, sgbeta_ref,
                g1_ref, b1_ref, wr_ref, br_ref,
                h1_ref, xs_hbm, ri_ref, rw_ref, cnt_ref,
                run_ref, hp_s, posv_ref, psm, ssem, rsem):
    i = pl.program_id(0)
    n = pl.num_programs(0)
    slot = i % 2
    prev = 1 - slot

    def pos_copy(sl):
        return pltpu.make_async_copy(posv_ref, psm.at[sl], ssem.at[sl])

    def row_copy(sl, which, j):
        return pltpu.make_async_copy(hp_s.at[sl, pl.ds(j, 1)],
                                     xs_hbm.at[pl.ds(psm[sl, which, j], 1)], rsem.at[sl])

    def drain(sl):
        def wait(j, c):
            row_copy(sl, 0, 0).wait()
            row_copy(sl, 1, 0).wait()
            return c
        lax.fori_loop(0, TM_MIX, wait, 0, unroll=8)

    @pl.when(i == 0)
    def _():
        run_ref[...] = jnp.zeros_like(run_ref)
        hp_s[1] = jnp.zeros(hp_s.shape[1:], hp_s.dtype)
        spare = N_EXPERTS * XS_CAP + lax.broadcasted_iota(I32, posv_ref.shape, 1)
        spare = spare + TM_MIX * lax.broadcasted_iota(I32, posv_ref.shape, 0)
        posv_ref[...] = jnp.where(spare < xs_hbm.shape[0], spare, 0)
        pos_copy(1).start()

    @pl.when(i > 0)
    def _():
        drain(slot)

    pos_copy(prev).wait()

    for j in range(TM_MIX):
        row_copy(prev, 0, j).start(priority=0)
        row_copy(prev, 1, j).start(priority=1)

    h1_ref[...] = jnp.dot(a_ref[...], wout_ref[:ATT_WIDTH, :], preferred_element_type=F32)

    n_chunks = TM_MIX // SG_CHUNK
    rows = []
    for c in range(n_chunks):
        cols = []
        for g in range(SG_GROUPS):
            rs = slice(c * SG_CHUNK, (c + 1) * SG_CHUNK)
            cs = slice(g * SG_DIM, (g + 1) * SG_DIM)
            vt = vg_ref[rs, cs].astype(F32)
            vn = _layernorm(vt, sgg_ref[g:g + 1, :], sgbeta_ref[g:g + 1, :])
            mixed = jnp.dot(sgw_ref[g], vn.astype(BF16), preferred_element_type=F32) + sgb_ref[g]
            cols.append((u_ref[rs, cs].astype(F32) * mixed).astype(BF16))
        rows.append(jnp.concatenate(cols, axis=1))
    sgate = jnp.concatenate(rows, axis=0)

    mix = h1_ref[...] + jnp.dot(sgate, wout_ref[ATT_WIDTH:, :], preferred_element_type=F32)
    h1 = _layernorm(DN_ALPHA * x_ref[...] + mix, g1_ref[...], b1_ref[...])
    h1_ref[...] = h1
    hp_s[slot] = _pack_bf16_pair(h1)

    nt = (((1,), (1,)), ((), ()))
    h1_hi = h1.astype(BF16)
    h1_lo = (h1 - h1_hi.astype(F32)).astype(BF16)
    both = lax.dot_general(wr_ref[...], h1_hi, nt, preferred_element_type=F32)
    lt = both[:NR_PAD] + both[NR_PAD:] + lax.dot_general(
        wr_ref[:NR_PAD, :], h1_lo, nt, preferred_element_type=F32)
    lt = lt + br_ref[...]
    g_logits = lt[0:N_GROUPS]
    gmax, g_idx = _first_argmax(g_logits, N_GROUPS)
    g_gate = 1.0 / jnp.sum(jnp.exp(g_logits - gmax), axis=0, keepdims=True)
    e_logits = jnp.zeros((EXPERTS_PER_GROUP, TM_MIX), F32)
    for g in range(N_GROUPS):
        lo = NR_EXPERT0 + g * EXPERTS_PER_GROUP
        e_logits = jnp.where(g_idx == g, lt[lo:lo + EXPERTS_PER_GROUP], e_logits)
    v1, i1 = _first_argmax(e_logits, EXPERTS_PER_GROUP)
    erow = lax.broadcasted_iota(I32, e_logits.shape, 0)
    v2, i2 = _first_argmax(jnp.where(erow == i1, -jnp.inf, e_logits), EXPERTS_PER_GROUP)
    t = jnp.exp(v2 - v1)
    w1 = g_gate / (1.0 + t)
    w2 = g_gate * t / (1.0 + t)
    e1 = g_idx * EXPERTS_PER_GROUP + i1
    e2 = g_idx * EXPERTS_PER_GROUP + i2

    xrow = lax.broadcasted_iota(I32, (N_EXPERTS, TM_MIX), 0)
    oh1 = xrow == e1
    oh2 = xrow == e2
    oh = jnp.where(jnp.logical_or(oh1, oh2), 1.0, 0.0)
    tr = lax.broadcasted_iota(I32, (TM_MIX, TM_MIX), 0)
    tc = lax.broadcasted_iota(I32, (TM_MIX, TM_MIX), 1)
    before = jnp.where(tr < tc, 1.0, 0.0).astype(BF16)
    cum = jnp.dot(oh.astype(BF16), before, preferred_element_type=F32)
    base = cum + run_ref[:, 0:1]
    rank1 = jnp.sum(jnp.where(oh1, base, 0.0), axis=0, keepdims=True).astype(I32)
    rank2 = jnp.sum(jnp.where(oh2, base, 0.0), axis=0, keepdims=True).astype(I32)
    run_ref[...] = run_ref[...] + jnp.sum(oh, axis=1, keepdims=True)
    cnt_ref[...] = run_ref[...].astype(I32)

    zi = jnp.zeros((ROUTE_ROWS - 4, TM_MIX), I32)
    ri_ref[...] = jnp.concatenate([e1, e2, rank1, rank2, zi], axis=0)
    zw = jnp.zeros((ROUTE_ROWS - 2, TM_MIX), F32)
    rw_ref[...] = jnp.concatenate([w1, w2, zw], axis=0)

    posv_ref[...] = jnp.concatenate([e1 * XS_CAP + rank1, e2 * XS_CAP + rank2, zi, zi[:2]], axis=0)
    pos_copy(slot).start()

    @pl.when(i == n - 1)
    def _():
        pos_copy(slot).wait()

        def last(j, c):
            row_copy(slot, 0, j).start(priority=0)
            row_copy(slot, 1, j).start(priority=1)
            return c
        lax.fori_loop(0, TM_MIX, last, 0, unroll=8)
        drain(prev)
        drain(slot)


def _mix(a2, proj2, x2, wout_b, sgw_b, sgb, sgg, sgbeta, g1, b1, wr, br):
    t, d = x2.shape
    assert XS_CAP >= t + TMX
    u_blk = (proj2.shape[1] - 2 * SG_WIDTH) // SG_WIDTH
    const = lambda *shape: pl.BlockSpec(shape, lambda i: (0,) * len(shape))
    return pl.pallas_call(
        _mix_kernel,
        grid=(t // TM_MIX,),
        in_specs=[
            pl.BlockSpec((TM_MIX, ATT_WIDTH), lambda i: (i, 0)),
            pl.BlockSpec((TM_MIX, SG_WIDTH), lambda i: (i, u_blk)),
            pl.BlockSpec((TM_MIX, SG_WIDTH), lambda i: (i, u_blk + 1)),
            pl.BlockSpec((TM_MIX, d), lambda i: (i, 0)),
            pl.BlockSpec((d, d), lambda i: (0, 0), pipeline_mode=pl.Buffered(1)),
            const(SG_GROUPS, SG_CHUNK, SG_CHUNK),
            const(SG_GROUPS, SG_CHUNK, 1),
            const(SG_GROUPS, SG_DIM),
            const(SG_GROUPS, SG_DIM),
            const(1, d),
            const(1, d),
            const(2 * NR_PAD, d),
            const(NR_PAD, 1),
        ],
        out_specs=[
            pl.BlockSpec((TM_MIX, d), lambda i: (i, 0)),
            pl.BlockSpec(memory_space=pl.ANY),
            pl.BlockSpec((ROUTE_ROWS, TM_MIX), lambda i: (0, i)),
            pl.BlockSpec((ROUTE_ROWS, TM_MIX), lambda i: (0, i)),
            const(N_EXPERTS, LANES),
        ],
        out_shape=[
            jax.ShapeDtypeStruct((t, d), F32),
            jax.ShapeDtypeStruct((N_EXPERTS * XS_CAP + 2 * TM_MIX, d // 2), U32),
            jax.ShapeDtypeStruct((ROUTE_ROWS, t), I32),
            jax.ShapeDtypeStruct((ROUTE_ROWS, t), F32),
            jax.ShapeDtypeStruct((N_EXPERTS, LANES), I32),
        ],
        scratch_shapes=[pltpu.VMEM((N_EXPERTS, LANES), F32),
                        pltpu.VMEM((2, TM_MIX, d // 2), U32),
                        pltpu.VMEM((ROUTE_ROWS, TM_MIX), I32),
                        pltpu.SMEM((2, ROUTE_ROWS, TM_MIX), I32),
                        pltpu.SemaphoreType.DMA((2,)),
                        pltpu.SemaphoreType.DMA((2,))],
        compiler_params=_cparams(("arbitrary",)),
        name="mix",
    )(a2, proj2, proj2, x2, wout_b, sgw_b, sgb, sgg, sgbeta, g1, b1, wr, br)


def _experts_kernel(ts_ref, nt_ref, nu_ref, trow_ref, nval_ref, x_hbm, wg_ref, wu_ref, wd_ref,
                    y_hbm, xbuf, ybuf, xsem, ysem):
    e = pl.program_id(0)
    d = wg_ref.shape[1]
    n_used = nu_ref[0]

    def x_copy(g):
        rows = pl.ds(pl.multiple_of(trow_ref[g], TMX), TMX)
        slot = g % N_XSLOTS
        return pltpu.make_async_copy(x_hbm.at[rows], xbuf.at[slot], xsem.at[slot])

    def y_copy(g):
        rows = pl.ds(pl.multiple_of(g * TMX, TMX), TMX)
        slot = g % N_XSLOTS
        return pltpu.make_async_copy(ybuf.at[slot], y_hbm.at[rows], ysem.at[slot])

    def start_x(g):
        @pl.when(g < n_used)
        def _():
            x_copy(g).start(priority=1)

    def free_y(g):
        @pl.when(g >= N_XSLOTS)
        def _():
            y_copy(g - N_XSLOTS).wait()

    def load(g):
        row = lax.broadcasted_iota(I32, xbuf.shape[1:], 0)
        xw = jnp.where(row < nval_ref[g], xbuf[g % N_XSLOTS], jnp.uint32(0))
        lo, hi = _unpack_bf16_pair(xw)
        return lo.astype(BF16), hi.astype(BF16)

    def mlp(lo, hi):
        gate = jnp.dot(lo, wg_ref[0, :d // 2, :], preferred_element_type=F32)
        gate = gate + jnp.dot(hi, wg_ref[0, d // 2:, :], preferred_element_type=F32)
        up = jnp.dot(lo, wu_ref[0, :d // 2, :], preferred_element_type=F32)
        up = up + jnp.dot(hi, wu_ref[0, d // 2:, :], preferred_element_type=F32)
        hmid = (gate * jax.nn.sigmoid(gate) * up).astype(BF16)
        return _pack_bf16_pair(jnp.dot(hmid, wd_ref[0], preferred_element_type=F32))

    @pl.when(e == 0)
    def _():
        start_x(0)
        start_x(1)

    n_pairs = nt_ref[e] // 2

    def pair(t, carry):
        g = ts_ref[e] + 2 * t
        x_copy(g).wait()
        x_copy(g + 1).wait()
        start_x(g + 2)
        start_x(g + 3)
        free_y(g)
        free_y(g + 1)
        lo0, hi0 = load(g)
        lo1, hi1 = load(g + 1)
        y = mlp(jnp.concatenate([lo0, lo1], axis=0), jnp.concatenate([hi0, hi1], axis=0))
        ybuf[g % N_XSLOTS] = y[:TMX]
        ybuf[(g + 1) % N_XSLOTS] = y[TMX:]
        y_copy(g).start(priority=1)
        y_copy(g + 1).start(priority=1)
        return carry

    def single(t, carry):
        g = ts_ref[e] + 2 * n_pairs
        x_copy(g).wait()
        start_x(g + 2)
        free_y(g)
        ybuf[g % N_XSLOTS] = mlp(*load(g))
        y_copy(g).start(priority=1)
        return carry

    lax.fori_loop(0, n_pairs, pair, 0)
    lax.fori_loop(0, nt_ref[e] - 2 * n_pairs, single, 0)

    @pl.when(e == pl.num_programs(0) - 1)
    def _():
        for back in range(N_XSLOTS, 0, -1):
            @pl.when(n_used >= back)
            def _():
                y_copy(n_used - back).wait()


def _experts(tile_start, n_tiles_e, n_used, tile_row, tile_valid, xs, w_gate, w_up, w_down):
    w = xs.shape[1]
    d = w_gate.shape[1]
    n_rows = tile_row.shape[0] * TMX

    def w_map(e, ts, nt, nu, trow, nval):
        return (e, 0, 0)

    grid_spec = pltpu.PrefetchScalarGridSpec(
        num_scalar_prefetch=5,
        grid=(N_EXPERTS,),
        in_specs=[
            pl.BlockSpec(memory_space=pl.ANY),
            pl.BlockSpec((1, d, D_EXPERT), w_map),
            pl.BlockSpec((1, d, D_EXPERT), w_map),
            pl.BlockSpec((1, D_EXPERT, d), w_map),
        ],
        out_specs=pl.BlockSpec(memory_space=pl.ANY),
        scratch_shapes=[pltpu.VMEM((N_XSLOTS, TMX, w), U32),
                        pltpu.VMEM((N_XSLOTS, TMX, w), U32),
                        pltpu.SemaphoreType.DMA((N_XSLOTS,)),
                        pltpu.SemaphoreType.DMA((N_XSLOTS,))],
    )
    return pl.pallas_call(
        _experts_kernel,
        grid_spec=grid_spec,
        out_shape=jax.ShapeDtypeStruct((n_rows, w), U32),
        compiler_params=_cparams(("arbitrary",)),
        name="experts",
    )(tile_start, n_tiles_e, n_used, tile_row, tile_valid, xs, w_gate, w_up, w_down)


def _combine_kernel(pos_ref, posn_ref, y_hbm, h1_ref, wt_ref, g2_ref, b2_ref, o_ref, ybuf, sem):
    i = pl.program_id(0)
    n = pl.num_programs(0)
    slot = i % 2

    def row_copy(src, sl, which, j):
        return pltpu.make_async_copy(y_hbm.at[pl.ds(src, 1)],
                                     ybuf.at[sl, which, pl.ds(j, 1)], sem.at[sl])

    def issue(p_ref, sl):
        def body(j, c):
            row_copy(p_ref[0, j], sl, 0, j).start(priority=0)
            row_copy(p_ref[1, j], sl, 1, j).start(priority=1)
            return c
        lax.fori_loop(0, TM_OUT, body, 0, unroll=8)

    def drain(sl):
        def wait(j, c):
            row_copy(0, sl, 0, 0).wait()
            row_copy(0, sl, 1, 0).wait()
            return c
        lax.fori_loop(0, TM_OUT, wait, 0, unroll=8)

    @pl.when(i == 0)
    def _():
        issue(pos_ref, 0)

    drain(slot)

    for j in range(TM_OUT):
        row_copy(posn_ref[0, j], 1 - slot, 0, j).start(priority=0)
        row_copy(posn_ref[1, j], 1 - slot, 1, j).start(priority=1)

    lo1, hi1 = _unpack_bf16_pair(ybuf[slot, 0])
    lo2, hi2 = _unpack_bf16_pair(ybuf[slot, 1])
    w1 = wt_ref[:, 0:1]
    w2 = wt_ref[:, 1:2]
    ffn = jnp.concatenate([w1 * lo1 + w2 * lo2, w1 * hi1 + w2 * hi2], axis=1)
    o_ref[...] = _layernorm(DN_ALPHA * h1_ref[...] + ffn, g2_ref[...], b2_ref[...])

    @pl.when(i == n - 1)
    def _():
        drain(1 - slot)


def _combine(pos3, ys, h1, wtok, g2, b2):
    t, d = h1.shape
    w = ys.shape[1]
    n = t // TM_OUT
    return pl.pallas_call(
        _combine_kernel,
        grid=(n,),
        in_specs=[
            pl.BlockSpec((None, 2, TM_OUT), lambda i: (i, 0, 0), memory_space=pltpu.SMEM),
            pl.BlockSpec((None, 2, TM_OUT), lambda i: (jnp.minimum(i + 1, n - 1), 0, 0),
                         memory_space=pltpu.SMEM),
            pl.BlockSpec(memory_space=pl.ANY),
            pl.BlockSpec((TM_OUT, d), lambda i: (i, 0)),
            pl.BlockSpec((TM_OUT, ROUTE_ROWS), lambda i: (i, 0)),
            pl.BlockSpec((1, d), lambda i: (0, 0)),
            pl.BlockSpec((1, d), lambda i: (0, 0)),
        ],
        out_specs=pl.BlockSpec((TM_OUT, d), lambda i: (i, 0)),
        out_shape=jax.ShapeDtypeStruct((t, d), F32),
        scratch_shapes=[pltpu.VMEM((2, 2, TM_OUT, w), U32),
                        pltpu.SemaphoreType.DMA((2,))],
        compiler_params=_cparams(("arbitrary",)),
        name="combine",
    )(pos3, pos3, ys, h1, wtok, g2, b2)


def _lambda_init(layer):
    return 0.8 - 0.6 * math.exp(-0.3 * layer)


def kernel(x, w_in, w_out, ln1_g, ln1_b, ln2_g, ln2_b, rel_bias, lam_q1, lam_k1, lam_q2, lam_k2,
           subln_g, sg_ln_g, sg_ln_b, sg_w, sg_b, w_router_group, b_router_group,
           w_router_expert, b_router_expert, w_exp_gate, w_exp_up, w_exp_down):
    b, s, d = x.shape
    t = b * s
    assert w_in.shape[0] == DEPTH == 1
    l = 0
    lam_init = _lambda_init(l)
    x2 = x.reshape(t, d)

    proj = _inproj(x2, w_in[l])

    lam = (jnp.exp(jnp.sum(lam_q1[l].astype(F32) * lam_k1[l].astype(F32)))
           - jnp.exp(jnp.sum(lam_q2[l].astype(F32) * lam_k2[l].astype(F32))) + lam_init)
    side = [w_exp_gate[l].reshape(-1, D_EXPERT), w_exp_up[l].reshape(-1, D_EXPERT),
            w_exp_down[l].reshape(-1, d)]
    a, wg_b, wu_b, wd_b = _attention(
        proj.reshape(b, s, -1), _bias_sequences(rel_bias, s), lam.reshape(1),
        subln_g[l].reshape(1, -1), lam_init, side)

    wr = jnp.concatenate([w_router_group[l].T,
                          jnp.zeros((NR_EXPERT0 - N_GROUPS, d), F32),
                          jnp.transpose(w_router_expert[l], (0, 2, 1)).reshape(N_EXPERTS, d)],
                         axis=0)
    br = jnp.concatenate([b_router_group[l], jnp.zeros((NR_EXPERT0 - N_GROUPS,), F32),
                          b_router_expert[l].reshape(-1)]).reshape(NR_PAD, 1)
    wr_hi = wr.astype(BF16)
    wr = jnp.concatenate([wr_hi, (wr - wr_hi.astype(F32)).astype(BF16)], axis=0)
    h1, xs, ri, rw, cnt = _mix(
        a.reshape(t, -1), proj, x2, w_out[l].astype(BF16), sg_w[l].astype(BF16),
        sg_b[l].reshape(SG_GROUPS, SG_CHUNK, 1), sg_ln_g[l], sg_ln_b[l],
        ln1_g[l].reshape(1, d), ln1_b[l].reshape(1, d), wr, br)

    counts = cnt[:, 0]
    tiles_e = (counts + TMX - 1) // TMX
    tile_end = jnp.cumsum(tiles_e)
    tile_start = tile_end - tiles_e
    n_used = tile_end[-1:]
    max_tiles = (2 * t) // TMX + N_EXPERTS
    row_start = (tile_start * TMX).astype(I32)
    eids = jnp.arange(N_EXPERTS, dtype=I32)[:, None, None]
    pos = jnp.sum(jnp.where(ri[0:2][None] == eids, row_start[:, None, None], 0), axis=0) + ri[2:4]
    pos3 = pos.reshape(2, t // TM_OUT, TM_OUT).transpose(1, 0, 2)
    tile_ids = jnp.arange(max_tiles, dtype=I32)
    tile_e = jnp.minimum(jnp.searchsorted(tile_end, tile_ids, side="right"), N_EXPERTS - 1)
    local = tile_ids - tile_start[tile_e]
    in_use = tile_ids < n_used[0]
    tile_row = jnp.where(in_use, tile_e * XS_CAP + local * TMX, 0).astype(I32)
    tile_valid = jnp.where(in_use, jnp.clip(counts[tile_e] - local * TMX, 0, TMX), 0).astype(I32)

    ys = _experts(tile_start.astype(I32), tiles_e.astype(I32), n_used.astype(I32),
                  tile_row, tile_valid, xs,
                  wg_b.reshape(N_EXPERTS, d, D_EXPERT), wu_b.reshape(N_EXPERTS, d, D_EXPERT),
                  wd_b.reshape(N_EXPERTS, D_EXPERT, d))
    out = _combine(pos3, ys, h1, rw.T, ln2_g[l].reshape(1, d), ln2_b[l].reshape(1, d))
    return out.reshape(b, s, d)
```

```python
import functools
import math

import jax
import jax.numpy as jnp
from jax import lax
from jax.experimental import pallas as pl
from jax.experimental.pallas import tpu as pltpu

F32 = jnp.float32
BF16 = jnp.bfloat16
U32 = jnp.uint32
I32 = jnp.int32

ATT_HEADS = 8
ATT_QK_DIM = 64
ATT_V_DIM = 128
ATT_WIDTH = ATT_HEADS * ATT_V_DIM
SG_GROUPS = 8
SG_DIM = 128
SG_WIDTH = SG_GROUPS * SG_DIM
SG_CHUNK = 128
REL_BUCKETS = 32
REL_MAX_DIST = 128
N_GROUPS = 4
EXPERTS_PER_GROUP = 8
N_EXPERTS = N_GROUPS * EXPERTS_PER_GROUP
D_EXPERT = 256
DEPTH = 1
DN_ALPHA = (2.0 * DEPTH) ** 0.25
LN_EPS = 1e-5
LOG2E = math.log2(math.e)

LANES = 128
SUBLANES = 8
VMEM_LIMIT_BYTES = 56 * 1024 * 1024

TM_IN = 1024
TN_IN = 1024
QB = 256
KC = 256
N_BIAS_TILES = 5
TM_MIX = 512
TMX = 256
TM_OUT = 256
X_AHEAD = 3
N_XSLOTS = 8
ROUTE_ROWS = 8
XS_CAP = 8192 + TMX
NR_EXPERT0 = 8
NR_PAD = NR_EXPERT0 + 32


def _cparams(sem, flags=None):
    return pltpu.CompilerParams(dimension_semantics=sem, vmem_limit_bytes=VMEM_LIMIT_BYTES,
                                flags=flags)


def _inproj_kernel(x_ref, w_ref, o_ref, xb_ref, *, n_q, n_lin, q_scale):
    j = pl.program_id(1)

    @pl.when(j == 0)
    def _():
        xb_ref[...] = x_ref[...].astype(BF16)

    acc = jnp.dot(xb_ref[...], w_ref[...].astype(BF16), preferred_element_type=F32)

    @pl.when(j < n_q)
    def _():
        o_ref[...] = (acc * q_scale).astype(BF16)

    @pl.when(jnp.logical_and(j >= n_q, j < n_lin))
    def _():
        o_ref[...] = acc.astype(BF16)

    @pl.when(j >= n_lin)
    def _():
        o_ref[...] = jax.nn.gelu(acc).astype(BF16)


def _inproj(x2, w_in):
    t, d = x2.shape
    n = w_in.shape[1]
    n_q = (ATT_HEADS * 2 * ATT_QK_DIM) // TN_IN
    n_lin = (2 * ATT_HEADS * 2 * ATT_QK_DIM + ATT_WIDTH) // TN_IN
    kern = functools.partial(_inproj_kernel, n_q=n_q, n_lin=n_lin,
                             q_scale=(ATT_QK_DIM ** -0.5) * LOG2E)
    return pl.pallas_call(
        kern,
        grid=(t // TM_IN, n // TN_IN),
        in_specs=[pl.BlockSpec((TM_IN, d), lambda i, j: (i, 0)),
                  pl.BlockSpec((d, TN_IN), lambda i, j: (0, j))],
        out_specs=pl.BlockSpec((TM_IN, TN_IN), lambda i, j: (i, j)),
        out_shape=jax.ShapeDtypeStruct((t, n), BF16),
        scratch_shapes=[pltpu.VMEM((TM_IN, d), BF16)],
        compiler_params=_cparams(("arbitrary", "arbitrary")),
        name="inproj",
    )(x2, w_in)


def _attn_kernel(lam_ref, q_ref, k_ref, v_ref, bseq_ref, g_ref, *rest, n_kc, out_scale):
    n_side = (len(rest) - 3) // 2
    side_in, o_ref, side_out = rest[:n_side], rest[n_side], rest[n_side + 1:2 * n_side + 1]
    vaug_ref, bias_ref = rest[2 * n_side + 1:]
    qi = pl.program_id(2)

    @pl.when(qi == 0)
    def _():
        vaug_ref[:, :ATT_V_DIM] = v_ref[0]
        vaug_ref[:, ATT_V_DIM:] = jnp.ones((vaug_ref.shape[0], ATT_V_DIM), BF16)
        for t in range(N_BIAS_TILES):
            seq = jnp.broadcast_to(bseq_ref[0, t:t + 1, :], (QB, 2 * KC))
            bias_ref[t] = pltpu.roll(seq, 0, 1, stride=1, stride_axis=0)[:, :KC]

    q = q_ref[0]
    lane = lax.broadcasted_iota(I32, q.shape, 1)
    zero = jnp.zeros_like(q)
    qq = jnp.concatenate([jnp.where(lane < ATT_QK_DIM, q, zero),
                          jnp.where(lane >= ATT_QK_DIM, q, zero)], axis=0)

    for src, dst in zip(side_in, side_out):
        dst[...] = src[...].astype(dst.dtype)

    m = jnp.full((2 * QB, 1), -jnp.inf, F32)
    acc = jnp.zeros((2 * QB, 2 * ATT_V_DIM), F32)
    for kc in range(n_kc):
        kblk = k_ref[0, kc * KC:(kc + 1) * KC, :]
        s = lax.dot_general(qq, kblk, (((1,), (1,)), ((), ())),
                            preferred_element_type=F32)
        b = bias_ref[jnp.clip(kc - qi, -2, 2) + 2]
        s = s + jnp.concatenate([b, b], axis=0)
        m_new = jnp.maximum(m, jnp.max(s, axis=1, keepdims=True))
        p = jnp.exp2(s - m_new).astype(BF16)
        acc = acc * jnp.exp2(m - m_new) + jnp.dot(
            p, vaug_ref[kc * KC:(kc + 1) * KC, :], preferred_element_type=F32)
        m = m_new
    lam = lam_ref[0]
    o0 = acc[:QB, :ATT_V_DIM] / acc[:QB, ATT_V_DIM:ATT_V_DIM + 1]
    o1 = acc[QB:, :ATT_V_DIM] / acc[QB:, ATT_V_DIM:ATT_V_DIM + 1]
    of = o0 - lam * o1
    of = of * lax.rsqrt(jnp.mean(of * of, axis=-1, keepdims=True) + LN_EPS)
    o_ref[0] = (of * (g_ref[...] * out_scale)).astype(o_ref.dtype)


def _attention(proj3, bias_seq, lam, subln_g, lam_init, side):
    b, s, _ = proj3.shape
    n_kc = s // KC
    n_q = s // QB
    n_steps = b * ATT_HEADS * n_q
    h_off_k = ATT_HEADS
    h_off_v = 2 * ATT_HEADS
    kern = functools.partial(_attn_kernel, n_kc=n_kc, out_scale=1.0 - lam_init)

    def side_map(bi, h, qi):
        return ((bi * ATT_HEADS + h) * n_q + qi, 0)

    side_specs = []
    for arr in side:
        rows, cols = arr.shape
        assert rows % (n_steps * 2 * SUBLANES) == 0
        side_specs.append(pl.BlockSpec((rows // n_steps, cols), side_map))
    return pl.pallas_call(
        kern,
        grid=(b, ATT_HEADS, n_q),
        in_specs=[
            pl.BlockSpec(memory_space=pltpu.SMEM),
            pl.BlockSpec((1, QB, LANES), lambda bi, h, qi: (bi, qi, h)),
            pl.BlockSpec((1, s, LANES), lambda bi, h, qi: (bi, 0, h_off_k + h)),
            pl.BlockSpec((1, s, LANES), lambda bi, h, qi: (bi, 0, h_off_v + h)),
            pl.BlockSpec((1, N_BIAS_TILES, 2 * KC), lambda bi, h, qi: (h, 0, 0)),
            pl.BlockSpec((1, ATT_V_DIM), lambda bi, h, qi: (0, 0)),
        ] + side_specs,
        out_specs=[pl.BlockSpec((1, QB, ATT_V_DIM), lambda bi, h, qi: (bi, qi, h))] + side_specs,
        out_shape=[jax.ShapeDtypeStruct((b, s, ATT_WIDTH), BF16)]
        + [jax.ShapeDtypeStruct(arr.shape, BF16) for arr in side],
        scratch_shapes=[pltpu.VMEM((s, 2 * ATT_V_DIM), BF16),
                        pltpu.VMEM((N_BIAS_TILES, QB, KC), F32)],
        compiler_params=_cparams(("arbitrary", "arbitrary", "arbitrary")),
        name="attn",
    )(lam, proj3, proj3, proj3, bias_seq, subln_g, *side)


def _rel_bucket(rel):
    half = REL_BUCKETS // 2
    max_exact = half // 2
    ret = jnp.where(rel > 0, half, 0)
    n = jnp.abs(rel)
    nf = jnp.maximum(n, 1).astype(F32)
    large = max_exact + (jnp.log(nf / max_exact) / math.log(REL_MAX_DIST / max_exact)
                         * (half - max_exact)).astype(I32)
    large = jnp.minimum(large, half - 1)
    return ret + jnp.where(n < max_exact, n, large)


def _bias_sequences(rel_bias, s):
    assert REL_MAX_DIST <= KC and QB == KC
    period = 2 * KC
    j = jnp.arange(period, dtype=I32)
    off = jnp.where(j < KC, j, j - period)
    d = jnp.arange(-2, 3, dtype=I32)[:, None]
    rel = jnp.clip(d * KC + off[None, :], -(s - 1), s - 1)
    return jnp.transpose(rel_bias[_rel_bucket(rel)], (2, 0, 1)).astype(F32) * LOG2E


def _layernorm(v, g, b):
    mu = jnp.mean(v, axis=-1, keepdims=True)
    var = jnp.mean(jnp.square(v - mu), axis=-1, keepdims=True)
    return (v - mu) * lax.rsqrt(var + LN_EPS) * g + b


def _pack_bf16_pair(v):
    n = v.shape[1] // 2
    bits = lax.bitcast_convert_type(v.astype(BF16).astype(F32), U32)
    return (bits[:, :n] >> 16) | (bits[:, n:] & jnp.uint32(0xFFFF0000))


def _unpack_bf16_pair(w):
    lo = lax.bitcast_convert_type(w << 16, F32)
    hi = lax.bitcast_convert_type(w & jnp.uint32(0xFFFF0000), F32)
    return lo, hi


def _first_argmax(vals, nrows):
    mx = jnp.max(vals, axis=0, keepdims=True)
    row = lax.broadcasted_iota(I32, vals.shape, 0)
    idx = jnp.min(jnp.where(vals == mx, row, nrows), axis=0, keepdims=True)
    return mx, idx


def _mix_kernel(a_ref, u_ref, vg_ref, x_ref, wout_ref, sgw_ref, sgb_ref, sgg_ref, sgbeta_ref,
                g1_ref, b1_ref, wr_ref, br_ref,
                h1_ref, xs_hbm, ri_ref, rw_ref, cnt_ref,
                run_ref, hp_s, posv_ref, psm, ssem, rsem):
    i = pl.program_id(0)
    n = pl.num_programs(0)
    slot = i % 2
    prev = 1 - slot

    def pos_copy(sl):
        return pltpu.make_async_copy(posv_ref, psm.at[sl], ssem.at[sl])

    def row_copy(sl, which, j):
        return pltpu.make_async_copy(hp_s.at[sl, pl.ds(j, 1)],
                                     xs_hbm.at[pl.ds(psm[sl, which, j], 1)], rsem.at[sl])

    def drain(sl):
        def wait(j, c):
            row_copy(sl, 0, 0).wait()
            row_copy(sl, 1, 0).wait()
            return c
        lax.fori_loop(0, TM_MIX, wait, 0, unroll=8)

    @pl.when(i == 0)
    def _():
        run_ref[...] = jnp.zeros_like(run_ref)
        hp_s[1] = jnp.zeros(hp_s.shape[1:], hp_s.dtype)
        spare = N_EXPERTS * XS_CAP + lax.broadcasted_iota(I32, posv_ref.shape, 1)
        spare = spare + TM_MIX * lax.broadcasted_iota(I32, posv_ref.shape, 0)
        posv_ref[...] = jnp.where(spare < xs_hbm.shape[0], spare, 0)
        pos_copy(1).start()

    @pl.when(i > 0)
    def _():
        drain(slot)

    pos_copy(prev).wait()

    for j in range(TM_MIX):
        row_copy(prev, 0, j).start(priority=0)
        row_copy(prev, 1, j).start(priority=1)

    h1_ref[...] = jnp.dot(a_ref[...], wout_ref[:ATT_WIDTH, :], preferred_element_type=F32)

    n_chunks = TM_MIX // SG_CHUNK
    rows = []
    for c in range(n_chunks):
        cols = []
        for g in range(SG_GROUPS):
            rs = slice(c * SG_CHUNK, (c + 1) * SG_CHUNK)
            cs = slice(g * SG_DIM, (g + 1) * SG_DIM)
            vt = vg_ref[rs, cs].astype(F32)
            vn = _layernorm(vt, sgg_ref[g:g + 1, :], sgbeta_ref[g:g + 1, :])
            mixed = jnp.dot(sgw_ref[g], vn.astype(BF16), preferred_element_type=F32) + sgb_ref[g]
            cols.append((u_ref[rs, cs].astype(F32) * mixed).astype(BF16))
        rows.append(jnp.concatenate(cols, axis=1))
    sgate = jnp.concatenate(rows, axis=0)

    mix = h1_ref[...] + jnp.dot(sgate, wout_ref[ATT_WIDTH:, :], preferred_element_type=F32)
    h1 = _layernorm(DN_ALPHA * x_ref[...] + mix, g1_ref[...], b1_ref[...])
    h1_ref[...] = h1
    hp_s[slot] = _pack_bf16_pair(h1)

    nt = (((1,), (1,)), ((), ()))
    h1_hi = h1.astype(BF16)
    h1_lo = (h1 - h1_hi.astype(F32)).astype(BF16)
    both = lax.dot_general(wr_ref[...], h1_hi, nt, preferred_element_type=F32)
    lt = both[:NR_PAD] + both[NR_PAD:] + lax.dot_general(
        wr_ref[:NR_PAD, :], h1_lo, nt, preferred_element_type=F32)
    lt = lt + br_ref[...]
    g_logits = lt[0:N_GROUPS]
    gmax, g_idx = _first_argmax(g_logits, N_GROUPS)
    g_gate = 1.0 / jnp.sum(jnp.exp(g_logits - gmax), axis=0, keepdims=True)
    e_logits = jnp.zeros((EXPERTS_PER_GROUP, TM_MIX), F32)
    for g in range(N_GROUPS):
        lo = NR_EXPERT0 + g * EXPERTS_PER_GROUP
        e_logits = jnp.where(g_idx == g, lt[lo:lo + EXPERTS_PER_GROUP], e_logits)
    v1, i1 = _first_argmax(e_logits, EXPERTS_PER_GROUP)
    erow = lax.broadcasted_iota(I32, e_logits.shape, 0)
    v2, i2 = _first_argmax(jnp.where(erow == i1, -jnp.inf, e_logits), EXPERTS_PER_GROUP)
    t = jnp.exp(v2 - v1)
    w1 = g_gate / (1.0 + t)
    w2 = g_gate * t / (1.0 + t)
    e1 = g_idx * EXPERTS_PER_GROUP + i1
    e2 = g_idx * EXPERTS_PER_GROUP + i2

    xrow = lax.broadcasted_iota(I32, (N_EXPERTS, TM_MIX), 0)
    oh1 = xrow == e1
    oh2 = xrow == e2
    oh = jnp.where(jnp.logical_or(oh1, oh2), 1.0, 0.0)
    tr = lax.broadcasted_iota(I32, (TM_MIX, TM_MIX), 0)
    tc = lax.broadcasted_iota(I32, (TM_MIX, TM_MIX), 1)
    before = jnp.where(tr < tc, 1.0, 0.0).astype(BF16)
    cum = jnp.dot(oh.astype(BF16), before, preferred_element_type=F32)
    base = cum + run_ref[:, 0:1]
    rank1 = jnp.sum(jnp.where(oh1, base, 0.0), axis=0, keepdims=True).astype(I32)
    rank2 = jnp.sum(jnp.where(oh2, base, 0.0), axis=0, keepdims=True).astype(I32)
    run_ref[...] = run_ref[...] + jnp.sum(oh, axis=1, keepdims=True)
    cnt_ref[...] = run_ref[...].astype(I32)

    zi = jnp.zeros((ROUTE_ROWS - 4, TM_MIX), I32)
    ri_ref[...] = jnp.concatenate([e1, e2, rank1, rank2, zi], axis=0)
    zw = jnp.zeros((ROUTE_ROWS - 2, TM_MIX), F32)
    rw_ref[...] = jnp.concatenate([w1, w2, zw], axis=0)

    posv_ref[...] = jnp.concatenate([e1 * XS_CAP + rank1, e2 * XS_CAP + rank2, zi, zi[:2]], axis=0)
    pos_copy(slot).start()

    @pl.when(i == n - 1)
    def _():
        pos_copy(slot).wait()

        def last(j, c):
            row_copy(slot, 0, j).start(priority=0)
            row_copy(slot, 1, j).start(priority=1)
            return c
        lax.fori_loop(0, TM_MIX, last, 0, unroll=8)
        drain(prev)
        drain(slot)


def _mix(a2, proj2, x2, wout_b, sgw_b, sgb, sgg, sgbeta, g1, b1, wr, br):
    t, d = x2.shape
    assert XS_CAP >= t + TMX
    u_blk = (proj2.shape[1] - 2 * SG_WIDTH) // SG_WIDTH
    const = lambda *shape: pl.BlockSpec(shape, lambda i: (0,) * len(shape))
    return pl.pallas_call(
        _mix_kernel,
        grid=(t // TM_MIX,),
        in_specs=[
            pl.BlockSpec((TM_MIX, ATT_WIDTH), lambda i: (i, 0)),
            pl.BlockSpec((TM_MIX, SG_WIDTH), lambda i: (i, u_blk)),
            pl.BlockSpec((TM_MIX, SG_WIDTH), lambda i: (i, u_blk + 1)),
            pl.BlockSpec((TM_MIX, d), lambda i: (i, 0)),
            pl.BlockSpec((d, d), lambda i: (0, 0), pipeline_mode=pl.Buffered(1)),
            const(SG_GROUPS, SG_CHUNK, SG_CHUNK),
            const(SG_GROUPS, SG_CHUNK, 1),
            const(SG_GROUPS, SG_DIM),
            const(SG_GROUPS, SG_DIM),
            const(1, d),
            const(1, d),
            const(2 * NR_PAD, d),
            const(NR_PAD, 1),
        ],
        out_specs=[
            pl.BlockSpec((TM_MIX, d), lambda i: (i, 0)),
            pl.BlockSpec(memory_space=pl.ANY),
            pl.BlockSpec((ROUTE_ROWS, TM_MIX), lambda i: (0, i)),
            pl.BlockSpec((ROUTE_ROWS, TM_MIX), lambda i: (0, i)),
            const(N_EXPERTS, LANES),
        ],
        out_shape=[
            jax.ShapeDtypeStruct((t, d), F32),
            jax.ShapeDtypeStruct((N_EXPERTS * XS_CAP + 2 * TM_MIX, d // 2), U32),
            jax.ShapeDtypeStruct((ROUTE_ROWS, t), I32),
            jax.ShapeDtypeStruct((ROUTE_ROWS, t), F32),
            jax.ShapeDtypeStruct((N_EXPERTS, LANES), I32),
        ],
        scratch_shapes=[pltpu.VMEM((N_EXPERTS, LANES), F32),
                        pltpu.VMEM((2, TM_MIX, d // 2), U32),
                        pltpu.VMEM((ROUTE_ROWS, TM_MIX), I32),
                        pltpu.SMEM((2, ROUTE_ROWS, TM_MIX), I32),
                        pltpu.SemaphoreType.DMA((2,)),
                        pltpu.SemaphoreType.DMA((2,))],
        compiler_params=_cparams(("arbitrary",)),
        name="mix",
    )(a2, proj2, proj2, x2, wout_b, sgw_b, sgb, sgg, sgbeta, g1, b1, wr, br)


def _experts_kernel(ts_ref, nt_ref, nu_ref, trow_ref, nval_ref, x_hbm, wg_ref, wu_ref, wd_ref,
                    y_hbm, xbuf, ybuf, xsem, ysem):
    e = pl.program_id(0)
    d = wg_ref.shape[1]
    n_used = nu_ref[0]

    def x_copy(g):
        rows = pl.ds(pl.multiple_of(trow_ref[g], TMX), TMX)
        slot = g % N_XSLOTS
        return pltpu.make_async_copy(x_hbm.at[rows], xbuf.at[slot], xsem.at[slot])

    def y_copy(g):
        rows = pl.ds(pl.multiple_of(g * TMX, TMX), TMX)
        slot = g % N_XSLOTS
        return pltpu.make_async_copy(ybuf.at[slot], y_hbm.at[rows], ysem.at[slot])

    def start_x(g):
        @pl.when(g < n_used)
        def _():
            x_copy(g).start(priority=1)

    def free_y(g):
        @pl.when(g >= N_XSLOTS)
        def _():
            y_copy(g - N_XSLOTS).wait()

    def load(g):
        row = lax.broadcasted_iota(I32, xbuf.shape[1:], 0)
        xw = jnp.where(row < nval_ref[g], xbuf[g % N_XSLOTS], jnp.uint32(0))
        lo, hi = _unpack_bf16_pair(xw)
        return lo.astype(BF16), hi.astype(BF16)

    def mlp(lo, hi):
        gate = jnp.dot(lo, wg_ref[0, :d // 2, :], preferred_element_type=F32)
        gate = gate + jnp.dot(hi, wg_ref[0, d // 2:, :], preferred_element_type=F32)
        up = jnp.dot(lo, wu_ref[0, :d // 2, :], preferred_element_type=F32)
        up = up + jnp.dot(hi, wu_ref[0, d // 2:, :], preferred_element_type=F32)
        hmid = (gate * jax.nn.sigmoid(gate) * up).astype(BF16)
        return _pack_bf16_pair(jnp.dot(hmid, wd_ref[0], preferred_element_type=F32))

    @pl.when(e == 0)
    def _():
        for g in range(X_AHEAD):
            start_x(g)

    def group(k, first):
        def body(t, carry):
            g = first(t)
            for j in range(k):
                x_copy(g + j).wait()
            for j in range(k):
                start_x(g + X_AHEAD + j)
                free_y(g + j)
            parts = [load(g + j) for j in range(k)]
            y = mlp(jnp.concatenate([p[0] for p in parts], axis=0),
                    jnp.concatenate([p[1] for p in parts], axis=0))
            for j in range(k):
                ybuf[(g + j) % N_XSLOTS] = y[j * TMX:(j + 1) * TMX]
                y_copy(g + j).start(priority=1)
            return carry
        return body

    nt = nt_ref[e]
    n3 = jnp.where(jnp.logical_and(nt >= 3, nt % 2 == 1), 1, 0)
    n2 = (nt - 3 * n3) // 2
    n1 = nt - 3 * n3 - 2 * n2
    base = ts_ref[e]
    lax.fori_loop(0, n3, group(3, lambda t: base), 0)
    lax.fori_loop(0, n2, group(2, lambda t: base + 3 * n3 + 2 * t), 0)
    lax.fori_loop(0, n1, group(1, lambda t: base + 3 * n3 + 2 * n2), 0)

    @pl.when(e == pl.num_programs(0) - 1)
    def _():
        for back in range(N_XSLOTS, 0, -1):
            @pl.when(n_used >= back)
            def _():
                y_copy(n_used - back).wait()


def _experts(tile_start, n_tiles_e, n_used, tile_row, tile_valid, xs, w_gate, w_up, w_down):
    w = xs.shape[1]
    d = w_gate.shape[1]
    n_rows = tile_row.shape[0] * TMX

    def w_map(e, ts, nt, nu, trow, nval):
        return (e, 0, 0)

    grid_spec = pltpu.PrefetchScalarGridSpec(
        num_scalar_prefetch=5,
        grid=(N_EXPERTS,),
        in_specs=[
            pl.BlockSpec(memory_space=pl.ANY),
            pl.BlockSpec((1, d, D_EXPERT), w_map),
            pl.BlockSpec((1, d, D_EXPERT), w_map),
            pl.BlockSpec((1, D_EXPERT, d), w_map),
        ],
        out_specs=pl.BlockSpec(memory_space=pl.ANY),
        scratch_shapes=[pltpu.VMEM((N_XSLOTS, TMX, w), U32),
                        pltpu.VMEM((N_XSLOTS, TMX, w), U32),
                        pltpu.SemaphoreType.DMA((N_XSLOTS,)),
                        pltpu.SemaphoreType.DMA((N_XSLOTS,))],
    )
    return pl.pallas_call(
        _experts_kernel,
        grid_spec=grid_spec,
        out_shape=jax.ShapeDtypeStruct((n_rows, w), U32),
        compiler_params=_cparams(("arbitrary",)),
        name="experts",
    )(tile_start, n_tiles_e, n_used, tile_row, tile_valid, xs, w_gate, w_up, w_down)


def _combine_kernel(pos_ref, posn_ref, y_hbm, h1_ref, wt_ref, g2_ref, b2_ref, o_ref, ybuf, sem):
    i = pl.program_id(0)
    n = pl.num_programs(0)
    slot = i % 2

    def row_copy(src, sl, which, j):
        return pltpu.make_async_copy(y_hbm.at[pl.ds(src, 1)],
                                     ybuf.at[sl, which, pl.ds(j, 1)], sem.at[sl])

    def issue(p_ref, sl):
        def body(j, c):
            row_copy(p_ref[0, j], sl, 0, j).start(priority=0)
            row_copy(p_ref[1, j], sl, 1, j).start(priority=1)
            return c
        lax.fori_loop(0, TM_OUT, body, 0, unroll=8)

    def drain(sl):
        def wait(j, c):
            row_copy(0, sl, 0, 0).wait()
            row_copy(0, sl, 1, 0).wait()
            return c
        lax.fori_loop(0, TM_OUT, wait, 0, unroll=8)

    @pl.when(i == 0)
    def _():
        issue(pos_ref, 0)

    drain(slot)

    for j in range(TM_OUT):
        row_copy(posn_ref[0, j], 1 - slot, 0, j).start(priority=0)
        row_copy(posn_ref[1, j], 1 - slot, 1, j).start(priority=1)

    lo1, hi1 = _unpack_bf16_pair(ybuf[slot, 0])
    lo2, hi2 = _unpack_bf16_pair(ybuf[slot, 1])
    w1 = wt_ref[:, 0:1]
    w2 = wt_ref[:, 1:2]
    ffn = jnp.concatenate([w1 * lo1 + w2 * lo2, w1 * hi1 + w2 * hi2], axis=1)
    o_ref[...] = _layernorm(DN_ALPHA * h1_ref[...] + ffn, g2_ref[...], b2_ref[...])

    @pl.when(i == n - 1)
    def _():
        drain(1 - slot)


def _combine(pos3, ys, h1, wtok, g2, b2):
    t, d = h1.shape
    w = ys.shape[1]
    n = t // TM_OUT
    return pl.pallas_call(
        _combine_kernel,
        grid=(n,),
        in_specs=[
            pl.BlockSpec((None, 2, TM_OUT), lambda i: (i, 0, 0), memory_space=pltpu.SMEM),
            pl.BlockSpec((None, 2, TM_OUT), lambda i: (jnp.minimum(i + 1, n - 1), 0, 0),
                         memory_space=pltpu.SMEM),
            pl.BlockSpec(memory_space=pl.ANY),
            pl.BlockSpec((TM_OUT, d), lambda i: (i, 0)),
            pl.BlockSpec((TM_OUT, ROUTE_ROWS), lambda i: (i, 0)),
            pl.BlockSpec((1, d), lambda i: (0, 0)),
            pl.BlockSpec((1, d), lambda i: (0, 0)),
        ],
        out_specs=pl.BlockSpec((TM_OUT, d), lambda i: (i, 0)),
        out_shape=jax.ShapeDtypeStruct((t, d), F32),
        scratch_shapes=[pltpu.VMEM((2, 2, TM_OUT, w), U32),
                        pltpu.SemaphoreType.DMA((2,))],
        compiler_params=_cparams(("arbitrary",)),
        name="combine",
    )(pos3, pos3, ys, h1, wtok, g2, b2)


def _lambda_init(layer):
    return 0.8 - 0.6 * math.exp(-0.3 * layer)


def kernel(x, w_in, w_out, ln1_g, ln1_b, ln2_g, ln2_b, rel_bias, lam_q1, lam_k1, lam_q2, lam_k2,
           subln_g, sg_ln_g, sg_ln_b, sg_w, sg_b, w_router_group, b_router_group,
           w_router_expert, b_router_expert, w_exp_gate, w_exp_up, w_exp_down):
    b, s, d = x.shape
    t = b * s
    assert w_in.shape[0] == DEPTH == 1
    l = 0
    lam_init = _lambda_init(l)
    x2 = x.reshape(t, d)

    proj = _inproj(x2, w_in[l])

    lam = (jnp.exp(jnp.sum(lam_q1[l].astype(F32) * lam_k1[l].astype(F32)))
           - jnp.exp(jnp.sum(lam_q2[l].astype(F32) * lam_k2[l].astype(F32))) + lam_init)
    side = [w_exp_gate[l].reshape(-1, D_EXPERT), w_exp_up[l].reshape(-1, D_EXPERT),
            w_exp_down[l].reshape(-1, d)]
    a, wg_b, wu_b, wd_b = _attention(
        proj.reshape(b, s, -1), _bias_sequences(rel_bias, s), lam.reshape(1),
        subln_g[l].reshape(1, -1), lam_init, side)

    wr = jnp.concatenate([w_router_group[l].T,
                          jnp.zeros((NR_EXPERT0 - N_GROUPS, d), F32),
                          jnp.transpose(w_router_expert[l], (0, 2, 1)).reshape(N_EXPERTS, d)],
                         axis=0)
    br = jnp.concatenate([b_router_group[l], jnp.zeros((NR_EXPERT0 - N_GROUPS,), F32),
                          b_router_expert[l].reshape(-1)]).reshape(NR_PAD, 1)
    wr_hi = wr.astype(BF16)
    wr = jnp.concatenate([wr_hi, (wr - wr_hi.astype(F32)).astype(BF16)], axis=0)
    h1, xs, ri, rw, cnt = _mix(
        a.reshape(t, -1), proj, x2, w_out[l].astype(BF16), sg_w[l].astype(BF16),
        sg_b[l].reshape(SG_GROUPS, SG_CHUNK, 1), sg_ln_g[l], sg_ln_b[l],
        ln1_g[l].reshape(1, d), ln1_b[l].reshape(1, d), wr, br)

    counts = cnt[:, 0]
    tiles_e = (counts + TMX - 1) // TMX
    tile_end = jnp.cumsum(tiles_e)
    tile_start = tile_end - tiles_e
    n_used = tile_end[-1:]
    max_tiles = (2 * t) // TMX + N_EXPERTS
    row_start = (tile_start * TMX).astype(I32)
    eids = jnp.arange(N_EXPERTS, dtype=I32)[:, None, None]
    pos = jnp.sum(jnp.where(ri[0:2][None] == eids, row_start[:, None, None], 0), axis=0) + ri[2:4]
    pos3 = pos.reshape(2, t // TM_OUT, TM_OUT).transpose(1, 0, 2)
    tile_ids = jnp.arange(max_tiles, dtype=I32)
    tile_e = jnp.minimum(jnp.searchsorted(tile_end, tile_ids, side="right"), N_EXPERTS - 1)
    local = tile_ids - tile_start[tile_e]
    in_use = tile_ids < n_used[0]
    tile_row = jnp.where(in_use, tile_e * XS_CAP + local * TMX, 0).astype(I32)
    tile_valid = jnp.where(in_use, jnp.clip(counts[tile_e] - local * TMX, 0, TMX), 0).astype(I32)

    ys = _experts(tile_start.astype(I32), tiles_e.astype(I32), n_used.astype(I32),
                  tile_row, tile_valid, xs,
                  wg_b.reshape(N_EXPERTS, d, D_EXPERT), wu_b.reshape(N_EXPERTS, d, D_EXPERT),
                  wd_b.reshape(N_EXPERTS, D_EXPERT, d))
    out = _combine(pos3, ys, h1, rw.T, ln2_g[l].reshape(1, d), ln2_b[l].reshape(1, d))
    return out.reshape(b, s, d)
```

```python
import functools
import math

import jax
import jax.numpy as jnp
from jax import lax
from jax.experimental import pallas as pl
from jax.experimental.pallas import tpu as pltpu

F32 = jnp.float32
BF16 = jnp.bfloat16
U32 = jnp.uint32
I32 = jnp.int32

ATT_HEADS = 8
ATT_QK_DIM = 64
ATT_V_DIM = 128
ATT_WIDTH = ATT_HEADS * ATT_V_DIM
SG_GROUPS = 8
SG_DIM = 128
SG_WIDTH = SG_GROUPS * SG_DIM
SG_CHUNK = 128
REL_BUCKETS = 32
REL_MAX_DIST = 128
N_GROUPS = 4
EXPERTS_PER_GROUP = 8
N_EXPERTS = N_GROUPS * EXPERTS_PER_GROUP
D_EXPERT = 256
DEPTH = 1
DN_ALPHA = (2.0 * DEPTH) ** 0.25
LN_EPS = 1e-5
LOG2E = math.log2(math.e)

LANES = 128
SUBLANES = 8
VMEM_LIMIT_BYTES = 56 * 1024 * 1024

TM_IN = 1024
TN_IN = 1024
QB = 256
KC = 256
N_BIAS_TILES = 5
TM_MIX = 512
TMX = 256
TM_OUT = 256
X_AHEAD = 3
N_XSLOTS = 8
ROUTE_ROWS = 8
XS_CAP = 8192 + TMX
NR_EXPERT0 = 8
NR_PAD = NR_EXPERT0 + 32


def _cparams(sem, flags=None):
    return pltpu.CompilerParams(dimension_semantics=sem, vmem_limit_bytes=VMEM_LIMIT_BYTES,
                                flags=flags)


def _inproj_kernel(x_ref, w_ref, o_ref, xb_ref, *, n_q, q_scale, gelu):
    j = pl.program_id(1)

    @pl.when(j == 0)
    def _():
        xb_ref[...] = x_ref[...].astype(BF16)

    acc = jnp.dot(xb_ref[...], w_ref[...].astype(BF16), preferred_element_type=F32)
    if gelu:
        o_ref[...] = jax.nn.gelu(acc).astype(BF16)
    else:
        o_ref[...] = (acc * jnp.where(j < n_q, q_scale, 1.0)).astype(BF16)


def _inproj(x2, w_in, col_lo, col_hi, *, gelu, name):
    t, d = x2.shape
    assert col_lo % TN_IN == 0 and col_hi % TN_IN == 0
    j0 = col_lo // TN_IN
    n_q = 0 if gelu else (ATT_HEADS * 2 * ATT_QK_DIM) // TN_IN
    kern = functools.partial(_inproj_kernel, n_q=n_q, gelu=gelu,
                             q_scale=(ATT_QK_DIM ** -0.5) * LOG2E)
    return pl.pallas_call(
        kern,
        grid=(t // TM_IN, (col_hi - col_lo) // TN_IN),
        in_specs=[pl.BlockSpec((TM_IN, d), lambda i, j: (i, 0)),
                  pl.BlockSpec((d, TN_IN), lambda i, j: (0, j0 + j))],
        out_specs=pl.BlockSpec((TM_IN, TN_IN), lambda i, j: (i, j)),
        out_shape=jax.ShapeDtypeStruct((t, col_hi - col_lo), BF16),
        scratch_shapes=[pltpu.VMEM((TM_IN, d), BF16)],
        compiler_params=_cparams(("arbitrary", "arbitrary")),
        name=name,
    )(x2, w_in)


def _attn_kernel(lam_ref, q_ref, k_ref, v_ref, bseq_ref, g_ref, *rest, n_kc, out_scale):
    n_side = (len(rest) - 3) // 2
    side_in, o_ref, side_out = rest[:n_side], rest[n_side], rest[n_side + 1:2 * n_side + 1]
    vaug_ref, bias_ref = rest[2 * n_side + 1:]
    qi = pl.program_id(2)

    @pl.when(qi == 0)
    def _():
        vaug_ref[:, :ATT_V_DIM] = v_ref[0]
        vaug_ref[:, ATT_V_DIM:] = jnp.ones((vaug_ref.shape[0], ATT_V_DIM), BF16)
        for t in range(N_BIAS_TILES):
            seq = jnp.broadcast_to(bseq_ref[0, t:t + 1, :], (QB, 2 * KC))
            bias_ref[t] = pltpu.roll(seq, 0, 1, stride=1, stride_axis=0)[:, :KC]

    q = q_ref[0]
    lane = lax.broadcasted_iota(I32, q.shape, 1)
    zero = jnp.zeros_like(q)
    qq = jnp.concatenate([jnp.where(lane < ATT_QK_DIM, q, zero),
                          jnp.where(lane >= ATT_QK_DIM, q, zero)], axis=0)

    for src, dst in zip(side_in, side_out):
        dst[...] = src[...].astype(dst.dtype)

    m = jnp.full((2 * QB, 1), -jnp.inf, F32)
    acc = jnp.zeros((2 * QB, 2 * ATT_V_DIM), F32)
    for kc in range(n_kc):
        kblk = k_ref[0, kc * KC:(kc + 1) * KC, :]
        s = lax.dot_general(qq, kblk, (((1,), (1,)), ((), ())),
                            preferred_element_type=F32)
        b = bias_ref[jnp.clip(kc - qi, -2, 2) + 2]
        s = s + jnp.concatenate([b, b], axis=0)
        m_new = jnp.maximum(m, jnp.max(s, axis=1, keepdims=True))
        p = jnp.exp2(s - m_new).astype(BF16)
        acc = acc * jnp.exp2(m - m_new) + jnp.dot(
            p, vaug_ref[kc * KC:(kc + 1) * KC, :], preferred_element_type=F32)
        m = m_new
    lam = lam_ref[0]
    o0 = acc[:QB, :ATT_V_DIM] / acc[:QB, ATT_V_DIM:ATT_V_DIM + 1]
    o1 = acc[QB:, :ATT_V_DIM] / acc[QB:, ATT_V_DIM:ATT_V_DIM + 1]
    of = o0 - lam * o1
    of = of * lax.rsqrt(jnp.mean(of * of, axis=-1, keepdims=True) + LN_EPS)
    o_ref[0] = (of * (g_ref[...] * out_scale)).astype(o_ref.dtype)


def _attention(proj3, bias_seq, lam, subln_g, lam_init, side):
    b, s, _ = proj3.shape
    n_kc = s // KC
    n_q = s // QB
    n_steps = b * ATT_HEADS * n_q
    h_off_k = ATT_HEADS
    h_off_v = 2 * ATT_HEADS
    kern = functools.partial(_attn_kernel, n_kc=n_kc, out_scale=1.0 - lam_init)

    side_specs = []
    for arr in side:
        rows, cols = arr.shape
        blk = max(rows // n_steps, 2 * SUBLANES)
        hold = blk * n_steps // rows
        assert rows % blk == 0 and hold * rows == blk * n_steps
        side_specs.append(pl.BlockSpec(
            (blk, cols),
            lambda bi, h, qi, hold=hold: (((bi * ATT_HEADS + h) * n_q + qi) // hold, 0)))
    return pl.pallas_call(
        kern,
        grid=(b, ATT_HEADS, n_q),
        in_specs=[
            pl.BlockSpec(memory_space=pltpu.SMEM),
            pl.BlockSpec((1, QB, LANES), lambda bi, h, qi: (bi, qi, h)),
            pl.BlockSpec((1, s, LANES), lambda bi, h, qi: (bi, 0, h_off_k + h)),
            pl.BlockSpec((1, s, LANES), lambda bi, h, qi: (bi, 0, h_off_v + h)),
            pl.BlockSpec((1, N_BIAS_TILES, 2 * KC), lambda bi, h, qi: (h, 0, 0)),
            pl.BlockSpec((1, ATT_V_DIM), lambda bi, h, qi: (0, 0)),
        ] + side_specs,
        out_specs=[pl.BlockSpec((1, QB, ATT_V_DIM), lambda bi, h, qi: (bi, qi, h))] + side_specs,
        out_shape=[jax.ShapeDtypeStruct((b, s, ATT_WIDTH), BF16)]
        + [jax.ShapeDtypeStruct(arr.shape, BF16) for arr in side],
        scratch_shapes=[pltpu.VMEM((s, 2 * ATT_V_DIM), BF16),
                        pltpu.VMEM((N_BIAS_TILES, QB, KC), F32)],
        compiler_params=_cparams(("arbitrary", "arbitrary", "arbitrary")),
        name="attn",
    )(lam, proj3, proj3, proj3, bias_seq, subln_g, *side)


def _rel_bucket(rel):
    half = REL_BUCKETS // 2
    max_exact = half // 2
    ret = jnp.where(rel > 0, half, 0)
    n = jnp.abs(rel)
    nf = jnp.maximum(n, 1).astype(F32)
    large = max_exact + (jnp.log(nf / max_exact) / math.log(REL_MAX_DIST / max_exact)
                         * (half - max_exact)).astype(I32)
    large = jnp.minimum(large, half - 1)
    return ret + jnp.where(n < max_exact, n, large)


def _bias_sequences(rel_bias, s):
    assert REL_MAX_DIST <= KC and QB == KC
    period = 2 * KC
    j = jnp.arange(period, dtype=I32)
    off = jnp.where(j < KC, j, j - period)
    d = jnp.arange(-2, 3, dtype=I32)[:, None]
    rel = jnp.clip(d * KC + off[None, :], -(s - 1), s - 1)
    return jnp.transpose(rel_bias[_rel_bucket(rel)], (2, 0, 1)).astype(F32) * LOG2E


def _layernorm(v, g, b):
    mu = jnp.mean(v, axis=-1, keepdims=True)
    var = jnp.mean(jnp.square(v - mu), axis=-1, keepdims=True)
    return (v - mu) * lax.rsqrt(var + LN_EPS) * g + b


def _pack_bf16_pair(v):
    n = v.shape[1] // 2
    bits = lax.bitcast_convert_type(v.astype(BF16).astype(F32), U32)
    return (bits[:, :n] >> 16) | (bits[:, n:] & jnp.uint32(0xFFFF0000))


def _unpack_bf16_pair(w):
    lo = lax.bitcast_convert_type(w << 16, F32)
    hi = lax.bitcast_convert_type(w & jnp.uint32(0xFFFF0000), F32)
    return lo, hi


def _first_argmax(vals, nrows):
    mx = jnp.max(vals, axis=0, keepdims=True)
    row = lax.broadcasted_iota(I32, vals.shape, 0)
    idx = jnp.min(jnp.where(vals == mx, row, nrows), axis=0, keepdims=True)
    return mx, idx


def _mix_kernel(a_ref, u_ref, vg_ref, x_ref, wout_ref, sgw_ref, sgb_ref, sgg_ref, sgbeta_ref,
                g1_ref, b1_ref, wr_ref, br_ref,
                h1_ref, xs_hbm, ri_ref, rw_ref, cnt_ref,
                run_ref, hp_s, posv_ref, psm, ssem, rsem):
    i = pl.program_id(0)
    n = pl.num_programs(0)
    slot = i % 2
    prev = 1 - slot

    def pos_copy(sl):
        return pltpu.make_async_copy(posv_ref, psm.at[sl], ssem.at[sl])

    def row_copy(sl, which, j):
        return pltpu.make_async_copy(hp_s.at[sl, pl.ds(j, 1)],
                                     xs_hbm.at[pl.ds(psm[sl, which, j], 1)], rsem.at[sl])

    def drain(sl):
        def wait(j, c):
            row_copy(sl, 0, 0).wait()
            row_copy(sl, 1, 0).wait()
            return c
        lax.fori_loop(0, TM_MIX, wait, 0, unroll=8)

    @pl.when(i == 0)
    def _():
        run_ref[...] = jnp.zeros_like(run_ref)
        hp_s[1] = jnp.zeros(hp_s.shape[1:], hp_s.dtype)
        spare = N_EXPERTS * XS_CAP + lax.broadcasted_iota(I32, posv_ref.shape, 1)
        spare = spare + TM_MIX * lax.broadcasted_iota(I32, posv_ref.shape, 0)
        posv_ref[...] = jnp.where(spare < xs_hbm.shape[0], spare, 0)
        pos_copy(1).start()

    @pl.when(i > 0)
    def _():
        drain(slot)

    pos_copy(prev).wait()

    for j in range(TM_MIX):
        row_copy(prev, 0, j).start(priority=0)
        row_copy(prev, 1, j).start(priority=1)

    h1_ref[...] = jnp.dot(a_ref[...], wout_ref[:ATT_WIDTH, :], preferred_element_type=F32)

    n_chunks = TM_MIX // SG_CHUNK
    rows = []
    for c in range(n_chunks):
        cols = []
        for g in range(SG_GROUPS):
            rs = slice(c * SG_CHUNK, (c + 1) * SG_CHUNK)
            cs = slice(g * SG_DIM, (g + 1) * SG_DIM)
            vt = vg_ref[rs, cs].astype(F32)
            vn = _layernorm(vt, sgg_ref[g:g + 1, :], sgbeta_ref[g:g + 1, :])
            mixed = jnp.dot(sgw_ref[g], vn.astype(BF16), preferred_element_type=F32) + sgb_ref[g]
            cols.append((u_ref[rs, cs].astype(F32) * mixed).astype(BF16))
        rows.append(jnp.concatenate(cols, axis=1))
    sgate = jnp.concatenate(rows, axis=0)

    mix = h1_ref[...] + jnp.dot(sgate, wout_ref[ATT_WIDTH:, :], preferred_element_type=F32)
    h1 = _layernorm(DN_ALPHA * x_ref[...] + mix, g1_ref[...], b1_ref[...])
    h1_ref[...] = h1
    hp_s[slot] = _pack_bf16_pair(h1)

    nt = (((1,), (1,)), ((), ()))
    h1_hi = h1.astype(BF16)
    h1_lo = (h1 - h1_hi.astype(F32)).astype(BF16)
    both = lax.dot_general(wr_ref[...], h1_hi, nt, preferred_element_type=F32)
    lt = both[:NR_PAD] + both[NR_PAD:] + lax.dot_general(
        wr_ref[:NR_PAD, :], h1_lo, nt, preferred_element_type=F32)
    lt = lt + br_ref[...]
    g_logits = lt[0:N_GROUPS]
    gmax, g_idx = _first_argmax(g_logits, N_GROUPS)
    g_gate = 1.0 / jnp.sum(jnp.exp(g_logits - gmax), axis=0, keepdims=True)
    e_logits = jnp.zeros((EXPERTS_PER_GROUP, TM_MIX), F32)
    for g in range(N_GROUPS):
        lo = NR_EXPERT0 + g * EXPERTS_PER_GROUP
        e_logits = jnp.where(g_idx == g, lt[lo:lo + EXPERTS_PER_GROUP], e_logits)
    v1, i1 = _first_argmax(e_logits, EXPERTS_PER_GROUP)
    erow = lax.broadcasted_iota(I32, e_logits.shape, 0)
    v2, i2 = _first_argmax(jnp.where(erow == i1, -jnp.inf, e_logits), EXPERTS_PER_GROUP)
    t = jnp.exp(v2 - v1)
    w1 = g_gate / (1.0 + t)
    w2 = g_gate * t / (1.0 + t)
    e1 = g_idx * EXPERTS_PER_GROUP + i1
    e2 = g_idx * EXPERTS_PER_GROUP + i2

    xrow = lax.broadcasted_iota(I32, (N_EXPERTS, TM_MIX), 0)
    oh1 = xrow == e1
    oh2 = xrow == e2
    oh = jnp.where(jnp.logical_or(oh1, oh2), 1.0, 0.0)
    tr = lax.broadcasted_iota(I32, (TM_MIX, TM_MIX), 0)
    tc = lax.broadcasted_iota(I32, (TM_MIX, TM_MIX), 1)
    before = jnp.where(tr < tc, 1.0, 0.0).astype(BF16)
    cum = jnp.dot(oh.astype(BF16), before, preferred_element_type=F32)
    base = cum + run_ref[:, 0:1]
    rank1 = jnp.sum(jnp.where(oh1, base, 0.0), axis=0, keepdims=True).astype(I32)
    rank2 = jnp.sum(jnp.where(oh2, base, 0.0), axis=0, keepdims=True).astype(I32)
    run_ref[...] = run_ref[...] + jnp.sum(oh, axis=1, keepdims=True)
    cnt_ref[...] = run_ref[...].astype(I32)

    zi = jnp.zeros((ROUTE_ROWS - 4, TM_MIX), I32)
    ri_ref[...] = jnp.concatenate([e1, e2, rank1, rank2, zi], axis=0)
    zw = jnp.zeros((ROUTE_ROWS - 2, TM_MIX), F32)
    rw_ref[...] = jnp.concatenate([w1, w2, zw], axis=0)

    posv_ref[...] = jnp.concatenate([e1 * XS_CAP + rank1, e2 * XS_CAP + rank2, zi, zi[:2]], axis=0)
    pos_copy(slot).start()

    @pl.when(i == n - 1)
    def _():
        pos_copy(slot).wait()

        def last(j, c):
            row_copy(slot, 0, j).start(priority=0)
            row_copy(slot, 1, j).start(priority=1)
            return c
        lax.fori_loop(0, TM_MIX, last, 0, unroll=8)
        drain(prev)
        drain(slot)


def _mix(a2, proj2, x2, wout_b, sgw_b, sgb, sgg, sgbeta, g1, b1, wr, br):
    t, d = x2.shape
    assert XS_CAP >= t + TMX
    u_blk = (proj2.shape[1] - 2 * SG_WIDTH) // SG_WIDTH
    const = lambda *shape: pl.BlockSpec(shape, lambda i: (0,) * len(shape))
    return pl.pallas_call(
        _mix_kernel,
        grid=(t // TM_MIX,),
        in_specs=[
            pl.BlockSpec((TM_MIX, ATT_WIDTH), lambda i: (i, 0)),
            pl.BlockSpec((TM_MIX, SG_WIDTH), lambda i: (i, u_blk)),
            pl.BlockSpec((TM_MIX, SG_WIDTH), lambda i: (i, u_blk + 1)),
            pl.BlockSpec((TM_MIX, d), lambda i: (i, 0)),
            pl.BlockSpec((d, d), lambda i: (0, 0), pipeline_mode=pl.Buffered(1)),
            const(SG_GROUPS, SG_CHUNK, SG_CHUNK),
            const(SG_GROUPS, SG_CHUNK, 1),
            const(SG_GROUPS, SG_DIM),
            const(SG_GROUPS, SG_DIM),
            const(1, d),
            const(1, d),
            const(2 * NR_PAD, d),
            const(NR_PAD, 1),
        ],
        out_specs=[
            pl.BlockSpec((TM_MIX, d), lambda i: (i, 0)),
            pl.BlockSpec(memory_space=pl.ANY),
            pl.BlockSpec((ROUTE_ROWS, TM_MIX), lambda i: (0, i)),
            pl.BlockSpec((ROUTE_ROWS, TM_MIX), lambda i: (0, i)),
            const(N_EXPERTS, LANES),
        ],
        out_shape=[
            jax.ShapeDtypeStruct((t, d), F32),
            jax.ShapeDtypeStruct((N_EXPERTS * XS_CAP + 2 * TM_MIX, d // 2), U32),
            jax.ShapeDtypeStruct((ROUTE_ROWS, t), I32),
            jax.ShapeDtypeStruct((ROUTE_ROWS, t), F32),
            jax.ShapeDtypeStruct((N_EXPERTS, LANES), I32),
        ],
        scratch_shapes=[pltpu.VMEM((N_EXPERTS, LANES), F32),
                        pltpu.VMEM((2, TM_MIX, d // 2), U32),
                        pltpu.VMEM((ROUTE_ROWS, TM_MIX), I32),
                        pltpu.SMEM((2, ROUTE_ROWS, TM_MIX), I32),
                        pltpu.SemaphoreType.DMA((2,)),
                        pltpu.SemaphoreType.DMA((2,))],
        compiler_params=_cparams(("arbitrary",)),
        name="mix",
    )(a2, proj2, proj2, x2, wout_b, sgw_b, sgb, sgg, sgbeta, g1, b1, wr, br)


def _experts_kernel(ts_ref, nt_ref, nu_ref, trow_ref, nval_ref, x_hbm, wg_ref, wu_ref, wd_ref,
                    y_hbm, xbuf, ybuf, xsem, ysem):
    e = pl.program_id(0)
    d = wg_ref.shape[1]
    n_used = nu_ref[0]

    def x_copy(g):
        rows = pl.ds(pl.multiple_of(trow_ref[g], TMX), TMX)
        slot = g % N_XSLOTS
        return pltpu.make_async_copy(x_hbm.at[rows], xbuf.at[slot], xsem.at[slot])

    def y_copy(g):
        rows = pl.ds(pl.multiple_of(g * TMX, TMX), TMX)
        slot = g % N_XSLOTS
        return pltpu.make_async_copy(ybuf.at[slot], y_hbm.at[rows], ysem.at[slot])

    def start_x(g):
        @pl.when(g < n_used)
        def _():
            x_copy(g).start(priority=1)

    def free_y(g):
        @pl.when(g >= N_XSLOTS)
        def _():
            y_copy(g - N_XSLOTS).wait()

    def load(g):
        row = lax.broadcasted_iota(I32, xbuf.shape[1:], 0)
        xw = jnp.where(row < nval_ref[g], xbuf[g % N_XSLOTS], jnp.uint32(0))
        lo, hi = _unpack_bf16_pair(xw)
        return lo.astype(BF16), hi.astype(BF16)

    def mlp(lo, hi):
        gate = jnp.dot(lo, wg_ref[0, :d // 2, :], preferred_element_type=F32)
        gate = gate + jnp.dot(hi, wg_ref[0, d // 2:, :], preferred_element_type=F32)
        up = jnp.dot(lo, wu_ref[0, :d // 2, :], preferred_element_type=F32)
        up = up + jnp.dot(hi, wu_ref[0, d // 2:, :], preferred_element_type=F32)
        hmid = (gate * jax.nn.sigmoid(gate) * up).astype(BF16)
        return _pack_bf16_pair(jnp.dot(hmid, wd_ref[0], preferred_element_type=F32))

    @pl.when(e == 0)
    def _():
        for g in range(X_AHEAD):
            start_x(g)

    def group(k, first):
        def body(t, carry):
            g = first(t)
            for j in range(k):
                x_copy(g + j).wait()
            for j in range(k):
                start_x(g + X_AHEAD + j)
                free_y(g + j)
            parts = [load(g + j) for j in range(k)]
            y = mlp(jnp.concatenate([p[0] for p in parts], axis=0),
                    jnp.concatenate([p[1] for p in parts], axis=0))
            for j in range(k):
                ybuf[(g + j) % N_XSLOTS] = y[j * TMX:(j + 1) * TMX]
                y_copy(g + j).start(priority=1)
            return carry
        return body

    nt = nt_ref[e]
    n3 = jnp.where(jnp.logical_and(nt >= 3, nt % 2 == 1), 1, 0)
    n2 = (nt - 3 * n3) // 2
    n1 = nt - 3 * n3 - 2 * n2
    base = ts_ref[e]
    lax.fori_loop(0, n3, group(3, lambda t: base), 0)
    lax.fori_loop(0, n2, group(2, lambda t: base + 3 * n3 + 2 * t), 0)
    lax.fori_loop(0, n1, group(1, lambda t: base + 3 * n3 + 2 * n2), 0)

    @pl.when(e == pl.num_programs(0) - 1)
    def _():
        for back in range(N_XSLOTS, 0, -1):
            @pl.when(n_used >= back)
            def _():
                y_copy(n_used - back).wait()


def _experts(tile_start, n_tiles_e, n_used, tile_row, tile_valid, xs, w_gate, w_up, w_down):
    w = xs.shape[1]
    d = w_gate.shape[1]
    n_rows = tile_row.shape[0] * TMX

    def w_map(e, ts, nt, nu, trow, nval):
        return (e, 0, 0)

    grid_spec = pltpu.PrefetchScalarGridSpec(
        num_scalar_prefetch=5,
        grid=(N_EXPERTS,),
        in_specs=[
            pl.BlockSpec(memory_space=pl.ANY),
            pl.BlockSpec((1, d, D_EXPERT), w_map),
            pl.BlockSpec((1, d, D_EXPERT), w_map),
            pl.BlockSpec((1, D_EXPERT, d), w_map),
        ],
        out_specs=pl.BlockSpec(memory_space=pl.ANY),
        scratch_shapes=[pltpu.VMEM((N_XSLOTS, TMX, w), U32),
                        pltpu.VMEM((N_XSLOTS, TMX, w), U32),
                        pltpu.SemaphoreType.DMA((N_XSLOTS,)),
                        pltpu.SemaphoreType.DMA((N_XSLOTS,))],
    )
    return pl.pallas_call(
        _experts_kernel,
        grid_spec=grid_spec,
        out_shape=jax.ShapeDtypeStruct((n_rows, w), U32),
        compiler_params=_cparams(("arbitrary",)),
        name="experts",
    )(tile_start, n_tiles_e, n_used, tile_row, tile_valid, xs, w_gate, w_up, w_down)


def _combine_kernel(pos_ref, posn_ref, y_hbm, h1_ref, wt_ref, g2_ref, b2_ref, o_ref, ybuf, sem):
    i = pl.program_id(0)
    n = pl.num_programs(0)
    slot = i % 2

    def row_copy(src, sl, which, j):
        return pltpu.make_async_copy(y_hbm.at[pl.ds(src, 1)],
                                     ybuf.at[sl, which, pl.ds(j, 1)], sem.at[sl])

    def issue(p_ref, sl):
        def body(j, c):
            row_copy(p_ref[0, j], sl, 0, j).start(priority=0)
            row_copy(p_ref[1, j], sl, 1, j).start(priority=1)
            return c
        lax.fori_loop(0, TM_OUT, body, 0, unroll=8)

    def drain(sl):
        def wait(j, c):
            row_copy(0, sl, 0, 0).wait()
            row_copy(0, sl, 1, 0).wait()
            return c
        lax.fori_loop(0, TM_OUT, wait, 0, unroll=8)

    @pl.when(i == 0)
    def _():
        issue(pos_ref, 0)

    drain(slot)

    for j in range(TM_OUT):
        row_copy(posn_ref[0, j], 1 - slot, 0, j).start(priority=0)
        row_copy(posn_ref[1, j], 1 - slot, 1, j).start(priority=1)

    lo1, hi1 = _unpack_bf16_pair(ybuf[slot, 0])
    lo2, hi2 = _unpack_bf16_pair(ybuf[slot, 1])
    w1 = wt_ref[:, 0:1]
    w2 = wt_ref[:, 1:2]
    ffn = jnp.concatenate([w1 * lo1 + w2 * lo2, w1 * hi1 + w2 * hi2], axis=1)
    o_ref[...] = _layernorm(DN_ALPHA * h1_ref[...] + ffn, g2_ref[...], b2_ref[...])

    @pl.when(i == n - 1)
    def _():
        drain(1 - slot)


def _combine(pos3, ys, h1, wtok, g2, b2):
    t, d = h1.shape
    w = ys.shape[1]
    n = t // TM_OUT
    return pl.pallas_call(
        _combine_kernel,
        grid=(n,),
        in_specs=[
            pl.BlockSpec((None, 2, TM_OUT), lambda i: (i, 0, 0), memory_space=pltpu.SMEM),
            pl.BlockSpec((None, 2, TM_OUT), lambda i: (jnp.minimum(i + 1, n - 1), 0, 0),
                         memory_space=pltpu.SMEM),
            pl.BlockSpec(memory_space=pl.ANY),
            pl.BlockSpec((TM_OUT, d), lambda i: (i, 0)),
            pl.BlockSpec((TM_OUT, ROUTE_ROWS), lambda i: (i, 0)),
            pl.BlockSpec((1, d), lambda i: (0, 0)),
            pl.BlockSpec((1, d), lambda i: (0, 0)),
        ],
        out_specs=pl.BlockSpec((TM_OUT, d), lambda i: (i, 0)),
        out_shape=jax.ShapeDtypeStruct((t, d), F32),
        scratch_shapes=[pltpu.VMEM((2, 2, TM_OUT, w), U32),
                        pltpu.SemaphoreType.DMA((2,))],
        compiler_params=_cparams(("arbitrary",)),
        name="combine",
    )(pos3, pos3, ys, h1, wtok, g2, b2)


def _lambda_init(layer):
    return 0.8 - 0.6 * math.exp(-0.3 * layer)


def kernel(x, w_in, w_out, ln1_g, ln1_b, ln2_g, ln2_b, rel_bias, lam_q1, lam_k1, lam_q2, lam_k2,
           subln_g, sg_ln_g, sg_ln_b, sg_w, sg_b, w_router_group, b_router_group,
           w_router_expert, b_router_expert, w_exp_gate, w_exp_up, w_exp_down):
    b, s, d = x.shape
    t = b * s
    assert w_in.shape[0] == DEPTH == 1
    l = 0
    lam_init = _lambda_init(l)
    x2 = x.reshape(t, d)

    n_qkv = 2 * ATT_HEADS * 2 * ATT_QK_DIM + ATT_WIDTH
    proj_qkv = _inproj(x2, w_in[l], 0, n_qkv, gelu=False, name="inproj_qkv")
    proj_uv = _inproj(x2, w_in[l], n_qkv, w_in.shape[2], gelu=True, name="inproj_uv")

    lam = (jnp.exp(jnp.sum(lam_q1[l].astype(F32) * lam_k1[l].astype(F32)))
           - jnp.exp(jnp.sum(lam_q2[l].astype(F32) * lam_k2[l].astype(F32))) + lam_init)
    side = [w_exp_gate[l].reshape(-1, D_EXPERT), w_exp_up[l].reshape(-1, D_EXPERT),
            w_exp_down[l].reshape(-1, d), w_out[l]]
    a, wg_b, wu_b, wd_b, wout_b = _attention(
        proj_qkv.reshape(b, s, -1), _bias_sequences(rel_bias, s), lam.reshape(1),
        subln_g[l].reshape(1, -1), lam_init, side)

    wr = jnp.concatenate([w_router_group[l].T,
                          jnp.zeros((NR_EXPERT0 - N_GROUPS, d), F32),
                          jnp.transpose(w_router_expert[l], (0, 2, 1)).reshape(N_EXPERTS, d)],
                         axis=0)
    br = jnp.concatenate([b_router_group[l], jnp.zeros((NR_EXPERT0 - N_GROUPS,), F32),
                          b_router_expert[l].reshape(-1)]).reshape(NR_PAD, 1)
    wr_hi = wr.astype(BF16)
    wr = jnp.concatenate([wr_hi, (wr - wr_hi.astype(F32)).astype(BF16)], axis=0)
    h1, xs, ri, rw, cnt = _mix(
        a.reshape(t, -1), proj_uv, x2, wout_b, sg_w[l].astype(BF16),
        sg_b[l].reshape(SG_GROUPS, SG_CHUNK, 1), sg_ln_g[l], sg_ln_b[l],
        ln1_g[l].reshape(1, d), ln1_b[l].reshape(1, d), wr, br)

    counts = cnt[:, 0]
    tiles_e = (counts + TMX - 1) // TMX
    tile_end = jnp.cumsum(tiles_e)
    tile_start = tile_end - tiles_e
    n_used = tile_end[-1:]
    max_tiles = (2 * t) // TMX + N_EXPERTS
    row_start = (tile_start * TMX).astype(I32)
    eids = jnp.arange(N_EXPERTS, dtype=I32)[:, None, None]
    pos = jnp.sum(jnp.where(ri[0:2][None] == eids, row_start[:, None, None], 0), axis=0) + ri[2:4]
    pos3 = pos.reshape(2, t // TM_OUT, TM_OUT).transpose(1, 0, 2)
    tile_ids = jnp.arange(max_tiles, dtype=I32)
    tile_e = jnp.minimum(jnp.sum(tile_end[None, :] <= tile_ids[:, None], axis=1), N_EXPERTS - 1)
    local = tile_ids - tile_start[tile_e]
    in_use = tile_ids < n_used[0]
    tile_row = jnp.where(in_use, tile_e * XS_CAP + local * TMX, 0).astype(I32)
    tile_valid = jnp.where(in_use, jnp.clip(counts[tile_e] - local * TMX, 0, TMX), 0).astype(I32)

    ys = _experts(tile_start.astype(I32), tiles_e.astype(I32), n_used.astype(I32),
                  tile_row, tile_valid, xs,
                  wg_b.reshape(N_EXPERTS, d, D_EXPERT), wu_b.reshape(N_EXPERTS, d, D_EXPERT),
                  wd_b.reshape(N_EXPERTS, D_EXPERT, d))
    out = _combine(pos3, ys, h1, rw.T, ln2_g[l].reshape(1, d), ln2_b[l].reshape(1, d))
    return out.reshape(b, s, d)
```

```python
import functools
import math

import jax
import jax.numpy as jnp
from jax import lax
from jax.experimental import pallas as pl
from jax.experimental.pallas import tpu as pltpu

F32 = jnp.float32
BF16 = jnp.bfloat16
U32 = jnp.uint32
I32 = jnp.int32

ATT_HEADS = 8
ATT_QK_DIM = 64
ATT_V_DIM = 128
ATT_WIDTH = ATT_HEADS * ATT_V_DIM
SG_GROUPS = 8
SG_DIM = 128
SG_WIDTH = SG_GROUPS * SG_DIM
SG_CHUNK = 128
REL_BUCKETS = 32
REL_MAX_DIST = 128
N_GROUPS = 4
EXPERTS_PER_GROUP = 8
N_EXPERTS = N_GROUPS * EXPERTS_PER_GROUP
D_EXPERT = 256
DEPTH = 1
DN_ALPHA = (2.0 * DEPTH) ** 0.25
LN_EPS = 1e-5
LOG2E = math.log2(math.e)

LANES = 128
SUBLANES = 8
VMEM_LIMIT_BYTES = 56 * 1024 * 1024

TM_IN = 2048
TN_IN = 512
QB = 256
KC = 256
N_BIAS_TILES = 5
TM_MIX = 512
TMX = 256
TM_OUT = 256
X_AHEAD = 3
N_XSLOTS = 8
ROUTE_ROWS = 8
XS_CAP = 8192 + TMX
NR_EXPERT0 = 8
NR_PAD = NR_EXPERT0 + 32


def _cparams(sem, flags=None):
    return pltpu.CompilerParams(dimension_semantics=sem, vmem_limit_bytes=VMEM_LIMIT_BYTES,
                                flags=flags)


def _inproj_kernel(x_ref, w_ref, o_ref, xb_ref, *, n_q, n_lin, q_scale):
    j = pl.program_id(1)

    @pl.when(j == 0)
    def _():
        xb_ref[...] = x_ref[...].astype(BF16)

    acc = jnp.dot(xb_ref[...], w_ref[...].astype(BF16), preferred_element_type=F32)

    @pl.when(j < n_q)
    def _():
        o_ref[...] = (acc * q_scale).astype(BF16)

    @pl.when(jnp.logical_and(j >= n_q, j < n_lin))
    def _():
        o_ref[...] = acc.astype(BF16)

    @pl.when(j >= n_lin)
    def _():
        o_ref[...] = jax.nn.gelu(acc).astype(BF16)


def _inproj(x2, w_in):
    t, d = x2.shape
    n = w_in.shape[1]
    n_q = (ATT_HEADS * 2 * ATT_QK_DIM) // TN_IN
    n_lin = (2 * ATT_HEADS * 2 * ATT_QK_DIM + ATT_WIDTH) // TN_IN
    kern = functools.partial(_inproj_kernel, n_q=n_q, n_lin=n_lin,
                             q_scale=(ATT_QK_DIM ** -0.5) * LOG2E)
    return pl.pallas_call(
        kern,
        grid=(t // TM_IN, n // TN_IN),
        in_specs=[pl.BlockSpec((TM_IN, d), lambda i, j: (i, 0), pipeline_mode=pl.Buffered(1)),
                  pl.BlockSpec((d, TN_IN), lambda i, j: (0, j))],
        out_specs=pl.BlockSpec((TM_IN, TN_IN), lambda i, j: (i, j)),
        out_shape=jax.ShapeDtypeStruct((t, n), BF16),
        scratch_shapes=[pltpu.VMEM((TM_IN, d), BF16)],
        compiler_params=_cparams(("arbitrary", "arbitrary")),
        name="inproj",
    )(x2, w_in)


def _attn_kernel(lam_ref, q_ref, k_ref, v_ref, bseq_ref, g_ref, *rest, n_kc, out_scale):
    n_side = (len(rest) - 3) // 2
    side_in, o_ref, side_out = rest[:n_side], rest[n_side], rest[n_side + 1:2 * n_side + 1]
    vaug_ref, bias_ref = rest[2 * n_side + 1:]
    qi = pl.program_id(2)

    @pl.when(qi == 0)
    def _():
        vaug_ref[:, :ATT_V_DIM] = v_ref[0]
        vaug_ref[:, ATT_V_DIM:] = jnp.ones((vaug_ref.shape[0], ATT_V_DIM), BF16)
        for t in range(N_BIAS_TILES):
            seq = jnp.broadcast_to(bseq_ref[0, t:t + 1, :], (QB, 2 * KC))
            bias_ref[t] = pltpu.roll(seq, 0, 1, stride=1, stride_axis=0)[:, :KC]

    q = q_ref[0]
    lane = lax.broadcasted_iota(I32, q.shape, 1)
    zero = jnp.zeros_like(q)
    qq = jnp.concatenate([jnp.where(lane < ATT_QK_DIM, q, zero),
                          jnp.where(lane >= ATT_QK_DIM, q, zero)], axis=0)

    for src, dst in zip(side_in, side_out):
        dst[...] = src[...].astype(dst.dtype)

    m = jnp.full((2 * QB, 1), -jnp.inf, F32)
    acc = jnp.zeros((2 * QB, 2 * ATT_V_DIM), F32)
    for kc in range(n_kc):
        kblk = k_ref[0, kc * KC:(kc + 1) * KC, :]
        s = lax.dot_general(qq, kblk, (((1,), (1,)), ((), ())),
                            preferred_element_type=F32)
        b = bias_ref[jnp.clip(kc - qi, -2, 2) + 2]
        s = s + jnp.concatenate([b, b], axis=0)
        m_new = jnp.maximum(m, jnp.max(s, axis=1, keepdims=True))
        p = jnp.exp2(s - m_new).astype(BF16)
        acc = acc * jnp.exp2(m - m_new) + jnp.dot(
            p, vaug_ref[kc * KC:(kc + 1) * KC, :], preferred_element_type=F32)
        m = m_new
    lam = lam_ref[0]
    o0 = acc[:QB, :ATT_V_DIM] / acc[:QB, ATT_V_DIM:ATT_V_DIM + 1]
    o1 = acc[QB:, :ATT_V_DIM] / acc[QB:, ATT_V_DIM:ATT_V_DIM + 1]
    of = o0 - lam * o1
    of = of * lax.rsqrt(jnp.mean(of * of, axis=-1, keepdims=True) + LN_EPS)
    o_ref[0] = (of * (g_ref[...] * out_scale)).astype(o_ref.dtype)


def _attention(proj3, bias_seq, lam, subln_g, lam_init, side):
    b, s, _ = proj3.shape
    n_kc = s // KC
    n_q = s // QB
    n_steps = b * ATT_HEADS * n_q
    h_off_k = ATT_HEADS
    h_off_v = 2 * ATT_HEADS
    kern = functools.partial(_attn_kernel, n_kc=n_kc, out_scale=1.0 - lam_init)

    side_specs = []
    for arr in side:
        rows, cols = arr.shape
        blk = max(rows // n_steps, 2 * SUBLANES)
        hold = blk * n_steps // rows
        assert rows % blk == 0 and hold * rows == blk * n_steps
        side_specs.append(pl.BlockSpec(
            (blk, cols),
            lambda bi, h, qi, hold=hold: (((bi * ATT_HEADS + h) * n_q + qi) // hold, 0)))
    return pl.pallas_call(
        kern,
        grid=(b, ATT_HEADS, n_q),
        in_specs=[
            pl.BlockSpec(memory_space=pltpu.SMEM),
            pl.BlockSpec((1, QB, LANES), lambda bi, h, qi: (bi, qi, h)),
            pl.BlockSpec((1, s, LANES), lambda bi, h, qi: (bi, 0, h_off_k + h)),
            pl.BlockSpec((1, s, LANES), lambda bi, h, qi: (bi, 0, h_off_v + h)),
            pl.BlockSpec((1, N_BIAS_TILES, 2 * KC), lambda bi, h, qi: (h, 0, 0)),
            pl.BlockSpec((1, ATT_V_DIM), lambda bi, h, qi: (0, 0)),
        ] + side_specs,
        out_specs=[pl.BlockSpec((1, QB, ATT_V_DIM), lambda bi, h, qi: (bi, qi, h))] + side_specs,
        out_shape=[jax.ShapeDtypeStruct((b, s, ATT_WIDTH), BF16)]
        + [jax.ShapeDtypeStruct(arr.shape, BF16) for arr in side],
        scratch_shapes=[pltpu.VMEM((s, 2 * ATT_V_DIM), BF16),
                        pltpu.VMEM((N_BIAS_TILES, QB, KC), F32)],
        compiler_params=_cparams(("arbitrary", "arbitrary", "arbitrary")),
        name="attn",
    )(lam, proj3, proj3, proj3, bias_seq, subln_g, *side)


def _rel_bucket(rel):
    half = REL_BUCKETS // 2
    max_exact = half // 2
    ret = jnp.where(rel > 0, half, 0)
    n = jnp.abs(rel)
    nf = jnp.maximum(n, 1).astype(F32)
    large = max_exact + (jnp.log(nf / max_exact) / math.log(REL_MAX_DIST / max_exact)
                         * (half - max_exact)).astype(I32)
    large = jnp.minimum(large, half - 1)
    return ret + jnp.where(n < max_exact, n, large)


def _bias_sequences(rel_bias, s):
    assert REL_MAX_DIST <= KC and QB == KC
    period = 2 * KC
    j = jnp.arange(period, dtype=I32)
    off = jnp.where(j < KC, j, j - period)
    d = jnp.arange(-2, 3, dtype=I32)[:, None]
    rel = jnp.clip(d * KC + off[None, :], -(s - 1), s - 1)
    return jnp.transpose(rel_bias[_rel_bucket(rel)], (2, 0, 1)).astype(F32) * LOG2E


def _layernorm(v, g, b):
    mu = jnp.mean(v, axis=-1, keepdims=True)
    var = jnp.mean(jnp.square(v - mu), axis=-1, keepdims=True)
    return (v - mu) * lax.rsqrt(var + LN_EPS) * g + b


def _pack_bf16_pair(v):
    n = v.shape[1] // 2
    bits = lax.bitcast_convert_type(v.astype(BF16).astype(F32), U32)
    return (bits[:, :n] >> 16) | (bits[:, n:] & jnp.uint32(0xFFFF0000))


def _unpack_bf16_pair(w):
    lo = lax.bitcast_convert_type(w << 16, F32)
    hi = lax.bitcast_convert_type(w & jnp.uint32(0xFFFF0000), F32)
    return lo, hi


def _first_argmax(vals, nrows):
    mx = jnp.max(vals, axis=0, keepdims=True)
    row = lax.broadcasted_iota(I32, vals.shape, 0)
    idx = jnp.min(jnp.where(vals == mx, row, nrows), axis=0, keepdims=True)
    return mx, idx


def _mix_kernel(a_ref, u_ref, vg_ref, x_ref, wout_ref, sgw_ref, sgb_ref, sgg_ref, sgbeta_ref,
                g1_ref, b1_ref, wr_ref, br_ref,
                h1_ref, xs_hbm, ri_ref, rw_ref, cnt_ref,
                run_ref, hp_s, posv_ref, psm, ssem, rsem):
    i = pl.program_id(0)
    n = pl.num_programs(0)
    slot = i % 2
    prev = 1 - slot

    def pos_copy(sl):
        return pltpu.make_async_copy(posv_ref, psm.at[sl], ssem.at[sl])

    def row_copy(sl, which, j):
        return pltpu.make_async_copy(hp_s.at[sl, pl.ds(j, 1)],
                                     xs_hbm.at[pl.ds(psm[sl, which, j], 1)], rsem.at[sl])

    def drain(sl):
        def wait(j, c):
            row_copy(sl, 0, 0).wait()
            row_copy(sl, 1, 0).wait()
            return c
        lax.fori_loop(0, TM_MIX, wait, 0, unroll=8)

    @pl.when(i == 0)
    def _():
        run_ref[...] = jnp.zeros_like(run_ref)
        hp_s[1] = jnp.zeros(hp_s.shape[1:], hp_s.dtype)
        spare = N_EXPERTS * XS_CAP + lax.broadcasted_iota(I32, posv_ref.shape, 1)
        spare = spare + TM_MIX * lax.broadcasted_iota(I32, posv_ref.shape, 0)
        posv_ref[...] = jnp.where(spare < xs_hbm.shape[0], spare, 0)
        pos_copy(1).start()

    @pl.when(i > 0)
    def _():
        drain(slot)

    pos_copy(prev).wait()

    for j in range(TM_MIX):
        row_copy(prev, 0, j).start(priority=0)
        row_copy(prev, 1, j).start(priority=1)

    h1_ref[...] = jnp.dot(a_ref[...], wout_ref[:ATT_WIDTH, :], preferred_element_type=F32)

    n_chunks = TM_MIX // SG_CHUNK
    rows = []
    for c in range(n_chunks):
        cols = []
        for g in range(SG_GROUPS):
            rs = slice(c * SG_CHUNK, (c + 1) * SG_CHUNK)
            cs = slice(g * SG_DIM, (g + 1) * SG_DIM)
            vt = vg_ref[rs, cs].astype(F32)
            vn = _layernorm(vt, sgg_ref[g:g + 1, :], sgbeta_ref[g:g + 1, :])
            mixed = jnp.dot(sgw_ref[g], vn.astype(BF16), preferred_element_type=F32) + sgb_ref[g]
            cols.append((u_ref[rs, cs].astype(F32) * mixed).astype(BF16))
        rows.append(jnp.concatenate(cols, axis=1))
    sgate = jnp.concatenate(rows, axis=0)

    mix = h1_ref[...] + jnp.dot(sgate, wout_ref[ATT_WIDTH:, :], preferred_element_type=F32)
    h1 = _layernorm(DN_ALPHA * x_ref[...] + mix, g1_ref[...], b1_ref[...])
    h1_ref[...] = h1
    hp_s[slot] = _pack_bf16_pair(h1)

    nt = (((1,), (1,)), ((), ()))
    h1_hi = h1.astype(BF16)
    h1_lo = (h1 - h1_hi.astype(F32)).astype(BF16)
    both = lax.dot_general(wr_ref[...], h1_hi, nt, preferred_element_type=F32)
    lt = both[:NR_PAD] + both[NR_PAD:] + lax.dot_general(
        wr_ref[:NR_PAD, :], h1_lo, nt, preferred_element_type=F32)
    lt = lt + br_ref[...]
    g_logits = lt[0:N_GROUPS]
    gmax, g_idx = _first_argmax(g_logits, N_GROUPS)
    g_gate = 1.0 / jnp.sum(jnp.exp(g_logits - gmax), axis=0, keepdims=True)
    e_logits = jnp.zeros((EXPERTS_PER_GROUP, TM_MIX), F32)
    for g in range(N_GROUPS):
        lo = NR_EXPERT0 + g * EXPERTS_PER_GROUP
        e_logits = jnp.where(g_idx == g, lt[lo:lo + EXPERTS_PER_GROUP], e_logits)
    v1, i1 = _first_argmax(e_logits, EXPERTS_PER_GROUP)
    erow = lax.broadcasted_iota(I32, e_logits.shape, 0)
    v2, i2 = _first_argmax(jnp.where(erow == i1, -jnp.inf, e_logits), EXPERTS_PER_GROUP)
    t = jnp.exp(v2 - v1)
    w1 = g_gate / (1.0 + t)
    w2 = g_gate * t / (1.0 + t)
    e1 = g_idx * EXPERTS_PER_GROUP + i1
    e2 = g_idx * EXPERTS_PER_GROUP + i2

    xrow = lax.broadcasted_iota(I32, (N_EXPERTS, TM_MIX), 0)
    oh1 = xrow == e1
    oh2 = xrow == e2
    oh = jnp.where(jnp.logical_or(oh1, oh2), 1.0, 0.0)
    tr = lax.broadcasted_iota(I32, (TM_MIX, TM_MIX), 0)
    tc = lax.broadcasted_iota(I32, (TM_MIX, TM_MIX), 1)
    before = jnp.where(tr < tc, 1.0, 0.0).astype(BF16)
    cum = jnp.dot(oh.astype(BF16), before, preferred_element_type=F32)
    base = cum + run_ref[:, 0:1]
    rank1 = jnp.sum(jnp.where(oh1, base, 0.0), axis=0, keepdims=True).astype(I32)
    rank2 = jnp.sum(jnp.where(oh2, base, 0.0), axis=0, keepdims=True).astype(I32)
    run_ref[...] = run_ref[...] + jnp.sum(oh, axis=1, keepdims=True)
    cnt_ref[...] = run_ref[...].astype(I32)

    zi = jnp.zeros((ROUTE_ROWS - 4, TM_MIX), I32)
    ri_ref[...] = jnp.concatenate([e1, e2, rank1, rank2, zi], axis=0)
    zw = jnp.zeros((ROUTE_ROWS - 2, TM_MIX), F32)
    rw_ref[...] = jnp.concatenate([w1, w2, zw], axis=0)

    posv_ref[...] = jnp.concatenate([e1 * XS_CAP + rank1, e2 * XS_CAP + rank2, zi, zi[:2]], axis=0)
    pos_copy(slot).start()

    @pl.when(i == n - 1)
    def _():
        pos_copy(slot).wait()

        def last(j, c):
            row_copy(slot, 0, j).start(priority=0)
            row_copy(slot, 1, j).start(priority=1)
            return c
        lax.fori_loop(0, TM_MIX, last, 0, unroll=8)
        drain(prev)
        drain(slot)


def _mix(a2, proj2, x2, wout_b, sgw_b, sgb, sgg, sgbeta, g1, b1, wr, br):
    t, d = x2.shape
    assert XS_CAP >= t + TMX
    u_blk = (proj2.shape[1] - 2 * SG_WIDTH) // SG_WIDTH
    const = lambda *shape: pl.BlockSpec(shape, lambda i: (0,) * len(shape))
    return pl.pallas_call(
        _mix_kernel,
        grid=(t // TM_MIX,),
        in_specs=[
            pl.BlockSpec((TM_MIX, ATT_WIDTH), lambda i: (i, 0)),
            pl.BlockSpec((TM_MIX, SG_WIDTH), lambda i: (i, u_blk)),
            pl.BlockSpec((TM_MIX, SG_WIDTH), lambda i: (i, u_blk + 1)),
            pl.BlockSpec((TM_MIX, d), lambda i: (i, 0)),
            pl.BlockSpec((d, d), lambda i: (0, 0), pipeline_mode=pl.Buffered(1)),
            const(SG_GROUPS, SG_CHUNK, SG_CHUNK),
            const(SG_GROUPS, SG_CHUNK, 1),
            const(SG_GROUPS, SG_DIM),
            const(SG_GROUPS, SG_DIM),
            const(1, d),
            const(1, d),
            const(2 * NR_PAD, d),
            const(NR_PAD, 1),
        ],
        out_specs=[
            pl.BlockSpec((TM_MIX, d), lambda i: (i, 0)),
            pl.BlockSpec(memory_space=pl.ANY),
            pl.BlockSpec((ROUTE_ROWS, TM_MIX), lambda i: (0, i)),
            pl.BlockSpec((ROUTE_ROWS, TM_MIX), lambda i: (0, i)),
            const(N_EXPERTS, LANES),
        ],
        out_shape=[
            jax.ShapeDtypeStruct((t, d), F32),
            jax.ShapeDtypeStruct((N_EXPERTS * XS_CAP + 2 * TM_MIX, d // 2), U32),
            jax.ShapeDtypeStruct((ROUTE_ROWS, t), I32),
            jax.ShapeDtypeStruct((ROUTE_ROWS, t), F32),
            jax.ShapeDtypeStruct((N_EXPERTS, LANES), I32),
        ],
        scratch_shapes=[pltpu.VMEM((N_EXPERTS, LANES), F32),
                        pltpu.VMEM((2, TM_MIX, d // 2), U32),
                        pltpu.VMEM((ROUTE_ROWS, TM_MIX), I32),
                        pltpu.SMEM((2, ROUTE_ROWS, TM_MIX), I32),
                        pltpu.SemaphoreType.DMA((2,)),
                        pltpu.SemaphoreType.DMA((2,))],
        compiler_params=_cparams(("arbitrary",)),
        name="mix",
    )(a2, proj2, proj2, x2, wout_b, sgw_b, sgb, sgg, sgbeta, g1, b1, wr, br)


def _experts_kernel(ts_ref, nt_ref, nu_ref, trow_ref, nval_ref, x_hbm, wg_ref, wu_ref, wd_ref,
                    y_hbm, xbuf, ybuf, xsem, ysem):
    e = pl.program_id(0)
    d = wg_ref.shape[1]
    n_used = nu_ref[0]

    def x_copy(g):
        rows = pl.ds(pl.multiple_of(trow_ref[g], TMX), TMX)
        slot = g % N_XSLOTS
        return pltpu.make_async_copy(x_hbm.at[rows], xbuf.at[slot], xsem.at[slot])

    def y_copy(g):
        rows = pl.ds(pl.multiple_of(g * TMX, TMX), TMX)
        slot = g % N_XSLOTS
        return pltpu.make_async_copy(ybuf.at[slot], y_hbm.at[rows], ysem.at[slot])

    def start_x(g):
        @pl.when(g < n_used)
        def _():
            x_copy(g).start(priority=1)

    def free_y(g):
        @pl.when(g >= N_XSLOTS)
        def _():
            y_copy(g - N_XSLOTS).wait()

    def load(g):
        row = lax.broadcasted_iota(I32, xbuf.shape[1:], 0)
        xw = jnp.where(row < nval_ref[g], xbuf[g % N_XSLOTS], jnp.uint32(0))
        lo, hi = _unpack_bf16_pair(xw)
        return lo.astype(BF16), hi.astype(BF16)

    def mlp(lo, hi):
        gate = jnp.dot(lo, wg_ref[0, :d // 2, :], preferred_element_type=F32)
        gate = gate + jnp.dot(hi, wg_ref[0, d // 2:, :], preferred_element_type=F32)
        up = jnp.dot(lo, wu_ref[0, :d // 2, :], preferred_element_type=F32)
        up = up + jnp.dot(hi, wu_ref[0, d // 2:, :], preferred_element_type=F32)
        hmid = (gate * jax.nn.sigmoid(gate) * up).astype(BF16)
        return _pack_bf16_pair(jnp.dot(hmid, wd_ref[0], preferred_element_type=F32))

    @pl.when(e == 0)
    def _():
        for g in range(X_AHEAD):
            start_x(g)

    def group(k, first):
        def body(t, carry):
            g = first(t)
            for j in range(k):
                x_copy(g + j).wait()
            for j in range(k):
                start_x(g + X_AHEAD + j)
                free_y(g + j)
            parts = [load(g + j) for j in range(k)]
            y = mlp(jnp.concatenate([p[0] for p in parts], axis=0),
                    jnp.concatenate([p[1] for p in parts], axis=0))
            for j in range(k):
                ybuf[(g + j) % N_XSLOTS] = y[j * TMX:(j + 1) * TMX]
                y_copy(g + j).start(priority=1)
            return carry
        return body

    nt = nt_ref[e]
    n3 = jnp.where(jnp.logical_and(nt >= 3, nt % 2 == 1), 1, 0)
    n2 = (nt - 3 * n3) // 2
    n1 = nt - 3 * n3 - 2 * n2
    base = ts_ref[e]
    lax.fori_loop(0, n3, group(3, lambda t: base), 0)
    lax.fori_loop(0, n2, group(2, lambda t: base + 3 * n3 + 2 * t), 0)
    lax.fori_loop(0, n1, group(1, lambda t: base + 3 * n3 + 2 * n2), 0)

    @pl.when(e == pl.num_programs(0) - 1)
    def _():
        for back in range(N_XSLOTS, 0, -1):
            @pl.when(n_used >= back)
            def _():
                y_copy(n_used - back).wait()


def _experts(tile_start, n_tiles_e, n_used, tile_row, tile_valid, xs, w_gate, w_up, w_down):
    w = xs.shape[1]
    d = w_gate.shape[1]
    n_rows = tile_row.shape[0] * TMX

    def w_map(e, ts, nt, nu, trow, nval):
        return (e, 0, 0)

    grid_spec = pltpu.PrefetchScalarGridSpec(
        num_scalar_prefetch=5,
        grid=(N_EXPERTS,),
        in_specs=[
            pl.BlockSpec(memory_space=pl.ANY),
            pl.BlockSpec((1, d, D_EXPERT), w_map),
            pl.BlockSpec((1, d, D_EXPERT), w_map),
            pl.BlockSpec((1, D_EXPERT, d), w_map),
        ],
        out_specs=pl.BlockSpec(memory_space=pl.ANY),
        scratch_shapes=[pltpu.VMEM((N_XSLOTS, TMX, w), U32),
                        pltpu.VMEM((N_XSLOTS, TMX, w), U32),
                        pltpu.SemaphoreType.DMA((N_XSLOTS,)),
                        pltpu.SemaphoreType.DMA((N_XSLOTS,))],
    )
    return pl.pallas_call(
        _experts_kernel,
        grid_spec=grid_spec,
        out_shape=jax.ShapeDtypeStruct((n_rows, w), U32),
        compiler_params=_cparams(("arbitrary",)),
        name="experts",
    )(tile_start, n_tiles_e, n_used, tile_row, tile_valid, xs, w_gate, w_up, w_down)


def _combine_kernel(pos_ref, posn_ref, y_hbm, h1_ref, wt_ref, g2_ref, b2_ref, o_ref, ybuf, sem):
    i = pl.program_id(0)
    n = pl.num_programs(0)
    slot = i % 2

    def row_copy(src, sl, which, j):
        return pltpu.make_async_copy(y_hbm.at[pl.ds(src, 1)],
                                     ybuf.at[sl, which, pl.ds(j, 1)], sem.at[sl])

    def issue(p_ref, sl):
        def body(j, c):
            row_copy(p_ref[0, j], sl, 0, j).start(priority=0)
            row_copy(p_ref[1, j], sl, 1, j).start(priority=1)
            return c
        lax.fori_loop(0, TM_OUT, body, 0, unroll=8)

    def drain(sl):
        def wait(j, c):
            row_copy(0, sl, 0, 0).wait()
            row_copy(0, sl, 1, 0).wait()
            return c
        lax.fori_loop(0, TM_OUT, wait, 0, unroll=8)

    @pl.when(i == 0)
    def _():
        issue(pos_ref, 0)

    drain(slot)

    for j in range(TM_OUT):
        row_copy(posn_ref[0, j], 1 - slot, 0, j).start(priority=0)
        row_copy(posn_ref[1, j], 1 - slot, 1, j).start(priority=1)

    lo1, hi1 = _unpack_bf16_pair(ybuf[slot, 0])
    lo2, hi2 = _unpack_bf16_pair(ybuf[slot, 1])
    w1 = wt_ref[:, 0:1]
    w2 = wt_ref[:, 1:2]
    ffn = jnp.concatenate([w1 * lo1 + w2 * lo2, w1 * hi1 + w2 * hi2], axis=1)
    o_ref[...] = _layernorm(DN_ALPHA * h1_ref[...] + ffn, g2_ref[...], b2_ref[...])

    @pl.when(i == n - 1)
    def _():
        drain(1 - slot)


def _combine(pos3, ys, h1, wtok, g2, b2):
    t, d = h1.shape
    w = ys.shape[1]
    n = t // TM_OUT
    return pl.pallas_call(
        _combine_kernel,
        grid=(n,),
        in_specs=[
            pl.BlockSpec((None, 2, TM_OUT), lambda i: (i, 0, 0), memory_space=pltpu.SMEM),
            pl.BlockSpec((None, 2, TM_OUT), lambda i: (jnp.minimum(i + 1, n - 1), 0, 0),
                         memory_space=pltpu.SMEM),
            pl.BlockSpec(memory_space=pl.ANY),
            pl.BlockSpec((TM_OUT, d), lambda i: (i, 0)),
            pl.BlockSpec((TM_OUT, ROUTE_ROWS), lambda i: (i, 0)),
            pl.BlockSpec((1, d), lambda i: (0, 0)),
            pl.BlockSpec((1, d), lambda i: (0, 0)),
        ],
        out_specs=pl.BlockSpec((TM_OUT, d), lambda i: (i, 0)),
        out_shape=jax.ShapeDtypeStruct((t, d), F32),
        scratch_shapes=[pltpu.VMEM((2, 2, TM_OUT, w), U32),
                        pltpu.SemaphoreType.DMA((2,))],
        compiler_params=_cparams(("arbitrary",)),
        name="combine",
    )(pos3, pos3, ys, h1, wtok, g2, b2)


def _lambda_init(layer):
    return 0.8 - 0.6 * math.exp(-0.3 * layer)


def kernel(x, w_in, w_out, ln1_g, ln1_b, ln2_g, ln2_b, rel_bias, lam_q1, lam_k1, lam_q2, lam_k2,
           subln_g, sg_ln_g, sg_ln_b, sg_w, sg_b, w_router_group, b_router_group,
           w_router_expert, b_router_expert, w_exp_gate, w_exp_up, w_exp_down):
    b, s, d = x.shape
    t = b * s
    assert w_in.shape[0] == DEPTH == 1
    l = 0
    lam_init = _lambda_init(l)
    x2 = x.reshape(t, d)

    proj = _inproj(x2, w_in[l])

    lam = (jnp.exp(jnp.sum(lam_q1[l].astype(F32) * lam_k1[l].astype(F32)))
           - jnp.exp(jnp.sum(lam_q2[l].astype(F32) * lam_k2[l].astype(F32))) + lam_init)
    side = [w_exp_gate[l].reshape(-1, D_EXPERT), w_exp_up[l].reshape(-1, D_EXPERT),
            w_exp_down[l].reshape(-1, d), w_out[l]]
    a, wg_b, wu_b, wd_b, wout_b = _attention(
        proj.reshape(b, s, -1), _bias_sequences(rel_bias, s), lam.reshape(1),
        subln_g[l].reshape(1, -1), lam_init, side)

    wr = jnp.concatenate([w_router_group[l].T,
                          jnp.zeros((NR_EXPERT0 - N_GROUPS, d), F32),
                          jnp.transpose(w_router_expert[l], (0, 2, 1)).reshape(N_EXPERTS, d)],
                         axis=0)
    br = jnp.concatenate([b_router_group[l], jnp.zeros((NR_EXPERT0 - N_GROUPS,), F32),
                          b_router_expert[l].reshape(-1)]).reshape(NR_PAD, 1)
    wr_hi = wr.astype(BF16)
    wr = jnp.concatenate([wr_hi, (wr - wr_hi.astype(F32)).astype(BF16)], axis=0)
    h1, xs, ri, rw, cnt = _mix(
        a.reshape(t, -1), proj, x2, wout_b, sg_w[l].astype(BF16),
        sg_b[l].reshape(SG_GROUPS, SG_CHUNK, 1), sg_ln_g[l], sg_ln_b[l],
        ln1_g[l].reshape(1, d), ln1_b[l].reshape(1, d), wr, br)

    counts = cnt[:, 0]
    tiles_e = (counts + TMX - 1) // TMX
    tile_end = jnp.cumsum(tiles_e)
    tile_start = tile_end - tiles_e
    n_used = tile_end[-1:]
    max_tiles = (2 * t) // TMX + N_EXPERTS
    row_start = (tile_start * TMX).astype(I32)
    eids = jnp.arange(N_EXPERTS, dtype=I32)[:, None, None]
    pos = jnp.sum(jnp.where(ri[0:2][None] == eids, row_start[:, None, None], 0), axis=0) + ri[2:4]
    pos3 = pos.reshape(2, t // TM_OUT, TM_OUT).transpose(1, 0, 2)
    tile_ids = jnp.arange(max_tiles, dtype=I32)
    tile_e = jnp.minimum(jnp.sum(tile_end[None, :] <= tile_ids[:, None], axis=1), N_EXPERTS - 1)
    local = tile_ids - tile_start[tile_e]
    in_use = tile_ids < n_used[0]
    tile_row = jnp.where(in_use, tile_e * XS_CAP + local * TMX, 0).astype(I32)
    tile_valid = jnp.where(in_use, jnp.clip(counts[tile_e] - local * TMX, 0, TMX), 0).astype(I32)

    ys = _experts(tile_start.astype(I32), tiles_e.astype(I32), n_used.astype(I32),
                  tile_row, tile_valid, xs,
                  wg_b.reshape(N_EXPERTS, d, D_EXPERT), wu_b.reshape(N_EXPERTS, d, D_EXPERT),
                  wd_b.reshape(N_EXPERTS, D_EXPERT, d))
    out = _combine(pos3, ys, h1, rw.T, ln2_g[l].reshape(1, d), ln2_b[l].reshape(1, d))
    return out.reshape(b, s, d)
```

```python
import functools
import math

import jax
import jax.numpy as jnp
from jax import lax
from jax.experimental import pallas as pl
from jax.experimental.pallas import tpu as pltpu

F32 = jnp.float32
BF16 = jnp.bfloat16
I32 = jnp.int32

ATT_HEADS = 8
ATT_QK_DIM = 64
ATT_V_DIM = 128
ATT_WIDTH = ATT_HEADS * ATT_V_DIM
SG_GROUPS = 8
SG_DIM = 128
SG_WIDTH = SG_GROUPS * SG_DIM
SG_CHUNK = 128
REL_BUCKETS = 32
REL_MAX_DIST = 128
N_GROUPS = 4
EXPERTS_PER_GROUP = 8
N_EXPERTS = N_GROUPS * EXPERTS_PER_GROUP
D_EXPERT = 256
DEPTH = 1
DN_ALPHA = (2.0 * DEPTH) ** 0.25
LN_EPS = 1e-5
LOG2E = math.log2(math.e)

LANES = 128
SUBLANES = 8
VMEM_LIMIT_BYTES = 56 * 1024 * 1024

TM_IN = 1024
TN_IN = 1024
QB = 256
KC = 256
N_BIAS_TILES = 5
TM_MIX = 512
TMX = 256
TM_OUT = 256
X_AHEAD = 3
N_XSLOTS = 2 * X_AHEAD
ROUTE_ROWS = 8
XS_CAP = 8192 + TMX
NR_EXPERT0 = 8
NR_PAD = NR_EXPERT0 + 32


def _cparams(sem, flags=None):
    return pltpu.CompilerParams(dimension_semantics=sem, vmem_limit_bytes=VMEM_LIMIT_BYTES,
                                flags=flags)


def _inproj_kernel(x_ref, w_ref, o_ref, xb_ref, *, n_q, n_lin, q_scale):
    j = pl.program_id(1)

    @pl.when(j == 0)
    def _():
        xb_ref[...] = x_ref[...].astype(BF16)

    acc = jnp.dot(xb_ref[...], w_ref[...].astype(BF16), preferred_element_type=F32)

    @pl.when(j < n_q)
    def _():
        o_ref[...] = (acc * q_scale).astype(BF16)

    @pl.when(jnp.logical_and(j >= n_q, j < n_lin))
    def _():
        o_ref[...] = acc.astype(BF16)

    @pl.when(j >= n_lin)
    def _():
        o_ref[...] = jax.nn.gelu(acc).astype(BF16)


def _inproj(x2, w_in):
    t, d = x2.shape
    n = w_in.shape[1]
    n_q = (ATT_HEADS * 2 * ATT_QK_DIM) // TN_IN
    n_lin = (2 * ATT_HEADS * 2 * ATT_QK_DIM + ATT_WIDTH) // TN_IN
    kern = functools.partial(_inproj_kernel, n_q=n_q, n_lin=n_lin,
                             q_scale=(ATT_QK_DIM ** -0.5) * LOG2E)
    return pl.pallas_call(
        kern,
        grid=(t // TM_IN, n // TN_IN),
        in_specs=[pl.BlockSpec((TM_IN, d), lambda i, j: (i, 0)),
                  pl.BlockSpec((d, TN_IN), lambda i, j: (0, j))],
        out_specs=pl.BlockSpec((TM_IN, TN_IN), lambda i, j: (i, j)),
        out_shape=jax.ShapeDtypeStruct((t, n), BF16),
        scratch_shapes=[pltpu.VMEM((TM_IN, d), BF16)],
        compiler_params=_cparams(("arbitrary", "arbitrary")),
        name="inproj",
    )(x2, w_in)


def _attn_kernel(lam_ref, q_ref, k_ref, v_ref, bseq_ref, g_ref, *rest, n_kc, out_scale):
    n_side = (len(rest) - 3) // 2
    side_in, o_ref, side_out = rest[:n_side], rest[n_side], rest[n_side + 1:2 * n_side + 1]
    vaug_ref, bias_ref = rest[2 * n_side + 1:]
    qi = pl.program_id(2)

    @pl.when(qi == 0)
    def _():
        vaug_ref[:, :ATT_V_DIM] = v_ref[0]
        vaug_ref[:, ATT_V_DIM:] = jnp.ones((vaug_ref.shape[0], ATT_V_DIM), BF16)
        for t in range(N_BIAS_TILES):
            seq = jnp.broadcast_to(bseq_ref[0, t:t + 1, :], (QB, 2 * KC))
            bias_ref[t] = pltpu.roll(seq, 0, 1, stride=1, stride_axis=0)[:, :KC]

    q = q_ref[0]
    lane = lax.broadcasted_iota(I32, q.shape, 1)
    zero = jnp.zeros_like(q)
    qq = jnp.concatenate([jnp.where(lane < ATT_QK_DIM, q, zero),
                          jnp.where(lane >= ATT_QK_DIM, q, zero)], axis=0)

    for src, dst in zip(side_in, side_out):
        dst[...] = src[...].astype(dst.dtype)

    m = jnp.full((2 * QB, 1), -jnp.inf, F32)
    acc = jnp.zeros((2 * QB, 2 * ATT_V_DIM), F32)
    for kc in range(n_kc):
        kblk = k_ref[0, kc * KC:(kc + 1) * KC, :]
        s = lax.dot_general(qq, kblk, (((1,), (1,)), ((), ())),
                            preferred_element_type=F32)
        b = bias_ref[jnp.clip(kc - qi, -2, 2) + 2]
        s = s + jnp.concatenate([b, b], axis=0)
        m_new = jnp.maximum(m, jnp.max(s, axis=1, keepdims=True))
        p = jnp.exp2(s - m_new).astype(BF16)
        acc = acc * jnp.exp2(m - m_new) + jnp.dot(
            p, vaug_ref[kc * KC:(kc + 1) * KC, :], preferred_element_type=F32)
        m = m_new
    lam = lam_ref[0]
    o0 = acc[:QB, :ATT_V_DIM] / acc[:QB, ATT_V_DIM:ATT_V_DIM + 1]
    o1 = acc[QB:, :ATT_V_DIM] / acc[QB:, ATT_V_DIM:ATT_V_DIM + 1]
    of = o0 - lam * o1
    of = of * lax.rsqrt(jnp.mean(of * of, axis=-1, keepdims=True) + LN_EPS)
    o_ref[0] = (of * (g_ref[...] * out_scale)).astype(o_ref.dtype)


def _attention(proj3, bias_seq, lam, subln_g, lam_init, side):
    b, s, _ = proj3.shape
    n_kc = s // KC
    n_q = s // QB
    n_steps = b * ATT_HEADS * n_q
    h_off_k = ATT_HEADS
    h_off_v = 2 * ATT_HEADS
    kern = functools.partial(_attn_kernel, n_kc=n_kc, out_scale=1.0 - lam_init)

    side_specs = []
    for arr in side:
        rows, cols = arr.shape
        blk = max(rows // n_steps, 2 * SUBLANES)
        hold = blk * n_steps // rows
        assert rows % blk == 0 and hold * rows == blk * n_steps
        side_specs.append(pl.BlockSpec(
            (blk, cols),
            lambda bi, h, qi, hold=hold: (((bi * ATT_HEADS + h) * n_q + qi) // hold, 0)))
    return pl.pallas_call(
        kern,
        grid=(b, ATT_HEADS, n_q),
        in_specs=[
            pl.BlockSpec(memory_space=pltpu.SMEM),
            pl.BlockSpec((1, QB, LANES), lambda bi, h, qi: (bi, qi, h)),
            pl.BlockSpec((1, s, LANES), lambda bi, h, qi: (bi, 0, h_off_k + h)),
            pl.BlockSpec((1, s, LANES), lambda bi, h, qi: (bi, 0, h_off_v + h)),
            pl.BlockSpec((1, N_BIAS_TILES, 2 * KC), lambda bi, h, qi: (h, 0, 0)),
            pl.BlockSpec((1, ATT_V_DIM), lambda bi, h, qi: (0, 0)),
        ] + side_specs,
        out_specs=[pl.BlockSpec((1, QB, ATT_V_DIM), lambda bi, h, qi: (bi, qi, h))] + side_specs,
        out_shape=[jax.ShapeDtypeStruct((b, s, ATT_WIDTH), BF16)]
        + [jax.ShapeDtypeStruct(arr.shape, BF16) for arr in side],
        scratch_shapes=[pltpu.VMEM((s, 2 * ATT_V_DIM), BF16),
                        pltpu.VMEM((N_BIAS_TILES, QB, KC), F32)],
        compiler_params=_cparams(("arbitrary", "arbitrary", "arbitrary")),
        name="attn",
    )(lam, proj3, proj3, proj3, bias_seq, subln_g, *side)


def _rel_bucket(rel):
    half = REL_BUCKETS // 2
    max_exact = half // 2
    ret = jnp.where(rel > 0, half, 0)
    n = jnp.abs(rel)
    nf = jnp.maximum(n, 1).astype(F32)
    large = max_exact + (jnp.log(nf / max_exact) / math.log(REL_MAX_DIST / max_exact)
                         * (half - max_exact)).astype(I32)
    large = jnp.minimum(large, half - 1)
    return ret + jnp.where(n < max_exact, n, large)


def _bias_sequences(rel_bias, s):
    assert REL_MAX_DIST <= KC and QB == KC
    period = 2 * KC
    j = jnp.arange(period, dtype=I32)
    off = jnp.where(j < KC, j, j - period)
    d = jnp.arange(-2, 3, dtype=I32)[:, None]
    rel = jnp.clip(d * KC + off[None, :], -(s - 1), s - 1)
    return jnp.transpose(rel_bias[_rel_bucket(rel)], (2, 0, 1)).astype(F32) * LOG2E


def _layernorm(v, g, b):
    mu = jnp.mean(v, axis=-1, keepdims=True)
    var = jnp.mean(jnp.square(v - mu), axis=-1, keepdims=True)
    return (v - mu) * lax.rsqrt(var + LN_EPS) * g + b


def _first_argmax(vals, nrows):
    mx = jnp.max(vals, axis=0, keepdims=True)
    row = lax.broadcasted_iota(I32, vals.shape, 0)
    idx = jnp.min(jnp.where(vals == mx, row, nrows), axis=0, keepdims=True)
    return mx, idx


def _mix_kernel(a_ref, u_ref, vg_ref, x_ref, wout_ref, sgw_ref, sgb_ref, sgg_ref, sgbeta_ref,
                g1_ref, b1_ref, wr_ref, br_ref,
                h1_ref, xs_hbm, ri_ref, rw_ref, cnt_ref,
                run_ref, hp_s, posv_ref, psm, ssem, rsem):
    i = pl.program_id(0)
    n = pl.num_programs(0)
    slot = i % 2
    prev = 1 - slot

    def pos_copy(sl):
        return pltpu.make_async_copy(posv_ref, psm.at[sl], ssem.at[sl])

    def row_copy(sl, which, j):
        return pltpu.make_async_copy(hp_s.at[sl, pl.ds(j, 1)],
                                     xs_hbm.at[pl.ds(psm[sl, which, j], 1)], rsem.at[sl])

    def drain(sl):
        def wait(j, c):
            row_copy(sl, 0, 0).wait()
            row_copy(sl, 1, 0).wait()
            return c
        lax.fori_loop(0, TM_MIX, wait, 0, unroll=8)

    @pl.when(i == 0)
    def _():
        run_ref[...] = jnp.zeros_like(run_ref)
        hp_s[1] = jnp.zeros(hp_s.shape[1:], hp_s.dtype)
        spare = N_EXPERTS * XS_CAP + lax.broadcasted_iota(I32, posv_ref.shape, 1)
        spare = spare + TM_MIX * lax.broadcasted_iota(I32, posv_ref.shape, 0)
        posv_ref[...] = jnp.where(spare < xs_hbm.shape[0], spare, 0)
        pos_copy(1).start()

    @pl.when(i > 0)
    def _():
        drain(slot)

    pos_copy(prev).wait()

    for j in range(TM_MIX):
        row_copy(prev, 0, j).start(priority=0)
        row_copy(prev, 1, j).start(priority=1)

    h1_ref[...] = jnp.dot(a_ref[...], wout_ref[:ATT_WIDTH, :], preferred_element_type=F32)

    n_chunks = TM_MIX // SG_CHUNK
    rows = []
    for c in range(n_chunks):
        cols = []
        for g in range(SG_GROUPS):
            rs = slice(c * SG_CHUNK, (c + 1) * SG_CHUNK)
            cs = slice(g * SG_DIM, (g + 1) * SG_DIM)
            vt = vg_ref[rs, cs].astype(F32)
            vn = _layernorm(vt, sgg_ref[g:g + 1, :], sgbeta_ref[g:g + 1, :])
            mixed = jnp.dot(sgw_ref[g], vn.astype(BF16), preferred_element_type=F32) + sgb_ref[g]
            cols.append((u_ref[rs, cs].astype(F32) * mixed).astype(BF16))
        rows.append(jnp.concatenate(cols, axis=1))
    sgate = jnp.concatenate(rows, axis=0)

    mix = h1_ref[...] + jnp.dot(sgate, wout_ref[ATT_WIDTH:, :], preferred_element_type=F32)
    h1 = _layernorm(DN_ALPHA * x_ref[...] + mix, g1_ref[...], b1_ref[...])
    h1_ref[...] = h1
    hp_s[slot] = h1

    nt = (((1,), (1,)), ((), ()))
    h1_hi = h1.astype(BF16)
    h1_lo = (h1 - h1_hi.astype(F32)).astype(BF16)
    both = lax.dot_general(wr_ref[...], h1_hi, nt, preferred_element_type=F32)
    lt = both[:NR_PAD] + both[NR_PAD:] + lax.dot_general(
        wr_ref[:NR_PAD, :], h1_lo, nt, preferred_element_type=F32)
    lt = lt + br_ref[...]
    g_logits = lt[0:N_GROUPS]
    gmax, g_idx = _first_argmax(g_logits, N_GROUPS)
    g_gate = 1.0 / jnp.sum(jnp.exp(g_logits - gmax), axis=0, keepdims=True)
    e_logits = jnp.zeros((EXPERTS_PER_GROUP, TM_MIX), F32)
    for g in range(N_GROUPS):
        lo = NR_EXPERT0 + g * EXPERTS_PER_GROUP
        e_logits = jnp.where(g_idx == g, lt[lo:lo + EXPERTS_PER_GROUP], e_logits)
    v1, i1 = _first_argmax(e_logits, EXPERTS_PER_GROUP)
    erow = lax.broadcasted_iota(I32, e_logits.shape, 0)
    v2, i2 = _first_argmax(jnp.where(erow == i1, -jnp.inf, e_logits), EXPERTS_PER_GROUP)
    t = jnp.exp(v2 - v1)
    w1 = g_gate / (1.0 + t)
    w2 = g_gate * t / (1.0 + t)
    e1 = g_idx * EXPERTS_PER_GROUP + i1
    e2 = g_idx * EXPERTS_PER_GROUP + i2

    xrow = lax.broadcasted_iota(I32, (N_EXPERTS, TM_MIX), 0)
    oh1 = xrow == e1
    oh2 = xrow == e2
    oh = jnp.where(jnp.logical_or(oh1, oh2), 1.0, 0.0)
    tr = lax.broadcasted_iota(I32, (TM_MIX, TM_MIX), 0)
    tc = lax.broadcasted_iota(I32, (TM_MIX, TM_MIX), 1)
    before = jnp.where(tr < tc, 1.0, 0.0).astype(BF16)
    cum = jnp.dot(oh.astype(BF16), before, preferred_element_type=F32)
    base = cum + run_ref[:, 0:1]
    rank1 = jnp.sum(jnp.where(oh1, base, 0.0), axis=0, keepdims=True).astype(I32)
    rank2 = jnp.sum(jnp.where(oh2, base, 0.0), axis=0, keepdims=True).astype(I32)
    run_ref[...] = run_ref[...] + jnp.sum(oh, axis=1, keepdims=True)
    cnt_ref[...] = run_ref[...].astype(I32)

    zi = jnp.zeros((ROUTE_ROWS - 4, TM_MIX), I32)
    ri_ref[...] = jnp.concatenate([e1, e2, rank1, rank2, zi], axis=0)
    zw = jnp.zeros((ROUTE_ROWS - 2, TM_MIX), F32)
    rw_ref[...] = jnp.concatenate([w1, w2, zw], axis=0)

    posv_ref[...] = jnp.concatenate([e1 * XS_CAP + rank1, e2 * XS_CAP + rank2, zi, zi[:2]], axis=0)
    pos_copy(slot).start()

    @pl.when(i == n - 1)
    def _():
        pos_copy(slot).wait()

        def last(j, c):
            row_copy(slot, 0, j).start(priority=0)
            row_copy(slot, 1, j).start(priority=1)
            return c
        lax.fori_loop(0, TM_MIX, last, 0, unroll=8)
        drain(prev)
        drain(slot)


def _mix(a2, proj2, x2, wout_b, sgw_b, sgb, sgg, sgbeta, g1, b1, wr, br):
    t, d = x2.shape
    assert XS_CAP >= t + TMX
    u_blk = (proj2.shape[1] - 2 * SG_WIDTH) // SG_WIDTH
    const = lambda *shape: pl.BlockSpec(shape, lambda i: (0,) * len(shape))
    return pl.pallas_call(
        _mix_kernel,
        grid=(t // TM_MIX,),
        in_specs=[
            pl.BlockSpec((TM_MIX, ATT_WIDTH), lambda i: (i, 0)),
            pl.BlockSpec((TM_MIX, SG_WIDTH), lambda i: (i, u_blk)),
            pl.BlockSpec((TM_MIX, SG_WIDTH), lambda i: (i, u_blk + 1)),
            pl.BlockSpec((TM_MIX, d), lambda i: (i, 0)),
            pl.BlockSpec((d, d), lambda i: (0, 0), pipeline_mode=pl.Buffered(1)),
            const(SG_GROUPS, SG_CHUNK, SG_CHUNK),
            const(SG_GROUPS, SG_CHUNK, 1),
            const(SG_GROUPS, SG_DIM),
            const(SG_GROUPS, SG_DIM),
            const(1, d),
            const(1, d),
            const(2 * NR_PAD, d),
            const(NR_PAD, 1),
        ],
        out_specs=[
            pl.BlockSpec((TM_MIX, d), lambda i: (i, 0)),
            pl.BlockSpec(memory_space=pl.ANY),
            pl.BlockSpec((ROUTE_ROWS, TM_MIX), lambda i: (0, i)),
            pl.BlockSpec((ROUTE_ROWS, TM_MIX), lambda i: (0, i)),
            const(N_EXPERTS, LANES),
        ],
        out_shape=[
            jax.ShapeDtypeStruct((t, d), F32),
            jax.ShapeDtypeStruct((N_EXPERTS * XS_CAP + 2 * TM_MIX, d), F32),
            jax.ShapeDtypeStruct((ROUTE_ROWS, t), I32),
            jax.ShapeDtypeStruct((ROUTE_ROWS, t), F32),
            jax.ShapeDtypeStruct((N_EXPERTS, LANES), I32),
        ],
        scratch_shapes=[pltpu.VMEM((N_EXPERTS, LANES), F32),
                        pltpu.VMEM((2, TM_MIX, d), F32),
                        pltpu.VMEM((ROUTE_ROWS, TM_MIX), I32),
                        pltpu.SMEM((2, ROUTE_ROWS, TM_MIX), I32),
                        pltpu.SemaphoreType.DMA((2,)),
                        pltpu.SemaphoreType.DMA((2,))],
        compiler_params=_cparams(("arbitrary",)),
        name="mix",
    )(a2, proj2, proj2, x2, wout_b, sgw_b, sgb, sgg, sgbeta, g1, b1, wr, br)


def _experts_kernel(ts_ref, nt_ref, nu_ref, trow_ref, nval_ref, x_hbm, wg_ref, wu_ref, wd_ref,
                    y_hbm, xbuf, ybuf, xsem, ysem):
    e = pl.program_id(0)
    d = wg_ref.shape[1]
    n_used = nu_ref[0]

    def x_copy(g):
        rows = pl.ds(pl.multiple_of(trow_ref[g], TMX), TMX)
        slot = lax.rem(g, N_XSLOTS)
        return pltpu.make_async_copy(x_hbm.at[rows], xbuf.at[slot], xsem.at[slot])

    def y_copy(g):
        rows = pl.ds(pl.multiple_of(g * TMX, TMX), TMX)
        slot = lax.rem(g, N_XSLOTS)
        return pltpu.make_async_copy(ybuf.at[slot], y_hbm.at[rows], ysem.at[slot])

    def start_x(g):
        @pl.when(g < n_used)
        def _():
            x_copy(g).start(priority=1)

    def free_y(g):
        @pl.when(g >= N_XSLOTS)
        def _():
            y_copy(g - N_XSLOTS).wait()

    def load(g):
        row = lax.broadcasted_iota(I32, xbuf.shape[1:], 0)
        return jnp.where(row < nval_ref[g], xbuf[lax.rem(g, N_XSLOTS)], 0.0).astype(BF16)

    def mlp(xb):
        gate = jnp.dot(xb, wg_ref[0], preferred_element_type=F32)
        up = jnp.dot(xb, wu_ref[0], preferred_element_type=F32)
        hmid = (gate * jax.nn.sigmoid(gate) * up).astype(BF16)
        return jnp.dot(hmid, wd_ref[0], preferred_element_type=F32)

    @pl.when(e == 0)
    def _():
        for g in range(X_AHEAD):
            start_x(g)

    def group(k, first):
        def body(t, carry):
            g = first(t)
            for j in range(k):
                x_copy(g + j).wait()
            for j in range(k):
                start_x(g + X_AHEAD + j)
                free_y(g + j)
            y = mlp(jnp.concatenate([load(g + j) for j in range(k)], axis=0))
            for j in range(k):
                ybuf[lax.rem(g + j, N_XSLOTS)] = y[j * TMX:(j + 1) * TMX]
                y_copy(g + j).start(priority=1)
            return carry
        return body

    nt = nt_ref[e]
    n3 = jnp.where(jnp.logical_and(nt >= 3, nt % 2 == 1), 1, 0)
    n2 = (nt - 3 * n3) // 2
    n1 = nt - 3 * n3 - 2 * n2
    base = ts_ref[e]
    lax.fori_loop(0, n3, group(3, lambda t: base), 0)
    lax.fori_loop(0, n2, group(2, lambda t: base + 3 * n3 + 2 * t), 0)
    lax.fori_loop(0, n1, group(1, lambda t: base + 3 * n3 + 2 * n2), 0)

    @pl.when(e == pl.num_programs(0) - 1)
    def _():
        for back in range(N_XSLOTS, 0, -1):
            @pl.when(n_used >= back)
            def _():
                y_copy(n_used - back).wait()


def _experts(tile_start, n_tiles_e, n_used, tile_row, tile_valid, xs, w_gate, w_up, w_down):
    w = xs.shape[1]
    d = w_gate.shape[1]
    n_rows = tile_row.shape[0] * TMX

    def w_map(e, ts, nt, nu, trow, nval):
        return (e, 0, 0)

    grid_spec = pltpu.PrefetchScalarGridSpec(
        num_scalar_prefetch=5,
        grid=(N_EXPERTS,),
        in_specs=[
            pl.BlockSpec(memory_space=pl.ANY),
            pl.BlockSpec((1, d, D_EXPERT), w_map),
            pl.BlockSpec((1, d, D_EXPERT), w_map),
            pl.BlockSpec((1, D_EXPERT, d), w_map),
        ],
        out_specs=pl.BlockSpec(memory_space=pl.ANY),
        scratch_shapes=[pltpu.VMEM((N_XSLOTS, TMX, w), F32),
                        pltpu.VMEM((N_XSLOTS, TMX, w), F32),
                        pltpu.SemaphoreType.DMA((N_XSLOTS,)),
                        pltpu.SemaphoreType.DMA((N_XSLOTS,))],
    )
    return pl.pallas_call(
        _experts_kernel,
        grid_spec=grid_spec,
        out_shape=jax.ShapeDtypeStruct((n_rows, w), F32),
        compiler_params=_cparams(("arbitrary",)),
        name="experts",
    )(tile_start, n_tiles_e, n_used, tile_row, tile_valid, xs, w_gate, w_up, w_down)


def _combine_kernel(pos_ref, posn_ref, y_hbm, h1_ref, wt_ref, g2_ref, b2_ref, o_ref, ybuf, sem):
    i = pl.program_id(0)
    n = pl.num_programs(0)
    slot = i % 2

    def row_copy(src, sl, which, j):
        return pltpu.make_async_copy(y_hbm.at[pl.ds(src, 1)],
                                     ybuf.at[sl, which, pl.ds(j, 1)], sem.at[sl])

    def issue(p_ref, sl):
        def body(j, c):
            row_copy(p_ref[0, j], sl, 0, j).start(priority=0)
            row_copy(p_ref[1, j], sl, 1, j).start(priority=1)
            return c
        lax.fori_loop(0, TM_OUT, body, 0, unroll=8)

    def drain(sl):
        def wait(j, c):
            row_copy(0, sl, 0, 0).wait()
            row_copy(0, sl, 1, 0).wait()
            return c
        lax.fori_loop(0, TM_OUT, wait, 0, unroll=8)

    @pl.when(i == 0)
    def _():
        issue(pos_ref, 0)

    drain(slot)

    for j in range(TM_OUT):
        row_copy(posn_ref[0, j], 1 - slot, 0, j).start(priority=0)
        row_copy(posn_ref[1, j], 1 - slot, 1, j).start(priority=1)

    ffn = wt_ref[:, 0:1] * ybuf[slot, 0] + wt_ref[:, 1:2] * ybuf[slot, 1]
    o_ref[...] = _layernorm(DN_ALPHA * h1_ref[...] + ffn, g2_ref[...], b2_ref[...])

    @pl.when(i == n - 1)
    def _():
        drain(1 - slot)


def _combine(pos3, ys, h1, wtok, g2, b2):
    t, d = h1.shape
    w = ys.shape[1]
    n = t // TM_OUT
    return pl.pallas_call(
        _combine_kernel,
        grid=(n,),
        in_specs=[
            pl.BlockSpec((None, 2, TM_OUT), lambda i: (i, 0, 0), memory_space=pltpu.SMEM),
            pl.BlockSpec((None, 2, TM_OUT), lambda i: (jnp.minimum(i + 1, n - 1), 0, 0),
                         memory_space=pltpu.SMEM),
            pl.BlockSpec(memory_space=pl.ANY),
            pl.BlockSpec((TM_OUT, d), lambda i: (i, 0)),
            pl.BlockSpec((TM_OUT, ROUTE_ROWS), lambda i: (i, 0)),
            pl.BlockSpec((1, d), lambda i: (0, 0)),
            pl.BlockSpec((1, d), lambda i: (0, 0)),
        ],
        out_specs=pl.BlockSpec((TM_OUT, d), lambda i: (i, 0)),
        out_shape=jax.ShapeDtypeStruct((t, d), F32),
        scratch_shapes=[pltpu.VMEM((2, 2, TM_OUT, w), F32),
                        pltpu.SemaphoreType.DMA((2,))],
        compiler_params=_cparams(("arbitrary",)),
        name="combine",
    )(pos3, pos3, ys, h1, wtok, g2, b2)


def _lambda_init(layer):
    return 0.8 - 0.6 * math.exp(-0.3 * layer)


def kernel(x, w_in, w_out, ln1_g, ln1_b, ln2_g, ln2_b, rel_bias, lam_q1, lam_k1, lam_q2, lam_k2,
           subln_g, sg_ln_g, sg_ln_b, sg_w, sg_b, w_router_group, b_router_group,
           w_router_expert, b_router_expert, w_exp_gate, w_exp_up, w_exp_down):
    b, s, d = x.shape
    t = b * s
    assert w_in.shape[0] == DEPTH == 1
    l = 0
    lam_init = _lambda_init(l)
    x2 = x.reshape(t, d)

    proj = _inproj(x2, w_in[l])

    lam = (jnp.exp(jnp.sum(lam_q1[l].astype(F32) * lam_k1[l].astype(F32)))
           - jnp.exp(jnp.sum(lam_q2[l].astype(F32) * lam_k2[l].astype(F32))) + lam_init)
    side = [w_exp_gate[l].reshape(-1, D_EXPERT), w_exp_up[l].reshape(-1, D_EXPERT),
            w_exp_down[l].reshape(-1, d), w_out[l]]
    a, wg_b, wu_b, wd_b, wout_b = _attention(
        proj.reshape(b, s, -1), _bias_sequences(rel_bias, s), lam.reshape(1),
        subln_g[l].reshape(1, -1), lam_init, side)

    wr = jnp.concatenate([w_router_group[l].T,
                          jnp.zeros((NR_EXPERT0 - N_GROUPS, d), F32),
                          jnp.transpose(w_router_expert[l], (0, 2, 1)).reshape(N_EXPERTS, d)],
                         axis=0)
    br = jnp.concatenate([b_router_group[l], jnp.zeros((NR_EXPERT0 - N_GROUPS,), F32),
                          b_router_expert[l].reshape(-1)]).reshape(NR_PAD, 1)
    wr_hi = wr.astype(BF16)
    wr = jnp.concatenate([wr_hi, (wr - wr_hi.astype(F32)).astype(BF16)], axis=0)
    h1, xs, ri, rw, cnt = _mix(
        a.reshape(t, -1), proj, x2, wout_b, sg_w[l].astype(BF16),
        sg_b[l].reshape(SG_GROUPS, SG_CHUNK, 1), sg_ln_g[l], sg_ln_b[l],
        ln1_g[l].reshape(1, d), ln1_b[l].reshape(1, d), wr, br)

    counts = cnt[:, 0]
    tiles_e = (counts + TMX - 1) // TMX
    tile_end = jnp.cumsum(tiles_e)
    tile_start = tile_end - tiles_e
    n_used = tile_end[-1:]
    max_tiles = (2 * t) // TMX + N_EXPERTS
    row_start = (tile_start * TMX).astype(I32)
    eids = jnp.arange(N_EXPERTS, dtype=I32)[:, None, None]
    pos = jnp.sum(jnp.where(ri[0:2][None] == eids, row_start[:, None, None], 0), axis=0) + ri[2:4]
    pos3 = pos.reshape(2, t // TM_OUT, TM_OUT).transpose(1, 0, 2)
    tile_ids = jnp.arange(max_tiles, dtype=I32)
    tile_e = jnp.minimum(jnp.sum(tile_end[None, :] <= tile_ids[:, None], axis=1), N_EXPERTS - 1)
    local = tile_ids - tile_start[tile_e]
    in_use = tile_ids < n_used[0]
    tile_row = jnp.where(in_use, tile_e * XS_CAP + local * TMX, 0).astype(I32)
    tile_valid = jnp.where(in_use, jnp.clip(counts[tile_e] - local * TMX, 0, TMX), 0).astype(I32)

    ys = _experts(tile_start.astype(I32), tiles_e.astype(I32), n_used.astype(I32),
                  tile_row, tile_valid, xs,
                  wg_b.reshape(N_EXPERTS, d, D_EXPERT), wu_b.reshape(N_EXPERTS, d, D_EXPERT),
                  wd_b.reshape(N_EXPERTS, D_EXPERT, d))
    out = _combine(pos3, ys, h1, rw.T, ln2_g[l].reshape(1, d), ln2_b[l].reshape(1, d))
    return out.reshape(b, s, d)
```

```python
import functools
import math

import jax
import jax.numpy as jnp
from jax import lax
from jax.experimental import pallas as pl
from jax.experimental.pallas import tpu as pltpu

F32 = jnp.float32
BF16 = jnp.bfloat16
I32 = jnp.int32

ATT_HEADS = 8
ATT_QK_DIM = 64
ATT_V_DIM = 128
ATT_WIDTH = ATT_HEADS * ATT_V_DIM
SG_GROUPS = 8
SG_DIM = 128
SG_WIDTH = SG_GROUPS * SG_DIM
SG_CHUNK = 128
REL_BUCKETS = 32
REL_MAX_DIST = 128
N_GROUPS = 4
EXPERTS_PER_GROUP = 8
N_EXPERTS = N_GROUPS * EXPERTS_PER_GROUP
D_EXPERT = 256
DEPTH = 1
DN_ALPHA = (2.0 * DEPTH) ** 0.25
LN_EPS = 1e-5
LOG2E = math.log2(math.e)

LANES = 128
SUBLANES = 8
VMEM_LIMIT_BYTES = 56 * 1024 * 1024

TM_IN = 1024
TN_IN = 1024
QB = 256
KC = 256
N_BIAS_TILES = 5
TM_MIX = 512
TMX = 256
TM_OUT = 256
X_AHEAD = 3
N_XSLOTS = 2 * X_AHEAD
ROUTE_ROWS = 8
XS_CAP = 8192 + TMX
NR_EXPERT0 = 8
NR_PAD = NR_EXPERT0 + 32


def _cparams(sem, flags=None):
    return pltpu.CompilerParams(dimension_semantics=sem, vmem_limit_bytes=VMEM_LIMIT_BYTES,
                                flags=flags)


def _inproj_kernel(x_ref, w_ref, o_ref, xb_ref, *, n_q, n_lin, q_scale):
    j = pl.program_id(1)

    @pl.when(j == 0)
    def _():
        xb_ref[...] = x_ref[...].astype(BF16)

    acc = jnp.dot(xb_ref[...], w_ref[...].astype(BF16), preferred_element_type=F32)

    @pl.when(j < n_q)
    def _():
        o_ref[...] = (acc * q_scale).astype(BF16)

    @pl.when(jnp.logical_and(j >= n_q, j < n_lin))
    def _():
        o_ref[...] = acc.astype(BF16)

    @pl.when(j >= n_lin)
    def _():
        o_ref[...] = jax.nn.gelu(acc).astype(BF16)


def _inproj(x2, w_in):
    t, d = x2.shape
    n = w_in.shape[1]
    n_q = (ATT_HEADS * 2 * ATT_QK_DIM) // TN_IN
    n_lin = (2 * ATT_HEADS * 2 * ATT_QK_DIM + ATT_WIDTH) // TN_IN
    kern = functools.partial(_inproj_kernel, n_q=n_q, n_lin=n_lin,
                             q_scale=(ATT_QK_DIM ** -0.5) * LOG2E)
    return pl.pallas_call(
        kern,
        grid=(t // TM_IN, n // TN_IN),
        in_specs=[pl.BlockSpec((TM_IN, d), lambda i, j: (i, 0)),
                  pl.BlockSpec((d, TN_IN), lambda i, j: (0, j))],
        out_specs=pl.BlockSpec((TM_IN, TN_IN), lambda i, j: (i, j)),
        out_shape=jax.ShapeDtypeStruct((t, n), BF16),
        scratch_shapes=[pltpu.VMEM((TM_IN, d), BF16)],
        compiler_params=_cparams(("arbitrary", "arbitrary")),
        name="inproj",
    )(x2, w_in)


def _attn_kernel(lam_ref, q_ref, k_ref, v_ref, bseq_ref, g_ref, *rest, n_kc, out_scale):
    n_side = (len(rest) - 3) // 2
    side_in, o_ref, side_out = rest[:n_side], rest[n_side], rest[n_side + 1:2 * n_side + 1]
    vaug_ref, bias_ref = rest[2 * n_side + 1:]
    qi = pl.program_id(2)

    @pl.when(qi == 0)
    def _():
        vaug_ref[:, :ATT_V_DIM] = v_ref[0]
        vaug_ref[:, ATT_V_DIM:] = jnp.ones((vaug_ref.shape[0], ATT_V_DIM), BF16)
        for t in range(N_BIAS_TILES):
            seq = jnp.broadcast_to(bseq_ref[0, t:t + 1, :], (QB, 2 * KC))
            bias_ref[t] = pltpu.roll(seq, 0, 1, stride=1, stride_axis=0)[:, :KC]

    q = q_ref[0]
    lane = lax.broadcasted_iota(I32, q.shape, 1)
    zero = jnp.zeros_like(q)
    qq = jnp.concatenate([jnp.where(lane < ATT_QK_DIM, q, zero),
                          jnp.where(lane >= ATT_QK_DIM, q, zero)], axis=0)

    for src, dst in zip(side_in, side_out):
        dst[...] = src[...].astype(dst.dtype)

    m = jnp.full((2 * QB, 1), -jnp.inf, F32)
    acc = jnp.zeros((2 * QB, 2 * ATT_V_DIM), F32)
    for kc in range(n_kc):
        kblk = k_ref[0, kc * KC:(kc + 1) * KC, :]
        s = lax.dot_general(qq, kblk, (((1,), (1,)), ((), ())),
                            preferred_element_type=F32)
        b = bias_ref[jnp.clip(kc - qi, -2, 2) + 2]
        s = s + jnp.concatenate([b, b], axis=0)
        m_new = jnp.maximum(m, jnp.max(s, axis=1, keepdims=True))
        p = jnp.exp2(s - m_new).astype(BF16)
        acc = acc * jnp.exp2(m - m_new) + jnp.dot(
            p, vaug_ref[kc * KC:(kc + 1) * KC, :], preferred_element_type=F32)
        m = m_new
    lam = lam_ref[0]
    o0 = acc[:QB, :ATT_V_DIM] / acc[:QB, ATT_V_DIM:ATT_V_DIM + 1]
    o1 = acc[QB:, :ATT_V_DIM] / acc[QB:, ATT_V_DIM:ATT_V_DIM + 1]
    of = o0 - lam * o1
    of = of * lax.rsqrt(jnp.mean(of * of, axis=-1, keepdims=True) + LN_EPS)
    o_ref[0] = (of * (g_ref[...] * out_scale)).astype(o_ref.dtype)


def _attention(proj3, bias_seq, lam, subln_g, lam_init, side):
    b, s, _ = proj3.shape
    n_kc = s // KC
    n_q = s // QB
    n_steps = b * ATT_HEADS * n_q
    h_off_k = ATT_HEADS
    h_off_v = 2 * ATT_HEADS
    kern = functools.partial(_attn_kernel, n_kc=n_kc, out_scale=1.0 - lam_init)

    side_specs = []
    for arr in side:
        rows, cols = arr.shape
        blk = max(rows // n_steps, 2 * SUBLANES)
        hold = blk * n_steps // rows
        assert rows % blk == 0 and hold * rows == blk * n_steps
        side_specs.append(pl.BlockSpec(
            (blk, cols),
            lambda bi, h, qi, hold=hold: (((bi * ATT_HEADS + h) * n_q + qi) // hold, 0)))
    return pl.pallas_call(
        kern,
        grid=(b, ATT_HEADS, n_q),
        in_specs=[
            pl.BlockSpec(memory_space=pltpu.SMEM),
            pl.BlockSpec((1, QB, LANES), lambda bi, h, qi: (bi, qi, h)),
            pl.BlockSpec((1, s, LANES), lambda bi, h, qi: (bi, 0, h_off_k + h)),
            pl.BlockSpec((1, s, LANES), lambda bi, h, qi: (bi, 0, h_off_v + h)),
            pl.BlockSpec((1, N_BIAS_TILES, 2 * KC), lambda bi, h, qi: (h, 0, 0)),
            pl.BlockSpec((1, ATT_V_DIM), lambda bi, h, qi: (0, 0)),
        ] + side_specs,
        out_specs=[pl.BlockSpec((1, QB, ATT_V_DIM), lambda bi, h, qi: (bi, qi, h))] + side_specs,
        out_shape=[jax.ShapeDtypeStruct((b, s, ATT_WIDTH), BF16)]
        + [jax.ShapeDtypeStruct(arr.shape, BF16) for arr in side],
        scratch_shapes=[pltpu.VMEM((s, 2 * ATT_V_DIM), BF16),
                        pltpu.VMEM((N_BIAS_TILES, QB, KC), F32)],
        compiler_params=_cparams(("arbitrary", "arbitrary", "arbitrary")),
        name="attn",
    )(lam, proj3, proj3, proj3, bias_seq, subln_g, *side)


def _rel_bucket(rel):
    half = REL_BUCKETS // 2
    max_exact = half // 2
    ret = jnp.where(rel > 0, half, 0)
    n = jnp.abs(rel)
    nf = jnp.maximum(n, 1).astype(F32)
    large = max_exact + (jnp.log(nf / max_exact) / math.log(REL_MAX_DIST / max_exact)
                         * (half - max_exact)).astype(I32)
    large = jnp.minimum(large, half - 1)
    return ret + jnp.where(n < max_exact, n, large)


def _bias_sequences(rel_bias, s):
    assert REL_MAX_DIST <= KC and QB == KC
    period = 2 * KC
    j = jnp.arange(period, dtype=I32)
    off = jnp.where(j < KC, j, j - period)
    d = jnp.arange(-2, 3, dtype=I32)[:, None]
    rel = jnp.clip(d * KC + off[None, :], -(s - 1), s - 1)
    return jnp.transpose(rel_bias[_rel_bucket(rel)], (2, 0, 1)).astype(F32) * LOG2E


def _layernorm(v, g, b):
    mu = jnp.mean(v, axis=-1, keepdims=True)
    var = jnp.mean(jnp.square(v - mu), axis=-1, keepdims=True)
    return (v - mu) * lax.rsqrt(var + LN_EPS) * g + b


def _first_argmax(vals, nrows):
    mx = jnp.max(vals, axis=0, keepdims=True)
    row = lax.broadcasted_iota(I32, vals.shape, 0)
    idx = jnp.min(jnp.where(vals == mx, row, nrows), axis=0, keepdims=True)
    return mx, idx


def _mix_kernel(a_ref, u_ref, vg_ref, x_ref, wout_ref, sgw_ref, sgb_ref, sgg_ref, sgbeta_ref,
                g1_ref, b1_ref, wr_ref, br_ref,
                h1_ref, xs_hbm, ri_ref, rw_ref, cnt_ref,
                run_ref, hp_s, posv_ref, psm, ssem, rsem):
    i = pl.program_id(0)
    n = pl.num_programs(0)
    slot = i % 2
    prev = 1 - slot

    def pos_copy(sl):
        return pltpu.make_async_copy(posv_ref, psm.at[sl], ssem.at[sl])

    def row_copy(sl, which, j):
        return pltpu.make_async_copy(hp_s.at[sl, pl.ds(j, 1)],
                                     xs_hbm.at[pl.ds(psm[sl, which, j], 1)], rsem.at[sl])

    def drain(sl):
        def wait(j, c):
            row_copy(sl, 0, 0).wait()
            row_copy(sl, 1, 0).wait()
            return c
        lax.fori_loop(0, TM_MIX, wait, 0, unroll=8)

    @pl.when(i == 0)
    def _():
        run_ref[...] = jnp.zeros_like(run_ref)
        hp_s[1] = jnp.zeros(hp_s.shape[1:], hp_s.dtype)
        spare = N_EXPERTS * XS_CAP + lax.broadcasted_iota(I32, posv_ref.shape, 1)
        spare = spare + TM_MIX * lax.broadcasted_iota(I32, posv_ref.shape, 0)
        posv_ref[...] = jnp.where(spare < xs_hbm.shape[0], spare, 0)
        pos_copy(1).start()

    @pl.when(i > 0)
    def _():
        drain(slot)

    pos_copy(prev).wait()

    for j in range(TM_MIX):
        row_copy(prev, 0, j).start(priority=0)
        row_copy(prev, 1, j).start(priority=1)

    h1_ref[...] = jnp.dot(a_ref[...], wout_ref[:ATT_WIDTH, :], preferred_element_type=F32)

    n_chunks = TM_MIX // SG_CHUNK
    rows = []
    for c in range(n_chunks):
        cols = []
        for g in range(SG_GROUPS):
            rs = slice(c * SG_CHUNK, (c + 1) * SG_CHUNK)
            cs = slice(g * SG_DIM, (g + 1) * SG_DIM)
            vt = vg_ref[rs, cs].astype(F32)
            vn = _layernorm(vt, sgg_ref[g:g + 1, :], sgbeta_ref[g:g + 1, :])
            mixed = jnp.dot(sgw_ref[g], vn.astype(BF16), preferred_element_type=F32) + sgb_ref[g]
            cols.append((u_ref[rs, cs].astype(F32) * mixed).astype(BF16))
        rows.append(jnp.concatenate(cols, axis=1))
    sgate = jnp.concatenate(rows, axis=0)

    mix = h1_ref[...] + jnp.dot(sgate, wout_ref[ATT_WIDTH:, :], preferred_element_type=F32)
    h1 = _layernorm(DN_ALPHA * x_ref[...] + mix, g1_ref[...], b1_ref[...])
    h1_ref[...] = h1
    hp_s[slot] = h1.astype(BF16).reshape(hp_s.shape[1:])

    nt = (((1,), (1,)), ((), ()))
    h1_hi = h1.astype(BF16)
    h1_lo = (h1 - h1_hi.astype(F32)).astype(BF16)
    both = lax.dot_general(wr_ref[...], h1_hi, nt, preferred_element_type=F32)
    lt = both[:NR_PAD] + both[NR_PAD:] + lax.dot_general(
        wr_ref[:NR_PAD, :], h1_lo, nt, preferred_element_type=F32)
    lt = lt + br_ref[...]
    g_logits = lt[0:N_GROUPS]
    gmax, g_idx = _first_argmax(g_logits, N_GROUPS)
    g_gate = 1.0 / jnp.sum(jnp.exp(g_logits - gmax), axis=0, keepdims=True)
    e_logits = jnp.zeros((EXPERTS_PER_GROUP, TM_MIX), F32)
    for g in range(N_GROUPS):
        lo = NR_EXPERT0 + g * EXPERTS_PER_GROUP
        e_logits = jnp.where(g_idx == g, lt[lo:lo + EXPERTS_PER_GROUP], e_logits)
    v1, i1 = _first_argmax(e_logits, EXPERTS_PER_GROUP)
    erow = lax.broadcasted_iota(I32, e_logits.shape, 0)
    v2, i2 = _first_argmax(jnp.where(erow == i1, -jnp.inf, e_logits), EXPERTS_PER_GROUP)
    t = jnp.exp(v2 - v1)
    w1 = g_gate / (1.0 + t)
    w2 = g_gate * t / (1.0 + t)
    e1 = g_idx * EXPERTS_PER_GROUP + i1
    e2 = g_idx * EXPERTS_PER_GROUP + i2

    xrow = lax.broadcasted_iota(I32, (N_EXPERTS, TM_MIX), 0)
    oh1 = xrow == e1
    oh2 = xrow == e2
    oh = jnp.where(jnp.logical_or(oh1, oh2), 1.0, 0.0)
    tr = lax.broadcasted_iota(I32, (TM_MIX, TM_MIX), 0)
    tc = lax.broadcasted_iota(I32, (TM_MIX, TM_MIX), 1)
    before = jnp.where(tr < tc, 1.0, 0.0).astype(BF16)
    cum = jnp.dot(oh.astype(BF16), before, preferred_element_type=F32)
    base = cum + run_ref[:, 0:1]
    rank1 = jnp.sum(jnp.where(oh1, base, 0.0), axis=0, keepdims=True).astype(I32)
    rank2 = jnp.sum(jnp.where(oh2, base, 0.0), axis=0, keepdims=True).astype(I32)
    run_ref[...] = run_ref[...] + jnp.sum(oh, axis=1, keepdims=True)
    cnt_ref[...] = run_ref[...].astype(I32)

    zi = jnp.zeros((ROUTE_ROWS - 4, TM_MIX), I32)
    ri_ref[...] = jnp.concatenate([e1, e2, rank1, rank2, zi], axis=0)
    zw = jnp.zeros((ROUTE_ROWS - 2, TM_MIX), F32)
    rw_ref[...] = jnp.concatenate([w1, w2, zw], axis=0)

    posv_ref[...] = jnp.concatenate([e1 * XS_CAP + rank1, e2 * XS_CAP + rank2, zi, zi[:2]], axis=0)
    pos_copy(slot).start()

    @pl.when(i == n - 1)
    def _():
        pos_copy(slot).wait()

        def last(j, c):
            row_copy(slot, 0, j).start(priority=0)
            row_copy(slot, 1, j).start(priority=1)
            return c
        lax.fori_loop(0, TM_MIX, last, 0, unroll=8)
        drain(prev)
        drain(slot)


def _mix(a2, proj2, x2, wout_b, sgw_b, sgb, sgg, sgbeta, g1, b1, wr, br):
    t, d = x2.shape
    assert XS_CAP >= t + TMX
    u_blk = (proj2.shape[1] - 2 * SG_WIDTH) // SG_WIDTH
    const = lambda *shape: pl.BlockSpec(shape, lambda i: (0,) * len(shape))
    return pl.pallas_call(
        _mix_kernel,
        grid=(t // TM_MIX,),
        in_specs=[
            pl.BlockSpec((TM_MIX, ATT_WIDTH), lambda i: (i, 0)),
            pl.BlockSpec((TM_MIX, SG_WIDTH), lambda i: (i, u_blk)),
            pl.BlockSpec((TM_MIX, SG_WIDTH), lambda i: (i, u_blk + 1)),
            pl.BlockSpec((TM_MIX, d), lambda i: (i, 0)),
            pl.BlockSpec((d, d), lambda i: (0, 0), pipeline_mode=pl.Buffered(1)),
            const(SG_GROUPS, SG_CHUNK, SG_CHUNK),
            const(SG_GROUPS, SG_CHUNK, 1),
            const(SG_GROUPS, SG_DIM),
            const(SG_GROUPS, SG_DIM),
            const(1, d),
            const(1, d),
            const(2 * NR_PAD, d),
            const(NR_PAD, 1),
        ],
        out_specs=[
            pl.BlockSpec((TM_MIX, d), lambda i: (i, 0)),
            pl.BlockSpec(memory_space=pl.ANY),
            pl.BlockSpec((ROUTE_ROWS, TM_MIX), lambda i: (0, i)),
            pl.BlockSpec((ROUTE_ROWS, TM_MIX), lambda i: (0, i)),
            const(N_EXPERTS, LANES),
        ],
        out_shape=[
            jax.ShapeDtypeStruct((t, d), F32),
            jax.ShapeDtypeStruct((N_EXPERTS * XS_CAP + 2 * TM_MIX, d // LANES, LANES), BF16),
            jax.ShapeDtypeStruct((ROUTE_ROWS, t), I32),
            jax.ShapeDtypeStruct((ROUTE_ROWS, t), F32),
            jax.ShapeDtypeStruct((N_EXPERTS, LANES), I32),
        ],
        scratch_shapes=[pltpu.VMEM((N_EXPERTS, LANES), F32),
                        pltpu.VMEM((2, TM_MIX, d // LANES, LANES), BF16),
                        pltpu.VMEM((ROUTE_ROWS, TM_MIX), I32),
                        pltpu.SMEM((2, ROUTE_ROWS, TM_MIX), I32),
                        pltpu.SemaphoreType.DMA((2,)),
                        pltpu.SemaphoreType.DMA((2,))],
        compiler_params=_cparams(("arbitrary",)),
        name="mix",
    )(a2, proj2, proj2, x2, wout_b, sgw_b, sgb, sgg, sgbeta, g1, b1, wr, br)


def _experts_kernel(ts_ref, nt_ref, nu_ref, trow_ref, nval_ref, x_hbm, wg_ref, wu_ref, wd_ref,
                    y_hbm, xbuf, ybuf, xsem, ysem):
    e = pl.program_id(0)
    d = wg_ref.shape[1]
    n_used = nu_ref[0]

    def x_copy(g):
        rows = pl.ds(pl.multiple_of(trow_ref[g], TMX), TMX)
        slot = lax.rem(g, N_XSLOTS)
        return pltpu.make_async_copy(x_hbm.at[rows], xbuf.at[slot], xsem.at[slot])

    def y_copy(g):
        rows = pl.ds(pl.multiple_of(g * TMX, TMX), TMX)
        slot = lax.rem(g, N_XSLOTS)
        return pltpu.make_async_copy(ybuf.at[slot], y_hbm.at[rows], ysem.at[slot])

    def start_x(g):
        @pl.when(g < n_used)
        def _():
            x_copy(g).start(priority=1)

    def free_y(g):
        @pl.when(g >= N_XSLOTS)
        def _():
            y_copy(g - N_XSLOTS).wait()

    def load(g):
        xt = xbuf[lax.rem(g, N_XSLOTS)].reshape(TMX, d)
        row = lax.broadcasted_iota(I32, xt.shape, 0)
        return jnp.where(row < nval_ref[g], xt, jnp.zeros_like(xt))

    def mlp(xb):
        gate = jnp.dot(xb, wg_ref[0], preferred_element_type=F32)
        up = jnp.dot(xb, wu_ref[0], preferred_element_type=F32)
        hmid = (gate * jax.nn.sigmoid(gate) * up).astype(BF16)
        return jnp.dot(hmid, wd_ref[0], preferred_element_type=F32)

    @pl.when(e == 0)
    def _():
        for g in range(X_AHEAD):
            start_x(g)

    def group(k, first):
        def body(t, carry):
            g = first(t)
            for j in range(k):
                x_copy(g + j).wait()
            for j in range(k):
                start_x(g + X_AHEAD + j)
                free_y(g + j)
            y = mlp(jnp.concatenate([load(g + j) for j in range(k)], axis=0))
            for j in range(k):
                ybuf[lax.rem(g + j, N_XSLOTS)] = y[j * TMX:(j + 1) * TMX].astype(BF16).reshape(
                    ybuf.shape[1:])
                y_copy(g + j).start(priority=1)
            return carry
        return body

    nt = nt_ref[e]
    n3 = jnp.where(jnp.logical_and(nt >= 3, nt % 2 == 1), 1, 0)
    n2 = (nt - 3 * n3) // 2
    n1 = nt - 3 * n3 - 2 * n2
    base = ts_ref[e]
    lax.fori_loop(0, n3, group(3, lambda t: base), 0)
    lax.fori_loop(0, n2, group(2, lambda t: base + 3 * n3 + 2 * t), 0)
    lax.fori_loop(0, n1, group(1, lambda t: base + 3 * n3 + 2 * n2), 0)

    @pl.when(e == pl.num_programs(0) - 1)
    def _():
        for back in range(N_XSLOTS, 0, -1):
            @pl.when(n_used >= back)
            def _():
                y_copy(n_used - back).wait()


def _experts(tile_start, n_tiles_e, n_used, tile_row, tile_valid, xs, w_gate, w_up, w_down):
    d = w_gate.shape[1]
    n_rows = tile_row.shape[0] * TMX

    def w_map(e, ts, nt, nu, trow, nval):
        return (e, 0, 0)

    grid_spec = pltpu.PrefetchScalarGridSpec(
        num_scalar_prefetch=5,
        grid=(N_EXPERTS,),
        in_specs=[
            pl.BlockSpec(memory_space=pl.ANY),
            pl.BlockSpec((1, d, D_EXPERT), w_map),
            pl.BlockSpec((1, d, D_EXPERT), w_map),
            pl.BlockSpec((1, D_EXPERT, d), w_map),
        ],
        out_specs=pl.BlockSpec(memory_space=pl.ANY),
        scratch_shapes=[pltpu.VMEM((N_XSLOTS, TMX) + xs.shape[1:], BF16),
                        pltpu.VMEM((N_XSLOTS, TMX) + xs.shape[1:], BF16),
                        pltpu.SemaphoreType.DMA((N_XSLOTS,)),
                        pltpu.SemaphoreType.DMA((N_XSLOTS,))],
    )
    return pl.pallas_call(
        _experts_kernel,
        grid_spec=grid_spec,
        out_shape=jax.ShapeDtypeStruct((n_rows,) + xs.shape[1:], BF16),
        compiler_params=_cparams(("arbitrary",)),
        name="experts",
    )(tile_start, n_tiles_e, n_used, tile_row, tile_valid, xs, w_gate, w_up, w_down)


def _combine_kernel(pos_ref, posn_ref, y_hbm, h1_ref, wt_ref, g2_ref, b2_ref, o_ref, ybuf, sem):
    i = pl.program_id(0)
    n = pl.num_programs(0)
    slot = i % 2

    def row_copy(src, sl, which, j):
        return pltpu.make_async_copy(y_hbm.at[pl.ds(src, 1)],
                                     ybuf.at[sl, which, pl.ds(j, 1)], sem.at[sl])

    def issue(p_ref, sl):
        def body(j, c):
            row_copy(p_ref[0, j], sl, 0, j).start(priority=0)
            row_copy(p_ref[1, j], sl, 1, j).start(priority=1)
            return c
        lax.fori_loop(0, TM_OUT, body, 0, unroll=8)

    def drain(sl):
        def wait(j, c):
            row_copy(0, sl, 0, 0).wait()
            row_copy(0, sl, 1, 0).wait()
            return c
        lax.fori_loop(0, TM_OUT, wait, 0, unroll=8)

    @pl.when(i == 0)
    def _():
        issue(pos_ref, 0)

    drain(slot)

    for j in range(TM_OUT):
        row_copy(posn_ref[0, j], 1 - slot, 0, j).start(priority=0)
        row_copy(posn_ref[1, j], 1 - slot, 1, j).start(priority=1)

    y1 = ybuf[slot, 0].reshape(h1_ref.shape).astype(F32)
    y2 = ybuf[slot, 1].reshape(h1_ref.shape).astype(F32)
    ffn = wt_ref[:, 0:1] * y1 + wt_ref[:, 1:2] * y2
    o_ref[...] = _layernorm(DN_ALPHA * h1_ref[...] + ffn, g2_ref[...], b2_ref[...])

    @pl.when(i == n - 1)
    def _():
        drain(1 - slot)


def _combine(pos3, ys, h1, wtok, g2, b2):
    t, d = h1.shape
    n = t // TM_OUT
    return pl.pallas_call(
        _combine_kernel,
        grid=(n,),
        in_specs=[
            pl.BlockSpec((None, 2, TM_OUT), lambda i: (i, 0, 0), memory_space=pltpu.SMEM),
            pl.BlockSpec((None, 2, TM_OUT), lambda i: (jnp.minimum(i + 1, n - 1), 0, 0),
                         memory_space=pltpu.SMEM),
            pl.BlockSpec(memory_space=pl.ANY),
            pl.BlockSpec((TM_OUT, d), lambda i: (i, 0)),
            pl.BlockSpec((TM_OUT, ROUTE_ROWS), lambda i: (i, 0)),
            pl.BlockSpec((1, d), lambda i: (0, 0)),
            pl.BlockSpec((1, d), lambda i: (0, 0)),
        ],
        out_specs=pl.BlockSpec((TM_OUT, d), lambda i: (i, 0)),
        out_shape=jax.ShapeDtypeStruct((t, d), F32),
        scratch_shapes=[pltpu.VMEM((2, 2, TM_OUT) + ys.shape[1:], BF16),
                        pltpu.SemaphoreType.DMA((2,))],
        compiler_params=_cparams(("arbitrary",)),
        name="combine",
    )(pos3, pos3, ys, h1, wtok, g2, b2)


def _lambda_init(layer):
    return 0.8 - 0.6 * math.exp(-0.3 * layer)


def kernel(x, w_in, w_out, ln1_g, ln1_b, ln2_g, ln2_b, rel_bias, lam_q1, lam_k1, lam_q2, lam_k2,
           subln_g, sg_ln_g, sg_ln_b, sg_w, sg_b, w_router_group, b_router_group,
           w_router_expert, b_router_expert, w_exp_gate, w_exp_up, w_exp_down):
    b, s, d = x.shape
    t = b * s
    assert w_in.shape[0] == DEPTH == 1
    l = 0
    lam_init = _lambda_init(l)
    x2 = x.reshape(t, d)

    proj = _inproj(x2, w_in[l])

    lam = (jnp.exp(jnp.sum(lam_q1[l].astype(F32) * lam_k1[l].astype(F32)))
           - jnp.exp(jnp.sum(lam_q2[l].astype(F32) * lam_k2[l].astype(F32))) + lam_init)
    side = [w_exp_gate[l].reshape(-1, D_EXPERT), w_exp_up[l].reshape(-1, D_EXPERT),
            w_exp_down[l].reshape(-1, d), w_out[l]]
    a, wg_b, wu_b, wd_b, wout_b = _attention(
        proj.reshape(b, s, -1), _bias_sequences(rel_bias, s), lam.reshape(1),
        subln_g[l].reshape(1, -1), lam_init, side)

    wr = jnp.concatenate([w_router_group[l].T,
                          jnp.zeros((NR_EXPERT0 - N_GROUPS, d), F32),
                          jnp.transpose(w_router_expert[l], (0, 2, 1)).reshape(N_EXPERTS, d)],
                         axis=0)
    br = jnp.concatenate([b_router_group[l], jnp.zeros((NR_EXPERT0 - N_GROUPS,), F32),
                          b_router_expert[l].reshape(-1)]).reshape(NR_PAD, 1)
    wr_hi = wr.astype(BF16)
    wr = jnp.concatenate([wr_hi, (wr - wr_hi.astype(F32)).astype(BF16)], axis=0)
    h1, xs, ri, rw, cnt = _mix(
        a.reshape(t, -1), proj, x2, wout_b, sg_w[l].astype(BF16),
        sg_b[l].reshape(SG_GROUPS, SG_CHUNK, 1), sg_ln_g[l], sg_ln_b[l],
        ln1_g[l].reshape(1, d), ln1_b[l].reshape(1, d), wr, br)

    counts = cnt[:, 0]
    tiles_e = (counts + TMX - 1) // TMX
    tile_end = jnp.cumsum(tiles_e)
    tile_start = tile_end - tiles_e
    n_used = tile_end[-1:]
    max_tiles = (2 * t) // TMX + N_EXPERTS
    row_start = (tile_start * TMX).astype(I32)
    eids = jnp.arange(N_EXPERTS, dtype=I32)[:, None, None]
    pos = jnp.sum(jnp.where(ri[0:2][None] == eids, row_start[:, None, None], 0), axis=0) + ri[2:4]
    pos3 = pos.reshape(2, t // TM_OUT, TM_OUT).transpose(1, 0, 2)
    tile_ids = jnp.arange(max_tiles, dtype=I32)
    tile_e = jnp.minimum(jnp.sum(tile_end[None, :] <= tile_ids[:, None], axis=1), N_EXPERTS - 1)
    local = tile_ids - tile_start[tile_e]
    in_use = tile_ids < n_used[0]
    tile_row = jnp.where(in_use, tile_e * XS_CAP + local * TMX, 0).astype(I32)
    tile_valid = jnp.where(in_use, jnp.clip(counts[tile_e] - local * TMX, 0, TMX), 0).astype(I32)

    ys = _experts(tile_start.astype(I32), tiles_e.astype(I32), n_used.astype(I32),
                  tile_row, tile_valid, xs,
                  wg_b.reshape(N_EXPERTS, d, D_EXPERT), wu_b.reshape(N_EXPERTS, d, D_EXPERT),
                  wd_b.reshape(N_EXPERTS, D_EXPERT, d))
    out = _combine(pos3, ys, h1, rw.T, ln2_g[l].reshape(1, d), ln2_b[l].reshape(1, d))
    return out.reshape(b, s, d)
```

```python
import functools
import math

import jax
import jax.numpy as jnp
from jax import lax
from jax.experimental import pallas as pl
from jax.experimental.pallas import tpu as pltpu

F32 = jnp.float32
BF16 = jnp.bfloat16
I32 = jnp.int32

ATT_HEADS = 8
ATT_QK_DIM = 64
ATT_V_DIM = 128
ATT_WIDTH = ATT_HEADS * ATT_V_DIM
SG_GROUPS = 8
SG_DIM = 128
SG_WIDTH = SG_GROUPS * SG_DIM
SG_CHUNK = 128
REL_BUCKETS = 32
REL_MAX_DIST = 128
N_GROUPS = 4
EXPERTS_PER_GROUP = 8
N_EXPERTS = N_GROUPS * EXPERTS_PER_GROUP
D_EXPERT = 256
DEPTH = 1
DN_ALPHA = (2.0 * DEPTH) ** 0.25
LN_EPS = 1e-5
LOG2E = math.log2(math.e)

LANES = 128
SUBLANES = 8
VMEM_LIMIT_BYTES = 56 * 1024 * 1024

TM_IN = 1024
TN_IN = 1024
QB = 256
KC = 256
N_BIAS_TILES = 5
TM_MIX = 512
TMX = 256
TM_OUT = 512
X_AHEAD = 3
N_XSLOTS = 2 * X_AHEAD
ROUTE_ROWS = 8
XS_CAP = 8192 + TMX
NR_EXPERT0 = 8
NR_PAD = NR_EXPERT0 + 32


def _cparams(sem, flags=None):
    return pltpu.CompilerParams(dimension_semantics=sem, vmem_limit_bytes=VMEM_LIMIT_BYTES,
                                flags=flags)


def _inproj_kernel(x_ref, w_ref, o_ref, xb_ref, *, n_q, n_lin, q_scale):
    j = pl.program_id(1)

    @pl.when(j == 0)
    def _():
        xb_ref[...] = x_ref[...].astype(BF16)

    acc = jnp.dot(xb_ref[...], w_ref[...].astype(BF16), preferred_element_type=F32)

    @pl.when(j < n_q)
    def _():
        o_ref[...] = (acc * q_scale).astype(BF16)

    @pl.when(jnp.logical_and(j >= n_q, j < n_lin))
    def _():
        o_ref[...] = acc.astype(BF16)

    @pl.when(j >= n_lin)
    def _():
        o_ref[...] = jax.nn.gelu(acc).astype(BF16)


def _inproj(x2, w_in):
    t, d = x2.shape
    n = w_in.shape[1]
    n_q = (ATT_HEADS * 2 * ATT_QK_DIM) // TN_IN
    n_lin = (2 * ATT_HEADS * 2 * ATT_QK_DIM + ATT_WIDTH) // TN_IN
    kern = functools.partial(_inproj_kernel, n_q=n_q, n_lin=n_lin,
                             q_scale=(ATT_QK_DIM ** -0.5) * LOG2E)
    return pl.pallas_call(
        kern,
        grid=(t // TM_IN, n // TN_IN),
        in_specs=[pl.BlockSpec((TM_IN, d), lambda i, j: (i, 0)),
                  pl.BlockSpec((d, TN_IN), lambda i, j: (0, j))],
        out_specs=pl.BlockSpec((TM_IN, TN_IN), lambda i, j: (i, j)),
        out_shape=jax.ShapeDtypeStruct((t, n), BF16),
        scratch_shapes=[pltpu.VMEM((TM_IN, d), BF16)],
        compiler_params=_cparams(("arbitrary", "arbitrary")),
        name="inproj",
    )(x2, w_in)


def _attn_kernel(lam_ref, q_ref, k_ref, v_ref, bseq_ref, g_ref, *rest, n_kc, out_scale):
    n_side = (len(rest) - 3) // 2
    side_in, o_ref, side_out = rest[:n_side], rest[n_side], rest[n_side + 1:2 * n_side + 1]
    vaug_ref, bias_ref = rest[2 * n_side + 1:]
    qi = pl.program_id(2)

    @pl.when(qi == 0)
    def _():
        vaug_ref[:, :ATT_V_DIM] = v_ref[0]
        vaug_ref[:, ATT_V_DIM:] = jnp.ones((vaug_ref.shape[0], ATT_V_DIM), BF16)
        for t in range(N_BIAS_TILES):
            seq = jnp.broadcast_to(bseq_ref[0, t:t + 1, :], (QB, 2 * KC))
            bias_ref[t] = pltpu.roll(seq, 0, 1, stride=1, stride_axis=0)[:, :KC]

    q = q_ref[0]
    lane = lax.broadcasted_iota(I32, q.shape, 1)
    zero = jnp.zeros_like(q)
    qq = jnp.concatenate([jnp.where(lane < ATT_QK_DIM, q, zero),
                          jnp.where(lane >= ATT_QK_DIM, q, zero)], axis=0)

    for src, dst in zip(side_in, side_out):
        dst[...] = src[...].astype(dst.dtype)

    m = jnp.full((2 * QB, 1), -jnp.inf, F32)
    acc = jnp.zeros((2 * QB, 2 * ATT_V_DIM), F32)
    for kc in range(n_kc):
        kblk = k_ref[0, kc * KC:(kc + 1) * KC, :]
        s = lax.dot_general(qq, kblk, (((1,), (1,)), ((), ())),
                            preferred_element_type=F32)
        b = bias_ref[jnp.clip(kc - qi, -2, 2) + 2]
        s = s + jnp.concatenate([b, b], axis=0)
        m_new = jnp.maximum(m, jnp.max(s, axis=1, keepdims=True))
        p = jnp.exp2(s - m_new).astype(BF16)
        acc = acc * jnp.exp2(m - m_new) + jnp.dot(
            p, vaug_ref[kc * KC:(kc + 1) * KC, :], preferred_element_type=F32)
        m = m_new
    lam = lam_ref[0]
    o0 = acc[:QB, :ATT_V_DIM] / acc[:QB, ATT_V_DIM:ATT_V_DIM + 1]
    o1 = acc[QB:, :ATT_V_DIM] / acc[QB:, ATT_V_DIM:ATT_V_DIM + 1]
    of = o0 - lam * o1
    of = of * lax.rsqrt(jnp.mean(of * of, axis=-1, keepdims=True) + LN_EPS)
    o_ref[0] = (of * (g_ref[...] * out_scale)).astype(o_ref.dtype)


def _attention(proj3, bias_seq, lam, subln_g, lam_init, side):
    b, s, _ = proj3.shape
    n_kc = s // KC
    n_q = s // QB
    n_steps = b * ATT_HEADS * n_q
    h_off_k = ATT_HEADS
    h_off_v = 2 * ATT_HEADS
    kern = functools.partial(_attn_kernel, n_kc=n_kc, out_scale=1.0 - lam_init)

    side_specs = []
    for arr in side:
        rows, cols = arr.shape
        blk = max(rows // n_steps, 2 * SUBLANES)
        hold = blk * n_steps // rows
        assert rows % blk == 0 and hold * rows == blk * n_steps
        side_specs.append(pl.BlockSpec(
            (blk, cols),
            lambda bi, h, qi, hold=hold: (((bi * ATT_HEADS + h) * n_q + qi) // hold, 0)))
    return pl.pallas_call(
        kern,
        grid=(b, ATT_HEADS, n_q),
        in_specs=[
            pl.BlockSpec(memory_space=pltpu.SMEM),
            pl.BlockSpec((1, QB, LANES), lambda bi, h, qi: (bi, qi, h)),
            pl.BlockSpec((1, s, LANES), lambda bi, h, qi: (bi, 0, h_off_k + h)),
            pl.BlockSpec((1, s, LANES), lambda bi, h, qi: (bi, 0, h_off_v + h)),
            pl.BlockSpec((1, N_BIAS_TILES, 2 * KC), lambda bi, h, qi: (h, 0, 0)),
            pl.BlockSpec((1, ATT_V_DIM), lambda bi, h, qi: (0, 0)),
        ] + side_specs,
        out_specs=[pl.BlockSpec((1, QB, ATT_V_DIM), lambda bi, h, qi: (bi, qi, h))] + side_specs,
        out_shape=[jax.ShapeDtypeStruct((b, s, ATT_WIDTH), BF16)]
        + [jax.ShapeDtypeStruct(arr.shape, BF16) for arr in side],
        scratch_shapes=[pltpu.VMEM((s, 2 * ATT_V_DIM), BF16),
                        pltpu.VMEM((N_BIAS_TILES, QB, KC), F32)],
        compiler_params=_cparams(("arbitrary", "arbitrary", "arbitrary")),
        name="attn",
    )(lam, proj3, proj3, proj3, bias_seq, subln_g, *side)


def _rel_bucket(rel):
    half = REL_BUCKETS // 2
    max_exact = half // 2
    ret = jnp.where(rel > 0, half, 0)
    n = jnp.abs(rel)
    nf = jnp.maximum(n, 1).astype(F32)
    large = max_exact + (jnp.log(nf / max_exact) / math.log(REL_MAX_DIST / max_exact)
                         * (half - max_exact)).astype(I32)
    large = jnp.minimum(large, half - 1)
    return ret + jnp.where(n < max_exact, n, large)


def _bias_sequences(rel_bias, s):
    assert REL_MAX_DIST <= KC and QB == KC
    period = 2 * KC
    j = jnp.arange(period, dtype=I32)
    off = jnp.where(j < KC, j, j - period)
    d = jnp.arange(-2, 3, dtype=I32)[:, None]
    rel = jnp.clip(d * KC + off[None, :], -(s - 1), s - 1)
    return jnp.transpose(rel_bias[_rel_bucket(rel)], (2, 0, 1)).astype(F32) * LOG2E


def _layernorm(v, g, b):
    mu = jnp.mean(v, axis=-1, keepdims=True)
    var = jnp.mean(jnp.square(v - mu), axis=-1, keepdims=True)
    return (v - mu) * lax.rsqrt(var + LN_EPS) * g + b


def _first_argmax(vals, nrows):
    mx = jnp.max(vals, axis=0, keepdims=True)
    row = lax.broadcasted_iota(I32, vals.shape, 0)
    idx = jnp.min(jnp.where(vals == mx, row, nrows), axis=0, keepdims=True)
    return mx, idx


def _mix_kernel(a_ref, u_ref, vg_ref, x_ref, wout_ref, sgw_ref, sgb_ref, sgg_ref, sgbeta_ref,
                g1_ref, b1_ref, wr_ref, br_ref,
                h1_ref, xs_hbm, ri_ref, rw_ref, cnt_ref,
                run_ref, hp_s, posv_ref, psm, ssem, rsem):
    i = pl.program_id(0)
    n = pl.num_programs(0)
    slot = i % 2
    prev = 1 - slot

    def pos_copy(sl):
        return pltpu.make_async_copy(posv_ref, psm.at[sl], ssem.at[sl])

    def row_copy(sl, which, j):
        return pltpu.make_async_copy(hp_s.at[sl, pl.ds(j, 1)],
                                     xs_hbm.at[pl.ds(psm[sl, which, j], 1)], rsem.at[sl])

    def drain(sl):
        def wait(j, c):
            row_copy(sl, 0, 0).wait()
            row_copy(sl, 1, 0).wait()
            return c
        lax.fori_loop(0, TM_MIX, wait, 0, unroll=8)

    @pl.when(i == 0)
    def _():
        run_ref[...] = jnp.zeros_like(run_ref)
        hp_s[1] = jnp.zeros(hp_s.shape[1:], hp_s.dtype)
        spare = N_EXPERTS * XS_CAP + lax.broadcasted_iota(I32, posv_ref.shape, 1)
        spare = spare + TM_MIX * lax.broadcasted_iota(I32, posv_ref.shape, 0)
        posv_ref[...] = jnp.where(spare < xs_hbm.shape[0], spare, 0)
        pos_copy(1).start()

    @pl.when(i > 0)
    def _():
        drain(slot)

    pos_copy(prev).wait()

    for j in range(TM_MIX):
        row_copy(prev, 0, j).start(priority=0)
        row_copy(prev, 1, j).start(priority=1)

    h1_ref[...] = jnp.dot(a_ref[...], wout_ref[:ATT_WIDTH, :], preferred_element_type=F32)

    n_chunks = TM_MIX // SG_CHUNK
    rows = []
    for c in range(n_chunks):
        cols = []
        for g in range(SG_GROUPS):
            rs = slice(c * SG_CHUNK, (c + 1) * SG_CHUNK)
            cs = slice(g * SG_DIM, (g + 1) * SG_DIM)
            vt = vg_ref[rs, cs].astype(F32)
            vn = _layernorm(vt, sgg_ref[g:g + 1, :], sgbeta_ref[g:g + 1, :])
            mixed = jnp.dot(sgw_ref[g], vn.astype(BF16), preferred_element_type=F32) + sgb_ref[g]
            cols.append((u_ref[rs, cs].astype(F32) * mixed).astype(BF16))
        rows.append(jnp.concatenate(cols, axis=1))
    sgate = jnp.concatenate(rows, axis=0)

    mix = h1_ref[...] + jnp.dot(sgate, wout_ref[ATT_WIDTH:, :], preferred_element_type=F32)
    h1 = _layernorm(DN_ALPHA * x_ref[...] + mix, g1_ref[...], b1_ref[...])
    h1_ref[...] = h1
    hp_s[slot] = h1.astype(BF16).reshape(hp_s.shape[1:])

    nt = (((1,), (1,)), ((), ()))
    h1_hi = h1.astype(BF16)
    h1_lo = (h1 - h1_hi.astype(F32)).astype(BF16)
    both = lax.dot_general(wr_ref[...], h1_hi, nt, preferred_element_type=F32)
    lt = both[:NR_PAD] + both[NR_PAD:] + lax.dot_general(
        wr_ref[:NR_PAD, :], h1_lo, nt, preferred_element_type=F32)
    lt = lt + br_ref[...]
    g_logits = lt[0:N_GROUPS]
    gmax, g_idx = _first_argmax(g_logits, N_GROUPS)
    g_gate = 1.0 / jnp.sum(jnp.exp(g_logits - gmax), axis=0, keepdims=True)
    e_logits = jnp.zeros((EXPERTS_PER_GROUP, TM_MIX), F32)
    for g in range(N_GROUPS):
        lo = NR_EXPERT0 + g * EXPERTS_PER_GROUP
        e_logits = jnp.where(g_idx == g, lt[lo:lo + EXPERTS_PER_GROUP], e_logits)
    v1, i1 = _first_argmax(e_logits, EXPERTS_PER_GROUP)
    erow = lax.broadcasted_iota(I32, e_logits.shape, 0)
    v2, i2 = _first_argmax(jnp.where(erow == i1, -jnp.inf, e_logits), EXPERTS_PER_GROUP)
    t = jnp.exp(v2 - v1)
    w1 = g_gate / (1.0 + t)
    w2 = g_gate * t / (1.0 + t)
    e1 = g_idx * EXPERTS_PER_GROUP + i1
    e2 = g_idx * EXPERTS_PER_GROUP + i2

    xrow = lax.broadcasted_iota(I32, (N_EXPERTS, TM_MIX), 0)
    oh1 = xrow == e1
    oh2 = xrow == e2
    oh = jnp.where(jnp.logical_or(oh1, oh2), 1.0, 0.0)
    tr = lax.broadcasted_iota(I32, (TM_MIX, TM_MIX), 0)
    tc = lax.broadcasted_iota(I32, (TM_MIX, TM_MIX), 1)
    before = jnp.where(tr < tc, 1.0, 0.0).astype(BF16)
    cum = jnp.dot(oh.astype(BF16), before, preferred_element_type=F32)
    base = cum + run_ref[:, 0:1]
    rank1 = jnp.sum(jnp.where(oh1, base, 0.0), axis=0, keepdims=True).astype(I32)
    rank2 = jnp.sum(jnp.where(oh2, base, 0.0), axis=0, keepdims=True).astype(I32)
    run_ref[...] = run_ref[...] + jnp.sum(oh, axis=1, keepdims=True)
    cnt_ref[...] = run_ref[...].astype(I32)

    zi = jnp.zeros((ROUTE_ROWS - 4, TM_MIX), I32)
    ri_ref[...] = jnp.concatenate([e1, e2, rank1, rank2, zi], axis=0)
    zw = jnp.zeros((ROUTE_ROWS - 2, TM_MIX), F32)
    rw_ref[...] = jnp.concatenate([w1, w2, zw], axis=0)

    posv_ref[...] = jnp.concatenate([e1 * XS_CAP + rank1, e2 * XS_CAP + rank2, zi, zi[:2]], axis=0)
    pos_copy(slot).start()

    @pl.when(i == n - 1)
    def _():
        pos_copy(slot).wait()

        def last(j, c):
            row_copy(slot, 0, j).start(priority=0)
            row_copy(slot, 1, j).start(priority=1)
            return c
        lax.fori_loop(0, TM_MIX, last, 0, unroll=8)
        drain(prev)
        drain(slot)


def _mix(a2, proj2, x2, wout_b, sgw_b, sgb, sgg, sgbeta, g1, b1, wr, br):
    t, d = x2.shape
    assert XS_CAP >= t + TMX
    u_blk = (proj2.shape[1] - 2 * SG_WIDTH) // SG_WIDTH
    const = lambda *shape: pl.BlockSpec(shape, lambda i: (0,) * len(shape))
    return pl.pallas_call(
        _mix_kernel,
        grid=(t // TM_MIX,),
        in_specs=[
            pl.BlockSpec((TM_MIX, ATT_WIDTH), lambda i: (i, 0)),
            pl.BlockSpec((TM_MIX, SG_WIDTH), lambda i: (i, u_blk)),
            pl.BlockSpec((TM_MIX, SG_WIDTH), lambda i: (i, u_blk + 1)),
            pl.BlockSpec((TM_MIX, d), lambda i: (i, 0)),
            pl.BlockSpec((d, d), lambda i: (0, 0), pipeline_mode=pl.Buffered(1)),
            const(SG_GROUPS, SG_CHUNK, SG_CHUNK),
            const(SG_GROUPS, SG_CHUNK, 1),
            const(SG_GROUPS, SG_DIM),
            const(SG_GROUPS, SG_DIM),
            const(1, d),
            const(1, d),
            const(2 * NR_PAD, d),
            const(NR_PAD, 1),
        ],
        out_specs=[
            pl.BlockSpec((TM_MIX, d), lambda i: (i, 0)),
            pl.BlockSpec(memory_space=pl.ANY),
            pl.BlockSpec((ROUTE_ROWS, TM_MIX), lambda i: (0, i)),
            pl.BlockSpec((ROUTE_ROWS, TM_MIX), lambda i: (0, i)),
            const(N_EXPERTS, LANES),
        ],
        out_shape=[
            jax.ShapeDtypeStruct((t, d), F32),
            jax.ShapeDtypeStruct((N_EXPERTS * XS_CAP + 2 * TM_MIX, d // LANES, LANES), BF16),
            jax.ShapeDtypeStruct((ROUTE_ROWS, t), I32),
            jax.ShapeDtypeStruct((ROUTE_ROWS, t), F32),
            jax.ShapeDtypeStruct((N_EXPERTS, LANES), I32),
        ],
        scratch_shapes=[pltpu.VMEM((N_EXPERTS, LANES), F32),
                        pltpu.VMEM((2, TM_MIX, d // LANES, LANES), BF16),
                        pltpu.VMEM((ROUTE_ROWS, TM_MIX), I32),
                        pltpu.SMEM((2, ROUTE_ROWS, TM_MIX), I32),
                        pltpu.SemaphoreType.DMA((2,)),
                        pltpu.SemaphoreType.DMA((2,))],
        compiler_params=_cparams(("arbitrary",)),
        name="mix",
    )(a2, proj2, proj2, x2, wout_b, sgw_b, sgb, sgg, sgbeta, g1, b1, wr, br)


def _experts_kernel(ts_ref, nt_ref, nu_ref, trow_ref, nval_ref, x_hbm, wg_ref, wu_ref, wd_ref,
                    y_hbm, xbuf, ybuf, xsem, ysem):
    e = pl.program_id(0)
    d = wg_ref.shape[1]
    n_used = nu_ref[0]

    def x_copy(g):
        rows = pl.ds(pl.multiple_of(trow_ref[g], TMX), TMX)
        slot = lax.rem(g, N_XSLOTS)
        return pltpu.make_async_copy(x_hbm.at[rows], xbuf.at[slot], xsem.at[slot])

    def y_copy(g):
        rows = pl.ds(pl.multiple_of(g * TMX, TMX), TMX)
        slot = lax.rem(g, N_XSLOTS)
        return pltpu.make_async_copy(ybuf.at[slot], y_hbm.at[rows], ysem.at[slot])

    def start_x(g):
        @pl.when(g < n_used)
        def _():
            x_copy(g).start(priority=1)

    def free_y(g):
        @pl.when(g >= N_XSLOTS)
        def _():
            y_copy(g - N_XSLOTS).wait()

    def load(g):
        xt = xbuf[lax.rem(g, N_XSLOTS)].reshape(TMX, d)
        row = lax.broadcasted_iota(I32, xt.shape, 0)
        return jnp.where(row < nval_ref[g], xt, jnp.zeros_like(xt))

    def mlp(xb):
        gate = jnp.dot(xb, wg_ref[0], preferred_element_type=F32)
        up = jnp.dot(xb, wu_ref[0], preferred_element_type=F32)
        hmid = (gate * jax.nn.sigmoid(gate) * up).astype(BF16)
        return jnp.dot(hmid, wd_ref[0], preferred_element_type=F32)

    @pl.when(e == 0)
    def _():
        for g in range(X_AHEAD):
            start_x(g)

    def group(k, first):
        def body(t, carry):
            g = first(t)
            for j in range(k):
                x_copy(g + j).wait()
            for j in range(k):
                start_x(g + X_AHEAD + j)
                free_y(g + j)
            y = mlp(jnp.concatenate([load(g + j) for j in range(k)], axis=0))
            for j in range(k):
                ybuf[lax.rem(g + j, N_XSLOTS)] = y[j * TMX:(j + 1) * TMX].astype(BF16).reshape(
                    ybuf.shape[1:])
                y_copy(g + j).start(priority=1)
            return carry
        return body

    nt = nt_ref[e]
    n3 = jnp.where(jnp.logical_and(nt >= 3, nt % 2 == 1), 1, 0)
    n2 = (nt - 3 * n3) // 2
    n1 = nt - 3 * n3 - 2 * n2
    base = ts_ref[e]
    lax.fori_loop(0, n3, group(3, lambda t: base), 0)
    lax.fori_loop(0, n2, group(2, lambda t: base + 3 * n3 + 2 * t), 0)
    lax.fori_loop(0, n1, group(1, lambda t: base + 3 * n3 + 2 * n2), 0)

    @pl.when(e == pl.num_programs(0) - 1)
    def _():
        for back in range(N_XSLOTS, 0, -1):
            @pl.when(n_used >= back)
            def _():
                y_copy(n_used - back).wait()


def _experts(tile_start, n_tiles_e, n_used, tile_row, tile_valid, xs, w_gate, w_up, w_down):
    d = w_gate.shape[1]
    n_rows = tile_row.shape[0] * TMX

    def w_map(e, ts, nt, nu, trow, nval):
        return (e, 0, 0)

    grid_spec = pltpu.PrefetchScalarGridSpec(
        num_scalar_prefetch=5,
        grid=(N_EXPERTS,),
        in_specs=[
            pl.BlockSpec(memory_space=pl.ANY),
            pl.BlockSpec((1, d, D_EXPERT), w_map),
            pl.BlockSpec((1, d, D_EXPERT), w_map),
            pl.BlockSpec((1, D_EXPERT, d), w_map),
        ],
        out_specs=pl.BlockSpec(memory_space=pl.ANY),
        scratch_shapes=[pltpu.VMEM((N_XSLOTS, TMX) + xs.shape[1:], BF16),
                        pltpu.VMEM((N_XSLOTS, TMX) + xs.shape[1:], BF16),
                        pltpu.SemaphoreType.DMA((N_XSLOTS,)),
                        pltpu.SemaphoreType.DMA((N_XSLOTS,))],
    )
    return pl.pallas_call(
        _experts_kernel,
        grid_spec=grid_spec,
        out_shape=jax.ShapeDtypeStruct((n_rows,) + xs.shape[1:], BF16),
        compiler_params=_cparams(("arbitrary",)),
        name="experts",
    )(tile_start, n_tiles_e, n_used, tile_row, tile_valid, xs, w_gate, w_up, w_down)


def _combine_kernel(pos_ref, posn_ref, y_hbm, h1_ref, wt_ref, g2_ref, b2_ref, o_ref, ybuf, sem):
    i = pl.program_id(0)
    n = pl.num_programs(0)
    slot = i % 2

    def row_copy(src, sl, which, j):
        return pltpu.make_async_copy(y_hbm.at[pl.ds(src, 1)],
                                     ybuf.at[sl, which, pl.ds(j, 1)], sem.at[sl])

    def issue(p_ref, sl):
        def body(j, c):
            row_copy(p_ref[0, j], sl, 0, j).start(priority=0)
            row_copy(p_ref[1, j], sl, 1, j).start(priority=1)
            return c
        lax.fori_loop(0, TM_OUT, body, 0, unroll=8)

    def drain(sl):
        def wait(j, c):
            row_copy(0, sl, 0, 0).wait()
            row_copy(0, sl, 1, 0).wait()
            return c
        lax.fori_loop(0, TM_OUT, wait, 0, unroll=8)

    @pl.when(i == 0)
    def _():
        issue(pos_ref, 0)

    drain(slot)

    for j in range(TM_OUT):
        row_copy(posn_ref[0, j], 1 - slot, 0, j).start(priority=0)
        row_copy(posn_ref[1, j], 1 - slot, 1, j).start(priority=1)

    y1 = ybuf[slot, 0].reshape(h1_ref.shape).astype(F32)
    y2 = ybuf[slot, 1].reshape(h1_ref.shape).astype(F32)
    ffn = wt_ref[:, 0:1] * y1 + wt_ref[:, 1:2] * y2
    o_ref[...] = _layernorm(DN_ALPHA * h1_ref[...] + ffn, g2_ref[...], b2_ref[...])

    @pl.when(i == n - 1)
    def _():
        drain(1 - slot)


def _combine(pos3, ys, h1, wtok, g2, b2):
    t, d = h1.shape
    n = t // TM_OUT
    return pl.pallas_call(
        _combine_kernel,
        grid=(n,),
        in_specs=[
            pl.BlockSpec((None, 2, TM_OUT), lambda i: (i, 0, 0), memory_space=pltpu.SMEM),
            pl.BlockSpec((None, 2, TM_OUT), lambda i: (jnp.minimum(i + 1, n - 1), 0, 0),
                         memory_space=pltpu.SMEM),
            pl.BlockSpec(memory_space=pl.ANY),
            pl.BlockSpec((TM_OUT, d), lambda i: (i, 0)),
            pl.BlockSpec((TM_OUT, ROUTE_ROWS), lambda i: (i, 0)),
            pl.BlockSpec((1, d), lambda i: (0, 0)),
            pl.BlockSpec((1, d), lambda i: (0, 0)),
        ],
        out_specs=pl.BlockSpec((TM_OUT, d), lambda i: (i, 0)),
        out_shape=jax.ShapeDtypeStruct((t, d), F32),
        scratch_shapes=[pltpu.VMEM((2, 2, TM_OUT) + ys.shape[1:], BF16),
                        pltpu.SemaphoreType.DMA((2,))],
        compiler_params=_cparams(("arbitrary",)),
        name="combine",
    )(pos3, pos3, ys, h1, wtok, g2, b2)


def _lambda_init(layer):
    return 0.8 - 0.6 * math.exp(-0.3 * layer)


def kernel(x, w_in, w_out, ln1_g, ln1_b, ln2_g, ln2_b, rel_bias, lam_q1, lam_k1, lam_q2, lam_k2,
           subln_g, sg_ln_g, sg_ln_b, sg_w, sg_b, w_router_group, b_router_group,
           w_router_expert, b_router_expert, w_exp_gate, w_exp_up, w_exp_down):
    b, s, d = x.shape
    t = b * s
    assert w_in.shape[0] == DEPTH == 1
    l = 0
    lam_init = _lambda_init(l)
    x2 = x.reshape(t, d)

    proj = _inproj(x2, w_in[l])

    lam = (jnp.exp(jnp.sum(lam_q1[l].astype(F32) * lam_k1[l].astype(F32)))
           - jnp.exp(jnp.sum(lam_q2[l].astype(F32) * lam_k2[l].astype(F32))) + lam_init)
    side = [w_exp_gate[l].reshape(-1, D_EXPERT), w_exp_up[l].reshape(-1, D_EXPERT),
            w_exp_down[l].reshape(-1, d), w_out[l]]
    a, wg_b, wu_b, wd_b, wout_b = _attention(
        proj.reshape(b, s, -1), _bias_sequences(rel_bias, s), lam.reshape(1),
        subln_g[l].reshape(1, -1), lam_init, side)

    wr = jnp.concatenate([w_router_group[l].T,
                          jnp.zeros((NR_EXPERT0 - N_GROUPS, d), F32),
                          jnp.transpose(w_router_expert[l], (0, 2, 1)).reshape(N_EXPERTS, d)],
                         axis=0)
    br = jnp.concatenate([b_router_group[l], jnp.zeros((NR_EXPERT0 - N_GROUPS,), F32),
                          b_router_expert[l].reshape(-1)]).reshape(NR_PAD, 1)
    wr_hi = wr.astype(BF16)
    wr = jnp.concatenate([wr_hi, (wr - wr_hi.astype(F32)).astype(BF16)], axis=0)
    h1, xs, ri, rw, cnt = _mix(
        a.reshape(t, -1), proj, x2, wout_b, sg_w[l].astype(BF16),
        sg_b[l].reshape(SG_GROUPS, SG_CHUNK, 1), sg_ln_g[l], sg_ln_b[l],
        ln1_g[l].reshape(1, d), ln1_b[l].reshape(1, d), wr, br)

    counts = cnt[:, 0]
    tiles_e = (counts + TMX - 1) // TMX
    tile_end = jnp.cumsum(tiles_e)
    tile_start = tile_end - tiles_e
    n_used = tile_end[-1:]
    max_tiles = (2 * t) // TMX + N_EXPERTS
    row_start = (tile_start * TMX).astype(I32)
    eids = jnp.arange(N_EXPERTS, dtype=I32)[:, None, None]
    pos = jnp.sum(jnp.where(ri[0:2][None] == eids, row_start[:, None, None], 0), axis=0) + ri[2:4]
    pos3 = pos.reshape(2, t // TM_OUT, TM_OUT).transpose(1, 0, 2)
    tile_ids = jnp.arange(max_tiles, dtype=I32)
    tile_e = jnp.minimum(jnp.sum(tile_end[None, :] <= tile_ids[:, None], axis=1), N_EXPERTS - 1)
    local = tile_ids - tile_start[tile_e]
    in_use = tile_ids < n_used[0]
    tile_row = jnp.where(in_use, tile_e * XS_CAP + local * TMX, 0).astype(I32)
    tile_valid = jnp.where(in_use, jnp.clip(counts[tile_e] - local * TMX, 0, TMX), 0).astype(I32)

    ys = _experts(tile_start.astype(I32), tiles_e.astype(I32), n_used.astype(I32),
                  tile_row, tile_valid, xs,
                  wg_b.reshape(N_EXPERTS, d, D_EXPERT), wu_b.reshape(N_EXPERTS, d, D_EXPERT),
                  wd_b.reshape(N_EXPERTS, D_EXPERT, d))
    out = _combine(pos3, ys, h1, rw.T, ln2_g[l].reshape(1, d), ln2_b[l].reshape(1, d))
    return out.reshape(b, s, d)
```

```python
import functools
import math

import jax
import jax.numpy as jnp
from jax import lax
from jax.experimental import pallas as pl
from jax.experimental.pallas import tpu as pltpu

F32 = jnp.float32
BF16 = jnp.bfloat16
I32 = jnp.int32

ATT_HEADS = 8
ATT_QK_DIM = 64
ATT_V_DIM = 128
ATT_WIDTH = ATT_HEADS * ATT_V_DIM
SG_GROUPS = 8
SG_DIM = 128
SG_WIDTH = SG_GROUPS * SG_DIM
SG_CHUNK = 128
REL_BUCKETS = 32
REL_MAX_DIST = 128
N_GROUPS = 4
EXPERTS_PER_GROUP = 8
N_EXPERTS = N_GROUPS * EXPERTS_PER_GROUP
D_EXPERT = 256
DEPTH = 1
DN_ALPHA = (2.0 * DEPTH) ** 0.25
LN_EPS = 1e-5
LOG2E = math.log2(math.e)

LANES = 128
SUBLANES = 8
VMEM_LIMIT_BYTES = 56 * 1024 * 1024

TM_IN = 1024
TN_IN = 1024
QB = 512
KC = 256
N_BIAS_TILES = 5
TM_MIX = 512
TMX = 256
TM_OUT = 256
X_AHEAD = 3
N_XSLOTS = 2 * X_AHEAD
ROUTE_ROWS = 8
XS_CAP = 8192 + TMX
NR_EXPERT0 = 8
NR_PAD = NR_EXPERT0 + 32


def _cparams(sem, flags=None):
    return pltpu.CompilerParams(dimension_semantics=sem, vmem_limit_bytes=VMEM_LIMIT_BYTES,
                                flags=flags)


def _inproj_kernel(x_ref, w_ref, o_ref, xb_ref, *, n_q, n_lin, q_scale):
    j = pl.program_id(1)

    @pl.when(j == 0)
    def _():
        xb_ref[...] = x_ref[...].astype(BF16)

    acc = jnp.dot(xb_ref[...], w_ref[...].astype(BF16), preferred_element_type=F32)

    @pl.when(j < n_q)
    def _():
        o_ref[...] = (acc * q_scale).astype(BF16)

    @pl.when(jnp.logical_and(j >= n_q, j < n_lin))
    def _():
        o_ref[...] = acc.astype(BF16)

    @pl.when(j >= n_lin)
    def _():
        o_ref[...] = jax.nn.gelu(acc).astype(BF16)


def _inproj(x2, w_in):
    t, d = x2.shape
    n = w_in.shape[1]
    n_q = (ATT_HEADS * 2 * ATT_QK_DIM) // TN_IN
    n_lin = (2 * ATT_HEADS * 2 * ATT_QK_DIM + ATT_WIDTH) // TN_IN
    kern = functools.partial(_inproj_kernel, n_q=n_q, n_lin=n_lin,
                             q_scale=(ATT_QK_DIM ** -0.5) * LOG2E)
    return pl.pallas_call(
        kern,
        grid=(t // TM_IN, n // TN_IN),
        in_specs=[pl.BlockSpec((TM_IN, d), lambda i, j: (i, 0)),
                  pl.BlockSpec((d, TN_IN), lambda i, j: (0, j))],
        out_specs=pl.BlockSpec((TM_IN, TN_IN), lambda i, j: (i, j)),
        out_shape=jax.ShapeDtypeStruct((t, n), BF16),
        scratch_shapes=[pltpu.VMEM((TM_IN, d), BF16)],
        compiler_params=_cparams(("arbitrary", "arbitrary")),
        name="inproj",
    )(x2, w_in)


def _attn_kernel(lam_ref, q_ref, k_ref, v_ref, bseq_ref, g_ref, *rest, n_kc, out_scale):
    n_side = (len(rest) - 3) // 2
    side_in, o_ref, side_out = rest[:n_side], rest[n_side], rest[n_side + 1:2 * n_side + 1]
    vaug_ref, bias_ref = rest[2 * n_side + 1:]
    qi = pl.program_id(2)

    @pl.when(qi == 0)
    def _():
        vaug_ref[:, :ATT_V_DIM] = v_ref[0]
        vaug_ref[:, ATT_V_DIM:] = jnp.ones((vaug_ref.shape[0], ATT_V_DIM), BF16)
        for t in range(N_BIAS_TILES):
            seq = jnp.broadcast_to(bseq_ref[0, t:t + 1, :], (KC, 2 * KC))
            bias_ref[t] = pltpu.roll(seq, 0, 1, stride=1, stride_axis=0)[:, :KC]

    q = q_ref[0]
    lane = lax.broadcasted_iota(I32, q.shape, 1)
    zero = jnp.zeros_like(q)
    qq = jnp.concatenate([jnp.where(lane < ATT_QK_DIM, q, zero),
                          jnp.where(lane >= ATT_QK_DIM, q, zero)], axis=0)

    for src, dst in zip(side_in, side_out):
        dst[...] = src[...].astype(dst.dtype)

    m = jnp.full((2 * QB, 1), -jnp.inf, F32)
    acc = jnp.zeros((2 * QB, 2 * ATT_V_DIM), F32)
    for kc in range(n_kc):
        kblk = k_ref[0, kc * KC:(kc + 1) * KC, :]
        s = lax.dot_general(qq, kblk, (((1,), (1,)), ((), ())),
                            preferred_element_type=F32)
        b = jnp.concatenate([bias_ref[jnp.clip(kc - (qi * (QB // KC) + sub), -2, 2) + 2]
                             for sub in range(QB // KC)], axis=0)
        s = s + jnp.concatenate([b, b], axis=0)
        m_new = jnp.maximum(m, jnp.max(s, axis=1, keepdims=True))
        p = jnp.exp2(s - m_new).astype(BF16)
        acc = acc * jnp.exp2(m - m_new) + jnp.dot(
            p, vaug_ref[kc * KC:(kc + 1) * KC, :], preferred_element_type=F32)
        m = m_new
    lam = lam_ref[0]
    o0 = acc[:QB, :ATT_V_DIM] / acc[:QB, ATT_V_DIM:ATT_V_DIM + 1]
    o1 = acc[QB:, :ATT_V_DIM] / acc[QB:, ATT_V_DIM:ATT_V_DIM + 1]
    of = o0 - lam * o1
    of = of * lax.rsqrt(jnp.mean(of * of, axis=-1, keepdims=True) + LN_EPS)
    o_ref[0] = (of * (g_ref[...] * out_scale)).astype(o_ref.dtype)


def _attention(proj3, bias_seq, lam, subln_g, lam_init, side):
    b, s, _ = proj3.shape
    n_kc = s // KC
    n_q = s // QB
    n_steps = b * ATT_HEADS * n_q
    h_off_k = ATT_HEADS
    h_off_v = 2 * ATT_HEADS
    kern = functools.partial(_attn_kernel, n_kc=n_kc, out_scale=1.0 - lam_init)

    side_specs = []
    for arr in side:
        rows, cols = arr.shape
        blk = max(rows // n_steps, 2 * SUBLANES)
        hold = blk * n_steps // rows
        assert rows % blk == 0 and hold * rows == blk * n_steps
        side_specs.append(pl.BlockSpec(
            (blk, cols),
            lambda bi, h, qi, hold=hold: (((bi * ATT_HEADS + h) * n_q + qi) // hold, 0)))
    return pl.pallas_call(
        kern,
        grid=(b, ATT_HEADS, n_q),
        in_specs=[
            pl.BlockSpec(memory_space=pltpu.SMEM),
            pl.BlockSpec((1, QB, LANES), lambda bi, h, qi: (bi, qi, h)),
            pl.BlockSpec((1, s, LANES), lambda bi, h, qi: (bi, 0, h_off_k + h)),
            pl.BlockSpec((1, s, LANES), lambda bi, h, qi: (bi, 0, h_off_v + h)),
            pl.BlockSpec((1, N_BIAS_TILES, 2 * KC), lambda bi, h, qi: (h, 0, 0)),
            pl.BlockSpec((1, ATT_V_DIM), lambda bi, h, qi: (0, 0)),
        ] + side_specs,
        out_specs=[pl.BlockSpec((1, QB, ATT_V_DIM), lambda bi, h, qi: (bi, qi, h))] + side_specs,
        out_shape=[jax.ShapeDtypeStruct((b, s, ATT_WIDTH), BF16)]
        + [jax.ShapeDtypeStruct(arr.shape, BF16) for arr in side],
        scratch_shapes=[pltpu.VMEM((s, 2 * ATT_V_DIM), BF16),
                        pltpu.VMEM((N_BIAS_TILES, KC, KC), F32)],
        compiler_params=_cparams(("arbitrary", "arbitrary", "arbitrary")),
        name="attn",
    )(lam, proj3, proj3, proj3, bias_seq, subln_g, *side)


def _rel_bucket(rel):
    half = REL_BUCKETS // 2
    max_exact = half // 2
    ret = jnp.where(rel > 0, half, 0)
    n = jnp.abs(rel)
    nf = jnp.maximum(n, 1).astype(F32)
    large = max_exact + (jnp.log(nf / max_exact) / math.log(REL_MAX_DIST / max_exact)
                         * (half - max_exact)).astype(I32)
    large = jnp.minimum(large, half - 1)
    return ret + jnp.where(n < max_exact, n, large)


def _bias_sequences(rel_bias, s):
    assert REL_MAX_DIST <= KC and QB % KC == 0
    period = 2 * KC
    j = jnp.arange(period, dtype=I32)
    off = jnp.where(j < KC, j, j - period)
    d = jnp.arange(-2, 3, dtype=I32)[:, None]
    rel = jnp.clip(d * KC + off[None, :], -(s - 1), s - 1)
    return jnp.transpose(rel_bias[_rel_bucket(rel)], (2, 0, 1)).astype(F32) * LOG2E


def _layernorm(v, g, b):
    mu = jnp.mean(v, axis=-1, keepdims=True)
    var = jnp.mean(jnp.square(v - mu), axis=-1, keepdims=True)
    return (v - mu) * lax.rsqrt(var + LN_EPS) * g + b


def _first_argmax(vals, nrows):
    mx = jnp.max(vals, axis=0, keepdims=True)
    row = lax.broadcasted_iota(I32, vals.shape, 0)
    idx = jnp.min(jnp.where(vals == mx, row, nrows), axis=0, keepdims=True)
    return mx, idx


def _mix_kernel(a_ref, u_ref, vg_ref, x_ref, wout_ref, sgw_ref, sgb_ref, sgg_ref, sgbeta_ref,
                g1_ref, b1_ref, wr_ref, br_ref,
                h1_ref, xs_hbm, ri_ref, rw_ref, cnt_ref,
                run_ref, hp_s, posv_ref, psm, ssem, rsem):
    i = pl.program_id(0)
    n = pl.num_programs(0)
    slot = i % 2
    prev = 1 - slot

    def pos_copy(sl):
        return pltpu.make_async_copy(posv_ref, psm.at[sl], ssem.at[sl])

    def row_copy(sl, which, j):
        return pltpu.make_async_copy(hp_s.at[sl, pl.ds(j, 1)],
                                     xs_hbm.at[pl.ds(psm[sl, which, j], 1)], rsem.at[sl])

    def drain(sl):
        def wait(j, c):
            row_copy(sl, 0, 0).wait()
            row_copy(sl, 1, 0).wait()
            return c
        lax.fori_loop(0, TM_MIX, wait, 0, unroll=8)

    @pl.when(i == 0)
    def _():
        run_ref[...] = jnp.zeros_like(run_ref)
        hp_s[1] = jnp.zeros(hp_s.shape[1:], hp_s.dtype)
        spare = N_EXPERTS * XS_CAP + lax.broadcasted_iota(I32, posv_ref.shape, 1)
        spare = spare + TM_MIX * lax.broadcasted_iota(I32, posv_ref.shape, 0)
        posv_ref[...] = jnp.where(spare < xs_hbm.shape[0], spare, 0)
        pos_copy(1).start()

    @pl.when(i > 0)
    def _():
        drain(slot)

    pos_copy(prev).wait()

    for j in range(TM_MIX):
        row_copy(prev, 0, j).start(priority=0)
        row_copy(prev, 1, j).start(priority=1)

    h1_ref[...] = jnp.dot(a_ref[...], wout_ref[:ATT_WIDTH, :], preferred_element_type=F32)

    n_chunks = TM_MIX // SG_CHUNK
    rows = []
    for c in range(n_chunks):
        cols = []
        for g in range(SG_GROUPS):
            rs = slice(c * SG_CHUNK, (c + 1) * SG_CHUNK)
            cs = slice(g * SG_DIM, (g + 1) * SG_DIM)
            vt = vg_ref[rs, cs].astype(F32)
            vn = _layernorm(vt, sgg_ref[g:g + 1, :], sgbeta_ref[g:g + 1, :])
            mixed = jnp.dot(sgw_ref[g], vn.astype(BF16), preferred_element_type=F32) + sgb_ref[g]
            cols.append((u_ref[rs, cs].astype(F32) * mixed).astype(BF16))
        rows.append(jnp.concatenate(cols, axis=1))
    sgate = jnp.concatenate(rows, axis=0)

    mix = h1_ref[...] + jnp.dot(sgate, wout_ref[ATT_WIDTH:, :], preferred_element_type=F32)
    h1 = _layernorm(DN_ALPHA * x_ref[...] + mix, g1_ref[...], b1_ref[...])
    h1_ref[...] = h1
    hp_s[slot] = h1.astype(BF16).reshape(hp_s.shape[1:])

    nt = (((1,), (1,)), ((), ()))
    h1_hi = h1.astype(BF16)
    h1_lo = (h1 - h1_hi.astype(F32)).astype(BF16)
    both = lax.dot_general(wr_ref[...], h1_hi, nt, preferred_element_type=F32)
    lt = both[:NR_PAD] + both[NR_PAD:] + lax.dot_general(
        wr_ref[:NR_PAD, :], h1_lo, nt, preferred_element_type=F32)
    lt = lt + br_ref[...]
    g_logits = lt[0:N_GROUPS]
    gmax, g_idx = _first_argmax(g_logits, N_GROUPS)
    g_gate = 1.0 / jnp.sum(jnp.exp(g_logits - gmax), axis=0, keepdims=True)
    e_logits = jnp.zeros((EXPERTS_PER_GROUP, TM_MIX), F32)
    for g in range(N_GROUPS):
        lo = NR_EXPERT0 + g * EXPERTS_PER_GROUP
        e_logits = jnp.where(g_idx == g, lt[lo:lo + EXPERTS_PER_GROUP], e_logits)
    v1, i1 = _first_argmax(e_logits, EXPERTS_PER_GROUP)
    erow = lax.broadcasted_iota(I32, e_logits.shape, 0)
    v2, i2 = _first_argmax(jnp.where(erow == i1, -jnp.inf, e_logits), EXPERTS_PER_GROUP)
    t = jnp.exp(v2 - v1)
    w1 = g_gate / (1.0 + t)
    w2 = g_gate * t / (1.0 + t)
    e1 = g_idx * EXPERTS_PER_GROUP + i1
    e2 = g_idx * EXPERTS_PER_GROUP + i2

    xrow = lax.broadcasted_iota(I32, (N_EXPERTS, TM_MIX), 0)
    oh1 = xrow == e1
    oh2 = xrow == e2
    oh = jnp.where(jnp.logical_or(oh1, oh2), 1.0, 0.0)
    tr = lax.broadcasted_iota(I32, (TM_MIX, TM_MIX), 0)
    tc = lax.broadcasted_iota(I32, (TM_MIX, TM_MIX), 1)
    before = jnp.where(tr < tc, 1.0, 0.0).astype(BF16)
    cum = jnp.dot(oh.astype(BF16), before, preferred_element_type=F32)
    base = cum + run_ref[:, 0:1]
    rank1 = jnp.sum(jnp.where(oh1, base, 0.0), axis=0, keepdims=True).astype(I32)
    rank2 = jnp.sum(jnp.where(oh2, base, 0.0), axis=0, keepdims=True).astype(I32)
    run_ref[...] = run_ref[...] + jnp.sum(oh, axis=1, keepdims=True)
    cnt_ref[...] = run_ref[...].astype(I32)

    zi = jnp.zeros((ROUTE_ROWS - 4, TM_MIX), I32)
    ri_ref[...] = jnp.concatenate([e1, e2, rank1, rank2, zi], axis=0)
    zw = jnp.zeros((ROUTE_ROWS - 2, TM_MIX), F32)
    rw_ref[...] = jnp.concatenate([w1, w2, zw], axis=0)

    posv_ref[...] = jnp.concatenate([e1 * XS_CAP + rank1, e2 * XS_CAP + rank2, zi, zi[:2]], axis=0)
    pos_copy(slot).start()

    @pl.when(i == n - 1)
    def _():
        pos_copy(slot).wait()

        def last(j, c):
            row_copy(slot, 0, j).start(priority=0)
            row_copy(slot, 1, j).start(priority=1)
            return c
        lax.fori_loop(0, TM_MIX, last, 0, unroll=8)
        drain(prev)
        drain(slot)


def _mix(a2, proj2, x2, wout_b, sgw_b, sgb, sgg, sgbeta, g1, b1, wr, br):
    t, d = x2.shape
    assert XS_CAP >= t + TMX
    u_blk = (proj2.shape[1] - 2 * SG_WIDTH) // SG_WIDTH
    const = lambda *shape: pl.BlockSpec(shape, lambda i: (0,) * len(shape))
    return pl.pallas_call(
        _mix_kernel,
        grid=(t // TM_MIX,),
        in_specs=[
            pl.BlockSpec((TM_MIX, ATT_WIDTH), lambda i: (i, 0)),
            pl.BlockSpec((TM_MIX, SG_WIDTH), lambda i: (i, u_blk)),
            pl.BlockSpec((TM_MIX, SG_WIDTH), lambda i: (i, u_blk + 1)),
            pl.BlockSpec((TM_MIX, d), lambda i: (i, 0)),
            pl.BlockSpec((d, d), lambda i: (0, 0), pipeline_mode=pl.Buffered(1)),
            const(SG_GROUPS, SG_CHUNK, SG_CHUNK),
            const(SG_GROUPS, SG_CHUNK, 1),
            const(SG_GROUPS, SG_DIM),
            const(SG_GROUPS, SG_DIM),
            const(1, d),
            const(1, d),
            const(2 * NR_PAD, d),
            const(NR_PAD, 1),
        ],
        out_specs=[
            pl.BlockSpec((TM_MIX, d), lambda i: (i, 0)),
            pl.BlockSpec(memory_space=pl.ANY),
            pl.BlockSpec((ROUTE_ROWS, TM_MIX), lambda i: (0, i)),
            pl.BlockSpec((ROUTE_ROWS, TM_MIX), lambda i: (0, i)),
            const(N_EXPERTS, LANES),
        ],
        out_shape=[
            jax.ShapeDtypeStruct((t, d), F32),
            jax.ShapeDtypeStruct((N_EXPERTS * XS_CAP + 2 * TM_MIX, d // LANES, LANES), BF16),
            jax.ShapeDtypeStruct((ROUTE_ROWS, t), I32),
            jax.ShapeDtypeStruct((ROUTE_ROWS, t), F32),
            jax.ShapeDtypeStruct((N_EXPERTS, LANES), I32),
        ],
        scratch_shapes=[pltpu.VMEM((N_EXPERTS, LANES), F32),
                        pltpu.VMEM((2, TM_MIX, d // LANES, LANES), BF16),
                        pltpu.VMEM((ROUTE_ROWS, TM_MIX), I32),
                        pltpu.SMEM((2, ROUTE_ROWS, TM_MIX), I32),
                        pltpu.SemaphoreType.DMA((2,)),
                        pltpu.SemaphoreType.DMA((2,))],
        compiler_params=_cparams(("arbitrary",)),
        name="mix",
    )(a2, proj2, proj2, x2, wout_b, sgw_b, sgb, sgg, sgbeta, g1, b1, wr, br)


def _experts_kernel(ts_ref, nt_ref, nu_ref, trow_ref, nval_ref, x_hbm, wg_ref, wu_ref, wd_ref,
                    y_hbm, xbuf, ybuf, xsem, ysem):
    e = pl.program_id(0)
    d = wg_ref.shape[1]
    n_used = nu_ref[0]

    def x_copy(g):
        rows = pl.ds(pl.multiple_of(trow_ref[g], TMX), TMX)
        slot = lax.rem(g, N_XSLOTS)
        return pltpu.make_async_copy(x_hbm.at[rows], xbuf.at[slot], xsem.at[slot])

    def y_copy(g):
        rows = pl.ds(pl.multiple_of(g * TMX, TMX), TMX)
        slot = lax.rem(g, N_XSLOTS)
        return pltpu.make_async_copy(ybuf.at[slot], y_hbm.at[rows], ysem.at[slot])

    def start_x(g):
        @pl.when(g < n_used)
        def _():
            x_copy(g).start(priority=1)

    def free_y(g):
        @pl.when(g >= N_XSLOTS)
        def _():
            y_copy(g - N_XSLOTS).wait()

    def load(g):
        xt = xbuf[lax.rem(g, N_XSLOTS)].reshape(TMX, d)
        row = lax.broadcasted_iota(I32, xt.shape, 0)
        return jnp.where(row < nval_ref[g], xt, jnp.zeros_like(xt))

    def mlp(xb):
        gate = jnp.dot(xb, wg_ref[0], preferred_element_type=F32)
        up = jnp.dot(xb, wu_ref[0], preferred_element_type=F32)
        hmid = (gate * jax.nn.sigmoid(gate) * up).astype(BF16)
        return jnp.dot(hmid, wd_ref[0], preferred_element_type=F32)

    @pl.when(e == 0)
    def _():
        for g in range(X_AHEAD):
            start_x(g)

    def group(k, first):
        def body(t, carry):
            g = first(t)
            for j in range(k):
                x_copy(g + j).wait()
            for j in range(k):
                start_x(g + X_AHEAD + j)
                free_y(g + j)
            y = mlp(jnp.concatenate([load(g + j) for j in range(k)], axis=0))
            for j in range(k):
                ybuf[lax.rem(g + j, N_XSLOTS)] = y[j * TMX:(j + 1) * TMX].astype(BF16).reshape(
                    ybuf.shape[1:])
                y_copy(g + j).start(priority=1)
            return carry
        return body

    nt = nt_ref[e]
    n3 = jnp.where(jnp.logical_and(nt >= 3, nt % 2 == 1), 1, 0)
    n2 = (nt - 3 * n3) // 2
    n1 = nt - 3 * n3 - 2 * n2
    base = ts_ref[e]
    lax.fori_loop(0, n3, group(3, lambda t: base), 0)
    lax.fori_loop(0, n2, group(2, lambda t: base + 3 * n3 + 2 * t), 0)
    lax.fori_loop(0, n1, group(1, lambda t: base + 3 * n3 + 2 * n2), 0)

    @pl.when(e == pl.num_programs(0) - 1)
    def _():
        for back in range(N_XSLOTS, 0, -1):
            @pl.when(n_used >= back)
            def _():
                y_copy(n_used - back).wait()


def _experts(tile_start, n_tiles_e, n_used, tile_row, tile_valid, xs, w_gate, w_up, w_down):
    d = w_gate.shape[1]
    n_rows = tile_row.shape[0] * TMX

    def w_map(e, ts, nt, nu, trow, nval):
        return (e, 0, 0)

    grid_spec = pltpu.PrefetchScalarGridSpec(
        num_scalar_prefetch=5,
        grid=(N_EXPERTS,),
        in_specs=[
            pl.BlockSpec(memory_space=pl.ANY),
            pl.BlockSpec((1, d, D_EXPERT), w_map),
            pl.BlockSpec((1, d, D_EXPERT), w_map),
            pl.BlockSpec((1, D_EXPERT, d), w_map),
        ],
        out_specs=pl.BlockSpec(memory_space=pl.ANY),
        scratch_shapes=[pltpu.VMEM((N_XSLOTS, TMX) + xs.shape[1:], BF16),
                        pltpu.VMEM((N_XSLOTS, TMX) + xs.shape[1:], BF16),
                        pltpu.SemaphoreType.DMA((N_XSLOTS,)),
                        pltpu.SemaphoreType.DMA((N_XSLOTS,))],
    )
    return pl.pallas_call(
        _experts_kernel,
        grid_spec=grid_spec,
        out_shape=jax.ShapeDtypeStruct((n_rows,) + xs.shape[1:], BF16),
        compiler_params=_cparams(("arbitrary",)),
        name="experts",
    )(tile_start, n_tiles_e, n_used, tile_row, tile_valid, xs, w_gate, w_up, w_down)


def _combine_kernel(pos_ref, posn_ref, y_hbm, h1_ref, wt_ref, g2_ref, b2_ref, o_ref, ybuf, sem):
    i = pl.program_id(0)
    n = pl.num_programs(0)
    slot = i % 2

    def row_copy(src, sl, which, j):
        return pltpu.make_async_copy(y_hbm.at[pl.ds(src, 1)],
                                     ybuf.at[sl, which, pl.ds(j, 1)], sem.at[sl])

    def issue(p_ref, sl):
        def body(j, c):
            row_copy(p_ref[0, j], sl, 0, j).start(priority=0)
            row_copy(p_ref[1, j], sl, 1, j).start(priority=1)
            return c
        lax.fori_loop(0, TM_OUT, body, 0, unroll=8)

    def drain(sl):
        def wait(j, c):
            row_copy(0, sl, 0, 0).wait()
            row_copy(0, sl, 1, 0).wait()
            return c
        lax.fori_loop(0, TM_OUT, wait, 0, unroll=8)

    @pl.when(i == 0)
    def _():
        issue(pos_ref, 0)

    drain(slot)

    for j in range(TM_OUT):
        row_copy(posn_ref[0, j], 1 - slot, 0, j).start(priority=0)
        row_copy(posn_ref[1, j], 1 - slot, 1, j).start(priority=1)

    y1 = ybuf[slot, 0].reshape(h1_ref.shape).astype(F32)
    y2 = ybuf[slot, 1].reshape(h1_ref.shape).astype(F32)
    ffn = wt_ref[:, 0:1] * y1 + wt_ref[:, 1:2] * y2
    o_ref[...] = _layernorm(DN_ALPHA * h1_ref[...] + ffn, g2_ref[...], b2_ref[...])

    @pl.when(i == n - 1)
    def _():
        drain(1 - slot)


def _combine(pos3, ys, h1, wtok, g2, b2):
    t, d = h1.shape
    n = t // TM_OUT
    return pl.pallas_call(
        _combine_kernel,
        grid=(n,),
        in_specs=[
            pl.BlockSpec((None, 2, TM_OUT), lambda i: (i, 0, 0), memory_space=pltpu.SMEM),
            pl.BlockSpec((None, 2, TM_OUT), lambda i: (jnp.minimum(i + 1, n - 1), 0, 0),
                         memory_space=pltpu.SMEM),
            pl.BlockSpec(memory_space=pl.ANY),
            pl.BlockSpec((TM_OUT, d), lambda i: (i, 0)),
            pl.BlockSpec((TM_OUT, ROUTE_ROWS), lambda i: (i, 0)),
            pl.BlockSpec((1, d), lambda i: (0, 0)),
            pl.BlockSpec((1, d), lambda i: (0, 0)),
        ],
        out_specs=pl.BlockSpec((TM_OUT, d), lambda i: (i, 0)),
        out_shape=jax.ShapeDtypeStruct((t, d), F32),
        scratch_shapes=[pltpu.VMEM((2, 2, TM_OUT) + ys.shape[1:], BF16),
                        pltpu.SemaphoreType.DMA((2,))],
        compiler_params=_cparams(("arbitrary",)),
        name="combine",
    )(pos3, pos3, ys, h1, wtok, g2, b2)


def _lambda_init(layer):
    return 0.8 - 0.6 * math.exp(-0.3 * layer)


def kernel(x, w_in, w_out, ln1_g, ln1_b, ln2_g, ln2_b, rel_bias, lam_q1, lam_k1, lam_q2, lam_k2,
           subln_g, sg_ln_g, sg_ln_b, sg_w, sg_b, w_router_group, b_router_group,
           w_router_expert, b_router_expert, w_exp_gate, w_exp_up, w_exp_down):
    b, s, d = x.shape
    t = b * s
    assert w_in.shape[0] == DEPTH == 1
    l = 0
    lam_init = _lambda_init(l)
    x2 = x.reshape(t, d)

    proj = _inproj(x2, w_in[l])

    lam = (jnp.exp(jnp.sum(lam_q1[l].astype(F32) * lam_k1[l].astype(F32)))
           - jnp.exp(jnp.sum(lam_q2[l].astype(F32) * lam_k2[l].astype(F32))) + lam_init)
    side = [w_exp_gate[l].reshape(-1, D_EXPERT), w_exp_up[l].reshape(-1, D_EXPERT),
            w_exp_down[l].reshape(-1, d), w_out[l]]
    a, wg_b, wu_b, wd_b, wout_b = _attention(
        proj.reshape(b, s, -1), _bias_sequences(rel_bias, s), lam.reshape(1),
        subln_g[l].reshape(1, -1), lam_init, side)

    wr = jnp.concatenate([w_router_group[l].T,
                          jnp.zeros((NR_EXPERT0 - N_GROUPS, d), F32),
                          jnp.transpose(w_router_expert[l], (0, 2, 1)).reshape(N_EXPERTS, d)],
                         axis=0)
    br = jnp.concatenate([b_router_group[l], jnp.zeros((NR_EXPERT0 - N_GROUPS,), F32),
                          b_router_expert[l].reshape(-1)]).reshape(NR_PAD, 1)
    wr_hi = wr.astype(BF16)
    wr = jnp.concatenate([wr_hi, (wr - wr_hi.astype(F32)).astype(BF16)], axis=0)
    h1, xs, ri, rw, cnt = _mix(
        a.reshape(t, -1), proj, x2, wout_b, sg_w[l].astype(BF16),
        sg_b[l].reshape(SG_GROUPS, SG_CHUNK, 1), sg_ln_g[l], sg_ln_b[l],
        ln1_g[l].reshape(1, d), ln1_b[l].reshape(1, d), wr, br)

    counts = cnt[:, 0]
    tiles_e = (counts + TMX - 1) // TMX
    tile_end = jnp.cumsum(tiles_e)
    tile_start = tile_end - tiles_e
    n_used = tile_end[-1:]
    max_tiles = (2 * t) // TMX + N_EXPERTS
    row_start = (tile_start * TMX).astype(I32)
    eids = jnp.arange(N_EXPERTS, dtype=I32)[:, None, None]
    pos = jnp.sum(jnp.where(ri[0:2][None] == eids, row_start[:, None, None], 0), axis=0) + ri[2:4]
    pos3 = pos.reshape(2, t // TM_OUT, TM_OUT).transpose(1, 0, 2)
    tile_ids = jnp.arange(max_tiles, dtype=I32)
    tile_e = jnp.minimum(jnp.sum(tile_end[None, :] <= tile_ids[:, None], axis=1), N_EXPERTS - 1)
    local = tile_ids - tile_start[tile_e]
    in_use = tile_ids < n_used[0]
    tile_row = jnp.where(in_use, tile_e * XS_CAP + local * TMX, 0).astype(I32)
    tile_valid = jnp.where(in_use, jnp.clip(counts[tile_e] - local * TMX, 0, TMX), 0).astype(I32)

    ys = _experts(tile_start.astype(I32), tiles_e.astype(I32), n_used.astype(I32),
                  tile_row, tile_valid, xs,
                  wg_b.reshape(N_EXPERTS, d, D_EXPERT), wu_b.reshape(N_EXPERTS, d, D_EXPERT),
                  wd_b.reshape(N_EXPERTS, D_EXPERT, d))
    out = _combine(pos3, ys, h1, rw.T, ln2_g[l].reshape(1, d), ln2_b[l].reshape(1, d))
    return out.reshape(b, s, d)
```

```python
import functools
import math

import jax
import jax.numpy as jnp
from jax import lax
from jax.experimental import pallas as pl
from jax.experimental.pallas import tpu as pltpu

F32 = jnp.float32
BF16 = jnp.bfloat16
I32 = jnp.int32

ATT_HEADS = 8
ATT_QK_DIM = 64
ATT_V_DIM = 128
ATT_WIDTH = ATT_HEADS * ATT_V_DIM
SG_GROUPS = 8
SG_DIM = 128
SG_WIDTH = SG_GROUPS * SG_DIM
SG_CHUNK = 128
REL_BUCKETS = 32
REL_MAX_DIST = 128
N_GROUPS = 4
EXPERTS_PER_GROUP = 8
N_EXPERTS = N_GROUPS * EXPERTS_PER_GROUP
D_EXPERT = 256
DEPTH = 1
DN_ALPHA = (2.0 * DEPTH) ** 0.25
LN_EPS = 1e-5
LOG2E = math.log2(math.e)

LANES = 128
SUBLANES = 8
VMEM_LIMIT_BYTES = 56 * 1024 * 1024

TM_IN = 1024
TN_IN = 1024
QB = 256
KC = 256
N_BIAS_TILES = 5
TM_MIX = 512
TMX = 256
TM_OUT = 256
X_AHEAD = 3
N_XSLOTS = 2 * X_AHEAD
ROUTE_ROWS = 8
XS_CAP = 8192 + TMX
NR_EXPERT0 = 8
NR_PAD = NR_EXPERT0 + 32


def _cparams(sem, flags=None):
    return pltpu.CompilerParams(dimension_semantics=sem, vmem_limit_bytes=VMEM_LIMIT_BYTES,
                                flags=flags)


def _inproj_kernel(x_ref, w_ref, o_ref, xb_ref, *, n_q, n_lin, q_scale):
    j = pl.program_id(1)

    @pl.when(j == 0)
    def _():
        xb_ref[...] = x_ref[...].astype(BF16)

    acc = jnp.dot(xb_ref[...], w_ref[...].astype(BF16), preferred_element_type=F32)

    @pl.when(j < n_q)
    def _():
        o_ref[...] = (acc * q_scale).astype(BF16)

    @pl.when(jnp.logical_and(j >= n_q, j < n_lin))
    def _():
        o_ref[...] = acc.astype(BF16)

    @pl.when(j >= n_lin)
    def _():
        o_ref[...] = jax.nn.gelu(acc).astype(BF16)


def _inproj(x2, w_in):
    t, d = x2.shape
    n = w_in.shape[1]
    n_q = (ATT_HEADS * 2 * ATT_QK_DIM) // TN_IN
    n_lin = (2 * ATT_HEADS * 2 * ATT_QK_DIM + ATT_WIDTH) // TN_IN
    kern = functools.partial(_inproj_kernel, n_q=n_q, n_lin=n_lin,
                             q_scale=(ATT_QK_DIM ** -0.5) * LOG2E)
    return pl.pallas_call(
        kern,
        grid=(t // TM_IN, n // TN_IN),
        in_specs=[pl.BlockSpec((TM_IN, d), lambda i, j: (i, 0)),
                  pl.BlockSpec((d, TN_IN), lambda i, j: (0, j))],
        out_specs=pl.BlockSpec((TM_IN, TN_IN), lambda i, j: (i, j)),
        out_shape=jax.ShapeDtypeStruct((t, n), BF16),
        scratch_shapes=[pltpu.VMEM((TM_IN, d), BF16)],
        compiler_params=_cparams(("arbitrary", "arbitrary")),
        name="inproj",
    )(x2, w_in)


def _attn_kernel(lam_ref, q_ref, k_ref, v_ref, bseq_ref, g_ref, *rest, n_kc, out_scale):
    n_side = (len(rest) - 3) // 2
    side_in, o_ref, side_out = rest[:n_side], rest[n_side], rest[n_side + 1:2 * n_side + 1]
    vaug_ref, bias_ref = rest[2 * n_side + 1:]
    qi = pl.program_id(2)

    @pl.when(qi == 0)
    def _():
        vaug_ref[:, :ATT_V_DIM] = v_ref[0]
        vaug_ref[:, ATT_V_DIM:] = jnp.ones((vaug_ref.shape[0], ATT_V_DIM), BF16)
        for t in range(N_BIAS_TILES):
            seq = jnp.broadcast_to(bseq_ref[0, t:t + 1, :], (QB, 2 * KC))
            bias_ref[t] = pltpu.roll(seq, 0, 1, stride=1, stride_axis=0)[:, :KC]

    q = q_ref[0]
    lane = lax.broadcasted_iota(I32, q.shape, 1)
    zero = jnp.zeros_like(q)
    qq = jnp.concatenate([jnp.where(lane < ATT_QK_DIM, q, zero),
                          jnp.where(lane >= ATT_QK_DIM, q, zero)], axis=0)

    for src, dst in zip(side_in, side_out):
        dst[...] = src[...].astype(dst.dtype)

    m = jnp.full((2 * QB, 1), -jnp.inf, F32)
    acc = jnp.zeros((2 * QB, 2 * ATT_V_DIM), F32)
    for kc in range(n_kc):
        kblk = k_ref[0, kc * KC:(kc + 1) * KC, :]
        s = lax.dot_general(qq, kblk, (((1,), (1,)), ((), ())),
                            preferred_element_type=F32)
        b = bias_ref[jnp.clip(kc - qi, -2, 2) + 2]
        s = s + jnp.concatenate([b, b], axis=0)
        m_new = jnp.maximum(m, jnp.max(s, axis=1, keepdims=True))
        p = jnp.exp2(s - m_new).astype(BF16)
        acc = acc * jnp.exp2(m - m_new) + jnp.dot(
            p, vaug_ref[kc * KC:(kc + 1) * KC, :], preferred_element_type=F32)
        m = m_new
    lam = lam_ref[0]
    o0 = acc[:QB, :ATT_V_DIM] / acc[:QB, ATT_V_DIM:ATT_V_DIM + 1]
    o1 = acc[QB:, :ATT_V_DIM] / acc[QB:, ATT_V_DIM:ATT_V_DIM + 1]
    of = o0 - lam * o1
    of = of * lax.rsqrt(jnp.mean(of * of, axis=-1, keepdims=True) + LN_EPS)
    o_ref[0] = (of * (g_ref[...] * out_scale)).astype(o_ref.dtype)


def _attention(proj3, bias_seq, lam, subln_g, lam_init, side):
    b, s, _ = proj3.shape
    n_kc = s // KC
    n_q = s // QB
    n_steps = b * ATT_HEADS * n_q
    h_off_k = ATT_HEADS
    h_off_v = 2 * ATT_HEADS
    kern = functools.partial(_attn_kernel, n_kc=n_kc, out_scale=1.0 - lam_init)

    side_specs = []
    for arr in side:
        rows, cols = arr.shape
        blk = max(rows // n_steps, 2 * SUBLANES)
        hold = blk * n_steps // rows
        assert rows % blk == 0 and hold * rows == blk * n_steps
        side_specs.append(pl.BlockSpec(
            (blk, cols),
            lambda bi, h, qi, hold=hold: (((bi * ATT_HEADS + h) * n_q + qi) // hold, 0)))
    return pl.pallas_call(
        kern,
        grid=(b, ATT_HEADS, n_q),
        in_specs=[
            pl.BlockSpec(memory_space=pltpu.SMEM),
            pl.BlockSpec((1, QB, LANES), lambda bi, h, qi: (bi, qi, h)),
            pl.BlockSpec((1, s, LANES), lambda bi, h, qi: (bi, 0, h_off_k + h)),
            pl.BlockSpec((1, s, LANES), lambda bi, h, qi: (bi, 0, h_off_v + h)),
            pl.BlockSpec((1, N_BIAS_TILES, 2 * KC), lambda bi, h, qi: (h, 0, 0)),
            pl.BlockSpec((1, ATT_V_DIM), lambda bi, h, qi: (0, 0)),
        ] + side_specs,
        out_specs=[pl.BlockSpec((1, QB, ATT_V_DIM), lambda bi, h, qi: (bi, qi, h))] + side_specs,
        out_shape=[jax.ShapeDtypeStruct((b, s, ATT_WIDTH), BF16)]
        + [jax.ShapeDtypeStruct(arr.shape, BF16) for arr in side],
        scratch_shapes=[pltpu.VMEM((s, 2 * ATT_V_DIM), BF16),
                        pltpu.VMEM((N_BIAS_TILES, QB, KC), F32)],
        compiler_params=_cparams(("arbitrary", "arbitrary", "arbitrary")),
        name="attn",
    )(lam, proj3, proj3, proj3, bias_seq, subln_g, *side)


def _rel_bucket(rel):
    half = REL_BUCKETS // 2
    max_exact = half // 2
    ret = jnp.where(rel > 0, half, 0)
    n = jnp.abs(rel)
    nf = jnp.maximum(n, 1).astype(F32)
    large = max_exact + (jnp.log(nf / max_exact) / math.log(REL_MAX_DIST / max_exact)
                         * (half - max_exact)).astype(I32)
    large = jnp.minimum(large, half - 1)
    return ret + jnp.where(n < max_exact, n, large)


def _bias_sequences(rel_bias, s):
    assert REL_MAX_DIST <= KC and QB == KC
    period = 2 * KC
    j = jnp.arange(period, dtype=I32)
    off = jnp.where(j < KC, j, j - period)
    d = jnp.arange(-2, 3, dtype=I32)[:, None]
    rel = jnp.clip(d * KC + off[None, :], -(s - 1), s - 1)
    return jnp.transpose(rel_bias[_rel_bucket(rel)], (2, 0, 1)).astype(F32) * LOG2E


def _layernorm(v, g, b):
    mu = jnp.mean(v, axis=-1, keepdims=True)
    var = jnp.mean(jnp.square(v - mu), axis=-1, keepdims=True)
    return (v - mu) * lax.rsqrt(var + LN_EPS) * g + b


def _first_argmax(vals, nrows):
    mx = jnp.max(vals, axis=0, keepdims=True)
    row = lax.broadcasted_iota(I32, vals.shape, 0)
    idx = jnp.min(jnp.where(vals == mx, row, nrows), axis=0, keepdims=True)
    return mx, idx


def _mix_kernel(a_ref, u_ref, vg_ref, x_ref, wout_ref, sgw_ref, sgb_ref, sgg_ref, sgbeta_ref,
                g1_ref, b1_ref, wr_ref, br_ref,
                h1_ref, xs_hbm, ri_ref, rw_ref, cnt_ref,
                run_ref, hp_s, posv_ref, psm, ssem, rsem):
    i = pl.program_id(0)
    n = pl.num_programs(0)
    slot = i % 2
    prev = 1 - slot

    def pos_copy(sl):
        return pltpu.make_async_copy(posv_ref, psm.at[sl], ssem.at[sl])

    def row_copy(sl, which, j):
        return pltpu.make_async_copy(hp_s.at[sl, pl.ds(j, 1)],
                                     xs_hbm.at[pl.ds(psm[sl, which, j], 1)], rsem.at[sl])

    def drain(sl):
        whole = pltpu.make_async_copy(hp_s.at[sl], xs_hbm.at[pl.ds(0, TM_MIX)], rsem.at[sl])
        whole.wait()
        whole.wait()

    @pl.when(i == 0)
    def _():
        run_ref[...] = jnp.zeros_like(run_ref)
        hp_s[1] = jnp.zeros(hp_s.shape[1:], hp_s.dtype)
        spare = N_EXPERTS * XS_CAP + lax.broadcasted_iota(I32, posv_ref.shape, 1)
        spare = spare + TM_MIX * lax.broadcasted_iota(I32, posv_ref.shape, 0)
        posv_ref[...] = jnp.where(spare < xs_hbm.shape[0], spare, 0)
        pos_copy(1).start()

    @pl.when(i > 0)
    def _():
        drain(slot)

    pos_copy(prev).wait()

    for j in range(TM_MIX):
        row_copy(prev, 0, j).start(priority=0)
        row_copy(prev, 1, j).start(priority=1)

    h1_ref[...] = jnp.dot(a_ref[...], wout_ref[:ATT_WIDTH, :], preferred_element_type=F32)

    n_chunks = TM_MIX // SG_CHUNK
    rows = []
    for c in range(n_chunks):
        cols = []
        for g in range(SG_GROUPS):
            rs = slice(c * SG_CHUNK, (c + 1) * SG_CHUNK)
            cs = slice(g * SG_DIM, (g + 1) * SG_DIM)
            vt = vg_ref[rs, cs].astype(F32)
            vn = _layernorm(vt, sgg_ref[g:g + 1, :], sgbeta_ref[g:g + 1, :])
            mixed = jnp.dot(sgw_ref[g], vn.astype(BF16), preferred_element_type=F32) + sgb_ref[g]
            cols.append((u_ref[rs, cs].astype(F32) * mixed).astype(BF16))
        rows.append(jnp.concatenate(cols, axis=1))
    sgate = jnp.concatenate(rows, axis=0)

    mix = h1_ref[...] + jnp.dot(sgate, wout_ref[ATT_WIDTH:, :], preferred_element_type=F32)
    h1 = _layernorm(DN_ALPHA * x_ref[...] + mix, g1_ref[...], b1_ref[...])
    h1_ref[...] = h1
    hp_s[slot] = h1.astype(BF16).reshape(hp_s.shape[1:])

    nt = (((1,), (1,)), ((), ()))
    h1_hi = h1.astype(BF16)
    h1_lo = (h1 - h1_hi.astype(F32)).astype(BF16)
    both = lax.dot_general(wr_ref[...], h1_hi, nt, preferred_element_type=F32)
    lt = both[:NR_PAD] + both[NR_PAD:] + lax.dot_general(
        wr_ref[:NR_PAD, :], h1_lo, nt, preferred_element_type=F32)
    lt = lt + br_ref[...]
    g_logits = lt[0:N_GROUPS]
    gmax, g_idx = _first_argmax(g_logits, N_GROUPS)
    g_gate = 1.0 / jnp.sum(jnp.exp(g_logits - gmax), axis=0, keepdims=True)
    e_logits = jnp.zeros((EXPERTS_PER_GROUP, TM_MIX), F32)
    for g in range(N_GROUPS):
        lo = NR_EXPERT0 + g * EXPERTS_PER_GROUP
        e_logits = jnp.where(g_idx == g, lt[lo:lo + EXPERTS_PER_GROUP], e_logits)
    v1, i1 = _first_argmax(e_logits, EXPERTS_PER_GROUP)
    erow = lax.broadcasted_iota(I32, e_logits.shape, 0)
    v2, i2 = _first_argmax(jnp.where(erow == i1, -jnp.inf, e_logits), EXPERTS_PER_GROUP)
    t = jnp.exp(v2 - v1)
    w1 = g_gate / (1.0 + t)
    w2 = g_gate * t / (1.0 + t)
    e1 = g_idx * EXPERTS_PER_GROUP + i1
    e2 = g_idx * EXPERTS_PER_GROUP + i2

    xrow = lax.broadcasted_iota(I32, (N_EXPERTS, TM_MIX), 0)
    oh1 = xrow == e1
    oh2 = xrow == e2
    oh = jnp.where(jnp.logical_or(oh1, oh2), 1.0, 0.0)
    tr = lax.broadcasted_iota(I32, (TM_MIX, TM_MIX), 0)
    tc = lax.broadcasted_iota(I32, (TM_MIX, TM_MIX), 1)
    before = jnp.where(tr < tc, 1.0, 0.0).astype(BF16)
    cum = jnp.dot(oh.astype(BF16), before, preferred_element_type=F32)
    base = cum + run_ref[:, 0:1]
    rank1 = jnp.sum(jnp.where(oh1, base, 0.0), axis=0, keepdims=True).astype(I32)
    rank2 = jnp.sum(jnp.where(oh2, base, 0.0), axis=0, keepdims=True).astype(I32)
    run_ref[...] = run_ref[...] + jnp.sum(oh, axis=1, keepdims=True)
    cnt_ref[...] = run_ref[...].astype(I32)

    zi = jnp.zeros((ROUTE_ROWS - 4, TM_MIX), I32)
    ri_ref[...] = jnp.concatenate([e1, e2, rank1, rank2, zi], axis=0)
    zw = jnp.zeros((ROUTE_ROWS - 2, TM_MIX), F32)
    rw_ref[...] = jnp.concatenate([w1, w2, zw], axis=0)

    posv_ref[...] = jnp.concatenate([e1 * XS_CAP + rank1, e2 * XS_CAP + rank2, zi, zi[:2]], axis=0)
    pos_copy(slot).start()

    @pl.when(i == n - 1)
    def _():
        pos_copy(slot).wait()

        def last(j, c):
            row_copy(slot, 0, j).start(priority=0)
            row_copy(slot, 1, j).start(priority=1)
            return c
        lax.fori_loop(0, TM_MIX, last, 0, unroll=8)
        drain(prev)
        drain(slot)


def _mix(a2, proj2, x2, wout_b, sgw_b, sgb, sgg, sgbeta, g1, b1, wr, br):
    t, d = x2.shape
    assert XS_CAP >= t + TMX
    u_blk = (proj2.shape[1] - 2 * SG_WIDTH) // SG_WIDTH
    const = lambda *shape: pl.BlockSpec(shape, lambda i: (0,) * len(shape))
    return pl.pallas_call(
        _mix_kernel,
        grid=(t // TM_MIX,),
        in_specs=[
            pl.BlockSpec((TM_MIX, ATT_WIDTH), lambda i: (i, 0)),
            pl.BlockSpec((TM_MIX, SG_WIDTH), lambda i: (i, u_blk)),
            pl.BlockSpec((TM_MIX, SG_WIDTH), lambda i: (i, u_blk + 1)),
            pl.BlockSpec((TM_MIX, d), lambda i: (i, 0)),
            pl.BlockSpec((d, d), lambda i: (0, 0), pipeline_mode=pl.Buffered(1)),
            const(SG_GROUPS, SG_CHUNK, SG_CHUNK),
            const(SG_GROUPS, SG_CHUNK, 1),
            const(SG_GROUPS, SG_DIM),
            const(SG_GROUPS, SG_DIM),
            const(1, d),
            const(1, d),
            const(2 * NR_PAD, d),
            const(NR_PAD, 1),
        ],
        out_specs=[
            pl.BlockSpec((TM_MIX, d), lambda i: (i, 0)),
            pl.BlockSpec(memory_space=pl.ANY),
            pl.BlockSpec((ROUTE_ROWS, TM_MIX), lambda i: (0, i)),
            pl.BlockSpec((ROUTE_ROWS, TM_MIX), lambda i: (0, i)),
            const(N_EXPERTS, LANES),
        ],
        out_shape=[
            jax.ShapeDtypeStruct((t, d), F32),
            jax.ShapeDtypeStruct((N_EXPERTS * XS_CAP + 2 * TM_MIX, d // LANES, LANES), BF16),
            jax.ShapeDtypeStruct((ROUTE_ROWS, t), I32),
            jax.ShapeDtypeStruct((ROUTE_ROWS, t), F32),
            jax.ShapeDtypeStruct((N_EXPERTS, LANES), I32),
        ],
        scratch_shapes=[pltpu.VMEM((N_EXPERTS, LANES), F32),
                        pltpu.VMEM((2, TM_MIX, d // LANES, LANES), BF16),
                        pltpu.VMEM((ROUTE_ROWS, TM_MIX), I32),
                        pltpu.SMEM((2, ROUTE_ROWS, TM_MIX), I32),
                        pltpu.SemaphoreType.DMA((2,)),
                        pltpu.SemaphoreType.DMA((2,))],
        compiler_params=_cparams(("arbitrary",)),
        name="mix",
    )(a2, proj2, proj2, x2, wout_b, sgw_b, sgb, sgg, sgbeta, g1, b1, wr, br)


def _experts_kernel(ts_ref, nt_ref, nu_ref, trow_ref, nval_ref, x_hbm, wg_ref, wu_ref, wd_ref,
                    y_hbm, xbuf, ybuf, xsem, ysem):
    e = pl.program_id(0)
    d = wg_ref.shape[1]
    n_used = nu_ref[0]

    def x_copy(g):
        rows = pl.ds(pl.multiple_of(trow_ref[g], TMX), TMX)
        slot = lax.rem(g, N_XSLOTS)
        return pltpu.make_async_copy(x_hbm.at[rows], xbuf.at[slot], xsem.at[slot])

    def y_copy(g):
        rows = pl.ds(pl.multiple_of(g * TMX, TMX), TMX)
        slot = lax.rem(g, N_XSLOTS)
        return pltpu.make_async_copy(ybuf.at[slot], y_hbm.at[rows], ysem.at[slot])

    def start_x(g):
        @pl.when(g < n_used)
        def _():
            x_copy(g).start(priority=1)

    def free_y(g):
        @pl.when(g >= N_XSLOTS)
        def _():
            y_copy(g - N_XSLOTS).wait()

    def load(g):
        xt = xbuf[lax.rem(g, N_XSLOTS)].reshape(TMX, d)
        row = lax.broadcasted_iota(I32, xt.shape, 0)
        return jnp.where(row < nval_ref[g], xt, jnp.zeros_like(xt))

    def mlp(xb):
        gate = jnp.dot(xb, wg_ref[0], preferred_element_type=F32)
        up = jnp.dot(xb, wu_ref[0], preferred_element_type=F32)
        hmid = (gate * jax.nn.sigmoid(gate) * up).astype(BF16)
        return jnp.dot(hmid, wd_ref[0], preferred_element_type=F32)

    @pl.when(e == 0)
    def _():
        for g in range(X_AHEAD):
            start_x(g)

    def group(k, first):
        def body(t, carry):
            g = first(t)
            for j in range(k):
                x_copy(g + j).wait()
            for j in range(k):
                start_x(g + X_AHEAD + j)
                free_y(g + j)
            y = mlp(jnp.concatenate([load(g + j) for j in range(k)], axis=0))
            for j in range(k):
                ybuf[lax.rem(g + j, N_XSLOTS)] = y[j * TMX:(j + 1) * TMX].astype(BF16).reshape(
                    ybuf.shape[1:])
                y_copy(g + j).start(priority=1)
            return carry
        return body

    nt = nt_ref[e]
    n3 = jnp.where(jnp.logical_and(nt >= 3, nt % 2 == 1), 1, 0)
    n2 = (nt - 3 * n3) // 2
    n1 = nt - 3 * n3 - 2 * n2
    base = ts_ref[e]
    lax.fori_loop(0, n3, group(3, lambda t: base), 0)
    lax.fori_loop(0, n2, group(2, lambda t: base + 3 * n3 + 2 * t), 0)
    lax.fori_loop(0, n1, group(1, lambda t: base + 3 * n3 + 2 * n2), 0)

    @pl.when(e == pl.num_programs(0) - 1)
    def _():
        for back in range(N_XSLOTS, 0, -1):
            @pl.when(n_used >= back)
            def _():
                y_copy(n_used - back).wait()


def _experts(tile_start, n_tiles_e, n_used, tile_row, tile_valid, xs, w_gate, w_up, w_down):
    d = w_gate.shape[1]
    n_rows = tile_row.shape[0] * TMX

    def w_map(e, ts, nt, nu, trow, nval):
        return (e, 0, 0)

    grid_spec = pltpu.PrefetchScalarGridSpec(
        num_scalar_prefetch=5,
        grid=(N_EXPERTS,),
        in_specs=[
            pl.BlockSpec(memory_space=pl.ANY),
            pl.BlockSpec((1, d, D_EXPERT), w_map),
            pl.BlockSpec((1, d, D_EXPERT), w_map),
            pl.BlockSpec((1, D_EXPERT, d), w_map),
        ],
        out_specs=pl.BlockSpec(memory_space=pl.ANY),
        scratch_shapes=[pltpu.VMEM((N_XSLOTS, TMX) + xs.shape[1:], BF16),
                        pltpu.VMEM((N_XSLOTS, TMX) + xs.shape[1:], BF16),
                        pltpu.SemaphoreType.DMA((N_XSLOTS,)),
                        pltpu.SemaphoreType.DMA((N_XSLOTS,))],
    )
    return pl.pallas_call(
        _experts_kernel,
        grid_spec=grid_spec,
        out_shape=jax.ShapeDtypeStruct((n_rows,) + xs.shape[1:], BF16),
        compiler_params=_cparams(("arbitrary",)),
        name="experts",
    )(tile_start, n_tiles_e, n_used, tile_row, tile_valid, xs, w_gate, w_up, w_down)


def _combine_kernel(pos_ref, posn_ref, y_hbm, h1_ref, wt_ref, g2_ref, b2_ref, o_ref, ybuf, sem):
    i = pl.program_id(0)
    n = pl.num_programs(0)
    slot = i % 2

    def row_copy(src, sl, which, j):
        return pltpu.make_async_copy(y_hbm.at[pl.ds(src, 1)],
                                     ybuf.at[sl, which, pl.ds(j, 1)], sem.at[sl])

    def issue(p_ref, sl):
        def body(j, c):
            row_copy(p_ref[0, j], sl, 0, j).start(priority=0)
            row_copy(p_ref[1, j], sl, 1, j).start(priority=1)
            return c
        lax.fori_loop(0, TM_OUT, body, 0, unroll=8)

    def drain(sl):
        for which in range(2):
            pltpu.make_async_copy(y_hbm.at[pl.ds(0, TM_OUT)], ybuf.at[sl, which],
                                  sem.at[sl]).wait()

    @pl.when(i == 0)
    def _():
        issue(pos_ref, 0)

    drain(slot)

    for j in range(TM_OUT):
        row_copy(posn_ref[0, j], 1 - slot, 0, j).start(priority=0)
        row_copy(posn_ref[1, j], 1 - slot, 1, j).start(priority=1)

    y1 = ybuf[slot, 0].reshape(h1_ref.shape).astype(F32)
    y2 = ybuf[slot, 1].reshape(h1_ref.shape).astype(F32)
    ffn = wt_ref[:, 0:1] * y1 + wt_ref[:, 1:2] * y2
    o_ref[...] = _layernorm(DN_ALPHA * h1_ref[...] + ffn, g2_ref[...], b2_ref[...])

    @pl.when(i == n - 1)
    def _():
        drain(1 - slot)


def _combine(pos3, ys, h1, wtok, g2, b2):
    t, d = h1.shape
    n = t // TM_OUT
    return pl.pallas_call(
        _combine_kernel,
        grid=(n,),
        in_specs=[
            pl.BlockSpec((None, 2, TM_OUT), lambda i: (i, 0, 0), memory_space=pltpu.SMEM),
            pl.BlockSpec((None, 2, TM_OUT), lambda i: (jnp.minimum(i + 1, n - 1), 0, 0),
                         memory_space=pltpu.SMEM),
            pl.BlockSpec(memory_space=pl.ANY),
            pl.BlockSpec((TM_OUT, d), lambda i: (i, 0)),
            pl.BlockSpec((TM_OUT, ROUTE_ROWS), lambda i: (i, 0)),
            pl.BlockSpec((1, d), lambda i: (0, 0)),
            pl.BlockSpec((1, d), lambda i: (0, 0)),
        ],
        out_specs=pl.BlockSpec((TM_OUT, d), lambda i: (i, 0)),
        out_shape=jax.ShapeDtypeStruct((t, d), F32),
        scratch_shapes=[pltpu.VMEM((2, 2, TM_OUT) + ys.shape[1:], BF16),
                        pltpu.SemaphoreType.DMA((2,))],
        compiler_params=_cparams(("arbitrary",)),
        name="combine",
    )(pos3, pos3, ys, h1, wtok, g2, b2)


def _lambda_init(layer):
    return 0.8 - 0.6 * math.exp(-0.3 * layer)


def kernel(x, w_in, w_out, ln1_g, ln1_b, ln2_g, ln2_b, rel_bias, lam_q1, lam_k1, lam_q2, lam_k2,
           subln_g, sg_ln_g, sg_ln_b, sg_w, sg_b, w_router_group, b_router_group,
           w_router_expert, b_router_expert, w_exp_gate, w_exp_up, w_exp_down):
    b, s, d = x.shape
    t = b * s
    assert w_in.shape[0] == DEPTH == 1
    l = 0
    lam_init = _lambda_init(l)
    x2 = x.reshape(t, d)

    proj = _inproj(x2, w_in[l])

    lam = (jnp.exp(jnp.sum(lam_q1[l].astype(F32) * lam_k1[l].astype(F32)))
           - jnp.exp(jnp.sum(lam_q2[l].astype(F32) * lam_k2[l].astype(F32))) + lam_init)
    side = [w_exp_gate[l].reshape(-1, D_EXPERT), w_exp_up[l].reshape(-1, D_EXPERT),
            w_exp_down[l].reshape(-1, d), w_out[l]]
    a, wg_b, wu_b, wd_b, wout_b = _attention(
        proj.reshape(b, s, -1), _bias_sequences(rel_bias, s), lam.reshape(1),
        subln_g[l].reshape(1, -1), lam_init, side)

    wr = jnp.concatenate([w_router_group[l].T,
                          jnp.zeros((NR_EXPERT0 - N_GROUPS, d), F32),
                          jnp.transpose(w_router_expert[l], (0, 2, 1)).reshape(N_EXPERTS, d)],
                         axis=0)
    br = jnp.concatenate([b_router_group[l], jnp.zeros((NR_EXPERT0 - N_GROUPS,), F32),
                          b_router_expert[l].reshape(-1)]).reshape(NR_PAD, 1)
    wr_hi = wr.astype(BF16)
    wr = jnp.concatenate([wr_hi, (wr - wr_hi.astype(F32)).astype(BF16)], axis=0)
    h1, xs, ri, rw, cnt = _mix(
        a.reshape(t, -1), proj, x2, wout_b, sg_w[l].astype(BF16),
        sg_b[l].reshape(SG_GROUPS, SG_CHUNK, 1), sg_ln_g[l], sg_ln_b[l],
        ln1_g[l].reshape(1, d), ln1_b[l].reshape(1, d), wr, br)

    counts = cnt[:, 0]
    tiles_e = (counts + TMX - 1) // TMX
    tile_end = jnp.cumsum(tiles_e)
    tile_start = tile_end - tiles_e
    n_used = tile_end[-1:]
    max_tiles = (2 * t) // TMX + N_EXPERTS
    row_start = (tile_start * TMX).astype(I32)
    eids = jnp.arange(N_EXPERTS, dtype=I32)[:, None, None]
    pos = jnp.sum(jnp.where(ri[0:2][None] == eids, row_start[:, None, None], 0), axis=0) + ri[2:4]
    pos3 = pos.reshape(2, t // TM_OUT, TM_OUT).transpose(1, 0, 2)
    tile_ids = jnp.arange(max_tiles, dtype=I32)
    tile_e = jnp.minimum(jnp.sum(tile_end[None, :] <= tile_ids[:, None], axis=1), N_EXPERTS - 1)
    local = tile_ids - tile_start[tile_e]
    in_use = tile_ids < n_used[0]
    tile_row = jnp.where(in_use, tile_e * XS_CAP + local * TMX, 0).astype(I32)
    tile_valid = jnp.where(in_use, jnp.clip(counts[tile_e] - local * TMX, 0, TMX), 0).astype(I32)

    ys = _experts(tile_start.astype(I32), tiles_e.astype(I32), n_used.astype(I32),
                  tile_row, tile_valid, xs,
                  wg_b.reshape(N_EXPERTS, d, D_EXPERT), wu_b.reshape(N_EXPERTS, d, D_EXPERT),
                  wd_b.reshape(N_EXPERTS, D_EXPERT, d))
    out = _combine(pos3, ys, h1, rw.T, ln2_g[l].reshape(1, d), ln2_b[l].reshape(1, d))
    return out.reshape(b, s, d)
```

```python
import functools
import math

import jax
import jax.numpy as jnp
from jax import lax
from jax.experimental import pallas as pl
from jax.experimental.pallas import tpu as pltpu

F32 = jnp.float32
BF16 = jnp.bfloat16
I32 = jnp.int32

ATT_HEADS = 8
ATT_QK_DIM = 64
ATT_V_DIM = 128
ATT_WIDTH = ATT_HEADS * ATT_V_DIM
SG_GROUPS = 8
SG_DIM = 128
SG_WIDTH = SG_GROUPS * SG_DIM
SG_CHUNK = 128
REL_BUCKETS = 32
REL_MAX_DIST = 128
N_GROUPS = 4
EXPERTS_PER_GROUP = 8
N_EXPERTS = N_GROUPS * EXPERTS_PER_GROUP
D_EXPERT = 256
DEPTH = 1
DN_ALPHA = (2.0 * DEPTH) ** 0.25
LN_EPS = 1e-5
LOG2E = math.log2(math.e)

LANES = 128
SUBLANES = 8
VMEM_LIMIT_BYTES = 56 * 1024 * 1024

TM_IN = 1024
TN_IN = 1024
W_AHEAD = 2
W_SLOTS = W_AHEAD + 1
QB = 256
KC = 256
N_BIAS_TILES = 5
TM_MIX = 512
TMX = 256
TM_OUT = 256
X_AHEAD = 3
N_XSLOTS = 2 * X_AHEAD
ROUTE_ROWS = 8
XS_CAP = 8192 + TMX
NR_EXPERT0 = 8
NR_PAD = NR_EXPERT0 + 32


def _cparams(sem, flags=None):
    return pltpu.CompilerParams(dimension_semantics=sem, vmem_limit_bytes=VMEM_LIMIT_BYTES,
                                flags=flags)


def _inproj_kernel(x_ref, w_hbm, o_ref, xb_ref, wbuf, wsem, *, n_q, n_lin, q_scale):
    j = pl.program_id(1)
    nj = pl.num_programs(1)
    step = pl.program_id(0) * nj + j
    n_steps = pl.num_programs(0) * nj

    def w_copy(s):
        cols = pl.ds(pl.multiple_of(lax.rem(s, nj) * TN_IN, TN_IN), TN_IN)
        slot = lax.rem(s, W_SLOTS)
        return pltpu.make_async_copy(w_hbm.at[:, cols], wbuf.at[slot], wsem.at[slot])

    @pl.when(step == 0)
    def _():
        for s in range(W_AHEAD):
            w_copy(s).start()

    @pl.when(step + W_AHEAD < n_steps)
    def _():
        w_copy(step + W_AHEAD).start()

    @pl.when(j == 0)
    def _():
        xb_ref[...] = x_ref[...].astype(BF16)

    w_copy(step).wait()
    w = wbuf[lax.rem(step, W_SLOTS)].astype(BF16)
    acc = jnp.dot(xb_ref[...], w, preferred_element_type=F32)

    @pl.when(j < n_q)
    def _():
        o_ref[...] = (acc * q_scale).astype(BF16)

    @pl.when(jnp.logical_and(j >= n_q, j < n_lin))
    def _():
        o_ref[...] = acc.astype(BF16)

    @pl.when(j >= n_lin)
    def _():
        o_ref[...] = jax.nn.gelu(acc).astype(BF16)


def _inproj(x2, w_in):
    t, d = x2.shape
    n = w_in.shape[1]
    assert (t // TM_IN) * (n // TN_IN) >= W_AHEAD
    n_q = (ATT_HEADS * 2 * ATT_QK_DIM) // TN_IN
    n_lin = (2 * ATT_HEADS * 2 * ATT_QK_DIM + ATT_WIDTH) // TN_IN
    kern = functools.partial(_inproj_kernel, n_q=n_q, n_lin=n_lin,
                             q_scale=(ATT_QK_DIM ** -0.5) * LOG2E)
    return pl.pallas_call(
        kern,
        grid=(t // TM_IN, n // TN_IN),
        in_specs=[pl.BlockSpec((TM_IN, d), lambda i, j: (i, 0)),
                  pl.BlockSpec(memory_space=pl.ANY)],
        out_specs=pl.BlockSpec((TM_IN, TN_IN), lambda i, j: (i, j)),
        out_shape=jax.ShapeDtypeStruct((t, n), BF16),
        scratch_shapes=[pltpu.VMEM((TM_IN, d), BF16),
                        pltpu.VMEM((W_SLOTS, d, TN_IN), w_in.dtype),
                        pltpu.SemaphoreType.DMA((W_SLOTS,))],
        compiler_params=_cparams(("arbitrary", "arbitrary")),
        name="inproj",
    )(x2, w_in)


def _attn_kernel(lam_ref, q_ref, k_ref, v_ref, bseq_ref, g_ref, *rest, n_kc, out_scale):
    n_side = (len(rest) - 3) // 2
    side_in, o_ref, side_out = rest[:n_side], rest[n_side], rest[n_side + 1:2 * n_side + 1]
    vaug_ref, bias_ref = rest[2 * n_side + 1:]
    qi = pl.program_id(2)

    @pl.when(qi == 0)
    def _():
        vaug_ref[:, :ATT_V_DIM] = v_ref[0]
        vaug_ref[:, ATT_V_DIM:] = jnp.ones((vaug_ref.shape[0], ATT_V_DIM), BF16)
        for t in range(N_BIAS_TILES):
            seq = jnp.broadcast_to(bseq_ref[0, t:t + 1, :], (QB, 2 * KC))
            bias_ref[t] = pltpu.roll(seq, 0, 1, stride=1, stride_axis=0)[:, :KC]

    q = q_ref[0]
    lane = lax.broadcasted_iota(I32, q.shape, 1)
    zero = jnp.zeros_like(q)
    qq = jnp.concatenate([jnp.where(lane < ATT_QK_DIM, q, zero),
                          jnp.where(lane >= ATT_QK_DIM, q, zero)], axis=0)

    for src, dst in zip(side_in, side_out):
        dst[...] = src[...].astype(dst.dtype)

    m = jnp.full((2 * QB, 1), -jnp.inf, F32)
    acc = jnp.zeros((2 * QB, 2 * ATT_V_DIM), F32)
    for kc in range(n_kc):
        kblk = k_ref[0, kc * KC:(kc + 1) * KC, :]
        s = lax.dot_general(qq, kblk, (((1,), (1,)), ((), ())),
                            preferred_element_type=F32)
        b = bias_ref[jnp.clip(kc - qi, -2, 2) + 2]
        s = s + jnp.concatenate([b, b], axis=0)
        m_new = jnp.maximum(m, jnp.max(s, axis=1, keepdims=True))
        p = jnp.exp2(s - m_new).astype(BF16)
        acc = acc * jnp.exp2(m - m_new) + jnp.dot(
            p, vaug_ref[kc * KC:(kc + 1) * KC, :], preferred_element_type=F32)
        m = m_new
    lam = lam_ref[0]
    o0 = acc[:QB, :ATT_V_DIM] / acc[:QB, ATT_V_DIM:ATT_V_DIM + 1]
    o1 = acc[QB:, :ATT_V_DIM] / acc[QB:, ATT_V_DIM:ATT_V_DIM + 1]
    of = o0 - lam * o1
    of = of * lax.rsqrt(jnp.mean(of * of, axis=-1, keepdims=True) + LN_EPS)
    o_ref[0] = (of * (g_ref[...] * out_scale)).astype(o_ref.dtype)


def _attention(proj3, bias_seq, lam, subln_g, lam_init, side):
    b, s, _ = proj3.shape
    n_kc = s // KC
    n_q = s // QB
    n_steps = b * ATT_HEADS * n_q
    h_off_k = ATT_HEADS
    h_off_v = 2 * ATT_HEADS
    kern = functools.partial(_attn_kernel, n_kc=n_kc, out_scale=1.0 - lam_init)

    side_specs = []
    for arr in side:
        rows, cols = arr.shape
        blk = max(rows // n_steps, 2 * SUBLANES)
        hold = blk * n_steps // rows
        assert rows % blk == 0 and hold * rows == blk * n_steps
        side_specs.append(pl.BlockSpec(
            (blk, cols),
            lambda bi, h, qi, hold=hold: (((bi * ATT_HEADS + h) * n_q + qi) // hold, 0)))
    return pl.pallas_call(
        kern,
        grid=(b, ATT_HEADS, n_q),
        in_specs=[
            pl.BlockSpec(memory_space=pltpu.SMEM),
            pl.BlockSpec((1, QB, LANES), lambda bi, h, qi: (bi, qi, h)),
            pl.BlockSpec((1, s, LANES), lambda bi, h, qi: (bi, 0, h_off_k + h)),
            pl.BlockSpec((1, s, LANES), lambda bi, h, qi: (bi, 0, h_off_v + h)),
            pl.BlockSpec((1, N_BIAS_TILES, 2 * KC), lambda bi, h, qi: (h, 0, 0)),
            pl.BlockSpec((1, ATT_V_DIM), lambda bi, h, qi: (0, 0)),
        ] + side_specs,
        out_specs=[pl.BlockSpec((1, QB, ATT_V_DIM), lambda bi, h, qi: (bi, qi, h))] + side_specs,
        out_shape=[jax.ShapeDtypeStruct((b, s, ATT_WIDTH), BF16)]
        + [jax.ShapeDtypeStruct(arr.shape, BF16) for arr in side],
        scratch_shapes=[pltpu.VMEM((s, 2 * ATT_V_DIM), BF16),
                        pltpu.VMEM((N_BIAS_TILES, QB, KC), F32)],
        compiler_params=_cparams(("arbitrary", "arbitrary", "arbitrary")),
        name="attn",
    )(lam, proj3, proj3, proj3, bias_seq, subln_g, *side)


def _rel_bucket(rel):
    half = REL_BUCKETS // 2
    max_exact = half // 2
    ret = jnp.where(rel > 0, half, 0)
    n = jnp.abs(rel)
    nf = jnp.maximum(n, 1).astype(F32)
    large = max_exact + (jnp.log(nf / max_exact) / math.log(REL_MAX_DIST / max_exact)
                         * (half - max_exact)).astype(I32)
    large = jnp.minimum(large, half - 1)
    return ret + jnp.where(n < max_exact, n, large)


def _bias_sequences(rel_bias, s):
    assert REL_MAX_DIST <= KC and QB == KC
    period = 2 * KC
    j = jnp.arange(period, dtype=I32)
    off = jnp.where(j < KC, j, j - period)
    d = jnp.arange(-2, 3, dtype=I32)[:, None]
    rel = jnp.clip(d * KC + off[None, :], -(s - 1), s - 1)
    return jnp.transpose(rel_bias[_rel_bucket(rel)], (2, 0, 1)).astype(F32) * LOG2E


def _layernorm(v, g, b):
    mu = jnp.mean(v, axis=-1, keepdims=True)
    var = jnp.mean(jnp.square(v - mu), axis=-1, keepdims=True)
    return (v - mu) * lax.rsqrt(var + LN_EPS) * g + b


def _first_argmax(vals, nrows):
    mx = jnp.max(vals, axis=0, keepdims=True)
    row = lax.broadcasted_iota(I32, vals.shape, 0)
    idx = jnp.min(jnp.where(vals == mx, row, nrows), axis=0, keepdims=True)
    return mx, idx


def _mix_kernel(a_ref, u_ref, vg_ref, x_ref, wout_ref, sgw_ref, sgb_ref, sgg_ref, sgbeta_ref,
                g1_ref, b1_ref, wr_ref, br_ref,
                h1_ref, xs_hbm, ri_ref, rw_ref, cnt_ref,
                run_ref, hp_s, posv_ref, psm, ssem, rsem):
    i = pl.program_id(0)
    n = pl.num_programs(0)
    slot = i % 2
    prev = 1 - slot

    def pos_copy(sl):
        return pltpu.make_async_copy(posv_ref, psm.at[sl], ssem.at[sl])

    def row_copy(sl, which, j):
        return pltpu.make_async_copy(hp_s.at[sl, pl.ds(j, 1)],
                                     xs_hbm.at[pl.ds(psm[sl, which, j], 1)], rsem.at[sl])

    def drain(sl):
        whole = pltpu.make_async_copy(hp_s.at[sl], xs_hbm.at[pl.ds(0, TM_MIX)], rsem.at[sl])
        whole.wait()
        whole.wait()

    @pl.when(i == 0)
    def _():
        run_ref[...] = jnp.zeros_like(run_ref)
        hp_s[1] = jnp.zeros(hp_s.shape[1:], hp_s.dtype)
        spare = N_EXPERTS * XS_CAP + lax.broadcasted_iota(I32, posv_ref.shape, 1)
        spare = spare + TM_MIX * lax.broadcasted_iota(I32, posv_ref.shape, 0)
        posv_ref[...] = jnp.where(spare < xs_hbm.shape[0], spare, 0)
        pos_copy(1).start()

    @pl.when(i > 0)
    def _():
        drain(slot)

    pos_copy(prev).wait()

    for j in range(TM_MIX):
        row_copy(prev, 0, j).start(priority=0)
        row_copy(prev, 1, j).start(priority=1)

    h1_ref[...] = jnp.dot(a_ref[...], wout_ref[:ATT_WIDTH, :], preferred_element_type=F32)

    n_chunks = TM_MIX // SG_CHUNK
    rows = []
    for c in range(n_chunks):
        cols = []
        for g in range(SG_GROUPS):
            rs = slice(c * SG_CHUNK, (c + 1) * SG_CHUNK)
            cs = slice(g * SG_DIM, (g + 1) * SG_DIM)
            vt = vg_ref[rs, cs].astype(F32)
            vn = _layernorm(vt, sgg_ref[g:g + 1, :], sgbeta_ref[g:g + 1, :])
            mixed = jnp.dot(sgw_ref[g], vn.astype(BF16), preferred_element_type=F32) + sgb_ref[g]
            cols.append((u_ref[rs, cs].astype(F32) * mixed).astype(BF16))
        rows.append(jnp.concatenate(cols, axis=1))
    sgate = jnp.concatenate(rows, axis=0)

    mix = h1_ref[...] + jnp.dot(sgate, wout_ref[ATT_WIDTH:, :], preferred_element_type=F32)
    h1 = _layernorm(DN_ALPHA * x_ref[...] + mix, g1_ref[...], b1_ref[...])
    h1_ref[...] = h1
    hp_s[slot] = h1.astype(BF16).reshape(hp_s.shape[1:])

    nt = (((1,), (1,)), ((), ()))
    h1_hi = h1.astype(BF16)
    h1_lo = (h1 - h1_hi.astype(F32)).astype(BF16)
    both = lax.dot_general(wr_ref[...], h1_hi, nt, preferred_element_type=F32)
    lt = both[:NR_PAD] + both[NR_PAD:] + lax.dot_general(
        wr_ref[:NR_PAD, :], h1_lo, nt, preferred_element_type=F32)
    lt = lt + br_ref[...]
    g_logits = lt[0:N_GROUPS]
    gmax, g_idx = _first_argmax(g_logits, N_GROUPS)
    g_gate = 1.0 / jnp.sum(jnp.exp(g_logits - gmax), axis=0, keepdims=True)
    e_logits = jnp.zeros((EXPERTS_PER_GROUP, TM_MIX), F32)
    for g in range(N_GROUPS):
        lo = NR_EXPERT0 + g * EXPERTS_PER_GROUP
        e_logits = jnp.where(g_idx == g, lt[lo:lo + EXPERTS_PER_GROUP], e_logits)
    v1, i1 = _first_argmax(e_logits, EXPERTS_PER_GROUP)
    erow = lax.broadcasted_iota(I32, e_logits.shape, 0)
    v2, i2 = _first_argmax(jnp.where(erow == i1, -jnp.inf, e_logits), EXPERTS_PER_GROUP)
    t = jnp.exp(v2 - v1)
    w1 = g_gate / (1.0 + t)
    w2 = g_gate * t / (1.0 + t)
    e1 = g_idx * EXPERTS_PER_GROUP + i1
    e2 = g_idx * EXPERTS_PER_GROUP + i2

    xrow = lax.broadcasted_iota(I32, (N_EXPERTS, TM_MIX), 0)
    oh1 = xrow == e1
    oh2 = xrow == e2
    oh = jnp.where(jnp.logical_or(oh1, oh2), 1.0, 0.0)
    tr = lax.broadcasted_iota(I32, (TM_MIX, TM_MIX), 0)
    tc = lax.broadcasted_iota(I32, (TM_MIX, TM_MIX), 1)
    before = jnp.where(tr < tc, 1.0, 0.0).astype(BF16)
    cum = jnp.dot(oh.astype(BF16), before, preferred_element_type=F32)
    base = cum + run_ref[:, 0:1]
    rank1 = jnp.sum(jnp.where(oh1, base, 0.0), axis=0, keepdims=True).astype(I32)
    rank2 = jnp.sum(jnp.where(oh2, base, 0.0), axis=0, keepdims=True).astype(I32)
    run_ref[...] = run_ref[...] + jnp.sum(oh, axis=1, keepdims=True)
    cnt_ref[...] = run_ref[...].astype(I32)

    zi = jnp.zeros((ROUTE_ROWS - 4, TM_MIX), I32)
    ri_ref[...] = jnp.concatenate([e1, e2, rank1, rank2, zi], axis=0)
    zw = jnp.zeros((ROUTE_ROWS - 2, TM_MIX), F32)
    rw_ref[...] = jnp.concatenate([w1, w2, zw], axis=0)

    posv_ref[...] = jnp.concatenate([e1 * XS_CAP + rank1, e2 * XS_CAP + rank2, zi, zi[:2]], axis=0)
    pos_copy(slot).start()

    @pl.when(i == n - 1)
    def _():
        pos_copy(slot).wait()

        def last(j, c):
            row_copy(slot, 0, j).start(priority=0)
            row_copy(slot, 1, j).start(priority=1)
            return c
        lax.fori_loop(0, TM_MIX, last, 0, unroll=8)
        drain(prev)
        drain(slot)


def _mix(a2, proj2, x2, wout_b, sgw_b, sgb, sgg, sgbeta, g1, b1, wr, br):
    t, d = x2.shape
    assert XS_CAP >= t + TMX
    u_blk = (proj2.shape[1] - 2 * SG_WIDTH) // SG_WIDTH
    const = lambda *shape: pl.BlockSpec(shape, lambda i: (0,) * len(shape))
    return pl.pallas_call(
        _mix_kernel,
        grid=(t // TM_MIX,),
        in_specs=[
            pl.BlockSpec((TM_MIX, ATT_WIDTH), lambda i: (i, 0)),
            pl.BlockSpec((TM_MIX, SG_WIDTH), lambda i: (i, u_blk)),
            pl.BlockSpec((TM_MIX, SG_WIDTH), lambda i: (i, u_blk + 1)),
            pl.BlockSpec((TM_MIX, d), lambda i: (i, 0)),
            pl.BlockSpec((d, d), lambda i: (0, 0), pipeline_mode=pl.Buffered(1)),
            const(SG_GROUPS, SG_CHUNK, SG_CHUNK),
            const(SG_GROUPS, SG_CHUNK, 1),
            const(SG_GROUPS, SG_DIM),
            const(SG_GROUPS, SG_DIM),
            const(1, d),
            const(1, d),
            const(2 * NR_PAD, d),
            const(NR_PAD, 1),
        ],
        out_specs=[
            pl.BlockSpec((TM_MIX, d), lambda i: (i, 0)),
            pl.BlockSpec(memory_space=pl.ANY),
            pl.BlockSpec((ROUTE_ROWS, TM_MIX), lambda i: (0, i)),
            pl.BlockSpec((ROUTE_ROWS, TM_MIX), lambda i: (0, i)),
            const(N_EXPERTS, LANES),
        ],
        out_shape=[
            jax.ShapeDtypeStruct((t, d), F32),
            jax.ShapeDtypeStruct((N_EXPERTS * XS_CAP + 2 * TM_MIX, d // LANES, LANES), BF16),
            jax.ShapeDtypeStruct((ROUTE_ROWS, t), I32),
            jax.ShapeDtypeStruct((ROUTE_ROWS, t), F32),
            jax.ShapeDtypeStruct((N_EXPERTS, LANES), I32),
        ],
        scratch_shapes=[pltpu.VMEM((N_EXPERTS, LANES), F32),
                        pltpu.VMEM((2, TM_MIX, d // LANES, LANES), BF16),
                        pltpu.VMEM((ROUTE_ROWS, TM_MIX), I32),
                        pltpu.SMEM((2, ROUTE_ROWS, TM_MIX), I32),
                        pltpu.SemaphoreType.DMA((2,)),
                        pltpu.SemaphoreType.DMA((2,))],
        compiler_params=_cparams(("arbitrary",)),
        name="mix",
    )(a2, proj2, proj2, x2, wout_b, sgw_b, sgb, sgg, sgbeta, g1, b1, wr, br)


def _experts_kernel(ts_ref, nt_ref, nu_ref, trow_ref, nval_ref, x_hbm, wg_ref, wu_ref, wd_ref,
                    y_hbm, xbuf, ybuf, xsem, ysem):
    e = pl.program_id(0)
    d = wg_ref.shape[1]
    n_used = nu_ref[0]

    def x_copy(g):
        rows = pl.ds(pl.multiple_of(trow_ref[g], TMX), TMX)
        slot = lax.rem(g, N_XSLOTS)
        return pltpu.make_async_copy(x_hbm.at[rows], xbuf.at[slot], xsem.at[slot])

    def y_copy(g):
        rows = pl.ds(pl.multiple_of(g * TMX, TMX), TMX)
        slot = lax.rem(g, N_XSLOTS)
        return pltpu.make_async_copy(ybuf.at[slot], y_hbm.at[rows], ysem.at[slot])

    def start_x(g):
        @pl.when(g < n_used)
        def _():
            x_copy(g).start(priority=1)

    def free_y(g):
        @pl.when(g >= N_XSLOTS)
        def _():
            y_copy(g - N_XSLOTS).wait()

    def load(g):
        xt = xbuf[lax.rem(g, N_XSLOTS)].reshape(TMX, d)
        row = lax.broadcasted_iota(I32, xt.shape, 0)
        return jnp.where(row < nval_ref[g], xt, jnp.zeros_like(xt))

    def mlp(xb):
        gate = jnp.dot(xb, wg_ref[0], preferred_element_type=F32)
        up = jnp.dot(xb, wu_ref[0], preferred_element_type=F32)
        hmid = (gate * jax.nn.sigmoid(gate) * up).astype(BF16)
        return jnp.dot(hmid, wd_ref[0], preferred_element_type=F32)

    @pl.when(e == 0)
    def _():
        for g in range(X_AHEAD):
            start_x(g)

    def group(k, first):
        def body(t, carry):
            g = first(t)
            for j in range(k):
                x_copy(g + j).wait()
            for j in range(k):
                start_x(g + X_AHEAD + j)
                free_y(g + j)
            y = mlp(jnp.concatenate([load(g + j) for j in range(k)], axis=0))
            for j in range(k):
                ybuf[lax.rem(g + j, N_XSLOTS)] = y[j * TMX:(j + 1) * TMX].astype(BF16).reshape(
                    ybuf.shape[1:])
                y_copy(g + j).start(priority=1)
            return carry
        return body

    nt = nt_ref[e]
    n3 = jnp.where(jnp.logical_and(nt >= 3, nt % 2 == 1), 1, 0)
    n2 = (nt - 3 * n3) // 2
    n1 = nt - 3 * n3 - 2 * n2
    base = ts_ref[e]
    lax.fori_loop(0, n3, group(3, lambda t: base), 0)
    lax.fori_loop(0, n2, group(2, lambda t: base + 3 * n3 + 2 * t), 0)
    lax.fori_loop(0, n1, group(1, lambda t: base + 3 * n3 + 2 * n2), 0)

    @pl.when(e == pl.num_programs(0) - 1)
    def _():
        for back in range(N_XSLOTS, 0, -1):
            @pl.when(n_used >= back)
            def _():
                y_copy(n_used - back).wait()


def _experts(tile_start, n_tiles_e, n_used, tile_row, tile_valid, xs, w_gate, w_up, w_down):
    d = w_gate.shape[1]
    n_rows = tile_row.shape[0] * TMX

    def w_map(e, ts, nt, nu, trow, nval):
        return (e, 0, 0)

    grid_spec = pltpu.PrefetchScalarGridSpec(
        num_scalar_prefetch=5,
        grid=(N_EXPERTS,),
        in_specs=[
            pl.BlockSpec(memory_space=pl.ANY),
            pl.BlockSpec((1, d, D_EXPERT), w_map),
            pl.BlockSpec((1, d, D_EXPERT), w_map),
            pl.BlockSpec((1, D_EXPERT, d), w_map),
        ],
        out_specs=pl.BlockSpec(memory_space=pl.ANY),
        scratch_shapes=[pltpu.VMEM((N_XSLOTS, TMX) + xs.shape[1:], BF16),
                        pltpu.VMEM((N_XSLOTS, TMX) + xs.shape[1:], BF16),
                        pltpu.SemaphoreType.DMA((N_XSLOTS,)),
                        pltpu.SemaphoreType.DMA((N_XSLOTS,))],
    )
    return pl.pallas_call(
        _experts_kernel,
        grid_spec=grid_spec,
        out_shape=jax.ShapeDtypeStruct((n_rows,) + xs.shape[1:], BF16),
        compiler_params=_cparams(("arbitrary",)),
        name="experts",
    )(tile_start, n_tiles_e, n_used, tile_row, tile_valid, xs, w_gate, w_up, w_down)


def _combine_kernel(pos_ref, posn_ref, y_hbm, h1_ref, wt_ref, g2_ref, b2_ref, o_ref, ybuf, sem):
    i = pl.program_id(0)
    n = pl.num_programs(0)
    slot = i % 2

    def row_copy(src, sl, which, j):
        return pltpu.make_async_copy(y_hbm.at[pl.ds(src, 1)],
                                     ybuf.at[sl, which, pl.ds(j, 1)], sem.at[sl])

    def issue(p_ref, sl):
        def body(j, c):
            row_copy(p_ref[0, j], sl, 0, j).start(priority=0)
            row_copy(p_ref[1, j], sl, 1, j).start(priority=1)
            return c
        lax.fori_loop(0, TM_OUT, body, 0, unroll=8)

    def drain(sl):
        for which in range(2):
            pltpu.make_async_copy(y_hbm.at[pl.ds(0, TM_OUT)], ybuf.at[sl, which],
                                  sem.at[sl]).wait()

    @pl.when(i == 0)
    def _():
        issue(pos_ref, 0)

    drain(slot)

    for j in range(TM_OUT):
        row_copy(posn_ref[0, j], 1 - slot, 0, j).start(priority=0)
        row_copy(posn_ref[1, j], 1 - slot, 1, j).start(priority=1)

    y1 = ybuf[slot, 0].reshape(h1_ref.shape).astype(F32)
    y2 = ybuf[slot, 1].reshape(h1_ref.shape).astype(F32)
    ffn = wt_ref[:, 0:1] * y1 + wt_ref[:, 1:2] * y2
    o_ref[...] = _layernorm(DN_ALPHA * h1_ref[...] + ffn, g2_ref[...], b2_ref[...])

    @pl.when(i == n - 1)
    def _():
        drain(1 - slot)


def _combine(pos3, ys, h1, wtok, g2, b2):
    t, d = h1.shape
    n = t // TM_OUT
    return pl.pallas_call(
        _combine_kernel,
        grid=(n,),
        in_specs=[
            pl.BlockSpec((None, 2, TM_OUT), lambda i: (i, 0, 0), memory_space=pltpu.SMEM),
            pl.BlockSpec((None, 2, TM_OUT), lambda i: (jnp.minimum(i + 1, n - 1), 0, 0),
                         memory_space=pltpu.SMEM),
            pl.BlockSpec(memory_space=pl.ANY),
            pl.BlockSpec((TM_OUT, d), lambda i: (i, 0)),
            pl.BlockSpec((TM_OUT, ROUTE_ROWS), lambda i: (i, 0)),
            pl.BlockSpec((1, d), lambda i: (0, 0)),
            pl.BlockSpec((1, d), lambda i: (0, 0)),
        ],
        out_specs=pl.BlockSpec((TM_OUT, d), lambda i: (i, 0)),
        out_shape=jax.ShapeDtypeStruct((t, d), F32),
        scratch_shapes=[pltpu.VMEM((2, 2, TM_OUT) + ys.shape[1:], BF16),
                        pltpu.SemaphoreType.DMA((2,))],
        compiler_params=_cparams(("arbitrary",)),
        name="combine",
    )(pos3, pos3, ys, h1, wtok, g2, b2)


def _lambda_init(layer):
    return 0.8 - 0.6 * math.exp(-0.3 * layer)


def kernel(x, w_in, w_out, ln1_g, ln1_b, ln2_g, ln2_b, rel_bias, lam_q1, lam_k1, lam_q2, lam_k2,
           subln_g, sg_ln_g, sg_ln_b, sg_w, sg_b, w_router_group, b_router_group,
           w_router_expert, b_router_expert, w_exp_gate, w_exp_up, w_exp_down):
    b, s, d = x.shape
    t = b * s
    assert w_in.shape[0] == DEPTH == 1
    l = 0
    lam_init = _lambda_init(l)
    x2 = x.reshape(t, d)

    proj = _inproj(x2, w_in[l])

    lam = (jnp.exp(jnp.sum(lam_q1[l].astype(F32) * lam_k1[l].astype(F32)))
           - jnp.exp(jnp.sum(lam_q2[l].astype(F32) * lam_k2[l].astype(F32))) + lam_init)
    side = [w_exp_gate[l].reshape(-1, D_EXPERT), w_exp_up[l].reshape(-1, D_EXPERT),
            w_exp_down[l].reshape(-1, d), w_out[l]]
    a, wg_b, wu_b, wd_b, wout_b = _attention(
        proj.reshape(b, s, -1), _bias_sequences(rel_bias, s), lam.reshape(1),
        subln_g[l].reshape(1, -1), lam_init, side)

    wr = jnp.concatenate([w_router_group[l].T,
                          jnp.zeros((NR_EXPERT0 - N_GROUPS, d), F32),
                          jnp.transpose(w_router_expert[l], (0, 2, 1)).reshape(N_EXPERTS, d)],
                         axis=0)
    br = jnp.concatenate([b_router_group[l], jnp.zeros((NR_EXPERT0 - N_GROUPS,), F32),
                          b_router_expert[l].reshape(-1)]).reshape(NR_PAD, 1)
    wr_hi = wr.astype(BF16)
    wr = jnp.concatenate([wr_hi, (wr - wr_hi.astype(F32)).astype(BF16)], axis=0)
    h1, xs, ri, rw, cnt = _mix(
        a.reshape(t, -1), proj, x2, wout_b, sg_w[l].astype(BF16),
        sg_b[l].reshape(SG_GROUPS, SG_CHUNK, 1), sg_ln_g[l], sg_ln_b[l],
        ln1_g[l].reshape(1, d), ln1_b[l].reshape(1, d), wr, br)

    counts = cnt[:, 0]
    tiles_e = (counts + TMX - 1) // TMX
    tile_end = jnp.cumsum(tiles_e)
    tile_start = tile_end - tiles_e
    n_used = tile_end[-1:]
    max_tiles = (2 * t) // TMX + N_EXPERTS
    row_start = (tile_start * TMX).astype(I32)
    eids = jnp.arange(N_EXPERTS, dtype=I32)[:, None, None]
    pos = jnp.sum(jnp.where(ri[0:2][None] == eids, row_start[:, None, None], 0), axis=0) + ri[2:4]
    pos3 = pos.reshape(2, t // TM_OUT, TM_OUT).transpose(1, 0, 2)
    tile_ids = jnp.arange(max_tiles, dtype=I32)
    tile_e = jnp.minimum(jnp.sum(tile_end[None, :] <= tile_ids[:, None], axis=1), N_EXPERTS - 1)
    local = tile_ids - tile_start[tile_e]
    in_use = tile_ids < n_used[0]
    tile_row = jnp.where(in_use, tile_e * XS_CAP + local * TMX, 0).astype(I32)
    tile_valid = jnp.where(in_use, jnp.clip(counts[tile_e] - local * TMX, 0, TMX), 0).astype(I32)

    ys = _experts(tile_start.astype(I32), tiles_e.astype(I32), n_used.astype(I32),
                  tile_row, tile_valid, xs,
                  wg_b.reshape(N_EXPERTS, d, D_EXPERT), wu_b.reshape(N_EXPERTS, d, D_EXPERT),
                  wd_b.reshape(N_EXPERTS, D_EXPERT, d))
    out = _combine(pos3, ys, h1, rw.T, ln2_g[l].reshape(1, d), ln2_b[l].reshape(1, d))
    return out.reshape(b, s, d)
```

```python
import functools
import math

import jax
import jax.numpy as jnp
from jax import lax
from jax.experimental import pallas as pl
from jax.experimental.pallas import tpu as pltpu

F32 = jnp.float32
BF16 = jnp.bfloat16
I32 = jnp.int32

ATT_HEADS = 8
ATT_QK_DIM = 64
ATT_V_DIM = 128
ATT_WIDTH = ATT_HEADS * ATT_V_DIM
SG_GROUPS = 8
SG_DIM = 128
SG_WIDTH = SG_GROUPS * SG_DIM
SG_CHUNK = 128
REL_BUCKETS = 32
REL_MAX_DIST = 128
N_GROUPS = 4
EXPERTS_PER_GROUP = 8
N_EXPERTS = N_GROUPS * EXPERTS_PER_GROUP
D_EXPERT = 256
DEPTH = 1
DN_ALPHA = (2.0 * DEPTH) ** 0.25
LN_EPS = 1e-5
LOG2E = math.log2(math.e)

LANES = 128
SUBLANES = 8
VMEM_LIMIT_BYTES = 56 * 1024 * 1024

TM_IN = 1024
TN_IN = 1024
W_AHEAD = 2
W_SLOTS = W_AHEAD + 1
QB = 256
KC = 256
N_BIAS_TILES = 5
TM_MIX = 512
TMX = 256
TM_OUT = 256
X_AHEAD = 3
N_XSLOTS = 2 * X_AHEAD
EW_AHEAD = 2
EW_SLOTS = EW_AHEAD + 1
ROUTE_ROWS = 8
XS_CAP = 8192 + TMX
NR_EXPERT0 = 8
NR_PAD = NR_EXPERT0 + 32


def _cparams(sem, flags=None):
    return pltpu.CompilerParams(dimension_semantics=sem, vmem_limit_bytes=VMEM_LIMIT_BYTES,
                                flags=flags)


def _inproj_kernel(x_ref, w_hbm, o_ref, xb_ref, wbuf, wsem, *, n_q, n_lin, q_scale):
    j = pl.program_id(1)
    nj = pl.num_programs(1)
    step = pl.program_id(0) * nj + j
    n_steps = pl.num_programs(0) * nj

    def w_copy(s):
        cols = pl.ds(pl.multiple_of(lax.rem(s, nj) * TN_IN, TN_IN), TN_IN)
        slot = lax.rem(s, W_SLOTS)
        return pltpu.make_async_copy(w_hbm.at[:, cols], wbuf.at[slot], wsem.at[slot])

    @pl.when(step == 0)
    def _():
        for s in range(W_AHEAD):
            w_copy(s).start()

    @pl.when(step + W_AHEAD < n_steps)
    def _():
        w_copy(step + W_AHEAD).start()

    @pl.when(j == 0)
    def _():
        xb_ref[...] = x_ref[...].astype(BF16)

    w_copy(step).wait()
    w = wbuf[lax.rem(step, W_SLOTS)].astype(BF16)
    acc = jnp.dot(xb_ref[...], w, preferred_element_type=F32)

    @pl.when(j < n_q)
    def _():
        o_ref[...] = (acc * q_scale).astype(BF16)

    @pl.when(jnp.logical_and(j >= n_q, j < n_lin))
    def _():
        o_ref[...] = acc.astype(BF16)

    @pl.when(j >= n_lin)
    def _():
        o_ref[...] = jax.nn.gelu(acc).astype(BF16)


def _inproj(x2, w_in):
    t, d = x2.shape
    n = w_in.shape[1]
    assert (t // TM_IN) * (n // TN_IN) >= W_AHEAD
    n_q = (ATT_HEADS * 2 * ATT_QK_DIM) // TN_IN
    n_lin = (2 * ATT_HEADS * 2 * ATT_QK_DIM + ATT_WIDTH) // TN_IN
    kern = functools.partial(_inproj_kernel, n_q=n_q, n_lin=n_lin,
                             q_scale=(ATT_QK_DIM ** -0.5) * LOG2E)
    return pl.pallas_call(
        kern,
        grid=(t // TM_IN, n // TN_IN),
        in_specs=[pl.BlockSpec((TM_IN, d), lambda i, j: (i, 0)),
                  pl.BlockSpec(memory_space=pl.ANY)],
        out_specs=pl.BlockSpec((TM_IN, TN_IN), lambda i, j: (i, j)),
        out_shape=jax.ShapeDtypeStruct((t, n), BF16),
        scratch_shapes=[pltpu.VMEM((TM_IN, d), BF16),
                        pltpu.VMEM((W_SLOTS, d, TN_IN), w_in.dtype),
                        pltpu.SemaphoreType.DMA((W_SLOTS,))],
        compiler_params=_cparams(("arbitrary", "arbitrary")),
        name="inproj",
    )(x2, w_in)


def _attn_kernel(lam_ref, q_ref, k_ref, v_ref, bseq_ref, g_ref, *rest, n_kc, out_scale):
    n_side = (len(rest) - 3) // 2
    side_in, o_ref, side_out = rest[:n_side], rest[n_side], rest[n_side + 1:2 * n_side + 1]
    vaug_ref, bias_ref = rest[2 * n_side + 1:]
    qi = pl.program_id(2)

    @pl.when(qi == 0)
    def _():
        vaug_ref[:, :ATT_V_DIM] = v_ref[0]
        vaug_ref[:, ATT_V_DIM:] = jnp.ones((vaug_ref.shape[0], ATT_V_DIM), BF16)
        for t in range(N_BIAS_TILES):
            seq = jnp.broadcast_to(bseq_ref[0, t:t + 1, :], (QB, 2 * KC))
            bias_ref[t] = pltpu.roll(seq, 0, 1, stride=1, stride_axis=0)[:, :KC]

    q = q_ref[0]
    lane = lax.broadcasted_iota(I32, q.shape, 1)
    zero = jnp.zeros_like(q)
    qq = jnp.concatenate([jnp.where(lane < ATT_QK_DIM, q, zero),
                          jnp.where(lane >= ATT_QK_DIM, q, zero)], axis=0)

    for src, dst in zip(side_in, side_out):
        dst[...] = src[...].astype(dst.dtype)

    m = jnp.full((2 * QB, 1), -jnp.inf, F32)
    acc = jnp.zeros((2 * QB, 2 * ATT_V_DIM), F32)
    for kc in range(n_kc):
        kblk = k_ref[0, kc * KC:(kc + 1) * KC, :]
        s = lax.dot_general(qq, kblk, (((1,), (1,)), ((), ())),
                            preferred_element_type=F32)
        b = bias_ref[jnp.clip(kc - qi, -2, 2) + 2]
        s = s + jnp.concatenate([b, b], axis=0)
        m_new = jnp.maximum(m, jnp.max(s, axis=1, keepdims=True))
        p = jnp.exp2(s - m_new).astype(BF16)
        acc = acc * jnp.exp2(m - m_new) + jnp.dot(
            p, vaug_ref[kc * KC:(kc + 1) * KC, :], preferred_element_type=F32)
        m = m_new
    lam = lam_ref[0]
    o0 = acc[:QB, :ATT_V_DIM] / acc[:QB, ATT_V_DIM:ATT_V_DIM + 1]
    o1 = acc[QB:, :ATT_V_DIM] / acc[QB:, ATT_V_DIM:ATT_V_DIM + 1]
    of = o0 - lam * o1
    of = of * lax.rsqrt(jnp.mean(of * of, axis=-1, keepdims=True) + LN_EPS)
    o_ref[0] = (of * (g_ref[...] * out_scale)).astype(o_ref.dtype)


def _attention(proj3, bias_seq, lam, subln_g, lam_init, side):
    b, s, _ = proj3.shape
    n_kc = s // KC
    n_q = s // QB
    n_steps = b * ATT_HEADS * n_q
    h_off_k = ATT_HEADS
    h_off_v = 2 * ATT_HEADS
    kern = functools.partial(_attn_kernel, n_kc=n_kc, out_scale=1.0 - lam_init)

    side_specs = []
    for arr in side:
        rows, cols = arr.shape
        blk = max(rows // n_steps, 2 * SUBLANES)
        hold = blk * n_steps // rows
        assert rows % blk == 0 and hold * rows == blk * n_steps
        side_specs.append(pl.BlockSpec(
            (blk, cols),
            lambda bi, h, qi, hold=hold: (((bi * ATT_HEADS + h) * n_q + qi) // hold, 0)))
    return pl.pallas_call(
        kern,
        grid=(b, ATT_HEADS, n_q),
        in_specs=[
            pl.BlockSpec(memory_space=pltpu.SMEM),
            pl.BlockSpec((1, QB, LANES), lambda bi, h, qi: (bi, qi, h)),
            pl.BlockSpec((1, s, LANES), lambda bi, h, qi: (bi, 0, h_off_k + h)),
            pl.BlockSpec((1, s, LANES), lambda bi, h, qi: (bi, 0, h_off_v + h)),
            pl.BlockSpec((1, N_BIAS_TILES, 2 * KC), lambda bi, h, qi: (h, 0, 0)),
            pl.BlockSpec((1, ATT_V_DIM), lambda bi, h, qi: (0, 0)),
        ] + side_specs,
        out_specs=[pl.BlockSpec((1, QB, ATT_V_DIM), lambda bi, h, qi: (bi, qi, h))] + side_specs,
        out_shape=[jax.ShapeDtypeStruct((b, s, ATT_WIDTH), BF16)]
        + [jax.ShapeDtypeStruct(arr.shape, BF16) for arr in side],
        scratch_shapes=[pltpu.VMEM((s, 2 * ATT_V_DIM), BF16),
                        pltpu.VMEM((N_BIAS_TILES, QB, KC), F32)],
        compiler_params=_cparams(("arbitrary", "arbitrary", "arbitrary")),
        name="attn",
    )(lam, proj3, proj3, proj3, bias_seq, subln_g, *side)


def _rel_bucket(rel):
    half = REL_BUCKETS // 2
    max_exact = half // 2
    ret = jnp.where(rel > 0, half, 0)
    n = jnp.abs(rel)
    nf = jnp.maximum(n, 1).astype(F32)
    large = max_exact + (jnp.log(nf / max_exact) / math.log(REL_MAX_DIST / max_exact)
                         * (half - max_exact)).astype(I32)
    large = jnp.minimum(large, half - 1)
    return ret + jnp.where(n < max_exact, n, large)


def _bias_sequences(rel_bias, s):
    assert REL_MAX_DIST <= KC and QB == KC
    period = 2 * KC
    j = jnp.arange(period, dtype=I32)
    off = jnp.where(j < KC, j, j - period)
    d = jnp.arange(-2, 3, dtype=I32)[:, None]
    rel = jnp.clip(d * KC + off[None, :], -(s - 1), s - 1)
    return jnp.transpose(rel_bias[_rel_bucket(rel)], (2, 0, 1)).astype(F32) * LOG2E


def _layernorm(v, g, b):
    mu = jnp.mean(v, axis=-1, keepdims=True)
    var = jnp.mean(jnp.square(v - mu), axis=-1, keepdims=True)
    return (v - mu) * lax.rsqrt(var + LN_EPS) * g + b


def _first_argmax(vals, nrows):
    mx = jnp.max(vals, axis=0, keepdims=True)
    row = lax.broadcasted_iota(I32, vals.shape, 0)
    idx = jnp.min(jnp.where(vals == mx, row, nrows), axis=0, keepdims=True)
    return mx, idx


def _mix_kernel(a_ref, u_ref, vg_ref, x_ref, wout_ref, sgw_ref, sgb_ref, sgg_ref, sgbeta_ref,
                g1_ref, b1_ref, wr_ref, br_ref,
                h1_ref, xs_hbm, ri_ref, rw_ref, cnt_ref,
                run_ref, hp_s, posv_ref, psm, ssem, rsem):
    i = pl.program_id(0)
    n = pl.num_programs(0)
    slot = i % 2
    prev = 1 - slot

    def pos_copy(sl):
        return pltpu.make_async_copy(posv_ref, psm.at[sl], ssem.at[sl])

    def row_copy(sl, which, j):
        return pltpu.make_async_copy(hp_s.at[sl, pl.ds(j, 1)],
                                     xs_hbm.at[pl.ds(psm[sl, which, j], 1)], rsem.at[sl])

    def drain(sl):
        whole = pltpu.make_async_copy(hp_s.at[sl], xs_hbm.at[pl.ds(0, TM_MIX)], rsem.at[sl])
        whole.wait()
        whole.wait()

    @pl.when(i == 0)
    def _():
        run_ref[...] = jnp.zeros_like(run_ref)
        hp_s[1] = jnp.zeros(hp_s.shape[1:], hp_s.dtype)
        spare = N_EXPERTS * XS_CAP + lax.broadcasted_iota(I32, posv_ref.shape, 1)
        spare = spare + TM_MIX * lax.broadcasted_iota(I32, posv_ref.shape, 0)
        posv_ref[...] = jnp.where(spare < xs_hbm.shape[0], spare, 0)
        pos_copy(1).start()

    @pl.when(i > 0)
    def _():
        drain(slot)

    pos_copy(prev).wait()

    for j in range(TM_MIX):
        row_copy(prev, 0, j).start(priority=0)
        row_copy(prev, 1, j).start(priority=1)

    h1_ref[...] = jnp.dot(a_ref[...], wout_ref[:ATT_WIDTH, :], preferred_element_type=F32)

    n_chunks = TM_MIX // SG_CHUNK
    rows = []
    for c in range(n_chunks):
        cols = []
        for g in range(SG_GROUPS):
            rs = slice(c * SG_CHUNK, (c + 1) * SG_CHUNK)
            cs = slice(g * SG_DIM, (g + 1) * SG_DIM)
            vt = vg_ref[rs, cs].astype(F32)
            vn = _layernorm(vt, sgg_ref[g:g + 1, :], sgbeta_ref[g:g + 1, :])
            mixed = jnp.dot(sgw_ref[g], vn.astype(BF16), preferred_element_type=F32) + sgb_ref[g]
            cols.append((u_ref[rs, cs].astype(F32) * mixed).astype(BF16))
        rows.append(jnp.concatenate(cols, axis=1))
    sgate = jnp.concatenate(rows, axis=0)

    mix = h1_ref[...] + jnp.dot(sgate, wout_ref[ATT_WIDTH:, :], preferred_element_type=F32)
    h1 = _layernorm(DN_ALPHA * x_ref[...] + mix, g1_ref[...], b1_ref[...])
    h1_ref[...] = h1
    hp_s[slot] = h1.astype(BF16).reshape(hp_s.shape[1:])

    nt = (((1,), (1,)), ((), ()))
    h1_hi = h1.astype(BF16)
    h1_lo = (h1 - h1_hi.astype(F32)).astype(BF16)
    both = lax.dot_general(wr_ref[...], h1_hi, nt, preferred_element_type=F32)
    lt = both[:NR_PAD] + both[NR_PAD:] + lax.dot_general(
        wr_ref[:NR_PAD, :], h1_lo, nt, preferred_element_type=F32)
    lt = lt + br_ref[...]
    g_logits = lt[0:N_GROUPS]
    gmax, g_idx = _first_argmax(g_logits, N_GROUPS)
    g_gate = 1.0 / jnp.sum(jnp.exp(g_logits - gmax), axis=0, keepdims=True)
    e_logits = jnp.zeros((EXPERTS_PER_GROUP, TM_MIX), F32)
    for g in range(N_GROUPS):
        lo = NR_EXPERT0 + g * EXPERTS_PER_GROUP
        e_logits = jnp.where(g_idx == g, lt[lo:lo + EXPERTS_PER_GROUP], e_logits)
    v1, i1 = _first_argmax(e_logits, EXPERTS_PER_GROUP)
    erow = lax.broadcasted_iota(I32, e_logits.shape, 0)
    v2, i2 = _first_argmax(jnp.where(erow == i1, -jnp.inf, e_logits), EXPERTS_PER_GROUP)
    t = jnp.exp(v2 - v1)
    w1 = g_gate / (1.0 + t)
    w2 = g_gate * t / (1.0 + t)
    e1 = g_idx * EXPERTS_PER_GROUP + i1
    e2 = g_idx * EXPERTS_PER_GROUP + i2

    xrow = lax.broadcasted_iota(I32, (N_EXPERTS, TM_MIX), 0)
    oh1 = xrow == e1
    oh2 = xrow == e2
    oh = jnp.where(jnp.logical_or(oh1, oh2), 1.0, 0.0)
    tr = lax.broadcasted_iota(I32, (TM_MIX, TM_MIX), 0)
    tc = lax.broadcasted_iota(I32, (TM_MIX, TM_MIX), 1)
    before = jnp.where(tr < tc, 1.0, 0.0).astype(BF16)
    cum = jnp.dot(oh.astype(BF16), before, preferred_element_type=F32)
    base = cum + run_ref[:, 0:1]
    rank1 = jnp.sum(jnp.where(oh1, base, 0.0), axis=0, keepdims=True).astype(I32)
    rank2 = jnp.sum(jnp.where(oh2, base, 0.0), axis=0, keepdims=True).astype(I32)
    run_ref[...] = run_ref[...] + jnp.sum(oh, axis=1, keepdims=True)
    cnt_ref[...] = run_ref[...].astype(I32)

    zi = jnp.zeros((ROUTE_ROWS - 4, TM_MIX), I32)
    ri_ref[...] = jnp.concatenate([e1, e2, rank1, rank2, zi], axis=0)
    zw = jnp.zeros((ROUTE_ROWS - 2, TM_MIX), F32)
    rw_ref[...] = jnp.concatenate([w1, w2, zw], axis=0)

    posv_ref[...] = jnp.concatenate([e1 * XS_CAP + rank1, e2 * XS_CAP + rank2, zi, zi[:2]], axis=0)
    pos_copy(slot).start()

    @pl.when(i == n - 1)
    def _():
        pos_copy(slot).wait()

        def last(j, c):
            row_copy(slot, 0, j).start(priority=0)
            row_copy(slot, 1, j).start(priority=1)
            return c
        lax.fori_loop(0, TM_MIX, last, 0, unroll=8)
        drain(prev)
        drain(slot)


def _mix(a2, proj2, x2, wout_b, sgw_b, sgb, sgg, sgbeta, g1, b1, wr, br):
    t, d = x2.shape
    assert XS_CAP >= t + TMX
    u_blk = (proj2.shape[1] - 2 * SG_WIDTH) // SG_WIDTH
    const = lambda *shape: pl.BlockSpec(shape, lambda i: (0,) * len(shape))
    return pl.pallas_call(
        _mix_kernel,
        grid=(t // TM_MIX,),
        in_specs=[
            pl.BlockSpec((TM_MIX, ATT_WIDTH), lambda i: (i, 0)),
            pl.BlockSpec((TM_MIX, SG_WIDTH), lambda i: (i, u_blk)),
            pl.BlockSpec((TM_MIX, SG_WIDTH), lambda i: (i, u_blk + 1)),
            pl.BlockSpec((TM_MIX, d), lambda i: (i, 0)),
            pl.BlockSpec((d, d), lambda i: (0, 0), pipeline_mode=pl.Buffered(1)),
            const(SG_GROUPS, SG_CHUNK, SG_CHUNK),
            const(SG_GROUPS, SG_CHUNK, 1),
            const(SG_GROUPS, SG_DIM),
            const(SG_GROUPS, SG_DIM),
            const(1, d),
            const(1, d),
            const(2 * NR_PAD, d),
            const(NR_PAD, 1),
        ],
        out_specs=[
            pl.BlockSpec((TM_MIX, d), lambda i: (i, 0)),
            pl.BlockSpec(memory_space=pl.ANY),
            pl.BlockSpec((ROUTE_ROWS, TM_MIX), lambda i: (0, i)),
            pl.BlockSpec((ROUTE_ROWS, TM_MIX), lambda i: (0, i)),
            const(N_EXPERTS, LANES),
        ],
        out_shape=[
            jax.ShapeDtypeStruct((t, d), F32),
            jax.ShapeDtypeStruct((N_EXPERTS * XS_CAP + 2 * TM_MIX, d // LANES, LANES), BF16),
            jax.ShapeDtypeStruct((ROUTE_ROWS, t), I32),
            jax.ShapeDtypeStruct((ROUTE_ROWS, t), F32),
            jax.ShapeDtypeStruct((N_EXPERTS, LANES), I32),
        ],
        scratch_shapes=[pltpu.VMEM((N_EXPERTS, LANES), F32),
                        pltpu.VMEM((2, TM_MIX, d // LANES, LANES), BF16),
                        pltpu.VMEM((ROUTE_ROWS, TM_MIX), I32),
                        pltpu.SMEM((2, ROUTE_ROWS, TM_MIX), I32),
                        pltpu.SemaphoreType.DMA((2,)),
                        pltpu.SemaphoreType.DMA((2,))],
        compiler_params=_cparams(("arbitrary",)),
        name="mix",
    )(a2, proj2, proj2, x2, wout_b, sgw_b, sgb, sgg, sgbeta, g1, b1, wr, br)


def _experts_kernel(ts_ref, nt_ref, nu_ref, trow_ref, nval_ref, x_hbm, wg_hbm, wu_hbm, wd_hbm,
                    y_hbm, xbuf, ybuf, wgbuf, wubuf, wdbuf, xsem, ysem, wsem):
    e = pl.program_id(0)
    n_e = pl.num_programs(0)
    d = wg_hbm.shape[1]
    n_used = nu_ref[0]
    wslot = lax.rem(e, EW_SLOTS)

    def w_copies(ex):
        slot = lax.rem(ex, EW_SLOTS)
        return [pltpu.make_async_copy(src.at[ex], buf.at[slot], wsem.at[slot])
                for src, buf in ((wg_hbm, wgbuf), (wu_hbm, wubuf), (wd_hbm, wdbuf))]

    @pl.when(e == 0)
    def _():
        for ex in range(EW_AHEAD):
            for c in w_copies(ex):
                c.start()

    @pl.when(e + EW_AHEAD < n_e)
    def _():
        for c in w_copies(e + EW_AHEAD):
            c.start()

    for c in w_copies(e):
        c.wait()

    def x_copy(g):
        rows = pl.ds(pl.multiple_of(trow_ref[g], TMX), TMX)
        slot = lax.rem(g, N_XSLOTS)
        return pltpu.make_async_copy(x_hbm.at[rows], xbuf.at[slot], xsem.at[slot])

    def y_copy(g):
        rows = pl.ds(pl.multiple_of(g * TMX, TMX), TMX)
        slot = lax.rem(g, N_XSLOTS)
        return pltpu.make_async_copy(ybuf.at[slot], y_hbm.at[rows], ysem.at[slot])

    def start_x(g):
        @pl.when(g < n_used)
        def _():
            x_copy(g).start(priority=1)

    def free_y(g):
        @pl.when(g >= N_XSLOTS)
        def _():
            y_copy(g - N_XSLOTS).wait()

    def load(g):
        xt = xbuf[lax.rem(g, N_XSLOTS)].reshape(TMX, d)
        row = lax.broadcasted_iota(I32, xt.shape, 0)
        return jnp.where(row < nval_ref[g], xt, jnp.zeros_like(xt))

    def mlp(xb):
        gate = jnp.dot(xb, wgbuf[wslot], preferred_element_type=F32)
        up = jnp.dot(xb, wubuf[wslot], preferred_element_type=F32)
        hmid = (gate * jax.nn.sigmoid(gate) * up).astype(BF16)
        return jnp.dot(hmid, wdbuf[wslot], preferred_element_type=F32)

    @pl.when(e == 0)
    def _():
        for g in range(X_AHEAD):
            start_x(g)

    def group(k, first):
        def body(t, carry):
            g = first(t)
            for j in range(k):
                x_copy(g + j).wait()
            for j in range(k):
                start_x(g + X_AHEAD + j)
                free_y(g + j)
            y = mlp(jnp.concatenate([load(g + j) for j in range(k)], axis=0))
            for j in range(k):
                ybuf[lax.rem(g + j, N_XSLOTS)] = y[j * TMX:(j + 1) * TMX].astype(BF16).reshape(
                    ybuf.shape[1:])
                y_copy(g + j).start(priority=1)
            return carry
        return body

    nt = nt_ref[e]
    n3 = jnp.where(jnp.logical_and(nt >= 3, nt % 2 == 1), 1, 0)
    n2 = (nt - 3 * n3) // 2
    n1 = nt - 3 * n3 - 2 * n2
    base = ts_ref[e]
    lax.fori_loop(0, n3, group(3, lambda t: base), 0)
    lax.fori_loop(0, n2, group(2, lambda t: base + 3 * n3 + 2 * t), 0)
    lax.fori_loop(0, n1, group(1, lambda t: base + 3 * n3 + 2 * n2), 0)

    @pl.when(e == pl.num_programs(0) - 1)
    def _():
        for back in range(N_XSLOTS, 0, -1):
            @pl.when(n_used >= back)
            def _():
                y_copy(n_used - back).wait()


def _experts(tile_start, n_tiles_e, n_used, tile_row, tile_valid, xs, w_gate, w_up, w_down):
    d = w_gate.shape[1]
    n_rows = tile_row.shape[0] * TMX

    assert N_EXPERTS >= EW_AHEAD
    grid_spec = pltpu.PrefetchScalarGridSpec(
        num_scalar_prefetch=5,
        grid=(N_EXPERTS,),
        in_specs=[pl.BlockSpec(memory_space=pl.ANY)] * 4,
        out_specs=pl.BlockSpec(memory_space=pl.ANY),
        scratch_shapes=[pltpu.VMEM((N_XSLOTS, TMX) + xs.shape[1:], BF16),
                        pltpu.VMEM((N_XSLOTS, TMX) + xs.shape[1:], BF16),
                        pltpu.VMEM((EW_SLOTS, d, D_EXPERT), w_gate.dtype),
                        pltpu.VMEM((EW_SLOTS, d, D_EXPERT), w_up.dtype),
                        pltpu.VMEM((EW_SLOTS, D_EXPERT, d), w_down.dtype),
                        pltpu.SemaphoreType.DMA((N_XSLOTS,)),
                        pltpu.SemaphoreType.DMA((N_XSLOTS,)),
                        pltpu.SemaphoreType.DMA((EW_SLOTS,))],
    )
    return pl.pallas_call(
        _experts_kernel,
        grid_spec=grid_spec,
        out_shape=jax.ShapeDtypeStruct((n_rows,) + xs.shape[1:], BF16),
        compiler_params=_cparams(("arbitrary",)),
        name="experts",
    )(tile_start, n_tiles_e, n_used, tile_row, tile_valid, xs, w_gate, w_up, w_down)


def _combine_kernel(pos_ref, posn_ref, y_hbm, h1_ref, wt_ref, g2_ref, b2_ref, o_ref, ybuf, sem):
    i = pl.program_id(0)
    n = pl.num_programs(0)
    slot = i % 2

    def row_copy(src, sl, which, j):
        return pltpu.make_async_copy(y_hbm.at[pl.ds(src, 1)],
                                     ybuf.at[sl, which, pl.ds(j, 1)], sem.at[sl])

    def issue(p_ref, sl):
        def body(j, c):
            row_copy(p_ref[0, j], sl, 0, j).start(priority=0)
            row_copy(p_ref[1, j], sl, 1, j).start(priority=1)
            return c
        lax.fori_loop(0, TM_OUT, body, 0, unroll=8)

    def drain(sl):
        for which in range(2):
            pltpu.make_async_copy(y_hbm.at[pl.ds(0, TM_OUT)], ybuf.at[sl, which],
                                  sem.at[sl]).wait()

    @pl.when(i == 0)
    def _():
        issue(pos_ref, 0)

    drain(slot)

    for j in range(TM_OUT):
        row_copy(posn_ref[0, j], 1 - slot, 0, j).start(priority=0)
        row_copy(posn_ref[1, j], 1 - slot, 1, j).start(priority=1)

    y1 = ybuf[slot, 0].reshape(h1_ref.shape).astype(F32)
    y2 = ybuf[slot, 1].reshape(h1_ref.shape).astype(F32)
    ffn = wt_ref[:, 0:1] * y1 + wt_ref[:, 1:2] * y2
    o_ref[...] = _layernorm(DN_ALPHA * h1_ref[...] + ffn, g2_ref[...], b2_ref[...])

    @pl.when(i == n - 1)
    def _():
        drain(1 - slot)


def _combine(pos3, ys, h1, wtok, g2, b2):
    t, d = h1.shape
    n = t // TM_OUT
    return pl.pallas_call(
        _combine_kernel,
        grid=(n,),
        in_specs=[
            pl.BlockSpec((None, 2, TM_OUT), lambda i: (i, 0, 0), memory_space=pltpu.SMEM),
            pl.BlockSpec((None, 2, TM_OUT), lambda i: (jnp.minimum(i + 1, n - 1), 0, 0),
                         memory_space=pltpu.SMEM),
            pl.BlockSpec(memory_space=pl.ANY),
            pl.BlockSpec((TM_OUT, d), lambda i: (i, 0)),
            pl.BlockSpec((TM_OUT, ROUTE_ROWS), lambda i: (i, 0)),
            pl.BlockSpec((1, d), lambda i: (0, 0)),
            pl.BlockSpec((1, d), lambda i: (0, 0)),
        ],
        out_specs=pl.BlockSpec((TM_OUT, d), lambda i: (i, 0)),
        out_shape=jax.ShapeDtypeStruct((t, d), F32),
        scratch_shapes=[pltpu.VMEM((2, 2, TM_OUT) + ys.shape[1:], BF16),
                        pltpu.SemaphoreType.DMA((2,))],
        compiler_params=_cparams(("arbitrary",)),
        name="combine",
    )(pos3, pos3, ys, h1, wtok, g2, b2)


def _lambda_init(layer):
    return 0.8 - 0.6 * math.exp(-0.3 * layer)


def kernel(x, w_in, w_out, ln1_g, ln1_b, ln2_g, ln2_b, rel_bias, lam_q1, lam_k1, lam_q2, lam_k2,
           subln_g, sg_ln_g, sg_ln_b, sg_w, sg_b, w_router_group, b_router_group,
           w_router_expert, b_router_expert, w_exp_gate, w_exp_up, w_exp_down):
    b, s, d = x.shape
    t = b * s
    assert w_in.shape[0] == DEPTH == 1
    l = 0
    lam_init = _lambda_init(l)
    x2 = x.reshape(t, d)

    proj = _inproj(x2, w_in[l])

    lam = (jnp.exp(jnp.sum(lam_q1[l].astype(F32) * lam_k1[l].astype(F32)))
           - jnp.exp(jnp.sum(lam_q2[l].astype(F32) * lam_k2[l].astype(F32))) + lam_init)
    side = [w_exp_gate[l].reshape(-1, D_EXPERT), w_exp_up[l].reshape(-1, D_EXPERT),
            w_exp_down[l].reshape(-1, d), w_out[l]]
    a, wg_b, wu_b, wd_b, wout_b = _attention(
        proj.reshape(b, s, -1), _bias_sequences(rel_bias, s), lam.reshape(1),
        subln_g[l].reshape(1, -1), lam_init, side)

    wr = jnp.concatenate([w_router_group[l].T,
                          jnp.zeros((NR_EXPERT0 - N_GROUPS, d), F32),
                          jnp.transpose(w_router_expert[l], (0, 2, 1)).reshape(N_EXPERTS, d)],
                         axis=0)
    br = jnp.concatenate([b_router_group[l], jnp.zeros((NR_EXPERT0 - N_GROUPS,), F32),
                          b_router_expert[l].reshape(-1)]).reshape(NR_PAD, 1)
    wr_hi = wr.astype(BF16)
    wr = jnp.concatenate([wr_hi, (wr - wr_hi.astype(F32)).astype(BF16)], axis=0)
    h1, xs, ri, rw, cnt = _mix(
        a.reshape(t, -1), proj, x2, wout_b, sg_w[l].astype(BF16),
        sg_b[l].reshape(SG_GROUPS, SG_CHUNK, 1), sg_ln_g[l], sg_ln_b[l],
        ln1_g[l].reshape(1, d), ln1_b[l].reshape(1, d), wr, br)

    counts = cnt[:, 0]
    tiles_e = (counts + TMX - 1) // TMX
    tile_end = jnp.cumsum(tiles_e)
    tile_start = tile_end - tiles_e
    n_used = tile_end[-1:]
    max_tiles = (2 * t) // TMX + N_EXPERTS
    row_start = (tile_start * TMX).astype(I32)
    eids = jnp.arange(N_EXPERTS, dtype=I32)[:, None, None]
    pos = jnp.sum(jnp.where(ri[0:2][None] == eids, row_start[:, None, None], 0), axis=0) + ri[2:4]
    pos3 = pos.reshape(2, t // TM_OUT, TM_OUT).transpose(1, 0, 2)
    tile_ids = jnp.arange(max_tiles, dtype=I32)
    tile_e = jnp.minimum(jnp.sum(tile_end[None, :] <= tile_ids[:, None], axis=1), N_EXPERTS - 1)
    local = tile_ids - tile_start[tile_e]
    in_use = tile_ids < n_used[0]
    tile_row = jnp.where(in_use, tile_e * XS_CAP + local * TMX, 0).astype(I32)
    tile_valid = jnp.where(in_use, jnp.clip(counts[tile_e] - local * TMX, 0, TMX), 0).astype(I32)

    ys = _experts(tile_start.astype(I32), tiles_e.astype(I32), n_used.astype(I32),
                  tile_row, tile_valid, xs,
                  wg_b.reshape(N_EXPERTS, d, D_EXPERT), wu_b.reshape(N_EXPERTS, d, D_EXPERT),
                  wd_b.reshape(N_EXPERTS, D_EXPERT, d))
    out = _combine(pos3, ys, h1, rw.T, ln2_g[l].reshape(1, d), ln2_b[l].reshape(1, d))
    return out.reshape(b, s, d)
```

```python
import functools
import math

import jax
import jax.numpy as jnp
from jax import lax
from jax.experimental import pallas as pl
from jax.experimental.pallas import tpu as pltpu

F32 = jnp.float32
BF16 = jnp.bfloat16
I32 = jnp.int32

ATT_HEADS = 8
ATT_QK_DIM = 64
ATT_V_DIM = 128
ATT_WIDTH = ATT_HEADS * ATT_V_DIM
SG_GROUPS = 8
SG_DIM = 128
SG_WIDTH = SG_GROUPS * SG_DIM
SG_CHUNK = 128
REL_BUCKETS = 32
REL_MAX_DIST = 128
N_GROUPS = 4
EXPERTS_PER_GROUP = 8
N_EXPERTS = N_GROUPS * EXPERTS_PER_GROUP
D_EXPERT = 256
DEPTH = 1
DN_ALPHA = (2.0 * DEPTH) ** 0.25
LN_EPS = 1e-5
LOG2E = math.log2(math.e)

LANES = 128
SUBLANES = 8
VMEM_LIMIT_BYTES = 56 * 1024 * 1024

TM_IN = 1024
TN_IN = 1024
W_AHEAD = 2
W_SLOTS = W_AHEAD + 1
QB = 256
KC = 256
N_BIAS_TILES = 5
TM_MIX = 512
TMX = 256
TM_OUT = 256
H_AHEAD = 2
H_SLOTS = H_AHEAD + 1
X_AHEAD = 3
N_XSLOTS = 2 * X_AHEAD
ROUTE_ROWS = 8
XS_CAP = 8192 + TMX
NR_EXPERT0 = 8
NR_PAD = NR_EXPERT0 + 32


def _cparams(sem, flags=None):
    return pltpu.CompilerParams(dimension_semantics=sem, vmem_limit_bytes=VMEM_LIMIT_BYTES,
                                flags=flags)


def _inproj_kernel(x_ref, w_hbm, o_ref, xb_ref, wbuf, wsem, *, n_q, n_lin, q_scale):
    j = pl.program_id(1)
    nj = pl.num_programs(1)
    step = pl.program_id(0) * nj + j
    n_steps = pl.num_programs(0) * nj

    def w_copy(s):
        cols = pl.ds(pl.multiple_of(lax.rem(s, nj) * TN_IN, TN_IN), TN_IN)
        slot = lax.rem(s, W_SLOTS)
        return pltpu.make_async_copy(w_hbm.at[:, cols], wbuf.at[slot], wsem.at[slot])

    @pl.when(step == 0)
    def _():
        for s in range(W_AHEAD):
            w_copy(s).start()

    @pl.when(step + W_AHEAD < n_steps)
    def _():
        w_copy(step + W_AHEAD).start()

    @pl.when(j == 0)
    def _():
        xb_ref[...] = x_ref[...].astype(BF16)

    w_copy(step).wait()
    w = wbuf[lax.rem(step, W_SLOTS)].astype(BF16)
    acc = jnp.dot(xb_ref[...], w, preferred_element_type=F32)

    @pl.when(j < n_q)
    def _():
        o_ref[...] = (acc * q_scale).astype(BF16)

    @pl.when(jnp.logical_and(j >= n_q, j < n_lin))
    def _():
        o_ref[...] = acc.astype(BF16)

    @pl.when(j >= n_lin)
    def _():
        o_ref[...] = jax.nn.gelu(acc).astype(BF16)


def _inproj(x2, w_in):
    t, d = x2.shape
    n = w_in.shape[1]
    assert (t // TM_IN) * (n // TN_IN) >= W_AHEAD
    n_q = (ATT_HEADS * 2 * ATT_QK_DIM) // TN_IN
    n_lin = (2 * ATT_HEADS * 2 * ATT_QK_DIM + ATT_WIDTH) // TN_IN
    kern = functools.partial(_inproj_kernel, n_q=n_q, n_lin=n_lin,
                             q_scale=(ATT_QK_DIM ** -0.5) * LOG2E)
    return pl.pallas_call(
        kern,
        grid=(t // TM_IN, n // TN_IN),
        in_specs=[pl.BlockSpec((TM_IN, d), lambda i, j: (i, 0)),
                  pl.BlockSpec(memory_space=pl.ANY)],
        out_specs=pl.BlockSpec((TM_IN, TN_IN), lambda i, j: (i, j)),
        out_shape=jax.ShapeDtypeStruct((t, n), BF16),
        scratch_shapes=[pltpu.VMEM((TM_IN, d), BF16),
                        pltpu.VMEM((W_SLOTS, d, TN_IN), w_in.dtype),
                        pltpu.SemaphoreType.DMA((W_SLOTS,))],
        compiler_params=_cparams(("arbitrary", "arbitrary")),
        name="inproj",
    )(x2, w_in)


def _attn_kernel(lam_ref, q_ref, k_ref, v_ref, bseq_ref, g_ref, *rest, n_kc, out_scale):
    n_side = (len(rest) - 3) // 2
    side_in, o_ref, side_out = rest[:n_side], rest[n_side], rest[n_side + 1:2 * n_side + 1]
    vaug_ref, bias_ref = rest[2 * n_side + 1:]
    qi = pl.program_id(2)

    @pl.when(qi == 0)
    def _():
        vaug_ref[:, :ATT_V_DIM] = v_ref[0]
        vaug_ref[:, ATT_V_DIM:] = jnp.ones((vaug_ref.shape[0], ATT_V_DIM), BF16)
        for t in range(N_BIAS_TILES):
            seq = jnp.broadcast_to(bseq_ref[0, t:t + 1, :], (QB, 2 * KC))
            bias_ref[t] = pltpu.roll(seq, 0, 1, stride=1, stride_axis=0)[:, :KC]

    q = q_ref[0]
    lane = lax.broadcasted_iota(I32, q.shape, 1)
    zero = jnp.zeros_like(q)
    qq = jnp.concatenate([jnp.where(lane < ATT_QK_DIM, q, zero),
                          jnp.where(lane >= ATT_QK_DIM, q, zero)], axis=0)

    for src, dst in zip(side_in, side_out):
        dst[...] = src[...].astype(dst.dtype)

    m = jnp.full((2 * QB, 1), -jnp.inf, F32)
    acc = jnp.zeros((2 * QB, 2 * ATT_V_DIM), F32)
    for kc in range(n_kc):
        kblk = k_ref[0, kc * KC:(kc + 1) * KC, :]
        s = lax.dot_general(qq, kblk, (((1,), (1,)), ((), ())),
                            preferred_element_type=F32)
        b = bias_ref[jnp.clip(kc - qi, -2, 2) + 2]
        s = s + jnp.concatenate([b, b], axis=0)
        m_new = jnp.maximum(m, jnp.max(s, axis=1, keepdims=True))
        p = jnp.exp2(s - m_new).astype(BF16)
        acc = acc * jnp.exp2(m - m_new) + jnp.dot(
            p, vaug_ref[kc * KC:(kc + 1) * KC, :], preferred_element_type=F32)
        m = m_new
    lam = lam_ref[0]
    o0 = acc[:QB, :ATT_V_DIM] / acc[:QB, ATT_V_DIM:ATT_V_DIM + 1]
    o1 = acc[QB:, :ATT_V_DIM] / acc[QB:, ATT_V_DIM:ATT_V_DIM + 1]
    of = o0 - lam * o1
    of = of * lax.rsqrt(jnp.mean(of * of, axis=-1, keepdims=True) + LN_EPS)
    o_ref[0] = (of * (g_ref[...] * out_scale)).astype(o_ref.dtype)


def _attention(proj3, bias_seq, lam, subln_g, lam_init, side):
    b, s, _ = proj3.shape
    n_kc = s // KC
    n_q = s // QB
    n_steps = b * ATT_HEADS * n_q
    h_off_k = ATT_HEADS
    h_off_v = 2 * ATT_HEADS
    kern = functools.partial(_attn_kernel, n_kc=n_kc, out_scale=1.0 - lam_init)

    side_specs = []
    for arr in side:
        rows, cols = arr.shape
        blk = max(rows // n_steps, 2 * SUBLANES)
        hold = blk * n_steps // rows
        assert rows % blk == 0 and hold * rows == blk * n_steps
        side_specs.append(pl.BlockSpec(
            (blk, cols),
            lambda bi, h, qi, hold=hold: (((bi * ATT_HEADS + h) * n_q + qi) // hold, 0)))
    return pl.pallas_call(
        kern,
        grid=(b, ATT_HEADS, n_q),
        in_specs=[
            pl.BlockSpec(memory_space=pltpu.SMEM),
            pl.BlockSpec((1, QB, LANES), lambda bi, h, qi: (bi, qi, h)),
            pl.BlockSpec((1, s, LANES), lambda bi, h, qi: (bi, 0, h_off_k + h)),
            pl.BlockSpec((1, s, LANES), lambda bi, h, qi: (bi, 0, h_off_v + h)),
            pl.BlockSpec((1, N_BIAS_TILES, 2 * KC), lambda bi, h, qi: (h, 0, 0)),
            pl.BlockSpec((1, ATT_V_DIM), lambda bi, h, qi: (0, 0)),
        ] + side_specs,
        out_specs=[pl.BlockSpec((1, QB, ATT_V_DIM), lambda bi, h, qi: (bi, qi, h))] + side_specs,
        out_shape=[jax.ShapeDtypeStruct((b, s, ATT_WIDTH), BF16)]
        + [jax.ShapeDtypeStruct(arr.shape, BF16) for arr in side],
        scratch_shapes=[pltpu.VMEM((s, 2 * ATT_V_DIM), BF16),
                        pltpu.VMEM((N_BIAS_TILES, QB, KC), F32)],
        compiler_params=_cparams(("arbitrary", "arbitrary", "arbitrary")),
        name="attn",
    )(lam, proj3, proj3, proj3, bias_seq, subln_g, *side)


def _rel_bucket(rel):
    half = REL_BUCKETS // 2
    max_exact = half // 2
    ret = jnp.where(rel > 0, half, 0)
    n = jnp.abs(rel)
    nf = jnp.maximum(n, 1).astype(F32)
    large = max_exact + (jnp.log(nf / max_exact) / math.log(REL_MAX_DIST / max_exact)
                         * (half - max_exact)).astype(I32)
    large = jnp.minimum(large, half - 1)
    return ret + jnp.where(n < max_exact, n, large)


def _bias_sequences(rel_bias, s):
    assert REL_MAX_DIST <= KC and QB == KC
    period = 2 * KC
    j = jnp.arange(period, dtype=I32)
    off = jnp.where(j < KC, j, j - period)
    d = jnp.arange(-2, 3, dtype=I32)[:, None]
    rel = jnp.clip(d * KC + off[None, :], -(s - 1), s - 1)
    return jnp.transpose(rel_bias[_rel_bucket(rel)], (2, 0, 1)).astype(F32) * LOG2E


def _layernorm(v, g, b):
    mu = jnp.mean(v, axis=-1, keepdims=True)
    var = jnp.mean(jnp.square(v - mu), axis=-1, keepdims=True)
    return (v - mu) * lax.rsqrt(var + LN_EPS) * g + b


def _first_argmax(vals, nrows):
    mx = jnp.max(vals, axis=0, keepdims=True)
    row = lax.broadcasted_iota(I32, vals.shape, 0)
    idx = jnp.min(jnp.where(vals == mx, row, nrows), axis=0, keepdims=True)
    return mx, idx


def _mix_kernel(a_ref, u_ref, vg_ref, x_ref, wout_ref, sgw_ref, sgb_ref, sgg_ref, sgbeta_ref,
                g1_ref, b1_ref, wr_ref, br_ref,
                h1_ref, xs_hbm, ri_ref, rw_ref, cnt_ref,
                run_ref, hp_s, posv_ref, psm, ssem, rsem):
    i = pl.program_id(0)
    n = pl.num_programs(0)
    slot = i % 2
    prev = 1 - slot

    def pos_copy(sl):
        return pltpu.make_async_copy(posv_ref, psm.at[sl], ssem.at[sl])

    def row_copy(sl, which, j):
        return pltpu.make_async_copy(hp_s.at[sl, pl.ds(j, 1)],
                                     xs_hbm.at[pl.ds(psm[sl, which, j], 1)], rsem.at[sl])

    def drain(sl):
        whole = pltpu.make_async_copy(hp_s.at[sl], xs_hbm.at[pl.ds(0, TM_MIX)], rsem.at[sl])
        whole.wait()
        whole.wait()

    @pl.when(i == 0)
    def _():
        run_ref[...] = jnp.zeros_like(run_ref)
        hp_s[1] = jnp.zeros(hp_s.shape[1:], hp_s.dtype)
        spare = N_EXPERTS * XS_CAP + lax.broadcasted_iota(I32, posv_ref.shape, 1)
        spare = spare + TM_MIX * lax.broadcasted_iota(I32, posv_ref.shape, 0)
        posv_ref[...] = jnp.where(spare < xs_hbm.shape[0], spare, 0)
        pos_copy(1).start()

    @pl.when(i > 0)
    def _():
        drain(slot)

    pos_copy(prev).wait()

    for j in range(TM_MIX):
        row_copy(prev, 0, j).start(priority=0)
        row_copy(prev, 1, j).start(priority=1)

    h1_ref[...] = jnp.dot(a_ref[...], wout_ref[:ATT_WIDTH, :], preferred_element_type=F32)

    n_chunks = TM_MIX // SG_CHUNK
    rows = []
    for c in range(n_chunks):
        cols = []
        for g in range(SG_GROUPS):
            rs = slice(c * SG_CHUNK, (c + 1) * SG_CHUNK)
            cs = slice(g * SG_DIM, (g + 1) * SG_DIM)
            vt = vg_ref[rs, cs].astype(F32)
            vn = _layernorm(vt, sgg_ref[g:g + 1, :], sgbeta_ref[g:g + 1, :])
            mixed = jnp.dot(sgw_ref[g], vn.astype(BF16), preferred_element_type=F32) + sgb_ref[g]
            cols.append((u_ref[rs, cs].astype(F32) * mixed).astype(BF16))
        rows.append(jnp.concatenate(cols, axis=1))
    sgate = jnp.concatenate(rows, axis=0)

    mix = h1_ref[...] + jnp.dot(sgate, wout_ref[ATT_WIDTH:, :], preferred_element_type=F32)
    h1 = _layernorm(DN_ALPHA * x_ref[...] + mix, g1_ref[...], b1_ref[...])
    h1_ref[...] = h1
    hp_s[slot] = h1.astype(BF16).reshape(hp_s.shape[1:])

    nt = (((1,), (1,)), ((), ()))
    h1_hi = h1.astype(BF16)
    h1_lo = (h1 - h1_hi.astype(F32)).astype(BF16)
    both = lax.dot_general(wr_ref[...], h1_hi, nt, preferred_element_type=F32)
    lt = both[:NR_PAD] + both[NR_PAD:] + lax.dot_general(
        wr_ref[:NR_PAD, :], h1_lo, nt, preferred_element_type=F32)
    lt = lt + br_ref[...]
    g_logits = lt[0:N_GROUPS]
    gmax, g_idx = _first_argmax(g_logits, N_GROUPS)
    g_gate = 1.0 / jnp.sum(jnp.exp(g_logits - gmax), axis=0, keepdims=True)
    e_logits = jnp.zeros((EXPERTS_PER_GROUP, TM_MIX), F32)
    for g in range(N_GROUPS):
        lo = NR_EXPERT0 + g * EXPERTS_PER_GROUP
        e_logits = jnp.where(g_idx == g, lt[lo:lo + EXPERTS_PER_GROUP], e_logits)
    v1, i1 = _first_argmax(e_logits, EXPERTS_PER_GROUP)
    erow = lax.broadcasted_iota(I32, e_logits.shape, 0)
    v2, i2 = _first_argmax(jnp.where(erow == i1, -jnp.inf, e_logits), EXPERTS_PER_GROUP)
    t = jnp.exp(v2 - v1)
    w1 = g_gate / (1.0 + t)
    w2 = g_gate * t / (1.0 + t)
    e1 = g_idx * EXPERTS_PER_GROUP + i1
    e2 = g_idx * EXPERTS_PER_GROUP + i2

    xrow = lax.broadcasted_iota(I32, (N_EXPERTS, TM_MIX), 0)
    oh1 = xrow == e1
    oh2 = xrow == e2
    oh = jnp.where(jnp.logical_or(oh1, oh2), 1.0, 0.0)
    tr = lax.broadcasted_iota(I32, (TM_MIX, TM_MIX), 0)
    tc = lax.broadcasted_iota(I32, (TM_MIX, TM_MIX), 1)
    before = jnp.where(tr < tc, 1.0, 0.0).astype(BF16)
    cum = jnp.dot(oh.astype(BF16), before, preferred_element_type=F32)
    base = cum + run_ref[:, 0:1]
    rank1 = jnp.sum(jnp.where(oh1, base, 0.0), axis=0, keepdims=True).astype(I32)
    rank2 = jnp.sum(jnp.where(oh2, base, 0.0), axis=0, keepdims=True).astype(I32)
    run_ref[...] = run_ref[...] + jnp.sum(oh, axis=1, keepdims=True)
    cnt_ref[...] = run_ref[...].astype(I32)

    zi = jnp.zeros((ROUTE_ROWS - 4, TM_MIX), I32)
    ri_ref[...] = jnp.concatenate([e1, e2, rank1, rank2, zi], axis=0)
    zw = jnp.zeros((ROUTE_ROWS - 2, TM_MIX), F32)
    rw_ref[...] = jnp.concatenate([w1, w2, zw], axis=0)

    posv_ref[...] = jnp.concatenate([e1 * XS_CAP + rank1, e2 * XS_CAP + rank2, zi, zi[:2]], axis=0)
    pos_copy(slot).start()

    @pl.when(i == n - 1)
    def _():
        pos_copy(slot).wait()

        def last(j, c):
            row_copy(slot, 0, j).start(priority=0)
            row_copy(slot, 1, j).start(priority=1)
            return c
        lax.fori_loop(0, TM_MIX, last, 0, unroll=8)
        drain(prev)
        drain(slot)


def _mix(a2, proj2, x2, wout_b, sgw_b, sgb, sgg, sgbeta, g1, b1, wr, br):
    t, d = x2.shape
    assert XS_CAP >= t + TMX
    u_blk = (proj2.shape[1] - 2 * SG_WIDTH) // SG_WIDTH
    const = lambda *shape: pl.BlockSpec(shape, lambda i: (0,) * len(shape))
    return pl.pallas_call(
        _mix_kernel,
        grid=(t // TM_MIX,),
        in_specs=[
            pl.BlockSpec((TM_MIX, ATT_WIDTH), lambda i: (i, 0)),
            pl.BlockSpec((TM_MIX, SG_WIDTH), lambda i: (i, u_blk)),
            pl.BlockSpec((TM_MIX, SG_WIDTH), lambda i: (i, u_blk + 1)),
            pl.BlockSpec((TM_MIX, d), lambda i: (i, 0)),
            pl.BlockSpec((d, d), lambda i: (0, 0), pipeline_mode=pl.Buffered(1)),
            const(SG_GROUPS, SG_CHUNK, SG_CHUNK),
            const(SG_GROUPS, SG_CHUNK, 1),
            const(SG_GROUPS, SG_DIM),
            const(SG_GROUPS, SG_DIM),
            const(1, d),
            const(1, d),
            const(2 * NR_PAD, d),
            const(NR_PAD, 1),
        ],
        out_specs=[
            pl.BlockSpec((TM_MIX, d), lambda i: (i, 0)),
            pl.BlockSpec(memory_space=pl.ANY),
            pl.BlockSpec((ROUTE_ROWS, TM_MIX), lambda i: (0, i)),
            pl.BlockSpec((ROUTE_ROWS, TM_MIX), lambda i: (0, i)),
            const(N_EXPERTS, LANES),
        ],
        out_shape=[
            jax.ShapeDtypeStruct((t, d), F32),
            jax.ShapeDtypeStruct((N_EXPERTS * XS_CAP + 2 * TM_MIX, d // LANES, LANES), BF16),
            jax.ShapeDtypeStruct((ROUTE_ROWS, t), I32),
            jax.ShapeDtypeStruct((ROUTE_ROWS, t), F32),
            jax.ShapeDtypeStruct((N_EXPERTS, LANES), I32),
        ],
        scratch_shapes=[pltpu.VMEM((N_EXPERTS, LANES), F32),
                        pltpu.VMEM((2, TM_MIX, d // LANES, LANES), BF16),
                        pltpu.VMEM((ROUTE_ROWS, TM_MIX), I32),
                        pltpu.SMEM((2, ROUTE_ROWS, TM_MIX), I32),
                        pltpu.SemaphoreType.DMA((2,)),
                        pltpu.SemaphoreType.DMA((2,))],
        compiler_params=_cparams(("arbitrary",)),
        name="mix",
    )(a2, proj2, proj2, x2, wout_b, sgw_b, sgb, sgg, sgbeta, g1, b1, wr, br)


def _experts_kernel(ts_ref, nt_ref, nu_ref, trow_ref, nval_ref, x_hbm, wg_ref, wu_ref, wd_ref,
                    y_hbm, xbuf, ybuf, xsem, ysem):
    e = pl.program_id(0)
    d = wg_ref.shape[1]
    n_used = nu_ref[0]

    def x_copy(g):
        rows = pl.ds(pl.multiple_of(trow_ref[g], TMX), TMX)
        slot = lax.rem(g, N_XSLOTS)
        return pltpu.make_async_copy(x_hbm.at[rows], xbuf.at[slot], xsem.at[slot])

    def y_copy(g):
        rows = pl.ds(pl.multiple_of(g * TMX, TMX), TMX)
        slot = lax.rem(g, N_XSLOTS)
        return pltpu.make_async_copy(ybuf.at[slot], y_hbm.at[rows], ysem.at[slot])

    def start_x(g):
        @pl.when(g < n_used)
        def _():
            x_copy(g).start(priority=1)

    def free_y(g):
        @pl.when(g >= N_XSLOTS)
        def _():
            y_copy(g - N_XSLOTS).wait()

    def load(g):
        xt = xbuf[lax.rem(g, N_XSLOTS)].reshape(TMX, d)
        row = lax.broadcasted_iota(I32, xt.shape, 0)
        return jnp.where(row < nval_ref[g], xt, jnp.zeros_like(xt))

    def mlp(xb):
        gate = jnp.dot(xb, wg_ref[0], preferred_element_type=F32)
        up = jnp.dot(xb, wu_ref[0], preferred_element_type=F32)
        hmid = (gate * jax.nn.sigmoid(gate) * up).astype(BF16)
        return jnp.dot(hmid, wd_ref[0], preferred_element_type=F32)

    @pl.when(e == 0)
    def _():
        for g in range(X_AHEAD):
            start_x(g)

    def group(k, first):
        def body(t, carry):
            g = first(t)
            for j in range(k):
                x_copy(g + j).wait()
            for j in range(k):
                start_x(g + X_AHEAD + j)
                free_y(g + j)
            y = mlp(jnp.concatenate([load(g + j) for j in range(k)], axis=0))
            for j in range(k):
                ybuf[lax.rem(g + j, N_XSLOTS)] = y[j * TMX:(j + 1) * TMX].astype(BF16).reshape(
                    ybuf.shape[1:])
                y_copy(g + j).start(priority=1)
            return carry
        return body

    nt = nt_ref[e]
    n3 = jnp.where(jnp.logical_and(nt >= 3, nt % 2 == 1), 1, 0)
    n2 = (nt - 3 * n3) // 2
    n1 = nt - 3 * n3 - 2 * n2
    base = ts_ref[e]
    lax.fori_loop(0, n3, group(3, lambda t: base), 0)
    lax.fori_loop(0, n2, group(2, lambda t: base + 3 * n3 + 2 * t), 0)
    lax.fori_loop(0, n1, group(1, lambda t: base + 3 * n3 + 2 * n2), 0)

    @pl.when(e == pl.num_programs(0) - 1)
    def _():
        for back in range(N_XSLOTS, 0, -1):
            @pl.when(n_used >= back)
            def _():
                y_copy(n_used - back).wait()


def _experts(tile_start, n_tiles_e, n_used, tile_row, tile_valid, xs, w_gate, w_up, w_down):
    d = w_gate.shape[1]
    n_rows = tile_row.shape[0] * TMX

    def w_map(e, ts, nt, nu, trow, nval):
        return (e, 0, 0)

    grid_spec = pltpu.PrefetchScalarGridSpec(
        num_scalar_prefetch=5,
        grid=(N_EXPERTS,),
        in_specs=[
            pl.BlockSpec(memory_space=pl.ANY),
            pl.BlockSpec((1, d, D_EXPERT), w_map),
            pl.BlockSpec((1, d, D_EXPERT), w_map),
            pl.BlockSpec((1, D_EXPERT, d), w_map),
        ],
        out_specs=pl.BlockSpec(memory_space=pl.ANY),
        scratch_shapes=[pltpu.VMEM((N_XSLOTS, TMX) + xs.shape[1:], BF16),
                        pltpu.VMEM((N_XSLOTS, TMX) + xs.shape[1:], BF16),
                        pltpu.SemaphoreType.DMA((N_XSLOTS,)),
                        pltpu.SemaphoreType.DMA((N_XSLOTS,))],
    )
    return pl.pallas_call(
        _experts_kernel,
        grid_spec=grid_spec,
        out_shape=jax.ShapeDtypeStruct((n_rows,) + xs.shape[1:], BF16),
        compiler_params=_cparams(("arbitrary",)),
        name="experts",
    )(tile_start, n_tiles_e, n_used, tile_row, tile_valid, xs, w_gate, w_up, w_down)


def _combine_kernel(pos_ref, posn_ref, y_hbm, h1_hbm, wt_ref, g2_ref, b2_ref, o_ref,
                    ybuf, hbuf, sem, hsem):
    i = pl.program_id(0)
    n = pl.num_programs(0)
    slot = i % 2

    def row_copy(src, sl, which, j):
        return pltpu.make_async_copy(y_hbm.at[pl.ds(src, 1)],
                                     ybuf.at[sl, which, pl.ds(j, 1)], sem.at[sl])

    def issue(p_ref, sl):
        def body(j, c):
            row_copy(p_ref[0, j], sl, 0, j).start(priority=0)
            row_copy(p_ref[1, j], sl, 1, j).start(priority=1)
            return c
        lax.fori_loop(0, TM_OUT, body, 0, unroll=8)

    def drain(sl):
        for which in range(2):
            pltpu.make_async_copy(y_hbm.at[pl.ds(0, TM_OUT)], ybuf.at[sl, which],
                                  sem.at[sl]).wait()

    def h_copy(s):
        rows = pl.ds(pl.multiple_of(s * TM_OUT, TM_OUT), TM_OUT)
        hs = lax.rem(s, H_SLOTS)
        return pltpu.make_async_copy(h1_hbm.at[rows], hbuf.at[hs], hsem.at[hs])

    @pl.when(i == 0)
    def _():
        for s in range(H_AHEAD):
            h_copy(s).start()
        issue(pos_ref, 0)

    @pl.when(i + H_AHEAD < n)
    def _():
        h_copy(i + H_AHEAD).start()

    drain(slot)
    h_copy(i).wait()

    for j in range(TM_OUT):
        row_copy(posn_ref[0, j], 1 - slot, 0, j).start(priority=0)
        row_copy(posn_ref[1, j], 1 - slot, 1, j).start(priority=1)

    y1 = ybuf[slot, 0].reshape(o_ref.shape).astype(F32)
    y2 = ybuf[slot, 1].reshape(o_ref.shape).astype(F32)
    h1 = hbuf[lax.rem(i, H_SLOTS)]
    ffn = wt_ref[:, 0:1] * y1 + wt_ref[:, 1:2] * y2
    o_ref[...] = _layernorm(DN_ALPHA * h1 + ffn, g2_ref[...], b2_ref[...])

    @pl.when(i == n - 1)
    def _():
        drain(1 - slot)


def _combine(pos3, ys, h1, wtok, g2, b2):
    t, d = h1.shape
    n = t // TM_OUT
    return pl.pallas_call(
        _combine_kernel,
        grid=(n,),
        in_specs=[
            pl.BlockSpec((None, 2, TM_OUT), lambda i: (i, 0, 0), memory_space=pltpu.SMEM),
            pl.BlockSpec((None, 2, TM_OUT), lambda i: (jnp.minimum(i + 1, n - 1), 0, 0),
                         memory_space=pltpu.SMEM),
            pl.BlockSpec(memory_space=pl.ANY),
            pl.BlockSpec(memory_space=pl.ANY),
            pl.BlockSpec((TM_OUT, ROUTE_ROWS), lambda i: (i, 0)),
            pl.BlockSpec((1, d), lambda i: (0, 0)),
            pl.BlockSpec((1, d), lambda i: (0, 0)),
        ],
        out_specs=pl.BlockSpec((TM_OUT, d), lambda i: (i, 0)),
        out_shape=jax.ShapeDtypeStruct((t, d), F32),
        scratch_shapes=[pltpu.VMEM((2, 2, TM_OUT) + ys.shape[1:], BF16),
                        pltpu.VMEM((H_SLOTS, TM_OUT, d), h1.dtype),
                        pltpu.SemaphoreType.DMA((2,)),
                        pltpu.SemaphoreType.DMA((H_SLOTS,))],
        compiler_params=_cparams(("arbitrary",)),
        name="combine",
    )(pos3, pos3, ys, h1, wtok, g2, b2)


def _lambda_init(layer):
    return 0.8 - 0.6 * math.exp(-0.3 * layer)


def kernel(x, w_in, w_out, ln1_g, ln1_b, ln2_g, ln2_b, rel_bias, lam_q1, lam_k1, lam_q2, lam_k2,
           subln_g, sg_ln_g, sg_ln_b, sg_w, sg_b, w_router_group, b_router_group,
           w_router_expert, b_router_expert, w_exp_gate, w_exp_up, w_exp_down):
    b, s, d = x.shape
    t = b * s
    assert w_in.shape[0] == DEPTH == 1
    l = 0
    lam_init = _lambda_init(l)
    x2 = x.reshape(t, d)

    proj = _inproj(x2, w_in[l])

    lam = (jnp.exp(jnp.sum(lam_q1[l].astype(F32) * lam_k1[l].astype(F32)))
           - jnp.exp(jnp.sum(lam_q2[l].astype(F32) * lam_k2[l].astype(F32))) + lam_init)
    side = [w_exp_gate[l].reshape(-1, D_EXPERT), w_exp_up[l].reshape(-1, D_EXPERT),
            w_exp_down[l].reshape(-1, d), w_out[l]]
    a, wg_b, wu_b, wd_b, wout_b = _attention(
        proj.reshape(b, s, -1), _bias_sequences(rel_bias, s), lam.reshape(1),
        subln_g[l].reshape(1, -1), lam_init, side)

    wr = jnp.concatenate([w_router_group[l].T,
                          jnp.zeros((NR_EXPERT0 - N_GROUPS, d), F32),
                          jnp.transpose(w_router_expert[l], (0, 2, 1)).reshape(N_EXPERTS, d)],
                         axis=0)
    br = jnp.concatenate([b_router_group[l], jnp.zeros((NR_EXPERT0 - N_GROUPS,), F32),
                          b_router_expert[l].reshape(-1)]).reshape(NR_PAD, 1)
    wr_hi = wr.astype(BF16)
    wr = jnp.concatenate([wr_hi, (wr - wr_hi.astype(F32)).astype(BF16)], axis=0)
    h1, xs, ri, rw, cnt = _mix(
        a.reshape(t, -1), proj, x2, wout_b, sg_w[l].astype(BF16),
        sg_b[l].reshape(SG_GROUPS, SG_CHUNK, 1), sg_ln_g[l], sg_ln_b[l],
        ln1_g[l].reshape(1, d), ln1_b[l].reshape(1, d), wr, br)

    counts = cnt[:, 0]
    tiles_e = (counts + TMX - 1) // TMX
    tile_end = jnp.cumsum(tiles_e)
    tile_start = tile_end - tiles_e
    n_used = tile_end[-1:]
    max_tiles = (2 * t) // TMX + N_EXPERTS
    row_start = (tile_start * TMX).astype(I32)
    eids = jnp.arange(N_EXPERTS, dtype=I32)[:, None, None]
    pos = jnp.sum(jnp.where(ri[0:2][None] == eids, row_start[:, None, None], 0), axis=0) + ri[2:4]
    pos3 = pos.reshape(2, t // TM_OUT, TM_OUT).transpose(1, 0, 2)
    tile_ids = jnp.arange(max_tiles, dtype=I32)
    tile_e = jnp.minimum(jnp.sum(tile_end[None, :] <= tile_ids[:, None], axis=1), N_EXPERTS - 1)
    local = tile_ids - tile_start[tile_e]
    in_use = tile_ids < n_used[0]
    tile_row = jnp.where(in_use, tile_e * XS_CAP + local * TMX, 0).astype(I32)
    tile_valid = jnp.where(in_use, jnp.clip(counts[tile_e] - local * TMX, 0, TMX), 0).astype(I32)

    ys = _experts(tile_start.astype(I32), tiles_e.astype(I32), n_used.astype(I32),
                  tile_row, tile_valid, xs,
                  wg_b.reshape(N_EXPERTS, d, D_EXPERT), wu_b.reshape(N_EXPERTS, d, D_EXPERT),
                  wd_b.reshape(N_EXPERTS, D_EXPERT, d))
    out = _combine(pos3, ys, h1, rw.T, ln2_g[l].reshape(1, d), ln2_b[l].reshape(1, d))
    return out.reshape(b, s, d)
```

```python
import functools
import math

import jax
import jax.numpy as jnp
from jax import lax
from jax.experimental import pallas as pl
from jax.experimental.pallas import tpu as pltpu

F32 = jnp.float32
BF16 = jnp.bfloat16
I32 = jnp.int32

ATT_HEADS = 8
ATT_QK_DIM = 64
ATT_V_DIM = 128
ATT_WIDTH = ATT_HEADS * ATT_V_DIM
SG_GROUPS = 8
SG_DIM = 128
SG_WIDTH = SG_GROUPS * SG_DIM
SG_CHUNK = 128
REL_BUCKETS = 32
REL_MAX_DIST = 128
N_GROUPS = 4
EXPERTS_PER_GROUP = 8
N_EXPERTS = N_GROUPS * EXPERTS_PER_GROUP
D_EXPERT = 256
DEPTH = 1
DN_ALPHA = (2.0 * DEPTH) ** 0.25
LN_EPS = 1e-5
LOG2E = math.log2(math.e)

LANES = 128
SUBLANES = 8
VMEM_LIMIT_BYTES = 56 * 1024 * 1024

TM_IN = 1024
TN_IN = 1024
W_AHEAD = 2
W_SLOTS = W_AHEAD + 1
QB = 256
KC = 256
N_BIAS_TILES = 5
TM_MIX = 512
TMX = 256
TM_OUT = 256
X_AHEAD = 3
N_XSLOTS = 2 * X_AHEAD
ROUTE_ROWS = 8
XS_CAP = 8192 + TMX
NR_EXPERT0 = 8
NR_PAD = NR_EXPERT0 + 32


def _cparams(sem, flags=None):
    return pltpu.CompilerParams(dimension_semantics=sem, vmem_limit_bytes=VMEM_LIMIT_BYTES,
                                flags=flags)


def _inproj_kernel(x_ref, w_hbm, o_ref, xb_ref, wbuf, wsem, *, n_q, n_lin, q_scale):
    j = pl.program_id(1)
    nj = pl.num_programs(1)
    step = pl.program_id(0) * nj + j
    n_steps = pl.num_programs(0) * nj

    def w_copy(s):
        cols = pl.ds(pl.multiple_of(lax.rem(s, nj) * TN_IN, TN_IN), TN_IN)
        slot = lax.rem(s, W_SLOTS)
        return pltpu.make_async_copy(w_hbm.at[:, cols], wbuf.at[slot], wsem.at[slot])

    @pl.when(step == 0)
    def _():
        for s in range(W_AHEAD):
            w_copy(s).start(priority=1)

    @pl.when(step + W_AHEAD < n_steps)
    def _():
        w_copy(step + W_AHEAD).start(priority=1)

    @pl.when(j == 0)
    def _():
        xb_ref[...] = x_ref[...].astype(BF16)

    w_copy(step).wait()
    w = wbuf[lax.rem(step, W_SLOTS)].astype(BF16)
    acc = jnp.dot(xb_ref[...], w, preferred_element_type=F32)

    @pl.when(j < n_q)
    def _():
        o_ref[...] = (acc * q_scale).astype(BF16)

    @pl.when(jnp.logical_and(j >= n_q, j < n_lin))
    def _():
        o_ref[...] = acc.astype(BF16)

    @pl.when(j >= n_lin)
    def _():
        o_ref[...] = jax.nn.gelu(acc).astype(BF16)


def _inproj(x2, w_in):
    t, d = x2.shape
    n = w_in.shape[1]
    assert (t // TM_IN) * (n // TN_IN) >= W_AHEAD
    n_q = (ATT_HEADS * 2 * ATT_QK_DIM) // TN_IN
    n_lin = (2 * ATT_HEADS * 2 * ATT_QK_DIM + ATT_WIDTH) // TN_IN
    kern = functools.partial(_inproj_kernel, n_q=n_q, n_lin=n_lin,
                             q_scale=(ATT_QK_DIM ** -0.5) * LOG2E)
    return pl.pallas_call(
        kern,
        grid=(t // TM_IN, n // TN_IN),
        in_specs=[pl.BlockSpec((TM_IN, d), lambda i, j: (i, 0)),
                  pl.BlockSpec(memory_space=pl.ANY)],
        out_specs=pl.BlockSpec((TM_IN, TN_IN), lambda i, j: (i, j)),
        out_shape=jax.ShapeDtypeStruct((t, n), BF16),
        scratch_shapes=[pltpu.VMEM((TM_IN, d), BF16),
                        pltpu.VMEM((W_SLOTS, d, TN_IN), w_in.dtype),
                        pltpu.SemaphoreType.DMA((W_SLOTS,))],
        compiler_params=_cparams(("arbitrary", "arbitrary")),
        name="inproj",
    )(x2, w_in)


def _attn_kernel(lam_ref, q_ref, k_ref, v_ref, bseq_ref, g_ref, *rest, n_kc, out_scale):
    n_side = (len(rest) - 3) // 2
    side_in, o_ref, side_out = rest[:n_side], rest[n_side], rest[n_side + 1:2 * n_side + 1]
    vaug_ref, bias_ref = rest[2 * n_side + 1:]
    qi = pl.program_id(2)

    @pl.when(qi == 0)
    def _():
        vaug_ref[:, :ATT_V_DIM] = v_ref[0]
        vaug_ref[:, ATT_V_DIM:] = jnp.ones((vaug_ref.shape[0], ATT_V_DIM), BF16)
        for t in range(N_BIAS_TILES):
            seq = jnp.broadcast_to(bseq_ref[0, t:t + 1, :], (QB, 2 * KC))
            bias_ref[t] = pltpu.roll(seq, 0, 1, stride=1, stride_axis=0)[:, :KC]

    q = q_ref[0]
    lane = lax.broadcasted_iota(I32, q.shape, 1)
    zero = jnp.zeros_like(q)
    qq = jnp.concatenate([jnp.where(lane < ATT_QK_DIM, q, zero),
                          jnp.where(lane >= ATT_QK_DIM, q, zero)], axis=0)

    for src, dst in zip(side_in, side_out):
        dst[...] = src[...].astype(dst.dtype)

    m = jnp.full((2 * QB, 1), -jnp.inf, F32)
    acc = jnp.zeros((2 * QB, 2 * ATT_V_DIM), F32)
    for kc in range(n_kc):
        kblk = k_ref[0, kc * KC:(kc + 1) * KC, :]
        s = lax.dot_general(qq, kblk, (((1,), (1,)), ((), ())),
                            preferred_element_type=F32)
        b = bias_ref[jnp.clip(kc - qi, -2, 2) + 2]
        s = s + jnp.concatenate([b, b], axis=0)
        m_new = jnp.maximum(m, jnp.max(s, axis=1, keepdims=True))
        p = jnp.exp2(s - m_new).astype(BF16)
        acc = acc * jnp.exp2(m - m_new) + jnp.dot(
            p, vaug_ref[kc * KC:(kc + 1) * KC, :], preferred_element_type=F32)
        m = m_new
    lam = lam_ref[0]
    o0 = acc[:QB, :ATT_V_DIM] / acc[:QB, ATT_V_DIM:ATT_V_DIM + 1]
    o1 = acc[QB:, :ATT_V_DIM] / acc[QB:, ATT_V_DIM:ATT_V_DIM + 1]
    of = o0 - lam * o1
    of = of * lax.rsqrt(jnp.mean(of * of, axis=-1, keepdims=True) + LN_EPS)
    o_ref[0] = (of * (g_ref[...] * out_scale)).astype(o_ref.dtype)


def _attention(proj3, bias_seq, lam, subln_g, lam_init, side):
    b, s, _ = proj3.shape
    n_kc = s // KC
    n_q = s // QB
    n_steps = b * ATT_HEADS * n_q
    h_off_k = ATT_HEADS
    h_off_v = 2 * ATT_HEADS
    kern = functools.partial(_attn_kernel, n_kc=n_kc, out_scale=1.0 - lam_init)

    side_specs = []
    for arr in side:
        rows, cols = arr.shape
        blk = max(rows // n_steps, 2 * SUBLANES)
        hold = blk * n_steps // rows
        assert rows % blk == 0 and hold * rows == blk * n_steps
        side_specs.append(pl.BlockSpec(
            (blk, cols),
            lambda bi, h, qi, hold=hold: (((bi * ATT_HEADS + h) * n_q + qi) // hold, 0)))
    return pl.pallas_call(
        kern,
        grid=(b, ATT_HEADS, n_q),
        in_specs=[
            pl.BlockSpec(memory_space=pltpu.SMEM),
            pl.BlockSpec((1, QB, LANES), lambda bi, h, qi: (bi, qi, h)),
            pl.BlockSpec((1, s, LANES), lambda bi, h, qi: (bi, 0, h_off_k + h)),
            pl.BlockSpec((1, s, LANES), lambda bi, h, qi: (bi, 0, h_off_v + h)),
            pl.BlockSpec((1, N_BIAS_TILES, 2 * KC), lambda bi, h, qi: (h, 0, 0)),
            pl.BlockSpec((1, ATT_V_DIM), lambda bi, h, qi: (0, 0)),
        ] + side_specs,
        out_specs=[pl.BlockSpec((1, QB, ATT_V_DIM), lambda bi, h, qi: (bi, qi, h))] + side_specs,
        out_shape=[jax.ShapeDtypeStruct((b, s, ATT_WIDTH), BF16)]
        + [jax.ShapeDtypeStruct(arr.shape, BF16) for arr in side],
        scratch_shapes=[pltpu.VMEM((s, 2 * ATT_V_DIM), BF16),
                        pltpu.VMEM((N_BIAS_TILES, QB, KC), F32)],
        compiler_params=_cparams(("arbitrary", "arbitrary", "arbitrary")),
        name="attn",
    )(lam, proj3, proj3, proj3, bias_seq, subln_g, *side)


def _rel_bucket(rel):
    half = REL_BUCKETS // 2
    max_exact = half // 2
    ret = jnp.where(rel > 0, half, 0)
    n = jnp.abs(rel)
    nf = jnp.maximum(n, 1).astype(F32)
    large = max_exact + (jnp.log(nf / max_exact) / math.log(REL_MAX_DIST / max_exact)
                         * (half - max_exact)).astype(I32)
    large = jnp.minimum(large, half - 1)
    return ret + jnp.where(n < max_exact, n, large)


def _bias_sequences(rel_bias, s):
    assert REL_MAX_DIST <= KC and QB == KC
    period = 2 * KC
    j = jnp.arange(period, dtype=I32)
    off = jnp.where(j < KC, j, j - period)
    d = jnp.arange(-2, 3, dtype=I32)[:, None]
    rel = jnp.clip(d * KC + off[None, :], -(s - 1), s - 1)
    return jnp.transpose(rel_bias[_rel_bucket(rel)], (2, 0, 1)).astype(F32) * LOG2E


def _layernorm(v, g, b):
    mu = jnp.mean(v, axis=-1, keepdims=True)
    var = jnp.mean(jnp.square(v - mu), axis=-1, keepdims=True)
    return (v - mu) * lax.rsqrt(var + LN_EPS) * g + b


def _first_argmax(vals, nrows):
    mx = jnp.max(vals, axis=0, keepdims=True)
    row = lax.broadcasted_iota(I32, vals.shape, 0)
    idx = jnp.min(jnp.where(vals == mx, row, nrows), axis=0, keepdims=True)
    return mx, idx


def _mix_kernel(a_ref, u_ref, vg_ref, x_ref, wout_ref, sgw_ref, sgb_ref, sgg_ref, sgbeta_ref,
                g1_ref, b1_ref, wr_ref, br_ref,
                h1_ref, xs_hbm, ri_ref, rw_ref, cnt_ref,
                run_ref, hp_s, posv_ref, psm, ssem, rsem):
    i = pl.program_id(0)
    n = pl.num_programs(0)
    slot = i % 2
    prev = 1 - slot

    def pos_copy(sl):
        return pltpu.make_async_copy(posv_ref, psm.at[sl], ssem.at[sl])

    def row_copy(sl, which, j):
        return pltpu.make_async_copy(hp_s.at[sl, pl.ds(j, 1)],
                                     xs_hbm.at[pl.ds(psm[sl, which, j], 1)], rsem.at[sl])

    def drain(sl):
        whole = pltpu.make_async_copy(hp_s.at[sl], xs_hbm.at[pl.ds(0, TM_MIX)], rsem.at[sl])
        whole.wait()
        whole.wait()

    @pl.when(i == 0)
    def _():
        run_ref[...] = jnp.zeros_like(run_ref)
        hp_s[1] = jnp.zeros(hp_s.shape[1:], hp_s.dtype)
        spare = N_EXPERTS * XS_CAP + lax.broadcasted_iota(I32, posv_ref.shape, 1)
        spare = spare + TM_MIX * lax.broadcasted_iota(I32, posv_ref.shape, 0)
        posv_ref[...] = jnp.where(spare < xs_hbm.shape[0], spare, 0)
        pos_copy(1).start()

    @pl.when(i > 0)
    def _():
        drain(slot)

    pos_copy(prev).wait()

    for j in range(TM_MIX):
        row_copy(prev, 0, j).start(priority=0)
        row_copy(prev, 1, j).start(priority=1)

    h1_ref[...] = jnp.dot(a_ref[...], wout_ref[:ATT_WIDTH, :], preferred_element_type=F32)

    n_chunks = TM_MIX // SG_CHUNK
    rows = []
    for c in range(n_chunks):
        cols = []
        for g in range(SG_GROUPS):
            rs = slice(c * SG_CHUNK, (c + 1) * SG_CHUNK)
            cs = slice(g * SG_DIM, (g + 1) * SG_DIM)
            vt = vg_ref[rs, cs].astype(F32)
            vn = _layernorm(vt, sgg_ref[g:g + 1, :], sgbeta_ref[g:g + 1, :])
            mixed = jnp.dot(sgw_ref[g], vn.astype(BF16), preferred_element_type=F32) + sgb_ref[g]
            cols.append((u_ref[rs, cs].astype(F32) * mixed).astype(BF16))
        rows.append(jnp.concatenate(cols, axis=1))
    sgate = jnp.concatenate(rows, axis=0)

    mix = h1_ref[...] + jnp.dot(sgate, wout_ref[ATT_WIDTH:, :], preferred_element_type=F32)
    h1 = _layernorm(DN_ALPHA * x_ref[...] + mix, g1_ref[...], b1_ref[...])
    h1_ref[...] = h1
    hp_s[slot] = h1.astype(BF16).reshape(hp_s.shape[1:])

    nt = (((1,), (1,)), ((), ()))
    h1_hi = h1.astype(BF16)
    h1_lo = (h1 - h1_hi.astype(F32)).astype(BF16)
    both = lax.dot_general(wr_ref[...], h1_hi, nt, preferred_element_type=F32)
    lt = both[:NR_PAD] + both[NR_PAD:] + lax.dot_general(
        wr_ref[:NR_PAD, :], h1_lo, nt, preferred_element_type=F32)
    lt = lt + br_ref[...]
    g_logits = lt[0:N_GROUPS]
    gmax, g_idx = _first_argmax(g_logits, N_GROUPS)
    g_gate = 1.0 / jnp.sum(jnp.exp(g_logits - gmax), axis=0, keepdims=True)
    e_logits = jnp.zeros((EXPERTS_PER_GROUP, TM_MIX), F32)
    for g in range(N_GROUPS):
        lo = NR_EXPERT0 + g * EXPERTS_PER_GROUP
        e_logits = jnp.where(g_idx == g, lt[lo:lo + EXPERTS_PER_GROUP], e_logits)
    v1, i1 = _first_argmax(e_logits, EXPERTS_PER_GROUP)
    erow = lax.broadcasted_iota(I32, e_logits.shape, 0)
    v2, i2 = _first_argmax(jnp.where(erow == i1, -jnp.inf, e_logits), EXPERTS_PER_GROUP)
    t = jnp.exp(v2 - v1)
    w1 = g_gate / (1.0 + t)
    w2 = g_gate * t / (1.0 + t)
    e1 = g_idx * EXPERTS_PER_GROUP + i1
    e2 = g_idx * EXPERTS_PER_GROUP + i2

    xrow = lax.broadcasted_iota(I32, (N_EXPERTS, TM_MIX), 0)
    oh1 = xrow == e1
    oh2 = xrow == e2
    oh = jnp.where(jnp.logical_or(oh1, oh2), 1.0, 0.0)
    tr = lax.broadcasted_iota(I32, (TM_MIX, TM_MIX), 0)
    tc = lax.broadcasted_iota(I32, (TM_MIX, TM_MIX), 1)
    before = jnp.where(tr < tc, 1.0, 0.0).astype(BF16)
    cum = jnp.dot(oh.astype(BF16), before, preferred_element_type=F32)
    base = cum + run_ref[:, 0:1]
    rank1 = jnp.sum(jnp.where(oh1, base, 0.0), axis=0, keepdims=True).astype(I32)
    rank2 = jnp.sum(jnp.where(oh2, base, 0.0), axis=0, keepdims=True).astype(I32)
    run_ref[...] = run_ref[...] + jnp.sum(oh, axis=1, keepdims=True)
    cnt_ref[...] = run_ref[...].astype(I32)

    zi = jnp.zeros((ROUTE_ROWS - 4, TM_MIX), I32)
    ri_ref[...] = jnp.concatenate([e1, e2, rank1, rank2, zi], axis=0)
    zw = jnp.zeros((ROUTE_ROWS - 2, TM_MIX), F32)
    rw_ref[...] = jnp.concatenate([w1, w2, zw], axis=0)

    posv_ref[...] = jnp.concatenate([e1 * XS_CAP + rank1, e2 * XS_CAP + rank2, zi, zi[:2]], axis=0)
    pos_copy(slot).start()

    @pl.when(i == n - 1)
    def _():
        pos_copy(slot).wait()

        def last(j, c):
            row_copy(slot, 0, j).start(priority=0)
            row_copy(slot, 1, j).start(priority=1)
            return c
        lax.fori_loop(0, TM_MIX, last, 0, unroll=8)
        drain(prev)
        drain(slot)


def _mix(a2, proj2, x2, wout_b, sgw_b, sgb, sgg, sgbeta, g1, b1, wr, br):
    t, d = x2.shape
    assert XS_CAP >= t + TMX
    u_blk = (proj2.shape[1] - 2 * SG_WIDTH) // SG_WIDTH
    const = lambda *shape: pl.BlockSpec(shape, lambda i: (0,) * len(shape))
    return pl.pallas_call(
        _mix_kernel,
        grid=(t // TM_MIX,),
        in_specs=[
            pl.BlockSpec((TM_MIX, ATT_WIDTH), lambda i: (i, 0)),
            pl.BlockSpec((TM_MIX, SG_WIDTH), lambda i: (i, u_blk)),
            pl.BlockSpec((TM_MIX, SG_WIDTH), lambda i: (i, u_blk + 1)),
            pl.BlockSpec((TM_MIX, d), lambda i: (i, 0)),
            pl.BlockSpec((d, d), lambda i: (0, 0), pipeline_mode=pl.Buffered(1)),
            const(SG_GROUPS, SG_CHUNK, SG_CHUNK),
            const(SG_GROUPS, SG_CHUNK, 1),
            const(SG_GROUPS, SG_DIM),
            const(SG_GROUPS, SG_DIM),
            const(1, d),
            const(1, d),
            const(2 * NR_PAD, d),
            const(NR_PAD, 1),
        ],
        out_specs=[
            pl.BlockSpec((TM_MIX, d), lambda i: (i, 0)),
            pl.BlockSpec(memory_space=pl.ANY),
            pl.BlockSpec((ROUTE_ROWS, TM_MIX), lambda i: (0, i)),
            pl.BlockSpec((ROUTE_ROWS, TM_MIX), lambda i: (0, i)),
            const(N_EXPERTS, LANES),
        ],
        out_shape=[
            jax.ShapeDtypeStruct((t, d), F32),
            jax.ShapeDtypeStruct((N_EXPERTS * XS_CAP + 2 * TM_MIX, d // LANES, LANES), BF16),
            jax.ShapeDtypeStruct((ROUTE_ROWS, t), I32),
            jax.ShapeDtypeStruct((ROUTE_ROWS, t), F32),
            jax.ShapeDtypeStruct((N_EXPERTS, LANES), I32),
        ],
        scratch_shapes=[pltpu.VMEM((N_EXPERTS, LANES), F32),
                        pltpu.VMEM((2, TM_MIX, d // LANES, LANES), BF16),
                        pltpu.VMEM((ROUTE_ROWS, TM_MIX), I32),
                        pltpu.SMEM((2, ROUTE_ROWS, TM_MIX), I32),
                        pltpu.SemaphoreType.DMA((2,)),
                        pltpu.SemaphoreType.DMA((2,))],
        compiler_params=_cparams(("arbitrary",)),
        name="mix",
    )(a2, proj2, proj2, x2, wout_b, sgw_b, sgb, sgg, sgbeta, g1, b1, wr, br)


def _experts_kernel(ts_ref, nt_ref, nu_ref, trow_ref, nval_ref, x_hbm, wg_ref, wu_ref, wd_ref,
                    y_hbm, xbuf, ybuf, xsem, ysem):
    e = pl.program_id(0)
    d = wg_ref.shape[1]
    n_used = nu_ref[0]

    def x_copy(g):
        rows = pl.ds(pl.multiple_of(trow_ref[g], TMX), TMX)
        slot = lax.rem(g, N_XSLOTS)
        return pltpu.make_async_copy(x_hbm.at[rows], xbuf.at[slot], xsem.at[slot])

    def y_copy(g):
        rows = pl.ds(pl.multiple_of(g * TMX, TMX), TMX)
        slot = lax.rem(g, N_XSLOTS)
        return pltpu.make_async_copy(ybuf.at[slot], y_hbm.at[rows], ysem.at[slot])

    def start_x(g):
        @pl.when(g < n_used)
        def _():
            x_copy(g).start(priority=1)

    def free_y(g):
        @pl.when(g >= N_XSLOTS)
        def _():
            y_copy(g - N_XSLOTS).wait()

    def load(g):
        xt = xbuf[lax.rem(g, N_XSLOTS)].reshape(TMX, d)
        row = lax.broadcasted_iota(I32, xt.shape, 0)
        return jnp.where(row < nval_ref[g], xt, jnp.zeros_like(xt))

    def mlp(xb):
        gate = jnp.dot(xb, wg_ref[0], preferred_element_type=F32)
        up = jnp.dot(xb, wu_ref[0], preferred_element_type=F32)
        hmid = (gate * jax.nn.sigmoid(gate) * up).astype(BF16)
        return jnp.dot(hmid, wd_ref[0], preferred_element_type=F32)

    @pl.when(e == 0)
    def _():
        for g in range(X_AHEAD):
            start_x(g)

    def group(k, first):
        def body(t, carry):
            g = first(t)
            for j in range(k):
                x_copy(g + j).wait()
            for j in range(k):
                start_x(g + X_AHEAD + j)
                free_y(g + j)
            y = mlp(jnp.concatenate([load(g + j) for j in range(k)], axis=0))
            for j in range(k):
                ybuf[lax.rem(g + j, N_XSLOTS)] = y[j * TMX:(j + 1) * TMX].astype(BF16).reshape(
                    ybuf.shape[1:])
                y_copy(g + j).start(priority=1)
            return carry
        return body

    nt = nt_ref[e]
    n3 = jnp.where(jnp.logical_and(nt >= 3, nt % 2 == 1), 1, 0)
    n2 = (nt - 3 * n3) // 2
    n1 = nt - 3 * n3 - 2 * n2
    base = ts_ref[e]
    lax.fori_loop(0, n3, group(3, lambda t: base), 0)
    lax.fori_loop(0, n2, group(2, lambda t: base + 3 * n3 + 2 * t), 0)
    lax.fori_loop(0, n1, group(1, lambda t: base + 3 * n3 + 2 * n2), 0)

    @pl.when(e == pl.num_programs(0) - 1)
    def _():
        for back in range(N_XSLOTS, 0, -1):
            @pl.when(n_used >= back)
            def _():
                y_copy(n_used - back).wait()


def _experts(tile_start, n_tiles_e, n_used, tile_row, tile_valid, xs, w_gate, w_up, w_down):
    d = w_gate.shape[1]
    n_rows = tile_row.shape[0] * TMX

    def w_map(e, ts, nt, nu, trow, nval):
        return (e, 0, 0)

    grid_spec = pltpu.PrefetchScalarGridSpec(
        num_scalar_prefetch=5,
        grid=(N_EXPERTS,),
        in_specs=[
            pl.BlockSpec(memory_space=pl.ANY),
            pl.BlockSpec((1, d, D_EXPERT), w_map),
            pl.BlockSpec((1, d, D_EXPERT), w_map),
            pl.BlockSpec((1, D_EXPERT, d), w_map),
        ],
        out_specs=pl.BlockSpec(memory_space=pl.ANY),
        scratch_shapes=[pltpu.VMEM((N_XSLOTS, TMX) + xs.shape[1:], BF16),
                        pltpu.VMEM((N_XSLOTS, TMX) + xs.shape[1:], BF16),
                        pltpu.SemaphoreType.DMA((N_XSLOTS,)),
                        pltpu.SemaphoreType.DMA((N_XSLOTS,))],
    )
    return pl.pallas_call(
        _experts_kernel,
        grid_spec=grid_spec,
        out_shape=jax.ShapeDtypeStruct((n_rows,) + xs.shape[1:], BF16),
        compiler_params=_cparams(("arbitrary",)),
        name="experts",
    )(tile_start, n_tiles_e, n_used, tile_row, tile_valid, xs, w_gate, w_up, w_down)


def _combine_kernel(pos_ref, posn_ref, y_hbm, h1_ref, wt_ref, g2_ref, b2_ref, o_ref, ybuf, sem):
    i = pl.program_id(0)
    n = pl.num_programs(0)
    slot = i % 2

    def row_copy(src, sl, which, j):
        return pltpu.make_async_copy(y_hbm.at[pl.ds(src, 1)],
                                     ybuf.at[sl, which, pl.ds(j, 1)], sem.at[sl])

    def issue(p_ref, sl):
        def body(j, c):
            row_copy(p_ref[0, j], sl, 0, j).start(priority=0)
            row_copy(p_ref[1, j], sl, 1, j).start(priority=1)
            return c
        lax.fori_loop(0, TM_OUT, body, 0, unroll=8)

    def drain(sl):
        for which in range(2):
            pltpu.make_async_copy(y_hbm.at[pl.ds(0, TM_OUT)], ybuf.at[sl, which],
                                  sem.at[sl]).wait()

    @pl.when(i == 0)
    def _():
        issue(pos_ref, 0)

    drain(slot)

    for j in range(TM_OUT):
        row_copy(posn_ref[0, j], 1 - slot, 0, j).start(priority=0)
        row_copy(posn_ref[1, j], 1 - slot, 1, j).start(priority=1)

    y1 = ybuf[slot, 0].reshape(h1_ref.shape).astype(F32)
    y2 = ybuf[slot, 1].reshape(h1_ref.shape).astype(F32)
    ffn = wt_ref[:, 0:1] * y1 + wt_ref[:, 1:2] * y2
    o_ref[...] = _layernorm(DN_ALPHA * h1_ref[...] + ffn, g2_ref[...], b2_ref[...])

    @pl.when(i == n - 1)
    def _():
        drain(1 - slot)


def _combine(pos3, ys, h1, wtok, g2, b2):
    t, d = h1.shape
    n = t // TM_OUT
    return pl.pallas_call(
        _combine_kernel,
        grid=(n,),
        in_specs=[
            pl.BlockSpec((None, 2, TM_OUT), lambda i: (i, 0, 0), memory_space=pltpu.SMEM),
            pl.BlockSpec((None, 2, TM_OUT), lambda i: (jnp.minimum(i + 1, n - 1), 0, 0),
                         memory_space=pltpu.SMEM),
            pl.BlockSpec(memory_space=pl.ANY),
            pl.BlockSpec((TM_OUT, d), lambda i: (i, 0)),
            pl.BlockSpec((TM_OUT, ROUTE_ROWS), lambda i: (i, 0)),
            pl.BlockSpec((1, d), lambda i: (0, 0)),
            pl.BlockSpec((1, d), lambda i: (0, 0)),
        ],
        out_specs=pl.BlockSpec((TM_OUT, d), lambda i: (i, 0)),
        out_shape=jax.ShapeDtypeStruct((t, d), F32),
        scratch_shapes=[pltpu.VMEM((2, 2, TM_OUT) + ys.shape[1:], BF16),
                        pltpu.SemaphoreType.DMA((2,))],
        compiler_params=_cparams(("arbitrary",)),
        name="combine",
    )(pos3, pos3, ys, h1, wtok, g2, b2)


def _lambda_init(layer):
    return 0.8 - 0.6 * math.exp(-0.3 * layer)


def kernel(x, w_in, w_out, ln1_g, ln1_b, ln2_g, ln2_b, rel_bias, lam_q1, lam_k1, lam_q2, lam_k2,
           subln_g, sg_ln_g, sg_ln_b, sg_w, sg_b, w_router_group, b_router_group,
           w_router_expert, b_router_expert, w_exp_gate, w_exp_up, w_exp_down):
    b, s, d = x.shape
    t = b * s
    assert w_in.shape[0] == DEPTH == 1
    l = 0
    lam_init = _lambda_init(l)
    x2 = x.reshape(t, d)

    proj = _inproj(x2, w_in[l])

    lam = (jnp.exp(jnp.sum(lam_q1[l].astype(F32) * lam_k1[l].astype(F32)))
           - jnp.exp(jnp.sum(lam_q2[l].astype(F32) * lam_k2[l].astype(F32))) + lam_init)
    side = [w_exp_gate[l].reshape(-1, D_EXPERT), w_exp_up[l].reshape(-1, D_EXPERT),
            w_exp_down[l].reshape(-1, d), w_out[l]]
    a, wg_b, wu_b, wd_b, wout_b = _attention(
        proj.reshape(b, s, -1), _bias_sequences(rel_bias, s), lam.reshape(1),
        subln_g[l].reshape(1, -1), lam_init, side)

    wr = jnp.concatenate([w_router_group[l].T,
                          jnp.zeros((NR_EXPERT0 - N_GROUPS, d), F32),
                          jnp.transpose(w_router_expert[l], (0, 2, 1)).reshape(N_EXPERTS, d)],
                         axis=0)
    br = jnp.concatenate([b_router_group[l], jnp.zeros((NR_EXPERT0 - N_GROUPS,), F32),
                          b_router_expert[l].reshape(-1)]).reshape(NR_PAD, 1)
    wr_hi = wr.astype(BF16)
    wr = jnp.concatenate([wr_hi, (wr - wr_hi.astype(F32)).astype(BF16)], axis=0)
    h1, xs, ri, rw, cnt = _mix(
        a.reshape(t, -1), proj, x2, wout_b, sg_w[l].astype(BF16),
        sg_b[l].reshape(SG_GROUPS, SG_CHUNK, 1), sg_ln_g[l], sg_ln_b[l],
        ln1_g[l].reshape(1, d), ln1_b[l].reshape(1, d), wr, br)

    counts = cnt[:, 0]
    tiles_e = (counts + TMX - 1) // TMX
    tile_end = jnp.cumsum(tiles_e)
    tile_start = tile_end - tiles_e
    n_used = tile_end[-1:]
    max_tiles = (2 * t) // TMX + N_EXPERTS
    row_start = (tile_start * TMX).astype(I32)
    eids = jnp.arange(N_EXPERTS, dtype=I32)[:, None, None]
    pos = jnp.sum(jnp.where(ri[0:2][None] == eids, row_start[:, None, None], 0), axis=0) + ri[2:4]
    pos3 = pos.reshape(2, t // TM_OUT, TM_OUT).transpose(1, 0, 2)
    tile_ids = jnp.arange(max_tiles, dtype=I32)
    tile_e = jnp.minimum(jnp.sum(tile_end[None, :] <= tile_ids[:, None], axis=1), N_EXPERTS - 1)
    local = tile_ids - tile_start[tile_e]
    in_use = tile_ids < n_used[0]
    tile_row = jnp.where(in_use, tile_e * XS_CAP + local * TMX, 0).astype(I32)
    tile_valid = jnp.where(in_use, jnp.clip(counts[tile_e] - local * TMX, 0, TMX), 0).astype(I32)

    ys = _experts(tile_start.astype(I32), tiles_e.astype(I32), n_used.astype(I32),
                  tile_row, tile_valid, xs,
                  wg_b.reshape(N_EXPERTS, d, D_EXPERT), wu_b.reshape(N_EXPERTS, d, D_EXPERT),
                  wd_b.reshape(N_EXPERTS, D_EXPERT, d))
    out = _combine(pos3, ys, h1, rw.T, ln2_g[l].reshape(1, d), ln2_b[l].reshape(1, d))
    return out.reshape(b, s, d)
```
